```python
import jax, jax.numpy as jnp
from jax import lax
import numpy as np

D_MODEL = 1024
BATCH = 32
SEQ = 256
DEPTH = 2
DEC_BATCH = 8
DEC_SEQ = 1024
PAST_LEN = 512

GRID_W = 64
HEAD_DIM = 64
CONV_WIDTH = 256
CONV_K = 3
GQA_HEADS = 8
GQA_KV_HEADS = 2
GQA_GROUP = GQA_HEADS // GQA_KV_HEADS
WINDOW = 128
BAND_BLOCK = 128
NA_HEADS = 4
NA_WIN_H = 8
NA_WIN_W = 16
MLA_HEADS = 4
MLA_Q_RANK = 256
MLA_KV_RANK = 128
MLA_NOPE = 64
MLA_ROPE = 32
MLA_V = 64
Q_BLOCK = 128
D_FF = 4 * D_MODEL
N_BRANCH = 4
ROPE_BASE = 10000.0
EPS = 1e-6
NEG_INF = -1e30
ATTN_SCALE = HEAD_DIM ** -0.5
MLA_SCALE = (MLA_NOPE + MLA_ROPE) ** -0.5
GQA_WIDTH = GQA_HEADS * HEAD_DIM
GQA_KV_WIDTH = GQA_KV_HEADS * HEAD_DIM
NA_WIDTH = NA_HEADS * HEAD_DIM
MLA_WIDTH = MLA_HEADS * MLA_V
IN_SIZES = (CONV_WIDTH, CONV_WIDTH, CONV_WIDTH,
            GQA_WIDTH, GQA_KV_WIDTH, GQA_KV_WIDTH,
            NA_WIDTH, NA_WIDTH, NA_WIDTH,
            MLA_Q_RANK, MLA_KV_RANK, MLA_ROPE,
            N_BRANCH * D_MODEL)
D_IN = sum(IN_SIZES)

kernel_name = "hybrid_diffusion_prefix_step"


def rms_norm(x, g):
    xf = x.astype(jnp.float32)
    y = xf * lax.rsqrt(jnp.mean(xf * xf, axis=-1, keepdims=True) + EPS)
    return (y * g.astype(jnp.float32)).astype(x.dtype)


def adaln(c_vec, w_mod, b_mod):
    m = jax.nn.silu(c_vec) @ w_mod + b_mod
    return jnp.split(m[:, None, :], 6, axis=-1)


def modulate(h, shift, scale):
    return h * (1 + scale) + shift


def grid_positions(n):
    t = jnp.arange(n)
    return (t // GRID_W).astype(jnp.float32), (t % GRID_W).astype(jnp.float32)


def rope_axis(x, pos):
    half = x.shape[-1] // 2
    inv = ROPE_BASE ** (-jnp.arange(half, dtype=jnp.float32) / half)
    ang = pos[:, None] * inv[None, :]
    cos = jnp.cos(ang)[None, :, None, :].astype(x.dtype)
    sin = jnp.sin(ang)[None, :, None, :].astype(x.dtype)
    x1, x2 = x[..., :half], x[..., half:]
    return jnp.concatenate([x1 * cos - x2 * sin, x1 * sin + x2 * cos], axis=-1)


def rope_2d(x, rows, cols):
    d = x.shape[-1] // 2
    return jnp.concatenate([rope_axis(x[..., :d], rows), rope_axis(x[..., d:], cols)], axis=-1)


def in_project(h, w_in):
    points = [int(v) for v in np.cumsum(IN_SIZES)[:-1]]
    return jnp.split(h @ w_in, points, axis=-1)


def short_conv(b_gate, c_gate, v, w_conv):
    u = c_gate * v
    s = u.shape[1]
    pad = CONV_K // 2
    up = jnp.pad(u, ((0, 0), (pad, pad), (0, 0)))
    y = sum(up[:, i:i + s] * w_conv[i] for i in range(CONV_K))
    return b_gate * y


def softmax_sink(s, sink):
    if sink is None:
        return jax.nn.softmax(s, axis=-1)
    col = jnp.broadcast_to(sink.astype(jnp.float32)[:, :, None, None], s.shape[:-1] + (1,))
    return jax.nn.softmax(jnp.concatenate([s, col], axis=-1), axis=-1)[..., :-1]


def dense_attention(q, k, v, scale, sink=None):
    n, sq, kh, g, dq = q.shape
    nb = sq // Q_BLOCK
    qb = jnp.moveaxis(q.reshape(n, nb, Q_BLOCK, kh, g, dq), 1, 0)

    def one_block(qi):
        s = jnp.einsum('nqkgd,nskd->nkgqs', qi, k).astype(jnp.float32) * scale
        p = softmax_sink(s, sink)
        return jnp.einsum('nkgqs,nskd->nqkgd', p.astype(v.dtype), v)

    o = lax.map(one_block, qb)
    return jnp.moveaxis(o, 0, 1).reshape(n, sq, kh, g, v.shape[-1])


def banded_window_attention(q, k, v, k_ctx, v_ctx, sink, scale):
    n, s, kh, g, d = q.shape
    nb = s // BAND_BLOCK
    span = BAND_BLOCK + 2 * WINDOW
    kp = jnp.pad(k, ((0, 0), (WINDOW, WINDOW), (0, 0), (0, 0)))
    vp = jnp.pad(v, ((0, 0), (WINDOW, WINDOW), (0, 0), (0, 0)))
    idx = jnp.arange(nb)[:, None] * BAND_BLOCK + jnp.arange(span)[None, :]
    kb = kp[:, idx]
    vb = vp[:, idx]
    qb = q.reshape(n, nb, BAND_BLOCK, kh, g, d)
    s_loc = jnp.einsum('nbqkgd,nbskd->nbkgqs', qb, kb).astype(jnp.float32) * scale
    qpos = jnp.arange(nb)[:, None] * BAND_BLOCK + jnp.arange(BAND_BLOCK)[None, :]
    kpos = idx - WINDOW
    valid = (jnp.abs(qpos[:, :, None] - kpos[:, None, :]) <= WINDOW) \
        & (kpos[:, None, :] >= 0) & (kpos[:, None, :] < s)
    s_loc = jnp.where(valid[None, :, None, None, :, :], s_loc, NEG_INF)
    s_ctx = jnp.einsum('nbqkgd,nskd->nbkgqs', qb, k_ctx).astype(jnp.float32) * scale
    p = softmax_sink(jnp.concatenate([s_loc, s_ctx], axis=-1), sink)
    p_loc = p[..., :span].astype(v.dtype)
    p_ctx = p[..., span:].astype(v.dtype)
    o = jnp.einsum('nbkgqs,nbskd->nbqkgd', p_loc, vb) + jnp.einsum('nbkgqs,nskd->nbqkgd', p_ctx, v_ctx)
    return o.reshape(n, s, kh, g, d)


def neighbourhood_attention(q, k, v, k_ctx, v_ctx, rpb, scale):
    n, s, h, d = q.shape
    rows = s // GRID_W
    wr = min(NA_WIN_H, rows)
    r = jnp.arange(rows)
    r0 = jnp.clip(r - wr // 2, 0, rows - wr)
    krow = r0[:, None] + jnp.arange(wr)[None, :]
    col = jnp.arange(GRID_W)
    c0 = jnp.clip(col - NA_WIN_W // 2, 0, GRID_W - NA_WIN_W)
    col_ok = (col[None, :] >= c0[:, None]) & (col[None, :] < c0[:, None] + NA_WIN_W)
    kg = k.reshape(n, rows, GRID_W, h, d)[:, krow]
    vg = v.reshape(n, rows, GRID_W, h, d)[:, krow]
    qg = q.reshape(n, rows, GRID_W, h, d)
    s_loc = jnp.einsum('nrqhd,nrawhd->nhrqaw', qg, kg).astype(jnp.float32) * scale
    drow = krow - r[:, None] + (NA_WIN_H - 1)
    dcol = jnp.clip(col[None, :] - col[:, None] + (NA_WIN_W - 1), 0, 2 * NA_WIN_W - 2)
    bias = rpb[:, drow[:, None, :, None], dcol[None, :, None, :]].astype(jnp.float32)
    s_loc = jnp.where(col_ok[:, None, :], s_loc + bias[None], NEG_INF)
    s_loc = s_loc.reshape(n, h, rows, GRID_W, wr * GRID_W)
    s_ctx = jnp.einsum('nrqhd,nshd->nhrqs', qg, k_ctx).astype(jnp.float32) * scale
    p = jax.nn.softmax(jnp.concatenate([s_loc, s_ctx], axis=-1), axis=-1)
    p_loc = p[..., :wr * GRID_W].reshape(n, h, rows, GRID_W, wr, GRID_W).astype(v.dtype)
    p_ctx = p[..., wr * GRID_W:].astype(v.dtype)
    o = jnp.einsum('nhrqaw,nrawhd->nrqhd', p_loc, vg) + jnp.einsum('nhrqs,nshd->nrqhd', p_ctx, v_ctx)
    return o.reshape(n, s, h, d)


def mla_queries(mq, g_q, w_uq):
    n, s, _ = mq.shape
    q = (rms_norm(mq, g_q) @ w_uq).reshape(n, s, MLA_HEADS, MLA_NOPE + MLA_ROPE)
    return q[..., :MLA_NOPE], q[..., MLA_NOPE:]


def mla_keys_values(ckv, k_rope, w_ukv):
    n, s, _ = ckv.shape
    kv = (ckv @ w_ukv).reshape(n, s, MLA_HEADS, MLA_NOPE + MLA_V)
    k = jnp.concatenate([kv[..., :MLA_NOPE],
                         jnp.broadcast_to(k_rope[:, :, None, :], (n, s, MLA_HEADS, MLA_ROPE))], axis=-1)
    return k, kv[..., MLA_NOPE:]


def merge_branches(gates, y_conv, y_gqa, y_na, y_mla, lw):
    g1, g2, g3, g4 = jnp.split(jax.nn.sigmoid(gates), N_BRANCH, axis=-1)
    merged = (g1 * (y_conv @ lw['w_branch_conv']) + g2 * (y_gqa @ lw['w_branch_gqa'])
              + g3 * (y_na @ lw['w_branch_na']) + g4 * (y_mla @ lw['w_branch_mla']))
    return merged @ lw['w_o']


def squared_relu_mlp(h, w1, w2):
    return jnp.square(jax.nn.relu(h @ w1)) @ w2


def context_layer(x, c_vec, lw):
    sh1, sc1, gt1, sh2, sc2, gt2 = adaln(c_vec, lw['w_mod'], lw['b_mod'])
    n, l, _ = x.shape
    h = modulate(rms_norm(x, lw['g_attn']), sh1, sc1)
    cb, cc, cv, gq, gk, gv, nq, nk, nv, mq, mkv, mkr, gates = in_project(h, lw['w_in'])
    y_conv = short_conv(cb, cc, cv, lw['w_conv'])
    k_gqa = gk.reshape(n, l, GQA_KV_HEADS, HEAD_DIM)
    v_gqa = gv.reshape(n, l, GQA_KV_HEADS, HEAD_DIM)
    y_gqa = dense_attention(gq.reshape(n, l, GQA_KV_HEADS, GQA_GROUP, HEAD_DIM), k_gqa, v_gqa,
                            ATTN_SCALE, lw['gqa_sink'].reshape(GQA_KV_HEADS, GQA_GROUP))
    k_na = nk.reshape(n, l, NA_HEADS, HEAD_DIM)
    v_na = nv.reshape(n, l, NA_HEADS, HEAD_DIM)
    y_na = dense_attention(nq.reshape(n, l, NA_HEADS, 1, HEAD_DIM), k_na, v_na, ATTN_SCALE)
    ckv = rms_norm(mkv, lw['mla_g_kv'])
    q_nope, q_rope = mla_queries(mq, lw['mla_g_q'], lw['mla_w_uq'])
    k_mla, v_mla = mla_keys_values(ckv, mkr, lw['mla_w_ukv'])
    q_mla = jnp.concatenate([q_nope, q_rope], axis=-1)[:, :, :, None, :]
    y_mla = dense_attention(q_mla, k_mla, v_mla, MLA_SCALE)
    out = merge_branches(gates, y_conv, y_gqa.reshape(n, l, GQA_WIDTH), y_na.reshape(n, l, NA_WIDTH),
                         y_mla.reshape(n, l, MLA_WIDTH), lw)
    x = x + gt1 * out
    x = x + gt2 * squared_relu_mlp(modulate(rms_norm(x, lw['g_mlp']), sh2, sc2), lw['w_ff1'], lw['w_ff2'])
    return x, (k_gqa, v_gqa, k_na, v_na, ckv, mkr)


def latent_layer(x, c, cache, lw):
    k_gqa_c, v_gqa_c, k_na_c, v_na_c, ckv_c, krope_c = cache
    sh1, sc1, gt1, sh2, sc2, gt2 = adaln(c, lw['w_mod'], lw['b_mod'])
    n, s, _ = x.shape
    rows, cols = grid_positions(s)
    h = modulate(rms_norm(x, lw['g_attn']), sh1, sc1)
    cb, cc, cv, gq, gk, gv, nq, nk, nv, mq, mkv, mkr, gates = in_project(h, lw['w_in'])
    y_conv = short_conv(cb, cc, cv, lw['w_conv'])
    q_g = rope_2d(gq.reshape(n, s, GQA_HEADS, HEAD_DIM), rows, cols).reshape(n, s, GQA_KV_HEADS, GQA_GROUP, HEAD_DIM)
    k_g = rope_2d(gk.reshape(n, s, GQA_KV_HEADS, HEAD_DIM), rows, cols)
    v_g = gv.reshape(n, s, GQA_KV_HEADS, HEAD_DIM)
    y_gqa = banded_window_attention(q_g, k_g, v_g, k_gqa_c, v_gqa_c,
                                    lw['gqa_sink'].reshape(GQA_KV_HEADS, GQA_GROUP), ATTN_SCALE)
    y_na = neighbourhood_attention(nq.reshape(n, s, NA_HEADS, HEAD_DIM), nk.reshape(n, s, NA_HEADS, HEAD_DIM),
                                   nv.reshape(n, s, NA_HEADS, HEAD_DIM), k_na_c, v_na_c, lw['na_rpb'], ATTN_SCALE)
    ckv = rms_norm(mkv, lw['mla_g_kv'])
    q_nope, q_rope = mla_queries(mq, lw['mla_g_q'], lw['mla_w_uq'])
    q_rope = rope_2d(q_rope, rows, cols)
    k_rope = rope_2d(mkr[:, :, None, :], rows, cols)[:, :, 0, :]
    k_lat, v_lat = mla_keys_values(ckv, k_rope, lw['mla_w_ukv'])
    k_ctx, v_ctx = mla_keys_values(ckv_c, krope_c, lw['mla_w_ukv'])
    q_mla = jnp.concatenate([q_nope, q_rope], axis=-1)[:, :, :, None, :]
    y_mla = dense_attention(q_mla, jnp.concatenate([k_lat, k_ctx], axis=1),
                            jnp.concatenate([v_lat, v_ctx], axis=1), MLA_SCALE)
    out = merge_branches(gates, y_conv, y_gqa.reshape(n, s, GQA_WIDTH), y_na.reshape(n, s, NA_WIDTH),
                         y_mla.reshape(n, s, MLA_WIDTH), lw)
    x = x + gt1 * out
    x = x + gt2 * squared_relu_mlp(modulate(rms_norm(x, lw['g_mlp']), sh2, sc2), lw['w_ff1'], lw['w_ff2'])
    return x


def setup_inputs(seed: int = 0) -> dict:
    key = jax.random.key(seed)
    ks = jax.random.split(key, 32)

    def nrm(k, shape, scale=1.0):
        return jax.random.normal(k, shape, dtype=jnp.float32) * scale

    d = D_MODEL
    return {
        "x_prompt": nrm(ks[0], (BATCH, SEQ, d)),
        "x_sample": nrm(ks[1], (DEC_BATCH, DEC_SEQ, d)),
        "cache_gqa_k": nrm(ks[2], (DEC_BATCH, DEPTH, PAST_LEN, GQA_KV_HEADS, HEAD_DIM)),
        "cache_gqa_v": nrm(ks[3], (DEC_BATCH, DEPTH, PAST_LEN, GQA_KV_HEADS, HEAD_DIM)),
        "cache_na_k": nrm(ks[4], (DEC_BATCH, DEPTH, PAST_LEN, NA_HEADS, HEAD_DIM)),
        "cache_na_v": nrm(ks[5], (DEC_BATCH, DEPTH, PAST_LEN, NA_HEADS, HEAD_DIM)),
        "cache_mla_ckv": nrm(ks[6], (DEC_BATCH, DEPTH, PAST_LEN, MLA_KV_RANK)),
        "cache_mla_krope": nrm(ks[7], (DEC_BATCH, DEPTH, PAST_LEN, MLA_ROPE)),
        "c": nrm(ks[8], (DEC_BATCH, d)),
        "c_ctx": nrm(ks[9], (d,)),
        "w_mod": nrm(ks[10], (DEPTH, d, 6 * d), 0.5 * d ** -0.5),
        "b_mod": nrm(ks[11], (DEPTH, 6 * d), 0.01),
        "g_attn": 1.0 + nrm(ks[12], (DEPTH, d), 0.02),
        "g_mlp": 1.0 + nrm(ks[13], (DEPTH, d), 0.02),
        "w_in": nrm(ks[14], (DEPTH, d, D_IN), d ** -0.5),
        "w_conv": nrm(ks[15], (DEPTH, CONV_K, CONV_WIDTH), CONV_K ** -0.5),
        "gqa_sink": nrm(ks[16], (DEPTH, GQA_HEADS), 0.5),
        "na_rpb": nrm(ks[17], (DEPTH, NA_HEADS, 2 * NA_WIN_H - 1, 2 * NA_WIN_W - 1), 0.1),
        "mla_g_q": 1.0 + nrm(ks[18], (DEPTH, MLA_Q_RANK), 0.02),
        "mla_w_uq": nrm(ks[19], (DEPTH, MLA_Q_RANK, MLA_HEADS * (MLA_NOPE + MLA_ROPE)), MLA_Q_RANK ** -0.5),
        "mla_g_kv": 1.0 + nrm(ks[20], (DEPTH, MLA_KV_RANK), 0.02),
        "mla_w_ukv": nrm(ks[21], (DEPTH, MLA_KV_RANK, MLA_HEADS * (MLA_NOPE + MLA_V)), MLA_KV_RANK ** -0.5),
        "w_branch_conv": nrm(ks[22], (DEPTH, CONV_WIDTH, d), CONV_WIDTH ** -0.5),
        "w_branch_gqa": nrm(ks[23], (DEPTH, GQA_WIDTH, d), GQA_WIDTH ** -0.5),
        "w_branch_na": nrm(ks[24], (DEPTH, NA_WIDTH, d), NA_WIDTH ** -0.5),
        "w_branch_mla": nrm(ks[25], (DEPTH, MLA_WIDTH, d), MLA_WIDTH ** -0.5),
        "w_o": nrm(ks[26], (DEPTH, d, d), d ** -0.5),
        "w_ff1": nrm(ks[27], (DEPTH, d, D_FF), d ** -0.5),
        "w_ff2": nrm(ks[28], (DEPTH, D_FF, d), D_FF ** -0.5),
        "g_final": 1.0 + nrm(ks[29], (d,), 0.02),
    }


def reference(x_prompt, x_sample, cache_gqa_k, cache_gqa_v, cache_na_k, cache_na_v, cache_mla_ckv,
              cache_mla_krope, c, c_ctx, w_mod, b_mod, g_attn, g_mlp, w_in, w_conv, gqa_sink, na_rpb,
              mla_g_q, mla_w_uq, mla_g_kv, mla_w_ukv, w_branch_conv, w_branch_gqa, w_branch_na,
              w_branch_mla, w_o, w_ff1, w_ff2, g_final):
    h_ctx = x_prompt
    h_lat = x_sample
    st_gqa_k, st_gqa_v, st_na_k, st_na_v, st_ckv, st_krope = [], [], [], [], [], []
    for l in range(DEPTH):
        lw = {
            'w_mod': w_mod[l], 'b_mod': b_mod[l], 'g_attn': g_attn[l], 'g_mlp': g_mlp[l],
            'w_in': w_in[l], 'w_conv': w_conv[l], 'gqa_sink': gqa_sink[l], 'na_rpb': na_rpb[l],
            'mla_g_q': mla_g_q[l], 'mla_w_uq': mla_w_uq[l], 'mla_g_kv': mla_g_kv[l],
            'mla_w_ukv': mla_w_ukv[l], 'w_branch_conv': w_branch_conv[l],
            'w_branch_gqa': w_branch_gqa[l], 'w_branch_na': w_branch_na[l],
            'w_branch_mla': w_branch_mla[l], 'w_o': w_o[l], 'w_ff1': w_ff1[l], 'w_ff2': w_ff2[l],
        }
        h_ctx, st = context_layer(h_ctx, c_ctx[None, :], lw)
        st_gqa_k.append(st[0]); st_gqa_v.append(st[1])
        st_na_k.append(st[2]); st_na_v.append(st[3])
        st_ckv.append(st[4]); st_krope.append(st[5])
        cache_l = (cache_gqa_k[:, l], cache_gqa_v[:, l], cache_na_k[:, l], cache_na_v[:, l],
                   cache_mla_ckv[:, l], cache_mla_krope[:, l])
        h_lat = latent_layer(h_lat, c, cache_l, lw)
    y_prompt = rms_norm(h_ctx, g_final)
    y_sample = rms_norm(h_lat, g_final)
    return (y_prompt, y_sample,
            jnp.stack(st_gqa_k, axis=1), jnp.stack(st_gqa_v, axis=1),
            jnp.stack(st_na_k, axis=1), jnp.stack(st_na_v, axis=1),
            jnp.stack(st_ckv, axis=1), jnp.stack(st_krope, axis=1))
```

```python
import functools

import numpy as np
import jax
import jax.numpy as jnp
from jax import lax
from jax.experimental import pallas as pl
from jax.experimental.pallas import tpu as pltpu

D_MODEL = 1024
BATCH = 32
SEQ = 256
DEPTH = 2
DEC_BATCH = 8
DEC_SEQ = 1024
PAST_LEN = 512
GRID_W = 64
GRID_ROWS = DEC_SEQ // GRID_W
HEAD_DIM = 64
CONV_WIDTH = 256
CONV_K = 3
GQA_HEADS = 8
GQA_KV_HEADS = 2
WINDOW = 128
BAND_BLOCK = 128
NA_HEADS = 4
NA_WIN_H = 8
NA_WIN_W = 16
MLA_HEADS = 4
MLA_Q_RANK = 256
MLA_KV_RANK = 128
MLA_NOPE = 64
MLA_ROPE = 32
MLA_V = 64
D_FF = 4 * D_MODEL
N_BRANCH = 4
ROPE_BASE = 10000.0
EPS = 1e-6
NEG_INF = -1e30
ATTN_SCALE = HEAD_DIM ** -0.5
MLA_SCALE = (MLA_NOPE + MLA_ROPE) ** -0.5

LANES = 128
MOD_ROWS = 16
CTX_MOD_ROW = DEC_BATCH

C_CB, C_CC, C_CV = 0, 256, 512
C_GQ, C_GK, C_GV = 768, 1280, 1408
C_NQ, C_NK, C_NV = 1536, 1792, 2048
C_MQ, C_MKV, C_MKR = 2304, 2560, 2688
MAIN_COLS = 2816
ORIG_MKR = 2688
ORIG_GATES = 2720
Y_CONV, Y_GQA, Y_NA, Y_MLA = 0, 256, 768, 1024
Y_COLS = 1280
NA_PATTERNS = 8
NA_KEYS = NA_WIN_H * GRID_W

VMEM_LIMIT = 56 * 1024 * 1024

F32 = jnp.float32
BF16 = jnp.bfloat16


def _dot(a, b):
    return jnp.dot(a, b, preferred_element_type=F32)


def _dot_nt(a, b):
    return lax.dot_general(a, b, (((1,), (1,)), ((), ())), preferred_element_type=F32)


def _rms(x, g):
    return x * lax.rsqrt(jnp.mean(x * x, axis=-1, keepdims=True) + EPS) * g


def _lane_lt(shape, n):
    return lax.broadcasted_iota(jnp.int32, shape, len(shape) - 1) < n


def _lane_group(shape, lo, hi):
    lane = lax.broadcasted_iota(jnp.int32, shape, len(shape) - 1)
    return (lane >= lo) & (lane < hi)


def _softmax_parts(parts, sink=None):
    m = parts[0].max(axis=-1, keepdims=True)
    for p in parts[1:]:
        m = jnp.maximum(m, p.max(axis=-1, keepdims=True))
    if sink is not None:
        m = jnp.maximum(m, sink)
    es = [jnp.exp(p - m) for p in parts]
    l = es[0].sum(axis=-1, keepdims=True)
    for e in es[1:]:
        l = l + e.sum(axis=-1, keepdims=True)
    if sink is not None:
        l = l + jnp.exp(sink - m)
    return [e.astype(BF16) for e in es], 1.0 / l


def _short_conv(cb, cc, cv, w):
    u = cc * cv
    t = u.shape[0]
    row = lax.broadcasted_iota(jnp.int32, u.shape, 0)
    prev = jnp.where(row == 0, 0.0, pltpu.roll(u, 1, 0))
    nxt = jnp.where(row == t - 1, 0.0, pltpu.roll(u, t - 1, 0))
    return cb * (prev * w[0:1, :] + u * w[1:2, :] + nxt * w[2:3, :])


def _rope(x, cos, sin_hi, sin_lo, half):
    n = x.shape[-1]
    return x * cos + pltpu.roll(x, n - half, 1) * sin_lo + pltpu.roll(x, half, 1) * sin_hi


def _mod_kernel(c_ref, w_ref, b_ref, o_ref):
    c = c_ref[...]
    s = c * jax.nn.sigmoid(c)
    o_ref[...] = _dot(s.astype(BF16), w_ref[...].astype(BF16)) + b_ref[...]


def _mod_call(c16, w_mod, b_mod):
    tn = 1536
    return pl.pallas_call(
        _mod_kernel,
        grid=(DEPTH, 6 * D_MODEL // tn),
        in_specs=[
            pl.BlockSpec((MOD_ROWS, D_MODEL), lambda l, j: (0, 0)),
            pl.BlockSpec((None, D_MODEL, tn), lambda l, j: (l, 0, j)),
            pl.BlockSpec((None, 1, tn), lambda l, j: (l, 0, j)),
        ],
        out_specs=pl.BlockSpec((None, MOD_ROWS, tn), lambda l, j: (l, 0, j)),
        out_shape=jax.ShapeDtypeStruct((DEPTH, MOD_ROWS, 6 * D_MODEL), F32),
        compiler_params=pltpu.CompilerParams(
            dimension_semantics=("arbitrary", "arbitrary"), vmem_limit_bytes=VMEM_LIMIT),
        name="adaln_mod",
    )(c16, w_mod, b_mod.reshape(DEPTH, 1, 6 * D_MODEL))


def _nab_kernel(rpb_ref, o_ref):
    l = pl.program_id(0)
    c = lax.broadcasted_iota(jnp.int32, (GRID_W, GRID_W), 0)
    w = lax.broadcasted_iota(jnp.int32, (GRID_W, GRID_W), 1)
    dc = w - c + (NA_WIN_W - 1)
    c0 = jnp.clip(c - NA_WIN_W // 2, 0, GRID_W - NA_WIN_W)
    outside = (w < c0) | (w >= c0 + NA_WIN_W)
    n_dr, n_dc = 2 * NA_WIN_H - 1, 2 * NA_WIN_W - 1
    for h in range(NA_HEADS):
        for d in range(n_dr):
            base = ((l * NA_HEADS + h) * n_dr + d) * n_dc
            tile = jnp.full((GRID_W, GRID_W), NEG_INF, F32)
            for j in range(n_dc):
                tile = jnp.where(dc == j, rpb_ref[base + j], tile)
            tile = jnp.where(outside, NEG_INF, tile)
            for p in range(NA_PATTERNS):
                a = d - (NA_WIN_H - 1) + p
                if 0 <= a < NA_WIN_H:
                    col = h * NA_KEYS + a * GRID_W
                    o_ref[p, :, col:col + GRID_W] = tile


def _nab_call(na_rpb):
    return pl.pallas_call(
        _nab_kernel,
        grid=(DEPTH,),
        in_specs=[pl.BlockSpec(memory_space=pltpu.SMEM)],
        out_specs=pl.BlockSpec((None, NA_PATTERNS, GRID_W, NA_HEADS * NA_KEYS), lambda l: (l, 0, 0, 0)),
        out_shape=jax.ShapeDtypeStruct((DEPTH, NA_PATTERNS, GRID_W, NA_HEADS * NA_KEYS), F32),
        compiler_params=pltpu.CompilerParams(dimension_semantics=("arbitrary",)),
        name="na_bias_tables",
    )(na_rpb.reshape(-1))


def _pre_kernel(x_ref, mod_ref, g_ref, w_ref, o_ref):
    h = _rms(x_ref[...], g_ref[...]) * (1.0 + mod_ref[:, D_MODEL:2 * D_MODEL]) + mod_ref[:, 0:D_MODEL]
    o_ref[...] = _dot(h.astype(BF16), w_ref[...])


def _pre_call(x, mods_l, g, w_main, ctx, name):
    n, t, _ = x.shape
    tm = min(t, 512)
    mod_map = (lambda s, i: (CTX_MOD_ROW, 0, 0)) if ctx else (lambda s, i: (s, 0, 0))
    return pl.pallas_call(
        _pre_kernel,
        grid=(n, t // tm),
        in_specs=[
            pl.BlockSpec((None, tm, D_MODEL), lambda s, i: (s, i, 0)),
            pl.BlockSpec((None, 1, 6 * D_MODEL), mod_map),
            pl.BlockSpec((1, D_MODEL), lambda s, i: (0, 0)),
            pl.BlockSpec((D_MODEL, MAIN_COLS), lambda s, i: (0, 0)),
        ],
        out_specs=pl.BlockSpec((None, tm, MAIN_COLS), lambda s, i: (s, i, 0)),
        out_shape=jax.ShapeDtypeStruct((n, t, MAIN_COLS), F32),
        compiler_params=pltpu.CompilerParams(
            dimension_semantics=("arbitrary", "arbitrary"), vmem_limit_bytes=VMEM_LIMIT),
        name=name,
    )(x, mods_l, g, w_main)


def _pair_stack(pair, first_lo, second_lo):
    lo = _lane_lt(pair.shape, HEAD_DIM)
    swapped = None
    if not first_lo or second_lo:
        swapped = pltpu.roll(pair, HEAD_DIM, 1)
    top = jnp.where(lo, pair if first_lo else swapped, 0.0)
    bot = jnp.where(lo, 0.0, swapped if second_lo else pair)
    return top, bot


def _mla_qkv(mq, ckv, g_q, w_uq, w_ukv):
    q = _dot(_rms(mq, g_q).astype(BF16), w_uq) * MLA_SCALE
    kv = _dot(ckv.astype(BF16), w_ukv)
    return q, kv


def _mla_key_rows(kn_pair, kr4, j, odd):
    lo = _lane_lt(kn_pair.shape, HEAD_DIM)
    h = 2 * j + odd
    nope = jnp.where(lo, 0.0, kn_pair) if odd else jnp.where(lo, kn_pair, 0.0)
    rope = jnp.where(_lane_group(kr4.shape, h * MLA_ROPE, (h + 1) * MLA_ROPE), kr4, 0.0)
    return jnp.concatenate([nope, rope], axis=1)


def _mla_val_rows(v_pair, odd):
    lo = _lane_lt(v_pair.shape, HEAD_DIM)
    return jnp.where(lo, 0.0, v_pair) if odd else jnp.where(lo, v_pair, 0.0)


def _finish_pair(pe, po, rle, rlo, ve, vo):
    o = None
    for p, v in list(zip(pe, ve)) + list(zip(po, vo)):
        t = _dot(p, v)
        o = t if o is None else o + t
    return o * jnp.where(_lane_lt(o.shape, HEAD_DIM), rle, rlo)


def _ctx_mixer_kernel(sink_ref, p_ref, wc_ref, gq_ref, wuq_ref, gkv_ref, wukv_ref,
                      y_ref, kg_ref, vg_ref, kn_ref, vn_ref, ckv_ref, kr_ref):
    t = SEQ
    col = lambda c, n: p_ref[:, c:c + n]

    kg_ref[...] = col(C_GK, LANES)
    vg_ref[...] = col(C_GV, LANES)
    kn_ref[...] = col(C_NK, 2 * LANES)
    vn_ref[...] = col(C_NV, 2 * LANES)
    ckv = _rms(col(C_MKV, MLA_KV_RANK), gkv_ref[...])
    ckv_ref[...] = ckv
    kr_ref[...] = col(C_MKR, MLA_ROPE)

    y_ref[:, Y_CONV:Y_CONV + CONV_WIDTH] = _short_conv(
        col(C_CB, CONV_WIDTH), col(C_CC, CONV_WIDTH), col(C_CV, CONV_WIDTH), wc_ref[...]).astype(BF16)

    kpair, vpair = col(C_GK, LANES), col(C_GV, LANES)
    row2 = lax.broadcasted_iota(jnp.int32, (2 * t, 1), 0) < t
    for g in range(GQA_KV_HEADS):
        ke, ko = _pair_stack(kpair, g == 0, g == 0)
        ve, vo = _pair_stack(vpair, g == 0, g == 0)
        q = jnp.concatenate([col(C_GQ + (2 * g) * LANES, LANES), col(C_GQ + (2 * g + 1) * LANES, LANES)], axis=0)
        q = (q * ATTN_SCALE).astype(BF16)
        se, so = _dot_nt(q, ke.astype(BF16)), _dot_nt(q, ko.astype(BF16))
        sink_e = jnp.where(row2, sink_ref[4 * g], sink_ref[4 * g + 2])
        sink_o = jnp.where(row2, sink_ref[4 * g + 1], sink_ref[4 * g + 3])
        pe, rle = _softmax_parts([se], sink_e)
        po, rlo = _softmax_parts([so], sink_o)
        o = _finish_pair(pe, po, rle, rlo, [ve.astype(BF16)], [vo.astype(BF16)]).astype(BF16)
        c0 = Y_GQA + (2 * g) * LANES
        y_ref[:, c0:c0 + LANES] = o[0:t]
        y_ref[:, c0 + LANES:c0 + 2 * LANES] = o[t:2 * t]

    for j in range(NA_HEADS // 2):
        kpair, vpair = col(C_NK + j * LANES, LANES), col(C_NV + j * LANES, LANES)
        ke, ko = _pair_stack(kpair, True, False)
        ve, vo = _pair_stack(vpair, True, False)
        q = (col(C_NQ + j * LANES, LANES) * ATTN_SCALE).astype(BF16)
        pe, rle = _softmax_parts([_dot_nt(q, ke.astype(BF16))])
        po, rlo = _softmax_parts([_dot_nt(q, ko.astype(BF16))])
        o = _finish_pair(pe, po, rle, rlo, [ve.astype(BF16)], [vo.astype(BF16)])
        y_ref[:, Y_NA + j * LANES:Y_NA + (j + 1) * LANES] = o.astype(BF16)

    q, kv = _mla_qkv(col(C_MQ, MLA_Q_RANK), ckv, gq_ref[...], wuq_ref[...], wukv_ref[...])
    kr4 = col(C_MKR, LANES)
    q_rope = q[:, 2 * LANES:3 * LANES]
    for j in range(MLA_HEADS // 2):
        qj = jnp.concatenate([q[:, j * LANES:(j + 1) * LANES], q_rope], axis=1).astype(BF16)
        kn_pair = kv[:, j * LANES:(j + 1) * LANES]
        v_pair = kv[:, 2 * LANES + j * LANES:2 * LANES + (j + 1) * LANES]
        pe, rle = _softmax_parts([_dot_nt(qj, _mla_key_rows(kn_pair, kr4, j, 0).astype(BF16))])
        po, rlo = _softmax_parts([_dot_nt(qj, _mla_key_rows(kn_pair, kr4, j, 1).astype(BF16))])
        o = _finish_pair(pe, po, rle, rlo, [_mla_val_rows(v_pair, 0).astype(BF16)],
                         [_mla_val_rows(v_pair, 1).astype(BF16)])
        y_ref[:, Y_MLA + j * LANES:Y_MLA + (j + 1) * LANES] = o.astype(BF16)


def _ctx_mixer_call(proj, sink, w_conv, g_q, w_uq, g_kv, w_ukv, name):
    n = BATCH
    seq_block = lambda w: pl.BlockSpec((None, SEQ, w), lambda s: (s, 0, 0))
    const = lambda shape: pl.BlockSpec(shape, lambda s: (0,) * len(shape))
    out_widths = (Y_COLS, LANES, LANES, 2 * LANES, 2 * LANES, MLA_KV_RANK, MLA_ROPE)
    out_dtypes = (BF16,) + (F32,) * 6
    return pl.pallas_call(
        _ctx_mixer_kernel,
        grid=(n,),
        in_specs=[
            pl.BlockSpec(memory_space=pltpu.SMEM),
            seq_block(MAIN_COLS),
            const((CONV_K, CONV_WIDTH)),
            const((1, MLA_Q_RANK)),
            const((MLA_Q_RANK, 3 * LANES)),
            const((1, MLA_KV_RANK)),
            const((MLA_KV_RANK, 4 * LANES)),
        ],
        out_specs=[seq_block(w) for w in out_widths],
        out_shape=[jax.ShapeDtypeStruct((n, SEQ, w), dt) for w, dt in zip(out_widths, out_dtypes)],
        compiler_params=pltpu.CompilerParams(
            dimension_semantics=("arbitrary",), vmem_limit_bytes=VMEM_LIMIT),
        name=name,
    )(sink, proj, w_conv, g_q, w_uq, g_kv, w_ukv)


def _lat_mixer_kernel(sink_ref, p_ref, wc_ref, gq_ref, wuq_ref, gkv_ref, wukv_ref,
                      rc_ref, rsh_ref, rsl_ref, mc_ref, msh_ref, msl_ref, band_ref, nab_ref,
                      cgk_ref, cgv_ref, cnk_ref, cnv_ref, cckv_ref, ckr_ref,
                      y_ref,
                      gq_s, gk_s, gv_s, gkc_s, gvc_s, nq_s, nk_s, nv_s, nkc_s, nvc_s, mq_s, mk_s, mv_s):
    t = DEC_SEQ
    col = lambda c, n: p_ref[:, c:c + n]
    rope64 = lambda x: _rope(x, rc_ref[...], rsh_ref[...], rsl_ref[...], HEAD_DIM // 4)
    rope32 = lambda x: _rope(x, mc_ref[...], msh_ref[...], msl_ref[...], MLA_ROPE // 4)

    y_ref[:, Y_CONV:Y_CONV + CONV_WIDTH] = _short_conv(
        col(C_CB, CONV_WIDTH), col(C_CC, CONV_WIDTH), col(C_CV, CONV_WIDTH), wc_ref[...]).astype(BF16)

    for j in range(GQA_HEADS // 2):
        gq_s[:, j * LANES:(j + 1) * LANES] = (rope64(col(C_GQ + j * LANES, LANES)) * ATTN_SCALE).astype(BF16)
    kpair, vpair = rope64(col(C_GK, LANES)), col(C_GV, LANES)
    for g in range(GQA_KV_HEADS):
        for dst, src in ((gk_s, kpair), (gv_s, vpair), (gkc_s, cgk_ref[...]), (gvc_s, cgv_ref[...])):
            top, bot = _pair_stack(src, g == 0, g == 0)
            dst[2 * g] = top.astype(BF16)
            dst[2 * g + 1] = bot.astype(BF16)

    def gqa_block(b, c_lo, c_hi):
        q0 = pl.multiple_of(b * BAND_BLOCK, BAND_BLOCK)
        k0 = pl.multiple_of(q0 + (c_lo - WINDOW), BAND_BLOCK)
        n = c_hi - c_lo
        band = band_ref[:, c_lo:c_hi]
        row2 = lax.broadcasted_iota(jnp.int32, (2 * BAND_BLOCK, 1), 0) < BAND_BLOCK
        for g in range(GQA_KV_HEADS):
            q = jnp.concatenate([gq_s[pl.ds(q0, BAND_BLOCK), (2 * g) * LANES:(2 * g + 1) * LANES],
                                 gq_s[pl.ds(q0, BAND_BLOCK), (2 * g + 1) * LANES:(2 * g + 2) * LANES]], axis=0)
            parts, recips = [], []
            for odd in range(2):
                s_loc = _dot_nt(q, gk_s[2 * g + odd, pl.ds(k0, n), :]) + band
                s_ctx = _dot_nt(q, gkc_s[2 * g + odd])
                sink = jnp.where(row2, sink_ref[4 * g + odd], sink_ref[4 * g + 2 + odd])
                p, r = _softmax_parts([s_loc, s_ctx], sink)
                parts.append(p)
                recips.append(r)
            o = _finish_pair(parts[0], parts[1], recips[0], recips[1],
                             [gv_s[2 * g, pl.ds(k0, n), :], gvc_s[2 * g]],
                             [gv_s[2 * g + 1, pl.ds(k0, n), :], gvc_s[2 * g + 1]]).astype(BF16)
            c0 = Y_GQA + (2 * g) * LANES
            y_ref[pl.ds(q0, BAND_BLOCK), c0:c0 + LANES] = o[0:BAND_BLOCK]
            y_ref[pl.ds(q0, BAND_BLOCK), c0 + LANES:c0 + 2 * LANES] = o[BAND_BLOCK:2 * BAND_BLOCK]

    span = BAND_BLOCK + 2 * WINDOW
    nb = t // BAND_BLOCK
    gqa_block(0, WINDOW, span)
    lax.fori_loop(1, nb - 1, lambda b, c: (gqa_block(b, 0, span), c)[1], 0)
    gqa_block(nb - 1, 0, span - WINDOW)

    for j in range(NA_HEADS // 2):
        nq_s[:, j * LANES:(j + 1) * LANES] = (col(C_NQ + j * LANES, LANES) * ATTN_SCALE).astype(BF16)
        for dst, src in ((nk_s, col(C_NK + j * LANES, LANES)), (nv_s, col(C_NV + j * LANES, LANES)),
                         (nkc_s, cnk_ref[:, j * LANES:(j + 1) * LANES]),
                         (nvc_s, cnv_ref[:, j * LANES:(j + 1) * LANES])):
            top, bot = _pair_stack(src, True, False)
            dst[2 * j] = top.astype(BF16)
            dst[2 * j + 1] = bot.astype(BF16)

    def na_row(r, carry):
        q0 = pl.multiple_of(r * GRID_W, GRID_W)
        r0 = jnp.clip(r - NA_WIN_H // 2, 0, GRID_ROWS - NA_WIN_H)
        k0 = pl.multiple_of(r0 * GRID_W, GRID_W)
        pat = r - r0
        for j in range(NA_HEADS // 2):
            q = nq_s[pl.ds(q0, GRID_W), j * LANES:(j + 1) * LANES]
            parts, recips = [], []
            for odd in range(2):
                h = 2 * j + odd
                s_loc = _dot_nt(q, nk_s[h, pl.ds(k0, NA_KEYS), :]) + nab_ref[pat, :, h * NA_KEYS:(h + 1) * NA_KEYS]
                s_ctx = _dot_nt(q, nkc_s[h])
                p, rcp = _softmax_parts([s_loc, s_ctx])
                parts.append(p)
                recips.append(rcp)
            o = _finish_pair(parts[0], parts[1], recips[0], recips[1],
                             [nv_s[2 * j, pl.ds(k0, NA_KEYS), :], nvc_s[2 * j]],
                             [nv_s[2 * j + 1, pl.ds(k0, NA_KEYS), :], nvc_s[2 * j + 1]])
            y_ref[pl.ds(q0, GRID_W), Y_NA + j * LANES:Y_NA + (j + 1) * LANES] = o.astype(BF16)
        return carry

    lax.fori_loop(0, GRID_ROWS, na_row, 0)

    ckv = _rms(col(C_MKV, MLA_KV_RANK), gkv_ref[...])
    q, kv = _mla_qkv(col(C_MQ, MLA_Q_RANK), ckv, gq_ref[...], wuq_ref[...], wukv_ref[...])
    mq_s[:, 0:2 * LANES] = q[:, 0:2 * LANES].astype(BF16)
    mq_s[:, 2 * LANES:3 * LANES] = rope32(q[:, 2 * LANES:3 * LANES]).astype(BF16)
    kr4 = rope32(col(C_MKR, LANES))
    kv_c = _dot(cckv_ref[...].astype(BF16), wukv_ref[...])
    kr4_c = ckr_ref[...]
    for j in range(MLA_HEADS // 2):
        for odd in range(2):
            h = 2 * j + odd
            for rows, kv_x, kr_x in ((slice(0, t), kv, kr4), (slice(t, t + PAST_LEN), kv_c, kr4_c)):
                kn_pair = kv_x[:, j * LANES:(j + 1) * LANES]
                v_pair = kv_x[:, 2 * LANES + j * LANES:2 * LANES + (j + 1) * LANES]
                mk_s[h, rows, :] = _mla_key_rows(kn_pair, kr_x, j, odd).astype(BF16)
                mv_s[h, rows, :] = _mla_val_rows(v_pair, odd).astype(BF16)

    tq = 256

    def mla_block(i, carry):
        q0 = pl.multiple_of(i * tq, tq)
        for j in range(MLA_HEADS // 2):
            qj = jnp.concatenate([mq_s[pl.ds(q0, tq), j * LANES:(j + 1) * LANES],
                                  mq_s[pl.ds(q0, tq), 2 * LANES:3 * LANES]], axis=1)
            pe, rle = _softmax_parts([_dot_nt(qj, mk_s[2 * j])])
            po, rlo = _softmax_parts([_dot_nt(qj, mk_s[2 * j + 1])])
            o = _finish_pair(pe, po, rle, rlo, [mv_s[2 * j]], [mv_s[2 * j + 1]])
            y_ref[pl.ds(q0, tq), Y_MLA + j * LANES:Y_MLA + (j + 1) * LANES] = o.astype(BF16)
        return carry

    lax.fori_loop(0, t // tq, mla_block, 0)


def _rope_tables(group, half):
    tok = np.arange(DEC_SEQ)
    pos = np.stack([tok // GRID_W, tok % GRID_W], axis=1).astype(np.float64)
    inv = ROPE_BASE ** (-np.arange(half, dtype=np.float64) / half)
    lane = np.arange(LANES) % group
    axis = lane // (2 * half)
    within = lane % (2 * half)
    ang = pos[:, axis] * inv[within % half][None, :]
    cos, sin = np.cos(ang), np.sin(ang)
    upper = (within >= half)[None, :]
    sin_hi = np.where(upper, sin, 0.0)
    sin_lo = np.where(upper, 0.0, -sin)
    return tuple(jnp.asarray(a, dtype=F32) for a in (cos, sin_hi, sin_lo))


def _band_mask():
    i = np.arange(BAND_BLOCK)[:, None]
    c = np.arange(BAND_BLOCK + 2 * WINDOW)[None, :]
    ok = (c >= i) & (c <= i + 2 * WINDOW)
    m = np.where(ok, 0.0, NEG_INF)
    return jnp.asarray(np.concatenate([m, m], axis=0), dtype=F32)


def _lat_mixer_call(proj, sink, w_conv, g_q, w_uq, g_kv, w_ukv, nab, caches, layer, name):
    n, t = DEC_BATCH, DEC_SEQ
    one = pl.Buffered(1)
    const = lambda shape: pl.BlockSpec(shape, lambda s: (0,) * len(shape), pipeline_mode=one)
    cache = lambda w: pl.BlockSpec((None, None, PAST_LEN, w), lambda s: (s, layer, 0, 0))
    rope_g = _rope_tables(HEAD_DIM, HEAD_DIM // 4)
    rope_m = _rope_tables(MLA_ROPE, MLA_ROPE // 4)
    span = BAND_BLOCK + 2 * WINDOW
    scratch = [
        pltpu.VMEM((t, GQA_HEADS // 2 * LANES), BF16),
        pltpu.VMEM((4, t, LANES), BF16),
        pltpu.VMEM((4, t, LANES), BF16),
        pltpu.VMEM((4, PAST_LEN, LANES), BF16),
        pltpu.VMEM((4, PAST_LEN, LANES), BF16),
        pltpu.VMEM((t, NA_HEADS // 2 * LANES), BF16),
        pltpu.VMEM((4, t, LANES), BF16),
        pltpu.VMEM((4, t, LANES), BF16),
        pltpu.VMEM((4, PAST_LEN, LANES), BF16),
        pltpu.VMEM((4, PAST_LEN, LANES), BF16),
        pltpu.VMEM((t, 3 * LANES), BF16),
        pltpu.VMEM((4, t + PAST_LEN, 2 * LANES), BF16),
        pltpu.VMEM((4, t + PAST_LEN, LANES), BF16),
    ]
    return pl.pallas_call(
        _lat_mixer_kernel,
        grid=(n,),
        in_specs=[
            pl.BlockSpec(memory_space=pltpu.SMEM),
            pl.BlockSpec((None, t, MAIN_COLS), lambda s: (s, 0, 0), pipeline_mode=one),
            const((CONV_K, CONV_WIDTH)),
            const((1, MLA_Q_RANK)),
            const((MLA_Q_RANK, 3 * LANES)),
            const((1, MLA_KV_RANK)),
            const((MLA_KV_RANK, 4 * LANES)),
        ] + [const((t, LANES))] * 6 + [
            const((2 * BAND_BLOCK, span)),
            const((NA_PATTERNS, GRID_W, NA_HEADS * NA_KEYS)),
            cache(LANES), cache(LANES), cache(2 * LANES), cache(2 * LANES), cache(MLA_KV_RANK), cache(LANES),
        ],
        out_specs=pl.BlockSpec((None, t, Y_COLS), lambda s: (s, 0, 0)),
        out_shape=jax.ShapeDtypeStruct((n, t, Y_COLS), BF16),
        scratch_shapes=scratch,
        compiler_params=pltpu.CompilerParams(
            dimension_semantics=("arbitrary",), vmem_limit_bytes=VMEM_LIMIT),
        name=name,
    )(sink, proj, w_conv, g_q, w_uq, g_kv, w_ukv, *rope_g, *rope_m, _band_mask(), nab, *caches)


def _post_kernel(x_ref, y_ref, mod_ref, ga_ref, gm_ref, gf_ref, wg_ref, wb_ref, wo_ref, w1_ref, w2_ref,
                 o_ref, *, final):
    d = D_MODEL
    mod = lambda i: mod_ref[:, i * d:(i + 1) * d]
    x = x_ref[...]
    h = (_rms(x, ga_ref[...]) * (1.0 + mod(1)) + mod(0)).astype(BF16)
    gates = jax.nn.sigmoid(_dot(h, wg_ref[...]))
    bounds = (Y_CONV, Y_GQA, Y_NA, Y_MLA, Y_COLS)
    merged = None
    for i in range(N_BRANCH):
        lo, hi = bounds[i], bounds[i + 1]
        term = gates[:, i * d:(i + 1) * d] * _dot(y_ref[:, lo:hi], wb_ref[lo:hi, :])
        merged = term if merged is None else merged + term
    x = x + mod(2) * _dot(merged.astype(BF16), wo_ref[...])
    h = (_rms(x, gm_ref[...]) * (1.0 + mod(4)) + mod(3)).astype(BF16)
    f = jnp.square(jnp.maximum(_dot(h, w1_ref[...]), 0.0)).astype(BF16)
    x = x + mod(5) * _dot(f, w2_ref[...])
    o_ref[...] = _rms(x, gf_ref[...]) if final else x


def _post_call(x, y, mods_l, g_attn, g_mlp, g_final, w_gates, w_br, w_o, w_ff1, w_ff2, ctx, final, name):
    n, t, _ = x.shape
    tm = 256
    one = pl.Buffered(1)
    mod_map = (lambda s, i: (CTX_MOD_ROW, 0, 0)) if ctx else (lambda s, i: (s, 0, 0))
    const = lambda shape: pl.BlockSpec(shape, lambda s, i: (0,) * len(shape), pipeline_mode=one)
    return pl.pallas_call(
        functools.partial(_post_kernel, final=final),
        grid=(n, t // tm),
        in_specs=[
            pl.BlockSpec((None, tm, D_MODEL), lambda s, i: (s, i, 0)),
            pl.BlockSpec((None, tm, Y_COLS), lambda s, i: (s, i, 0)),
            pl.BlockSpec((None, 1, 6 * D_MODEL), mod_map),
            const((1, D_MODEL)), const((1, D_MODEL)), const((1, D_MODEL)),
            const((D_MODEL, N_BRANCH * D_MODEL)),
            const((Y_COLS, D_MODEL)),
            const((D_MODEL, D_MODEL)),
            const((D_MODEL, D_FF)),
            const((D_FF, D_MODEL)),
        ],
        out_specs=pl.BlockSpec((None, tm, D_MODEL), lambda s, i: (s, i, 0)),
        out_shape=jax.ShapeDtypeStruct((n, t, D_MODEL), F32),
        compiler_params=pltpu.CompilerParams(
            dimension_semantics=("arbitrary", "arbitrary"), vmem_limit_bytes=VMEM_LIMIT),
        name=name,
    )(x, y, mods_l, g_attn, g_mlp, g_final, w_gates, w_br, w_o, w_ff1, w_ff2)


def kernel(x_prompt, x_sample, cache_gqa_k, cache_gqa_v, cache_na_k, cache_na_v, cache_mla_ckv, cache_mla_krope, c, c_ctx, w_mod, b_mod, g_attn, g_mlp, w_in, w_conv, gqa_sink, na_rpb, mla_g_q, mla_w_uq, mla_g_kv, mla_w_ukv, w_branch_conv, w_branch_gqa, w_branch_na, w_branch_mla, w_o, w_ff1, w_ff2, g_final):
    w_main = jnp.concatenate(
        [w_in[:, :, :ORIG_MKR]] + [w_in[:, :, ORIG_MKR:ORIG_GATES]] * MLA_HEADS, axis=-1).astype(BF16)
    w_gates = w_in[:, :, ORIG_GATES:].astype(BF16)
    uq = mla_w_uq.reshape(DEPTH, MLA_Q_RANK, MLA_HEADS, MLA_NOPE + MLA_ROPE)
    w_uq = jnp.concatenate([uq[..., :MLA_NOPE].reshape(DEPTH, MLA_Q_RANK, -1),
                            uq[..., MLA_NOPE:].reshape(DEPTH, MLA_Q_RANK, -1)], axis=-1).astype(BF16)
    ukv = mla_w_ukv.reshape(DEPTH, MLA_KV_RANK, MLA_HEADS, MLA_NOPE + MLA_V)
    w_ukv = jnp.concatenate([ukv[..., :MLA_NOPE].reshape(DEPTH, MLA_KV_RANK, -1),
                             ukv[..., MLA_NOPE:].reshape(DEPTH, MLA_KV_RANK, -1)], axis=-1).astype(BF16)
    w_br = jnp.concatenate([w_branch_conv, w_branch_gqa, w_branch_na, w_branch_mla], axis=1).astype(BF16)
    w_o_b, w_ff1_b, w_ff2_b = w_o.astype(BF16), w_ff1.astype(BF16), w_ff2.astype(BF16)

    c16 = jnp.concatenate([c, c_ctx[None, :], jnp.zeros((MOD_ROWS - DEC_BATCH - 1, D_MODEL), F32)], axis=0)
    mods = _mod_call(c16, w_mod, b_mod).reshape(DEPTH, MOD_ROWS, 1, 6 * D_MODEL)
    nab = _nab_call(na_rpb)

    merge_heads = lambda a: a.reshape(a.shape[0], DEPTH, PAST_LEN, -1)
    caches = (merge_heads(cache_gqa_k), merge_heads(cache_gqa_v), merge_heads(cache_na_k), merge_heads(cache_na_v),
              cache_mla_ckv, jnp.tile(cache_mla_krope, (1, 1, 1, MLA_HEADS)))

    h_ctx, h_lat = x_prompt, x_sample
    states = []
    for l in range(DEPTH):
        final = l == DEPTH - 1
        g_a, g_m, g_f = g_attn[l][None, :], g_mlp[l][None, :], g_final[None, :]
        mixer_w = (gqa_sink[l], w_conv[l], mla_g_q[l][None, :], w_uq[l], mla_g_kv[l][None, :], w_ukv[l])
        post_w = (g_a, g_m, g_f, w_gates[l], w_br[l], w_o_b[l], w_ff1_b[l], w_ff2_b[l])

        proj = _pre_call(h_ctx, mods[l], g_a, w_main[l], True, f"pre_ctx_{l}")
        sink, *rest = mixer_w
        y, *st = _ctx_mixer_call(proj, sink, *rest, name=f"mixer_ctx_{l}")
        states.append(st)
        h_ctx = _post_call(h_ctx, y, mods[l], *post_w, ctx=True, final=final, name=f"post_ctx_{l}")

        proj = _pre_call(h_lat, mods[l], g_a, w_main[l], False, f"pre_lat_{l}")
        y = _lat_mixer_call(proj, sink, *rest, nab[l], caches, l, f"mixer_lat_{l}")
        h_lat = _post_call(h_lat, y, mods[l], *post_w, ctx=False, final=final, name=f"post_lat_{l}")

    def stacked(i, heads):
        a = jnp.stack([states[l][i] for l in range(DEPTH)], axis=1)
        return a.reshape(BATCH, DEPTH, SEQ, heads, HEAD_DIM) if heads else a

    return (h_ctx, h_lat, stacked(0, GQA_KV_HEADS), stacked(1, GQA_KV_HEADS),
            stacked(2, NA_HEADS), stacked(3, NA_HEADS), stacked(4, 0), stacked(5, 0))
```

```python
import functools

import numpy as np
import jax
import jax.numpy as jnp
from jax import lax
from jax.experimental import pallas as pl
from jax.experimental.pallas import tpu as pltpu

D_MODEL = 1024
BATCH = 32
SEQ = 256
DEPTH = 2
DEC_BATCH = 8
DEC_SEQ = 1024
PAST_LEN = 512
GRID_W = 64
GRID_ROWS = DEC_SEQ // GRID_W
HEAD_DIM = 64
CONV_WIDTH = 256
CONV_K = 3
GQA_HEADS = 8
GQA_KV_HEADS = 2
WINDOW = 128
BAND_BLOCK = 128
NA_HEADS = 4
NA_WIN_H = 8
NA_WIN_W = 16
MLA_HEADS = 4
MLA_Q_RANK = 256
MLA_KV_RANK = 128
MLA_NOPE = 64
MLA_ROPE = 32
MLA_V = 64
D_FF = 4 * D_MODEL
N_BRANCH = 4
ROPE_BASE = 10000.0
EPS = 1e-6
NEG_INF = -1e30
ATTN_SCALE = HEAD_DIM ** -0.5
MLA_SCALE = (MLA_NOPE + MLA_ROPE) ** -0.5

LANES = 128
MOD_ROWS = 16
CTX_MOD_ROW = DEC_BATCH

C_CB, C_CC, C_CV = 0, 256, 512
C_GQ, C_GK, C_GV = 768, 1280, 1408
C_NQ, C_NK, C_NV = 1536, 1792, 2048
C_MQ, C_MKV, C_MKR = 2304, 2560, 2688
MAIN_COLS = 2816
GATE_COL0 = 2720
Y_CONV, Y_GQA, Y_NA, Y_MLA = 0, 256, 768, 1024
Y_COLS = 1280
NA_PATTERNS = 8
NA_KEYS = NA_WIN_H * GRID_W

VMEM_LIMIT = 56 * 1024 * 1024

F32 = jnp.float32
BF16 = jnp.bfloat16


def _dot(a, b):
    return jnp.dot(a, b, preferred_element_type=F32)


def _dot_nt(a, b):
    return lax.dot_general(a, b, (((1,), (1,)), ((), ())), preferred_element_type=F32)


def _rms(x, g):
    return x * lax.rsqrt(jnp.mean(x * x, axis=-1, keepdims=True) + EPS) * g


def _lane_lt(shape, n):
    return lax.broadcasted_iota(jnp.int32, shape, len(shape) - 1) < n


def _lane_group(shape, lo, hi):
    lane = lax.broadcasted_iota(jnp.int32, shape, len(shape) - 1)
    return (lane >= lo) & (lane < hi)


def _softmax_parts(parts, sink=None):
    m = parts[0].max(axis=-1, keepdims=True)
    for p in parts[1:]:
        m = jnp.maximum(m, p.max(axis=-1, keepdims=True))
    if sink is not None:
        m = jnp.maximum(m, sink)
    es = [jnp.exp(p - m) for p in parts]
    l = es[0].sum(axis=-1, keepdims=True)
    for e in es[1:]:
        l = l + e.sum(axis=-1, keepdims=True)
    if sink is not None:
        l = l + jnp.exp(sink - m)
    return [e.astype(BF16) for e in es], 1.0 / l


def _short_conv(cb, cc, cv, w):
    u = cc * cv
    t = u.shape[0]
    row = lax.broadcasted_iota(jnp.int32, u.shape, 0)
    prev = jnp.where(row == 0, 0.0, pltpu.roll(u, 1, 0))
    nxt = jnp.where(row == t - 1, 0.0, pltpu.roll(u, t - 1, 0))
    return cb * (prev * w[0:1, :] + u * w[1:2, :] + nxt * w[2:3, :])


def _rope(x, cos, sin_hi, sin_lo, half):
    n = x.shape[-1]
    return x * cos + pltpu.roll(x, n - half, 1) * sin_lo + pltpu.roll(x, half, 1) * sin_hi


def _repeat_rope_key(tile):
    k = jnp.where(_lane_lt(tile.shape, MLA_ROPE), tile, 0.0)
    k = k + pltpu.roll(k, MLA_ROPE, 1)
    return k + pltpu.roll(k, 2 * MLA_ROPE, 1)


def _mod_kernel(c_ref, w_ref, b_ref, o_ref):
    c = c_ref[...]
    s = c * jax.nn.sigmoid(c)
    o_ref[...] = _dot(s.astype(BF16), w_ref[...].astype(BF16)) + b_ref[...]


def _mod_call(c16, w_mod, b_mod):
    tn = 1536
    return pl.pallas_call(
        _mod_kernel,
        grid=(DEPTH, 6 * D_MODEL // tn),
        in_specs=[
            pl.BlockSpec((MOD_ROWS, D_MODEL), lambda l, j: (0, 0)),
            pl.BlockSpec((None, D_MODEL, tn), lambda l, j: (l, 0, j)),
            pl.BlockSpec((None, 1, tn), lambda l, j: (l, 0, j)),
        ],
        out_specs=pl.BlockSpec((None, MOD_ROWS, tn), lambda l, j: (l, 0, j)),
        out_shape=jax.ShapeDtypeStruct((DEPTH, MOD_ROWS, 6 * D_MODEL), F32),
        compiler_params=pltpu.CompilerParams(
            dimension_semantics=("arbitrary", "arbitrary"), vmem_limit_bytes=VMEM_LIMIT),
        name="adaln_mod",
    )(c16, w_mod, b_mod.reshape(DEPTH, 1, 6 * D_MODEL))


def _nab_kernel(rpb_ref, o_ref):
    l = pl.program_id(0)
    c = lax.broadcasted_iota(jnp.int32, (GRID_W, GRID_W), 0)
    w = lax.broadcasted_iota(jnp.int32, (GRID_W, GRID_W), 1)
    dc = w - c + (NA_WIN_W - 1)
    c0 = jnp.clip(c - NA_WIN_W // 2, 0, GRID_W - NA_WIN_W)
    outside = (w < c0) | (w >= c0 + NA_WIN_W)
    n_dr, n_dc = 2 * NA_WIN_H - 1, 2 * NA_WIN_W - 1
    for h in range(NA_HEADS):
        for d in range(n_dr):
            base = ((l * NA_HEADS + h) * n_dr + d) * n_dc
            tile = jnp.full((GRID_W, GRID_W), NEG_INF, F32)
            for j in range(n_dc):
                tile = jnp.where(dc == j, rpb_ref[base + j], tile)
            tile = jnp.where(outside, NEG_INF, tile)
            for p in range(NA_PATTERNS):
                a = d - (NA_WIN_H - 1) + p
                if 0 <= a < NA_WIN_H:
                    col = h * NA_KEYS + a * GRID_W
                    o_ref[p, :, col:col + GRID_W] = tile


def _nab_call(na_rpb):
    return pl.pallas_call(
        _nab_kernel,
        grid=(DEPTH,),
        in_specs=[pl.BlockSpec(memory_space=pltpu.SMEM)],
        out_specs=pl.BlockSpec((None, NA_PATTERNS, GRID_W, NA_HEADS * NA_KEYS), lambda l: (l, 0, 0, 0)),
        out_shape=jax.ShapeDtypeStruct((DEPTH, NA_PATTERNS, GRID_W, NA_HEADS * NA_KEYS), F32),
        compiler_params=pltpu.CompilerParams(dimension_semantics=("arbitrary",)),
        name="na_bias_tables",
    )(na_rpb.reshape(-1))


def _pre_kernel(x_ref, mod_ref, g_ref, w_ref, o_ref):
    h = _rms(x_ref[...], g_ref[...]) * (1.0 + mod_ref[:, D_MODEL:2 * D_MODEL]) + mod_ref[:, 0:D_MODEL]
    o_ref[...] = _dot(h.astype(BF16), w_ref[...])


def _mod_spec(layer, ctx, n_grid):
    if n_grid == 1:
        index = (lambda s: (layer, CTX_MOD_ROW, 0, 0)) if ctx else (lambda s: (layer, s, 0, 0))
    else:
        index = (lambda s, i: (layer, CTX_MOD_ROW, 0, 0)) if ctx else (lambda s, i: (layer, s, 0, 0))
    return pl.BlockSpec((None, None, 1, 6 * D_MODEL), index)


def _layer_spec(shape, layer, n_grid, single=True):
    zeros = (0,) * len(shape)
    index = (lambda s: (layer,) + zeros) if n_grid == 1 else (lambda s, i: (layer,) + zeros)
    return pl.BlockSpec((None,) + tuple(shape), index, pipeline_mode=pl.Buffered(1) if single else None)


def _pre_call(x, mods, g, w_main, layer, ctx, name):
    n, t, _ = x.shape
    tm = min(t, 512)
    return pl.pallas_call(
        _pre_kernel,
        grid=(n, t // tm),
        in_specs=[
            pl.BlockSpec((None, tm, D_MODEL), lambda s, i: (s, i, 0)),
            _mod_spec(layer, ctx, 2),
            _layer_spec((1, D_MODEL), layer, 2),
            _layer_spec((D_MODEL, MAIN_COLS), layer, 2),
        ],
        out_specs=pl.BlockSpec((None, tm, MAIN_COLS), lambda s, i: (s, i, 0)),
        out_shape=jax.ShapeDtypeStruct((n, t, MAIN_COLS), F32),
        compiler_params=pltpu.CompilerParams(
            dimension_semantics=("arbitrary", "arbitrary"), vmem_limit_bytes=VMEM_LIMIT),
        name=name,
    )(x, mods, g, w_main)


def _pair_stack(pair, first_lo, second_lo):
    lo = _lane_lt(pair.shape, HEAD_DIM)
    swapped = None
    if not first_lo or second_lo:
        swapped = pltpu.roll(pair, HEAD_DIM, 1)
    top = jnp.where(lo, pair if first_lo else swapped, 0.0)
    bot = jnp.where(lo, 0.0, swapped if second_lo else pair)
    return top, bot


def _pair_stack_t(even_t, odd_t):
    z = jnp.zeros_like(even_t)
    return jnp.concatenate([jnp.concatenate([even_t, z], axis=1), jnp.concatenate([z, odd_t], axis=1)], axis=0)


def _mla_qkv(mq, ckv, g_q, w_uq, w_ukv):
    q = _dot(_rms(mq, g_q).astype(BF16), w_uq) * MLA_SCALE
    kv = _dot(ckv.astype(BF16), w_ukv)
    return q, kv


def _mla_key_rows(kn_pair, kr4, j, odd):
    lo = _lane_lt(kn_pair.shape, HEAD_DIM)
    h = 2 * j + odd
    nope = jnp.where(lo, 0.0, kn_pair) if odd else jnp.where(lo, kn_pair, 0.0)
    rope = jnp.where(_lane_group(kr4.shape, h * MLA_ROPE, (h + 1) * MLA_ROPE), kr4, 0.0)
    return jnp.concatenate([nope, rope], axis=1)


def _mla_val_rows(v_pair, odd):
    lo = _lane_lt(v_pair.shape, HEAD_DIM)
    return jnp.where(lo, 0.0, v_pair) if odd else jnp.where(lo, v_pair, 0.0)


def _finish_pair(terms, rle, rlo):
    o = None
    for p, v, feature_major in terms:
        t = _dot_nt(p, v) if feature_major else _dot(p, v)
        o = t if o is None else o + t
    return o * jnp.where(_lane_lt(o.shape, HEAD_DIM), rle, rlo)


def _ctx_mixer_kernel(sink_ref, p_ref, wc_ref, gq_ref, wuq_ref, gkv_ref, wukv_ref, *rest, layer):
    y_ref, kg_ref, vg_ref, kn_ref, vn_ref, ckv_ref, kr_ref = rest[-7:]
    t = SEQ
    col = lambda c, n: p_ref[:, c:c + n]

    kg_ref[...] = col(C_GK, LANES).T
    vg_ref[...] = col(C_GV, LANES).T
    kn_ref[...] = col(C_NK, 2 * LANES).T
    vn_ref[...] = col(C_NV, 2 * LANES).T
    ckv = _rms(col(C_MKV, MLA_KV_RANK), gkv_ref[...])
    ckv_ref[...] = ckv
    kr_tile = col(C_MKR, LANES)
    kr_ref[...] = kr_tile.T[0:MLA_ROPE, :]

    y_ref[:, Y_CONV:Y_CONV + CONV_WIDTH] = _short_conv(
        col(C_CB, CONV_WIDTH), col(C_CC, CONV_WIDTH), col(C_CV, CONV_WIDTH), wc_ref[...]).astype(BF16)

    kpair, vpair = col(C_GK, LANES), col(C_GV, LANES)
    row2 = lax.broadcasted_iota(jnp.int32, (2 * t, 1), 0) < t
    for g in range(GQA_KV_HEADS):
        ke, ko = _pair_stack(kpair, g == 0, g == 0)
        ve, vo = _pair_stack(vpair, g == 0, g == 0)
        q = jnp.concatenate([col(C_GQ + (2 * g) * LANES, LANES), col(C_GQ + (2 * g + 1) * LANES, LANES)], axis=0)
        q = (q * ATTN_SCALE).astype(BF16)
        se, so = _dot_nt(q, ke.astype(BF16)), _dot_nt(q, ko.astype(BF16))
        sink_e = jnp.where(row2, sink_ref[layer, 4 * g], sink_ref[layer, 4 * g + 2])
        sink_o = jnp.where(row2, sink_ref[layer, 4 * g + 1], sink_ref[layer, 4 * g + 3])
        (pe,), rle = _softmax_parts([se], sink_e)
        (po,), rlo = _softmax_parts([so], sink_o)
        o = _finish_pair([(pe, ve.astype(BF16), False), (po, vo.astype(BF16), False)], rle, rlo).astype(BF16)
        c0 = Y_GQA + (2 * g) * LANES
        y_ref[:, c0:c0 + LANES] = o[0:t]
        y_ref[:, c0 + LANES:c0 + 2 * LANES] = o[t:2 * t]

    for j in range(NA_HEADS // 2):
        kpair, vpair = col(C_NK + j * LANES, LANES), col(C_NV + j * LANES, LANES)
        ke, ko = _pair_stack(kpair, True, False)
        ve, vo = _pair_stack(vpair, True, False)
        q = (col(C_NQ + j * LANES, LANES) * ATTN_SCALE).astype(BF16)
        (pe,), rle = _softmax_parts([_dot_nt(q, ke.astype(BF16))])
        (po,), rlo = _softmax_parts([_dot_nt(q, ko.astype(BF16))])
        o = _finish_pair([(pe, ve.astype(BF16), False), (po, vo.astype(BF16), False)], rle, rlo)
        y_ref[:, Y_NA + j * LANES:Y_NA + (j + 1) * LANES] = o.astype(BF16)

    q, kv = _mla_qkv(col(C_MQ, MLA_Q_RANK), ckv, gq_ref[...], wuq_ref[...], wukv_ref[...])
    kr4 = _repeat_rope_key(kr_tile)
    q_rope = q[:, 2 * LANES:3 * LANES]
    for j in range(MLA_HEADS // 2):
        qj = jnp.concatenate([q[:, j * LANES:(j + 1) * LANES], q_rope], axis=1).astype(BF16)
        kn_pair = kv[:, j * LANES:(j + 1) * LANES]
        v_pair = kv[:, 2 * LANES + j * LANES:2 * LANES + (j + 1) * LANES]
        (pe,), rle = _softmax_parts([_dot_nt(qj, _mla_key_rows(kn_pair, kr4, j, 0).astype(BF16))])
        (po,), rlo = _softmax_parts([_dot_nt(qj, _mla_key_rows(kn_pair, kr4, j, 1).astype(BF16))])
        o = _finish_pair([(pe, _mla_val_rows(v_pair, 0).astype(BF16), False),
                          (po, _mla_val_rows(v_pair, 1).astype(BF16), False)], rle, rlo)
        y_ref[:, Y_MLA + j * LANES:Y_MLA + (j + 1) * LANES] = o.astype(BF16)


def _ctx_mixer_call(proj, sink, w_conv, g_q, w_uq, g_kv, w_ukv, prev_states, layer, name):
    n = BATCH
    state_shapes = ((2 * HEAD_DIM, SEQ), (2 * HEAD_DIM, SEQ), (4 * HEAD_DIM, SEQ), (4 * HEAD_DIM, SEQ),
                    (SEQ, MLA_KV_RANK), (MLA_ROPE, SEQ))
    state_spec = lambda shape: pl.BlockSpec((None, None) + shape, lambda s: (s, layer, 0, 0))
    in_specs = [
        pl.BlockSpec(memory_space=pltpu.SMEM),
        pl.BlockSpec((None, SEQ, MAIN_COLS), lambda s: (s, 0, 0)),
        _layer_spec((CONV_K, CONV_WIDTH), layer, 1),
        _layer_spec((1, MLA_Q_RANK), layer, 1),
        _layer_spec((MLA_Q_RANK, 3 * LANES), layer, 1),
        _layer_spec((1, MLA_KV_RANK), layer, 1),
        _layer_spec((MLA_KV_RANK, 4 * LANES), layer, 1),
    ]
    args = [sink, proj, w_conv, g_q, w_uq, g_kv, w_ukv]
    aliases = {}
    if prev_states is not None:
        for i, st in enumerate(prev_states):
            aliases[len(args)] = 1 + i
            in_specs.append(pl.BlockSpec(memory_space=pl.ANY))
            args.append(st)
    outs = pl.pallas_call(
        functools.partial(_ctx_mixer_kernel, layer=layer),
        grid=(n,),
        in_specs=in_specs,
        out_specs=[pl.BlockSpec((None, SEQ, Y_COLS), lambda s: (s, 0, 0))] + [state_spec(s) for s in state_shapes],
        out_shape=[jax.ShapeDtypeStruct((n, SEQ, Y_COLS), BF16)]
        + [jax.ShapeDtypeStruct((n, DEPTH) + s, F32) for s in state_shapes],
        input_output_aliases=aliases,
        compiler_params=pltpu.CompilerParams(
            dimension_semantics=("arbitrary",), vmem_limit_bytes=VMEM_LIMIT),
        name=name,
    )(*args)
    return outs[0], outs[1:]


def _lat_mixer_kernel(sink_ref, p_ref, wc_ref, gq_ref, wuq_ref, gkv_ref, wukv_ref,
                      rc_ref, rsh_ref, rsl_ref, mc_ref, msh_ref, msl_ref, band_ref, nab_ref,
                      cgk_ref, cgv_ref, cnk_ref, cnv_ref, cckv_ref, ckr_ref,
                      y_ref,
                      gq_s, gk_s, gv_s, gkc_s, gvc_s, nq_s, nk_s, nv_s, nkc_s, nvc_s, mq_s, mk_s, mv_s, *, layer):
    t = DEC_SEQ
    col = lambda c, n: p_ref[:, c:c + n]
    rope64 = lambda x: _rope(x, rc_ref[...], rsh_ref[...], rsl_ref[...], HEAD_DIM // 4)
    rope32 = lambda x: _rope(x, mc_ref[...], msh_ref[...], msl_ref[...], MLA_ROPE // 4)

    y_ref[:, Y_CONV:Y_CONV + CONV_WIDTH] = _short_conv(
        col(C_CB, CONV_WIDTH), col(C_CC, CONV_WIDTH), col(C_CV, CONV_WIDTH), wc_ref[...]).astype(BF16)

    for j in range(GQA_HEADS // 2):
        gq_s[:, j * LANES:(j + 1) * LANES] = (rope64(col(C_GQ + j * LANES, LANES)) * ATTN_SCALE).astype(BF16)
    kpair, vpair = rope64(col(C_GK, LANES)), col(C_GV, LANES)
    for g in range(GQA_KV_HEADS):
        for dst, src in ((gk_s, kpair), (gv_s, vpair)):
            top, bot = _pair_stack(src, g == 0, g == 0)
            dst[2 * g] = top.astype(BF16)
            dst[2 * g + 1] = bot.astype(BF16)
        for dst, src in ((gkc_s, cgk_ref), (gvc_s, cgv_ref)):
            head_t = src[g * HEAD_DIM:(g + 1) * HEAD_DIM, :]
            dst[g] = _pair_stack_t(head_t, head_t).astype(BF16)

    def gqa_block(b, c_lo, c_hi):
        q0 = pl.multiple_of(b * BAND_BLOCK, BAND_BLOCK)
        k0 = pl.multiple_of(q0 + (c_lo - WINDOW), BAND_BLOCK)
        n = c_hi - c_lo
        band = band_ref[:, c_lo:c_hi]
        row2 = lax.broadcasted_iota(jnp.int32, (2 * BAND_BLOCK, 1), 0) < BAND_BLOCK
        for g in range(GQA_KV_HEADS):
            q = jnp.concatenate([gq_s[pl.ds(q0, BAND_BLOCK), (2 * g) * LANES:(2 * g + 1) * LANES],
                                 gq_s[pl.ds(q0, BAND_BLOCK), (2 * g + 1) * LANES:(2 * g + 2) * LANES]], axis=0)
            s_ctx = _dot(q, gkc_s[g])
            parts, recips = [], []
            for odd in range(2):
                s_loc = _dot_nt(q, gk_s[2 * g + odd, pl.ds(k0, n), :]) + band
                sink = jnp.where(row2, sink_ref[layer, 4 * g + odd], sink_ref[layer, 4 * g + 2 + odd])
                p, r = _softmax_parts([s_loc, s_ctx[:, odd * PAST_LEN:(odd + 1) * PAST_LEN]], sink)
                parts.append(p)
                recips.append(r)
            o = _finish_pair([(parts[0][0], gv_s[2 * g, pl.ds(k0, n), :], False),
                              (parts[1][0], gv_s[2 * g + 1, pl.ds(k0, n), :], False),
                              (jnp.concatenate([parts[0][1], parts[1][1]], axis=1), gvc_s[g], True)],
                             recips[0], recips[1]).astype(BF16)
            c0 = Y_GQA + (2 * g) * LANES
            y_ref[pl.ds(q0, BAND_BLOCK), c0:c0 + LANES] = o[0:BAND_BLOCK]
            y_ref[pl.ds(q0, BAND_BLOCK), c0 + LANES:c0 + 2 * LANES] = o[BAND_BLOCK:2 * BAND_BLOCK]

    span = BAND_BLOCK + 2 * WINDOW
    nb = t // BAND_BLOCK
    gqa_block(0, WINDOW, span)
    lax.fori_loop(1, nb - 1, lambda b, c: (gqa_block(b, 0, span), c)[1], 0)
    gqa_block(nb - 1, 0, span - WINDOW)

    for j in range(NA_HEADS // 2):
        nq_s[:, j * LANES:(j + 1) * LANES] = (col(C_NQ + j * LANES, LANES) * ATTN_SCALE).astype(BF16)
        for dst, src in ((nk_s, col(C_NK + j * LANES, LANES)), (nv_s, col(C_NV + j * LANES, LANES))):
            top, bot = _pair_stack(src, True, False)
            dst[2 * j] = top.astype(BF16)
            dst[2 * j + 1] = bot.astype(BF16)
        for dst, src in ((nkc_s, cnk_ref), (nvc_s, cnv_ref)):
            r0 = 2 * j * HEAD_DIM
            dst[j] = _pair_stack_t(src[r0:r0 + HEAD_DIM, :], src[r0 + HEAD_DIM:r0 + 2 * HEAD_DIM, :]).astype(BF16)

    def na_row(r, carry):
        q0 = pl.multiple_of(r * GRID_W, GRID_W)
        r0 = jnp.clip(r - NA_WIN_H // 2, 0, GRID_ROWS - NA_WIN_H)
        k0 = pl.multiple_of(r0 * GRID_W, GRID_W)
        pat = r - r0
        for j in range(NA_HEADS // 2):
            q = nq_s[pl.ds(q0, GRID_W), j * LANES:(j + 1) * LANES]
            s_ctx = _dot(q, nkc_s[j])
            parts, recips = [], []
            for odd in range(2):
                h = 2 * j + odd
                s_loc = _dot_nt(q, nk_s[h, pl.ds(k0, NA_KEYS), :]) + nab_ref[pat, :, h * NA_KEYS:(h + 1) * NA_KEYS]
                p, rcp = _softmax_parts([s_loc, s_ctx[:, odd * PAST_LEN:(odd + 1) * PAST_LEN]])
                parts.append(p)
                recips.append(rcp)
            o = _finish_pair([(parts[0][0], nv_s[2 * j, pl.ds(k0, NA_KEYS), :], False),
                              (parts[1][0], nv_s[2 * j + 1, pl.ds(k0, NA_KEYS), :], False),
                              (jnp.concatenate([parts[0][1], parts[1][1]], axis=1), nvc_s[j], True)],
                             recips[0], recips[1])
            y_ref[pl.ds(q0, GRID_W), Y_NA + j * LANES:Y_NA + (j + 1) * LANES] = o.astype(BF16)
        return carry

    lax.fori_loop(0, GRID_ROWS, na_row, 0)

    ckv = _rms(col(C_MKV, MLA_KV_RANK), gkv_ref[...])
    q, kv = _mla_qkv(col(C_MQ, MLA_Q_RANK), ckv, gq_ref[...], wuq_ref[...], wukv_ref[...])
    mq_s[:, 0:2 * LANES] = q[:, 0:2 * LANES].astype(BF16)
    mq_s[:, 2 * LANES:3 * LANES] = rope32(q[:, 2 * LANES:3 * LANES]).astype(BF16)
    kr4 = rope32(_repeat_rope_key(col(C_MKR, LANES)))
    kv_c = _dot(cckv_ref[...].astype(BF16), wukv_ref[...])
    kr4_c = jnp.concatenate([ckr_ref[...]] * MLA_HEADS, axis=0).T
    for j in range(MLA_HEADS // 2):
        for odd in range(2):
            h = 2 * j + odd
            for rows, kv_x, kr_x in ((slice(0, t), kv, kr4), (slice(t, t + PAST_LEN), kv_c, kr4_c)):
                kn_pair = kv_x[:, j * LANES:(j + 1) * LANES]
                v_pair = kv_x[:, 2 * LANES + j * LANES:2 * LANES + (j + 1) * LANES]
                mk_s[h, rows, :] = _mla_key_rows(kn_pair, kr_x, j, odd).astype(BF16)
                mv_s[h, rows, :] = _mla_val_rows(v_pair, odd).astype(BF16)

    tq = 256

    def mla_block(i, carry):
        q0 = pl.multiple_of(i * tq, tq)
        for j in range(MLA_HEADS // 2):
            qj = jnp.concatenate([mq_s[pl.ds(q0, tq), j * LANES:(j + 1) * LANES],
                                  mq_s[pl.ds(q0, tq), 2 * LANES:3 * LANES]], axis=1)
            (pe,), rle = _softmax_parts([_dot_nt(qj, mk_s[2 * j])])
            (po,), rlo = _softmax_parts([_dot_nt(qj, mk_s[2 * j + 1])])
            o = _finish_pair([(pe, mv_s[2 * j], False), (po, mv_s[2 * j + 1], False)], rle, rlo)
            y_ref[pl.ds(q0, tq), Y_MLA + j * LANES:Y_MLA + (j + 1) * LANES] = o.astype(BF16)
        return carry

    lax.fori_loop(0, t // tq, mla_block, 0)


def _rope_tables(group, half):
    tok = np.arange(DEC_SEQ)
    pos = np.stack([tok // GRID_W, tok % GRID_W], axis=1).astype(np.float64)
    inv = ROPE_BASE ** (-np.arange(half, dtype=np.float64) / half)
    lane = np.arange(LANES) % group
    axis = lane // (2 * half)
    within = lane % (2 * half)
    ang = pos[:, axis] * inv[within % half][None, :]
    cos, sin = np.cos(ang), np.sin(ang)
    upper = (within >= half)[None, :]
    sin_hi = np.where(upper, sin, 0.0)
    sin_lo = np.where(upper, 0.0, -sin)
    return tuple(jnp.asarray(a, dtype=F32) for a in (cos, sin_hi, sin_lo))


def _band_mask():
    i = np.arange(BAND_BLOCK)[:, None]
    c = np.arange(BAND_BLOCK + 2 * WINDOW)[None, :]
    ok = (c >= i) & (c <= i + 2 * WINDOW)
    m = np.where(ok, 0.0, NEG_INF)
    return jnp.asarray(np.concatenate([m, m], axis=0), dtype=F32)


def _lat_mixer_call(proj, sink, w_conv, g_q, w_uq, g_kv, w_ukv, nab, caches, layer, name):
    n, t = DEC_BATCH, DEC_SEQ
    one = pl.Buffered(1)
    const = lambda shape: pl.BlockSpec(shape, lambda s: (0,) * len(shape), pipeline_mode=one)
    cache = lambda shape: pl.BlockSpec((None, None) + shape, lambda s: (s, layer, 0, 0))
    rope_g = _rope_tables(HEAD_DIM, HEAD_DIM // 4)
    rope_m = _rope_tables(MLA_ROPE, MLA_ROPE // 4)
    span = BAND_BLOCK + 2 * WINDOW
    scratch = [
        pltpu.VMEM((t, GQA_HEADS // 2 * LANES), BF16),
        pltpu.VMEM((4, t, LANES), BF16),
        pltpu.VMEM((4, t, LANES), BF16),
        pltpu.VMEM((GQA_KV_HEADS, LANES, 2 * PAST_LEN), BF16),
        pltpu.VMEM((GQA_KV_HEADS, LANES, 2 * PAST_LEN), BF16),
        pltpu.VMEM((t, NA_HEADS // 2 * LANES), BF16),
        pltpu.VMEM((4, t, LANES), BF16),
        pltpu.VMEM((4, t, LANES), BF16),
        pltpu.VMEM((NA_HEADS // 2, LANES, 2 * PAST_LEN), BF16),
        pltpu.VMEM((NA_HEADS // 2, LANES, 2 * PAST_LEN), BF16),
        pltpu.VMEM((t, 3 * LANES), BF16),
        pltpu.VMEM((4, t + PAST_LEN, 2 * LANES), BF16),
        pltpu.VMEM((4, t + PAST_LEN, LANES), BF16),
    ]
    return pl.pallas_call(
        functools.partial(_lat_mixer_kernel, layer=layer),
        grid=(n,),
        in_specs=[
            pl.BlockSpec(memory_space=pltpu.SMEM),
            pl.BlockSpec((None, t, MAIN_COLS), lambda s: (s, 0, 0), pipeline_mode=one),
            _layer_spec((CONV_K, CONV_WIDTH), layer, 1),
            _layer_spec((1, MLA_Q_RANK), layer, 1),
            _layer_spec((MLA_Q_RANK, 3 * LANES), layer, 1),
            _layer_spec((1, MLA_KV_RANK), layer, 1),
            _layer_spec((MLA_KV_RANK, 4 * LANES), layer, 1),
        ] + [const((t, LANES))] * 6 + [
            const((2 * BAND_BLOCK, span)),
            _layer_spec((NA_PATTERNS, GRID_W, NA_HEADS * NA_KEYS), layer, 1),
            cache((2 * HEAD_DIM, PAST_LEN)), cache((2 * HEAD_DIM, PAST_LEN)),
            cache((4 * HEAD_DIM, PAST_LEN)), cache((4 * HEAD_DIM, PAST_LEN)),
            cache((PAST_LEN, MLA_KV_RANK)), cache((MLA_ROPE, PAST_LEN)),
        ],
        out_specs=pl.BlockSpec((None, t, Y_COLS), lambda s: (s, 0, 0)),
        out_shape=jax.ShapeDtypeStruct((n, t, Y_COLS), BF16),
        scratch_shapes=scratch,
        compiler_params=pltpu.CompilerParams(
            dimension_semantics=("arbitrary",), vmem_limit_bytes=VMEM_LIMIT),
        name=name,
    )(sink, proj, w_conv, g_q, w_uq, g_kv, w_ukv, *rope_g, *rope_m, _band_mask(), nab, *caches)


def _post_kernel(x_ref, y_ref, mod_ref, ga_ref, gm_ref, gf_ref, wg_ref, wb_ref, wo_ref, w1_ref, w2_ref,
                 o_ref, *, final):
    d = D_MODEL
    mod = lambda i: mod_ref[:, i * d:(i + 1) * d]
    x = x_ref[...]
    h = (_rms(x, ga_ref[...]) * (1.0 + mod(1)) + mod(0)).astype(BF16)
    gates = jax.nn.sigmoid(_dot(h, wg_ref[...]))
    bounds = (Y_CONV, Y_GQA, Y_NA, Y_MLA, Y_COLS)
    merged = None
    for i in range(N_BRANCH):
        lo, hi = bounds[i], bounds[i + 1]
        term = gates[:, i * d:(i + 1) * d] * _dot(y_ref[:, lo:hi], wb_ref[lo:hi, :])
        merged = term if merged is None else merged + term
    x = x + mod(2) * _dot(merged.astype(BF16), wo_ref[...])
    h = (_rms(x, gm_ref[...]) * (1.0 + mod(4)) + mod(3)).astype(BF16)
    f = jnp.square(jnp.maximum(_dot(h, w1_ref[...]), 0.0)).astype(BF16)
    x = x + mod(5) * _dot(f, w2_ref[...])
    o_ref[...] = _rms(x, gf_ref[...]) if final else x


def _post_call(x, y, mods, g_attn, g_mlp, g_final, w_gates, w_br, w_o, w_ff1, w_ff2, layer, ctx, name):
    n, t, _ = x.shape
    tm = 256
    return pl.pallas_call(
        functools.partial(_post_kernel, final=layer == DEPTH - 1),
        grid=(n, t // tm),
        in_specs=[
            pl.BlockSpec((None, tm, D_MODEL), lambda s, i: (s, i, 0)),
            pl.BlockSpec((None, tm, Y_COLS), lambda s, i: (s, i, 0)),
            _mod_spec(layer, ctx, 2),
            _layer_spec((1, D_MODEL), layer, 2),
            _layer_spec((1, D_MODEL), layer, 2),
            pl.BlockSpec((1, D_MODEL), lambda s, i: (0, 0)),
            _layer_spec((D_MODEL, N_BRANCH * D_MODEL), layer, 2),
            _layer_spec((Y_COLS, D_MODEL), layer, 2),
            _layer_spec((D_MODEL, D_MODEL), layer, 2),
            _layer_spec((D_MODEL, D_FF), layer, 2),
            _layer_spec((D_FF, D_MODEL), layer, 2),
        ],
        out_specs=pl.BlockSpec((None, tm, D_MODEL), lambda s, i: (s, i, 0)),
        out_shape=jax.ShapeDtypeStruct((n, t, D_MODEL), F32),
        compiler_params=pltpu.CompilerParams(
            dimension_semantics=("arbitrary", "arbitrary"), vmem_limit_bytes=VMEM_LIMIT),
        name=name,
    )(x, y, mods, g_attn, g_mlp, g_final, w_gates, w_br, w_o, w_ff1, w_ff2)


def kernel(x_prompt, x_sample, cache_gqa_k, cache_gqa_v, cache_na_k, cache_na_v, cache_mla_ckv, cache_mla_krope, c, c_ctx, w_mod, b_mod, g_attn, g_mlp, w_in, w_conv, gqa_sink, na_rpb, mla_g_q, mla_w_uq, mla_g_kv, mla_w_ukv, w_branch_conv, w_branch_gqa, w_branch_na, w_branch_mla, w_o, w_ff1, w_ff2, g_final):
    w_main = w_in[:, :, :MAIN_COLS].astype(BF16)
    w_gates = w_in[:, :, GATE_COL0:].astype(BF16)
    uq = mla_w_uq.reshape(DEPTH, MLA_Q_RANK, MLA_HEADS, MLA_NOPE + MLA_ROPE)
    w_uq = jnp.concatenate([uq[..., :MLA_NOPE].reshape(DEPTH, MLA_Q_RANK, -1),
                            uq[..., MLA_NOPE:].reshape(DEPTH, MLA_Q_RANK, -1)], axis=-1).astype(BF16)
    ukv = mla_w_ukv.reshape(DEPTH, MLA_KV_RANK, MLA_HEADS, MLA_NOPE + MLA_V)
    w_ukv = jnp.concatenate([ukv[..., :MLA_NOPE].reshape(DEPTH, MLA_KV_RANK, -1),
                             ukv[..., MLA_NOPE:].reshape(DEPTH, MLA_KV_RANK, -1)], axis=-1).astype(BF16)
    w_br = jnp.concatenate([w_branch_conv, w_branch_gqa, w_branch_na, w_branch_mla], axis=1).astype(BF16)
    w_o_b, w_ff1_b, w_ff2_b = w_o.astype(BF16), w_ff1.astype(BF16), w_ff2.astype(BF16)
    g_a, g_m = g_attn[:, None, :], g_mlp[:, None, :]
    g_q, g_kv, g_f = mla_g_q[:, None, :], mla_g_kv[:, None, :], g_final[None, :]

    c16 = jnp.concatenate([c, c_ctx[None, :], jnp.zeros((MOD_ROWS - DEC_BATCH - 1, D_MODEL), F32)], axis=0)
    mods = _mod_call(c16, w_mod, b_mod).reshape(DEPTH, MOD_ROWS, 1, 6 * D_MODEL)
    nab = _nab_call(na_rpb)

    heads_t = lambda a: jnp.transpose(a, (0, 1, 3, 4, 2)).reshape(a.shape[0], DEPTH, -1, a.shape[2])
    caches = (heads_t(cache_gqa_k), heads_t(cache_gqa_v), heads_t(cache_na_k), heads_t(cache_na_v),
              cache_mla_ckv, jnp.transpose(cache_mla_krope, (0, 1, 3, 2)))

    h_ctx, h_lat = x_prompt, x_sample
    states = None
    for l in range(DEPTH):
        mixer_w = (gqa_sink, w_conv, g_q, w_uq, g_kv, w_ukv)
        post_w = (g_a, g_m, g_f, w_gates, w_br, w_o_b, w_ff1_b, w_ff2_b)

        proj = _pre_call(h_ctx, mods, g_a, w_main, l, True, f"pre_ctx_{l}")
        y, states = _ctx_mixer_call(proj, *mixer_w, states, l, f"mixer_ctx_{l}")
        h_ctx = _post_call(h_ctx, y, mods, *post_w, l, True, f"post_ctx_{l}")

        proj = _pre_call(h_lat, mods, g_a, w_main, l, False, f"pre_lat_{l}")
        y = _lat_mixer_call(proj, *mixer_w, nab, caches, l, f"mixer_lat_{l}")
        h_lat = _post_call(h_lat, y, mods, *post_w, l, False, f"post_lat_{l}")

    def heads_out(a, heads):
        return jnp.transpose(a.reshape(BATCH, DEPTH, heads, HEAD_DIM, SEQ), (0, 1, 4, 2, 3))

    kg, vg, kn, vn, ckv, kr = states
    return (h_ctx, h_lat, heads_out(kg, GQA_KV_HEADS), heads_out(vg, GQA_KV_HEADS),
            heads_out(kn, NA_HEADS), heads_out(vn, NA_HEADS), ckv, jnp.transpose(kr, (0, 1, 3, 2)))
```

```python
import functools
import math

import numpy as np
import jax
import jax.numpy as jnp
from jax import lax
from jax.experimental import pallas as pl
from jax.experimental.pallas import tpu as pltpu

D_MODEL = 1024
BATCH = 32
SEQ = 256
DEPTH = 2
DEC_BATCH = 8
DEC_SEQ = 1024
PAST_LEN = 512
GRID_W = 64
GRID_ROWS = DEC_SEQ // GRID_W
HEAD_DIM = 64
CONV_WIDTH = 256
CONV_K = 3
GQA_HEADS = 8
GQA_KV_HEADS = 2
WINDOW = 128
BAND_BLOCK = 128
NA_HEADS = 4
NA_WIN_H = 8
NA_WIN_W = 16
MLA_HEADS = 4
MLA_Q_RANK = 256
MLA_KV_RANK = 128
MLA_NOPE = 64
MLA_ROPE = 32
MLA_V = 64
D_FF = 4 * D_MODEL
N_BRANCH = 4
ROPE_BASE = 10000.0
EPS = 1e-6
NEG_INF = -1e30
LOG2E = math.log2(math.e)
ATTN_SCALE = HEAD_DIM ** -0.5
MLA_SCALE = (MLA_NOPE + MLA_ROPE) ** -0.5

LANES = 128
MOD_ROWS = 16
CTX_MOD_ROW = DEC_BATCH

C_CB, C_CC, C_CV = 0, 256, 512
C_GQ, C_GK, C_GV = 768, 1280, 1408
C_NQ, C_NK, C_NV = 1536, 1792, 2048
C_MQ, C_MKV, C_MKR = 2304, 2560, 2688
MAIN_COLS = 2816
GATE_COL0 = 2720
Y_CONV, Y_GQA, Y_NA, Y_MLA = 0, 256, 768, 1024
Y_COLS = 1280
NA_PATTERNS = 8
NA_KEYS = NA_WIN_H * GRID_W

VMEM_LIMIT = 56 * 1024 * 1024

F32 = jnp.float32
BF16 = jnp.bfloat16


def _dot(a, b):
    return jnp.dot(a, b, preferred_element_type=F32)


def _dot_nt(a, b):
    return lax.dot_general(a, b, (((1,), (1,)), ((), ())), preferred_element_type=F32)


def _rms(x, g):
    return x * lax.rsqrt(jnp.mean(x * x, axis=-1, keepdims=True) + EPS) * g


def _lane_lt(shape, n):
    return lax.broadcasted_iota(jnp.int32, shape, len(shape) - 1) < n


def _row_lt(shape, n):
    return lax.broadcasted_iota(jnp.int32, shape, 0) < n


def _row_group(shape, lo, hi):
    row = lax.broadcasted_iota(jnp.int32, shape, 0)
    return (row >= lo) & (row < hi)


def _short_conv(cb, cc, cv, w):
    u = cc * cv
    t = u.shape[0]
    row = lax.broadcasted_iota(jnp.int32, u.shape, 0)
    prev = jnp.where(row == 0, 0.0, pltpu.roll(u, 1, 0))
    nxt = jnp.where(row == t - 1, 0.0, pltpu.roll(u, t - 1, 0))
    return cb * (prev * w[0:1, :] + u * w[1:2, :] + nxt * w[2:3, :])


def _rope(x, cos, sin_hi, sin_lo, half):
    n = x.shape[-1]
    return x * cos + pltpu.roll(x, n - half, 1) * sin_lo + pltpu.roll(x, half, 1) * sin_hi


def _repeat_rope_key(tile):
    k = jnp.where(_lane_lt(tile.shape, MLA_ROPE), tile, 0.0)
    k = k + pltpu.roll(k, MLA_ROPE, 1)
    return k + pltpu.roll(k, 2 * MLA_ROPE, 1)


def _mod_kernel(c_ref, w_ref, b_ref, o_ref):
    c = c_ref[...]
    s = c * jax.nn.sigmoid(c)
    o_ref[...] = _dot(s.astype(BF16), w_ref[...].astype(BF16)) + b_ref[...]


def _mod_call(c16, w_mod, b_mod):
    tn = 1536
    return pl.pallas_call(
        _mod_kernel,
        grid=(DEPTH, 6 * D_MODEL // tn),
        in_specs=[
            pl.BlockSpec((MOD_ROWS, D_MODEL), lambda l, j: (0, 0)),
            pl.BlockSpec((None, D_MODEL, tn), lambda l, j: (l, 0, j)),
            pl.BlockSpec((None, 1, tn), lambda l, j: (l, 0, j)),
        ],
        out_specs=pl.BlockSpec((None, MOD_ROWS, tn), lambda l, j: (l, 0, j)),
        out_shape=jax.ShapeDtypeStruct((DEPTH, MOD_ROWS, 6 * D_MODEL), F32),
        compiler_params=pltpu.CompilerParams(
            dimension_semantics=("arbitrary", "arbitrary"), vmem_limit_bytes=VMEM_LIMIT),
        name="adaln_mod",
    )(c16, w_mod, b_mod.reshape(DEPTH, 1, 6 * D_MODEL))


def _nab_kernel(rpb_ref, o_ref):
    l = pl.program_id(0)
    c = lax.broadcasted_iota(jnp.int32, (GRID_W, GRID_W), 0)
    w = lax.broadcasted_iota(jnp.int32, (GRID_W, GRID_W), 1)
    dc = w - c + (NA_WIN_W - 1)
    c0 = jnp.clip(c - NA_WIN_W // 2, 0, GRID_W - NA_WIN_W)
    outside = (w < c0) | (w >= c0 + NA_WIN_W)
    n_dr, n_dc = 2 * NA_WIN_H - 1, 2 * NA_WIN_W - 1
    for h in range(NA_HEADS):
        for d in range(n_dr):
            base = ((l * NA_HEADS + h) * n_dr + d) * n_dc
            tile = jnp.full((GRID_W, GRID_W), NEG_INF, F32)
            for j in range(n_dc):
                tile = jnp.where(dc == j, rpb_ref[base + j] * LOG2E, tile)
            tile = jnp.where(outside, NEG_INF, tile)
            for p in range(NA_PATTERNS):
                a = d - (NA_WIN_H - 1) + p
                if 0 <= a < NA_WIN_H:
                    col = h * NA_KEYS + a * GRID_W
                    o_ref[p, :, col:col + GRID_W] = tile


def _nab_call(na_rpb):
    return pl.pallas_call(
        _nab_kernel,
        grid=(DEPTH,),
        in_specs=[pl.BlockSpec(memory_space=pltpu.SMEM)],
        out_specs=pl.BlockSpec((None, NA_PATTERNS, GRID_W, NA_HEADS * NA_KEYS), lambda l: (l, 0, 0, 0)),
        out_shape=jax.ShapeDtypeStruct((DEPTH, NA_PATTERNS, GRID_W, NA_HEADS * NA_KEYS), F32),
        compiler_params=pltpu.CompilerParams(dimension_semantics=("arbitrary",)),
        name="na_bias_tables",
    )(na_rpb.reshape(-1))


def _pre_kernel(x_ref, mod_ref, g_ref, w_ref, o_ref):
    h = _rms(x_ref[...], g_ref[...]) * (1.0 + mod_ref[:, D_MODEL:2 * D_MODEL]) + mod_ref[:, 0:D_MODEL]
    o_ref[...] = _dot(h.astype(BF16), w_ref[...])


def _mod_spec(layer, ctx, n_grid):
    if n_grid == 1:
        index = (lambda s: (layer, CTX_MOD_ROW, 0, 0)) if ctx else (lambda s: (layer, s, 0, 0))
    else:
        index = (lambda s, i: (layer, CTX_MOD_ROW, 0, 0)) if ctx else (lambda s, i: (layer, s, 0, 0))
    return pl.BlockSpec((None, None, 1, 6 * D_MODEL), index)


def _layer_spec(shape, layer, n_grid):
    zeros = (0,) * len(shape)
    index = (lambda s: (layer,) + zeros) if n_grid == 1 else (lambda s, i: (layer,) + zeros)
    return pl.BlockSpec((None,) + tuple(shape), index, pipeline_mode=pl.Buffered(1))


def _pre_call(x, mods, g, w_main, layer, ctx, name):
    n, t, _ = x.shape
    tm = min(t, 512)
    return pl.pallas_call(
        _pre_kernel,
        grid=(n, t // tm),
        in_specs=[
            pl.BlockSpec((None, tm, D_MODEL), lambda s, i: (s, i, 0)),
            _mod_spec(layer, ctx, 2),
            _layer_spec((1, D_MODEL), layer, 2),
            _layer_spec((D_MODEL, MAIN_COLS), layer, 2),
        ],
        out_specs=pl.BlockSpec((None, tm, MAIN_COLS), lambda s, i: (s, i, 0)),
        out_shape=jax.ShapeDtypeStruct((n, t, MAIN_COLS), F32),
        compiler_params=pltpu.CompilerParams(
            dimension_semantics=("arbitrary", "arbitrary"), vmem_limit_bytes=VMEM_LIMIT),
        name=name,
    )(x, mods, g, w_main)


def _key_planes(k_t, even_first, odd_first):
    lo = _row_lt(k_t.shape, HEAD_DIM)
    swapped = None
    if not even_first or odd_first:
        swapped = pltpu.roll(k_t, HEAD_DIM, 0)
    top = jnp.where(lo, k_t if even_first else swapped, 0.0)
    bot = jnp.where(lo, 0.0, swapped if odd_first else k_t)
    return top.astype(BF16), bot.astype(BF16)


def _value_planes(pair, even_first, odd_first):
    lo = _lane_lt(pair.shape, HEAD_DIM)
    swapped = None
    if not even_first or odd_first:
        swapped = pltpu.roll(pair, HEAD_DIM, 1)
    top = jnp.where(lo, pair if even_first else swapped, 0.0)
    bot = jnp.where(lo, 0.0, swapped if odd_first else pair)
    one_e = jnp.where(lo, 1.0, 0.0)
    return (jnp.concatenate([top, one_e], axis=1).astype(BF16),
            jnp.concatenate([bot, 1.0 - one_e], axis=1).astype(BF16))


def _probabilities(parts, sink=None):
    m = parts[0].max(axis=-1, keepdims=True)
    for p in parts[1:]:
        m = jnp.maximum(m, p.max(axis=-1, keepdims=True))
    if sink is not None:
        m = jnp.maximum(m, sink)
    probs = [jnp.exp2((p - m).astype(BF16)) for p in parts]
    return probs, (None if sink is None else jnp.exp2(sink - m))


def _attend(terms, sink_e=None, sink_o=None):
    o = None
    for p, v in terms:
        t = _dot(p, v)
        o = t if o is None else o + t
    den = o[:, LANES:]
    if sink_e is not None:
        den = den + jnp.where(_lane_lt(den.shape, HEAD_DIM), sink_e, sink_o)
    return o[:, :LANES] / den


def _mla_key_plane(kn_t_pair, kr4_t, j, odd):
    h = 2 * j + odd
    lo = _row_lt(kn_t_pair.shape, HEAD_DIM)
    nope = jnp.where(lo, 0.0, kn_t_pair) if odd else jnp.where(lo, kn_t_pair, 0.0)
    rope = jnp.where(_row_group(kr4_t.shape, h * MLA_ROPE, (h + 1) * MLA_ROPE), kr4_t, 0.0)
    return jnp.concatenate([nope, rope], axis=0).astype(BF16)


def _ctx_mixer_kernel(sink_ref, p_ref, wc_ref, gq_ref, wuq_ref, gkv_ref, wukt_ref, wuv_ref, *rest, layer):
    y_ref, kg_ref, vg_ref, kn_ref, vn_ref, ckv_ref, kr_ref = rest[-7:]
    t = SEQ
    col = lambda c, n: p_ref[:, c:c + n]

    kg_t = col(C_GK, LANES).T
    kn_t = col(C_NK, 2 * LANES).T
    kg_ref[...] = kg_t
    vg_ref[...] = col(C_GV, LANES).T
    kn_ref[...] = kn_t
    vn_ref[...] = col(C_NV, 2 * LANES).T
    ckv = _rms(col(C_MKV, MLA_KV_RANK), gkv_ref[...])
    ckv_ref[...] = ckv
    kr_t = col(C_MKR, LANES).T[0:MLA_ROPE, :]
    kr_ref[...] = kr_t

    y_ref[:, Y_CONV:Y_CONV + CONV_WIDTH] = _short_conv(
        col(C_CB, CONV_WIDTH), col(C_CC, CONV_WIDTH), col(C_CV, CONV_WIDTH), wc_ref[...]).astype(BF16)

    vpair = col(C_GV, LANES)
    row2 = lax.broadcasted_iota(jnp.int32, (2 * t, 1), 0) < t
    for g in range(GQA_KV_HEADS):
        ke, ko = _key_planes(kg_t, g == 0, g == 0)
        ve, vo = _value_planes(vpair, g == 0, g == 0)
        q = jnp.concatenate([col(C_GQ + (2 * g) * LANES, LANES), col(C_GQ + (2 * g + 1) * LANES, LANES)], axis=0)
        q = (q * (ATTN_SCALE * LOG2E)).astype(BF16)
        s = _dot(q, jnp.concatenate([ke, ko], axis=1))
        sink = [jnp.where(row2, sink_ref[layer, 4 * g + odd], sink_ref[layer, 4 * g + 2 + odd]) * LOG2E
                for odd in range(2)]
        (pe,), xe = _probabilities([s[:, 0:t]], sink[0])
        (po,), xo = _probabilities([s[:, t:2 * t]], sink[1])
        o = _attend([(pe, ve), (po, vo)], xe, xo).astype(BF16)
        c0 = Y_GQA + (2 * g) * LANES
        y_ref[:, c0:c0 + LANES] = o[0:t]
        y_ref[:, c0 + LANES:c0 + 2 * LANES] = o[t:2 * t]

    for j in range(NA_HEADS // 2):
        ke, ko = _key_planes(kn_t[j * LANES:(j + 1) * LANES, :], True, False)
        ve, vo = _value_planes(col(C_NV + j * LANES, LANES), True, False)
        q = (col(C_NQ + j * LANES, LANES) * (ATTN_SCALE * LOG2E)).astype(BF16)
        s = _dot(q, jnp.concatenate([ke, ko], axis=1))
        (pe,), _ = _probabilities([s[:, 0:t]])
        (po,), _ = _probabilities([s[:, t:2 * t]])
        y_ref[:, Y_NA + j * LANES:Y_NA + (j + 1) * LANES] = _attend([(pe, ve), (po, vo)]).astype(BF16)

    ckv_b = ckv.astype(BF16)
    q = _dot(_rms(col(C_MQ, MLA_Q_RANK), gq_ref[...]).astype(BF16), wuq_ref[...]) * (MLA_SCALE * LOG2E)
    kn_t_all = _dot_nt(wukt_ref[...], ckv_b)
    v_all = _dot(ckv_b, wuv_ref[...])
    kr4_t = jnp.concatenate([kr_t] * MLA_HEADS, axis=0)
    q_rope = q[:, 2 * LANES:3 * LANES]
    for j in range(MLA_HEADS // 2):
        qj = jnp.concatenate([q[:, j * LANES:(j + 1) * LANES], q_rope], axis=1).astype(BF16)
        kn_t_pair = kn_t_all[j * LANES:(j + 1) * LANES, :]
        keys = jnp.concatenate([_mla_key_plane(kn_t_pair, kr4_t, j, 0), _mla_key_plane(kn_t_pair, kr4_t, j, 1)], axis=1)
        s = _dot(qj, keys)
        ve, vo = _value_planes(v_all[:, j * LANES:(j + 1) * LANES], True, False)
        (pe,), _ = _probabilities([s[:, 0:t]])
        (po,), _ = _probabilities([s[:, t:2 * t]])
        y_ref[:, Y_MLA + j * LANES:Y_MLA + (j + 1) * LANES] = _attend([(pe, ve), (po, vo)]).astype(BF16)


def _ctx_mixer_call(proj, sink, w_conv, g_q, w_uq, g_kv, w_uk_t, w_uv, prev_states, layer, name):
    n = BATCH
    state_shapes = ((2 * HEAD_DIM, SEQ), (2 * HEAD_DIM, SEQ), (4 * HEAD_DIM, SEQ), (4 * HEAD_DIM, SEQ),
                    (SEQ, MLA_KV_RANK), (MLA_ROPE, SEQ))
    state_spec = lambda shape: pl.BlockSpec((None, None) + shape, lambda s: (s, layer, 0, 0))
    in_specs = [
        pl.BlockSpec(memory_space=pltpu.SMEM),
        pl.BlockSpec((None, SEQ, MAIN_COLS), lambda s: (s, 0, 0)),
        _layer_spec((CONV_K, CONV_WIDTH), layer, 1),
        _layer_spec((1, MLA_Q_RANK), layer, 1),
        _layer_spec((MLA_Q_RANK, 3 * LANES), layer, 1),
        _layer_spec((1, MLA_KV_RANK), layer, 1),
        _layer_spec((2 * LANES, MLA_KV_RANK), layer, 1),
        _layer_spec((MLA_KV_RANK, 2 * LANES), layer, 1),
    ]
    args = [sink, proj, w_conv, g_q, w_uq, g_kv, w_uk_t, w_uv]
    aliases = {}
    if prev_states is not None:
        for i, st in enumerate(prev_states):
            aliases[len(args)] = 1 + i
            in_specs.append(pl.BlockSpec(memory_space=pl.ANY))
            args.append(st)
    outs = pl.pallas_call(
        functools.partial(_ctx_mixer_kernel, layer=layer),
        grid=(n,),
        in_specs=in_specs,
        out_specs=[pl.BlockSpec((None, SEQ, Y_COLS), lambda s: (s, 0, 0))] + [state_spec(s) for s in state_shapes],
        out_shape=[jax.ShapeDtypeStruct((n, SEQ, Y_COLS), BF16)]
        + [jax.ShapeDtypeStruct((n, DEPTH) + s, F32) for s in state_shapes],
        input_output_aliases=aliases,
        compiler_params=pltpu.CompilerParams(
            dimension_semantics=("arbitrary",), vmem_limit_bytes=VMEM_LIMIT),
        name=name,
    )(*args)
    return outs[0], outs[1:]


def _lat_mixer_kernel(sink_ref, p_ref, wc_ref, gq_ref, wuq_ref, gkv_ref, wukt_ref, wuv_ref,
                      rc_ref, rsh_ref, rsl_ref, mc_ref, msh_ref, msl_ref, band_ref, nab_ref,
                      cgk_ref, cgv_ref, cnk_ref, cnv_ref, cckv_ref, ckr_ref,
                      y_ref,
                      gq_s, gk_s, gv_s, gkc_s, gvc_s, nq_s, nk_s, nv_s, nkc_s, nvc_s, mq_s, mk_s, mv_s, *, layer):
    t = DEC_SEQ
    col = lambda c, n: p_ref[:, c:c + n]
    rope64 = lambda x: _rope(x, rc_ref[...], rsh_ref[...], rsl_ref[...], HEAD_DIM // 4)
    rope32 = lambda x: _rope(x, mc_ref[...], msh_ref[...], msl_ref[...], MLA_ROPE // 4)

    y_ref[:, Y_CONV:Y_CONV + CONV_WIDTH] = _short_conv(
        col(C_CB, CONV_WIDTH), col(C_CC, CONV_WIDTH), col(C_CV, CONV_WIDTH), wc_ref[...]).astype(BF16)

    for j in range(GQA_HEADS // 2):
        gq_s[:, j * LANES:(j + 1) * LANES] = (
            rope64(col(C_GQ + j * LANES, LANES)) * (ATTN_SCALE * LOG2E)).astype(BF16)
    k_t, vpair = rope64(col(C_GK, LANES)).T, col(C_GV, LANES)
    kc_t, vc_pair = cgk_ref[...], cgv_ref[...].T
    for g in range(GQA_KV_HEADS):
        gk_s[2 * g], gk_s[2 * g + 1] = _key_planes(k_t, g == 0, g == 0)
        gv_s[2 * g], gv_s[2 * g + 1] = _value_planes(vpair, g == 0, g == 0)
        gkc_s[g] = jnp.concatenate(_key_planes(kc_t, g == 0, g == 0), axis=1)
        gvc_s[2 * g], gvc_s[2 * g + 1] = _value_planes(vc_pair, g == 0, g == 0)

    def gqa_block(b, c_lo, c_hi):
        q0 = pl.multiple_of(b * BAND_BLOCK, BAND_BLOCK)
        k0 = pl.multiple_of(q0 + (c_lo - WINDOW), BAND_BLOCK)
        n = c_hi - c_lo
        band = band_ref[:, c_lo:c_hi]
        row2 = lax.broadcasted_iota(jnp.int32, (2 * BAND_BLOCK, 1), 0) < BAND_BLOCK
        for g in range(GQA_KV_HEADS):
            q = jnp.concatenate([gq_s[pl.ds(q0, BAND_BLOCK), (2 * g) * LANES:(2 * g + 1) * LANES],
                                 gq_s[pl.ds(q0, BAND_BLOCK), (2 * g + 1) * LANES:(2 * g + 2) * LANES]], axis=0)
            s_ctx = _dot(q, gkc_s[g])
            terms, sinks = [], []
            for odd in range(2):
                s_loc = _dot(q, gk_s[2 * g + odd, :, pl.ds(k0, n)]) + band
                sink = jnp.where(row2, sink_ref[layer, 4 * g + odd], sink_ref[layer, 4 * g + 2 + odd]) * LOG2E
                (p_loc, p_ctx), x = _probabilities([s_loc, s_ctx[:, odd * PAST_LEN:(odd + 1) * PAST_LEN]], sink)
                terms += [(p_loc, gv_s[2 * g + odd, pl.ds(k0, n), :]), (p_ctx, gvc_s[2 * g + odd])]
                sinks.append(x)
            o = _attend(terms, sinks[0], sinks[1]).astype(BF16)
            c0 = Y_GQA + (2 * g) * LANES
            y_ref[pl.ds(q0, BAND_BLOCK), c0:c0 + LANES] = o[0:BAND_BLOCK]
            y_ref[pl.ds(q0, BAND_BLOCK), c0 + LANES:c0 + 2 * LANES] = o[BAND_BLOCK:2 * BAND_BLOCK]

    span = BAND_BLOCK + 2 * WINDOW
    nb = t // BAND_BLOCK
    gqa_block(0, WINDOW, span)
    lax.fori_loop(1, nb - 1, lambda b, c: (gqa_block(b, 0, span), c)[1], 0)
    gqa_block(nb - 1, 0, span - WINDOW)

    for j in range(NA_HEADS // 2):
        nq_s[:, j * LANES:(j + 1) * LANES] = (col(C_NQ + j * LANES, LANES) * (ATTN_SCALE * LOG2E)).astype(BF16)
        nk_t = col(C_NK + j * LANES, LANES).T
        for shift, k_t in enumerate((nk_t, pltpu.roll(nk_t, t - GRID_W, 1))):
            nk_s[shift, 2 * j], nk_s[shift, 2 * j + 1] = _key_planes(k_t, True, False)
        nv_s[2 * j], nv_s[2 * j + 1] = _value_planes(col(C_NV + j * LANES, LANES), True, False)
        nkc_s[j] = jnp.concatenate(_key_planes(cnk_ref[j * LANES:(j + 1) * LANES, :], True, False), axis=1)
        nvc_s[2 * j], nvc_s[2 * j + 1] = _value_planes(cnv_ref[j * LANES:(j + 1) * LANES, :].T, True, False)

    def na_row(r, carry):
        q0 = pl.multiple_of(r * GRID_W, GRID_W)
        r0 = jnp.clip(r - NA_WIN_H // 2, 0, GRID_ROWS - NA_WIN_H)
        k0 = pl.multiple_of(r0 * GRID_W, GRID_W)
        kt0 = pl.multiple_of(lax.shift_right_logical(r0, 1) * LANES, LANES)
        pat = r - r0
        for j in range(NA_HEADS // 2):
            q = nq_s[pl.ds(q0, GRID_W), j * LANES:(j + 1) * LANES]
            s_ctx = _dot(q, nkc_s[j])
            terms = []
            for odd in range(2):
                h = 2 * j + odd
                s_loc = _dot(q, nk_s[r0 & 1, h, :, pl.ds(kt0, NA_KEYS)])
                s_loc = s_loc + nab_ref[pat, :, h * NA_KEYS:(h + 1) * NA_KEYS]
                (p_loc, p_ctx), _ = _probabilities([s_loc, s_ctx[:, odd * PAST_LEN:(odd + 1) * PAST_LEN]])
                terms += [(p_loc, nv_s[h, pl.ds(k0, NA_KEYS), :]), (p_ctx, nvc_s[h])]
            y_ref[pl.ds(q0, GRID_W), Y_NA + j * LANES:Y_NA + (j + 1) * LANES] = _attend(terms).astype(BF16)
        return carry

    lax.fori_loop(0, GRID_ROWS, na_row, 0)

    ckv_b = _rms(col(C_MKV, MLA_KV_RANK), gkv_ref[...]).astype(BF16)
    cckv_b = cckv_ref[...].astype(BF16)
    q = _dot(_rms(col(C_MQ, MLA_Q_RANK), gq_ref[...]).astype(BF16), wuq_ref[...]) * (MLA_SCALE * LOG2E)
    mq_s[:, 0:2 * LANES] = q[:, 0:2 * LANES].astype(BF16)
    mq_s[:, 2 * LANES:3 * LANES] = rope32(q[:, 2 * LANES:3 * LANES]).astype(BF16)
    kr4_t = rope32(_repeat_rope_key(col(C_MKR, LANES))).T
    kr4_t_c = jnp.concatenate([ckr_ref[...]] * MLA_HEADS, axis=0)
    for cols, ckv_x, kr_x in ((slice(0, t), ckv_b, kr4_t), (slice(t, t + PAST_LEN), cckv_b, kr4_t_c)):
        kn_t_all = _dot_nt(wukt_ref[...], ckv_x)
        v_all = _dot(ckv_x, wuv_ref[...])
        for j in range(MLA_HEADS // 2):
            kn_t_pair = kn_t_all[j * LANES:(j + 1) * LANES, :]
            mv_s[2 * j, cols, :], mv_s[2 * j + 1, cols, :] = _value_planes(
                v_all[:, j * LANES:(j + 1) * LANES], True, False)
            for odd in range(2):
                mk_s[2 * j + odd, :, cols] = _mla_key_plane(kn_t_pair, kr_x, j, odd)

    tq = 256

    def mla_block(i, carry):
        q0 = pl.multiple_of(i * tq, tq)
        for j in range(MLA_HEADS // 2):
            qj = jnp.concatenate([mq_s[pl.ds(q0, tq), j * LANES:(j + 1) * LANES],
                                  mq_s[pl.ds(q0, tq), 2 * LANES:3 * LANES]], axis=1)
            (pe,), _ = _probabilities([_dot(qj, mk_s[2 * j])])
            (po,), _ = _probabilities([_dot(qj, mk_s[2 * j + 1])])
            o = _attend([(pe, mv_s[2 * j]), (po, mv_s[2 * j + 1])])
            y_ref[pl.ds(q0, tq), Y_MLA + j * LANES:Y_MLA + (j + 1) * LANES] = o.astype(BF16)
        return carry

    lax.fori_loop(0, t // tq, mla_block, 0)


def _rope_tables(group, half):
    tok = np.arange(DEC_SEQ)
    pos = np.stack([tok // GRID_W, tok % GRID_W], axis=1).astype(np.float64)
    inv = ROPE_BASE ** (-np.arange(half, dtype=np.float64) / half)
    lane = np.arange(LANES) % group
    axis = lane // (2 * half)
    within = lane % (2 * half)
    ang = pos[:, axis] * inv[within % half][None, :]
    cos, sin = np.cos(ang), np.sin(ang)
    upper = (within >= half)[None, :]
    sin_hi = np.where(upper, sin, 0.0)
    sin_lo = np.where(upper, 0.0, -sin)
    return tuple(jnp.asarray(a, dtype=F32) for a in (cos, sin_hi, sin_lo))


def _band_mask():
    i = np.arange(BAND_BLOCK)[:, None]
    c = np.arange(BAND_BLOCK + 2 * WINDOW)[None, :]
    ok = (c >= i) & (c <= i + 2 * WINDOW)
    m = np.where(ok, 0.0, NEG_INF)
    return jnp.asarray(np.concatenate([m, m], axis=0), dtype=F32)


def _lat_mixer_call(proj, sink, w_conv, g_q, w_uq, g_kv, w_uk_t, w_uv, nab, caches, layer, name):
    n, t = DEC_BATCH, DEC_SEQ
    one = pl.Buffered(1)
    const = lambda shape: pl.BlockSpec(shape, lambda s: (0,) * len(shape), pipeline_mode=one)
    cache = lambda shape: pl.BlockSpec((None, None) + shape, lambda s: (s, layer, 0, 0))
    rope_g = _rope_tables(HEAD_DIM, HEAD_DIM // 4)
    rope_m = _rope_tables(MLA_ROPE, MLA_ROPE // 4)
    span = BAND_BLOCK + 2 * WINDOW
    keys = t + PAST_LEN
    scratch = [
        pltpu.VMEM((t, GQA_HEADS // 2 * LANES), BF16),
        pltpu.VMEM((2 * GQA_KV_HEADS, LANES, t), BF16),
        pltpu.VMEM((2 * GQA_KV_HEADS, t, 2 * LANES), BF16),
        pltpu.VMEM((GQA_KV_HEADS, LANES, 2 * PAST_LEN), BF16),
        pltpu.VMEM((2 * GQA_KV_HEADS, PAST_LEN, 2 * LANES), BF16),
        pltpu.VMEM((t, NA_HEADS // 2 * LANES), BF16),
        pltpu.VMEM((2, NA_HEADS, LANES, t), BF16),
        pltpu.VMEM((NA_HEADS, t, 2 * LANES), BF16),
        pltpu.VMEM((NA_HEADS // 2, LANES, 2 * PAST_LEN), BF16),
        pltpu.VMEM((NA_HEADS, PAST_LEN, 2 * LANES), BF16),
        pltpu.VMEM((t, 3 * LANES), BF16),
        pltpu.VMEM((MLA_HEADS, 2 * LANES, keys), BF16),
        pltpu.VMEM((MLA_HEADS, keys, 2 * LANES), BF16),
    ]
    return pl.pallas_call(
        functools.partial(_lat_mixer_kernel, layer=layer),
        grid=(n,),
        in_specs=[
            pl.BlockSpec(memory_space=pltpu.SMEM),
            pl.BlockSpec((None, t, MAIN_COLS), lambda s: (s, 0, 0), pipeline_mode=one),
            _layer_spec((CONV_K, CONV_WIDTH), layer, 1),
            _layer_spec((1, MLA_Q_RANK), layer, 1),
            _layer_spec((MLA_Q_RANK, 3 * LANES), layer, 1),
            _layer_spec((1, MLA_KV_RANK), layer, 1),
            _layer_spec((2 * LANES, MLA_KV_RANK), layer, 1),
            _layer_spec((MLA_KV_RANK, 2 * LANES), layer, 1),
        ] + [const((t, LANES))] * 6 + [
            const((2 * BAND_BLOCK, span)),
            _layer_spec((NA_PATTERNS, GRID_W, NA_HEADS * NA_KEYS), layer, 1),
            cache((2 * HEAD_DIM, PAST_LEN)), cache((2 * HEAD_DIM, PAST_LEN)),
            cache((4 * HEAD_DIM, PAST_LEN)), cache((4 * HEAD_DIM, PAST_LEN)),
            cache((PAST_LEN, MLA_KV_RANK)), cache((MLA_ROPE, PAST_LEN)),
        ],
        out_specs=pl.BlockSpec((None, t, Y_COLS), lambda s: (s, 0, 0)),
        out_shape=jax.ShapeDtypeStruct((n, t, Y_COLS), BF16),
        scratch_shapes=scratch,
        compiler_params=pltpu.CompilerParams(
            dimension_semantics=("arbitrary",), vmem_limit_bytes=VMEM_LIMIT),
        name=name,
    )(sink, proj, w_conv, g_q, w_uq, g_kv, w_uk_t, w_uv, *rope_g, *rope_m, _band_mask(), nab, *caches)


def _post_kernel(x_ref, y_ref, mod_ref, ga_ref, gm_ref, gf_ref, wg_ref, wb_ref, wo_ref, w1_ref, w2_ref,
                 o_ref, *, final):
    d = D_MODEL
    mod = lambda i: mod_ref[:, i * d:(i + 1) * d]
    x = x_ref[...]
    h = (_rms(x, ga_ref[...]) * (1.0 + mod(1)) + mod(0)).astype(BF16)
    gates = jax.nn.sigmoid(_dot(h, wg_ref[...]))
    bounds = (Y_CONV, Y_GQA, Y_NA, Y_MLA, Y_COLS)
    merged = None
    for i in range(N_BRANCH):
        lo, hi = bounds[i], bounds[i + 1]
        term = gates[:, i * d:(i + 1) * d] * _dot(y_ref[:, lo:hi], wb_ref[lo:hi, :])
        merged = term if merged is None else merged + term
    x = x + mod(2) * _dot(merged.astype(BF16), wo_ref[...])
    h = (_rms(x, gm_ref[...]) * (1.0 + mod(4)) + mod(3)).astype(BF16)
    f = jnp.square(jnp.maximum(_dot(h, w1_ref[...]), 0.0)).astype(BF16)
    x = x + mod(5) * _dot(f, w2_ref[...])
    o_ref[...] = _rms(x, gf_ref[...]) if final else x


def _post_call(x, y, mods, g_attn, g_mlp, g_final, w_gates, w_br, w_o, w_ff1, w_ff2, layer, ctx, name):
    n, t, _ = x.shape
    tm = 256
    return pl.pallas_call(
        functools.partial(_post_kernel, final=layer == DEPTH - 1),
        grid=(n, t // tm),
        in_specs=[
            pl.BlockSpec((None, tm, D_MODEL), lambda s, i: (s, i, 0)),
            pl.BlockSpec((None, tm, Y_COLS), lambda s, i: (s, i, 0)),
            _mod_spec(layer, ctx, 2),
            _layer_spec((1, D_MODEL), layer, 2),
            _layer_spec((1, D_MODEL), layer, 2),
            pl.BlockSpec((1, D_MODEL), lambda s, i: (0, 0)),
            _layer_spec((D_MODEL, N_BRANCH * D_MODEL), layer, 2),
            _layer_spec((Y_COLS, D_MODEL), layer, 2),
            _layer_spec((D_MODEL, D_MODEL), layer, 2),
            _layer_spec((D_MODEL, D_FF), layer, 2),
            _layer_spec((D_FF, D_MODEL), layer, 2),
        ],
        out_specs=pl.BlockSpec((None, tm, D_MODEL), lambda s, i: (s, i, 0)),
        out_shape=jax.ShapeDtypeStruct((n, t, D_MODEL), F32),
        compiler_params=pltpu.CompilerParams(
            dimension_semantics=("arbitrary", "arbitrary"), vmem_limit_bytes=VMEM_LIMIT),
        name=name,
    )(x, y, mods, g_attn, g_mlp, g_final, w_gates, w_br, w_o, w_ff1, w_ff2)


def kernel(x_prompt, x_sample, cache_gqa_k, cache_gqa_v, cache_na_k, cache_na_v, cache_mla_ckv, cache_mla_krope, c, c_ctx, w_mod, b_mod, g_attn, g_mlp, w_in, w_conv, gqa_sink, na_rpb, mla_g_q, mla_w_uq, mla_g_kv, mla_w_ukv, w_branch_conv, w_branch_gqa, w_branch_na, w_branch_mla, w_o, w_ff1, w_ff2, g_final):
    w_main = w_in[:, :, :MAIN_COLS].astype(BF16)
    w_gates = w_in[:, :, GATE_COL0:].astype(BF16)
    uq = mla_w_uq.reshape(DEPTH, MLA_Q_RANK, MLA_HEADS, MLA_NOPE + MLA_ROPE)
    w_uq = jnp.concatenate([uq[..., :MLA_NOPE].reshape(DEPTH, MLA_Q_RANK, -1),
                            uq[..., MLA_NOPE:].reshape(DEPTH, MLA_Q_RANK, -1)], axis=-1).astype(BF16)
    ukv = mla_w_ukv.reshape(DEPTH, MLA_KV_RANK, MLA_HEADS, MLA_NOPE + MLA_V)
    w_uk_t = jnp.transpose(ukv[..., :MLA_NOPE].reshape(DEPTH, MLA_KV_RANK, -1), (0, 2, 1)).astype(BF16)
    w_uv = ukv[..., MLA_NOPE:].reshape(DEPTH, MLA_KV_RANK, -1).astype(BF16)
    w_br = jnp.concatenate([w_branch_conv, w_branch_gqa, w_branch_na, w_branch_mla], axis=1).astype(BF16)
    w_o_b, w_ff1_b, w_ff2_b = w_o.astype(BF16), w_ff1.astype(BF16), w_ff2.astype(BF16)
    g_a, g_m = g_attn[:, None, :], g_mlp[:, None, :]
    g_q, g_kv, g_f = mla_g_q[:, None, :], mla_g_kv[:, None, :], g_final[None, :]

    c16 = jnp.concatenate([c, c_ctx[None, :], jnp.zeros((MOD_ROWS - DEC_BATCH - 1, D_MODEL), F32)], axis=0)
    mods = _mod_call(c16, w_mod, b_mod).reshape(DEPTH, MOD_ROWS, 1, 6 * D_MODEL)
    nab = _nab_call(na_rpb)

    heads_t = lambda a: jnp.transpose(a, (0, 1, 3, 4, 2)).reshape(a.shape[0], DEPTH, -1, a.shape[2])
    caches = (heads_t(cache_gqa_k), heads_t(cache_gqa_v), heads_t(cache_na_k), heads_t(cache_na_v),
              cache_mla_ckv, jnp.transpose(cache_mla_krope, (0, 1, 3, 2)))

    h_ctx, h_lat = x_prompt, x_sample
    states = None
    for l in range(DEPTH):
        mixer_w = (gqa_sink, w_conv, g_q, w_uq, g_kv, w_uk_t, w_uv)
        post_w = (g_a, g_m, g_f, w_gates, w_br, w_o_b, w_ff1_b, w_ff2_b)

        proj = _pre_call(h_ctx, mods, g_a, w_main, l, True, f"pre_ctx_{l}")
        y, states = _ctx_mixer_call(proj, *mixer_w, states, l, f"mixer_ctx_{l}")
        h_ctx = _post_call(h_ctx, y, mods, *post_w, l, True, f"post_ctx_{l}")

        proj = _pre_call(h_lat, mods, g_a, w_main, l, False, f"pre_lat_{l}")
        y = _lat_mixer_call(proj, *mixer_w, nab, caches, l, f"mixer_lat_{l}")
        h_lat = _post_call(h_lat, y, mods, *post_w, l, False, f"post_lat_{l}")

    def heads_out(a, heads):
        return jnp.transpose(a.reshape(BATCH, DEPTH, heads, HEAD_DIM, SEQ), (0, 1, 4, 2, 3))

    kg, vg, kn, vn, ckv, kr = states
    return (h_ctx, h_lat, heads_out(kg, GQA_KV_HEADS), heads_out(vg, GQA_KV_HEADS),
            heads_out(kn, NA_HEADS), heads_out(vn, NA_HEADS), ckv, jnp.transpose(kr, (0, 1, 3, 2)))
```

```python
import functools
import math

import numpy as np
import jax
import jax.numpy as jnp
from jax import lax
from jax.experimental import pallas as pl
from jax.experimental.pallas import tpu as pltpu

D_MODEL = 1024
BATCH = 32
SEQ = 256
DEPTH = 2
DEC_BATCH = 8
DEC_SEQ = 1024
PAST_LEN = 512
GRID_W = 64
GRID_ROWS = DEC_SEQ // GRID_W
HEAD_DIM = 64
CONV_WIDTH = 256
CONV_K = 3
GQA_HEADS = 8
GQA_KV_HEADS = 2
WINDOW = 128
BAND_BLOCK = 128
NA_HEADS = 4
NA_WIN_H = 8
NA_WIN_W = 16
MLA_HEADS = 4
MLA_Q_RANK = 256
MLA_KV_RANK = 128
MLA_NOPE = 64
MLA_ROPE = 32
MLA_V = 64
D_FF = 4 * D_MODEL
N_BRANCH = 4
ROPE_BASE = 10000.0
EPS = 1e-6
NEG_INF = -1e30
LOG2E = math.log2(math.e)
ATTN_SCALE = HEAD_DIM ** -0.5
MLA_SCALE = (MLA_NOPE + MLA_ROPE) ** -0.5

LANES = 128
MOD_ROWS = 16
CTX_MOD_ROW = DEC_BATCH

C_CB, C_CC, C_CV = 0, 256, 512
C_GQ, C_GK, C_GV = 768, 1280, 1408
C_NQ, C_NK, C_NV = 1536, 1792, 2048
C_MQ, C_MKV, C_MKR = 2304, 2560, 2688
MAIN_COLS = 2816
GATE_COL0 = 2720
Y_CONV, Y_GQA, Y_NA, Y_MLA = 0, 256, 768, 1024
Y_COLS = 1280
NA_PATTERNS = 8
NA_KEYS = NA_WIN_H * GRID_W

VMEM_LIMIT = 56 * 1024 * 1024

F32 = jnp.float32
BF16 = jnp.bfloat16


def _dot(a, b):
    return jnp.dot(a, b, preferred_element_type=F32)


def _dot_nt(a, b):
    return lax.dot_general(a, b, (((1,), (1,)), ((), ())), preferred_element_type=F32)


def _rms(x, g):
    return x * lax.rsqrt(jnp.mean(x * x, axis=-1, keepdims=True) + EPS) * g


def _lane_lt(shape, n):
    return lax.broadcasted_iota(jnp.int32, shape, len(shape) - 1) < n


def _row_lt(shape, n):
    return lax.broadcasted_iota(jnp.int32, shape, 0) < n


def _row_group(shape, lo, hi):
    row = lax.broadcasted_iota(jnp.int32, shape, 0)
    return (row >= lo) & (row < hi)


def _short_conv(cb, cc, cv, w):
    u = cc * cv
    t = u.shape[0]
    row = lax.broadcasted_iota(jnp.int32, u.shape, 0)
    prev = jnp.where(row == 0, 0.0, pltpu.roll(u, 1, 0))
    nxt = jnp.where(row == t - 1, 0.0, pltpu.roll(u, t - 1, 0))
    return cb * (prev * w[0:1, :] + u * w[1:2, :] + nxt * w[2:3, :])


def _rope(x, cos, sin_hi, sin_lo, half):
    n = x.shape[-1]
    return x * cos + pltpu.roll(x, n - half, 1) * sin_lo + pltpu.roll(x, half, 1) * sin_hi


def _repeat_rope_key(tile):
    k = jnp.where(_lane_lt(tile.shape, MLA_ROPE), tile, 0.0)
    k = k + pltpu.roll(k, MLA_ROPE, 1)
    return k + pltpu.roll(k, 2 * MLA_ROPE, 1)


def _mod_kernel(c_ref, w_ref, b_ref, o_ref):
    c = c_ref[...]
    s = c * jax.nn.sigmoid(c)
    o_ref[...] = _dot(s.astype(BF16), w_ref[...].astype(BF16)) + b_ref[...]


def _mod_call(c16, w_mod, b_mod):
    tn = 1536
    return pl.pallas_call(
        _mod_kernel,
        grid=(DEPTH, 6 * D_MODEL // tn),
        in_specs=[
            pl.BlockSpec((MOD_ROWS, D_MODEL), lambda l, j: (0, 0)),
            pl.BlockSpec((None, D_MODEL, tn), lambda l, j: (l, 0, j)),
            pl.BlockSpec((None, 1, tn), lambda l, j: (l, 0, j)),
        ],
        out_specs=pl.BlockSpec((None, MOD_ROWS, tn), lambda l, j: (l, 0, j)),
        out_shape=jax.ShapeDtypeStruct((DEPTH, MOD_ROWS, 6 * D_MODEL), F32),
        compiler_params=pltpu.CompilerParams(
            dimension_semantics=("arbitrary", "arbitrary"), vmem_limit_bytes=VMEM_LIMIT),
        name="adaln_mod",
    )(c16, w_mod, b_mod.reshape(DEPTH, 1, 6 * D_MODEL))


def _nab_kernel(rpb_ref, o_ref):
    l = pl.program_id(0)
    c = lax.broadcasted_iota(jnp.int32, (GRID_W, GRID_W), 0)
    w = lax.broadcasted_iota(jnp.int32, (GRID_W, GRID_W), 1)
    dc = w - c + (NA_WIN_W - 1)
    c0 = jnp.clip(c - NA_WIN_W // 2, 0, GRID_W - NA_WIN_W)
    outside = (w < c0) | (w >= c0 + NA_WIN_W)
    n_dr, n_dc = 2 * NA_WIN_H - 1, 2 * NA_WIN_W - 1
    for h in range(NA_HEADS):
        for d in range(n_dr):
            base = ((l * NA_HEADS + h) * n_dr + d) * n_dc
            tile = jnp.full((GRID_W, GRID_W), NEG_INF, F32)
            for j in range(n_dc):
                tile = jnp.where(dc == j, rpb_ref[base + j] * LOG2E, tile)
            tile = jnp.where(outside, NEG_INF, tile)
            for p in range(NA_PATTERNS):
                a = d - (NA_WIN_H - 1) + p
                if 0 <= a < NA_WIN_H:
                    col = h * NA_KEYS + a * GRID_W
                    o_ref[p, :, col:col + GRID_W] = tile


def _nab_call(na_rpb):
    return pl.pallas_call(
        _nab_kernel,
        grid=(DEPTH,),
        in_specs=[pl.BlockSpec(memory_space=pltpu.SMEM)],
        out_specs=pl.BlockSpec((None, NA_PATTERNS, GRID_W, NA_HEADS * NA_KEYS), lambda l: (l, 0, 0, 0)),
        out_shape=jax.ShapeDtypeStruct((DEPTH, NA_PATTERNS, GRID_W, NA_HEADS * NA_KEYS), F32),
        compiler_params=pltpu.CompilerParams(dimension_semantics=("arbitrary",)),
        name="na_bias_tables",
    )(na_rpb.reshape(-1))


def _pre_kernel(x_ref, mod_ref, g_ref, w_ref, o_ref):
    h = _rms(x_ref[...], g_ref[...]) * (1.0 + mod_ref[:, D_MODEL:2 * D_MODEL]) + mod_ref[:, 0:D_MODEL]
    o_ref[...] = _dot(h.astype(BF16), w_ref[...])


def _mod_spec(layer, ctx, n_grid):
    if n_grid == 1:
        index = (lambda s: (layer, CTX_MOD_ROW, 0, 0)) if ctx else (lambda s: (layer, s, 0, 0))
    else:
        index = (lambda s, i: (layer, CTX_MOD_ROW, 0, 0)) if ctx else (lambda s, i: (layer, s, 0, 0))
    return pl.BlockSpec((None, None, 1, 6 * D_MODEL), index)


def _layer_spec(shape, layer, n_grid):
    zeros = (0,) * len(shape)
    index = (lambda s: (layer,) + zeros) if n_grid == 1 else (lambda s, i: (layer,) + zeros)
    return pl.BlockSpec((None,) + tuple(shape), index, pipeline_mode=pl.Buffered(1))


def _pre_call(x, mods, g, w_main, layer, ctx, name):
    n, t, _ = x.shape
    tm = min(t, 512)
    return pl.pallas_call(
        _pre_kernel,
        grid=(n, t // tm),
        in_specs=[
            pl.BlockSpec((None, tm, D_MODEL), lambda s, i: (s, i, 0)),
            _mod_spec(layer, ctx, 2),
            _layer_spec((1, D_MODEL), layer, 2),
            _layer_spec((D_MODEL, MAIN_COLS), layer, 2),
        ],
        out_specs=pl.BlockSpec((None, tm, MAIN_COLS), lambda s, i: (s, i, 0)),
        out_shape=jax.ShapeDtypeStruct((n, t, MAIN_COLS), F32),
        compiler_params=pltpu.CompilerParams(
            dimension_semantics=("arbitrary", "arbitrary"), vmem_limit_bytes=VMEM_LIMIT),
        name=name,
    )(x, mods, g, w_main)


def _key_planes(k_t, even_first, odd_first):
    lo = _row_lt(k_t.shape, HEAD_DIM)
    swapped = None
    if not even_first or odd_first:
        swapped = pltpu.roll(k_t, HEAD_DIM, 0)
    top = jnp.where(lo, k_t if even_first else swapped, 0.0)
    bot = jnp.where(lo, 0.0, swapped if odd_first else k_t)
    return top.astype(BF16), bot.astype(BF16)


def _value_planes(pair, even_first, odd_first):
    lo = _lane_lt(pair.shape, HEAD_DIM)
    swapped = None
    if not even_first or odd_first:
        swapped = pltpu.roll(pair, HEAD_DIM, 1)
    top = jnp.where(lo, pair if even_first else swapped, 0.0)
    bot = jnp.where(lo, 0.0, swapped if odd_first else pair)
    one_e = jnp.where(lo, 1.0, 0.0)
    return (jnp.concatenate([top, one_e], axis=1).astype(BF16),
            jnp.concatenate([bot, 1.0 - one_e], axis=1).astype(BF16))


def _probabilities(parts, sink=None):
    m = parts[0].max(axis=-1, keepdims=True)
    for p in parts[1:]:
        m = jnp.maximum(m, p.max(axis=-1, keepdims=True))
    if sink is not None:
        m = jnp.maximum(m, sink)
    probs = [jnp.exp2((p - m).astype(BF16)) for p in parts]
    return probs, (None if sink is None else jnp.exp2(sink - m))


def _attend(terms, sink_e=None, sink_o=None):
    o = None
    for p, v in terms:
        t = _dot(p, v)
        o = t if o is None else o + t
    den = o[:, LANES:]
    if sink_e is not None:
        den = den + jnp.where(_lane_lt(den.shape, HEAD_DIM), sink_e, sink_o)
    return o[:, :LANES] / den


def _mla_key_plane(kn_t_pair, kr4_t, j, odd):
    h = 2 * j + odd
    lo = _row_lt(kn_t_pair.shape, HEAD_DIM)
    nope = jnp.where(lo, 0.0, kn_t_pair) if odd else jnp.where(lo, kn_t_pair, 0.0)
    rope = jnp.where(_row_group(kr4_t.shape, h * MLA_ROPE, (h + 1) * MLA_ROPE), kr4_t, 0.0)
    return jnp.concatenate([nope, rope], axis=0).astype(BF16)


def _ctx_mixer_kernel(sink_ref, p_ref, wc_ref, gq_ref, wuq_ref, gkv_ref, wukt_ref, wuv_ref, *rest, layer):
    y_ref, kg_ref, vg_ref, kn_ref, vn_ref, ckv_ref, kr_ref = rest[-7:]
    t = SEQ
    col = lambda c, n: p_ref[:, c:c + n]

    kg_t = col(C_GK, LANES).T
    kn_t = col(C_NK, 2 * LANES).T
    kg_ref[...] = kg_t
    vg_ref[...] = col(C_GV, LANES).T
    kn_ref[...] = kn_t
    vn_ref[...] = col(C_NV, 2 * LANES).T
    ckv = _rms(col(C_MKV, MLA_KV_RANK), gkv_ref[...])
    ckv_ref[...] = ckv
    kr_t = col(C_MKR, LANES).T[0:MLA_ROPE, :]
    kr_ref[...] = kr_t

    y_ref[:, Y_CONV:Y_CONV + CONV_WIDTH] = _short_conv(
        col(C_CB, CONV_WIDTH), col(C_CC, CONV_WIDTH), col(C_CV, CONV_WIDTH), wc_ref[...]).astype(BF16)

    vpair = col(C_GV, LANES)
    row2 = lax.broadcasted_iota(jnp.int32, (2 * t, 1), 0) < t
    for g in range(GQA_KV_HEADS):
        ke, ko = _key_planes(kg_t, g == 0, g == 0)
        ve, vo = _value_planes(vpair, g == 0, g == 0)
        q = jnp.concatenate([col(C_GQ + (2 * g) * LANES, LANES), col(C_GQ + (2 * g + 1) * LANES, LANES)], axis=0)
        q = (q * (ATTN_SCALE * LOG2E)).astype(BF16)
        s = _dot(q, jnp.concatenate([ke, ko], axis=1))
        sink = [jnp.where(row2, sink_ref[layer, 4 * g + odd], sink_ref[layer, 4 * g + 2 + odd]) * LOG2E
                for odd in range(2)]
        (pe,), xe = _probabilities([s[:, 0:t]], sink[0])
        (po,), xo = _probabilities([s[:, t:2 * t]], sink[1])
        o = _attend([(pe, ve), (po, vo)], xe, xo).astype(BF16)
        c0 = Y_GQA + (2 * g) * LANES
        y_ref[:, c0:c0 + LANES] = o[0:t]
        y_ref[:, c0 + LANES:c0 + 2 * LANES] = o[t:2 * t]

    for j in range(NA_HEADS // 2):
        ke, ko = _key_planes(kn_t[j * LANES:(j + 1) * LANES, :], True, False)
        ve, vo = _value_planes(col(C_NV + j * LANES, LANES), True, False)
        q = (col(C_NQ + j * LANES, LANES) * (ATTN_SCALE * LOG2E)).astype(BF16)
        s = _dot(q, jnp.concatenate([ke, ko], axis=1))
        (pe,), _ = _probabilities([s[:, 0:t]])
        (po,), _ = _probabilities([s[:, t:2 * t]])
        y_ref[:, Y_NA + j * LANES:Y_NA + (j + 1) * LANES] = _attend([(pe, ve), (po, vo)]).astype(BF16)

    ckv_b = ckv.astype(BF16)
    q = _dot(_rms(col(C_MQ, MLA_Q_RANK), gq_ref[...]).astype(BF16), wuq_ref[...]) * (MLA_SCALE * LOG2E)
    kn_t_all = _dot_nt(wukt_ref[...], ckv_b)
    v_all = _dot(ckv_b, wuv_ref[...])
    kr4_t = jnp.concatenate([kr_t] * MLA_HEADS, axis=0)
    q_rope = q[:, 2 * LANES:3 * LANES]
    for j in range(MLA_HEADS // 2):
        qj = jnp.concatenate([q[:, j * LANES:(j + 1) * LANES], q_rope], axis=1).astype(BF16)
        kn_t_pair = kn_t_all[j * LANES:(j + 1) * LANES, :]
        keys = jnp.concatenate([_mla_key_plane(kn_t_pair, kr4_t, j, 0), _mla_key_plane(kn_t_pair, kr4_t, j, 1)], axis=1)
        s = _dot(qj, keys)
        ve, vo = _value_planes(v_all[:, j * LANES:(j + 1) * LANES], True, False)
        (pe,), _ = _probabilities([s[:, 0:t]])
        (po,), _ = _probabilities([s[:, t:2 * t]])
        y_ref[:, Y_MLA + j * LANES:Y_MLA + (j + 1) * LANES] = _attend([(pe, ve), (po, vo)]).astype(BF16)


def _ctx_mixer_call(proj, sink, w_conv, g_q, w_uq, g_kv, w_uk_t, w_uv, prev_states, layer, name):
    n = BATCH
    state_shapes = ((2 * HEAD_DIM, SEQ), (2 * HEAD_DIM, SEQ), (4 * HEAD_DIM, SEQ), (4 * HEAD_DIM, SEQ),
                    (SEQ, MLA_KV_RANK), (MLA_ROPE, SEQ))
    state_spec = lambda shape: pl.BlockSpec((None, None) + shape, lambda s: (s, layer, 0, 0))
    in_specs = [
        pl.BlockSpec(memory_space=pltpu.SMEM),
        pl.BlockSpec((None, SEQ, MAIN_COLS), lambda s: (s, 0, 0)),
        _layer_spec((CONV_K, CONV_WIDTH), layer, 1),
        _layer_spec((1, MLA_Q_RANK), layer, 1),
        _layer_spec((MLA_Q_RANK, 3 * LANES), layer, 1),
        _layer_spec((1, MLA_KV_RANK), layer, 1),
        _layer_spec((2 * LANES, MLA_KV_RANK), layer, 1),
        _layer_spec((MLA_KV_RANK, 2 * LANES), layer, 1),
    ]
    args = [sink, proj, w_conv, g_q, w_uq, g_kv, w_uk_t, w_uv]
    aliases = {}
    if prev_states is not None:
        for i, st in enumerate(prev_states):
            aliases[len(args)] = 1 + i
            in_specs.append(pl.BlockSpec(memory_space=pl.ANY))
            args.append(st)
    outs = pl.pallas_call(
        functools.partial(_ctx_mixer_kernel, layer=layer),
        grid=(n,),
        in_specs=in_specs,
        out_specs=[pl.BlockSpec((None, SEQ, Y_COLS), lambda s: (s, 0, 0))] + [state_spec(s) for s in state_shapes],
        out_shape=[jax.ShapeDtypeStruct((n, SEQ, Y_COLS), BF16)]
        + [jax.ShapeDtypeStruct((n, DEPTH) + s, F32) for s in state_shapes],
        input_output_aliases=aliases,
        compiler_params=pltpu.CompilerParams(
            dimension_semantics=("arbitrary",), vmem_limit_bytes=VMEM_LIMIT),
        name=name,
    )(*args)
    return outs[0], outs[1:]


def _lat_mixer_kernel(sink_ref, p_ref, wc_ref, gq_ref, wuq_ref, gkv_ref, wukt_ref, wuv_ref,
                      rc_ref, rsh_ref, rsl_ref, mc_ref, msh_ref, msl_ref, band_ref, nab_ref,
                      cgk_ref, cgv_ref, cnk_ref, cnv_ref, cckv_ref, ckr_ref,
                      y_ref,
                      gq_s, gk_s, gv_s, gkc_s, gvc_s, nq_s, nk_s, nv_s, nkc_s, nvc_s, mq_s, mk_s, mv_s, *, layer):
    t = DEC_SEQ
    col = lambda c, n: p_ref[:, c:c + n]
    rope64 = lambda x: _rope(x, rc_ref[...], rsh_ref[...], rsl_ref[...], HEAD_DIM // 4)
    rope32 = lambda x: _rope(x, mc_ref[...], msh_ref[...], msl_ref[...], MLA_ROPE // 4)

    y_ref[:, Y_CONV:Y_CONV + CONV_WIDTH] = _short_conv(
        col(C_CB, CONV_WIDTH), col(C_CC, CONV_WIDTH), col(C_CV, CONV_WIDTH), wc_ref[...]).astype(BF16)

    for j in range(GQA_HEADS // 2):
        gq_s[:, j * LANES:(j + 1) * LANES] = (
            rope64(col(C_GQ + j * LANES, LANES)) * (ATTN_SCALE * LOG2E)).astype(BF16)
    k_t, vpair = rope64(col(C_GK, LANES)).T, col(C_GV, LANES)
    kc_t, vc_pair = cgk_ref[...], cgv_ref[...].T
    for g in range(GQA_KV_HEADS):
        gk_s[2 * g], gk_s[2 * g + 1] = _key_planes(k_t, g == 0, g == 0)
        gv_s[2 * g], gv_s[2 * g + 1] = _value_planes(vpair, g == 0, g == 0)
        gkc_s[g] = jnp.concatenate(_key_planes(kc_t, g == 0, g == 0), axis=1)
        gvc_s[2 * g], gvc_s[2 * g + 1] = _value_planes(vc_pair, g == 0, g == 0)

    def gqa_block(b, c_lo, c_hi):
        q0 = pl.multiple_of(b * BAND_BLOCK, BAND_BLOCK)
        k0 = pl.multiple_of(q0 + (c_lo - WINDOW), BAND_BLOCK)
        n = c_hi - c_lo
        band = band_ref[:, c_lo:c_hi]
        row2 = lax.broadcasted_iota(jnp.int32, (2 * BAND_BLOCK, 1), 0) < BAND_BLOCK
        for g in range(GQA_KV_HEADS):
            q = jnp.concatenate([gq_s[pl.ds(q0, BAND_BLOCK), (2 * g) * LANES:(2 * g + 1) * LANES],
                                 gq_s[pl.ds(q0, BAND_BLOCK), (2 * g + 1) * LANES:(2 * g + 2) * LANES]], axis=0)
            s_ctx = _dot(q, gkc_s[g])
            terms, sinks = [], []
            for odd in range(2):
                s_loc = _dot(q, gk_s[2 * g + odd, :, pl.ds(k0, n)]) + band
                sink = jnp.where(row2, sink_ref[layer, 4 * g + odd], sink_ref[layer, 4 * g + 2 + odd]) * LOG2E
                (p_loc, p_ctx), x = _probabilities([s_loc, s_ctx[:, odd * PAST_LEN:(odd + 1) * PAST_LEN]], sink)
                terms += [(p_loc, gv_s[2 * g + odd, pl.ds(k0, n), :]), (p_ctx, gvc_s[2 * g + odd])]
                sinks.append(x)
            o = _attend(terms, sinks[0], sinks[1]).astype(BF16)
            c0 = Y_GQA + (2 * g) * LANES
            y_ref[pl.ds(q0, BAND_BLOCK), c0:c0 + LANES] = o[0:BAND_BLOCK]
            y_ref[pl.ds(q0, BAND_BLOCK), c0 + LANES:c0 + 2 * LANES] = o[BAND_BLOCK:2 * BAND_BLOCK]

    span = BAND_BLOCK + 2 * WINDOW
    nb = t // BAND_BLOCK
    gqa_block(0, WINDOW, span)
    lax.fori_loop(1, nb - 1, lambda b, c: (gqa_block(b, 0, span), c)[1], 0, unroll=2)
    gqa_block(nb - 1, 0, span - WINDOW)

    for j in range(NA_HEADS // 2):
        nq_s[:, j * LANES:(j + 1) * LANES] = (col(C_NQ + j * LANES, LANES) * (ATTN_SCALE * LOG2E)).astype(BF16)
        nk_t = col(C_NK + j * LANES, LANES).T
        for shift, k_t in enumerate((nk_t, pltpu.roll(nk_t, t - GRID_W, 1))):
            nk_s[shift, 2 * j], nk_s[shift, 2 * j + 1] = _key_planes(k_t, True, False)
        nv_s[2 * j], nv_s[2 * j + 1] = _value_planes(col(C_NV + j * LANES, LANES), True, False)
        nkc_s[j] = jnp.concatenate(_key_planes(cnk_ref[j * LANES:(j + 1) * LANES, :], True, False), axis=1)
        nvc_s[2 * j], nvc_s[2 * j + 1] = _value_planes(cnv_ref[j * LANES:(j + 1) * LANES, :].T, True, False)

    def na_row(r, carry):
        q0 = pl.multiple_of(r * GRID_W, GRID_W)
        r0 = jnp.clip(r - NA_WIN_H // 2, 0, GRID_ROWS - NA_WIN_H)
        k0 = pl.multiple_of(r0 * GRID_W, GRID_W)
        kt0 = pl.multiple_of(lax.shift_right_logical(r0, 1) * LANES, LANES)
        pat = r - r0
        for j in range(NA_HEADS // 2):
            q = nq_s[pl.ds(q0, GRID_W), j * LANES:(j + 1) * LANES]
            s_ctx = _dot(q, nkc_s[j])
            terms = []
            for odd in range(2):
                h = 2 * j + odd
                s_loc = _dot(q, nk_s[r0 & 1, h, :, pl.ds(kt0, NA_KEYS)])
                s_loc = s_loc + nab_ref[pat, :, h * NA_KEYS:(h + 1) * NA_KEYS]
                (p_loc, p_ctx), _ = _probabilities([s_loc, s_ctx[:, odd * PAST_LEN:(odd + 1) * PAST_LEN]])
                terms += [(p_loc, nv_s[h, pl.ds(k0, NA_KEYS), :]), (p_ctx, nvc_s[h])]
            y_ref[pl.ds(q0, GRID_W), Y_NA + j * LANES:Y_NA + (j + 1) * LANES] = _attend(terms).astype(BF16)
        return carry

    lax.fori_loop(0, GRID_ROWS, na_row, 0, unroll=4)

    ckv_b = _rms(col(C_MKV, MLA_KV_RANK), gkv_ref[...]).astype(BF16)
    cckv_b = cckv_ref[...].astype(BF16)
    q = _dot(_rms(col(C_MQ, MLA_Q_RANK), gq_ref[...]).astype(BF16), wuq_ref[...]) * (MLA_SCALE * LOG2E)
    mq_s[:, 0:2 * LANES] = q[:, 0:2 * LANES].astype(BF16)
    mq_s[:, 2 * LANES:3 * LANES] = rope32(q[:, 2 * LANES:3 * LANES]).astype(BF16)
    kr4_t = rope32(_repeat_rope_key(col(C_MKR, LANES))).T
    kr4_t_c = jnp.concatenate([ckr_ref[...]] * MLA_HEADS, axis=0)
    for cols, ckv_x, kr_x in ((slice(0, t), ckv_b, kr4_t), (slice(t, t + PAST_LEN), cckv_b, kr4_t_c)):
        kn_t_all = _dot_nt(wukt_ref[...], ckv_x)
        v_all = _dot(ckv_x, wuv_ref[...])
        for j in range(MLA_HEADS // 2):
            kn_t_pair = kn_t_all[j * LANES:(j + 1) * LANES, :]
            mv_s[2 * j, cols, :], mv_s[2 * j + 1, cols, :] = _value_planes(
                v_all[:, j * LANES:(j + 1) * LANES], True, False)
            for odd in range(2):
                mk_s[2 * j + odd, :, cols] = _mla_key_plane(kn_t_pair, kr_x, j, odd)

    tq = 256

    def mla_block(i, carry):
        q0 = pl.multiple_of(i * tq, tq)
        for j in range(MLA_HEADS // 2):
            qj = jnp.concatenate([mq_s[pl.ds(q0, tq), j * LANES:(j + 1) * LANES],
                                  mq_s[pl.ds(q0, tq), 2 * LANES:3 * LANES]], axis=1)
            (pe,), _ = _probabilities([_dot(qj, mk_s[2 * j])])
            (po,), _ = _probabilities([_dot(qj, mk_s[2 * j + 1])])
            o = _attend([(pe, mv_s[2 * j]), (po, mv_s[2 * j + 1])])
            y_ref[pl.ds(q0, tq), Y_MLA + j * LANES:Y_MLA + (j + 1) * LANES] = o.astype(BF16)
        return carry

    lax.fori_loop(0, t // tq, mla_block, 0, unroll=True)


def _rope_tables(group, half):
    tok = np.arange(DEC_SEQ)
    pos = np.stack([tok // GRID_W, tok % GRID_W], axis=1).astype(np.float64)
    inv = ROPE_BASE ** (-np.arange(half, dtype=np.float64) / half)
    lane = np.arange(LANES) % group
    axis = lane // (2 * half)
    within = lane % (2 * half)
    ang = pos[:, axis] * inv[within % half][None, :]
    cos, sin = np.cos(ang), np.sin(ang)
    upper = (within >= half)[None, :]
    sin_hi = np.where(upper, sin, 0.0)
    sin_lo = np.where(upper, 0.0, -sin)
    return tuple(jnp.asarray(a, dtype=F32) for a in (cos, sin_hi, sin_lo))


def _band_mask():
    i = np.arange(BAND_BLOCK)[:, None]
    c = np.arange(BAND_BLOCK + 2 * WINDOW)[None, :]
    ok = (c >= i) & (c <= i + 2 * WINDOW)
    m = np.where(ok, 0.0, NEG_INF)
    return jnp.asarray(np.concatenate([m, m], axis=0), dtype=F32)


def _lat_mixer_call(proj, sink, w_conv, g_q, w_uq, g_kv, w_uk_t, w_uv, nab, caches, layer, name):
    n, t = DEC_BATCH, DEC_SEQ
    one = pl.Buffered(1)
    const = lambda shape: pl.BlockSpec(shape, lambda s: (0,) * len(shape), pipeline_mode=one)
    cache = lambda shape: pl.BlockSpec((None, None) + shape, lambda s: (s, layer, 0, 0))
    rope_g = _rope_tables(HEAD_DIM, HEAD_DIM // 4)
    rope_m = _rope_tables(MLA_ROPE, MLA_ROPE // 4)
    span = BAND_BLOCK + 2 * WINDOW
    keys = t + PAST_LEN
    scratch = [
        pltpu.VMEM((t, GQA_HEADS // 2 * LANES), BF16),
        pltpu.VMEM((2 * GQA_KV_HEADS, LANES, t), BF16),
        pltpu.VMEM((2 * GQA_KV_HEADS, t, 2 * LANES), BF16),
        pltpu.VMEM((GQA_KV_HEADS, LANES, 2 * PAST_LEN), BF16),
        pltpu.VMEM((2 * GQA_KV_HEADS, PAST_LEN, 2 * LANES), BF16),
        pltpu.VMEM((t, NA_HEADS // 2 * LANES), BF16),
        pltpu.VMEM((2, NA_HEADS, LANES, t), BF16),
        pltpu.VMEM((NA_HEADS, t, 2 * LANES), BF16),
        pltpu.VMEM((NA_HEADS // 2, LANES, 2 * PAST_LEN), BF16),
        pltpu.VMEM((NA_HEADS, PAST_LEN, 2 * LANES), BF16),
        pltpu.VMEM((t, 3 * LANES), BF16),
        pltpu.VMEM((MLA_HEADS, 2 * LANES, keys), BF16),
        pltpu.VMEM((MLA_HEADS, keys, 2 * LANES), BF16),
    ]
    return pl.pallas_call(
        functools.partial(_lat_mixer_kernel, layer=layer),
        grid=(n,),
        in_specs=[
            pl.BlockSpec(memory_space=pltpu.SMEM),
            pl.BlockSpec((None, t, MAIN_COLS), lambda s: (s, 0, 0), pipeline_mode=one),
            _layer_spec((CONV_K, CONV_WIDTH), layer, 1),
            _layer_spec((1, MLA_Q_RANK), layer, 1),
            _layer_spec((MLA_Q_RANK, 3 * LANES), layer, 1),
            _layer_spec((1, MLA_KV_RANK), layer, 1),
            _layer_spec((2 * LANES, MLA_KV_RANK), layer, 1),
            _layer_spec((MLA_KV_RANK, 2 * LANES), layer, 1),
        ] + [const((t, LANES))] * 6 + [
            const((2 * BAND_BLOCK, span)),
            _layer_spec((NA_PATTERNS, GRID_W, NA_HEADS * NA_KEYS), layer, 1),
            cache((2 * HEAD_DIM, PAST_LEN)), cache((2 * HEAD_DIM, PAST_LEN)),
            cache((4 * HEAD_DIM, PAST_LEN)), cache((4 * HEAD_DIM, PAST_LEN)),
            cache((PAST_LEN, MLA_KV_RANK)), cache((MLA_ROPE, PAST_LEN)),
        ],
        out_specs=pl.BlockSpec((None, t, Y_COLS), lambda s: (s, 0, 0)),
        out_shape=jax.ShapeDtypeStruct((n, t, Y_COLS), BF16),
        scratch_shapes=scratch,
        compiler_params=pltpu.CompilerParams(
            dimension_semantics=("arbitrary",), vmem_limit_bytes=VMEM_LIMIT),
        name=name,
    )(sink, proj, w_conv, g_q, w_uq, g_kv, w_uk_t, w_uv, *rope_g, *rope_m, _band_mask(), nab, *caches)


def _post_kernel(x_ref, y_ref, mod_ref, ga_ref, gm_ref, gf_ref, wg_ref, wb_ref, wo_ref, w1_ref, w2_ref,
                 o_ref, *, final):
    d = D_MODEL
    mod = lambda i: mod_ref[:, i * d:(i + 1) * d]
    x = x_ref[...]
    h = (_rms(x, ga_ref[...]) * (1.0 + mod(1)) + mod(0)).astype(BF16)
    gates = jax.nn.sigmoid(_dot(h, wg_ref[...]))
    bounds = (Y_CONV, Y_GQA, Y_NA, Y_MLA, Y_COLS)
    merged = None
    for i in range(N_BRANCH):
        lo, hi = bounds[i], bounds[i + 1]
        term = gates[:, i * d:(i + 1) * d] * _dot(y_ref[:, lo:hi], wb_ref[lo:hi, :])
        merged = term if merged is None else merged + term
    x = x + mod(2) * _dot(merged.astype(BF16), wo_ref[...])
    h = (_rms(x, gm_ref[...]) * (1.0 + mod(4)) + mod(3)).astype(BF16)
    f = jnp.square(jnp.maximum(_dot(h, w1_ref[...]), 0.0)).astype(BF16)
    x = x + mod(5) * _dot(f, w2_ref[...])
    o_ref[...] = _rms(x, gf_ref[...]) if final else x


def _post_call(x, y, mods, g_attn, g_mlp, g_final, w_gates, w_br, w_o, w_ff1, w_ff2, layer, ctx, name):
    n, t, _ = x.shape
    tm = 256
    return pl.pallas_call(
        functools.partial(_post_kernel, final=layer == DEPTH - 1),
        grid=(n, t // tm),
        in_specs=[
            pl.BlockSpec((None, tm, D_MODEL), lambda s, i: (s, i, 0)),
            pl.BlockSpec((None, tm, Y_COLS), lambda s, i: (s, i, 0)),
            _mod_spec(layer, ctx, 2),
            _layer_spec((1, D_MODEL), layer, 2),
            _layer_spec((1, D_MODEL), layer, 2),
            pl.BlockSpec((1, D_MODEL), lambda s, i: (0, 0)),
            _layer_spec((D_MODEL, N_BRANCH * D_MODEL), layer, 2),
            _layer_spec((Y_COLS, D_MODEL), layer, 2),
            _layer_spec((D_MODEL, D_MODEL), layer, 2),
            _layer_spec((D_MODEL, D_FF), layer, 2),
            _layer_spec((D_FF, D_MODEL), layer, 2),
        ],
        out_specs=pl.BlockSpec((None, tm, D_MODEL), lambda s, i: (s, i, 0)),
        out_shape=jax.ShapeDtypeStruct((n, t, D_MODEL), F32),
        compiler_params=pltpu.CompilerParams(
            dimension_semantics=("arbitrary", "arbitrary"), vmem_limit_bytes=VMEM_LIMIT),
        name=name,
    )(x, y, mods, g_attn, g_mlp, g_final, w_gates, w_br, w_o, w_ff1, w_ff2)


def kernel(x_prompt, x_sample, cache_gqa_k, cache_gqa_v, cache_na_k, cache_na_v, cache_mla_ckv, cache_mla_krope, c, c_ctx, w_mod, b_mod, g_attn, g_mlp, w_in, w_conv, gqa_sink, na_rpb, mla_g_q, mla_w_uq, mla_g_kv, mla_w_ukv, w_branch_conv, w_branch_gqa, w_branch_na, w_branch_mla, w_o, w_ff1, w_ff2, g_final):
    w_main = w_in[:, :, :MAIN_COLS].astype(BF16)
    w_gates = w_in[:, :, GATE_COL0:].astype(BF16)
    uq = mla_w_uq.reshape(DEPTH, MLA_Q_RANK, MLA_HEADS, MLA_NOPE + MLA_ROPE)
    w_uq = jnp.concatenate([uq[..., :MLA_NOPE].reshape(DEPTH, MLA_Q_RANK, -1),
                            uq[..., MLA_NOPE:].reshape(DEPTH, MLA_Q_RANK, -1)], axis=-1).astype(BF16)
    ukv = mla_w_ukv.reshape(DEPTH, MLA_KV_RANK, MLA_HEADS, MLA_NOPE + MLA_V)
    w_uk_t = jnp.transpose(ukv[..., :MLA_NOPE].reshape(DEPTH, MLA_KV_RANK, -1), (0, 2, 1)).astype(BF16)
    w_uv = ukv[..., MLA_NOPE:].reshape(DEPTH, MLA_KV_RANK, -1).astype(BF16)
    w_br = jnp.concatenate([w_branch_conv, w_branch_gqa, w_branch_na, w_branch_mla], axis=1).astype(BF16)
    w_o_b, w_ff1_b, w_ff2_b = w_o.astype(BF16), w_ff1.astype(BF16), w_ff2.astype(BF16)
    g_a, g_m = g_attn[:, None, :], g_mlp[:, None, :]
    g_q, g_kv, g_f = mla_g_q[:, None, :], mla_g_kv[:, None, :], g_final[None, :]

    c16 = jnp.concatenate([c, c_ctx[None, :], jnp.zeros((MOD_ROWS - DEC_BATCH - 1, D_MODEL), F32)], axis=0)
    mods = _mod_call(c16, w_mod, b_mod).reshape(DEPTH, MOD_ROWS, 1, 6 * D_MODEL)
    nab = _nab_call(na_rpb)

    heads_t = lambda a: jnp.transpose(a, (0, 1, 3, 4, 2)).reshape(a.shape[0], DEPTH, -1, a.shape[2])
    caches = (heads_t(cache_gqa_k), heads_t(cache_gqa_v), heads_t(cache_na_k), heads_t(cache_na_v),
              cache_mla_ckv, jnp.transpose(cache_mla_krope, (0, 1, 3, 2)))

    h_ctx, h_lat = x_prompt, x_sample
    states = None
    for l in range(DEPTH):
        mixer_w = (gqa_sink, w_conv, g_q, w_uq, g_kv, w_uk_t, w_uv)
        post_w = (g_a, g_m, g_f, w_gates, w_br, w_o_b, w_ff1_b, w_ff2_b)

        proj = _pre_call(h_ctx, mods, g_a, w_main, l, True, f"pre_ctx_{l}")
        y, states = _ctx_mixer_call(proj, *mixer_w, states, l, f"mixer_ctx_{l}")
        h_ctx = _post_call(h_ctx, y, mods, *post_w, l, True, f"post_ctx_{l}")

        proj = _pre_call(h_lat, mods, g_a, w_main, l, False, f"pre_lat_{l}")
        y = _lat_mixer_call(proj, *mixer_w, nab, caches, l, f"mixer_lat_{l}")
        h_lat = _post_call(h_lat, y, mods, *post_w, l, False, f"post_lat_{l}")

    def heads_out(a, heads):
        return jnp.transpose(a.reshape(BATCH, DEPTH, heads, HEAD_DIM, SEQ), (0, 1, 4, 2, 3))

    kg, vg, kn, vn, ckv, kr = states
    return (h_ctx, h_lat, heads_out(kg, GQA_KV_HEADS), heads_out(vg, GQA_KV_HEADS),
            heads_out(kn, NA_HEADS), heads_out(vn, NA_HEADS), ckv, jnp.transpose(kr, (0, 1, 3, 2)))
```

```python
import functools
import math

import numpy as np
import jax
import jax.numpy as jnp
from jax import lax
from jax.experimental import pallas as pl
from jax.experimental.pallas import tpu as pltpu

D_MODEL = 1024
BATCH = 32
SEQ = 256
DEPTH = 2
DEC_BATCH = 8
DEC_SEQ = 1024
PAST_LEN = 512
GRID_W = 64
GRID_ROWS = DEC_SEQ // GRID_W
HEAD_DIM = 64
CONV_WIDTH = 256
CONV_K = 3
GQA_HEADS = 8
GQA_KV_HEADS = 2
WINDOW = 128
BAND_BLOCK = 128
NA_HEADS = 4
NA_WIN_H = 8
NA_WIN_W = 16
MLA_HEADS = 4
MLA_Q_RANK = 256
MLA_KV_RANK = 128
MLA_NOPE = 64
MLA_ROPE = 32
MLA_V = 64
D_FF = 4 * D_MODEL
N_BRANCH = 4
ROPE_BASE = 10000.0
EPS = 1e-6
NEG_INF = -1e30
LOG2E = math.log2(math.e)
ATTN_SCALE = HEAD_DIM ** -0.5
MLA_SCALE = (MLA_NOPE + MLA_ROPE) ** -0.5

LANES = 128
MOD_ROWS = 16
CTX_MOD_ROW = DEC_BATCH

C_CB, C_CC, C_CV = 0, 256, 512
C_GQ, C_GK, C_GV = 768, 1280, 1408
C_NQ, C_NK, C_NV = 1536, 1792, 2048
C_MQ, C_MKV, C_MKR = 2304, 2560, 2688
MAIN_COLS = 2816
GATE_COL0 = 2720
Y_CONV, Y_GQA, Y_NA, Y_MLA = 0, 256, 768, 1024
Y_COLS = 1280
NA_PATTERNS = 8
NA_KEYS = NA_WIN_H * GRID_W

VMEM_LIMIT = 56 * 1024 * 1024

F32 = jnp.float32
BF16 = jnp.bfloat16


def _dot(a, b):
    return jnp.dot(a, b, preferred_element_type=F32)


def _dot_nt(a, b):
    return lax.dot_general(a, b, (((1,), (1,)), ((), ())), preferred_element_type=F32)


def _rms(x, g):
    return x * lax.rsqrt(jnp.mean(x * x, axis=-1, keepdims=True) + EPS) * g


def _lane_lt(shape, n):
    return lax.broadcasted_iota(jnp.int32, shape, len(shape) - 1) < n


def _row_lt(shape, n):
    return lax.broadcasted_iota(jnp.int32, shape, 0) < n


def _row_group(shape, lo, hi):
    row = lax.broadcasted_iota(jnp.int32, shape, 0)
    return (row >= lo) & (row < hi)


def _short_conv(cb, cc, cv, w):
    u = cc * cv
    t = u.shape[0]
    row = lax.broadcasted_iota(jnp.int32, u.shape, 0)
    prev = jnp.where(row == 0, 0.0, pltpu.roll(u, 1, 0))
    nxt = jnp.where(row == t - 1, 0.0, pltpu.roll(u, t - 1, 0))
    return cb * (prev * w[0:1, :] + u * w[1:2, :] + nxt * w[2:3, :])


def _rope(x, cos, sin_hi, sin_lo, half):
    n = x.shape[-1]
    return x * cos + pltpu.roll(x, n - half, 1) * sin_lo + pltpu.roll(x, half, 1) * sin_hi


def _repeat_rope_key(tile):
    k = jnp.where(_lane_lt(tile.shape, MLA_ROPE), tile, 0.0)
    k = k + pltpu.roll(k, MLA_ROPE, 1)
    return k + pltpu.roll(k, 2 * MLA_ROPE, 1)


def _mod_kernel(c_ref, w_ref, b_ref, o_ref):
    c = c_ref[...]
    s = c * jax.nn.sigmoid(c)
    o_ref[...] = _dot(s.astype(BF16), w_ref[...].astype(BF16)) + b_ref[...]


def _mod_call(c16, w_mod, b_mod):
    tn = 1536
    return pl.pallas_call(
        _mod_kernel,
        grid=(DEPTH, 6 * D_MODEL // tn),
        in_specs=[
            pl.BlockSpec((MOD_ROWS, D_MODEL), lambda l, j: (0, 0)),
            pl.BlockSpec((None, D_MODEL, tn), lambda l, j: (l, 0, j)),
            pl.BlockSpec((None, 1, tn), lambda l, j: (l, 0, j)),
        ],
        out_specs=pl.BlockSpec((None, MOD_ROWS, tn), lambda l, j: (l, 0, j)),
        out_shape=jax.ShapeDtypeStruct((DEPTH, MOD_ROWS, 6 * D_MODEL), F32),
        compiler_params=pltpu.CompilerParams(
            dimension_semantics=("arbitrary", "arbitrary"), vmem_limit_bytes=VMEM_LIMIT),
        name="adaln_mod",
    )(c16, w_mod, b_mod.reshape(DEPTH, 1, 6 * D_MODEL))


def _nab_kernel(rpb_ref, o_ref):
    l = pl.program_id(0)
    c = lax.broadcasted_iota(jnp.int32, (GRID_W, GRID_W), 0)
    w = lax.broadcasted_iota(jnp.int32, (GRID_W, GRID_W), 1)
    dc = w - c + (NA_WIN_W - 1)
    c0 = jnp.clip(c - NA_WIN_W // 2, 0, GRID_W - NA_WIN_W)
    outside = (w < c0) | (w >= c0 + NA_WIN_W)
    n_dr, n_dc = 2 * NA_WIN_H - 1, 2 * NA_WIN_W - 1
    for h in range(NA_HEADS):
        for d in range(n_dr):
            base = ((l * NA_HEADS + h) * n_dr + d) * n_dc
            tile = jnp.full((GRID_W, GRID_W), NEG_INF, F32)
            for j in range(n_dc):
                tile = jnp.where(dc == j, rpb_ref[base + j] * LOG2E, tile)
            tile = jnp.where(outside, NEG_INF, tile)
            for p in range(NA_PATTERNS):
                a = d - (NA_WIN_H - 1) + p
                if 0 <= a < NA_WIN_H:
                    col = h * NA_KEYS + a * GRID_W
                    o_ref[p, :, col:col + GRID_W] = tile


def _nab_call(na_rpb):
    return pl.pallas_call(
        _nab_kernel,
        grid=(DEPTH,),
        in_specs=[pl.BlockSpec(memory_space=pltpu.SMEM)],
        out_specs=pl.BlockSpec((None, NA_PATTERNS, GRID_W, NA_HEADS * NA_KEYS), lambda l: (l, 0, 0, 0)),
        out_shape=jax.ShapeDtypeStruct((DEPTH, NA_PATTERNS, GRID_W, NA_HEADS * NA_KEYS), F32),
        compiler_params=pltpu.CompilerParams(dimension_semantics=("arbitrary",)),
        name="na_bias_tables",
    )(na_rpb.reshape(-1))


def _pre_kernel(x_ref, mod_ref, g_ref, w_ref, o_ref):
    h = _rms(x_ref[...], g_ref[...]) * (1.0 + mod_ref[:, D_MODEL:2 * D_MODEL]) + mod_ref[:, 0:D_MODEL]
    o_ref[...] = _dot(h.astype(BF16), w_ref[...])


def _mod_spec(layer, ctx, n_grid):
    if n_grid == 1:
        index = (lambda s: (layer, CTX_MOD_ROW, 0, 0)) if ctx else (lambda s: (layer, s, 0, 0))
    else:
        index = (lambda s, i: (layer, CTX_MOD_ROW, 0, 0)) if ctx else (lambda s, i: (layer, s, 0, 0))
    return pl.BlockSpec((None, None, 1, 6 * D_MODEL), index)


def _layer_spec(shape, layer, n_grid):
    zeros = (0,) * len(shape)
    index = (lambda s: (layer,) + zeros) if n_grid == 1 else (lambda s, i: (layer,) + zeros)
    return pl.BlockSpec((None,) + tuple(shape), index, pipeline_mode=pl.Buffered(1))


def _pre_call(x, mods, g, w_main, layer, ctx, name):
    n, t, _ = x.shape
    tm = min(t, 512)
    return pl.pallas_call(
        _pre_kernel,
        grid=(n, t // tm),
        in_specs=[
            pl.BlockSpec((None, tm, D_MODEL), lambda s, i: (s, i, 0)),
            _mod_spec(layer, ctx, 2),
            _layer_spec((1, D_MODEL), layer, 2),
            _layer_spec((D_MODEL, MAIN_COLS), layer, 2),
        ],
        out_specs=pl.BlockSpec((None, tm, MAIN_COLS), lambda s, i: (s, i, 0)),
        out_shape=jax.ShapeDtypeStruct((n, t, MAIN_COLS), F32),
        compiler_params=pltpu.CompilerParams(
            dimension_semantics=("arbitrary", "arbitrary"), vmem_limit_bytes=VMEM_LIMIT),
        name=name,
    )(x, mods, g, w_main)


def _key_planes(k_t, even_first, odd_first):
    lo = _row_lt(k_t.shape, HEAD_DIM)
    swapped = None
    if not even_first or odd_first:
        swapped = pltpu.roll(k_t, HEAD_DIM, 0)
    top = jnp.where(lo, k_t if even_first else swapped, 0.0)
    bot = jnp.where(lo, 0.0, swapped if odd_first else k_t)
    return top.astype(BF16), bot.astype(BF16)


def _value_planes(pair, even_first, odd_first):
    lo = _lane_lt(pair.shape, HEAD_DIM)
    swapped = None
    if not even_first or odd_first:
        swapped = pltpu.roll(pair, HEAD_DIM, 1)
    top = jnp.where(lo, pair if even_first else swapped, 0.0)
    bot = jnp.where(lo, 0.0, swapped if odd_first else pair)
    one_e = jnp.where(lo, 1.0, 0.0)
    return (jnp.concatenate([top, one_e], axis=1).astype(BF16),
            jnp.concatenate([bot, 1.0 - one_e], axis=1).astype(BF16))


def _probabilities(parts, sink=None):
    m = parts[0].max(axis=-1, keepdims=True)
    for p in parts[1:]:
        m = jnp.maximum(m, p.max(axis=-1, keepdims=True))
    if sink is not None:
        m = jnp.maximum(m, sink)
    probs = [jnp.exp2((p - m).astype(BF16)) for p in parts]
    return probs, (None if sink is None else jnp.exp2(sink - m))


def _attend(terms, sink_e=None, sink_o=None):
    o = None
    for p, v in terms:
        t = _dot(p, v)
        o = t if o is None else o + t
    den = o[:, LANES:]
    if sink_e is not None:
        den = den + jnp.where(_lane_lt(den.shape, HEAD_DIM), sink_e, sink_o)
    return o[:, :LANES] / den


def _mla_key_plane(kn_t_pair, kr4_t, j, odd):
    h = 2 * j + odd
    lo = _row_lt(kn_t_pair.shape, HEAD_DIM)
    nope = jnp.where(lo, 0.0, kn_t_pair) if odd else jnp.where(lo, kn_t_pair, 0.0)
    rope = jnp.where(_row_group(kr4_t.shape, h * MLA_ROPE, (h + 1) * MLA_ROPE), kr4_t, 0.0)
    return jnp.concatenate([nope, rope], axis=0).astype(BF16)


def _ctx_mixer_kernel(sink_ref, p_ref, wc_ref, gq_ref, wuq_ref, gkv_ref, wukt_ref, wuv_ref, *rest, layer, first):
    y_ref, kg_ref, vg_ref, kn_ref, vn_ref, ckv_ref, kr_ref = rest[-7:]
    t = SEQ
    col = lambda c, n: p_ref[:, c:c + n]

    def put_state(ref, val):
        if first:
            for d in range(DEPTH):
                ref[d] = val if d == layer else jnp.zeros_like(val)
        else:
            ref[...] = val

    kg_t = col(C_GK, LANES).T
    kn_t = col(C_NK, 2 * LANES).T
    put_state(kg_ref, kg_t)
    put_state(vg_ref, col(C_GV, LANES).T)
    put_state(kn_ref, kn_t)
    put_state(vn_ref, col(C_NV, 2 * LANES).T)
    ckv = _rms(col(C_MKV, MLA_KV_RANK), gkv_ref[...])
    put_state(ckv_ref, ckv)
    kr_t = col(C_MKR, LANES).T[0:MLA_ROPE, :]
    put_state(kr_ref, kr_t)

    y_ref[:, Y_CONV:Y_CONV + CONV_WIDTH] = _short_conv(
        col(C_CB, CONV_WIDTH), col(C_CC, CONV_WIDTH), col(C_CV, CONV_WIDTH), wc_ref[...]).astype(BF16)

    vpair = col(C_GV, LANES)
    row2 = lax.broadcasted_iota(jnp.int32, (2 * t, 1), 0) < t
    for g in range(GQA_KV_HEADS):
        ke, ko = _key_planes(kg_t, g == 0, g == 0)
        ve, vo = _value_planes(vpair, g == 0, g == 0)
        q = jnp.concatenate([col(C_GQ + (2 * g) * LANES, LANES), col(C_GQ + (2 * g + 1) * LANES, LANES)], axis=0)
        q = (q * (ATTN_SCALE * LOG2E)).astype(BF16)
        s = _dot(q, jnp.concatenate([ke, ko], axis=1))
        sink = [jnp.where(row2, sink_ref[layer, 4 * g + odd], sink_ref[layer, 4 * g + 2 + odd]) * LOG2E
                for odd in range(2)]
        (pe,), xe = _probabilities([s[:, 0:t]], sink[0])
        (po,), xo = _probabilities([s[:, t:2 * t]], sink[1])
        o = _attend([(pe, ve), (po, vo)], xe, xo).astype(BF16)
        c0 = Y_GQA + (2 * g) * LANES
        y_ref[:, c0:c0 + LANES] = o[0:t]
        y_ref[:, c0 + LANES:c0 + 2 * LANES] = o[t:2 * t]

    for j in range(NA_HEADS // 2):
        ke, ko = _key_planes(kn_t[j * LANES:(j + 1) * LANES, :], True, False)
        ve, vo = _value_planes(col(C_NV + j * LANES, LANES), True, False)
        q = (col(C_NQ + j * LANES, LANES) * (ATTN_SCALE * LOG2E)).astype(BF16)
        s = _dot(q, jnp.concatenate([ke, ko], axis=1))
        (pe,), _ = _probabilities([s[:, 0:t]])
        (po,), _ = _probabilities([s[:, t:2 * t]])
        y_ref[:, Y_NA + j * LANES:Y_NA + (j + 1) * LANES] = _attend([(pe, ve), (po, vo)]).astype(BF16)

    ckv_b = ckv.astype(BF16)
    q = _dot(_rms(col(C_MQ, MLA_Q_RANK), gq_ref[...]).astype(BF16), wuq_ref[...]) * (MLA_SCALE * LOG2E)
    kn_t_all = _dot_nt(wukt_ref[...], ckv_b)
    v_all = _dot(ckv_b, wuv_ref[...])
    kr4_t = jnp.concatenate([kr_t] * MLA_HEADS, axis=0)
    q_rope = q[:, 2 * LANES:3 * LANES]
    for j in range(MLA_HEADS // 2):
        qj = jnp.concatenate([q[:, j * LANES:(j + 1) * LANES], q_rope], axis=1).astype(BF16)
        kn_t_pair = kn_t_all[j * LANES:(j + 1) * LANES, :]
        keys = jnp.concatenate([_mla_key_plane(kn_t_pair, kr4_t, j, 0), _mla_key_plane(kn_t_pair, kr4_t, j, 1)], axis=1)
        s = _dot(qj, keys)
        ve, vo = _value_planes(v_all[:, j * LANES:(j + 1) * LANES], True, False)
        (pe,), _ = _probabilities([s[:, 0:t]])
        (po,), _ = _probabilities([s[:, t:2 * t]])
        y_ref[:, Y_MLA + j * LANES:Y_MLA + (j + 1) * LANES] = _attend([(pe, ve), (po, vo)]).astype(BF16)


def _ctx_mixer_call(proj, sink, w_conv, g_q, w_uq, g_kv, w_uk_t, w_uv, prev_states, layer, name):
    n = BATCH
    state_shapes = ((2 * HEAD_DIM, SEQ), (2 * HEAD_DIM, SEQ), (4 * HEAD_DIM, SEQ), (4 * HEAD_DIM, SEQ),
                    (SEQ, MLA_KV_RANK), (MLA_ROPE, SEQ))
    first = prev_states is None
    if first:
        state_spec = lambda shape: pl.BlockSpec((None, DEPTH) + shape, lambda s: (s, 0, 0, 0))
    else:
        state_spec = lambda shape: pl.BlockSpec((None, None) + shape, lambda s: (s, layer, 0, 0))
    in_specs = [
        pl.BlockSpec(memory_space=pltpu.SMEM),
        pl.BlockSpec((None, SEQ, MAIN_COLS), lambda s: (s, 0, 0)),
        _layer_spec((CONV_K, CONV_WIDTH), layer, 1),
        _layer_spec((1, MLA_Q_RANK), layer, 1),
        _layer_spec((MLA_Q_RANK, 3 * LANES), layer, 1),
        _layer_spec((1, MLA_KV_RANK), layer, 1),
        _layer_spec((2 * LANES, MLA_KV_RANK), layer, 1),
        _layer_spec((MLA_KV_RANK, 2 * LANES), layer, 1),
    ]
    args = [sink, proj, w_conv, g_q, w_uq, g_kv, w_uk_t, w_uv]
    aliases = {}
    if not first:
        for i, st in enumerate(prev_states):
            aliases[len(args)] = 1 + i
            in_specs.append(pl.BlockSpec(memory_space=pl.ANY))
            args.append(st)
    outs = pl.pallas_call(
        functools.partial(_ctx_mixer_kernel, layer=layer, first=first),
        grid=(n,),
        in_specs=in_specs,
        out_specs=[pl.BlockSpec((None, SEQ, Y_COLS), lambda s: (s, 0, 0))] + [state_spec(s) for s in state_shapes],
        out_shape=[jax.ShapeDtypeStruct((n, SEQ, Y_COLS), BF16)]
        + [jax.ShapeDtypeStruct((n, DEPTH) + s, F32) for s in state_shapes],
        input_output_aliases=aliases,
        compiler_params=pltpu.CompilerParams(
            dimension_semantics=("arbitrary",), vmem_limit_bytes=VMEM_LIMIT),
        name=name,
    )(*args)
    return outs[0], outs[1:]


def _lat_mixer_kernel(sink_ref, p_ref, wc_ref, gq_ref, wuq_ref, gkv_ref, wukt_ref, wuv_ref,
                      rc_ref, rsh_ref, rsl_ref, mc_ref, msh_ref, msl_ref, band_ref, nab_ref,
                      cgk_ref, cgv_ref, cnk_ref, cnv_ref, cckv_ref, ckr_ref,
                      y_ref,
                      gq_s, gk_s, gv_s, gkc_s, gvc_s, nq_s, nk_s, nv_s, nkc_s, nvc_s, mq_s, mk_s, mv_s, *, layer):
    t = DEC_SEQ
    col = lambda c, n: p_ref[:, c:c + n]
    rope64 = lambda x: _rope(x, rc_ref[...], rsh_ref[...], rsl_ref[...], HEAD_DIM // 4)
    rope32 = lambda x: _rope(x, mc_ref[...], msh_ref[...], msl_ref[...], MLA_ROPE // 4)

    y_ref[:, Y_CONV:Y_CONV + CONV_WIDTH] = _short_conv(
        col(C_CB, CONV_WIDTH), col(C_CC, CONV_WIDTH), col(C_CV, CONV_WIDTH), wc_ref[...]).astype(BF16)

    for j in range(GQA_HEADS // 2):
        gq_s[:, j * LANES:(j + 1) * LANES] = (
            rope64(col(C_GQ + j * LANES, LANES)) * (ATTN_SCALE * LOG2E)).astype(BF16)
    k_t, vpair = rope64(col(C_GK, LANES)).T, col(C_GV, LANES)
    kc_t, vc_pair = cgk_ref[...], cgv_ref[...].T
    for g in range(GQA_KV_HEADS):
        gk_s[2 * g], gk_s[2 * g + 1] = _key_planes(k_t, g == 0, g == 0)
        gv_s[2 * g], gv_s[2 * g + 1] = _value_planes(vpair, g == 0, g == 0)
        gkc_s[g] = jnp.concatenate(_key_planes(kc_t, g == 0, g == 0), axis=1)
        gvc_s[2 * g], gvc_s[2 * g + 1] = _value_planes(vc_pair, g == 0, g == 0)

    def gqa_block(b, c_lo, c_hi):
        q0 = pl.multiple_of(b * BAND_BLOCK, BAND_BLOCK)
        k0 = pl.multiple_of(q0 + (c_lo - WINDOW), BAND_BLOCK)
        n = c_hi - c_lo
        band = band_ref[:, c_lo:c_hi]
        row2 = lax.broadcasted_iota(jnp.int32, (2 * BAND_BLOCK, 1), 0) < BAND_BLOCK
        for g in range(GQA_KV_HEADS):
            q = jnp.concatenate([gq_s[pl.ds(q0, BAND_BLOCK), (2 * g) * LANES:(2 * g + 1) * LANES],
                                 gq_s[pl.ds(q0, BAND_BLOCK), (2 * g + 1) * LANES:(2 * g + 2) * LANES]], axis=0)
            s_ctx = _dot(q, gkc_s[g])
            terms, sinks = [], []
            for odd in range(2):
                s_loc = _dot(q, gk_s[2 * g + odd, :, pl.ds(k0, n)]) + band
                sink = jnp.where(row2, sink_ref[layer, 4 * g + odd], sink_ref[layer, 4 * g + 2 + odd]) * LOG2E
                (p_loc, p_ctx), x = _probabilities([s_loc, s_ctx[:, odd * PAST_LEN:(odd + 1) * PAST_LEN]], sink)
                terms += [(p_loc, gv_s[2 * g + odd, pl.ds(k0, n), :]), (p_ctx, gvc_s[2 * g + odd])]
                sinks.append(x)
            o = _attend(terms, sinks[0], sinks[1]).astype(BF16)
            c0 = Y_GQA + (2 * g) * LANES
            y_ref[pl.ds(q0, BAND_BLOCK), c0:c0 + LANES] = o[0:BAND_BLOCK]
            y_ref[pl.ds(q0, BAND_BLOCK), c0 + LANES:c0 + 2 * LANES] = o[BAND_BLOCK:2 * BAND_BLOCK]

    span = BAND_BLOCK + 2 * WINDOW
    nb = t // BAND_BLOCK
    gqa_block(0, WINDOW, span)
    lax.fori_loop(1, nb - 1, lambda b, c: (gqa_block(b, 0, span), c)[1], 0, unroll=True)
    gqa_block(nb - 1, 0, span - WINDOW)

    for j in range(NA_HEADS // 2):
        nq_s[:, j * LANES:(j + 1) * LANES] = (col(C_NQ + j * LANES, LANES) * (ATTN_SCALE * LOG2E)).astype(BF16)
        nk_t = col(C_NK + j * LANES, LANES).T
        for shift, k_t in enumerate((nk_t, pltpu.roll(nk_t, t - GRID_W, 1))):
            nk_s[shift, 2 * j], nk_s[shift, 2 * j + 1] = _key_planes(k_t, True, False)
        nv_s[2 * j], nv_s[2 * j + 1] = _value_planes(col(C_NV + j * LANES, LANES), True, False)
        nkc_s[j] = jnp.concatenate(_key_planes(cnk_ref[j * LANES:(j + 1) * LANES, :], True, False), axis=1)
        nvc_s[2 * j], nvc_s[2 * j + 1] = _value_planes(cnv_ref[j * LANES:(j + 1) * LANES, :].T, True, False)

    def na_row(r, carry):
        q0 = pl.multiple_of(r * GRID_W, GRID_W)
        r0 = jnp.clip(r - NA_WIN_H // 2, 0, GRID_ROWS - NA_WIN_H)
        k0 = pl.multiple_of(r0 * GRID_W, GRID_W)
        kt0 = pl.multiple_of(lax.shift_right_logical(r0, 1) * LANES, LANES)
        pat = r - r0
        for j in range(NA_HEADS // 2):
            q = nq_s[pl.ds(q0, GRID_W), j * LANES:(j + 1) * LANES]
            s_ctx = _dot(q, nkc_s[j])
            terms = []
            for odd in range(2):
                h = 2 * j + odd
                s_loc = _dot(q, nk_s[r0 & 1, h, :, pl.ds(kt0, NA_KEYS)])
                s_loc = s_loc + nab_ref[pat, :, h * NA_KEYS:(h + 1) * NA_KEYS]
                (p_loc, p_ctx), _ = _probabilities([s_loc, s_ctx[:, odd * PAST_LEN:(odd + 1) * PAST_LEN]])
                terms += [(p_loc, nv_s[h, pl.ds(k0, NA_KEYS), :]), (p_ctx, nvc_s[h])]
            y_ref[pl.ds(q0, GRID_W), Y_NA + j * LANES:Y_NA + (j + 1) * LANES] = _attend(terms).astype(BF16)
        return carry

    lax.fori_loop(0, GRID_ROWS, na_row, 0, unroll=True)

    ckv_b = _rms(col(C_MKV, MLA_KV_RANK), gkv_ref[...]).astype(BF16)
    cckv_b = cckv_ref[...].astype(BF16)
    q = _dot(_rms(col(C_MQ, MLA_Q_RANK), gq_ref[...]).astype(BF16), wuq_ref[...]) * (MLA_SCALE * LOG2E)
    mq_s[:, 0:2 * LANES] = q[:, 0:2 * LANES].astype(BF16)
    mq_s[:, 2 * LANES:3 * LANES] = rope32(q[:, 2 * LANES:3 * LANES]).astype(BF16)
    kr4_t = rope32(_repeat_rope_key(col(C_MKR, LANES))).T
    kr4_t_c = jnp.concatenate([ckr_ref[...]] * MLA_HEADS, axis=0)
    for cols, ckv_x, kr_x in ((slice(0, t), ckv_b, kr4_t), (slice(t, t + PAST_LEN), cckv_b, kr4_t_c)):
        kn_t_all = _dot_nt(wukt_ref[...], ckv_x)
        v_all = _dot(ckv_x, wuv_ref[...])
        for j in range(MLA_HEADS // 2):
            kn_t_pair = kn_t_all[j * LANES:(j + 1) * LANES, :]
            mv_s[2 * j, cols, :], mv_s[2 * j + 1, cols, :] = _value_planes(
                v_all[:, j * LANES:(j + 1) * LANES], True, False)
            for odd in range(2):
                mk_s[2 * j + odd, :, cols] = _mla_key_plane(kn_t_pair, kr_x, j, odd)

    tq = 256

    def mla_block(i, carry):
        q0 = pl.multiple_of(i * tq, tq)
        for j in range(MLA_HEADS // 2):
            qj = jnp.concatenate([mq_s[pl.ds(q0, tq), j * LANES:(j + 1) * LANES],
                                  mq_s[pl.ds(q0, tq), 2 * LANES:3 * LANES]], axis=1)
            (pe,), _ = _probabilities([_dot(qj, mk_s[2 * j])])
            (po,), _ = _probabilities([_dot(qj, mk_s[2 * j + 1])])
            o = _attend([(pe, mv_s[2 * j]), (po, mv_s[2 * j + 1])])
            y_ref[pl.ds(q0, tq), Y_MLA + j * LANES:Y_MLA + (j + 1) * LANES] = o.astype(BF16)
        return carry

    lax.fori_loop(0, t // tq, mla_block, 0, unroll=True)


def _rope_tables(group, half):
    tok = np.arange(DEC_SEQ)
    pos = np.stack([tok // GRID_W, tok % GRID_W], axis=1).astype(np.float64)
    inv = ROPE_BASE ** (-np.arange(half, dtype=np.float64) / half)
    lane = np.arange(LANES) % group
    axis = lane // (2 * half)
    within = lane % (2 * half)
    ang = pos[:, axis] * inv[within % half][None, :]
    cos, sin = np.cos(ang), np.sin(ang)
    upper = (within >= half)[None, :]
    sin_hi = np.where(upper, sin, 0.0)
    sin_lo = np.where(upper, 0.0, -sin)
    return tuple(jnp.asarray(a, dtype=F32) for a in (cos, sin_hi, sin_lo))


def _band_mask():
    i = np.arange(BAND_BLOCK)[:, None]
    c = np.arange(BAND_BLOCK + 2 * WINDOW)[None, :]
    ok = (c >= i) & (c <= i + 2 * WINDOW)
    m = np.where(ok, 0.0, NEG_INF)
    return jnp.asarray(np.concatenate([m, m], axis=0), dtype=F32)


def _lat_mixer_call(proj, sink, w_conv, g_q, w_uq, g_kv, w_uk_t, w_uv, nab, caches, layer, name):
    n, t = DEC_BATCH, DEC_SEQ
    one = pl.Buffered(1)
    const = lambda shape: pl.BlockSpec(shape, lambda s: (0,) * len(shape), pipeline_mode=one)
    cache = lambda shape: pl.BlockSpec((None, None) + shape, lambda s: (s, layer, 0, 0))
    rope_g = _rope_tables(HEAD_DIM, HEAD_DIM // 4)
    rope_m = _rope_tables(MLA_ROPE, MLA_ROPE // 4)
    span = BAND_BLOCK + 2 * WINDOW
    keys = t + PAST_LEN
    scratch = [
        pltpu.VMEM((t, GQA_HEADS // 2 * LANES), BF16),
        pltpu.VMEM((2 * GQA_KV_HEADS, LANES, t), BF16),
        pltpu.VMEM((2 * GQA_KV_HEADS, t, 2 * LANES), BF16),
        pltpu.VMEM((GQA_KV_HEADS, LANES, 2 * PAST_LEN), BF16),
        pltpu.VMEM((2 * GQA_KV_HEADS, PAST_LEN, 2 * LANES), BF16),
        pltpu.VMEM((t, NA_HEADS // 2 * LANES), BF16),
        pltpu.VMEM((2, NA_HEADS, LANES, t), BF16),
        pltpu.VMEM((NA_HEADS, t, 2 * LANES), BF16),
        pltpu.VMEM((NA_HEADS // 2, LANES, 2 * PAST_LEN), BF16),
        pltpu.VMEM((NA_HEADS, PAST_LEN, 2 * LANES), BF16),
        pltpu.VMEM((t, 3 * LANES), BF16),
        pltpu.VMEM((MLA_HEADS, 2 * LANES, keys), BF16),
        pltpu.VMEM((MLA_HEADS, keys, 2 * LANES), BF16),
    ]
    return pl.pallas_call(
        functools.partial(_lat_mixer_kernel, layer=layer),
        grid=(n,),
        in_specs=[
            pl.BlockSpec(memory_space=pltpu.SMEM),
            pl.BlockSpec((None, t, MAIN_COLS), lambda s: (s, 0, 0), pipeline_mode=one),
            _layer_spec((CONV_K, CONV_WIDTH), layer, 1),
            _layer_spec((1, MLA_Q_RANK), layer, 1),
            _layer_spec((MLA_Q_RANK, 3 * LANES), layer, 1),
            _layer_spec((1, MLA_KV_RANK), layer, 1),
            _layer_spec((2 * LANES, MLA_KV_RANK), layer, 1),
            _layer_spec((MLA_KV_RANK, 2 * LANES), layer, 1),
        ] + [const((t, LANES))] * 6 + [
            const((2 * BAND_BLOCK, span)),
            _layer_spec((NA_PATTERNS, GRID_W, NA_HEADS * NA_KEYS), layer, 1),
            cache((2 * HEAD_DIM, PAST_LEN)), cache((2 * HEAD_DIM, PAST_LEN)),
            cache((4 * HEAD_DIM, PAST_LEN)), cache((4 * HEAD_DIM, PAST_LEN)),
            cache((PAST_LEN, MLA_KV_RANK)), cache((MLA_ROPE, PAST_LEN)),
        ],
        out_specs=pl.BlockSpec((None, t, Y_COLS), lambda s: (s, 0, 0)),
        out_shape=jax.ShapeDtypeStruct((n, t, Y_COLS), BF16),
        scratch_shapes=scratch,
        compiler_params=pltpu.CompilerParams(
            dimension_semantics=("arbitrary",), vmem_limit_bytes=VMEM_LIMIT),
        name=name,
    )(sink, proj, w_conv, g_q, w_uq, g_kv, w_uk_t, w_uv, *rope_g, *rope_m, _band_mask(), nab, *caches)


def _post_kernel(x_ref, y_ref, mod_ref, ga_ref, gm_ref, gf_ref, wg_ref, wb_ref, wo_ref, w1_ref, w2_ref,
                 o_ref, *, final):
    d = D_MODEL
    mod = lambda i: mod_ref[:, i * d:(i + 1) * d]
    x = x_ref[...]
    h = (_rms(x, ga_ref[...]) * (1.0 + mod(1)) + mod(0)).astype(BF16)
    gates = jax.nn.sigmoid(_dot(h, wg_ref[...]))
    bounds = (Y_CONV, Y_GQA, Y_NA, Y_MLA, Y_COLS)
    merged = None
    for i in range(N_BRANCH):
        lo, hi = bounds[i], bounds[i + 1]
        term = gates[:, i * d:(i + 1) * d] * _dot(y_ref[:, lo:hi], wb_ref[lo:hi, :])
        merged = term if merged is None else merged + term
    x = x + mod(2) * _dot(merged.astype(BF16), wo_ref[...])
    h = (_rms(x, gm_ref[...]) * (1.0 + mod(4)) + mod(3)).astype(BF16)
    f = jnp.square(jnp.maximum(_dot(h, w1_ref[...]), 0.0)).astype(BF16)
    x = x + mod(5) * _dot(f, w2_ref[...])
    o_ref[...] = _rms(x, gf_ref[...]) if final else x


def _post_call(x, y, mods, g_attn, g_mlp, g_final, w_gates, w_br, w_o, w_ff1, w_ff2, layer, ctx, name):
    n, t, _ = x.shape
    tm = 256
    return pl.pallas_call(
        functools.partial(_post_kernel, final=layer == DEPTH - 1),
        grid=(n, t // tm),
        in_specs=[
            pl.BlockSpec((None, tm, D_MODEL), lambda s, i: (s, i, 0)),
            pl.BlockSpec((None, tm, Y_COLS), lambda s, i: (s, i, 0)),
            _mod_spec(layer, ctx, 2),
            _layer_spec((1, D_MODEL), layer, 2),
            _layer_spec((1, D_MODEL), layer, 2),
            pl.BlockSpec((1, D_MODEL), lambda s, i: (0, 0)),
            _layer_spec((D_MODEL, N_BRANCH * D_MODEL), layer, 2),
            _layer_spec((Y_COLS, D_MODEL), layer, 2),
            _layer_spec((D_MODEL, D_MODEL), layer, 2),
            _layer_spec((D_MODEL, D_FF), layer, 2),
            _layer_spec((D_FF, D_MODEL), layer, 2),
        ],
        out_specs=pl.BlockSpec((None, tm, D_MODEL), lambda s, i: (s, i, 0)),
        out_shape=jax.ShapeDtypeStruct((n, t, D_MODEL), F32),
        compiler_params=pltpu.CompilerParams(
            dimension_semantics=("arbitrary", "arbitrary"), vmem_limit_bytes=VMEM_LIMIT),
        name=name,
    )(x, y, mods, g_attn, g_mlp, g_final, w_gates, w_br, w_o, w_ff1, w_ff2)


def kernel(x_prompt, x_sample, cache_gqa_k, cache_gqa_v, cache_na_k, cache_na_v, cache_mla_ckv, cache_mla_krope, c, c_ctx, w_mod, b_mod, g_attn, g_mlp, w_in, w_conv, gqa_sink, na_rpb, mla_g_q, mla_w_uq, mla_g_kv, mla_w_ukv, w_branch_conv, w_branch_gqa, w_branch_na, w_branch_mla, w_o, w_ff1, w_ff2, g_final):
    w_main = w_in[:, :, :MAIN_COLS].astype(BF16)
    w_gates = w_in[:, :, GATE_COL0:].astype(BF16)
    uq = mla_w_uq.reshape(DEPTH, MLA_Q_RANK, MLA_HEADS, MLA_NOPE + MLA_ROPE)
    w_uq = jnp.concatenate([uq[..., :MLA_NOPE].reshape(DEPTH, MLA_Q_RANK, -1),
                            uq[..., MLA_NOPE:].reshape(DEPTH, MLA_Q_RANK, -1)], axis=-1).astype(BF16)
    ukv = mla_w_ukv.reshape(DEPTH, MLA_KV_RANK, MLA_HEADS, MLA_NOPE + MLA_V)
    w_uk_t = jnp.transpose(ukv[..., :MLA_NOPE].reshape(DEPTH, MLA_KV_RANK, -1), (0, 2, 1)).astype(BF16)
    w_uv = ukv[..., MLA_NOPE:].reshape(DEPTH, MLA_KV_RANK, -1).astype(BF16)
    w_br = jnp.concatenate([w_branch_conv, w_branch_gqa, w_branch_na, w_branch_mla], axis=1).astype(BF16)
    w_o_b, w_ff1_b, w_ff2_b = w_o.astype(BF16), w_ff1.astype(BF16), w_ff2.astype(BF16)
    g_a, g_m = g_attn[:, None, :], g_mlp[:, None, :]
    g_q, g_kv, g_f = mla_g_q[:, None, :], mla_g_kv[:, None, :], g_final[None, :]

    c16 = jnp.concatenate([c, c_ctx[None, :], jnp.zeros((MOD_ROWS - DEC_BATCH - 1, D_MODEL), F32)], axis=0)
    mods = _mod_call(c16, w_mod, b_mod).reshape(DEPTH, MOD_ROWS, 1, 6 * D_MODEL)
    nab = _nab_call(na_rpb)

    heads_t = lambda a: jnp.transpose(a, (0, 1, 3, 4, 2)).reshape(a.shape[0], DEPTH, -1, a.shape[2])
    caches = (heads_t(cache_gqa_k), heads_t(cache_gqa_v), heads_t(cache_na_k), heads_t(cache_na_v),
              cache_mla_ckv, jnp.transpose(cache_mla_krope, (0, 1, 3, 2)))

    h_ctx, h_lat = x_prompt, x_sample
    states = None
    for l in range(DEPTH):
        mixer_w = (gqa_sink, w_conv, g_q, w_uq, g_kv, w_uk_t, w_uv)
        post_w = (g_a, g_m, g_f, w_gates, w_br, w_o_b, w_ff1_b, w_ff2_b)

        proj = _pre_call(h_ctx, mods, g_a, w_main, l, True, f"pre_ctx_{l}")
        y, states = _ctx_mixer_call(proj, *mixer_w, states, l, f"mixer_ctx_{l}")
        h_ctx = _post_call(h_ctx, y, mods, *post_w, l, True, f"post_ctx_{l}")

        proj = _pre_call(h_lat, mods, g_a, w_main, l, False, f"pre_lat_{l}")
        y = _lat_mixer_call(proj, *mixer_w, nab, caches, l, f"mixer_lat_{l}")
        h_lat = _post_call(h_lat, y, mods, *post_w, l, False, f"post_lat_{l}")

    def heads_out(a, heads):
        return jnp.transpose(a.reshape(BATCH, DEPTH, heads, HEAD_DIM, SEQ), (0, 1, 4, 2, 3))

    kg, vg, kn, vn, ckv, kr = states
    return (h_ctx, h_lat, heads_out(kg, GQA_KV_HEADS), heads_out(vg, GQA_KV_HEADS),
            heads_out(kn, NA_HEADS), heads_out(vn, NA_HEADS), ckv, jnp.transpose(kr, (0, 1, 3, 2)))
```

```python
import functools
import math

import numpy as np
import jax
import jax.numpy as jnp
from jax import lax
from jax.experimental import pallas as pl
from jax.experimental.pallas import tpu as pltpu

D_MODEL = 1024
BATCH = 32
SEQ = 256
DEPTH = 2
DEC_BATCH = 8
DEC_SEQ = 1024
PAST_LEN = 512
GRID_W = 64
GRID_ROWS = DEC_SEQ // GRID_W
HEAD_DIM = 64
CONV_WIDTH = 256
CONV_K = 3
GQA_HEADS = 8
GQA_KV_HEADS = 2
WINDOW = 128
BAND_BLOCK = 128
NA_HEADS = 4
NA_WIN_H = 8
NA_WIN_W = 16
MLA_HEADS = 4
MLA_Q_RANK = 256
MLA_KV_RANK = 128
MLA_NOPE = 64
MLA_ROPE = 32
MLA_V = 64
D_FF = 4 * D_MODEL
N_BRANCH = 4
ROPE_BASE = 10000.0
EPS = 1e-6
NEG_INF = -1e30
LOG2E = math.log2(math.e)
ATTN_SCALE = HEAD_DIM ** -0.5
MLA_SCALE = (MLA_NOPE + MLA_ROPE) ** -0.5

LANES = 128
MOD_ROWS = 16
CTX_MOD_ROW = DEC_BATCH

C_CB, C_CC, C_CV = 0, 256, 512
C_GQ, C_GK, C_GV = 768, 1280, 1408
C_NQ, C_NK, C_NV = 1536, 1792, 2048
C_MQ, C_MKV, C_MKR = 2304, 2560, 2688
MAIN_COLS = 2816
GATE_COL0 = 2720
Y_CONV, Y_GQA, Y_NA, Y_MLA = 0, 256, 768, 1024
Y_COLS = 1280
NA_PATTERNS = 8
NA_KEYS = NA_WIN_H * GRID_W

VMEM_LIMIT = 56 * 1024 * 1024

F32 = jnp.float32
BF16 = jnp.bfloat16


def _dot(a, b):
    return jnp.dot(a, b, preferred_element_type=F32)


def _dot_nt(a, b):
    return lax.dot_general(a, b, (((1,), (1,)), ((), ())), preferred_element_type=F32)


def _rms(x, g):
    return x * lax.rsqrt(jnp.mean(x * x, axis=-1, keepdims=True) + EPS) * g


def _lane_lt(shape, n):
    return lax.broadcasted_iota(jnp.int32, shape, len(shape) - 1) < n


def _row_lt(shape, n):
    return lax.broadcasted_iota(jnp.int32, shape, 0) < n


def _row_group(shape, lo, hi):
    row = lax.broadcasted_iota(jnp.int32, shape, 0)
    return (row >= lo) & (row < hi)


def _short_conv(cb, cc, cv, w):
    u = cc * cv
    t = u.shape[0]
    row = lax.broadcasted_iota(jnp.int32, u.shape, 0)
    prev = jnp.where(row == 0, 0.0, pltpu.roll(u, 1, 0))
    nxt = jnp.where(row == t - 1, 0.0, pltpu.roll(u, t - 1, 0))
    return cb * (prev * w[0:1, :] + u * w[1:2, :] + nxt * w[2:3, :])


def _rope(x, cos, sin_hi, sin_lo, half):
    n = x.shape[-1]
    return x * cos + pltpu.roll(x, n - half, 1) * sin_lo + pltpu.roll(x, half, 1) * sin_hi


def _repeat_rope_key(tile):
    k = jnp.where(_lane_lt(tile.shape, MLA_ROPE), tile, 0.0)
    k = k + pltpu.roll(k, MLA_ROPE, 1)
    return k + pltpu.roll(k, 2 * MLA_ROPE, 1)


def _mod_kernel(c_ref, w_ref, b_ref, o_ref):
    c = c_ref[...]
    s = c * jax.nn.sigmoid(c)
    o_ref[...] = _dot(s.astype(BF16), w_ref[...].astype(BF16)) + b_ref[...]


def _mod_call(c16, w_mod, b_mod):
    tn = 1536
    return pl.pallas_call(
        _mod_kernel,
        grid=(DEPTH, 6 * D_MODEL // tn),
        in_specs=[
            pl.BlockSpec((MOD_ROWS, D_MODEL), lambda l, j: (0, 0)),
            pl.BlockSpec((None, D_MODEL, tn), lambda l, j: (l, 0, j)),
            pl.BlockSpec((None, 1, tn), lambda l, j: (l, 0, j)),
        ],
        out_specs=pl.BlockSpec((None, MOD_ROWS, tn), lambda l, j: (l, 0, j)),
        out_shape=jax.ShapeDtypeStruct((DEPTH, MOD_ROWS, 6 * D_MODEL), F32),
        compiler_params=pltpu.CompilerParams(
            dimension_semantics=("arbitrary", "arbitrary"), vmem_limit_bytes=VMEM_LIMIT),
        name="adaln_mod",
    )(c16, w_mod, b_mod.reshape(DEPTH, 1, 6 * D_MODEL))


def _nab_kernel(rpb_ref, o_ref):
    l = pl.program_id(0)
    c = lax.broadcasted_iota(jnp.int32, (GRID_W, GRID_W), 0)
    w = lax.broadcasted_iota(jnp.int32, (GRID_W, GRID_W), 1)
    dc = w - c + (NA_WIN_W - 1)
    c0 = jnp.clip(c - NA_WIN_W // 2, 0, GRID_W - NA_WIN_W)
    outside = (w < c0) | (w >= c0 + NA_WIN_W)
    n_dr, n_dc = 2 * NA_WIN_H - 1, 2 * NA_WIN_W - 1
    for h in range(NA_HEADS):
        for d in range(n_dr):
            base = ((l * NA_HEADS + h) * n_dr + d) * n_dc
            tile = jnp.full((GRID_W, GRID_W), NEG_INF, F32)
            for j in range(n_dc):
                tile = jnp.where(dc == j, rpb_ref[base + j] * LOG2E, tile)
            tile = jnp.where(outside, NEG_INF, tile)
            for p in range(NA_PATTERNS):
                a = d - (NA_WIN_H - 1) + p
                if 0 <= a < NA_WIN_H:
                    col = h * NA_KEYS + a * GRID_W
                    o_ref[p, :, col:col + GRID_W] = tile


def _nab_call(na_rpb):
    return pl.pallas_call(
        _nab_kernel,
        grid=(DEPTH,),
        in_specs=[pl.BlockSpec(memory_space=pltpu.SMEM)],
        out_specs=pl.BlockSpec((None, NA_PATTERNS, GRID_W, NA_HEADS * NA_KEYS), lambda l: (l, 0, 0, 0)),
        out_shape=jax.ShapeDtypeStruct((DEPTH, NA_PATTERNS, GRID_W, NA_HEADS * NA_KEYS), F32),
        compiler_params=pltpu.CompilerParams(dimension_semantics=("arbitrary",)),
        name="na_bias_tables",
    )(na_rpb.reshape(-1))


def _mod_spec(layer, ctx, n_grid):
    if n_grid == 1:
        index = (lambda s: (layer, CTX_MOD_ROW, 0, 0)) if ctx else (lambda s: (layer, s, 0, 0))
    else:
        index = (lambda s, i: (layer, CTX_MOD_ROW, 0, 0)) if ctx else (lambda s, i: (layer, s, 0, 0))
    return pl.BlockSpec((None, None, 1, 6 * D_MODEL), index)


def _layer_spec(shape, layer, n_grid):
    zeros = (0,) * len(shape)
    index = (lambda s: (layer,) + zeros) if n_grid == 1 else (lambda s, i: (layer,) + zeros)
    return pl.BlockSpec((None,) + tuple(shape), index, pipeline_mode=pl.Buffered(1))


_COLUMN_GROUPS = ((C_CB, C_GQ), (C_GQ, C_NQ), (C_NQ, C_MQ), (C_MQ, MAIN_COLS))


def _modulated_norm(x, g, mod_ref, shift, scale):
    d = D_MODEL
    return _rms(x, g) * (1.0 + mod_ref[:, scale * d:(scale + 1) * d]) + mod_ref[:, shift * d:(shift + 1) * d]


def _projected_columns(h_ref, w_ref):
    cache = {}

    def col(c, n):
        lo, hi = next(g for g in _COLUMN_GROUPS if g[0] <= c and c + n <= g[1])
        if lo not in cache:
            cache.clear()
            cache[lo] = _dot(h_ref[...], w_ref[:, lo:hi])
        return cache[lo][:, c - lo:c - lo + n]

    return col


def _key_planes(k_t, even_first, odd_first):
    lo = _row_lt(k_t.shape, HEAD_DIM)
    swapped = None
    if not even_first or odd_first:
        swapped = pltpu.roll(k_t, HEAD_DIM, 0)
    top = jnp.where(lo, k_t if even_first else swapped, 0.0)
    bot = jnp.where(lo, 0.0, swapped if odd_first else k_t)
    return top.astype(BF16), bot.astype(BF16)


def _value_planes(pair, even_first, odd_first):
    lo = _lane_lt(pair.shape, HEAD_DIM)
    swapped = None
    if not even_first or odd_first:
        swapped = pltpu.roll(pair, HEAD_DIM, 1)
    top = jnp.where(lo, pair if even_first else swapped, 0.0)
    bot = jnp.where(lo, 0.0, swapped if odd_first else pair)
    one_e = jnp.where(lo, 1.0, 0.0)
    return (jnp.concatenate([top, one_e], axis=1).astype(BF16),
            jnp.concatenate([bot, 1.0 - one_e], axis=1).astype(BF16))


def _probabilities(parts, sink=None):
    m = parts[0].max(axis=-1, keepdims=True)
    for p in parts[1:]:
        m = jnp.maximum(m, p.max(axis=-1, keepdims=True))
    if sink is not None:
        m = jnp.maximum(m, sink)
    probs = [jnp.exp2((p - m).astype(BF16)) for p in parts]
    return probs, (None if sink is None else jnp.exp2(sink - m))


def _attend(terms, sink_e=None, sink_o=None):
    o = None
    for p, v in terms:
        t = _dot(p, v)
        o = t if o is None else o + t
    den = o[:, LANES:]
    if sink_e is not None:
        den = den + jnp.where(_lane_lt(den.shape, HEAD_DIM), sink_e, sink_o)
    return o[:, :LANES] / den


def _mla_key_plane(kn_t_pair, kr4_t, j, odd):
    h = 2 * j + odd
    lo = _row_lt(kn_t_pair.shape, HEAD_DIM)
    nope = jnp.where(lo, 0.0, kn_t_pair) if odd else jnp.where(lo, kn_t_pair, 0.0)
    rope = jnp.where(_row_group(kr4_t.shape, h * MLA_ROPE, (h + 1) * MLA_ROPE), kr4_t, 0.0)
    return jnp.concatenate([nope, rope], axis=0).astype(BF16)


def _ctx_mixer_kernel(sink_ref, x_ref, mod_ref, ga_ref, win_ref, wc_ref, gq_ref, wuq_ref, gkv_ref, wukt_ref, wuv_ref,
                      *rest, layer, first):
    y_ref, kg_ref, vg_ref, kn_ref, vn_ref, ckv_ref, kr_ref, h_s = rest[-8:]
    t = SEQ
    h_s[...] = _modulated_norm(x_ref[...], ga_ref[...], mod_ref, 0, 1).astype(BF16)
    col = _projected_columns(h_s, win_ref)

    def put_state(ref, val):
        if first:
            for d in range(DEPTH):
                ref[d] = val if d == layer else jnp.zeros_like(val)
        else:
            ref[...] = val

    y_ref[:, Y_CONV:Y_CONV + CONV_WIDTH] = _short_conv(
        col(C_CB, CONV_WIDTH), col(C_CC, CONV_WIDTH), col(C_CV, CONV_WIDTH), wc_ref[...]).astype(BF16)

    kg_t, vpair = col(C_GK, LANES).T, col(C_GV, LANES)
    put_state(kg_ref, kg_t)
    put_state(vg_ref, vpair.T)
    row2 = lax.broadcasted_iota(jnp.int32, (2 * t, 1), 0) < t
    for g in range(GQA_KV_HEADS):
        ke, ko = _key_planes(kg_t, g == 0, g == 0)
        ve, vo = _value_planes(vpair, g == 0, g == 0)
        q = jnp.concatenate([col(C_GQ + (2 * g) * LANES, LANES), col(C_GQ + (2 * g + 1) * LANES, LANES)], axis=0)
        q = (q * (ATTN_SCALE * LOG2E)).astype(BF16)
        s = _dot(q, jnp.concatenate([ke, ko], axis=1))
        sink = [jnp.where(row2, sink_ref[layer, 4 * g + odd], sink_ref[layer, 4 * g + 2 + odd]) * LOG2E
                for odd in range(2)]
        (pe,), xe = _probabilities([s[:, 0:t]], sink[0])
        (po,), xo = _probabilities([s[:, t:2 * t]], sink[1])
        o = _attend([(pe, ve), (po, vo)], xe, xo).astype(BF16)
        c0 = Y_GQA + (2 * g) * LANES
        y_ref[:, c0:c0 + LANES] = o[0:t]
        y_ref[:, c0 + LANES:c0 + 2 * LANES] = o[t:2 * t]

    kn_t = col(C_NK, 2 * LANES).T
    put_state(kn_ref, kn_t)
    put_state(vn_ref, col(C_NV, 2 * LANES).T)
    for j in range(NA_HEADS // 2):
        ke, ko = _key_planes(kn_t[j * LANES:(j + 1) * LANES, :], True, False)
        ve, vo = _value_planes(col(C_NV + j * LANES, LANES), True, False)
        q = (col(C_NQ + j * LANES, LANES) * (ATTN_SCALE * LOG2E)).astype(BF16)
        s = _dot(q, jnp.concatenate([ke, ko], axis=1))
        (pe,), _ = _probabilities([s[:, 0:t]])
        (po,), _ = _probabilities([s[:, t:2 * t]])
        y_ref[:, Y_NA + j * LANES:Y_NA + (j + 1) * LANES] = _attend([(pe, ve), (po, vo)]).astype(BF16)

    ckv = _rms(col(C_MKV, MLA_KV_RANK), gkv_ref[...])
    put_state(ckv_ref, ckv)
    kr_t = col(C_MKR, LANES).T[0:MLA_ROPE, :]
    put_state(kr_ref, kr_t)
    ckv_b = ckv.astype(BF16)
    q = _dot(_rms(col(C_MQ, MLA_Q_RANK), gq_ref[...]).astype(BF16), wuq_ref[...]) * (MLA_SCALE * LOG2E)
    kn_t_all = _dot_nt(wukt_ref[...], ckv_b)
    v_all = _dot(ckv_b, wuv_ref[...])
    kr4_t = jnp.concatenate([kr_t] * MLA_HEADS, axis=0)
    q_rope = q[:, 2 * LANES:3 * LANES]
    for j in range(MLA_HEADS // 2):
        qj = jnp.concatenate([q[:, j * LANES:(j + 1) * LANES], q_rope], axis=1).astype(BF16)
        kn_t_pair = kn_t_all[j * LANES:(j + 1) * LANES, :]
        keys = jnp.concatenate([_mla_key_plane(kn_t_pair, kr4_t, j, 0), _mla_key_plane(kn_t_pair, kr4_t, j, 1)], axis=1)
        s = _dot(qj, keys)
        ve, vo = _value_planes(v_all[:, j * LANES:(j + 1) * LANES], True, False)
        (pe,), _ = _probabilities([s[:, 0:t]])
        (po,), _ = _probabilities([s[:, t:2 * t]])
        y_ref[:, Y_MLA + j * LANES:Y_MLA + (j + 1) * LANES] = _attend([(pe, ve), (po, vo)]).astype(BF16)


def _ctx_mixer_call(x, mods, g_attn, w_main, sink, w_conv, g_q, w_uq, g_kv, w_uk_t, w_uv, prev_states, layer, name):
    n = BATCH
    state_shapes = ((2 * HEAD_DIM, SEQ), (2 * HEAD_DIM, SEQ), (4 * HEAD_DIM, SEQ), (4 * HEAD_DIM, SEQ),
                    (SEQ, MLA_KV_RANK), (MLA_ROPE, SEQ))
    first = prev_states is None
    if first:
        state_spec = lambda shape: pl.BlockSpec((None, DEPTH) + shape, lambda s: (s, 0, 0, 0))
    else:
        state_spec = lambda shape: pl.BlockSpec((None, None) + shape, lambda s: (s, layer, 0, 0))
    in_specs = [
        pl.BlockSpec(memory_space=pltpu.SMEM),
        pl.BlockSpec((None, SEQ, D_MODEL), lambda s: (s, 0, 0)),
        _mod_spec(layer, True, 1),
        _layer_spec((1, D_MODEL), layer, 1),
        _layer_spec((D_MODEL, MAIN_COLS), layer, 1),
        _layer_spec((CONV_K, CONV_WIDTH), layer, 1),
        _layer_spec((1, MLA_Q_RANK), layer, 1),
        _layer_spec((MLA_Q_RANK, 3 * LANES), layer, 1),
        _layer_spec((1, MLA_KV_RANK), layer, 1),
        _layer_spec((2 * LANES, MLA_KV_RANK), layer, 1),
        _layer_spec((MLA_KV_RANK, 2 * LANES), layer, 1),
    ]
    args = [sink, x, mods, g_attn, w_main, w_conv, g_q, w_uq, g_kv, w_uk_t, w_uv]
    aliases = {}
    if not first:
        for i, st in enumerate(prev_states):
            aliases[len(args)] = 1 + i
            in_specs.append(pl.BlockSpec(memory_space=pl.ANY))
            args.append(st)
    outs = pl.pallas_call(
        functools.partial(_ctx_mixer_kernel, layer=layer, first=first),
        grid=(n,),
        in_specs=in_specs,
        out_specs=[pl.BlockSpec((None, SEQ, Y_COLS), lambda s: (s, 0, 0))] + [state_spec(s) for s in state_shapes],
        out_shape=[jax.ShapeDtypeStruct((n, SEQ, Y_COLS), BF16)]
        + [jax.ShapeDtypeStruct((n, DEPTH) + s, F32) for s in state_shapes],
        input_output_aliases=aliases,
        scratch_shapes=[pltpu.VMEM((SEQ, D_MODEL), BF16)],
        compiler_params=pltpu.CompilerParams(
            dimension_semantics=("arbitrary",), vmem_limit_bytes=VMEM_LIMIT),
        name=name,
    )(*args)
    return outs[0], outs[1:]


def _lat_mixer_kernel(sink_ref, x_ref, mod_ref, ga_ref, win_ref, wc_ref, gq_ref, wuq_ref, gkv_ref, wukt_ref, wuv_ref,
                      rc_ref, rsh_ref, rsl_ref, mc_ref, msh_ref, msl_ref, band_ref, nab_ref,
                      cgk_ref, cgv_ref, cnk_ref, cnv_ref, cckv_ref, ckr_ref,
                      y_ref,
                      h_s, gq_s, gk_s, gv_s, gkc_s, gvc_s, nq_s, nk_s, nv_s, nkc_s, nvc_s, mq_s, mk_s, mv_s, *, layer):
    t = DEC_SEQ
    h_s[...] = _modulated_norm(x_ref[...], ga_ref[...], mod_ref, 0, 1).astype(BF16)
    col = _projected_columns(h_s, win_ref)
    rope64 = lambda x: _rope(x, rc_ref[...], rsh_ref[...], rsl_ref[...], HEAD_DIM // 4)
    rope32 = lambda x: _rope(x, mc_ref[...], msh_ref[...], msl_ref[...], MLA_ROPE // 4)

    y_ref[:, Y_CONV:Y_CONV + CONV_WIDTH] = _short_conv(
        col(C_CB, CONV_WIDTH), col(C_CC, CONV_WIDTH), col(C_CV, CONV_WIDTH), wc_ref[...]).astype(BF16)

    for j in range(GQA_HEADS // 2):
        gq_s[:, j * LANES:(j + 1) * LANES] = (
            rope64(col(C_GQ + j * LANES, LANES)) * (ATTN_SCALE * LOG2E)).astype(BF16)
    k_t, vpair = rope64(col(C_GK, LANES)).T, col(C_GV, LANES)
    kc_t, vc_pair = cgk_ref[...], cgv_ref[...].T
    for g in range(GQA_KV_HEADS):
        gk_s[2 * g], gk_s[2 * g + 1] = _key_planes(k_t, g == 0, g == 0)
        gv_s[2 * g], gv_s[2 * g + 1] = _value_planes(vpair, g == 0, g == 0)
        gkc_s[g] = jnp.concatenate(_key_planes(kc_t, g == 0, g == 0), axis=1)
        gvc_s[2 * g], gvc_s[2 * g + 1] = _value_planes(vc_pair, g == 0, g == 0)

    def gqa_block(b, c_lo, c_hi):
        q0 = pl.multiple_of(b * BAND_BLOCK, BAND_BLOCK)
        k0 = pl.multiple_of(q0 + (c_lo - WINDOW), BAND_BLOCK)
        n = c_hi - c_lo
        band = band_ref[:, c_lo:c_hi]
        row2 = lax.broadcasted_iota(jnp.int32, (2 * BAND_BLOCK, 1), 0) < BAND_BLOCK
        for g in range(GQA_KV_HEADS):
            q = jnp.concatenate([gq_s[pl.ds(q0, BAND_BLOCK), (2 * g) * LANES:(2 * g + 1) * LANES],
                                 gq_s[pl.ds(q0, BAND_BLOCK), (2 * g + 1) * LANES:(2 * g + 2) * LANES]], axis=0)
            s_ctx = _dot(q, gkc_s[g])
            terms, sinks = [], []
            for odd in range(2):
                s_loc = _dot(q, gk_s[2 * g + odd, :, pl.ds(k0, n)]) + band
                sink = jnp.where(row2, sink_ref[layer, 4 * g + odd], sink_ref[layer, 4 * g + 2 + odd]) * LOG2E
                (p_loc, p_ctx), x = _probabilities([s_loc, s_ctx[:, odd * PAST_LEN:(odd + 1) * PAST_LEN]], sink)
                terms += [(p_loc, gv_s[2 * g + odd, pl.ds(k0, n), :]), (p_ctx, gvc_s[2 * g + odd])]
                sinks.append(x)
            o = _attend(terms, sinks[0], sinks[1]).astype(BF16)
            c0 = Y_GQA + (2 * g) * LANES
            y_ref[pl.ds(q0, BAND_BLOCK), c0:c0 + LANES] = o[0:BAND_BLOCK]
            y_ref[pl.ds(q0, BAND_BLOCK), c0 + LANES:c0 + 2 * LANES] = o[BAND_BLOCK:2 * BAND_BLOCK]

    span = BAND_BLOCK + 2 * WINDOW
    nb = t // BAND_BLOCK
    gqa_block(0, WINDOW, span)
    lax.fori_loop(1, nb - 1, lambda b, c: (gqa_block(b, 0, span), c)[1], 0, unroll=True)
    gqa_block(nb - 1, 0, span - WINDOW)

    for j in range(NA_HEADS // 2):
        nq_s[:, j * LANES:(j + 1) * LANES] = (col(C_NQ + j * LANES, LANES) * (ATTN_SCALE * LOG2E)).astype(BF16)
        nk_t = col(C_NK + j * LANES, LANES).T
        for shift, k_t in enumerate((nk_t, pltpu.roll(nk_t, t - GRID_W, 1))):
            nk_s[shift, 2 * j], nk_s[shift, 2 * j + 1] = _key_planes(k_t, True, False)
        nv_s[2 * j], nv_s[2 * j + 1] = _value_planes(col(C_NV + j * LANES, LANES), True, False)
        nkc_s[j] = jnp.concatenate(_key_planes(cnk_ref[j * LANES:(j + 1) * LANES, :], True, False), axis=1)
        nvc_s[2 * j], nvc_s[2 * j + 1] = _value_planes(cnv_ref[j * LANES:(j + 1) * LANES, :].T, True, False)

    def na_row(r, carry):
        q0 = pl.multiple_of(r * GRID_W, GRID_W)
        r0 = jnp.clip(r - NA_WIN_H // 2, 0, GRID_ROWS - NA_WIN_H)
        k0 = pl.multiple_of(r0 * GRID_W, GRID_W)
        kt0 = pl.multiple_of(lax.shift_right_logical(r0, 1) * LANES, LANES)
        pat = r - r0
        for j in range(NA_HEADS // 2):
            q = nq_s[pl.ds(q0, GRID_W), j * LANES:(j + 1) * LANES]
            s_ctx = _dot(q, nkc_s[j])
            terms = []
            for odd in range(2):
                h = 2 * j + odd
                s_loc = _dot(q, nk_s[r0 & 1, h, :, pl.ds(kt0, NA_KEYS)])
                s_loc = s_loc + nab_ref[pat, :, h * NA_KEYS:(h + 1) * NA_KEYS]
                (p_loc, p_ctx), _ = _probabilities([s_loc, s_ctx[:, odd * PAST_LEN:(odd + 1) * PAST_LEN]])
                terms += [(p_loc, nv_s[h, pl.ds(k0, NA_KEYS), :]), (p_ctx, nvc_s[h])]
            y_ref[pl.ds(q0, GRID_W), Y_NA + j * LANES:Y_NA + (j + 1) * LANES] = _attend(terms).astype(BF16)
        return carry

    lax.fori_loop(0, GRID_ROWS, na_row, 0, unroll=True)

    ckv_b = _rms(col(C_MKV, MLA_KV_RANK), gkv_ref[...]).astype(BF16)
    cckv_b = cckv_ref[...].astype(BF16)
    q = _dot(_rms(col(C_MQ, MLA_Q_RANK), gq_ref[...]).astype(BF16), wuq_ref[...]) * (MLA_SCALE * LOG2E)
    mq_s[:, 0:2 * LANES] = q[:, 0:2 * LANES].astype(BF16)
    mq_s[:, 2 * LANES:3 * LANES] = rope32(q[:, 2 * LANES:3 * LANES]).astype(BF16)
    kr4_t = rope32(_repeat_rope_key(col(C_MKR, LANES))).T
    kr4_t_c = jnp.concatenate([ckr_ref[...]] * MLA_HEADS, axis=0)
    for cols, ckv_x, kr_x in ((slice(0, t), ckv_b, kr4_t), (slice(t, t + PAST_LEN), cckv_b, kr4_t_c)):
        kn_t_all = _dot_nt(wukt_ref[...], ckv_x)
        v_all = _dot(ckv_x, wuv_ref[...])
        for j in range(MLA_HEADS // 2):
            kn_t_pair = kn_t_all[j * LANES:(j + 1) * LANES, :]
            mv_s[2 * j, cols, :], mv_s[2 * j + 1, cols, :] = _value_planes(
                v_all[:, j * LANES:(j + 1) * LANES], True, False)
            for odd in range(2):
                mk_s[2 * j + odd, :, cols] = _mla_key_plane(kn_t_pair, kr_x, j, odd)

    tq = 256

    def mla_block(i, carry):
        q0 = pl.multiple_of(i * tq, tq)
        for j in range(MLA_HEADS // 2):
            qj = jnp.concatenate([mq_s[pl.ds(q0, tq), j * LANES:(j + 1) * LANES],
                                  mq_s[pl.ds(q0, tq), 2 * LANES:3 * LANES]], axis=1)
            (pe,), _ = _probabilities([_dot(qj, mk_s[2 * j])])
            (po,), _ = _probabilities([_dot(qj, mk_s[2 * j + 1])])
            o = _attend([(pe, mv_s[2 * j]), (po, mv_s[2 * j + 1])])
            y_ref[pl.ds(q0, tq), Y_MLA + j * LANES:Y_MLA + (j + 1) * LANES] = o.astype(BF16)
        return carry

    lax.fori_loop(0, t // tq, mla_block, 0, unroll=True)


def _rope_tables(group, half):
    tok = np.arange(DEC_SEQ)
    pos = np.stack([tok // GRID_W, tok % GRID_W], axis=1).astype(np.float64)
    inv = ROPE_BASE ** (-np.arange(half, dtype=np.float64) / half)
    lane = np.arange(LANES) % group
    axis = lane // (2 * half)
    within = lane % (2 * half)
    ang = pos[:, axis] * inv[within % half][None, :]
    cos, sin = np.cos(ang), np.sin(ang)
    upper = (within >= half)[None, :]
    sin_hi = np.where(upper, sin, 0.0)
    sin_lo = np.where(upper, 0.0, -sin)
    return tuple(jnp.asarray(a, dtype=F32) for a in (cos, sin_hi, sin_lo))


def _band_mask():
    i = np.arange(BAND_BLOCK)[:, None]
    c = np.arange(BAND_BLOCK + 2 * WINDOW)[None, :]
    ok = (c >= i) & (c <= i + 2 * WINDOW)
    m = np.where(ok, 0.0, NEG_INF)
    return jnp.asarray(np.concatenate([m, m], axis=0), dtype=F32)


def _lat_mixer_call(x, mods, g_attn, w_main, sink, w_conv, g_q, w_uq, g_kv, w_uk_t, w_uv, nab, caches, layer, name):
    n, t = DEC_BATCH, DEC_SEQ
    one = pl.Buffered(1)
    const = lambda shape: pl.BlockSpec(shape, lambda s: (0,) * len(shape), pipeline_mode=one)
    cache = lambda shape: pl.BlockSpec((None, None) + shape, lambda s: (s, layer, 0, 0), pipeline_mode=one)
    rope_g = _rope_tables(HEAD_DIM, HEAD_DIM // 4)
    rope_m = _rope_tables(MLA_ROPE, MLA_ROPE // 4)
    span = BAND_BLOCK + 2 * WINDOW
    keys = t + PAST_LEN
    scratch = [
        pltpu.VMEM((t, D_MODEL), BF16),
        pltpu.VMEM((t, GQA_HEADS // 2 * LANES), BF16),
        pltpu.VMEM((2 * GQA_KV_HEADS, LANES, t), BF16),
        pltpu.VMEM((2 * GQA_KV_HEADS, t, 2 * LANES), BF16),
        pltpu.VMEM((GQA_KV_HEADS, LANES, 2 * PAST_LEN), BF16),
        pltpu.VMEM((2 * GQA_KV_HEADS, PAST_LEN, 2 * LANES), BF16),
        pltpu.VMEM((t, NA_HEADS // 2 * LANES), BF16),
        pltpu.VMEM((2, NA_HEADS, LANES, t), BF16),
        pltpu.VMEM((NA_HEADS, t, 2 * LANES), BF16),
        pltpu.VMEM((NA_HEADS // 2, LANES, 2 * PAST_LEN), BF16),
        pltpu.VMEM((NA_HEADS, PAST_LEN, 2 * LANES), BF16),
        pltpu.VMEM((t, 3 * LANES), BF16),
        pltpu.VMEM((MLA_HEADS, 2 * LANES, keys), BF16),
        pltpu.VMEM((MLA_HEADS, keys, 2 * LANES), BF16),
    ]
    return pl.pallas_call(
        functools.partial(_lat_mixer_kernel, layer=layer),
        grid=(n,),
        in_specs=[
            pl.BlockSpec(memory_space=pltpu.SMEM),
            pl.BlockSpec((None, t, D_MODEL), lambda s: (s, 0, 0), pipeline_mode=one),
            _mod_spec(layer, False, 1),
            _layer_spec((1, D_MODEL), layer, 1),
            _layer_spec((D_MODEL, MAIN_COLS), layer, 1),
            _layer_spec((CONV_K, CONV_WIDTH), layer, 1),
            _layer_spec((1, MLA_Q_RANK), layer, 1),
            _layer_spec((MLA_Q_RANK, 3 * LANES), layer, 1),
            _layer_spec((1, MLA_KV_RANK), layer, 1),
            _layer_spec((2 * LANES, MLA_KV_RANK), layer, 1),
            _layer_spec((MLA_KV_RANK, 2 * LANES), layer, 1),
        ] + [const((t, LANES))] * 6 + [
            const((2 * BAND_BLOCK, span)),
            _layer_spec((NA_PATTERNS, GRID_W, NA_HEADS * NA_KEYS), layer, 1),
            cache((2 * HEAD_DIM, PAST_LEN)), cache((2 * HEAD_DIM, PAST_LEN)),
            cache((4 * HEAD_DIM, PAST_LEN)), cache((4 * HEAD_DIM, PAST_LEN)),
            cache((PAST_LEN, MLA_KV_RANK)), cache((MLA_ROPE, PAST_LEN)),
        ],
        out_specs=pl.BlockSpec((None, t, Y_COLS), lambda s: (s, 0, 0)),
        out_shape=jax.ShapeDtypeStruct((n, t, Y_COLS), BF16),
        scratch_shapes=scratch,
        compiler_params=pltpu.CompilerParams(
            dimension_semantics=("arbitrary",), vmem_limit_bytes=VMEM_LIMIT),
        name=name,
    )(sink, x, mods, g_attn, w_main, w_conv, g_q, w_uq, g_kv, w_uk_t, w_uv, *rope_g, *rope_m, _band_mask(), nab, *caches)


def _post_kernel(x_ref, y_ref, mod_ref, ga_ref, gm_ref, gf_ref, wg_ref, wb_ref, wo_ref, w1_ref, w2_ref,
                 o_ref, *, final):
    d = D_MODEL
    mod = lambda i: mod_ref[:, i * d:(i + 1) * d]
    x = x_ref[...]
    h = _modulated_norm(x, ga_ref[...], mod_ref, 0, 1).astype(BF16)
    gates = jax.nn.sigmoid(_dot(h, wg_ref[...]))
    bounds = (Y_CONV, Y_GQA, Y_NA, Y_MLA, Y_COLS)
    merged = None
    for i in range(N_BRANCH):
        lo, hi = bounds[i], bounds[i + 1]
        term = gates[:, i * d:(i + 1) * d] * _dot(y_ref[:, lo:hi], wb_ref[lo:hi, :])
        merged = term if merged is None else merged + term
    x = x + mod(2) * _dot(merged.astype(BF16), wo_ref[...])
    h = _modulated_norm(x, gm_ref[...], mod_ref, 3, 4).astype(BF16)
    f = jnp.square(jnp.maximum(_dot(h, w1_ref[...]), 0.0)).astype(BF16)
    x = x + mod(5) * _dot(f, w2_ref[...])
    o_ref[...] = _rms(x, gf_ref[...]) if final else x


def _post_call(x, y, mods, g_attn, g_mlp, g_final, w_gates, w_br, w_o, w_ff1, w_ff2, layer, ctx, name):
    n, t, _ = x.shape
    tm = 256
    return pl.pallas_call(
        functools.partial(_post_kernel, final=layer == DEPTH - 1),
        grid=(n, t // tm),
        in_specs=[
            pl.BlockSpec((None, tm, D_MODEL), lambda s, i: (s, i, 0)),
            pl.BlockSpec((None, tm, Y_COLS), lambda s, i: (s, i, 0)),
            _mod_spec(layer, ctx, 2),
            _layer_spec((1, D_MODEL), layer, 2),
            _layer_spec((1, D_MODEL), layer, 2),
            pl.BlockSpec((1, D_MODEL), lambda s, i: (0, 0)),
            _layer_spec((D_MODEL, N_BRANCH * D_MODEL), layer, 2),
            _layer_spec((Y_COLS, D_MODEL), layer, 2),
            _layer_spec((D_MODEL, D_MODEL), layer, 2),
            _layer_spec((D_MODEL, D_FF), layer, 2),
            _layer_spec((D_FF, D_MODEL), layer, 2),
        ],
        out_specs=pl.BlockSpec((None, tm, D_MODEL), lambda s, i: (s, i, 0)),
        out_shape=jax.ShapeDtypeStruct((n, t, D_MODEL), F32),
        compiler_params=pltpu.CompilerParams(
            dimension_semantics=("arbitrary", "arbitrary"), vmem_limit_bytes=VMEM_LIMIT),
        name=name,
    )(x, y, mods, g_attn, g_mlp, g_final, w_gates, w_br, w_o, w_ff1, w_ff2)


def kernel(x_prompt, x_sample, cache_gqa_k, cache_gqa_v, cache_na_k, cache_na_v, cache_mla_ckv, cache_mla_krope, c, c_ctx, w_mod, b_mod, g_attn, g_mlp, w_in, w_conv, gqa_sink, na_rpb, mla_g_q, mla_w_uq, mla_g_kv, mla_w_ukv, w_branch_conv, w_branch_gqa, w_branch_na, w_branch_mla, w_o, w_ff1, w_ff2, g_final):
    w_main = w_in[:, :, :MAIN_COLS].astype(BF16)
    w_gates = w_in[:, :, GATE_COL0:].astype(BF16)
    uq = mla_w_uq.reshape(DEPTH, MLA_Q_RANK, MLA_HEADS, MLA_NOPE + MLA_ROPE)
    w_uq = jnp.concatenate([uq[..., :MLA_NOPE].reshape(DEPTH, MLA_Q_RANK, -1),
                            uq[..., MLA_NOPE:].reshape(DEPTH, MLA_Q_RANK, -1)], axis=-1).astype(BF16)
    ukv = mla_w_ukv.reshape(DEPTH, MLA_KV_RANK, MLA_HEADS, MLA_NOPE + MLA_V)
    w_uk_t = jnp.transpose(ukv[..., :MLA_NOPE].reshape(DEPTH, MLA_KV_RANK, -1), (0, 2, 1)).astype(BF16)
    w_uv = ukv[..., MLA_NOPE:].reshape(DEPTH, MLA_KV_RANK, -1).astype(BF16)
    w_br = jnp.concatenate([w_branch_conv, w_branch_gqa, w_branch_na, w_branch_mla], axis=1).astype(BF16)
    w_o_b, w_ff1_b, w_ff2_b = w_o.astype(BF16), w_ff1.astype(BF16), w_ff2.astype(BF16)
    g_a, g_m = g_attn[:, None, :], g_mlp[:, None, :]
    g_q, g_kv, g_f = mla_g_q[:, None, :], mla_g_kv[:, None, :], g_final[None, :]

    c16 = jnp.concatenate([c, c_ctx[None, :], jnp.zeros((MOD_ROWS - DEC_BATCH - 1, D_MODEL), F32)], axis=0)
    mods = _mod_call(c16, w_mod, b_mod).reshape(DEPTH, MOD_ROWS, 1, 6 * D_MODEL)
    nab = _nab_call(na_rpb)

    heads_t = lambda a: jnp.transpose(a, (0, 1, 3, 4, 2)).reshape(a.shape[0], DEPTH, -1, a.shape[2])
    caches = (heads_t(cache_gqa_k), heads_t(cache_gqa_v), heads_t(cache_na_k), heads_t(cache_na_v),
              cache_mla_ckv, jnp.transpose(cache_mla_krope, (0, 1, 3, 2)))

    h_ctx, h_lat = x_prompt, x_sample
    states = None
    for l in range(DEPTH):
        mixer_w = (gqa_sink, w_conv, g_q, w_uq, g_kv, w_uk_t, w_uv)
        post_w = (g_a, g_m, g_f, w_gates, w_br, w_o_b, w_ff1_b, w_ff2_b)

        y, states = _ctx_mixer_call(h_ctx, mods, g_a, w_main, *mixer_w, states, l, f"mixer_ctx_{l}")
        h_ctx = _post_call(h_ctx, y, mods, *post_w, l, True, f"post_ctx_{l}")

        y = _lat_mixer_call(h_lat, mods, g_a, w_main, *mixer_w, nab, caches, l, f"mixer_lat_{l}")
        h_lat = _post_call(h_lat, y, mods, *post_w, l, False, f"post_lat_{l}")

    def heads_out(a, heads):
        return jnp.transpose(a.reshape(BATCH, DEPTH, heads, HEAD_DIM, SEQ), (0, 1, 4, 2, 3))

    kg, vg, kn, vn, ckv, kr = states
    return (h_ctx, h_lat, heads_out(kg, GQA_KV_HEADS), heads_out(vg, GQA_KV_HEADS),
            heads_out(kn, NA_HEADS), heads_out(vn, NA_HEADS), ckv, jnp.transpose(kr, (0, 1, 3, 2)))
```

```python
import functools
import math

import numpy as np
import jax
import jax.numpy as jnp
from jax import lax
from jax.experimental import pallas as pl
from jax.experimental.pallas import tpu as pltpu

D_MODEL = 1024
BATCH = 32
SEQ = 256
DEPTH = 2
DEC_BATCH = 8
DEC_SEQ = 1024
PAST_LEN = 512
GRID_W = 64
GRID_ROWS = DEC_SEQ // GRID_W
HEAD_DIM = 64
CONV_WIDTH = 256
CONV_K = 3
GQA_HEADS = 8
GQA_KV_HEADS = 2
WINDOW = 128
BAND_BLOCK = 128
NA_HEADS = 4
NA_WIN_H = 8
NA_WIN_W = 16
MLA_HEADS = 4
MLA_Q_RANK = 256
MLA_KV_RANK = 128
MLA_NOPE = 64
MLA_ROPE = 32
MLA_V = 64
D_FF = 4 * D_MODEL
N_BRANCH = 4
ROPE_BASE = 10000.0
EPS = 1e-6
NEG_INF = -1e30
LOG2E = math.log2(math.e)
ATTN_SCALE = HEAD_DIM ** -0.5
MLA_SCALE = (MLA_NOPE + MLA_ROPE) ** -0.5

LANES = 128
MOD_ROWS = 16
CTX_MOD_ROW = DEC_BATCH

C_CB, C_CC, C_CV = 0, 256, 512
C_GQ, C_GK, C_GV = 768, 1280, 1408
C_NQ, C_NK, C_NV = 1536, 1792, 2048
C_MQ, C_MKV, C_MKR = 2304, 2560, 2688
MAIN_COLS = 2816
GATE_COL0 = 2720
Y_CONV, Y_GQA, Y_NA, Y_MLA = 0, 256, 768, 1024
Y_COLS = 1280
NA_PATTERNS = 8
NA_KEYS = NA_WIN_H * GRID_W

VMEM_LIMIT = 56 * 1024 * 1024
POST_CHAIN_ROWS = 256

F32 = jnp.float32
BF16 = jnp.bfloat16


def _dot(a, b):
    return jnp.dot(a, b, preferred_element_type=F32)


def _dot_nt(a, b):
    return lax.dot_general(a, b, (((1,), (1,)), ((), ())), preferred_element_type=F32)


def _rms(x, g):
    return x * lax.rsqrt(jnp.mean(x * x, axis=-1, keepdims=True) + EPS) * g


def _lane_lt(shape, n):
    return lax.broadcasted_iota(jnp.int32, shape, len(shape) - 1) < n


def _row_lt(shape, n):
    return lax.broadcasted_iota(jnp.int32, shape, 0) < n


def _row_group(shape, lo, hi):
    row = lax.broadcasted_iota(jnp.int32, shape, 0)
    return (row >= lo) & (row < hi)


def _short_conv(cb, cc, cv, w):
    u = cc * cv
    t = u.shape[0]
    row = lax.broadcasted_iota(jnp.int32, u.shape, 0)
    prev = jnp.where(row == 0, 0.0, pltpu.roll(u, 1, 0))
    nxt = jnp.where(row == t - 1, 0.0, pltpu.roll(u, t - 1, 0))
    return cb * (prev * w[0:1, :] + u * w[1:2, :] + nxt * w[2:3, :])


def _rope(x, cos, sin_hi, sin_lo, half):
    n = x.shape[-1]
    return x * cos + pltpu.roll(x, n - half, 1) * sin_lo + pltpu.roll(x, half, 1) * sin_hi


def _repeat_rope_key(tile):
    k = jnp.where(_lane_lt(tile.shape, MLA_ROPE), tile, 0.0)
    k = k + pltpu.roll(k, MLA_ROPE, 1)
    return k + pltpu.roll(k, 2 * MLA_ROPE, 1)


def _mod_kernel(c_ref, w_ref, b_ref, o_ref):
    c = c_ref[...]
    s = c * jax.nn.sigmoid(c)
    o_ref[...] = _dot(s.astype(BF16), w_ref[...].astype(BF16)) + b_ref[...]


def _mod_call(c16, w_mod, b_mod):
    tn = 1536
    return pl.pallas_call(
        _mod_kernel,
        grid=(DEPTH, 6 * D_MODEL // tn),
        in_specs=[
            pl.BlockSpec((MOD_ROWS, D_MODEL), lambda l, j: (0, 0)),
            pl.BlockSpec((None, D_MODEL, tn), lambda l, j: (l, 0, j)),
            pl.BlockSpec((None, 1, tn), lambda l, j: (l, 0, j)),
        ],
        out_specs=pl.BlockSpec((None, MOD_ROWS, tn), lambda l, j: (l, 0, j)),
        out_shape=jax.ShapeDtypeStruct((DEPTH, MOD_ROWS, 6 * D_MODEL), F32),
        compiler_params=pltpu.CompilerParams(
            dimension_semantics=("arbitrary", "arbitrary"), vmem_limit_bytes=VMEM_LIMIT),
        name="adaln_mod",
    )(c16, w_mod, b_mod.reshape(DEPTH, 1, 6 * D_MODEL))


def _nab_kernel(rpb_ref, o_ref):
    l = pl.program_id(0)
    c = lax.broadcasted_iota(jnp.int32, (GRID_W, GRID_W), 0)
    w = lax.broadcasted_iota(jnp.int32, (GRID_W, GRID_W), 1)
    dc = w - c + (NA_WIN_W - 1)
    c0 = jnp.clip(c - NA_WIN_W // 2, 0, GRID_W - NA_WIN_W)
    outside = (w < c0) | (w >= c0 + NA_WIN_W)
    n_dr, n_dc = 2 * NA_WIN_H - 1, 2 * NA_WIN_W - 1
    for h in range(NA_HEADS):
        for d in range(n_dr):
            base = ((l * NA_HEADS + h) * n_dr + d) * n_dc
            tile = jnp.full((GRID_W, GRID_W), NEG_INF, F32)
            for j in range(n_dc):
                tile = jnp.where(dc == j, rpb_ref[base + j] * LOG2E, tile)
            tile = jnp.where(outside, NEG_INF, tile)
            for p in range(NA_PATTERNS):
                a = d - (NA_WIN_H - 1) + p
                if 0 <= a < NA_WIN_H:
                    col = h * NA_KEYS + a * GRID_W
                    o_ref[p, :, col:col + GRID_W] = tile


def _nab_call(na_rpb):
    return pl.pallas_call(
        _nab_kernel,
        grid=(DEPTH,),
        in_specs=[pl.BlockSpec(memory_space=pltpu.SMEM)],
        out_specs=pl.BlockSpec((None, NA_PATTERNS, GRID_W, NA_HEADS * NA_KEYS), lambda l: (l, 0, 0, 0)),
        out_shape=jax.ShapeDtypeStruct((DEPTH, NA_PATTERNS, GRID_W, NA_HEADS * NA_KEYS), F32),
        compiler_params=pltpu.CompilerParams(dimension_semantics=("arbitrary",)),
        name="na_bias_tables",
    )(na_rpb.reshape(-1))


def _mod_spec(layer, ctx, n_grid):
    if n_grid == 1:
        index = (lambda s: (layer, CTX_MOD_ROW, 0, 0)) if ctx else (lambda s: (layer, s, 0, 0))
    else:
        index = (lambda s, i: (layer, CTX_MOD_ROW, 0, 0)) if ctx else (lambda s, i: (layer, s, 0, 0))
    return pl.BlockSpec((None, None, 1, 6 * D_MODEL), index)


def _layer_spec(shape, layer, n_grid):
    zeros = (0,) * len(shape)
    index = (lambda s: (layer,) + zeros) if n_grid == 1 else (lambda s, i: (layer,) + zeros)
    return pl.BlockSpec((None,) + tuple(shape), index, pipeline_mode=pl.Buffered(1))


_COLUMN_GROUPS = ((C_CB, C_GQ), (C_GQ, C_NQ), (C_NQ, C_MQ), (C_MQ, MAIN_COLS))


def _modulated_norm(x, g, mod_ref, shift, scale):
    d = D_MODEL
    return _rms(x, g) * (1.0 + mod_ref[:, scale * d:(scale + 1) * d]) + mod_ref[:, shift * d:(shift + 1) * d]


def _projected_columns(h_ref, w_ref):
    cache = {}

    def col(c, n):
        lo, hi = next(g for g in _COLUMN_GROUPS if g[0] <= c and c + n <= g[1])
        if lo not in cache:
            cache.clear()
            cache[lo] = _dot(h_ref[...], w_ref[:, lo:hi])
        return cache[lo][:, c - lo:c - lo + n]

    return col


def _key_planes(k_t, even_first, odd_first):
    lo = _row_lt(k_t.shape, HEAD_DIM)
    swapped = None
    if not even_first or odd_first:
        swapped = pltpu.roll(k_t, HEAD_DIM, 0)
    top = jnp.where(lo, k_t if even_first else swapped, 0.0)
    bot = jnp.where(lo, 0.0, swapped if odd_first else k_t)
    return top.astype(BF16), bot.astype(BF16)


def _value_planes(pair, even_first, odd_first):
    lo = _lane_lt(pair.shape, HEAD_DIM)
    swapped = None
    if not even_first or odd_first:
        swapped = pltpu.roll(pair, HEAD_DIM, 1)
    top = jnp.where(lo, pair if even_first else swapped, 0.0)
    bot = jnp.where(lo, 0.0, swapped if odd_first else pair)
    one_e = jnp.where(lo, 1.0, 0.0)
    return (jnp.concatenate([top, one_e], axis=1).astype(BF16),
            jnp.concatenate([bot, 1.0 - one_e], axis=1).astype(BF16))


def _probabilities(parts, sink=None):
    m = parts[0].max(axis=-1, keepdims=True)
    for p in parts[1:]:
        m = jnp.maximum(m, p.max(axis=-1, keepdims=True))
    if sink is not None:
        m = jnp.maximum(m, sink)
    probs = [jnp.exp2((p - m).astype(BF16)) for p in parts]
    return probs, (None if sink is None else jnp.exp2(sink - m))


def _attend(terms, sink_e=None, sink_o=None):
    o = None
    for p, v in terms:
        t = _dot(p, v)
        o = t if o is None else o + t
    den = o[:, LANES:]
    if sink_e is not None:
        den = den + jnp.where(_lane_lt(den.shape, HEAD_DIM), sink_e, sink_o)
    return o[:, :LANES] / den


def _mla_key_plane(kn_t_pair, kr4_t, j, odd):
    h = 2 * j + odd
    lo = _row_lt(kn_t_pair.shape, HEAD_DIM)
    nope = jnp.where(lo, 0.0, kn_t_pair) if odd else jnp.where(lo, kn_t_pair, 0.0)
    rope = jnp.where(_row_group(kr4_t.shape, h * MLA_ROPE, (h + 1) * MLA_ROPE), kr4_t, 0.0)
    return jnp.concatenate([nope, rope], axis=0).astype(BF16)


def _ctx_mixer_kernel(sink_ref, x_ref, mod_ref, ga_ref, win_ref, wc_ref, gq_ref, wuq_ref, gkv_ref, wukt_ref, wuv_ref,
                      *rest, layer, first):
    y_ref, kg_ref, vg_ref, kn_ref, vn_ref, ckv_ref, kr_ref, h_s = rest[-8:]
    t = SEQ
    h_s[...] = _modulated_norm(x_ref[...], ga_ref[...], mod_ref, 0, 1).astype(BF16)
    col = _projected_columns(h_s, win_ref)

    def put_state(ref, val):
        if first:
            for d in range(DEPTH):
                ref[d] = val if d == layer else jnp.zeros_like(val)
        else:
            ref[...] = val

    y_ref[:, Y_CONV:Y_CONV + CONV_WIDTH] = _short_conv(
        col(C_CB, CONV_WIDTH), col(C_CC, CONV_WIDTH), col(C_CV, CONV_WIDTH), wc_ref[...]).astype(BF16)

    kg_t, vpair = col(C_GK, LANES).T, col(C_GV, LANES)
    put_state(kg_ref, kg_t)
    put_state(vg_ref, vpair.T)
    row2 = lax.broadcasted_iota(jnp.int32, (2 * t, 1), 0) < t
    for g in range(GQA_KV_HEADS):
        ke, ko = _key_planes(kg_t, g == 0, g == 0)
        ve, vo = _value_planes(vpair, g == 0, g == 0)
        q = jnp.concatenate([col(C_GQ + (2 * g) * LANES, LANES), col(C_GQ + (2 * g + 1) * LANES, LANES)], axis=0)
        q = (q * (ATTN_SCALE * LOG2E)).astype(BF16)
        s = _dot(q, jnp.concatenate([ke, ko], axis=1))
        sink = [jnp.where(row2, sink_ref[layer, 4 * g + odd], sink_ref[layer, 4 * g + 2 + odd]) * LOG2E
                for odd in range(2)]
        (pe,), xe = _probabilities([s[:, 0:t]], sink[0])
        (po,), xo = _probabilities([s[:, t:2 * t]], sink[1])
        o = _attend([(pe, ve), (po, vo)], xe, xo).astype(BF16)
        c0 = Y_GQA + (2 * g) * LANES
        y_ref[:, c0:c0 + LANES] = o[0:t]
        y_ref[:, c0 + LANES:c0 + 2 * LANES] = o[t:2 * t]

    kn_t = col(C_NK, 2 * LANES).T
    put_state(kn_ref, kn_t)
    put_state(vn_ref, col(C_NV, 2 * LANES).T)
    for j in range(NA_HEADS // 2):
        ke, ko = _key_planes(kn_t[j * LANES:(j + 1) * LANES, :], True, False)
        ve, vo = _value_planes(col(C_NV + j * LANES, LANES), True, False)
        q = (col(C_NQ + j * LANES, LANES) * (ATTN_SCALE * LOG2E)).astype(BF16)
        s = _dot(q, jnp.concatenate([ke, ko], axis=1))
        (pe,), _ = _probabilities([s[:, 0:t]])
        (po,), _ = _probabilities([s[:, t:2 * t]])
        y_ref[:, Y_NA + j * LANES:Y_NA + (j + 1) * LANES] = _attend([(pe, ve), (po, vo)]).astype(BF16)

    ckv = _rms(col(C_MKV, MLA_KV_RANK), gkv_ref[...])
    put_state(ckv_ref, ckv)
    kr_t = col(C_MKR, LANES).T[0:MLA_ROPE, :]
    put_state(kr_ref, kr_t)
    ckv_b = ckv.astype(BF16)
    q = _dot(_rms(col(C_MQ, MLA_Q_RANK), gq_ref[...]).astype(BF16), wuq_ref[...]) * (MLA_SCALE * LOG2E)
    kn_t_all = _dot_nt(wukt_ref[...], ckv_b)
    v_all = _dot(ckv_b, wuv_ref[...])
    kr4_t = jnp.concatenate([kr_t] * MLA_HEADS, axis=0)
    q_rope = q[:, 2 * LANES:3 * LANES]
    for j in range(MLA_HEADS // 2):
        qj = jnp.concatenate([q[:, j * LANES:(j + 1) * LANES], q_rope], axis=1).astype(BF16)
        kn_t_pair = kn_t_all[j * LANES:(j + 1) * LANES, :]
        keys = jnp.concatenate([_mla_key_plane(kn_t_pair, kr4_t, j, 0), _mla_key_plane(kn_t_pair, kr4_t, j, 1)], axis=1)
        s = _dot(qj, keys)
        ve, vo = _value_planes(v_all[:, j * LANES:(j + 1) * LANES], True, False)
        (pe,), _ = _probabilities([s[:, 0:t]])
        (po,), _ = _probabilities([s[:, t:2 * t]])
        y_ref[:, Y_MLA + j * LANES:Y_MLA + (j + 1) * LANES] = _attend([(pe, ve), (po, vo)]).astype(BF16)


def _ctx_mixer_call(x, mods, g_attn, w_main, sink, w_conv, g_q, w_uq, g_kv, w_uk_t, w_uv, prev_states, layer, name):
    n = BATCH
    state_shapes = ((2 * HEAD_DIM, SEQ), (2 * HEAD_DIM, SEQ), (4 * HEAD_DIM, SEQ), (4 * HEAD_DIM, SEQ),
                    (SEQ, MLA_KV_RANK), (MLA_ROPE, SEQ))
    first = prev_states is None
    if first:
        state_spec = lambda shape: pl.BlockSpec((None, DEPTH) + shape, lambda s: (s, 0, 0, 0))
    else:
        state_spec = lambda shape: pl.BlockSpec((None, None) + shape, lambda s: (s, layer, 0, 0))
    in_specs = [
        pl.BlockSpec(memory_space=pltpu.SMEM),
        pl.BlockSpec((None, SEQ, D_MODEL), lambda s: (s, 0, 0)),
        _mod_spec(layer, True, 1),
        _layer_spec((1, D_MODEL), layer, 1),
        _layer_spec((D_MODEL, MAIN_COLS), layer, 1),
        _layer_spec((CONV_K, CONV_WIDTH), layer, 1),
        _layer_spec((1, MLA_Q_RANK), layer, 1),
        _layer_spec((MLA_Q_RANK, 3 * LANES), layer, 1),
        _layer_spec((1, MLA_KV_RANK), layer, 1),
        _layer_spec((2 * LANES, MLA_KV_RANK), layer, 1),
        _layer_spec((MLA_KV_RANK, 2 * LANES), layer, 1),
    ]
    args = [sink, x, mods, g_attn, w_main, w_conv, g_q, w_uq, g_kv, w_uk_t, w_uv]
    aliases = {}
    if not first:
        for i, st in enumerate(prev_states):
            aliases[len(args)] = 1 + i
            in_specs.append(pl.BlockSpec(memory_space=pl.ANY))
            args.append(st)
    outs = pl.pallas_call(
        functools.partial(_ctx_mixer_kernel, layer=layer, first=first),
        grid=(n,),
        in_specs=in_specs,
        out_specs=[pl.BlockSpec((None, SEQ, Y_COLS), lambda s: (s, 0, 0))] + [state_spec(s) for s in state_shapes],
        out_shape=[jax.ShapeDtypeStruct((n, SEQ, Y_COLS), BF16)]
        + [jax.ShapeDtypeStruct((n, DEPTH) + s, F32) for s in state_shapes],
        input_output_aliases=aliases,
        scratch_shapes=[pltpu.VMEM((SEQ, D_MODEL), BF16)],
        compiler_params=pltpu.CompilerParams(
            dimension_semantics=("arbitrary",), vmem_limit_bytes=VMEM_LIMIT),
        name=name,
    )(*args)
    return outs[0], outs[1:]


def _lat_mixer_kernel(sink_ref, x_ref, mod_ref, ga_ref, win_ref, wc_ref, gq_ref, wuq_ref, gkv_ref, wukt_ref, wuv_ref,
                      rc_ref, rsh_ref, rsl_ref, mc_ref, msh_ref, msl_ref, band_ref, nab_ref,
                      cgk_ref, cgv_ref, cnk_ref, cnv_ref, cckv_ref, ckr_ref,
                      y_ref,
                      h_s, gq_s, gk_s, gv_s, gkc_s, gvc_s, nq_s, nk_s, nv_s, nkc_s, nvc_s, mq_s, mk_s, mv_s, *, layer):
    t = DEC_SEQ
    h_s[...] = _modulated_norm(x_ref[...], ga_ref[...], mod_ref, 0, 1).astype(BF16)
    col = _projected_columns(h_s, win_ref)
    rope64 = lambda x: _rope(x, rc_ref[...], rsh_ref[...], rsl_ref[...], HEAD_DIM // 4)
    rope32 = lambda x: _rope(x, mc_ref[...], msh_ref[...], msl_ref[...], MLA_ROPE // 4)

    y_ref[:, Y_CONV:Y_CONV + CONV_WIDTH] = _short_conv(
        col(C_CB, CONV_WIDTH), col(C_CC, CONV_WIDTH), col(C_CV, CONV_WIDTH), wc_ref[...]).astype(BF16)

    for j in range(GQA_HEADS // 2):
        gq_s[:, j * LANES:(j + 1) * LANES] = (
            rope64(col(C_GQ + j * LANES, LANES)) * (ATTN_SCALE * LOG2E)).astype(BF16)
    k_t, vpair = rope64(col(C_GK, LANES)).T, col(C_GV, LANES)
    kc_t, vc_pair = cgk_ref[...], cgv_ref[...].T
    for g in range(GQA_KV_HEADS):
        gk_s[2 * g], gk_s[2 * g + 1] = _key_planes(k_t, g == 0, g == 0)
        gv_s[2 * g], gv_s[2 * g + 1] = _value_planes(vpair, g == 0, g == 0)
        gkc_s[g] = jnp.concatenate(_key_planes(kc_t, g == 0, g == 0), axis=1)
        gvc_s[2 * g], gvc_s[2 * g + 1] = _value_planes(vc_pair, g == 0, g == 0)

    def gqa_block(b, c_lo, c_hi):
        q0 = pl.multiple_of(b * BAND_BLOCK, BAND_BLOCK)
        k0 = pl.multiple_of(q0 + (c_lo - WINDOW), BAND_BLOCK)
        n = c_hi - c_lo
        band = band_ref[:, c_lo:c_hi]
        row2 = lax.broadcasted_iota(jnp.int32, (2 * BAND_BLOCK, 1), 0) < BAND_BLOCK
        for g in range(GQA_KV_HEADS):
            q = jnp.concatenate([gq_s[pl.ds(q0, BAND_BLOCK), (2 * g) * LANES:(2 * g + 1) * LANES],
                                 gq_s[pl.ds(q0, BAND_BLOCK), (2 * g + 1) * LANES:(2 * g + 2) * LANES]], axis=0)
            s_ctx = _dot(q, gkc_s[g])
            terms, sinks = [], []
            for odd in range(2):
                s_loc = _dot(q, gk_s[2 * g + odd, :, pl.ds(k0, n)]) + band
                sink = jnp.where(row2, sink_ref[layer, 4 * g + odd], sink_ref[layer, 4 * g + 2 + odd]) * LOG2E
                (p_loc, p_ctx), x = _probabilities([s_loc, s_ctx[:, odd * PAST_LEN:(odd + 1) * PAST_LEN]], sink)
                terms += [(p_loc, gv_s[2 * g + odd, pl.ds(k0, n), :]), (p_ctx, gvc_s[2 * g + odd])]
                sinks.append(x)
            o = _attend(terms, sinks[0], sinks[1]).astype(BF16)
            c0 = Y_GQA + (2 * g) * LANES
            y_ref[pl.ds(q0, BAND_BLOCK), c0:c0 + LANES] = o[0:BAND_BLOCK]
            y_ref[pl.ds(q0, BAND_BLOCK), c0 + LANES:c0 + 2 * LANES] = o[BAND_BLOCK:2 * BAND_BLOCK]

    span = BAND_BLOCK + 2 * WINDOW
    nb = t // BAND_BLOCK
    gqa_block(0, WINDOW, span)
    lax.fori_loop(1, nb - 1, lambda b, c: (gqa_block(b, 0, span), c)[1], 0, unroll=True)
    gqa_block(nb - 1, 0, span - WINDOW)

    for j in range(NA_HEADS // 2):
        nq_s[:, j * LANES:(j + 1) * LANES] = (col(C_NQ + j * LANES, LANES) * (ATTN_SCALE * LOG2E)).astype(BF16)
        nk_t = col(C_NK + j * LANES, LANES).T
        for shift, k_t in enumerate((nk_t, pltpu.roll(nk_t, t - GRID_W, 1))):
            nk_s[shift, 2 * j], nk_s[shift, 2 * j + 1] = _key_planes(k_t, True, False)
        nv_s[2 * j], nv_s[2 * j + 1] = _value_planes(col(C_NV + j * LANES, LANES), True, False)
        nkc_s[j] = jnp.concatenate(_key_planes(cnk_ref[j * LANES:(j + 1) * LANES, :], True, False), axis=1)
        nvc_s[2 * j], nvc_s[2 * j + 1] = _value_planes(cnv_ref[j * LANES:(j + 1) * LANES, :].T, True, False)

    def na_row(r, carry):
        q0 = pl.multiple_of(r * GRID_W, GRID_W)
        r0 = jnp.clip(r - NA_WIN_H // 2, 0, GRID_ROWS - NA_WIN_H)
        k0 = pl.multiple_of(r0 * GRID_W, GRID_W)
        kt0 = pl.multiple_of(lax.shift_right_logical(r0, 1) * LANES, LANES)
        pat = r - r0
        for j in range(NA_HEADS // 2):
            q = nq_s[pl.ds(q0, GRID_W), j * LANES:(j + 1) * LANES]
            s_ctx = _dot(q, nkc_s[j])
            terms = []
            for odd in range(2):
                h = 2 * j + odd
                s_loc = _dot(q, nk_s[r0 & 1, h, :, pl.ds(kt0, NA_KEYS)])
                s_loc = s_loc + nab_ref[pat, :, h * NA_KEYS:(h + 1) * NA_KEYS]
                (p_loc, p_ctx), _ = _probabilities([s_loc, s_ctx[:, odd * PAST_LEN:(odd + 1) * PAST_LEN]])
                terms += [(p_loc, nv_s[h, pl.ds(k0, NA_KEYS), :]), (p_ctx, nvc_s[h])]
            y_ref[pl.ds(q0, GRID_W), Y_NA + j * LANES:Y_NA + (j + 1) * LANES] = _attend(terms).astype(BF16)
        return carry

    lax.fori_loop(0, GRID_ROWS, na_row, 0, unroll=True)

    ckv_b = _rms(col(C_MKV, MLA_KV_RANK), gkv_ref[...]).astype(BF16)
    cckv_b = cckv_ref[...].astype(BF16)
    q = _dot(_rms(col(C_MQ, MLA_Q_RANK), gq_ref[...]).astype(BF16), wuq_ref[...]) * (MLA_SCALE * LOG2E)
    mq_s[:, 0:2 * LANES] = q[:, 0:2 * LANES].astype(BF16)
    mq_s[:, 2 * LANES:3 * LANES] = rope32(q[:, 2 * LANES:3 * LANES]).astype(BF16)
    kr4_t = rope32(_repeat_rope_key(col(C_MKR, LANES))).T
    kr4_t_c = jnp.concatenate([ckr_ref[...]] * MLA_HEADS, axis=0)
    for cols, ckv_x, kr_x in ((slice(0, t), ckv_b, kr4_t), (slice(t, t + PAST_LEN), cckv_b, kr4_t_c)):
        kn_t_all = _dot_nt(wukt_ref[...], ckv_x)
        v_all = _dot(ckv_x, wuv_ref[...])
        for j in range(MLA_HEADS // 2):
            kn_t_pair = kn_t_all[j * LANES:(j + 1) * LANES, :]
            mv_s[2 * j, cols, :], mv_s[2 * j + 1, cols, :] = _value_planes(
                v_all[:, j * LANES:(j + 1) * LANES], True, False)
            for odd in range(2):
                mk_s[2 * j + odd, :, cols] = _mla_key_plane(kn_t_pair, kr_x, j, odd)

    tq = 256

    def mla_block(i, carry):
        q0 = pl.multiple_of(i * tq, tq)
        for j in range(MLA_HEADS // 2):
            qj = jnp.concatenate([mq_s[pl.ds(q0, tq), j * LANES:(j + 1) * LANES],
                                  mq_s[pl.ds(q0, tq), 2 * LANES:3 * LANES]], axis=1)
            (pe,), _ = _probabilities([_dot(qj, mk_s[2 * j])])
            (po,), _ = _probabilities([_dot(qj, mk_s[2 * j + 1])])
            o = _attend([(pe, mv_s[2 * j]), (po, mv_s[2 * j + 1])])
            y_ref[pl.ds(q0, tq), Y_MLA + j * LANES:Y_MLA + (j + 1) * LANES] = o.astype(BF16)
        return carry

    lax.fori_loop(0, t // tq, mla_block, 0, unroll=True)


def _rope_tables(group, half):
    tok = np.arange(DEC_SEQ)
    pos = np.stack([tok // GRID_W, tok % GRID_W], axis=1).astype(np.float64)
    inv = ROPE_BASE ** (-np.arange(half, dtype=np.float64) / half)
    lane = np.arange(LANES) % group
    axis = lane // (2 * half)
    within = lane % (2 * half)
    ang = pos[:, axis] * inv[within % half][None, :]
    cos, sin = np.cos(ang), np.sin(ang)
    upper = (within >= half)[None, :]
    sin_hi = np.where(upper, sin, 0.0)
    sin_lo = np.where(upper, 0.0, -sin)
    return tuple(jnp.asarray(a, dtype=F32) for a in (cos, sin_hi, sin_lo))


def _band_mask():
    i = np.arange(BAND_BLOCK)[:, None]
    c = np.arange(BAND_BLOCK + 2 * WINDOW)[None, :]
    ok = (c >= i) & (c <= i + 2 * WINDOW)
    m = np.where(ok, 0.0, NEG_INF)
    return jnp.asarray(np.concatenate([m, m], axis=0), dtype=F32)


def _lat_mixer_call(x, mods, g_attn, w_main, sink, w_conv, g_q, w_uq, g_kv, w_uk_t, w_uv, nab, caches, layer, name):
    n, t = DEC_BATCH, DEC_SEQ
    one = pl.Buffered(1)
    const = lambda shape: pl.BlockSpec(shape, lambda s: (0,) * len(shape), pipeline_mode=one)
    cache = lambda shape: pl.BlockSpec((None, None) + shape, lambda s: (s, layer, 0, 0), pipeline_mode=one)
    rope_g = _rope_tables(HEAD_DIM, HEAD_DIM // 4)
    rope_m = _rope_tables(MLA_ROPE, MLA_ROPE // 4)
    span = BAND_BLOCK + 2 * WINDOW
    keys = t + PAST_LEN
    scratch = [
        pltpu.VMEM((t, D_MODEL), BF16),
        pltpu.VMEM((t, GQA_HEADS // 2 * LANES), BF16),
        pltpu.VMEM((2 * GQA_KV_HEADS, LANES, t), BF16),
        pltpu.VMEM((2 * GQA_KV_HEADS, t, 2 * LANES), BF16),
        pltpu.VMEM((GQA_KV_HEADS, LANES, 2 * PAST_LEN), BF16),
        pltpu.VMEM((2 * GQA_KV_HEADS, PAST_LEN, 2 * LANES), BF16),
        pltpu.VMEM((t, NA_HEADS // 2 * LANES), BF16),
        pltpu.VMEM((2, NA_HEADS, LANES, t), BF16),
        pltpu.VMEM((NA_HEADS, t, 2 * LANES), BF16),
        pltpu.VMEM((NA_HEADS // 2, LANES, 2 * PAST_LEN), BF16),
        pltpu.VMEM((NA_HEADS, PAST_LEN, 2 * LANES), BF16),
        pltpu.VMEM((t, 3 * LANES), BF16),
        pltpu.VMEM((MLA_HEADS, 2 * LANES, keys), BF16),
        pltpu.VMEM((MLA_HEADS, keys, 2 * LANES), BF16),
    ]
    return pl.pallas_call(
        functools.partial(_lat_mixer_kernel, layer=layer),
        grid=(n,),
        in_specs=[
            pl.BlockSpec(memory_space=pltpu.SMEM),
            pl.BlockSpec((None, t, D_MODEL), lambda s: (s, 0, 0), pipeline_mode=one),
            _mod_spec(layer, False, 1),
            _layer_spec((1, D_MODEL), layer, 1),
            _layer_spec((D_MODEL, MAIN_COLS), layer, 1),
            _layer_spec((CONV_K, CONV_WIDTH), layer, 1),
            _layer_spec((1, MLA_Q_RANK), layer, 1),
            _layer_spec((MLA_Q_RANK, 3 * LANES), layer, 1),
            _layer_spec((1, MLA_KV_RANK), layer, 1),
            _layer_spec((2 * LANES, MLA_KV_RANK), layer, 1),
            _layer_spec((MLA_KV_RANK, 2 * LANES), layer, 1),
        ] + [const((t, LANES))] * 6 + [
            const((2 * BAND_BLOCK, span)),
            _layer_spec((NA_PATTERNS, GRID_W, NA_HEADS * NA_KEYS), layer, 1),
            cache((2 * HEAD_DIM, PAST_LEN)), cache((2 * HEAD_DIM, PAST_LEN)),
            cache((4 * HEAD_DIM, PAST_LEN)), cache((4 * HEAD_DIM, PAST_LEN)),
            cache((PAST_LEN, MLA_KV_RANK)), cache((MLA_ROPE, PAST_LEN)),
        ],
        out_specs=pl.BlockSpec((None, t, Y_COLS), lambda s: (s, 0, 0)),
        out_shape=jax.ShapeDtypeStruct((n, t, Y_COLS), BF16),
        scratch_shapes=scratch,
        compiler_params=pltpu.CompilerParams(
            dimension_semantics=("arbitrary",), vmem_limit_bytes=VMEM_LIMIT),
        name=name,
    )(sink, x, mods, g_attn, w_main, w_conv, g_q, w_uq, g_kv, w_uk_t, w_uv, *rope_g, *rope_m, _band_mask(), nab, *caches)


def _post_kernel(x_ref, y_ref, mod_ref, ga_ref, gm_ref, gf_ref, wg_ref, wb_ref, wo_ref, w1_ref, w2_ref,
                 o_ref, *, final):
    d = D_MODEL
    mod = lambda i: mod_ref[:, i * d:(i + 1) * d]
    bounds = (Y_CONV, Y_GQA, Y_NA, Y_MLA, Y_COLS)
    for r0 in range(0, x_ref.shape[0], POST_CHAIN_ROWS):
        rows = slice(r0, r0 + POST_CHAIN_ROWS)
        x = x_ref[rows, :]
        h = _modulated_norm(x, ga_ref[...], mod_ref, 0, 1).astype(BF16)
        merged = None
        for i in range(N_BRANCH):
            lo, hi = bounds[i], bounds[i + 1]
            gate = jax.nn.sigmoid(_dot(h, wg_ref[:, i * d:(i + 1) * d]))
            term = gate * _dot(y_ref[rows, lo:hi], wb_ref[lo:hi, :])
            merged = term if merged is None else merged + term
        x = x + mod(2) * _dot(merged.astype(BF16), wo_ref[...])
        h = _modulated_norm(x, gm_ref[...], mod_ref, 3, 4).astype(BF16)
        mlp = None
        for c0 in range(0, D_FF, d):
            f = jnp.square(jnp.maximum(_dot(h, w1_ref[:, c0:c0 + d]), 0.0)).astype(BF16)
            term = _dot(f, w2_ref[c0:c0 + d, :])
            mlp = term if mlp is None else mlp + term
        x = x + mod(5) * mlp
        o_ref[rows, :] = _rms(x, gf_ref[...]) if final else x


def _post_call(x, y, mods, g_attn, g_mlp, g_final, w_gates, w_br, w_o, w_ff1, w_ff2, layer, ctx, name):
    n, t, _ = x.shape
    tm = 2 * POST_CHAIN_ROWS
    return pl.pallas_call(
        functools.partial(_post_kernel, final=layer == DEPTH - 1),
        grid=(n, t // tm),
        in_specs=[
            pl.BlockSpec((None, tm, D_MODEL), lambda s, i: (s, i, 0)),
            pl.BlockSpec((None, tm, Y_COLS), lambda s, i: (s, i, 0)),
            _mod_spec(layer, ctx, 2),
            _layer_spec((1, D_MODEL), layer, 2),
            _layer_spec((1, D_MODEL), layer, 2),
            pl.BlockSpec((1, D_MODEL), lambda s, i: (0, 0)),
            _layer_spec((D_MODEL, N_BRANCH * D_MODEL), layer, 2),
            _layer_spec((Y_COLS, D_MODEL), layer, 2),
            _layer_spec((D_MODEL, D_MODEL), layer, 2),
            _layer_spec((D_MODEL, D_FF), layer, 2),
            _layer_spec((D_FF, D_MODEL), layer, 2),
        ],
        out_specs=pl.BlockSpec((None, tm, D_MODEL), lambda s, i: (s, i, 0)),
        out_shape=jax.ShapeDtypeStruct((n, t, D_MODEL), F32),
        compiler_params=pltpu.CompilerParams(
            dimension_semantics=("arbitrary", "arbitrary"), vmem_limit_bytes=VMEM_LIMIT),
        name=name,
    )(x, y, mods, g_attn, g_mlp, g_final, w_gates, w_br, w_o, w_ff1, w_ff2)


def kernel(x_prompt, x_sample, cache_gqa_k, cache_gqa_v, cache_na_k, cache_na_v, cache_mla_ckv, cache_mla_krope, c, c_ctx, w_mod, b_mod, g_attn, g_mlp, w_in, w_conv, gqa_sink, na_rpb, mla_g_q, mla_w_uq, mla_g_kv, mla_w_ukv, w_branch_conv, w_branch_gqa, w_branch_na, w_branch_mla, w_o, w_ff1, w_ff2, g_final):
    w_main = w_in[:, :, :MAIN_COLS].astype(BF16)
    w_gates = w_in[:, :, GATE_COL0:].astype(BF16)
    uq = mla_w_uq.reshape(DEPTH, MLA_Q_RANK, MLA_HEADS, MLA_NOPE + MLA_ROPE)
    w_uq = jnp.concatenate([uq[..., :MLA_NOPE].reshape(DEPTH, MLA_Q_RANK, -1),
                            uq[..., MLA_NOPE:].reshape(DEPTH, MLA_Q_RANK, -1)], axis=-1).astype(BF16)
    ukv = mla_w_ukv.reshape(DEPTH, MLA_KV_RANK, MLA_HEADS, MLA_NOPE + MLA_V)
    w_uk_t = jnp.transpose(ukv[..., :MLA_NOPE].reshape(DEPTH, MLA_KV_RANK, -1), (0, 2, 1)).astype(BF16)
    w_uv = ukv[..., MLA_NOPE:].reshape(DEPTH, MLA_KV_RANK, -1).astype(BF16)
    w_br = jnp.concatenate([w_branch_conv, w_branch_gqa, w_branch_na, w_branch_mla], axis=1).astype(BF16)
    w_o_b, w_ff1_b, w_ff2_b = w_o.astype(BF16), w_ff1.astype(BF16), w_ff2.astype(BF16)
    g_a, g_m = g_attn[:, None, :], g_mlp[:, None, :]
    g_q, g_kv, g_f = mla_g_q[:, None, :], mla_g_kv[:, None, :], g_final[None, :]

    c16 = jnp.concatenate([c, c_ctx[None, :], jnp.zeros((MOD_ROWS - DEC_BATCH - 1, D_MODEL), F32)], axis=0)
    mods = _mod_call(c16, w_mod, b_mod).reshape(DEPTH, MOD_ROWS, 1, 6 * D_MODEL)
    nab = _nab_call(na_rpb)

    heads_t = lambda a: jnp.transpose(a, (0, 1, 3, 4, 2)).reshape(a.shape[0], DEPTH, -1, a.shape[2])
    caches = (heads_t(cache_gqa_k), heads_t(cache_gqa_v), heads_t(cache_na_k), heads_t(cache_na_v),
              cache_mla_ckv, jnp.transpose(cache_mla_krope, (0, 1, 3, 2)))

    h_ctx, h_lat = x_prompt, x_sample
    states = None
    for l in range(DEPTH):
        mixer_w = (gqa_sink, w_conv, g_q, w_uq, g_kv, w_uk_t, w_uv)
        post_w = (g_a, g_m, g_f, w_gates, w_br, w_o_b, w_ff1_b, w_ff2_b)

        y, states = _ctx_mixer_call(h_ctx, mods, g_a, w_main, *mixer_w, states, l, f"mixer_ctx_{l}")
        flat = lambda a: a.reshape(1, BATCH * SEQ, a.shape[-1])
        h_ctx = _post_call(flat(h_ctx), flat(y), mods, *post_w, l, True, f"post_ctx_{l}").reshape(BATCH, SEQ, D_MODEL)

        y = _lat_mixer_call(h_lat, mods, g_a, w_main, *mixer_w, nab, caches, l, f"mixer_lat_{l}")
        h_lat = _post_call(h_lat, y, mods, *post_w, l, False, f"post_lat_{l}")

    def heads_out(a, heads):
        return jnp.transpose(a.reshape(BATCH, DEPTH, heads, HEAD_DIM, SEQ), (0, 1, 4, 2, 3))

    kg, vg, kn, vn, ckv, kr = states
    return (h_ctx, h_lat, heads_out(kg, GQA_KV_HEADS), heads_out(vg, GQA_KV_HEADS),
            heads_out(kn, NA_HEADS), heads_out(vn, NA_HEADS), ckv, jnp.transpose(kr, (0, 1, 3, 2)))
```

```python
import functools
import math

import numpy as np
import jax
import jax.numpy as jnp
from jax import lax
from jax.experimental import pallas as pl
from jax.experimental.pallas import tpu as pltpu

D_MODEL = 1024
BATCH = 32
SEQ = 256
DEPTH = 2
DEC_BATCH = 8
DEC_SEQ = 1024
PAST_LEN = 512
GRID_W = 64
GRID_ROWS = DEC_SEQ // GRID_W
HEAD_DIM = 64
CONV_WIDTH = 256
CONV_K = 3
GQA_HEADS = 8
GQA_KV_HEADS = 2
WINDOW = 128
BAND_BLOCK = 128
NA_HEADS = 4
NA_WIN_H = 8
NA_WIN_W = 16
MLA_HEADS = 4
MLA_Q_RANK = 256
MLA_KV_RANK = 128
MLA_NOPE = 64
MLA_ROPE = 32
MLA_V = 64
D_FF = 4 * D_MODEL
N_BRANCH = 4
ROPE_BASE = 10000.0
EPS = 1e-6
NEG_INF = -1e30
LOG2E = math.log2(math.e)
ATTN_SCALE = HEAD_DIM ** -0.5
MLA_SCALE = (MLA_NOPE + MLA_ROPE) ** -0.5

LANES = 128
MOD_ROWS = 16
CTX_MOD_ROW = DEC_BATCH

C_CB, C_CC, C_CV = 0, 256, 512
C_GQ, C_GK, C_GV = 768, 1280, 1408
C_NQ, C_NK, C_NV = 1536, 1792, 2048
C_MQ, C_MKV, C_MKR = 2304, 2560, 2688
MAIN_COLS = 2816
GATE_COL0 = 2720
Y_CONV, Y_GQA, Y_NA, Y_MLA = 0, 256, 768, 1024
Y_COLS = 1280
NA_PATTERNS = 8
NA_KEYS = NA_WIN_H * GRID_W

VMEM_LIMIT = 56 * 1024 * 1024
POST_CHAIN_ROWS = 256
CTX_SEQS = 2

F32 = jnp.float32
BF16 = jnp.bfloat16


def _dot(a, b):
    return jnp.dot(a, b, preferred_element_type=F32)


def _dot_nt(a, b):
    return lax.dot_general(a, b, (((1,), (1,)), ((), ())), preferred_element_type=F32)


def _rms(x, g):
    return x * lax.rsqrt(jnp.mean(x * x, axis=-1, keepdims=True) + EPS) * g


def _lane_lt(shape, n):
    return lax.broadcasted_iota(jnp.int32, shape, len(shape) - 1) < n


def _row_lt(shape, n):
    return lax.broadcasted_iota(jnp.int32, shape, 0) < n


def _row_group(shape, lo, hi):
    row = lax.broadcasted_iota(jnp.int32, shape, 0)
    return (row >= lo) & (row < hi)


def _short_conv(cb, cc, cv, w):
    u = cc * cv
    t = u.shape[0]
    row = lax.broadcasted_iota(jnp.int32, u.shape, 0)
    prev = jnp.where(row == 0, 0.0, pltpu.roll(u, 1, 0))
    nxt = jnp.where(row == t - 1, 0.0, pltpu.roll(u, t - 1, 0))
    return cb * (prev * w[0:1, :] + u * w[1:2, :] + nxt * w[2:3, :])


def _rope(x, cos, sin_hi, sin_lo, half):
    n = x.shape[-1]
    return x * cos + pltpu.roll(x, n - half, 1) * sin_lo + pltpu.roll(x, half, 1) * sin_hi


def _repeat_rope_key(tile):
    k = jnp.where(_lane_lt(tile.shape, MLA_ROPE), tile, 0.0)
    k = k + pltpu.roll(k, MLA_ROPE, 1)
    return k + pltpu.roll(k, 2 * MLA_ROPE, 1)


def _mod_kernel(c_ref, w_ref, b_ref, o_ref):
    c = c_ref[...]
    s = c * jax.nn.sigmoid(c)
    o_ref[...] = _dot(s.astype(BF16), w_ref[...].astype(BF16)) + b_ref[...]


def _mod_call(c16, w_mod, b_mod):
    tn = 1536
    return pl.pallas_call(
        _mod_kernel,
        grid=(DEPTH, 6 * D_MODEL // tn),
        in_specs=[
            pl.BlockSpec((MOD_ROWS, D_MODEL), lambda l, j: (0, 0)),
            pl.BlockSpec((None, D_MODEL, tn), lambda l, j: (l, 0, j)),
            pl.BlockSpec((None, 1, tn), lambda l, j: (l, 0, j)),
        ],
        out_specs=pl.BlockSpec((None, MOD_ROWS, tn), lambda l, j: (l, 0, j)),
        out_shape=jax.ShapeDtypeStruct((DEPTH, MOD_ROWS, 6 * D_MODEL), F32),
        compiler_params=pltpu.CompilerParams(
            dimension_semantics=("arbitrary", "arbitrary"), vmem_limit_bytes=VMEM_LIMIT),
        name="adaln_mod",
    )(c16, w_mod, b_mod.reshape(DEPTH, 1, 6 * D_MODEL))


def _nab_kernel(rpb_ref, o_ref):
    l = pl.program_id(0)
    c = lax.broadcasted_iota(jnp.int32, (GRID_W, GRID_W), 0)
    w = lax.broadcasted_iota(jnp.int32, (GRID_W, GRID_W), 1)
    dc = w - c + (NA_WIN_W - 1)
    c0 = jnp.clip(c - NA_WIN_W // 2, 0, GRID_W - NA_WIN_W)
    outside = (w < c0) | (w >= c0 + NA_WIN_W)
    n_dr, n_dc = 2 * NA_WIN_H - 1, 2 * NA_WIN_W - 1
    for h in range(NA_HEADS):
        for d in range(n_dr):
            base = ((l * NA_HEADS + h) * n_dr + d) * n_dc
            tile = jnp.full((GRID_W, GRID_W), NEG_INF, F32)
            for j in range(n_dc):
                tile = jnp.where(dc == j, rpb_ref[base + j] * LOG2E, tile)
            tile = jnp.where(outside, NEG_INF, tile)
            for p in range(NA_PATTERNS):
                a = d - (NA_WIN_H - 1) + p
                if 0 <= a < NA_WIN_H:
                    col = h * NA_KEYS + a * GRID_W
                    o_ref[p, :, col:col + GRID_W] = tile


def _nab_call(na_rpb):
    return pl.pallas_call(
        _nab_kernel,
        grid=(DEPTH,),
        in_specs=[pl.BlockSpec(memory_space=pltpu.SMEM)],
        out_specs=pl.BlockSpec((None, NA_PATTERNS, GRID_W, NA_HEADS * NA_KEYS), lambda l: (l, 0, 0, 0)),
        out_shape=jax.ShapeDtypeStruct((DEPTH, NA_PATTERNS, GRID_W, NA_HEADS * NA_KEYS), F32),
        compiler_params=pltpu.CompilerParams(dimension_semantics=("arbitrary",)),
        name="na_bias_tables",
    )(na_rpb.reshape(-1))


def _mod_spec(layer, ctx, n_grid):
    if n_grid == 1:
        index = (lambda s: (layer, CTX_MOD_ROW, 0, 0)) if ctx else (lambda s: (layer, s, 0, 0))
    else:
        index = (lambda s, i: (layer, CTX_MOD_ROW, 0, 0)) if ctx else (lambda s, i: (layer, s, 0, 0))
    return pl.BlockSpec((None, None, 1, 6 * D_MODEL), index)


def _layer_spec(shape, layer, n_grid):
    zeros = (0,) * len(shape)
    index = (lambda s: (layer,) + zeros) if n_grid == 1 else (lambda s, i: (layer,) + zeros)
    return pl.BlockSpec((None,) + tuple(shape), index, pipeline_mode=pl.Buffered(1))


_COLUMN_GROUPS = ((C_CB, C_GQ), (C_GQ, C_NQ), (C_NQ, C_MQ), (C_MQ, MAIN_COLS))


def _modulated_norm(x, g, mod_ref, shift, scale):
    d = D_MODEL
    return _rms(x, g) * (1.0 + mod_ref[:, scale * d:(scale + 1) * d]) + mod_ref[:, shift * d:(shift + 1) * d]


def _projected_columns(h_ref, w_ref):
    cache = {}

    def col(c, n):
        lo, hi = next(g for g in _COLUMN_GROUPS if g[0] <= c and c + n <= g[1])
        if lo not in cache:
            cache.clear()
            cache[lo] = _dot(h_ref[...], w_ref[:, lo:hi])
        return cache[lo][:, c - lo:c - lo + n]

    return col


def _key_planes(k_t, even_first, odd_first):
    lo = _row_lt(k_t.shape, HEAD_DIM)
    swapped = None
    if not even_first or odd_first:
        swapped = pltpu.roll(k_t, HEAD_DIM, 0)
    top = jnp.where(lo, k_t if even_first else swapped, 0.0)
    bot = jnp.where(lo, 0.0, swapped if odd_first else k_t)
    return top.astype(BF16), bot.astype(BF16)


def _value_planes(pair, even_first, odd_first):
    lo = _lane_lt(pair.shape, HEAD_DIM)
    swapped = None
    if not even_first or odd_first:
        swapped = pltpu.roll(pair, HEAD_DIM, 1)
    top = jnp.where(lo, pair if even_first else swapped, 0.0)
    bot = jnp.where(lo, 0.0, swapped if odd_first else pair)
    one_e = jnp.where(lo, 1.0, 0.0)
    return (jnp.concatenate([top, one_e], axis=1).astype(BF16),
            jnp.concatenate([bot, 1.0 - one_e], axis=1).astype(BF16))


def _probabilities(parts, sink=None):
    m = parts[0].max(axis=-1, keepdims=True)
    for p in parts[1:]:
        m = jnp.maximum(m, p.max(axis=-1, keepdims=True))
    if sink is not None:
        m = jnp.maximum(m, sink)
    probs = [jnp.exp2((p - m).astype(BF16)) for p in parts]
    return probs, (None if sink is None else jnp.exp2(sink - m))


def _attend(terms, sink_e=None, sink_o=None):
    o = None
    for p, v in terms:
        t = _dot(p, v)
        o = t if o is None else o + t
    den = o[:, LANES:]
    if sink_e is not None:
        den = den + jnp.where(_lane_lt(den.shape, HEAD_DIM), sink_e, sink_o)
    return o[:, :LANES] / den


def _mla_key_plane(kn_t_pair, kr4_t, j, odd):
    h = 2 * j + odd
    lo = _row_lt(kn_t_pair.shape, HEAD_DIM)
    nope = jnp.where(lo, 0.0, kn_t_pair) if odd else jnp.where(lo, kn_t_pair, 0.0)
    rope = jnp.where(_row_group(kr4_t.shape, h * MLA_ROPE, (h + 1) * MLA_ROPE), kr4_t, 0.0)
    return jnp.concatenate([nope, rope], axis=0).astype(BF16)


def _ctx_mixer_kernel(sink_ref, x_ref, mod_ref, ga_ref, win_ref, wc_ref, gq_ref, wuq_ref, gkv_ref, wukt_ref, wuv_ref,
                      *rest, layer, first):
    y_ref, kg_ref, vg_ref, kn_ref, vn_ref, ckv_ref, kr_ref, h_s = rest[-8:]
    t = SEQ
    for b in range(CTX_SEQS):
        h_s[b * t:(b + 1) * t, :] = _modulated_norm(x_ref[b], ga_ref[...], mod_ref, 0, 1).astype(BF16)
    col_all = _projected_columns(h_s, win_ref)

    def put_state(ref, b, val):
        if first:
            for d in range(DEPTH):
                ref[b, d] = val if d == layer else jnp.zeros_like(val)
        else:
            ref[b] = val

    def conv(b, col):
        y_ref[b, :, Y_CONV:Y_CONV + CONV_WIDTH] = _short_conv(
            col(C_CB, CONV_WIDTH), col(C_CC, CONV_WIDTH), col(C_CV, CONV_WIDTH), wc_ref[...]).astype(BF16)

    def gqa(b, col):
        kg_t, vpair = col(C_GK, LANES).T, col(C_GV, LANES)
        put_state(kg_ref, b, kg_t)
        put_state(vg_ref, b, vpair.T)
        row2 = lax.broadcasted_iota(jnp.int32, (2 * t, 1), 0) < t
        for g in range(GQA_KV_HEADS):
            ke, ko = _key_planes(kg_t, g == 0, g == 0)
            ve, vo = _value_planes(vpair, g == 0, g == 0)
            q = jnp.concatenate([col(C_GQ + (2 * g) * LANES, LANES), col(C_GQ + (2 * g + 1) * LANES, LANES)], axis=0)
            q = (q * (ATTN_SCALE * LOG2E)).astype(BF16)
            s = _dot(q, jnp.concatenate([ke, ko], axis=1))
            sink = [jnp.where(row2, sink_ref[layer, 4 * g + odd], sink_ref[layer, 4 * g + 2 + odd]) * LOG2E
                    for odd in range(2)]
            (pe,), xe = _probabilities([s[:, 0:t]], sink[0])
            (po,), xo = _probabilities([s[:, t:2 * t]], sink[1])
            o = _attend([(pe, ve), (po, vo)], xe, xo).astype(BF16)
            c0 = Y_GQA + (2 * g) * LANES
            y_ref[b, :, c0:c0 + LANES] = o[0:t]
            y_ref[b, :, c0 + LANES:c0 + 2 * LANES] = o[t:2 * t]

    def na(b, col):
        kn_t = col(C_NK, 2 * LANES).T
        put_state(kn_ref, b, kn_t)
        put_state(vn_ref, b, col(C_NV, 2 * LANES).T)
        for j in range(NA_HEADS // 2):
            ke, ko = _key_planes(kn_t[j * LANES:(j + 1) * LANES, :], True, False)
            ve, vo = _value_planes(col(C_NV + j * LANES, LANES), True, False)
            q = (col(C_NQ + j * LANES, LANES) * (ATTN_SCALE * LOG2E)).astype(BF16)
            s = _dot(q, jnp.concatenate([ke, ko], axis=1))
            (pe,), _ = _probabilities([s[:, 0:t]])
            (po,), _ = _probabilities([s[:, t:2 * t]])
            y_ref[b, :, Y_NA + j * LANES:Y_NA + (j + 1) * LANES] = _attend([(pe, ve), (po, vo)]).astype(BF16)

    def mla(b, col):
        ckv = _rms(col(C_MKV, MLA_KV_RANK), gkv_ref[...])
        put_state(ckv_ref, b, ckv)
        kr_t = col(C_MKR, LANES).T[0:MLA_ROPE, :]
        put_state(kr_ref, b, kr_t)
        ckv_b = ckv.astype(BF16)
        q = _dot(_rms(col(C_MQ, MLA_Q_RANK), gq_ref[...]).astype(BF16), wuq_ref[...]) * (MLA_SCALE * LOG2E)
        kn_t_all = _dot_nt(wukt_ref[...], ckv_b)
        v_all = _dot(ckv_b, wuv_ref[...])
        kr4_t = jnp.concatenate([kr_t] * MLA_HEADS, axis=0)
        q_rope = q[:, 2 * LANES:3 * LANES]
        for j in range(MLA_HEADS // 2):
            qj = jnp.concatenate([q[:, j * LANES:(j + 1) * LANES], q_rope], axis=1).astype(BF16)
            kn_t_pair = kn_t_all[j * LANES:(j + 1) * LANES, :]
            keys = jnp.concatenate(
                [_mla_key_plane(kn_t_pair, kr4_t, j, 0), _mla_key_plane(kn_t_pair, kr4_t, j, 1)], axis=1)
            s = _dot(qj, keys)
            ve, vo = _value_planes(v_all[:, j * LANES:(j + 1) * LANES], True, False)
            (pe,), _ = _probabilities([s[:, 0:t]])
            (po,), _ = _probabilities([s[:, t:2 * t]])
            y_ref[b, :, Y_MLA + j * LANES:Y_MLA + (j + 1) * LANES] = _attend([(pe, ve), (po, vo)]).astype(BF16)

    for mixer in (conv, gqa, na, mla):
        for b in range(CTX_SEQS):
            mixer(b, lambda c, n, b=b: col_all(c, n)[b * t:(b + 1) * t])


def _ctx_mixer_call(x, mods, g_attn, w_main, sink, w_conv, g_q, w_uq, g_kv, w_uk_t, w_uv, prev_states, layer, name):
    n, nb = BATCH, CTX_SEQS
    state_shapes = ((2 * HEAD_DIM, SEQ), (2 * HEAD_DIM, SEQ), (4 * HEAD_DIM, SEQ), (4 * HEAD_DIM, SEQ),
                    (SEQ, MLA_KV_RANK), (MLA_ROPE, SEQ))
    first = prev_states is None
    if first:
        state_spec = lambda shape: pl.BlockSpec((nb, DEPTH) + shape, lambda s: (s, 0, 0, 0))
    else:
        state_spec = lambda shape: pl.BlockSpec((nb, None) + shape, lambda s: (s, layer, 0, 0))
    in_specs = [
        pl.BlockSpec(memory_space=pltpu.SMEM),
        pl.BlockSpec((nb, SEQ, D_MODEL), lambda s: (s, 0, 0)),
        _mod_spec(layer, True, 1),
        _layer_spec((1, D_MODEL), layer, 1),
        _layer_spec((D_MODEL, MAIN_COLS), layer, 1),
        _layer_spec((CONV_K, CONV_WIDTH), layer, 1),
        _layer_spec((1, MLA_Q_RANK), layer, 1),
        _layer_spec((MLA_Q_RANK, 3 * LANES), layer, 1),
        _layer_spec((1, MLA_KV_RANK), layer, 1),
        _layer_spec((2 * LANES, MLA_KV_RANK), layer, 1),
        _layer_spec((MLA_KV_RANK, 2 * LANES), layer, 1),
    ]
    args = [sink, x, mods, g_attn, w_main, w_conv, g_q, w_uq, g_kv, w_uk_t, w_uv]
    aliases = {}
    if not first:
        for i, st in enumerate(prev_states):
            aliases[len(args)] = 1 + i
            in_specs.append(pl.BlockSpec(memory_space=pl.ANY))
            args.append(st)
    outs = pl.pallas_call(
        functools.partial(_ctx_mixer_kernel, layer=layer, first=first),
        grid=(n // nb,),
        in_specs=in_specs,
        out_specs=[pl.BlockSpec((nb, SEQ, Y_COLS), lambda s: (s, 0, 0))] + [state_spec(s) for s in state_shapes],
        out_shape=[jax.ShapeDtypeStruct((n, SEQ, Y_COLS), BF16)]
        + [jax.ShapeDtypeStruct((n, DEPTH) + s, F32) for s in state_shapes],
        input_output_aliases=aliases,
        scratch_shapes=[pltpu.VMEM((nb * SEQ, D_MODEL), BF16)],
        compiler_params=pltpu.CompilerParams(
            dimension_semantics=("arbitrary",), vmem_limit_bytes=VMEM_LIMIT),
        name=name,
    )(*args)
    return outs[0], outs[1:]


def _lat_mixer_kernel(sink_ref, x_ref, mod_ref, ga_ref, win_ref, wc_ref, gq_ref, wuq_ref, gkv_ref, wukt_ref, wuv_ref,
                      rc_ref, rsh_ref, rsl_ref, mc_ref, msh_ref, msl_ref, band_ref, nab_ref,
                      cgk_ref, cgv_ref, cnk_ref, cnv_ref, cckv_ref, ckr_ref,
                      y_ref,
                      h_s, gq_s, gk_s, gv_s, gkc_s, gvc_s, nq_s, nk_s, nv_s, nkc_s, nvc_s, mq_s, mk_s, mv_s, *, layer):
    t = DEC_SEQ
    h_s[...] = _modulated_norm(x_ref[...], ga_ref[...], mod_ref, 0, 1).astype(BF16)
    col = _projected_columns(h_s, win_ref)
    rope64 = lambda x: _rope(x, rc_ref[...], rsh_ref[...], rsl_ref[...], HEAD_DIM // 4)
    rope32 = lambda x: _rope(x, mc_ref[...], msh_ref[...], msl_ref[...], MLA_ROPE // 4)

    y_ref[:, Y_CONV:Y_CONV + CONV_WIDTH] = _short_conv(
        col(C_CB, CONV_WIDTH), col(C_CC, CONV_WIDTH), col(C_CV, CONV_WIDTH), wc_ref[...]).astype(BF16)

    for j in range(GQA_HEADS // 2):
        gq_s[:, j * LANES:(j + 1) * LANES] = (
            rope64(col(C_GQ + j * LANES, LANES)) * (ATTN_SCALE * LOG2E)).astype(BF16)
    k_t, vpair = rope64(col(C_GK, LANES)).T, col(C_GV, LANES)
    kc_t, vc_pair = cgk_ref[...], cgv_ref[...].T
    for g in range(GQA_KV_HEADS):
        gk_s[2 * g], gk_s[2 * g + 1] = _key_planes(k_t, g == 0, g == 0)
        gv_s[2 * g], gv_s[2 * g + 1] = _value_planes(vpair, g == 0, g == 0)
        gkc_s[g] = jnp.concatenate(_key_planes(kc_t, g == 0, g == 0), axis=1)
        gvc_s[2 * g], gvc_s[2 * g + 1] = _value_planes(vc_pair, g == 0, g == 0)

    def gqa_block(b, c_lo, c_hi):
        q0 = pl.multiple_of(b * BAND_BLOCK, BAND_BLOCK)
        k0 = pl.multiple_of(q0 + (c_lo - WINDOW), BAND_BLOCK)
        n = c_hi - c_lo
        band = band_ref[:, c_lo:c_hi]
        row2 = lax.broadcasted_iota(jnp.int32, (2 * BAND_BLOCK, 1), 0) < BAND_BLOCK
        for g in range(GQA_KV_HEADS):
            q = jnp.concatenate([gq_s[pl.ds(q0, BAND_BLOCK), (2 * g) * LANES:(2 * g + 1) * LANES],
                                 gq_s[pl.ds(q0, BAND_BLOCK), (2 * g + 1) * LANES:(2 * g + 2) * LANES]], axis=0)
            s_ctx = _dot(q, gkc_s[g])
            terms, sinks = [], []
            for odd in range(2):
                s_loc = _dot(q, gk_s[2 * g + odd, :, pl.ds(k0, n)]) + band
                sink = jnp.where(row2, sink_ref[layer, 4 * g + odd], sink_ref[layer, 4 * g + 2 + odd]) * LOG2E
                (p_loc, p_ctx), x = _probabilities([s_loc, s_ctx[:, odd * PAST_LEN:(odd + 1) * PAST_LEN]], sink)
                terms += [(p_loc, gv_s[2 * g + odd, pl.ds(k0, n), :]), (p_ctx, gvc_s[2 * g + odd])]
                sinks.append(x)
            o = _attend(terms, sinks[0], sinks[1]).astype(BF16)
            c0 = Y_GQA + (2 * g) * LANES
            y_ref[pl.ds(q0, BAND_BLOCK), c0:c0 + LANES] = o[0:BAND_BLOCK]
            y_ref[pl.ds(q0, BAND_BLOCK), c0 + LANES:c0 + 2 * LANES] = o[BAND_BLOCK:2 * BAND_BLOCK]

    span = BAND_BLOCK + 2 * WINDOW
    nb = t // BAND_BLOCK
    gqa_block(0, WINDOW, span)
    lax.fori_loop(1, nb - 1, lambda b, c: (gqa_block(b, 0, span), c)[1], 0, unroll=True)
    gqa_block(nb - 1, 0, span - WINDOW)

    for j in range(NA_HEADS // 2):
        nq_s[:, j * LANES:(j + 1) * LANES] = (col(C_NQ + j * LANES, LANES) * (ATTN_SCALE * LOG2E)).astype(BF16)
        nk_t = col(C_NK + j * LANES, LANES).T
        for shift, k_t in enumerate((nk_t, pltpu.roll(nk_t, t - GRID_W, 1))):
            nk_s[shift, 2 * j], nk_s[shift, 2 * j + 1] = _key_planes(k_t, True, False)
        nv_s[2 * j], nv_s[2 * j + 1] = _value_planes(col(C_NV + j * LANES, LANES), True, False)
        nkc_s[j] = jnp.concatenate(_key_planes(cnk_ref[j * LANES:(j + 1) * LANES, :], True, False), axis=1)
        nvc_s[2 * j], nvc_s[2 * j + 1] = _value_planes(cnv_ref[j * LANES:(j + 1) * LANES, :].T, True, False)

    def na_row(r, carry):
        q0 = pl.multiple_of(r * GRID_W, GRID_W)
        r0 = jnp.clip(r - NA_WIN_H // 2, 0, GRID_ROWS - NA_WIN_H)
        k0 = pl.multiple_of(r0 * GRID_W, GRID_W)
        kt0 = pl.multiple_of(lax.shift_right_logical(r0, 1) * LANES, LANES)
        pat = r - r0
        for j in range(NA_HEADS // 2):
            q = nq_s[pl.ds(q0, GRID_W), j * LANES:(j + 1) * LANES]
            s_ctx = _dot(q, nkc_s[j])
            terms = []
            for odd in range(2):
                h = 2 * j + odd
                s_loc = _dot(q, nk_s[r0 & 1, h, :, pl.ds(kt0, NA_KEYS)])
                s_loc = s_loc + nab_ref[pat, :, h * NA_KEYS:(h + 1) * NA_KEYS]
                (p_loc, p_ctx), _ = _probabilities([s_loc, s_ctx[:, odd * PAST_LEN:(odd + 1) * PAST_LEN]])
                terms += [(p_loc, nv_s[h, pl.ds(k0, NA_KEYS), :]), (p_ctx, nvc_s[h])]
            y_ref[pl.ds(q0, GRID_W), Y_NA + j * LANES:Y_NA + (j + 1) * LANES] = _attend(terms).astype(BF16)
        return carry

    lax.fori_loop(0, GRID_ROWS, na_row, 0, unroll=True)

    ckv_b = _rms(col(C_MKV, MLA_KV_RANK), gkv_ref[...]).astype(BF16)
    cckv_b = cckv_ref[...].astype(BF16)
    q = _dot(_rms(col(C_MQ, MLA_Q_RANK), gq_ref[...]).astype(BF16), wuq_ref[...]) * (MLA_SCALE * LOG2E)
    mq_s[:, 0:2 * LANES] = q[:, 0:2 * LANES].astype(BF16)
    mq_s[:, 2 * LANES:3 * LANES] = rope32(q[:, 2 * LANES:3 * LANES]).astype(BF16)
    kr4_t = rope32(_repeat_rope_key(col(C_MKR, LANES))).T
    kr4_t_c = jnp.concatenate([ckr_ref[...]] * MLA_HEADS, axis=0)
    for cols, ckv_x, kr_x in ((slice(0, t), ckv_b, kr4_t), (slice(t, t + PAST_LEN), cckv_b, kr4_t_c)):
        kn_t_all = _dot_nt(wukt_ref[...], ckv_x)
        v_all = _dot(ckv_x, wuv_ref[...])
        for j in range(MLA_HEADS // 2):
            kn_t_pair = kn_t_all[j * LANES:(j + 1) * LANES, :]
            mv_s[2 * j, cols, :], mv_s[2 * j + 1, cols, :] = _value_planes(
                v_all[:, j * LANES:(j + 1) * LANES], True, False)
            for odd in range(2):
                mk_s[2 * j + odd, :, cols] = _mla_key_plane(kn_t_pair, kr_x, j, odd)

    tq = 256

    def mla_block(i, carry):
        q0 = pl.multiple_of(i * tq, tq)
        for j in range(MLA_HEADS // 2):
            qj = jnp.concatenate([mq_s[pl.ds(q0, tq), j * LANES:(j + 1) * LANES],
                                  mq_s[pl.ds(q0, tq), 2 * LANES:3 * LANES]], axis=1)
            (pe,), _ = _probabilities([_dot(qj, mk_s[2 * j])])
            (po,), _ = _probabilities([_dot(qj, mk_s[2 * j + 1])])
            o = _attend([(pe, mv_s[2 * j]), (po, mv_s[2 * j + 1])])
            y_ref[pl.ds(q0, tq), Y_MLA + j * LANES:Y_MLA + (j + 1) * LANES] = o.astype(BF16)
        return carry

    lax.fori_loop(0, t // tq, mla_block, 0, unroll=True)


def _rope_tables(group, half):
    tok = np.arange(DEC_SEQ)
    pos = np.stack([tok // GRID_W, tok % GRID_W], axis=1).astype(np.float64)
    inv = ROPE_BASE ** (-np.arange(half, dtype=np.float64) / half)
    lane = np.arange(LANES) % group
    axis = lane // (2 * half)
    within = lane % (2 * half)
    ang = pos[:, axis] * inv[within % half][None, :]
    cos, sin = np.cos(ang), np.sin(ang)
    upper = (within >= half)[None, :]
    sin_hi = np.where(upper, sin, 0.0)
    sin_lo = np.where(upper, 0.0, -sin)
    return tuple(jnp.asarray(a, dtype=F32) for a in (cos, sin_hi, sin_lo))


def _band_mask():
    i = np.arange(BAND_BLOCK)[:, None]
    c = np.arange(BAND_BLOCK + 2 * WINDOW)[None, :]
    ok = (c >= i) & (c <= i + 2 * WINDOW)
    m = np.where(ok, 0.0, NEG_INF)
    return jnp.asarray(np.concatenate([m, m], axis=0), dtype=F32)


def _lat_mixer_call(x, mods, g_attn, w_main, sink, w_conv, g_q, w_uq, g_kv, w_uk_t, w_uv, nab, caches, layer, name):
    n, t = DEC_BATCH, DEC_SEQ
    one = pl.Buffered(1)
    const = lambda shape: pl.BlockSpec(shape, lambda s: (0,) * len(shape), pipeline_mode=one)
    cache = lambda shape: pl.BlockSpec((None, None) + shape, lambda s: (s, layer, 0, 0), pipeline_mode=one)
    rope_g = _rope_tables(HEAD_DIM, HEAD_DIM // 4)
    rope_m = _rope_tables(MLA_ROPE, MLA_ROPE // 4)
    span = BAND_BLOCK + 2 * WINDOW
    keys = t + PAST_LEN
    scratch = [
        pltpu.VMEM((t, D_MODEL), BF16),
        pltpu.VMEM((t, GQA_HEADS // 2 * LANES), BF16),
        pltpu.VMEM((2 * GQA_KV_HEADS, LANES, t), BF16),
        pltpu.VMEM((2 * GQA_KV_HEADS, t, 2 * LANES), BF16),
        pltpu.VMEM((GQA_KV_HEADS, LANES, 2 * PAST_LEN), BF16),
        pltpu.VMEM((2 * GQA_KV_HEADS, PAST_LEN, 2 * LANES), BF16),
        pltpu.VMEM((t, NA_HEADS // 2 * LANES), BF16),
        pltpu.VMEM((2, NA_HEADS, LANES, t), BF16),
        pltpu.VMEM((NA_HEADS, t, 2 * LANES), BF16),
        pltpu.VMEM((NA_HEADS // 2, LANES, 2 * PAST_LEN), BF16),
        pltpu.VMEM((NA_HEADS, PAST_LEN, 2 * LANES), BF16),
        pltpu.VMEM((t, 3 * LANES), BF16),
        pltpu.VMEM((MLA_HEADS, 2 * LANES, keys), BF16),
        pltpu.VMEM((MLA_HEADS, keys, 2 * LANES), BF16),
    ]
    return pl.pallas_call(
        functools.partial(_lat_mixer_kernel, layer=layer),
        grid=(n,),
        in_specs=[
            pl.BlockSpec(memory_space=pltpu.SMEM),
            pl.BlockSpec((None, t, D_MODEL), lambda s: (s, 0, 0), pipeline_mode=one),
            _mod_spec(layer, False, 1),
            _layer_spec((1, D_MODEL), layer, 1),
            _layer_spec((D_MODEL, MAIN_COLS), layer, 1),
            _layer_spec((CONV_K, CONV_WIDTH), layer, 1),
            _layer_spec((1, MLA_Q_RANK), layer, 1),
            _layer_spec((MLA_Q_RANK, 3 * LANES), layer, 1),
            _layer_spec((1, MLA_KV_RANK), layer, 1),
            _layer_spec((2 * LANES, MLA_KV_RANK), layer, 1),
            _layer_spec((MLA_KV_RANK, 2 * LANES), layer, 1),
        ] + [const((t, LANES))] * 6 + [
            const((2 * BAND_BLOCK, span)),
            _layer_spec((NA_PATTERNS, GRID_W, NA_HEADS * NA_KEYS), layer, 1),
            cache((2 * HEAD_DIM, PAST_LEN)), cache((2 * HEAD_DIM, PAST_LEN)),
            cache((4 * HEAD_DIM, PAST_LEN)), cache((4 * HEAD_DIM, PAST_LEN)),
            cache((PAST_LEN, MLA_KV_RANK)), cache((MLA_ROPE, PAST_LEN)),
        ],
        out_specs=pl.BlockSpec((None, t, Y_COLS), lambda s: (s, 0, 0)),
        out_shape=jax.ShapeDtypeStruct((n, t, Y_COLS), BF16),
        scratch_shapes=scratch,
        compiler_params=pltpu.CompilerParams(
            dimension_semantics=("arbitrary",), vmem_limit_bytes=VMEM_LIMIT),
        name=name,
    )(sink, x, mods, g_attn, w_main, w_conv, g_q, w_uq, g_kv, w_uk_t, w_uv, *rope_g, *rope_m, _band_mask(), nab, *caches)


def _post_kernel(x_ref, y_ref, mod_ref, ga_ref, gm_ref, gf_ref, wg_ref, wb_ref, wo_ref, w1_ref, w2_ref,
                 o_ref, *, final):
    d = D_MODEL
    mod = lambda i: mod_ref[:, i * d:(i + 1) * d]
    bounds = (Y_CONV, Y_GQA, Y_NA, Y_MLA, Y_COLS)
    for r0 in range(0, x_ref.shape[0], POST_CHAIN_ROWS):
        rows = slice(r0, r0 + POST_CHAIN_ROWS)
        x = x_ref[rows, :]
        h = _modulated_norm(x, ga_ref[...], mod_ref, 0, 1).astype(BF16)
        merged = None
        for i in range(N_BRANCH):
            lo, hi = bounds[i], bounds[i + 1]
            gate = jax.nn.sigmoid(_dot(h, wg_ref[:, i * d:(i + 1) * d]))
            term = gate * _dot(y_ref[rows, lo:hi], wb_ref[lo:hi, :])
            merged = term if merged is None else merged + term
        x = x + mod(2) * _dot(merged.astype(BF16), wo_ref[...])
        h = _modulated_norm(x, gm_ref[...], mod_ref, 3, 4).astype(BF16)
        mlp = None
        for c0 in range(0, D_FF, d):
            f = jnp.square(jnp.maximum(_dot(h, w1_ref[:, c0:c0 + d]), 0.0)).astype(BF16)
            term = _dot(f, w2_ref[c0:c0 + d, :])
            mlp = term if mlp is None else mlp + term
        x = x + mod(5) * mlp
        o_ref[rows, :] = _rms(x, gf_ref[...]) if final else x


def _post_call(x, y, mods, g_attn, g_mlp, g_final, w_gates, w_br, w_o, w_ff1, w_ff2, layer, ctx, name):
    n, t, _ = x.shape
    tm = 2 * POST_CHAIN_ROWS
    return pl.pallas_call(
        functools.partial(_post_kernel, final=layer == DEPTH - 1),
        grid=(n, t // tm),
        in_specs=[
            pl.BlockSpec((None, tm, D_MODEL), lambda s, i: (s, i, 0)),
            pl.BlockSpec((None, tm, Y_COLS), lambda s, i: (s, i, 0)),
            _mod_spec(layer, ctx, 2),
            _layer_spec((1, D_MODEL), layer, 2),
            _layer_spec((1, D_MODEL), layer, 2),
            pl.BlockSpec((1, D_MODEL), lambda s, i: (0, 0)),
            _layer_spec((D_MODEL, N_BRANCH * D_MODEL), layer, 2),
            _layer_spec((Y_COLS, D_MODEL), layer, 2),
            _layer_spec((D_MODEL, D_MODEL), layer, 2),
            _layer_spec((D_MODEL, D_FF), layer, 2),
            _layer_spec((D_FF, D_MODEL), layer, 2),
        ],
        out_specs=pl.BlockSpec((None, tm, D_MODEL), lambda s, i: (s, i, 0)),
        out_shape=jax.ShapeDtypeStruct((n, t, D_MODEL), F32),
        compiler_params=pltpu.CompilerParams(
            dimension_semantics=("arbitrary", "arbitrary"), vmem_limit_bytes=VMEM_LIMIT),
        name=name,
    )(x, y, mods, g_attn, g_mlp, g_final, w_gates, w_br, w_o, w_ff1, w_ff2)


def kernel(x_prompt, x_sample, cache_gqa_k, cache_gqa_v, cache_na_k, cache_na_v, cache_mla_ckv, cache_mla_krope, c, c_ctx, w_mod, b_mod, g_attn, g_mlp, w_in, w_conv, gqa_sink, na_rpb, mla_g_q, mla_w_uq, mla_g_kv, mla_w_ukv, w_branch_conv, w_branch_gqa, w_branch_na, w_branch_mla, w_o, w_ff1, w_ff2, g_final):
    w_main = w_in[:, :, :MAIN_COLS].astype(BF16)
    w_gates = w_in[:, :, GATE_COL0:].astype(BF16)
    uq = mla_w_uq.reshape(DEPTH, MLA_Q_RANK, MLA_HEADS, MLA_NOPE + MLA_ROPE)
    w_uq = jnp.concatenate([uq[..., :MLA_NOPE].reshape(DEPTH, MLA_Q_RANK, -1),
                            uq[..., MLA_NOPE:].reshape(DEPTH, MLA_Q_RANK, -1)], axis=-1).astype(BF16)
    ukv = mla_w_ukv.reshape(DEPTH, MLA_KV_RANK, MLA_HEADS, MLA_NOPE + MLA_V)
    w_uk_t = jnp.transpose(ukv[..., :MLA_NOPE].reshape(DEPTH, MLA_KV_RANK, -1), (0, 2, 1)).astype(BF16)
    w_uv = ukv[..., MLA_NOPE:].reshape(DEPTH, MLA_KV_RANK, -1).astype(BF16)
    w_br = jnp.concatenate([w_branch_conv, w_branch_gqa, w_branch_na, w_branch_mla], axis=1).astype(BF16)
    w_o_b, w_ff1_b, w_ff2_b = w_o.astype(BF16), w_ff1.astype(BF16), w_ff2.astype(BF16)
    g_a, g_m = g_attn[:, None, :], g_mlp[:, None, :]
    g_q, g_kv, g_f = mla_g_q[:, None, :], mla_g_kv[:, None, :], g_final[None, :]

    c16 = jnp.concatenate([c, c_ctx[None, :], jnp.zeros((MOD_ROWS - DEC_BATCH - 1, D_MODEL), F32)], axis=0)
    mods = _mod_call(c16, w_mod, b_mod).reshape(DEPTH, MOD_ROWS, 1, 6 * D_MODEL)
    nab = _nab_call(na_rpb)

    heads_t = lambda a: jnp.transpose(a, (0, 1, 3, 4, 2)).reshape(a.shape[0], DEPTH, -1, a.shape[2])
    caches = (heads_t(cache_gqa_k), heads_t(cache_gqa_v), heads_t(cache_na_k), heads_t(cache_na_v),
              cache_mla_ckv, jnp.transpose(cache_mla_krope, (0, 1, 3, 2)))

    h_ctx, h_lat = x_prompt, x_sample
    states = None
    for l in range(DEPTH):
        mixer_w = (gqa_sink, w_conv, g_q, w_uq, g_kv, w_uk_t, w_uv)
        post_w = (g_a, g_m, g_f, w_gates, w_br, w_o_b, w_ff1_b, w_ff2_b)

        y, states = _ctx_mixer_call(h_ctx, mods, g_a, w_main, *mixer_w, states, l, f"mixer_ctx_{l}")
        flat = lambda a: a.reshape(1, BATCH * SEQ, a.shape[-1])
        h_ctx = _post_call(flat(h_ctx), flat(y), mods, *post_w, l, True, f"post_ctx_{l}").reshape(BATCH, SEQ, D_MODEL)

        y = _lat_mixer_call(h_lat, mods, g_a, w_main, *mixer_w, nab, caches, l, f"mixer_lat_{l}")
        h_lat = _post_call(h_lat, y, mods, *post_w, l, False, f"post_lat_{l}")

    def heads_out(a, heads):
        return jnp.transpose(a.reshape(BATCH, DEPTH, heads, HEAD_DIM, SEQ), (0, 1, 4, 2, 3))

    kg, vg, kn, vn, ckv, kr = states
    return (h_ctx, h_lat, heads_out(kg, GQA_KV_HEADS), heads_out(vg, GQA_KV_HEADS),
            heads_out(kn, NA_HEADS), heads_out(vn, NA_HEADS), ckv, jnp.transpose(kr, (0, 1, 3, 2)))
```

```python
import functools
import math

import numpy as np
import jax
import jax.numpy as jnp
from jax import lax
from jax.experimental import pallas as pl
from jax.experimental.pallas import tpu as pltpu

D_MODEL = 1024
BATCH = 32
SEQ = 256
DEPTH = 2
DEC_BATCH = 8
DEC_SEQ = 1024
PAST_LEN = 512
GRID_W = 64
GRID_ROWS = DEC_SEQ // GRID_W
HEAD_DIM = 64
CONV_WIDTH = 256
CONV_K = 3
GQA_HEADS = 8
GQA_KV_HEADS = 2
WINDOW = 128
BAND_BLOCK = 128
NA_HEADS = 4
NA_WIN_H = 8
NA_WIN_W = 16
MLA_HEADS = 4
MLA_Q_RANK = 256
MLA_KV_RANK = 128
MLA_NOPE = 64
MLA_ROPE = 32
MLA_V = 64
D_FF = 4 * D_MODEL
N_BRANCH = 4
ROPE_BASE = 10000.0
EPS = 1e-6
NEG_INF = -1e30
LOG2E = math.log2(math.e)
ATTN_SCALE = HEAD_DIM ** -0.5
MLA_SCALE = (MLA_NOPE + MLA_ROPE) ** -0.5

LANES = 128
MOD_ROWS = 16
CTX_MOD_ROW = DEC_BATCH

C_CB, C_CC, C_CV = 0, 256, 512
C_GQ, C_GK, C_GV = 768, 1280, 1408
C_NQ, C_NK, C_NV = 1536, 1792, 2048
C_MQ, C_MKV, C_MKR = 2304, 2560, 2688
MAIN_COLS = 2816
GATE_COL0 = 2720
Y_CONV, Y_GQA, Y_NA, Y_MLA = 0, 256, 768, 1024
Y_COLS = 1280
NA_PATTERNS = 8
NA_KEYS = NA_WIN_H * GRID_W

VMEM_LIMIT = 56 * 1024 * 1024
POST_CHAIN_ROWS = 256
CTX_SEQS = 2

F32 = jnp.float32
BF16 = jnp.bfloat16


def _dot(a, b):
    return jnp.dot(a, b, preferred_element_type=F32)


def _dot_nt(a, b):
    return lax.dot_general(a, b, (((1,), (1,)), ((), ())), preferred_element_type=F32)


def _rms(x, g):
    return x * lax.rsqrt(jnp.mean(x * x, axis=-1, keepdims=True) + EPS) * g


def _lane_lt(shape, n):
    return lax.broadcasted_iota(jnp.int32, shape, len(shape) - 1) < n


def _row_lt(shape, n):
    return lax.broadcasted_iota(jnp.int32, shape, 0) < n


def _row_group(shape, lo, hi):
    row = lax.broadcasted_iota(jnp.int32, shape, 0)
    return (row >= lo) & (row < hi)


def _short_conv(cb, cc, cv, w):
    u = cc * cv
    t = u.shape[0]
    row = lax.broadcasted_iota(jnp.int32, u.shape, 0)
    prev = jnp.where(row == 0, 0.0, pltpu.roll(u, 1, 0))
    nxt = jnp.where(row == t - 1, 0.0, pltpu.roll(u, t - 1, 0))
    return cb * (prev * w[0:1, :] + u * w[1:2, :] + nxt * w[2:3, :])


def _rope(x, cos, sin_hi, sin_lo, half):
    n = x.shape[-1]
    return x * cos + pltpu.roll(x, n - half, 1) * sin_lo + pltpu.roll(x, half, 1) * sin_hi


def _repeat_rope_key(tile):
    k = jnp.where(_lane_lt(tile.shape, MLA_ROPE), tile, 0.0)
    k = k + pltpu.roll(k, MLA_ROPE, 1)
    return k + pltpu.roll(k, 2 * MLA_ROPE, 1)


def _mod_kernel(c_ref, w_ref, b_ref, o_ref):
    c = c_ref[...]
    s = c * jax.nn.sigmoid(c)
    o_ref[...] = _dot(s.astype(BF16), w_ref[...].astype(BF16)) + b_ref[...]


def _mod_call(c16, w_mod, b_mod):
    tn = 1536
    return pl.pallas_call(
        _mod_kernel,
        grid=(DEPTH, 6 * D_MODEL // tn),
        in_specs=[
            pl.BlockSpec((MOD_ROWS, D_MODEL), lambda l, j: (0, 0)),
            pl.BlockSpec((None, D_MODEL, tn), lambda l, j: (l, 0, j)),
            pl.BlockSpec((None, 1, tn), lambda l, j: (l, 0, j)),
        ],
        out_specs=pl.BlockSpec((None, MOD_ROWS, tn), lambda l, j: (l, 0, j)),
        out_shape=jax.ShapeDtypeStruct((DEPTH, MOD_ROWS, 6 * D_MODEL), F32),
        compiler_params=pltpu.CompilerParams(
            dimension_semantics=("arbitrary", "arbitrary"), vmem_limit_bytes=VMEM_LIMIT),
        name="adaln_mod",
    )(c16, w_mod, b_mod.reshape(DEPTH, 1, 6 * D_MODEL))


def _nab_kernel(rpb_ref, o_ref):
    l = pl.program_id(0)
    c = lax.broadcasted_iota(jnp.int32, (GRID_W, GRID_W), 0)
    w = lax.broadcasted_iota(jnp.int32, (GRID_W, GRID_W), 1)
    dc = w - c + (NA_WIN_W - 1)
    c0 = jnp.clip(c - NA_WIN_W // 2, 0, GRID_W - NA_WIN_W)
    outside = (w < c0) | (w >= c0 + NA_WIN_W)
    n_dr, n_dc = 2 * NA_WIN_H - 1, 2 * NA_WIN_W - 1
    for h in range(NA_HEADS):
        for d in range(n_dr):
            base = ((l * NA_HEADS + h) * n_dr + d) * n_dc
            tile = jnp.full((GRID_W, GRID_W), NEG_INF, F32)
            for j in range(n_dc):
                tile = jnp.where(dc == j, rpb_ref[base + j] * LOG2E, tile)
            tile = jnp.where(outside, NEG_INF, tile)
            for p in range(NA_PATTERNS):
                a = d - (NA_WIN_H - 1) + p
                if 0 <= a < NA_WIN_H:
                    col = h * NA_KEYS + a * GRID_W
                    o_ref[p, :, col:col + GRID_W] = tile


def _nab_call(na_rpb):
    return pl.pallas_call(
        _nab_kernel,
        grid=(DEPTH,),
        in_specs=[pl.BlockSpec(memory_space=pltpu.SMEM)],
        out_specs=pl.BlockSpec((None, NA_PATTERNS, GRID_W, NA_HEADS * NA_KEYS), lambda l: (l, 0, 0, 0)),
        out_shape=jax.ShapeDtypeStruct((DEPTH, NA_PATTERNS, GRID_W, NA_HEADS * NA_KEYS), F32),
        compiler_params=pltpu.CompilerParams(dimension_semantics=("arbitrary",)),
        name="na_bias_tables",
    )(na_rpb.reshape(-1))


def _mod_spec(layer, ctx, n_grid):
    if n_grid == 1:
        index = (lambda s: (layer, CTX_MOD_ROW, 0, 0)) if ctx else (lambda s: (layer, s, 0, 0))
    else:
        index = (lambda s, i: (layer, CTX_MOD_ROW, 0, 0)) if ctx else (lambda s, i: (layer, s, 0, 0))
    return pl.BlockSpec((None, None, 1, 6 * D_MODEL), index)


def _layer_spec(shape, layer, n_grid):
    zeros = (0,) * len(shape)
    index = (lambda s: (layer,) + zeros) if n_grid == 1 else (lambda s, i: (layer,) + zeros)
    return pl.BlockSpec((None,) + tuple(shape), index, pipeline_mode=pl.Buffered(1))


_COLUMN_GROUPS = ((C_CB, C_GQ), (C_GQ, C_NQ), (C_NQ, C_MQ), (C_MQ, MAIN_COLS))


def _modulated_norm(x, g, mod_ref, shift, scale):
    d = D_MODEL
    return _rms(x, g) * (1.0 + mod_ref[:, scale * d:(scale + 1) * d]) + mod_ref[:, shift * d:(shift + 1) * d]


def _projected_columns(h_ref, w_ref):
    cache = {}

    def col(c, n):
        lo, hi = next(g for g in _COLUMN_GROUPS if g[0] <= c and c + n <= g[1])
        if lo not in cache:
            cache.clear()
            cache[lo] = _dot(h_ref[...], w_ref[:, lo:hi])
        return cache[lo][:, c - lo:c - lo + n]

    return col


def _key_planes(k_t, even_first, odd_first):
    lo = _row_lt(k_t.shape, HEAD_DIM)
    swapped = None
    if not even_first or odd_first:
        swapped = pltpu.roll(k_t, HEAD_DIM, 0)
    top = jnp.where(lo, k_t if even_first else swapped, 0.0)
    bot = jnp.where(lo, 0.0, swapped if odd_first else k_t)
    return top.astype(BF16), bot.astype(BF16)


def _value_planes(pair, even_first, odd_first):
    lo = _lane_lt(pair.shape, HEAD_DIM)
    swapped = None
    if not even_first or odd_first:
        swapped = pltpu.roll(pair, HEAD_DIM, 1)
    top = jnp.where(lo, pair if even_first else swapped, 0.0)
    bot = jnp.where(lo, 0.0, swapped if odd_first else pair)
    one_e = jnp.where(lo, 1.0, 0.0)
    return (jnp.concatenate([top, one_e], axis=1).astype(BF16),
            jnp.concatenate([bot, 1.0 - one_e], axis=1).astype(BF16))


def _probabilities(parts, sink=None):
    m = parts[0].max(axis=-1, keepdims=True)
    for p in parts[1:]:
        m = jnp.maximum(m, p.max(axis=-1, keepdims=True))
    if sink is not None:
        m = jnp.maximum(m, sink)
    probs = [jnp.exp2((p - m).astype(BF16)) for p in parts]
    return probs, (None if sink is None else jnp.exp2(sink - m))


def _attend(terms, sink_e=None, sink_o=None):
    o = None
    for p, v in terms:
        t = _dot(p, v)
        o = t if o is None else o + t
    den = o[:, LANES:]
    if sink_e is not None:
        den = den + jnp.where(_lane_lt(den.shape, HEAD_DIM), sink_e, sink_o)
    return o[:, :LANES] / den


ONES_ROWS = 16


def _value_plane_t(v_t):
    return jnp.concatenate([v_t, jnp.ones((ONES_ROWS, v_t.shape[1]), v_t.dtype)], axis=0).astype(BF16)


def _probabilities_t(parts, sink=None):
    m = parts[0].max(axis=0, keepdims=True)
    for p in parts[1:]:
        m = jnp.maximum(m, p.max(axis=0, keepdims=True))
    if sink is not None:
        m = jnp.maximum(m, sink)
    probs = [jnp.exp2((p - m).astype(BF16)) for p in parts]
    return probs, (None if sink is None else jnp.exp2(sink - m))


def _attend_t(terms, sink_num=None):
    o = None
    for v_t, p in terms:
        t = _dot(v_t, p)
        o = t if o is None else o + t
    den = o[HEAD_DIM:HEAD_DIM + 1, :]
    if sink_num is not None:
        den = den + sink_num
    return o[0:HEAD_DIM, :] / den


def _mla_key_plane(kn_t_pair, kr4_t, j, odd):
    h = 2 * j + odd
    lo = _row_lt(kn_t_pair.shape, HEAD_DIM)
    nope = jnp.where(lo, 0.0, kn_t_pair) if odd else jnp.where(lo, kn_t_pair, 0.0)
    rope = jnp.where(_row_group(kr4_t.shape, h * MLA_ROPE, (h + 1) * MLA_ROPE), kr4_t, 0.0)
    return jnp.concatenate([nope, rope], axis=0).astype(BF16)


def _ctx_mixer_kernel(sink_ref, x_ref, mod_ref, ga_ref, win_ref, wc_ref, gq_ref, wuq_ref, gkv_ref, wukt_ref, wuv_ref,
                      *rest, layer, first):
    y_ref, kg_ref, vg_ref, kn_ref, vn_ref, ckv_ref, kr_ref, h_s = rest[-8:]
    t = SEQ
    for b in range(CTX_SEQS):
        h_s[b * t:(b + 1) * t, :] = _modulated_norm(x_ref[b], ga_ref[...], mod_ref, 0, 1).astype(BF16)
    col_all = _projected_columns(h_s, win_ref)

    def put_state(ref, b, val):
        if first:
            for d in range(DEPTH):
                ref[b, d] = val if d == layer else jnp.zeros_like(val)
        else:
            ref[b] = val

    def conv(b, col):
        y_ref[b, :, Y_CONV:Y_CONV + CONV_WIDTH] = _short_conv(
            col(C_CB, CONV_WIDTH), col(C_CC, CONV_WIDTH), col(C_CV, CONV_WIDTH), wc_ref[...]).astype(BF16)

    def gqa(b, col):
        kg_t, vpair = col(C_GK, LANES).T, col(C_GV, LANES)
        put_state(kg_ref, b, kg_t)
        put_state(vg_ref, b, vpair.T)
        row2 = lax.broadcasted_iota(jnp.int32, (2 * t, 1), 0) < t
        for g in range(GQA_KV_HEADS):
            ke, ko = _key_planes(kg_t, g == 0, g == 0)
            ve, vo = _value_planes(vpair, g == 0, g == 0)
            q = jnp.concatenate([col(C_GQ + (2 * g) * LANES, LANES), col(C_GQ + (2 * g + 1) * LANES, LANES)], axis=0)
            q = (q * (ATTN_SCALE * LOG2E)).astype(BF16)
            s = _dot(q, jnp.concatenate([ke, ko], axis=1))
            sink = [jnp.where(row2, sink_ref[layer, 4 * g + odd], sink_ref[layer, 4 * g + 2 + odd]) * LOG2E
                    for odd in range(2)]
            (pe,), xe = _probabilities([s[:, 0:t]], sink[0])
            (po,), xo = _probabilities([s[:, t:2 * t]], sink[1])
            o = _attend([(pe, ve), (po, vo)], xe, xo).astype(BF16)
            c0 = Y_GQA + (2 * g) * LANES
            y_ref[b, :, c0:c0 + LANES] = o[0:t]
            y_ref[b, :, c0 + LANES:c0 + 2 * LANES] = o[t:2 * t]

    def na(b, col):
        kn_t = col(C_NK, 2 * LANES).T
        put_state(kn_ref, b, kn_t)
        put_state(vn_ref, b, col(C_NV, 2 * LANES).T)
        for j in range(NA_HEADS // 2):
            ke, ko = _key_planes(kn_t[j * LANES:(j + 1) * LANES, :], True, False)
            ve, vo = _value_planes(col(C_NV + j * LANES, LANES), True, False)
            q = (col(C_NQ + j * LANES, LANES) * (ATTN_SCALE * LOG2E)).astype(BF16)
            s = _dot(q, jnp.concatenate([ke, ko], axis=1))
            (pe,), _ = _probabilities([s[:, 0:t]])
            (po,), _ = _probabilities([s[:, t:2 * t]])
            y_ref[b, :, Y_NA + j * LANES:Y_NA + (j + 1) * LANES] = _attend([(pe, ve), (po, vo)]).astype(BF16)

    def mla(b, col):
        ckv = _rms(col(C_MKV, MLA_KV_RANK), gkv_ref[...])
        put_state(ckv_ref, b, ckv)
        kr_t = col(C_MKR, LANES).T[0:MLA_ROPE, :]
        put_state(kr_ref, b, kr_t)
        ckv_b = ckv.astype(BF16)
        q = _dot(_rms(col(C_MQ, MLA_Q_RANK), gq_ref[...]).astype(BF16), wuq_ref[...]) * (MLA_SCALE * LOG2E)
        kn_t_all = _dot_nt(wukt_ref[...], ckv_b)
        v_all = _dot(ckv_b, wuv_ref[...])
        kr4_t = jnp.concatenate([kr_t] * MLA_HEADS, axis=0)
        q_rope = q[:, 2 * LANES:3 * LANES]
        for j in range(MLA_HEADS // 2):
            qj = jnp.concatenate([q[:, j * LANES:(j + 1) * LANES], q_rope], axis=1).astype(BF16)
            kn_t_pair = kn_t_all[j * LANES:(j + 1) * LANES, :]
            keys = jnp.concatenate(
                [_mla_key_plane(kn_t_pair, kr4_t, j, 0), _mla_key_plane(kn_t_pair, kr4_t, j, 1)], axis=1)
            s = _dot(qj, keys)
            ve, vo = _value_planes(v_all[:, j * LANES:(j + 1) * LANES], True, False)
            (pe,), _ = _probabilities([s[:, 0:t]])
            (po,), _ = _probabilities([s[:, t:2 * t]])
            y_ref[b, :, Y_MLA + j * LANES:Y_MLA + (j + 1) * LANES] = _attend([(pe, ve), (po, vo)]).astype(BF16)

    for mixer in (conv, gqa, na, mla):
        for b in range(CTX_SEQS):
            mixer(b, lambda c, n, b=b: col_all(c, n)[b * t:(b + 1) * t])


def _ctx_mixer_call(x, mods, g_attn, w_main, sink, w_conv, g_q, w_uq, g_kv, w_uk_t, w_uv, prev_states, layer, name):
    n, nb = BATCH, CTX_SEQS
    state_shapes = ((2 * HEAD_DIM, SEQ), (2 * HEAD_DIM, SEQ), (4 * HEAD_DIM, SEQ), (4 * HEAD_DIM, SEQ),
                    (SEQ, MLA_KV_RANK), (MLA_ROPE, SEQ))
    first = prev_states is None
    if first:
        state_spec = lambda shape: pl.BlockSpec((nb, DEPTH) + shape, lambda s: (s, 0, 0, 0))
    else:
        state_spec = lambda shape: pl.BlockSpec((nb, None) + shape, lambda s: (s, layer, 0, 0))
    in_specs = [
        pl.BlockSpec(memory_space=pltpu.SMEM),
        pl.BlockSpec((nb, SEQ, D_MODEL), lambda s: (s, 0, 0)),
        _mod_spec(layer, True, 1),
        _layer_spec((1, D_MODEL), layer, 1),
        _layer_spec((D_MODEL, MAIN_COLS), layer, 1),
        _layer_spec((CONV_K, CONV_WIDTH), layer, 1),
        _layer_spec((1, MLA_Q_RANK), layer, 1),
        _layer_spec((MLA_Q_RANK, 3 * LANES), layer, 1),
        _layer_spec((1, MLA_KV_RANK), layer, 1),
        _layer_spec((2 * LANES, MLA_KV_RANK), layer, 1),
        _layer_spec((MLA_KV_RANK, 2 * LANES), layer, 1),
    ]
    args = [sink, x, mods, g_attn, w_main, w_conv, g_q, w_uq, g_kv, w_uk_t, w_uv]
    aliases = {}
    if not first:
        for i, st in enumerate(prev_states):
            aliases[len(args)] = 1 + i
            in_specs.append(pl.BlockSpec(memory_space=pl.ANY))
            args.append(st)
    outs = pl.pallas_call(
        functools.partial(_ctx_mixer_kernel, layer=layer, first=first),
        grid=(n // nb,),
        in_specs=in_specs,
        out_specs=[pl.BlockSpec((nb, SEQ, Y_COLS), lambda s: (s, 0, 0))] + [state_spec(s) for s in state_shapes],
        out_shape=[jax.ShapeDtypeStruct((n, SEQ, Y_COLS), BF16)]
        + [jax.ShapeDtypeStruct((n, DEPTH) + s, F32) for s in state_shapes],
        input_output_aliases=aliases,
        scratch_shapes=[pltpu.VMEM((nb * SEQ, D_MODEL), BF16)],
        compiler_params=pltpu.CompilerParams(
            dimension_semantics=("arbitrary",), vmem_limit_bytes=VMEM_LIMIT),
        name=name,
    )(*args)
    return outs[0], outs[1:]


def _lat_mixer_kernel(sink_ref, x_ref, mod_ref, ga_ref, win_ref, wc_ref, gq_ref, wuq_ref, gkv_ref, wuk_ref, wuvt_ref,
                      rc_ref, rsh_ref, rsl_ref, mc_ref, msh_ref, msl_ref, band_ref, nab_ref,
                      cgk_ref, cgv_ref, cnk_ref, cnv_ref, cckv_ref, ckr_ref,
                      y_ref,
                      h_s, gq_s, gk_s, gv_s, gkc_s, gvc_s, nq_s, nk_s, nv_s, nkc_s, nvc_s, mq_s, mk_s, mv_s, *, layer):
    t = DEC_SEQ
    h_s[...] = _modulated_norm(x_ref[...], ga_ref[...], mod_ref, 0, 1).astype(BF16)
    col = _projected_columns(h_s, win_ref)
    rope64 = lambda x: _rope(x, rc_ref[...], rsh_ref[...], rsl_ref[...], HEAD_DIM // 4)
    rope32 = lambda x: _rope(x, mc_ref[...], msh_ref[...], msl_ref[...], MLA_ROPE // 4)

    y_ref[:, Y_CONV:Y_CONV + CONV_WIDTH] = _short_conv(
        col(C_CB, CONV_WIDTH), col(C_CC, CONV_WIDTH), col(C_CV, CONV_WIDTH), wc_ref[...]).astype(BF16)

    group = GQA_HEADS // GQA_KV_HEADS
    for j in range(GQA_HEADS // 2):
        q_pair = rope64(col(C_GQ + j * LANES, LANES)) * (ATTN_SCALE * LOG2E)
        gq_s[j * LANES:(j + 1) * LANES, :] = q_pair.T.astype(BF16)
    gk_s[...] = rope64(col(C_GK, LANES)).astype(BF16)
    gkc_s[...] = cgk_ref[...].T.astype(BF16)
    v_t = col(C_GV, LANES).T
    for g in range(GQA_KV_HEADS):
        gv_s[g] = _value_plane_t(v_t[g * HEAD_DIM:(g + 1) * HEAD_DIM, :])
        gvc_s[g] = _value_plane_t(cgv_ref[g * HEAD_DIM:(g + 1) * HEAD_DIM, :])

    def gqa_block(b, c_lo, c_hi):
        q0 = pl.multiple_of(b * BAND_BLOCK, BAND_BLOCK)
        k0 = pl.multiple_of(q0 + (c_lo - WINDOW), BAND_BLOCK)
        n = c_hi - c_lo
        band = band_ref[c_lo:c_hi, :]
        lane_head = lax.shift_right_logical(
            lax.broadcasted_iota(jnp.int32, (1, group * BAND_BLOCK), 1), BAND_BLOCK.bit_length() - 1)
        zeros = jnp.zeros((HEAD_DIM, group * BAND_BLOCK), BF16)
        for g in range(GQA_KV_HEADS):
            q_t = jnp.concatenate(
                [gq_s[(group * g + h) * HEAD_DIM:(group * g + h + 1) * HEAD_DIM, pl.ds(q0, BAND_BLOCK)]
                 for h in range(group)], axis=1)
            q_t = jnp.concatenate([q_t, zeros] if g == 0 else [zeros, q_t], axis=0)
            s_loc = _dot(gk_s[pl.ds(k0, n), :], q_t) + band
            s_ctx = _dot(gkc_s[...], q_t)
            sink = sink_ref[layer, group * g + group - 1]
            for h in range(group - 2, -1, -1):
                sink = jnp.where(lane_head == h, sink_ref[layer, group * g + h], sink)
            (p_loc, p_ctx), x = _probabilities_t([s_loc, s_ctx], sink * LOG2E)
            o = _attend_t([(gv_s[g, :, pl.ds(k0, n)], p_loc), (gvc_s[g], p_ctx)], x)
            for pr in range(group // 2):
                pair = jnp.concatenate([o[:, (2 * pr) * BAND_BLOCK:(2 * pr + 1) * BAND_BLOCK],
                                        o[:, (2 * pr + 1) * BAND_BLOCK:(2 * pr + 2) * BAND_BLOCK]], axis=0)
                c0 = Y_GQA + (group // 2 * g + pr) * LANES
                y_ref[pl.ds(q0, BAND_BLOCK), c0:c0 + LANES] = pair.T.astype(BF16)

    span = BAND_BLOCK + 2 * WINDOW
    nb = t // BAND_BLOCK
    gqa_block(0, WINDOW, span)
    lax.fori_loop(1, nb - 1, lambda b, c: (gqa_block(b, 0, span), c)[1], 0, unroll=True)
    gqa_block(nb - 1, 0, span - WINDOW)

    for j in range(NA_HEADS // 2):
        nq_s[:, j * LANES:(j + 1) * LANES] = (col(C_NQ + j * LANES, LANES) * (ATTN_SCALE * LOG2E)).astype(BF16)
        nk_t = col(C_NK + j * LANES, LANES).T
        for shift, k_t in enumerate((nk_t, pltpu.roll(nk_t, t - GRID_W, 1))):
            nk_s[shift, 2 * j], nk_s[shift, 2 * j + 1] = _key_planes(k_t, True, False)
        nv_s[2 * j], nv_s[2 * j + 1] = _value_planes(col(C_NV + j * LANES, LANES), True, False)
        nkc_s[j] = jnp.concatenate(_key_planes(cnk_ref[j * LANES:(j + 1) * LANES, :], True, False), axis=1)
        nvc_s[2 * j], nvc_s[2 * j + 1] = _value_planes(cnv_ref[j * LANES:(j + 1) * LANES, :].T, True, False)

    def na_row(r, carry):
        q0 = pl.multiple_of(r * GRID_W, GRID_W)
        r0 = jnp.clip(r - NA_WIN_H // 2, 0, GRID_ROWS - NA_WIN_H)
        k0 = pl.multiple_of(r0 * GRID_W, GRID_W)
        kt0 = pl.multiple_of(lax.shift_right_logical(r0, 1) * LANES, LANES)
        pat = r - r0
        for j in range(NA_HEADS // 2):
            q = nq_s[pl.ds(q0, GRID_W), j * LANES:(j + 1) * LANES]
            s_ctx = _dot(q, nkc_s[j])
            terms = []
            for odd in range(2):
                h = 2 * j + odd
                s_loc = _dot(q, nk_s[r0 & 1, h, :, pl.ds(kt0, NA_KEYS)])
                s_loc = s_loc + nab_ref[pat, :, h * NA_KEYS:(h + 1) * NA_KEYS]
                (p_loc, p_ctx), _ = _probabilities([s_loc, s_ctx[:, odd * PAST_LEN:(odd + 1) * PAST_LEN]])
                terms += [(p_loc, nv_s[h, pl.ds(k0, NA_KEYS), :]), (p_ctx, nvc_s[h])]
            y_ref[pl.ds(q0, GRID_W), Y_NA + j * LANES:Y_NA + (j + 1) * LANES] = _attend(terms).astype(BF16)
        return carry

    lax.fori_loop(0, GRID_ROWS, na_row, 0, unroll=True)

    ckv_b = _rms(col(C_MKV, MLA_KV_RANK), gkv_ref[...]).astype(BF16)
    cckv_b = cckv_ref[...].astype(BF16)
    q = _dot(_rms(col(C_MQ, MLA_Q_RANK), gq_ref[...]).astype(BF16), wuq_ref[...]) * (MLA_SCALE * LOG2E)
    for i in range(3):
        tile = q[:, i * LANES:(i + 1) * LANES]
        mq_s[i * LANES:(i + 1) * LANES, :] = (rope32(tile) if i == 2 else tile).T.astype(BF16)
    kr = jnp.where(_lane_lt((t, LANES), MLA_ROPE), rope32(col(C_MKR, LANES)), 0.0)
    kr_c = jnp.concatenate([ckr_ref[...], jnp.zeros((LANES - MLA_ROPE, PAST_LEN), F32)], axis=0).T
    for rows, ckv_x, kr_x in ((slice(0, t), ckv_b, kr), (slice(t, t + PAST_LEN), cckv_b, kr_c)):
        kn_all = _dot(ckv_x, wuk_ref[...])
        v_t_all = _dot_nt(wuvt_ref[...], ckv_x)
        for j in range(MLA_HEADS // 2):
            mk_s[j, rows, :] = jnp.concatenate([kn_all[:, j * LANES:(j + 1) * LANES], kr_x], axis=1).astype(BF16)
        for h in range(MLA_HEADS):
            mv_s[h, :, rows] = _value_plane_t(v_t_all[h * MLA_V:(h + 1) * MLA_V, :])

    tq = 256

    def mla_block(i, carry):
        q0 = pl.multiple_of(i * tq, tq)
        zeros = lambda r: jnp.zeros((r, tq), BF16)
        q_nope = lambda h: mq_s[h * MLA_NOPE:(h + 1) * MLA_NOPE, pl.ds(q0, tq)]
        q_rope = lambda h: mq_s[MLA_HEADS * MLA_NOPE + h * MLA_ROPE:MLA_HEADS * MLA_NOPE + (h + 1) * MLA_ROPE,
                                pl.ds(q0, tq)]
        pad = 2 * LANES - 2 * MLA_NOPE - MLA_ROPE
        for j in range(MLA_HEADS // 2):
            he, ho = 2 * j, 2 * j + 1
            q_e = jnp.concatenate([q_nope(he), zeros(MLA_NOPE), q_rope(he), zeros(pad)], axis=0)
            q_o = jnp.concatenate([zeros(MLA_NOPE), q_nope(ho), q_rope(ho), zeros(pad)], axis=0)
            s = _dot(mk_s[j], jnp.concatenate([q_e, q_o], axis=1))
            outs = []
            for odd in range(2):
                (p,), _ = _probabilities_t([s[:, odd * tq:(odd + 1) * tq]])
                outs.append(_attend_t([(mv_s[2 * j + odd], p)]))
            y_ref[pl.ds(q0, tq), Y_MLA + j * LANES:Y_MLA + (j + 1) * LANES] = (
                jnp.concatenate(outs, axis=0).T.astype(BF16))
        return carry

    lax.fori_loop(0, t // tq, mla_block, 0, unroll=True)


def _rope_tables(group, half):
    tok = np.arange(DEC_SEQ)
    pos = np.stack([tok // GRID_W, tok % GRID_W], axis=1).astype(np.float64)
    inv = ROPE_BASE ** (-np.arange(half, dtype=np.float64) / half)
    lane = np.arange(LANES) % group
    axis = lane // (2 * half)
    within = lane % (2 * half)
    ang = pos[:, axis] * inv[within % half][None, :]
    cos, sin = np.cos(ang), np.sin(ang)
    upper = (within >= half)[None, :]
    sin_hi = np.where(upper, sin, 0.0)
    sin_lo = np.where(upper, 0.0, -sin)
    return tuple(jnp.asarray(a, dtype=F32) for a in (cos, sin_hi, sin_lo))


def _band_mask():
    c = np.arange(BAND_BLOCK + 2 * WINDOW)[:, None]
    i = np.arange(BAND_BLOCK)[None, :]
    ok = (c >= i) & (c <= i + 2 * WINDOW)
    m = np.where(ok, 0.0, NEG_INF)
    return jnp.asarray(np.concatenate([m] * (GQA_HEADS // GQA_KV_HEADS), axis=1), dtype=F32)


def _lat_mixer_call(x, mods, g_attn, w_main, sink, w_conv, g_q, w_uq, g_kv, w_uk, w_uv_t, nab, caches, layer, name):
    n, t = DEC_BATCH, DEC_SEQ
    one = pl.Buffered(1)
    const = lambda shape: pl.BlockSpec(shape, lambda s: (0,) * len(shape), pipeline_mode=one)
    cache = lambda shape: pl.BlockSpec((None, None) + shape, lambda s: (s, layer, 0, 0), pipeline_mode=one)
    rope_g = _rope_tables(HEAD_DIM, HEAD_DIM // 4)
    rope_m = _rope_tables(MLA_ROPE, MLA_ROPE // 4)
    span = BAND_BLOCK + 2 * WINDOW
    keys = t + PAST_LEN
    scratch = [
        pltpu.VMEM((t, D_MODEL), BF16),
        pltpu.VMEM((GQA_HEADS * HEAD_DIM, t), BF16),
        pltpu.VMEM((t, LANES), BF16),
        pltpu.VMEM((GQA_KV_HEADS, HEAD_DIM + ONES_ROWS, t), BF16),
        pltpu.VMEM((PAST_LEN, LANES), BF16),
        pltpu.VMEM((GQA_KV_HEADS, HEAD_DIM + ONES_ROWS, PAST_LEN), BF16),
        pltpu.VMEM((t, NA_HEADS // 2 * LANES), BF16),
        pltpu.VMEM((2, NA_HEADS, LANES, t), BF16),
        pltpu.VMEM((NA_HEADS, t, 2 * LANES), BF16),
        pltpu.VMEM((NA_HEADS // 2, LANES, 2 * PAST_LEN), BF16),
        pltpu.VMEM((NA_HEADS, PAST_LEN, 2 * LANES), BF16),
        pltpu.VMEM((3 * LANES, t), BF16),
        pltpu.VMEM((MLA_HEADS // 2, keys, 2 * LANES), BF16),
        pltpu.VMEM((MLA_HEADS, MLA_V + ONES_ROWS, keys), BF16),
    ]
    return pl.pallas_call(
        functools.partial(_lat_mixer_kernel, layer=layer),
        grid=(n,),
        in_specs=[
            pl.BlockSpec(memory_space=pltpu.SMEM),
            pl.BlockSpec((None, t, D_MODEL), lambda s: (s, 0, 0), pipeline_mode=one),
            _mod_spec(layer, False, 1),
            _layer_spec((1, D_MODEL), layer, 1),
            _layer_spec((D_MODEL, MAIN_COLS), layer, 1),
            _layer_spec((CONV_K, CONV_WIDTH), layer, 1),
            _layer_spec((1, MLA_Q_RANK), layer, 1),
            _layer_spec((MLA_Q_RANK, 3 * LANES), layer, 1),
            _layer_spec((1, MLA_KV_RANK), layer, 1),
            _layer_spec((MLA_KV_RANK, 2 * LANES), layer, 1),
            _layer_spec((2 * LANES, MLA_KV_RANK), layer, 1),
        ] + [const((t, LANES))] * 6 + [
            const((span, GQA_HEADS // GQA_KV_HEADS * BAND_BLOCK)),
            _layer_spec((NA_PATTERNS, GRID_W, NA_HEADS * NA_KEYS), layer, 1),
            cache((2 * HEAD_DIM, PAST_LEN)), cache((2 * HEAD_DIM, PAST_LEN)),
            cache((4 * HEAD_DIM, PAST_LEN)), cache((4 * HEAD_DIM, PAST_LEN)),
            cache((PAST_LEN, MLA_KV_RANK)), cache((MLA_ROPE, PAST_LEN)),
        ],
        out_specs=pl.BlockSpec((None, t, Y_COLS), lambda s: (s, 0, 0)),
        out_shape=jax.ShapeDtypeStruct((n, t, Y_COLS), BF16),
        scratch_shapes=scratch,
        compiler_params=pltpu.CompilerParams(
            dimension_semantics=("arbitrary",), vmem_limit_bytes=VMEM_LIMIT),
        name=name,
    )(sink, x, mods, g_attn, w_main, w_conv, g_q, w_uq, g_kv, w_uk, w_uv_t, *rope_g, *rope_m, _band_mask(), nab, *caches)


def _post_kernel(x_ref, y_ref, mod_ref, ga_ref, gm_ref, gf_ref, wg_ref, wb_ref, wo_ref, w1_ref, w2_ref,
                 o_ref, *, final):
    d = D_MODEL
    mod = lambda i: mod_ref[:, i * d:(i + 1) * d]
    bounds = (Y_CONV, Y_GQA, Y_NA, Y_MLA, Y_COLS)
    for r0 in range(0, x_ref.shape[0], POST_CHAIN_ROWS):
        rows = slice(r0, r0 + POST_CHAIN_ROWS)
        x = x_ref[rows, :]
        h = _modulated_norm(x, ga_ref[...], mod_ref, 0, 1).astype(BF16)
        merged = None
        for i in range(N_BRANCH):
            lo, hi = bounds[i], bounds[i + 1]
            gate = jax.nn.sigmoid(_dot(h, wg_ref[:, i * d:(i + 1) * d]))
            term = gate * _dot(y_ref[rows, lo:hi], wb_ref[lo:hi, :])
            merged = term if merged is None else merged + term
        x = x + mod(2) * _dot(merged.astype(BF16), wo_ref[...])
        h = _modulated_norm(x, gm_ref[...], mod_ref, 3, 4).astype(BF16)
        mlp = None
        for c0 in range(0, D_FF, d):
            f = jnp.square(jnp.maximum(_dot(h, w1_ref[:, c0:c0 + d]), 0.0)).astype(BF16)
            term = _dot(f, w2_ref[c0:c0 + d, :])
            mlp = term if mlp is None else mlp + term
        x = x + mod(5) * mlp
        o_ref[rows, :] = _rms(x, gf_ref[...]) if final else x


def _post_call(x, y, mods, g_attn, g_mlp, g_final, w_gates, w_br, w_o, w_ff1, w_ff2, layer, ctx, name):
    n, t, _ = x.shape
    tm = 2 * POST_CHAIN_ROWS
    return pl.pallas_call(
        functools.partial(_post_kernel, final=layer == DEPTH - 1),
        grid=(n, t // tm),
        in_specs=[
            pl.BlockSpec((None, tm, D_MODEL), lambda s, i: (s, i, 0)),
            pl.BlockSpec((None, tm, Y_COLS), lambda s, i: (s, i, 0)),
            _mod_spec(layer, ctx, 2),
            _layer_spec((1, D_MODEL), layer, 2),
            _layer_spec((1, D_MODEL), layer, 2),
            pl.BlockSpec((1, D_MODEL), lambda s, i: (0, 0)),
            _layer_spec((D_MODEL, N_BRANCH * D_MODEL), layer, 2),
            _layer_spec((Y_COLS, D_MODEL), layer, 2),
            _layer_spec((D_MODEL, D_MODEL), layer, 2),
            _layer_spec((D_MODEL, D_FF), layer, 2),
            _layer_spec((D_FF, D_MODEL), layer, 2),
        ],
        out_specs=pl.BlockSpec((None, tm, D_MODEL), lambda s, i: (s, i, 0)),
        out_shape=jax.ShapeDtypeStruct((n, t, D_MODEL), F32),
        compiler_params=pltpu.CompilerParams(
            dimension_semantics=("arbitrary", "arbitrary"), vmem_limit_bytes=VMEM_LIMIT),
        name=name,
    )(x, y, mods, g_attn, g_mlp, g_final, w_gates, w_br, w_o, w_ff1, w_ff2)


def kernel(x_prompt, x_sample, cache_gqa_k, cache_gqa_v, cache_na_k, cache_na_v, cache_mla_ckv, cache_mla_krope, c, c_ctx, w_mod, b_mod, g_attn, g_mlp, w_in, w_conv, gqa_sink, na_rpb, mla_g_q, mla_w_uq, mla_g_kv, mla_w_ukv, w_branch_conv, w_branch_gqa, w_branch_na, w_branch_mla, w_o, w_ff1, w_ff2, g_final):
    w_main = w_in[:, :, :MAIN_COLS].astype(BF16)
    w_gates = w_in[:, :, GATE_COL0:].astype(BF16)
    uq = mla_w_uq.reshape(DEPTH, MLA_Q_RANK, MLA_HEADS, MLA_NOPE + MLA_ROPE)
    w_uq = jnp.concatenate([uq[..., :MLA_NOPE].reshape(DEPTH, MLA_Q_RANK, -1),
                            uq[..., MLA_NOPE:].reshape(DEPTH, MLA_Q_RANK, -1)], axis=-1).astype(BF16)
    ukv = mla_w_ukv.reshape(DEPTH, MLA_KV_RANK, MLA_HEADS, MLA_NOPE + MLA_V)
    w_uk = ukv[..., :MLA_NOPE].reshape(DEPTH, MLA_KV_RANK, -1).astype(BF16)
    w_uv = ukv[..., MLA_NOPE:].reshape(DEPTH, MLA_KV_RANK, -1).astype(BF16)
    w_uk_t, w_uv_t = jnp.transpose(w_uk, (0, 2, 1)), jnp.transpose(w_uv, (0, 2, 1))
    w_br = jnp.concatenate([w_branch_conv, w_branch_gqa, w_branch_na, w_branch_mla], axis=1).astype(BF16)
    w_o_b, w_ff1_b, w_ff2_b = w_o.astype(BF16), w_ff1.astype(BF16), w_ff2.astype(BF16)
    g_a, g_m = g_attn[:, None, :], g_mlp[:, None, :]
    g_q, g_kv, g_f = mla_g_q[:, None, :], mla_g_kv[:, None, :], g_final[None, :]

    c16 = jnp.concatenate([c, c_ctx[None, :], jnp.zeros((MOD_ROWS - DEC_BATCH - 1, D_MODEL), F32)], axis=0)
    mods = _mod_call(c16, w_mod, b_mod).reshape(DEPTH, MOD_ROWS, 1, 6 * D_MODEL)
    nab = _nab_call(na_rpb)

    heads_t = lambda a: jnp.transpose(a, (0, 1, 3, 4, 2)).reshape(a.shape[0], DEPTH, -1, a.shape[2])
    caches = (heads_t(cache_gqa_k), heads_t(cache_gqa_v), heads_t(cache_na_k), heads_t(cache_na_v),
              cache_mla_ckv, jnp.transpose(cache_mla_krope, (0, 1, 3, 2)))

    h_ctx, h_lat = x_prompt, x_sample
    states = None
    for l in range(DEPTH):
        mixer_w = (gqa_sink, w_conv, g_q, w_uq, g_kv)
        post_w = (g_a, g_m, g_f, w_gates, w_br, w_o_b, w_ff1_b, w_ff2_b)

        y, states = _ctx_mixer_call(h_ctx, mods, g_a, w_main, *mixer_w, w_uk_t, w_uv, states, l, f"mixer_ctx_{l}")
        flat = lambda a: a.reshape(1, BATCH * SEQ, a.shape[-1])
        h_ctx = _post_call(flat(h_ctx), flat(y), mods, *post_w, l, True, f"post_ctx_{l}").reshape(BATCH, SEQ, D_MODEL)

        y = _lat_mixer_call(h_lat, mods, g_a, w_main, *mixer_w, w_uk, w_uv_t, nab, caches, l, f"mixer_lat_{l}")
        h_lat = _post_call(h_lat, y, mods, *post_w, l, False, f"post_lat_{l}")

    def heads_out(a, heads):
        return jnp.transpose(a.reshape(BATCH, DEPTH, heads, HEAD_DIM, SEQ), (0, 1, 4, 2, 3))

    kg, vg, kn, vn, ckv, kr = states
    return (h_ctx, h_lat, heads_out(kg, GQA_KV_HEADS), heads_out(vg, GQA_KV_HEADS),
            heads_out(kn, NA_HEADS), heads_out(vn, NA_HEADS), ckv, jnp.transpose(kr, (0, 1, 3, 2)))
```

```python
import functools
import math

import numpy as np
import jax
import jax.numpy as jnp
from jax import lax
from jax.experimental import pallas as pl
from jax.experimental.pallas import tpu as pltpu

D_MODEL = 1024
BATCH = 32
SEQ = 256
DEPTH = 2
DEC_BATCH = 8
DEC_SEQ = 1024
PAST_LEN = 512
GRID_W = 64
GRID_ROWS = DEC_SEQ // GRID_W
HEAD_DIM = 64
CONV_WIDTH = 256
CONV_K = 3
GQA_HEADS = 8
GQA_KV_HEADS = 2
WINDOW = 128
BAND_BLOCK = 128
NA_HEADS = 4
NA_WIN_H = 8
NA_WIN_W = 16
MLA_HEADS = 4
MLA_Q_RANK = 256
MLA_KV_RANK = 128
MLA_NOPE = 64
MLA_ROPE = 32
MLA_V = 64
D_FF = 4 * D_MODEL
N_BRANCH = 4
ROPE_BASE = 10000.0
EPS = 1e-6
NEG_INF = -1e30
LOG2E = math.log2(math.e)
ATTN_SCALE = HEAD_DIM ** -0.5
MLA_SCALE = (MLA_NOPE + MLA_ROPE) ** -0.5

LANES = 128
MOD_ROWS = 16
CTX_MOD_ROW = DEC_BATCH

C_CB, C_CC, C_CV = 0, 256, 512
C_GQ, C_GK, C_GV = 768, 1280, 1408
C_NQ, C_NK, C_NV = 1536, 1792, 2048
C_MQ, C_MKV, C_MKR = 2304, 2560, 2688
MAIN_COLS = 2816
GATE_COL0 = 2720
Y_CONV, Y_GQA, Y_NA, Y_MLA = 0, 256, 768, 1024
Y_COLS = 1280
NA_PATTERNS = 8
NA_KEYS = NA_WIN_H * GRID_W

VMEM_LIMIT = 56 * 1024 * 1024
POST_CHAIN_ROWS = 256
CTX_SEQS = 2

F32 = jnp.float32
BF16 = jnp.bfloat16


def _dot(a, b):
    return jnp.dot(a, b, preferred_element_type=F32)


def _dot_nt(a, b):
    return lax.dot_general(a, b, (((1,), (1,)), ((), ())), preferred_element_type=F32)


def _rms(x, g):
    return x * lax.rsqrt(jnp.mean(x * x, axis=-1, keepdims=True) + EPS) * g


def _lane_lt(shape, n):
    return lax.broadcasted_iota(jnp.int32, shape, len(shape) - 1) < n


def _row_lt(shape, n):
    return lax.broadcasted_iota(jnp.int32, shape, 0) < n


def _row_group(shape, lo, hi):
    row = lax.broadcasted_iota(jnp.int32, shape, 0)
    return (row >= lo) & (row < hi)


def _short_conv(cb, cc, cv, w):
    u = cc * cv
    t = u.shape[0]
    row = lax.broadcasted_iota(jnp.int32, u.shape, 0)
    prev = jnp.where(row == 0, 0.0, pltpu.roll(u, 1, 0))
    nxt = jnp.where(row == t - 1, 0.0, pltpu.roll(u, t - 1, 0))
    return cb * (prev * w[0:1, :] + u * w[1:2, :] + nxt * w[2:3, :])


def _rope(x, cos, sin_hi, sin_lo, half):
    n = x.shape[-1]
    return x * cos + pltpu.roll(x, n - half, 1) * sin_lo + pltpu.roll(x, half, 1) * sin_hi


def _repeat_rope_key(tile):
    k = jnp.where(_lane_lt(tile.shape, MLA_ROPE), tile, 0.0)
    k = k + pltpu.roll(k, MLA_ROPE, 1)
    return k + pltpu.roll(k, 2 * MLA_ROPE, 1)


def _mod_kernel(c_ref, w_ref, b_ref, o_ref):
    c = c_ref[...]
    s = c * jax.nn.sigmoid(c)
    o_ref[...] = _dot(s.astype(BF16), w_ref[...].astype(BF16)) + b_ref[...]


def _mod_call(c16, w_mod, b_mod):
    tn = 1536
    return pl.pallas_call(
        _mod_kernel,
        grid=(DEPTH, 6 * D_MODEL // tn),
        in_specs=[
            pl.BlockSpec((MOD_ROWS, D_MODEL), lambda l, j: (0, 0)),
            pl.BlockSpec((None, D_MODEL, tn), lambda l, j: (l, 0, j)),
            pl.BlockSpec((None, 1, tn), lambda l, j: (l, 0, j)),
        ],
        out_specs=pl.BlockSpec((None, MOD_ROWS, tn), lambda l, j: (l, 0, j)),
        out_shape=jax.ShapeDtypeStruct((DEPTH, MOD_ROWS, 6 * D_MODEL), F32),
        compiler_params=pltpu.CompilerParams(
            dimension_semantics=("arbitrary", "arbitrary"), vmem_limit_bytes=VMEM_LIMIT),
        name="adaln_mod",
    )(c16, w_mod, b_mod.reshape(DEPTH, 1, 6 * D_MODEL))


def _nab_kernel(rpb_ref, o_ref):
    l = pl.program_id(0)
    c = lax.broadcasted_iota(jnp.int32, (GRID_W, GRID_W), 0)
    w = lax.broadcasted_iota(jnp.int32, (GRID_W, GRID_W), 1)
    dc = w - c + (NA_WIN_W - 1)
    c0 = jnp.clip(c - NA_WIN_W // 2, 0, GRID_W - NA_WIN_W)
    outside = (w < c0) | (w >= c0 + NA_WIN_W)
    n_dr, n_dc = 2 * NA_WIN_H - 1, 2 * NA_WIN_W - 1
    for h in range(NA_HEADS):
        for d in range(n_dr):
            base = ((l * NA_HEADS + h) * n_dr + d) * n_dc
            tile = jnp.full((GRID_W, GRID_W), NEG_INF, F32)
            for j in range(n_dc):
                tile = jnp.where(dc == j, rpb_ref[base + j] * LOG2E, tile)
            tile = jnp.where(outside, NEG_INF, tile)
            for p in range(NA_PATTERNS):
                a = d - (NA_WIN_H - 1) + p
                if 0 <= a < NA_WIN_H:
                    col = h * NA_KEYS + a * GRID_W
                    o_ref[p, :, col:col + GRID_W] = tile


def _nab_call(na_rpb):
    return pl.pallas_call(
        _nab_kernel,
        grid=(DEPTH,),
        in_specs=[pl.BlockSpec(memory_space=pltpu.SMEM)],
        out_specs=pl.BlockSpec((None, NA_PATTERNS, GRID_W, NA_HEADS * NA_KEYS), lambda l: (l, 0, 0, 0)),
        out_shape=jax.ShapeDtypeStruct((DEPTH, NA_PATTERNS, GRID_W, NA_HEADS * NA_KEYS), F32),
        compiler_params=pltpu.CompilerParams(dimension_semantics=("arbitrary",)),
        name="na_bias_tables",
    )(na_rpb.reshape(-1))


def _mod_spec(layer, ctx, n_grid):
    if n_grid == 1:
        index = (lambda s: (layer, CTX_MOD_ROW, 0, 0)) if ctx else (lambda s: (layer, s, 0, 0))
    else:
        index = (lambda s, i: (layer, CTX_MOD_ROW, 0, 0)) if ctx else (lambda s, i: (layer, s, 0, 0))
    return pl.BlockSpec((None, None, 1, 6 * D_MODEL), index)


def _layer_spec(shape, layer, n_grid):
    zeros = (0,) * len(shape)
    index = (lambda s: (layer,) + zeros) if n_grid == 1 else (lambda s, i: (layer,) + zeros)
    return pl.BlockSpec((None,) + tuple(shape), index, pipeline_mode=pl.Buffered(1))


_COLUMN_GROUPS = ((C_CB, C_GQ), (C_GQ, C_NQ), (C_NQ, C_MQ), (C_MQ, MAIN_COLS))


def _modulated_norm(x, g, mod_ref, shift, scale):
    d = D_MODEL
    return _rms(x, g) * (1.0 + mod_ref[:, scale * d:(scale + 1) * d]) + mod_ref[:, shift * d:(shift + 1) * d]


def _projected_columns(h_ref, w_ref):
    cache = {}

    def col(c, n):
        lo, hi = next(g for g in _COLUMN_GROUPS if g[0] <= c and c + n <= g[1])
        if lo not in cache:
            cache.clear()
            cache[lo] = _dot(h_ref[...], w_ref[:, lo:hi])
        return cache[lo][:, c - lo:c - lo + n]

    return col


def _key_planes(k_t, even_first, odd_first):
    lo = _row_lt(k_t.shape, HEAD_DIM)
    swapped = None
    if not even_first or odd_first:
        swapped = pltpu.roll(k_t, HEAD_DIM, 0)
    top = jnp.where(lo, k_t if even_first else swapped, 0.0)
    bot = jnp.where(lo, 0.0, swapped if odd_first else k_t)
    return top.astype(BF16), bot.astype(BF16)


def _value_planes(pair, even_first, odd_first):
    lo = _lane_lt(pair.shape, HEAD_DIM)
    swapped = None
    if not even_first or odd_first:
        swapped = pltpu.roll(pair, HEAD_DIM, 1)
    top = jnp.where(lo, pair if even_first else swapped, 0.0)
    bot = jnp.where(lo, 0.0, swapped if odd_first else pair)
    one_e = jnp.where(lo, 1.0, 0.0)
    return (jnp.concatenate([top, one_e], axis=1).astype(BF16),
            jnp.concatenate([bot, 1.0 - one_e], axis=1).astype(BF16))


def _probabilities(parts, sink=None):
    m = parts[0].max(axis=-1, keepdims=True)
    for p in parts[1:]:
        m = jnp.maximum(m, p.max(axis=-1, keepdims=True))
    if sink is not None:
        m = jnp.maximum(m, sink)
    probs = [jnp.exp2((p - m).astype(BF16)) for p in parts]
    return probs, (None if sink is None else jnp.exp2(sink - m))


def _attend(terms, sink_e=None, sink_o=None):
    o = None
    for p, v in terms:
        t = _dot(p, v)
        o = t if o is None else o + t
    den = o[:, LANES:]
    if sink_e is not None:
        den = den + jnp.where(_lane_lt(den.shape, HEAD_DIM), sink_e, sink_o)
    return o[:, :LANES] / den


ONES_ROWS = 16


def _value_plane_t(v_t):
    return jnp.concatenate([v_t, jnp.ones((ONES_ROWS, v_t.shape[1]), v_t.dtype)], axis=0).astype(BF16)


def _probabilities_t(parts, sink=None):
    m = parts[0].max(axis=0, keepdims=True)
    for p in parts[1:]:
        m = jnp.maximum(m, p.max(axis=0, keepdims=True))
    if sink is not None:
        m = jnp.maximum(m, sink)
    probs = [jnp.exp2((p - m).astype(BF16)) for p in parts]
    return probs, (None if sink is None else jnp.exp2(sink - m))


def _attend_t(terms, sink_num=None):
    o = None
    for v_t, p in terms:
        t = _dot(v_t, p)
        o = t if o is None else o + t
    den = o[HEAD_DIM:HEAD_DIM + 1, :]
    if sink_num is not None:
        den = den + sink_num
    return o[0:HEAD_DIM, :] / den


def _mla_key_plane(kn_t_pair, kr4_t, j, odd):
    h = 2 * j + odd
    lo = _row_lt(kn_t_pair.shape, HEAD_DIM)
    nope = jnp.where(lo, 0.0, kn_t_pair) if odd else jnp.where(lo, kn_t_pair, 0.0)
    rope = jnp.where(_row_group(kr4_t.shape, h * MLA_ROPE, (h + 1) * MLA_ROPE), kr4_t, 0.0)
    return jnp.concatenate([nope, rope], axis=0).astype(BF16)


def _ctx_mixer_kernel(sink_ref, x_ref, mod_ref, ga_ref, win_ref, wc_ref, gq_ref, wuq_ref, gkv_ref, wukt_ref, wuv_ref,
                      *rest, layer, first):
    y_ref, kg_ref, vg_ref, kn_ref, vn_ref, ckv_ref, kr_ref, h_s = rest[-8:]
    t = SEQ
    for b in range(CTX_SEQS):
        h_s[b * t:(b + 1) * t, :] = _modulated_norm(x_ref[b], ga_ref[...], mod_ref, 0, 1).astype(BF16)
    col_all = _projected_columns(h_s, win_ref)

    def put_state(ref, b, val):
        if first:
            for d in range(DEPTH):
                ref[b, d] = val if d == layer else jnp.zeros_like(val)
        else:
            ref[b] = val

    def conv(b, col):
        y_ref[b, :, Y_CONV:Y_CONV + CONV_WIDTH] = _short_conv(
            col(C_CB, CONV_WIDTH), col(C_CC, CONV_WIDTH), col(C_CV, CONV_WIDTH), wc_ref[...]).astype(BF16)

    def gqa(b, col):
        kg_t, vpair = col(C_GK, LANES).T, col(C_GV, LANES)
        put_state(kg_ref, b, kg_t)
        put_state(vg_ref, b, vpair.T)
        row2 = lax.broadcasted_iota(jnp.int32, (2 * t, 1), 0) < t
        for g in range(GQA_KV_HEADS):
            ke, ko = _key_planes(kg_t, g == 0, g == 0)
            ve, vo = _value_planes(vpair, g == 0, g == 0)
            q = jnp.concatenate([col(C_GQ + (2 * g) * LANES, LANES), col(C_GQ + (2 * g + 1) * LANES, LANES)], axis=0)
            q = (q * (ATTN_SCALE * LOG2E)).astype(BF16)
            s = _dot(q, jnp.concatenate([ke, ko], axis=1))
            sink = [jnp.where(row2, sink_ref[layer, 4 * g + odd], sink_ref[layer, 4 * g + 2 + odd]) * LOG2E
                    for odd in range(2)]
            (pe,), xe = _probabilities([s[:, 0:t]], sink[0])
            (po,), xo = _probabilities([s[:, t:2 * t]], sink[1])
            o = _attend([(pe, ve), (po, vo)], xe, xo).astype(BF16)
            c0 = Y_GQA + (2 * g) * LANES
            y_ref[b, :, c0:c0 + LANES] = o[0:t]
            y_ref[b, :, c0 + LANES:c0 + 2 * LANES] = o[t:2 * t]

    def na(b, col):
        kn_t = col(C_NK, 2 * LANES).T
        put_state(kn_ref, b, kn_t)
        put_state(vn_ref, b, col(C_NV, 2 * LANES).T)
        for j in range(NA_HEADS // 2):
            ke, ko = _key_planes(kn_t[j * LANES:(j + 1) * LANES, :], True, False)
            ve, vo = _value_planes(col(C_NV + j * LANES, LANES), True, False)
            q = (col(C_NQ + j * LANES, LANES) * (ATTN_SCALE * LOG2E)).astype(BF16)
            s = _dot(q, jnp.concatenate([ke, ko], axis=1))
            (pe,), _ = _probabilities([s[:, 0:t]])
            (po,), _ = _probabilities([s[:, t:2 * t]])
            y_ref[b, :, Y_NA + j * LANES:Y_NA + (j + 1) * LANES] = _attend([(pe, ve), (po, vo)]).astype(BF16)

    def mla(b, col):
        ckv = _rms(col(C_MKV, MLA_KV_RANK), gkv_ref[...])
        put_state(ckv_ref, b, ckv)
        kr_t = col(C_MKR, LANES).T[0:MLA_ROPE, :]
        put_state(kr_ref, b, kr_t)
        ckv_b = ckv.astype(BF16)
        q = _dot(_rms(col(C_MQ, MLA_Q_RANK), gq_ref[...]).astype(BF16), wuq_ref[...]) * (MLA_SCALE * LOG2E)
        kn_t_all = _dot_nt(wukt_ref[...], ckv_b)
        v_all = _dot(ckv_b, wuv_ref[...])
        kr4_t = jnp.concatenate([kr_t] * MLA_HEADS, axis=0)
        q_rope = q[:, 2 * LANES:3 * LANES]
        for j in range(MLA_HEADS // 2):
            qj = jnp.concatenate([q[:, j * LANES:(j + 1) * LANES], q_rope], axis=1).astype(BF16)
            kn_t_pair = kn_t_all[j * LANES:(j + 1) * LANES, :]
            keys = jnp.concatenate(
                [_mla_key_plane(kn_t_pair, kr4_t, j, 0), _mla_key_plane(kn_t_pair, kr4_t, j, 1)], axis=1)
            s = _dot(qj, keys)
            ve, vo = _value_planes(v_all[:, j * LANES:(j + 1) * LANES], True, False)
            (pe,), _ = _probabilities([s[:, 0:t]])
            (po,), _ = _probabilities([s[:, t:2 * t]])
            y_ref[b, :, Y_MLA + j * LANES:Y_MLA + (j + 1) * LANES] = _attend([(pe, ve), (po, vo)]).astype(BF16)

    for mixer in (conv, gqa, na, mla):
        for b in range(CTX_SEQS):
            mixer(b, lambda c, n, b=b: col_all(c, n)[b * t:(b + 1) * t])


def _ctx_mixer_call(x, mods, g_attn, w_main, sink, w_conv, g_q, w_uq, g_kv, w_uk_t, w_uv, prev_states, layer, name):
    n, nb = BATCH, CTX_SEQS
    state_shapes = ((2 * HEAD_DIM, SEQ), (2 * HEAD_DIM, SEQ), (4 * HEAD_DIM, SEQ), (4 * HEAD_DIM, SEQ),
                    (SEQ, MLA_KV_RANK), (MLA_ROPE, SEQ))
    first = prev_states is None
    if first:
        state_spec = lambda shape: pl.BlockSpec((nb, DEPTH) + shape, lambda s: (s, 0, 0, 0))
    else:
        state_spec = lambda shape: pl.BlockSpec((nb, None) + shape, lambda s: (s, layer, 0, 0))
    in_specs = [
        pl.BlockSpec(memory_space=pltpu.SMEM),
        pl.BlockSpec((nb, SEQ, D_MODEL), lambda s: (s, 0, 0)),
        _mod_spec(layer, True, 1),
        _layer_spec((1, D_MODEL), layer, 1),
        _layer_spec((D_MODEL, MAIN_COLS), layer, 1),
        _layer_spec((CONV_K, CONV_WIDTH), layer, 1),
        _layer_spec((1, MLA_Q_RANK), layer, 1),
        _layer_spec((MLA_Q_RANK, 3 * LANES), layer, 1),
        _layer_spec((1, MLA_KV_RANK), layer, 1),
        _layer_spec((2 * LANES, MLA_KV_RANK), layer, 1),
        _layer_spec((MLA_KV_RANK, 2 * LANES), layer, 1),
    ]
    args = [sink, x, mods, g_attn, w_main, w_conv, g_q, w_uq, g_kv, w_uk_t, w_uv]
    aliases = {}
    if not first:
        for i, st in enumerate(prev_states):
            aliases[len(args)] = 1 + i
            in_specs.append(pl.BlockSpec(memory_space=pl.ANY))
            args.append(st)
    outs = pl.pallas_call(
        functools.partial(_ctx_mixer_kernel, layer=layer, first=first),
        grid=(n // nb,),
        in_specs=in_specs,
        out_specs=[pl.BlockSpec((nb, SEQ, Y_COLS), lambda s: (s, 0, 0))] + [state_spec(s) for s in state_shapes],
        out_shape=[jax.ShapeDtypeStruct((n, SEQ, Y_COLS), BF16)]
        + [jax.ShapeDtypeStruct((n, DEPTH) + s, F32) for s in state_shapes],
        input_output_aliases=aliases,
        scratch_shapes=[pltpu.VMEM((nb * SEQ, D_MODEL), BF16)],
        compiler_params=pltpu.CompilerParams(
            dimension_semantics=("arbitrary",), vmem_limit_bytes=VMEM_LIMIT),
        name=name,
    )(*args)
    return outs[0], outs[1:]


def _lat_mixer_kernel(sink_ref, x_ref, mod_ref, ga_ref, win_ref, wc_ref, gq_ref, wuq_ref, gkv_ref, wuk_ref, wuvt_ref,
                      rc_ref, rsh_ref, rsl_ref, mc_ref, msh_ref, msl_ref, band_ref, nab_ref,
                      cgk_ref, cgv_ref, cnk_ref, cnv_ref, cckv_ref, ckr_ref,
                      y_ref,
                      h_s, gq_s, gk_s, gv_s, gkc_s, gvc_s, nq_s, nk_s, nv_s, nkc_s, nvc_s, mq_s, mk_s, mv_s, *, layer):
    t = DEC_SEQ
    h_s[...] = _modulated_norm(x_ref[...], ga_ref[...], mod_ref, 0, 1).astype(BF16)
    col = _projected_columns(h_s, win_ref)
    rope64 = lambda x: _rope(x, rc_ref[...], rsh_ref[...], rsl_ref[...], HEAD_DIM // 4)
    rope32 = lambda x: _rope(x, mc_ref[...], msh_ref[...], msl_ref[...], MLA_ROPE // 4)

    y_ref[:, Y_CONV:Y_CONV + CONV_WIDTH] = _short_conv(
        col(C_CB, CONV_WIDTH), col(C_CC, CONV_WIDTH), col(C_CV, CONV_WIDTH), wc_ref[...]).astype(BF16)

    group = GQA_HEADS // GQA_KV_HEADS
    for j in range(GQA_HEADS // 2):
        q_pair = rope64(col(C_GQ + j * LANES, LANES)) * (ATTN_SCALE * LOG2E)
        gq_s[j * LANES:(j + 1) * LANES, :] = q_pair.T.astype(BF16)
    gk_s[...] = rope64(col(C_GK, LANES)).astype(BF16)
    gkc_s[...] = cgk_ref[...].T.astype(BF16)
    v_t = col(C_GV, LANES).T
    for g in range(GQA_KV_HEADS):
        gv_s[g] = _value_plane_t(v_t[g * HEAD_DIM:(g + 1) * HEAD_DIM, :])
        gvc_s[g] = _value_plane_t(cgv_ref[g * HEAD_DIM:(g + 1) * HEAD_DIM, :])

    def gqa_block(b, c_lo, c_hi):
        q0 = pl.multiple_of(b * BAND_BLOCK, BAND_BLOCK)
        k0 = pl.multiple_of(q0 + (c_lo - WINDOW), BAND_BLOCK)
        n = c_hi - c_lo
        band = band_ref[c_lo:c_hi, :]
        lane_head = lax.shift_right_logical(
            lax.broadcasted_iota(jnp.int32, (1, group * BAND_BLOCK), 1), BAND_BLOCK.bit_length() - 1)
        zeros = jnp.zeros((HEAD_DIM, group * BAND_BLOCK), BF16)
        for g in range(GQA_KV_HEADS):
            q_t = jnp.concatenate(
                [gq_s[(group * g + h) * HEAD_DIM:(group * g + h + 1) * HEAD_DIM, pl.ds(q0, BAND_BLOCK)]
                 for h in range(group)], axis=1)
            q_t = jnp.concatenate([q_t, zeros] if g == 0 else [zeros, q_t], axis=0)
            s_loc = _dot(gk_s[pl.ds(k0, n), :], q_t) + band
            s_ctx = _dot(gkc_s[...], q_t)
            sink = sink_ref[layer, group * g + group - 1]
            for h in range(group - 2, -1, -1):
                sink = jnp.where(lane_head == h, sink_ref[layer, group * g + h], sink)
            (p_loc, p_ctx), x = _probabilities_t([s_loc, s_ctx], sink * LOG2E)
            o = _attend_t([(gv_s[g, :, pl.ds(k0, n)], p_loc), (gvc_s[g], p_ctx)], x)
            for pr in range(group // 2):
                pair = jnp.concatenate([o[:, (2 * pr) * BAND_BLOCK:(2 * pr + 1) * BAND_BLOCK],
                                        o[:, (2 * pr + 1) * BAND_BLOCK:(2 * pr + 2) * BAND_BLOCK]], axis=0)
                c0 = Y_GQA + (group // 2 * g + pr) * LANES
                y_ref[pl.ds(q0, BAND_BLOCK), c0:c0 + LANES] = pair.T.astype(BF16)

    for j in range(NA_HEADS // 2):
        nq_s[:, j * LANES:(j + 1) * LANES] = (col(C_NQ + j * LANES, LANES) * (ATTN_SCALE * LOG2E)).astype(BF16)
        nk_t = col(C_NK + j * LANES, LANES).T
        for shift, k_t in enumerate((nk_t, pltpu.roll(nk_t, t - GRID_W, 1))):
            nk_s[shift, 2 * j], nk_s[shift, 2 * j + 1] = _key_planes(k_t, True, False)
        nv_s[2 * j], nv_s[2 * j + 1] = _value_planes(col(C_NV + j * LANES, LANES), True, False)
        nkc_s[j] = jnp.concatenate(_key_planes(cnk_ref[j * LANES:(j + 1) * LANES, :], True, False), axis=1)
        nvc_s[2 * j], nvc_s[2 * j + 1] = _value_planes(cnv_ref[j * LANES:(j + 1) * LANES, :].T, True, False)

    def na_row(r, carry):
        q0 = pl.multiple_of(r * GRID_W, GRID_W)
        r0 = jnp.clip(r - NA_WIN_H // 2, 0, GRID_ROWS - NA_WIN_H)
        k0 = pl.multiple_of(r0 * GRID_W, GRID_W)
        kt0 = pl.multiple_of(lax.shift_right_logical(r0, 1) * LANES, LANES)
        pat = r - r0
        for j in range(NA_HEADS // 2):
            q = nq_s[pl.ds(q0, GRID_W), j * LANES:(j + 1) * LANES]
            s_ctx = _dot(q, nkc_s[j])
            terms = []
            for odd in range(2):
                h = 2 * j + odd
                s_loc = _dot(q, nk_s[r0 & 1, h, :, pl.ds(kt0, NA_KEYS)])
                s_loc = s_loc + nab_ref[pat, :, h * NA_KEYS:(h + 1) * NA_KEYS]
                (p_loc, p_ctx), _ = _probabilities([s_loc, s_ctx[:, odd * PAST_LEN:(odd + 1) * PAST_LEN]])
                terms += [(p_loc, nv_s[h, pl.ds(k0, NA_KEYS), :]), (p_ctx, nvc_s[h])]
            y_ref[pl.ds(q0, GRID_W), Y_NA + j * LANES:Y_NA + (j + 1) * LANES] = _attend(terms).astype(BF16)
        return carry

    ckv_b = _rms(col(C_MKV, MLA_KV_RANK), gkv_ref[...]).astype(BF16)
    cckv_b = cckv_ref[...].astype(BF16)
    q = _dot(_rms(col(C_MQ, MLA_Q_RANK), gq_ref[...]).astype(BF16), wuq_ref[...]) * (MLA_SCALE * LOG2E)
    for i in range(3):
        tile = q[:, i * LANES:(i + 1) * LANES]
        mq_s[i * LANES:(i + 1) * LANES, :] = (rope32(tile) if i == 2 else tile).T.astype(BF16)
    kr = jnp.where(_lane_lt((t, LANES), MLA_ROPE), rope32(col(C_MKR, LANES)), 0.0)
    kr_c = jnp.concatenate([ckr_ref[...], jnp.zeros((LANES - MLA_ROPE, PAST_LEN), F32)], axis=0).T
    for rows, ckv_x, kr_x in ((slice(0, t), ckv_b, kr), (slice(t, t + PAST_LEN), cckv_b, kr_c)):
        kn_all = _dot(ckv_x, wuk_ref[...])
        v_t_all = _dot_nt(wuvt_ref[...], ckv_x)
        for j in range(MLA_HEADS // 2):
            mk_s[j, rows, :] = jnp.concatenate([kn_all[:, j * LANES:(j + 1) * LANES], kr_x], axis=1).astype(BF16)
        for h in range(MLA_HEADS):
            mv_s[h, :, rows] = _value_plane_t(v_t_all[h * MLA_V:(h + 1) * MLA_V, :])

    tq = 256

    def mla_block(i, carry):
        q0 = pl.multiple_of(i * tq, tq)
        zeros = lambda r: jnp.zeros((r, tq), BF16)
        q_nope = lambda h: mq_s[h * MLA_NOPE:(h + 1) * MLA_NOPE, pl.ds(q0, tq)]
        q_rope = lambda h: mq_s[MLA_HEADS * MLA_NOPE + h * MLA_ROPE:MLA_HEADS * MLA_NOPE + (h + 1) * MLA_ROPE,
                                pl.ds(q0, tq)]
        pad = 2 * LANES - 2 * MLA_NOPE - MLA_ROPE
        for j in range(MLA_HEADS // 2):
            he, ho = 2 * j, 2 * j + 1
            q_e = jnp.concatenate([q_nope(he), zeros(MLA_NOPE), q_rope(he), zeros(pad)], axis=0)
            q_o = jnp.concatenate([zeros(MLA_NOPE), q_nope(ho), q_rope(ho), zeros(pad)], axis=0)
            s = _dot(mk_s[j], jnp.concatenate([q_e, q_o], axis=1))
            outs = []
            for odd in range(2):
                (p,), _ = _probabilities_t([s[:, odd * tq:(odd + 1) * tq]])
                outs.append(_attend_t([(mv_s[2 * j + odd], p)]))
            y_ref[pl.ds(q0, tq), Y_MLA + j * LANES:Y_MLA + (j + 1) * LANES] = (
                jnp.concatenate(outs, axis=0).T.astype(BF16))
        return carry

    span = BAND_BLOCK + 2 * WINDOW
    nb = t // BAND_BLOCK
    for i in range(nb):
        gqa_block(jnp.int32(i), WINDOW if i == 0 else 0, span - WINDOW if i == nb - 1 else span)
        for r in range(i * GRID_ROWS // nb, (i + 1) * GRID_ROWS // nb):
            na_row(jnp.int32(r), 0)
        if i % (nb * tq // t) == 0:
            mla_block(jnp.int32(i // (nb * tq // t)), 0)


def _rope_tables(group, half):
    tok = np.arange(DEC_SEQ)
    pos = np.stack([tok // GRID_W, tok % GRID_W], axis=1).astype(np.float64)
    inv = ROPE_BASE ** (-np.arange(half, dtype=np.float64) / half)
    lane = np.arange(LANES) % group
    axis = lane // (2 * half)
    within = lane % (2 * half)
    ang = pos[:, axis] * inv[within % half][None, :]
    cos, sin = np.cos(ang), np.sin(ang)
    upper = (within >= half)[None, :]
    sin_hi = np.where(upper, sin, 0.0)
    sin_lo = np.where(upper, 0.0, -sin)
    return tuple(jnp.asarray(a, dtype=F32) for a in (cos, sin_hi, sin_lo))


def _band_mask():
    c = np.arange(BAND_BLOCK + 2 * WINDOW)[:, None]
    i = np.arange(BAND_BLOCK)[None, :]
    ok = (c >= i) & (c <= i + 2 * WINDOW)
    m = np.where(ok, 0.0, NEG_INF)
    return jnp.asarray(np.concatenate([m] * (GQA_HEADS // GQA_KV_HEADS), axis=1), dtype=F32)


def _lat_mixer_call(x, mods, g_attn, w_main, sink, w_conv, g_q, w_uq, g_kv, w_uk, w_uv_t, nab, caches, layer, name):
    n, t = DEC_BATCH, DEC_SEQ
    one = pl.Buffered(1)
    const = lambda shape: pl.BlockSpec(shape, lambda s: (0,) * len(shape), pipeline_mode=one)
    cache = lambda shape: pl.BlockSpec((None, None) + shape, lambda s: (s, layer, 0, 0), pipeline_mode=one)
    rope_g = _rope_tables(HEAD_DIM, HEAD_DIM // 4)
    rope_m = _rope_tables(MLA_ROPE, MLA_ROPE // 4)
    span = BAND_BLOCK + 2 * WINDOW
    keys = t + PAST_LEN
    scratch = [
        pltpu.VMEM((t, D_MODEL), BF16),
        pltpu.VMEM((GQA_HEADS * HEAD_DIM, t), BF16),
        pltpu.VMEM((t, LANES), BF16),
        pltpu.VMEM((GQA_KV_HEADS, HEAD_DIM + ONES_ROWS, t), BF16),
        pltpu.VMEM((PAST_LEN, LANES), BF16),
        pltpu.VMEM((GQA_KV_HEADS, HEAD_DIM + ONES_ROWS, PAST_LEN), BF16),
        pltpu.VMEM((t, NA_HEADS // 2 * LANES), BF16),
        pltpu.VMEM((2, NA_HEADS, LANES, t), BF16),
        pltpu.VMEM((NA_HEADS, t, 2 * LANES), BF16),
        pltpu.VMEM((NA_HEADS // 2, LANES, 2 * PAST_LEN), BF16),
        pltpu.VMEM((NA_HEADS, PAST_LEN, 2 * LANES), BF16),
        pltpu.VMEM((3 * LANES, t), BF16),
        pltpu.VMEM((MLA_HEADS // 2, keys, 2 * LANES), BF16),
        pltpu.VMEM((MLA_HEADS, MLA_V + ONES_ROWS, keys), BF16),
    ]
    return pl.pallas_call(
        functools.partial(_lat_mixer_kernel, layer=layer),
        grid=(n,),
        in_specs=[
            pl.BlockSpec(memory_space=pltpu.SMEM),
            pl.BlockSpec((None, t, D_MODEL), lambda s: (s, 0, 0), pipeline_mode=one),
            _mod_spec(layer, False, 1),
            _layer_spec((1, D_MODEL), layer, 1),
            _layer_spec((D_MODEL, MAIN_COLS), layer, 1),
            _layer_spec((CONV_K, CONV_WIDTH), layer, 1),
            _layer_spec((1, MLA_Q_RANK), layer, 1),
            _layer_spec((MLA_Q_RANK, 3 * LANES), layer, 1),
            _layer_spec((1, MLA_KV_RANK), layer, 1),
            _layer_spec((MLA_KV_RANK, 2 * LANES), layer, 1),
            _layer_spec((2 * LANES, MLA_KV_RANK), layer, 1),
        ] + [const((t, LANES))] * 6 + [
            const((span, GQA_HEADS // GQA_KV_HEADS * BAND_BLOCK)),
            _layer_spec((NA_PATTERNS, GRID_W, NA_HEADS * NA_KEYS), layer, 1),
            cache((2 * HEAD_DIM, PAST_LEN)), cache((2 * HEAD_DIM, PAST_LEN)),
            cache((4 * HEAD_DIM, PAST_LEN)), cache((4 * HEAD_DIM, PAST_LEN)),
            cache((PAST_LEN, MLA_KV_RANK)), cache((MLA_ROPE, PAST_LEN)),
        ],
        out_specs=pl.BlockSpec((None, t, Y_COLS), lambda s: (s, 0, 0)),
        out_shape=jax.ShapeDtypeStruct((n, t, Y_COLS), BF16),
        scratch_shapes=scratch,
        compiler_params=pltpu.CompilerParams(
            dimension_semantics=("arbitrary",), vmem_limit_bytes=VMEM_LIMIT),
        name=name,
    )(sink, x, mods, g_attn, w_main, w_conv, g_q, w_uq, g_kv, w_uk, w_uv_t, *rope_g, *rope_m, _band_mask(), nab, *caches)


def _post_kernel(x_ref, y_ref, mod_ref, ga_ref, gm_ref, gf_ref, wg_ref, wb_ref, wo_ref, w1_ref, w2_ref,
                 o_ref, *, final):
    d = D_MODEL
    mod = lambda i: mod_ref[:, i * d:(i + 1) * d]
    bounds = (Y_CONV, Y_GQA, Y_NA, Y_MLA, Y_COLS)
    for r0 in range(0, x_ref.shape[0], POST_CHAIN_ROWS):
        rows = slice(r0, r0 + POST_CHAIN_ROWS)
        x = x_ref[rows, :]
        h = _modulated_norm(x, ga_ref[...], mod_ref, 0, 1).astype(BF16)
        merged = None
        for i in range(N_BRANCH):
            lo, hi = bounds[i], bounds[i + 1]
            gate = jax.nn.sigmoid(_dot(h, wg_ref[:, i * d:(i + 1) * d]))
            term = gate * _dot(y_ref[rows, lo:hi], wb_ref[lo:hi, :])
            merged = term if merged is None else merged + term
        x = x + mod(2) * _dot(merged.astype(BF16), wo_ref[...])
        h = _modulated_norm(x, gm_ref[...], mod_ref, 3, 4).astype(BF16)
        mlp = None
        for c0 in range(0, D_FF, d):
            f = jnp.square(jnp.maximum(_dot(h, w1_ref[:, c0:c0 + d]), 0.0)).astype(BF16)
            term = _dot(f, w2_ref[c0:c0 + d, :])
            mlp = term if mlp is None else mlp + term
        x = x + mod(5) * mlp
        o_ref[rows, :] = _rms(x, gf_ref[...]) if final else x


def _post_call(x, y, mods, g_attn, g_mlp, g_final, w_gates, w_br, w_o, w_ff1, w_ff2, layer, ctx, name):
    n, t, _ = x.shape
    tm = 2 * POST_CHAIN_ROWS
    return pl.pallas_call(
        functools.partial(_post_kernel, final=layer == DEPTH - 1),
        grid=(n, t // tm),
        in_specs=[
            pl.BlockSpec((None, tm, D_MODEL), lambda s, i: (s, i, 0)),
            pl.BlockSpec((None, tm, Y_COLS), lambda s, i: (s, i, 0)),
            _mod_spec(layer, ctx, 2),
            _layer_spec((1, D_MODEL), layer, 2),
            _layer_spec((1, D_MODEL), layer, 2),
            pl.BlockSpec((1, D_MODEL), lambda s, i: (0, 0)),
            _layer_spec((D_MODEL, N_BRANCH * D_MODEL), layer, 2),
            _layer_spec((Y_COLS, D_MODEL), layer, 2),
            _layer_spec((D_MODEL, D_MODEL), layer, 2),
            _layer_spec((D_MODEL, D_FF), layer, 2),
            _layer_spec((D_FF, D_MODEL), layer, 2),
        ],
        out_specs=pl.BlockSpec((None, tm, D_MODEL), lambda s, i: (s, i, 0)),
        out_shape=jax.ShapeDtypeStruct((n, t, D_MODEL), F32),
        compiler_params=pltpu.CompilerParams(
            dimension_semantics=("arbitrary", "arbitrary"), vmem_limit_bytes=VMEM_LIMIT),
        name=name,
    )(x, y, mods, g_attn, g_mlp, g_final, w_gates, w_br, w_o, w_ff1, w_ff2)


def kernel(x_prompt, x_sample, cache_gqa_k, cache_gqa_v, cache_na_k, cache_na_v, cache_mla_ckv, cache_mla_krope, c, c_ctx, w_mod, b_mod, g_attn, g_mlp, w_in, w_conv, gqa_sink, na_rpb, mla_g_q, mla_w_uq, mla_g_kv, mla_w_ukv, w_branch_conv, w_branch_gqa, w_branch_na, w_branch_mla, w_o, w_ff1, w_ff2, g_final):
    w_main = w_in[:, :, :MAIN_COLS].astype(BF16)
    w_gates = w_in[:, :, GATE_COL0:].astype(BF16)
    uq = mla_w_uq.reshape(DEPTH, MLA_Q_RANK, MLA_HEADS, MLA_NOPE + MLA_ROPE)
    w_uq = jnp.concatenate([uq[..., :MLA_NOPE].reshape(DEPTH, MLA_Q_RANK, -1),
                            uq[..., MLA_NOPE:].reshape(DEPTH, MLA_Q_RANK, -1)], axis=-1).astype(BF16)
    ukv = mla_w_ukv.reshape(DEPTH, MLA_KV_RANK, MLA_HEADS, MLA_NOPE + MLA_V)
    w_uk = ukv[..., :MLA_NOPE].reshape(DEPTH, MLA_KV_RANK, -1).astype(BF16)
    w_uv = ukv[..., MLA_NOPE:].reshape(DEPTH, MLA_KV_RANK, -1).astype(BF16)
    w_uk_t, w_uv_t = jnp.transpose(w_uk, (0, 2, 1)), jnp.transpose(w_uv, (0, 2, 1))
    w_br = jnp.concatenate([w_branch_conv, w_branch_gqa, w_branch_na, w_branch_mla], axis=1).astype(BF16)
    w_o_b, w_ff1_b, w_ff2_b = w_o.astype(BF16), w_ff1.astype(BF16), w_ff2.astype(BF16)
    g_a, g_m = g_attn[:, None, :], g_mlp[:, None, :]
    g_q, g_kv, g_f = mla_g_q[:, None, :], mla_g_kv[:, None, :], g_final[None, :]

    c16 = jnp.concatenate([c, c_ctx[None, :], jnp.zeros((MOD_ROWS - DEC_BATCH - 1, D_MODEL), F32)], axis=0)
    mods = _mod_call(c16, w_mod, b_mod).reshape(DEPTH, MOD_ROWS, 1, 6 * D_MODEL)
    nab = _nab_call(na_rpb)

    heads_t = lambda a: jnp.transpose(a, (0, 1, 3, 4, 2)).reshape(a.shape[0], DEPTH, -1, a.shape[2])
    caches = (heads_t(cache_gqa_k), heads_t(cache_gqa_v), heads_t(cache_na_k), heads_t(cache_na_v),
              cache_mla_ckv, jnp.transpose(cache_mla_krope, (0, 1, 3, 2)))

    h_ctx, h_lat = x_prompt, x_sample
    states = None
    for l in range(DEPTH):
        mixer_w = (gqa_sink, w_conv, g_q, w_uq, g_kv)
        post_w = (g_a, g_m, g_f, w_gates, w_br, w_o_b, w_ff1_b, w_ff2_b)

        y, states = _ctx_mixer_call(h_ctx, mods, g_a, w_main, *mixer_w, w_uk_t, w_uv, states, l, f"mixer_ctx_{l}")
        flat = lambda a: a.reshape(1, BATCH * SEQ, a.shape[-1])
        h_ctx = _post_call(flat(h_ctx), flat(y), mods, *post_w, l, True, f"post_ctx_{l}").reshape(BATCH, SEQ, D_MODEL)

        y = _lat_mixer_call(h_lat, mods, g_a, w_main, *mixer_w, w_uk, w_uv_t, nab, caches, l, f"mixer_lat_{l}")
        h_lat = _post_call(h_lat, y, mods, *post_w, l, False, f"post_lat_{l}")

    def heads_out(a, heads):
        return jnp.transpose(a.reshape(BATCH, DEPTH, heads, HEAD_DIM, SEQ), (0, 1, 4, 2, 3))

    kg, vg, kn, vn, ckv, kr = states
    return (h_ctx, h_lat, heads_out(kg, GQA_KV_HEADS), heads_out(vg, GQA_KV_HEADS),
            heads_out(kn, NA_HEADS), heads_out(vn, NA_HEADS), ckv, jnp.transpose(kr, (0, 1, 3, 2)))
```

```python
import functools
import math

import numpy as np
import jax
import jax.numpy as jnp
from jax import lax
from jax.experimental import pallas as pl
from jax.experimental.pallas import tpu as pltpu

D_MODEL = 1024
BATCH = 32
SEQ = 256
DEPTH = 2
DEC_BATCH = 8
DEC_SEQ = 1024
PAST_LEN = 512
GRID_W = 64
GRID_ROWS = DEC_SEQ // GRID_W
HEAD_DIM = 64
CONV_WIDTH = 256
CONV_K = 3
GQA_HEADS = 8
GQA_KV_HEADS = 2
WINDOW = 128
BAND_BLOCK = 128
NA_HEADS = 4
NA_WIN_H = 8
NA_WIN_W = 16
MLA_HEADS = 4
MLA_Q_RANK = 256
MLA_KV_RANK = 128
MLA_NOPE = 64
MLA_ROPE = 32
MLA_V = 64
D_FF = 4 * D_MODEL
N_BRANCH = 4
ROPE_BASE = 10000.0
EPS = 1e-6
NEG_INF = -1e30
LOG2E = math.log2(math.e)
ATTN_SCALE = HEAD_DIM ** -0.5
MLA_SCALE = (MLA_NOPE + MLA_ROPE) ** -0.5

LANES = 128
MOD_ROWS = 16
CTX_MOD_ROW = DEC_BATCH

C_CB, C_CC, C_CV = 0, 256, 512
C_GQ, C_GK, C_GV = 768, 1280, 1408
C_NQ, C_NK, C_NV = 1536, 1792, 2048
C_MQ, C_MKV, C_MKR = 2304, 2560, 2688
MAIN_COLS = 2816
GATE_COL0 = 2720
Y_CONV, Y_GQA, Y_NA, Y_MLA = 0, 256, 768, 1024
Y_COLS = 1280
NA_GROUPS = 4
NA_GROUP_ROWS = GRID_ROWS // NA_GROUPS

VMEM_LIMIT = 56 * 1024 * 1024
POST_CHAIN_ROWS = 256
CTX_SEQS = 2

F32 = jnp.float32
BF16 = jnp.bfloat16


def _dot(a, b):
    return jnp.dot(a, b, preferred_element_type=F32)


def _dot_nt(a, b):
    return lax.dot_general(a, b, (((1,), (1,)), ((), ())), preferred_element_type=F32)


def _rms(x, g):
    return x * lax.rsqrt(jnp.mean(x * x, axis=-1, keepdims=True) + EPS) * g


def _lane_lt(shape, n):
    return lax.broadcasted_iota(jnp.int32, shape, len(shape) - 1) < n


def _row_lt(shape, n):
    return lax.broadcasted_iota(jnp.int32, shape, 0) < n


def _row_group(shape, lo, hi):
    row = lax.broadcasted_iota(jnp.int32, shape, 0)
    return (row >= lo) & (row < hi)


def _short_conv(cb, cc, cv, w):
    u = cc * cv
    t = u.shape[0]
    row = lax.broadcasted_iota(jnp.int32, u.shape, 0)
    prev = jnp.where(row == 0, 0.0, pltpu.roll(u, 1, 0))
    nxt = jnp.where(row == t - 1, 0.0, pltpu.roll(u, t - 1, 0))
    return cb * (prev * w[0:1, :] + u * w[1:2, :] + nxt * w[2:3, :])


def _rope(x, cos, sin_hi, sin_lo, half):
    n = x.shape[-1]
    return x * cos + pltpu.roll(x, n - half, 1) * sin_lo + pltpu.roll(x, half, 1) * sin_hi


def _repeat_rope_key(tile):
    k = jnp.where(_lane_lt(tile.shape, MLA_ROPE), tile, 0.0)
    k = k + pltpu.roll(k, MLA_ROPE, 1)
    return k + pltpu.roll(k, 2 * MLA_ROPE, 1)


def _mod_kernel(c_ref, w_ref, b_ref, o_ref):
    c = c_ref[...]
    s = c * jax.nn.sigmoid(c)
    o_ref[...] = _dot(s.astype(BF16), w_ref[...].astype(BF16)) + b_ref[...]


def _mod_call(c16, w_mod, b_mod):
    tn = 1536
    return pl.pallas_call(
        _mod_kernel,
        grid=(DEPTH, 6 * D_MODEL // tn),
        in_specs=[
            pl.BlockSpec((MOD_ROWS, D_MODEL), lambda l, j: (0, 0)),
            pl.BlockSpec((None, D_MODEL, tn), lambda l, j: (l, 0, j)),
            pl.BlockSpec((None, 1, tn), lambda l, j: (l, 0, j)),
        ],
        out_specs=pl.BlockSpec((None, MOD_ROWS, tn), lambda l, j: (l, 0, j)),
        out_shape=jax.ShapeDtypeStruct((DEPTH, MOD_ROWS, 6 * D_MODEL), F32),
        compiler_params=pltpu.CompilerParams(
            dimension_semantics=("arbitrary", "arbitrary"), vmem_limit_bytes=VMEM_LIMIT),
        name="adaln_mod",
    )(c16, w_mod, b_mod.reshape(DEPTH, 1, 6 * D_MODEL))


def _na_window_start(r):
    return min(max(r - NA_WIN_H // 2, 0), GRID_ROWS - NA_WIN_H)


def _na_group_slab(k):
    starts = [_na_window_start(r) for r in range(k * NA_GROUP_ROWS, (k + 1) * NA_GROUP_ROWS)]
    lo = min(starts) // 2 * 2
    n = -(-(max(starts) + NA_WIN_H - lo) // 4) * 4
    assert lo + n <= GRID_ROWS
    return lo, n


def _na_group_kinds():
    kinds, of_group = [], []
    for k in range(NA_GROUPS):
        lo, n = _na_group_slab(k)
        layout = {}
        for dr in range(NA_GROUP_ROWS):
            r = k * NA_GROUP_ROWS + dr
            r0 = _na_window_start(r)
            for a in range(n):
                inside = r0 <= lo + a < r0 + NA_WIN_H
                layout[(dr, a)] = lo + a - r + NA_WIN_H - 1 if inside else None
        if (n, layout) not in kinds:
            kinds.append((n, layout))
        of_group.append(kinds.index((n, layout)))
    return kinds, of_group


def _na_table_width():
    return sum(n for n, _ in _na_group_kinds()[0]) * GRID_W


def _nab_kernel(rpb_ref, o_ref):
    l = pl.program_id(0)
    c = lax.broadcasted_iota(jnp.int32, (GRID_W, GRID_W), 0)
    w = lax.broadcasted_iota(jnp.int32, (GRID_W, GRID_W), 1)
    dc = w - c + (NA_WIN_W - 1)
    c0 = jnp.clip(c - NA_WIN_W // 2, 0, GRID_W - NA_WIN_W)
    outside = (w < c0) | (w >= c0 + NA_WIN_W)
    n_dr, n_dc = 2 * NA_WIN_H - 1, 2 * NA_WIN_W - 1
    kinds, _ = _na_group_kinds()
    width = _na_table_width()

    def put(h, d, tile):
        col0 = h * width
        for n, layout in kinds:
            for (dr, a), want in layout.items():
                if want == d:
                    o_ref[dr * GRID_W:(dr + 1) * GRID_W, col0 + a * GRID_W:col0 + (a + 1) * GRID_W] = tile
            col0 += n * GRID_W

    for h in range(NA_HEADS):
        put(h, None, jnp.full((GRID_W, GRID_W), NEG_INF, F32))
        for d in range(n_dr):
            base = ((l * NA_HEADS + h) * n_dr + d) * n_dc
            tile = jnp.full((GRID_W, GRID_W), NEG_INF, F32)
            for j in range(n_dc):
                tile = jnp.where(dc == j, rpb_ref[base + j] * LOG2E, tile)
            put(h, d, jnp.where(outside, NEG_INF, tile))


def _nab_call(na_rpb):
    shape = (NA_GROUP_ROWS * GRID_W, NA_HEADS * _na_table_width())
    return pl.pallas_call(
        _nab_kernel,
        grid=(DEPTH,),
        in_specs=[pl.BlockSpec(memory_space=pltpu.SMEM)],
        out_specs=pl.BlockSpec((None,) + shape, lambda l: (l, 0, 0)),
        out_shape=jax.ShapeDtypeStruct((DEPTH,) + shape, F32),
        compiler_params=pltpu.CompilerParams(dimension_semantics=("arbitrary",), vmem_limit_bytes=VMEM_LIMIT),
        name="na_bias_tables",
    )(na_rpb.reshape(-1))


def _mod_spec(layer, ctx, n_grid):
    if n_grid == 1:
        index = (lambda s: (layer, CTX_MOD_ROW, 0, 0)) if ctx else (lambda s: (layer, s, 0, 0))
    else:
        index = (lambda s, i: (layer, CTX_MOD_ROW, 0, 0)) if ctx else (lambda s, i: (layer, s, 0, 0))
    return pl.BlockSpec((None, None, 1, 6 * D_MODEL), index)


def _layer_spec(shape, layer, n_grid):
    zeros = (0,) * len(shape)
    index = (lambda s: (layer,) + zeros) if n_grid == 1 else (lambda s, i: (layer,) + zeros)
    return pl.BlockSpec((None,) + tuple(shape), index, pipeline_mode=pl.Buffered(1))


_COLUMN_GROUPS = ((C_CB, C_GQ), (C_GQ, C_NQ), (C_NQ, C_MQ), (C_MQ, MAIN_COLS))


def _modulated_norm(x, g, mod_ref, shift, scale):
    d = D_MODEL
    return _rms(x, g) * (1.0 + mod_ref[:, scale * d:(scale + 1) * d]) + mod_ref[:, shift * d:(shift + 1) * d]


def _projected_columns(h_ref, w_ref):
    cache = {}

    def col(c, n):
        lo, hi = next(g for g in _COLUMN_GROUPS if g[0] <= c and c + n <= g[1])
        if lo not in cache:
            cache.clear()
            cache[lo] = _dot(h_ref[...], w_ref[:, lo:hi])
        return cache[lo][:, c - lo:c - lo + n]

    return col


def _key_planes(k_t, even_first, odd_first):
    lo = _row_lt(k_t.shape, HEAD_DIM)
    swapped = None
    if not even_first or odd_first:
        swapped = pltpu.roll(k_t, HEAD_DIM, 0)
    top = jnp.where(lo, k_t if even_first else swapped, 0.0)
    bot = jnp.where(lo, 0.0, swapped if odd_first else k_t)
    return top.astype(BF16), bot.astype(BF16)


def _value_planes(pair, even_first, odd_first):
    lo = _lane_lt(pair.shape, HEAD_DIM)
    swapped = None
    if not even_first or odd_first:
        swapped = pltpu.roll(pair, HEAD_DIM, 1)
    top = jnp.where(lo, pair if even_first else swapped, 0.0)
    bot = jnp.where(lo, 0.0, swapped if odd_first else pair)
    one_e = jnp.where(lo, 1.0, 0.0)
    return (jnp.concatenate([top, one_e], axis=1).astype(BF16),
            jnp.concatenate([bot, 1.0 - one_e], axis=1).astype(BF16))


def _probabilities(parts, sink=None):
    m = parts[0].max(axis=-1, keepdims=True)
    for p in parts[1:]:
        m = jnp.maximum(m, p.max(axis=-1, keepdims=True))
    if sink is not None:
        m = jnp.maximum(m, sink)
    probs = [jnp.exp2((p - m).astype(BF16)) for p in parts]
    return probs, (None if sink is None else jnp.exp2(sink - m))


def _attend(terms, sink_e=None, sink_o=None):
    o = None
    for p, v in terms:
        t = _dot(p, v)
        o = t if o is None else o + t
    den = o[:, LANES:]
    if sink_e is not None:
        den = den + jnp.where(_lane_lt(den.shape, HEAD_DIM), sink_e, sink_o)
    return o[:, :LANES] / den


ONES_ROWS = 16


def _value_plane_t(v_t):
    return jnp.concatenate([v_t, jnp.ones((ONES_ROWS, v_t.shape[1]), v_t.dtype)], axis=0).astype(BF16)


def _probabilities_t(parts, sink=None):
    m = parts[0].max(axis=0, keepdims=True)
    for p in parts[1:]:
        m = jnp.maximum(m, p.max(axis=0, keepdims=True))
    if sink is not None:
        m = jnp.maximum(m, sink)
    probs = [jnp.exp2((p - m).astype(BF16)) for p in parts]
    return probs, (None if sink is None else jnp.exp2(sink - m))


def _attend_t(terms, sink_num=None):
    o = None
    for v_t, p in terms:
        t = _dot(v_t, p)
        o = t if o is None else o + t
    den = o[HEAD_DIM:HEAD_DIM + 1, :]
    if sink_num is not None:
        den = den + sink_num
    return o[0:HEAD_DIM, :] / den


def _mla_key_plane(kn_t_pair, kr4_t, j, odd):
    h = 2 * j + odd
    lo = _row_lt(kn_t_pair.shape, HEAD_DIM)
    nope = jnp.where(lo, 0.0, kn_t_pair) if odd else jnp.where(lo, kn_t_pair, 0.0)
    rope = jnp.where(_row_group(kr4_t.shape, h * MLA_ROPE, (h + 1) * MLA_ROPE), kr4_t, 0.0)
    return jnp.concatenate([nope, rope], axis=0).astype(BF16)


def _ctx_mixer_kernel(sink_ref, x_ref, mod_ref, ga_ref, win_ref, wc_ref, gq_ref, wuq_ref, gkv_ref, wukt_ref, wuv_ref,
                      *rest, layer, first):
    y_ref, kg_ref, vg_ref, kn_ref, vn_ref, ckv_ref, kr_ref, h_s = rest[-8:]
    t = SEQ
    for b in range(CTX_SEQS):
        h_s[b * t:(b + 1) * t, :] = _modulated_norm(x_ref[b], ga_ref[...], mod_ref, 0, 1).astype(BF16)
    col_all = _projected_columns(h_s, win_ref)

    def put_state(ref, b, val):
        if first:
            for d in range(DEPTH):
                ref[b, d] = val if d == layer else jnp.zeros_like(val)
        else:
            ref[b] = val

    def conv(b, col):
        y_ref[b, :, Y_CONV:Y_CONV + CONV_WIDTH] = _short_conv(
            col(C_CB, CONV_WIDTH), col(C_CC, CONV_WIDTH), col(C_CV, CONV_WIDTH), wc_ref[...]).astype(BF16)

    def gqa(b, col):
        kg_t, vpair = col(C_GK, LANES).T, col(C_GV, LANES)
        put_state(kg_ref, b, kg_t)
        put_state(vg_ref, b, vpair.T)
        row2 = lax.broadcasted_iota(jnp.int32, (2 * t, 1), 0) < t
        for g in range(GQA_KV_HEADS):
            ke, ko = _key_planes(kg_t, g == 0, g == 0)
            ve, vo = _value_planes(vpair, g == 0, g == 0)
            q = jnp.concatenate([col(C_GQ + (2 * g) * LANES, LANES), col(C_GQ + (2 * g + 1) * LANES, LANES)], axis=0)
            q = (q * (ATTN_SCALE * LOG2E)).astype(BF16)
            s = _dot(q, jnp.concatenate([ke, ko], axis=1))
            sink = [jnp.where(row2, sink_ref[layer, 4 * g + odd], sink_ref[layer, 4 * g + 2 + odd]) * LOG2E
                    for odd in range(2)]
            (pe,), xe = _probabilities([s[:, 0:t]], sink[0])
            (po,), xo = _probabilities([s[:, t:2 * t]], sink[1])
            o = _attend([(pe, ve), (po, vo)], xe, xo).astype(BF16)
            c0 = Y_GQA + (2 * g) * LANES
            y_ref[b, :, c0:c0 + LANES] = o[0:t]
            y_ref[b, :, c0 + LANES:c0 + 2 * LANES] = o[t:2 * t]

    def na(b, col):
        kn_t = col(C_NK, 2 * LANES).T
        put_state(kn_ref, b, kn_t)
        put_state(vn_ref, b, col(C_NV, 2 * LANES).T)
        for j in range(NA_HEADS // 2):
            ke, ko = _key_planes(kn_t[j * LANES:(j + 1) * LANES, :], True, False)
            ve, vo = _value_planes(col(C_NV + j * LANES, LANES), True, False)
            q = (col(C_NQ + j * LANES, LANES) * (ATTN_SCALE * LOG2E)).astype(BF16)
            s = _dot(q, jnp.concatenate([ke, ko], axis=1))
            (pe,), _ = _probabilities([s[:, 0:t]])
            (po,), _ = _probabilities([s[:, t:2 * t]])
            y_ref[b, :, Y_NA + j * LANES:Y_NA + (j + 1) * LANES] = _attend([(pe, ve), (po, vo)]).astype(BF16)

    def mla(b, col):
        ckv = _rms(col(C_MKV, MLA_KV_RANK), gkv_ref[...])
        put_state(ckv_ref, b, ckv)
        kr_t = col(C_MKR, LANES).T[0:MLA_ROPE, :]
        put_state(kr_ref, b, kr_t)
        ckv_b = ckv.astype(BF16)
        q = _dot(_rms(col(C_MQ, MLA_Q_RANK), gq_ref[...]).astype(BF16), wuq_ref[...]) * (MLA_SCALE * LOG2E)
        kn_t_all = _dot_nt(wukt_ref[...], ckv_b)
        v_all = _dot(ckv_b, wuv_ref[...])
        kr4_t = jnp.concatenate([kr_t] * MLA_HEADS, axis=0)
        q_rope = q[:, 2 * LANES:3 * LANES]
        for j in range(MLA_HEADS // 2):
            qj = jnp.concatenate([q[:, j * LANES:(j + 1) * LANES], q_rope], axis=1).astype(BF16)
            kn_t_pair = kn_t_all[j * LANES:(j + 1) * LANES, :]
            keys = jnp.concatenate(
                [_mla_key_plane(kn_t_pair, kr4_t, j, 0), _mla_key_plane(kn_t_pair, kr4_t, j, 1)], axis=1)
            s = _dot(qj, keys)
            ve, vo = _value_planes(v_all[:, j * LANES:(j + 1) * LANES], True, False)
            (pe,), _ = _probabilities([s[:, 0:t]])
            (po,), _ = _probabilities([s[:, t:2 * t]])
            y_ref[b, :, Y_MLA + j * LANES:Y_MLA + (j + 1) * LANES] = _attend([(pe, ve), (po, vo)]).astype(BF16)

    for mixer in (conv, gqa, na, mla):
        for b in range(CTX_SEQS):
            mixer(b, lambda c, n, b=b: col_all(c, n)[b * t:(b + 1) * t])


def _ctx_mixer_call(x, mods, g_attn, w_main, sink, w_conv, g_q, w_uq, g_kv, w_uk_t, w_uv, prev_states, layer, name):
    n, nb = BATCH, CTX_SEQS
    state_shapes = ((2 * HEAD_DIM, SEQ), (2 * HEAD_DIM, SEQ), (4 * HEAD_DIM, SEQ), (4 * HEAD_DIM, SEQ),
                    (SEQ, MLA_KV_RANK), (MLA_ROPE, SEQ))
    first = prev_states is None
    if first:
        state_spec = lambda shape: pl.BlockSpec((nb, DEPTH) + shape, lambda s: (s, 0, 0, 0))
    else:
        state_spec = lambda shape: pl.BlockSpec((nb, None) + shape, lambda s: (s, layer, 0, 0))
    in_specs = [
        pl.BlockSpec(memory_space=pltpu.SMEM),
        pl.BlockSpec((nb, SEQ, D_MODEL), lambda s: (s, 0, 0)),
        _mod_spec(layer, True, 1),
        _layer_spec((1, D_MODEL), layer, 1),
        _layer_spec((D_MODEL, MAIN_COLS), layer, 1),
        _layer_spec((CONV_K, CONV_WIDTH), layer, 1),
        _layer_spec((1, MLA_Q_RANK), layer, 1),
        _layer_spec((MLA_Q_RANK, 3 * LANES), layer, 1),
        _layer_spec((1, MLA_KV_RANK), layer, 1),
        _layer_spec((2 * LANES, MLA_KV_RANK), layer, 1),
        _layer_spec((MLA_KV_RANK, 2 * LANES), layer, 1),
    ]
    args = [sink, x, mods, g_attn, w_main, w_conv, g_q, w_uq, g_kv, w_uk_t, w_uv]
    aliases = {}
    if not first:
        for i, st in enumerate(prev_states):
            aliases[len(args)] = 1 + i
            in_specs.append(pl.BlockSpec(memory_space=pl.ANY))
            args.append(st)
    outs = pl.pallas_call(
        functools.partial(_ctx_mixer_kernel, layer=layer, first=first),
        grid=(n // nb,),
        in_specs=in_specs,
        out_specs=[pl.BlockSpec((nb, SEQ, Y_COLS), lambda s: (s, 0, 0))] + [state_spec(s) for s in state_shapes],
        out_shape=[jax.ShapeDtypeStruct((n, SEQ, Y_COLS), BF16)]
        + [jax.ShapeDtypeStruct((n, DEPTH) + s, F32) for s in state_shapes],
        input_output_aliases=aliases,
        scratch_shapes=[pltpu.VMEM((nb * SEQ, D_MODEL), BF16)],
        compiler_params=pltpu.CompilerParams(
            dimension_semantics=("arbitrary",), vmem_limit_bytes=VMEM_LIMIT),
        name=name,
    )(*args)
    return outs[0], outs[1:]


def _lat_mixer_kernel(sink_ref, x_ref, mod_ref, ga_ref, win_ref, wc_ref, gq_ref, wuq_ref, gkv_ref, wuk_ref, wuvt_ref,
                      rc_ref, rsh_ref, rsl_ref, mc_ref, msh_ref, msl_ref, band_ref, nab_ref,
                      cgk_ref, cgv_ref, cnk_ref, cnv_ref, cckv_ref, ckr_ref,
                      y_ref,
                      h_s, gq_s, gk_s, gv_s, gkc_s, gvc_s, nq_s, nk_s, nv_s, nkc_s, nvc_s, mq_s, mk_s, mv_s, *, layer):
    t = DEC_SEQ
    h_s[...] = _modulated_norm(x_ref[...], ga_ref[...], mod_ref, 0, 1).astype(BF16)
    col = _projected_columns(h_s, win_ref)
    rope64 = lambda x: _rope(x, rc_ref[...], rsh_ref[...], rsl_ref[...], HEAD_DIM // 4)
    rope32 = lambda x: _rope(x, mc_ref[...], msh_ref[...], msl_ref[...], MLA_ROPE // 4)

    y_ref[:, Y_CONV:Y_CONV + CONV_WIDTH] = _short_conv(
        col(C_CB, CONV_WIDTH), col(C_CC, CONV_WIDTH), col(C_CV, CONV_WIDTH), wc_ref[...]).astype(BF16)

    group = GQA_HEADS // GQA_KV_HEADS
    for j in range(GQA_HEADS // 2):
        q_pair = rope64(col(C_GQ + j * LANES, LANES)) * (ATTN_SCALE * LOG2E)
        gq_s[j * LANES:(j + 1) * LANES, :] = q_pair.T.astype(BF16)
    gk_s[...] = rope64(col(C_GK, LANES)).astype(BF16)
    gkc_s[...] = cgk_ref[...].T.astype(BF16)
    v_t = col(C_GV, LANES).T
    for g in range(GQA_KV_HEADS):
        gv_s[g] = _value_plane_t(v_t[g * HEAD_DIM:(g + 1) * HEAD_DIM, :])
        gvc_s[g] = _value_plane_t(cgv_ref[g * HEAD_DIM:(g + 1) * HEAD_DIM, :])

    def gqa_block(b, c_lo, c_hi):
        q0 = pl.multiple_of(b * BAND_BLOCK, BAND_BLOCK)
        k0 = pl.multiple_of(q0 + (c_lo - WINDOW), BAND_BLOCK)
        n = c_hi - c_lo
        band = band_ref[c_lo:c_hi, :]
        lane_head = lax.shift_right_logical(
            lax.broadcasted_iota(jnp.int32, (1, group * BAND_BLOCK), 1), BAND_BLOCK.bit_length() - 1)
        zeros = jnp.zeros((HEAD_DIM, group * BAND_BLOCK), BF16)
        for g in range(GQA_KV_HEADS):
            q_t = jnp.concatenate(
                [gq_s[(group * g + h) * HEAD_DIM:(group * g + h + 1) * HEAD_DIM, pl.ds(q0, BAND_BLOCK)]
                 for h in range(group)], axis=1)
            q_t = jnp.concatenate([q_t, zeros] if g == 0 else [zeros, q_t], axis=0)
            s_loc = _dot(gk_s[pl.ds(k0, n), :], q_t) + band
            s_ctx = _dot(gkc_s[...], q_t)
            sink = sink_ref[layer, group * g + group - 1]
            for h in range(group - 2, -1, -1):
                sink = jnp.where(lane_head == h, sink_ref[layer, group * g + h], sink)
            (p_loc, p_ctx), x = _probabilities_t([s_loc, s_ctx], sink * LOG2E)
            o = _attend_t([(gv_s[g, :, pl.ds(k0, n)], p_loc), (gvc_s[g], p_ctx)], x)
            for pr in range(group // 2):
                pair = jnp.concatenate([o[:, (2 * pr) * BAND_BLOCK:(2 * pr + 1) * BAND_BLOCK],
                                        o[:, (2 * pr + 1) * BAND_BLOCK:(2 * pr + 2) * BAND_BLOCK]], axis=0)
                c0 = Y_GQA + (group // 2 * g + pr) * LANES
                y_ref[pl.ds(q0, BAND_BLOCK), c0:c0 + LANES] = pair.T.astype(BF16)

    for j in range(NA_HEADS // 2):
        nq_s[:, j * LANES:(j + 1) * LANES] = (col(C_NQ + j * LANES, LANES) * (ATTN_SCALE * LOG2E)).astype(BF16)
        nk_s[2 * j], nk_s[2 * j + 1] = _key_planes(col(C_NK + j * LANES, LANES).T, True, False)
        nv_s[2 * j], nv_s[2 * j + 1] = _value_planes(col(C_NV + j * LANES, LANES), True, False)
        nkc_s[j] = jnp.concatenate(_key_planes(cnk_ref[j * LANES:(j + 1) * LANES, :], True, False), axis=1)
        nvc_s[2 * j], nvc_s[2 * j + 1] = _value_planes(cnv_ref[j * LANES:(j + 1) * LANES, :].T, True, False)

    na_kinds, na_kind_of = _na_group_kinds()
    na_width = _na_table_width()

    def na_group(k):
        lo, n = _na_group_slab(k)
        q0, nq = k * NA_GROUP_ROWS * GRID_W, NA_GROUP_ROWS * GRID_W
        k0, nk = lo * GRID_W, n * GRID_W
        b0 = sum(kn for kn, _ in na_kinds[:na_kind_of[k]]) * GRID_W
        for j in range(NA_HEADS // 2):
            q = nq_s[q0:q0 + nq, j * LANES:(j + 1) * LANES]
            s_ctx = _dot(q, nkc_s[j])
            terms = []
            for odd in range(2):
                h = 2 * j + odd
                s_loc = _dot(q, nk_s[h, :, k0:k0 + nk]) + nab_ref[:, h * na_width + b0:h * na_width + b0 + nk]
                (p_loc, p_ctx), _ = _probabilities([s_loc, s_ctx[:, odd * PAST_LEN:(odd + 1) * PAST_LEN]])
                terms += [(p_loc, nv_s[h, k0:k0 + nk, :]), (p_ctx, nvc_s[h])]
            y_ref[q0:q0 + nq, Y_NA + j * LANES:Y_NA + (j + 1) * LANES] = _attend(terms).astype(BF16)

    ckv_b = _rms(col(C_MKV, MLA_KV_RANK), gkv_ref[...]).astype(BF16)
    cckv_b = cckv_ref[...].astype(BF16)
    q = _dot(_rms(col(C_MQ, MLA_Q_RANK), gq_ref[...]).astype(BF16), wuq_ref[...]) * (MLA_SCALE * LOG2E)
    for i in range(3):
        tile = q[:, i * LANES:(i + 1) * LANES]
        mq_s[i * LANES:(i + 1) * LANES, :] = (rope32(tile) if i == 2 else tile).T.astype(BF16)
    kr = jnp.where(_lane_lt((t, LANES), MLA_ROPE), rope32(col(C_MKR, LANES)), 0.0)
    kr_c = jnp.concatenate([ckr_ref[...], jnp.zeros((LANES - MLA_ROPE, PAST_LEN), F32)], axis=0).T
    for rows, ckv_x, kr_x in ((slice(0, t), ckv_b, kr), (slice(t, t + PAST_LEN), cckv_b, kr_c)):
        kn_all = _dot(ckv_x, wuk_ref[...])
        v_t_all = _dot_nt(wuvt_ref[...], ckv_x)
        for j in range(MLA_HEADS // 2):
            mk_s[j, rows, :] = jnp.concatenate([kn_all[:, j * LANES:(j + 1) * LANES], kr_x], axis=1).astype(BF16)
        for h in range(MLA_HEADS):
            mv_s[h, :, rows] = _value_plane_t(v_t_all[h * MLA_V:(h + 1) * MLA_V, :])

    tq = 256

    def mla_block(i, carry):
        q0 = pl.multiple_of(i * tq, tq)
        zeros = lambda r: jnp.zeros((r, tq), BF16)
        q_nope = lambda h: mq_s[h * MLA_NOPE:(h + 1) * MLA_NOPE, pl.ds(q0, tq)]
        q_rope = lambda h: mq_s[MLA_HEADS * MLA_NOPE + h * MLA_ROPE:MLA_HEADS * MLA_NOPE + (h + 1) * MLA_ROPE,
                                pl.ds(q0, tq)]
        pad = 2 * LANES - 2 * MLA_NOPE - MLA_ROPE
        for j in range(MLA_HEADS // 2):
            he, ho = 2 * j, 2 * j + 1
            q_e = jnp.concatenate([q_nope(he), zeros(MLA_NOPE), q_rope(he), zeros(pad)], axis=0)
            q_o = jnp.concatenate([zeros(MLA_NOPE), q_nope(ho), q_rope(ho), zeros(pad)], axis=0)
            s = _dot(mk_s[j], jnp.concatenate([q_e, q_o], axis=1))
            outs = []
            for odd in range(2):
                (p,), _ = _probabilities_t([s[:, odd * tq:(odd + 1) * tq]])
                outs.append(_attend_t([(mv_s[2 * j + odd], p)]))
            y_ref[pl.ds(q0, tq), Y_MLA + j * LANES:Y_MLA + (j + 1) * LANES] = (
                jnp.concatenate(outs, axis=0).T.astype(BF16))
        return carry

    span = BAND_BLOCK + 2 * WINDOW
    nb = t // BAND_BLOCK
    for i in range(nb):
        gqa_block(jnp.int32(i), WINDOW if i == 0 else 0, span - WINDOW if i == nb - 1 else span)
        if i % (nb // NA_GROUPS) == 0:
            na_group(i // (nb // NA_GROUPS))
        if i % (nb * tq // t) == 0:
            mla_block(jnp.int32(i // (nb * tq // t)), 0)


def _rope_tables(group, half):
    tok = np.arange(DEC_SEQ)
    pos = np.stack([tok // GRID_W, tok % GRID_W], axis=1).astype(np.float64)
    inv = ROPE_BASE ** (-np.arange(half, dtype=np.float64) / half)
    lane = np.arange(LANES) % group
    axis = lane // (2 * half)
    within = lane % (2 * half)
    ang = pos[:, axis] * inv[within % half][None, :]
    cos, sin = np.cos(ang), np.sin(ang)
    upper = (within >= half)[None, :]
    sin_hi = np.where(upper, sin, 0.0)
    sin_lo = np.where(upper, 0.0, -sin)
    return tuple(jnp.asarray(a, dtype=F32) for a in (cos, sin_hi, sin_lo))


def _band_mask():
    c = np.arange(BAND_BLOCK + 2 * WINDOW)[:, None]
    i = np.arange(BAND_BLOCK)[None, :]
    ok = (c >= i) & (c <= i + 2 * WINDOW)
    m = np.where(ok, 0.0, NEG_INF)
    return jnp.asarray(np.concatenate([m] * (GQA_HEADS // GQA_KV_HEADS), axis=1), dtype=F32)


def _lat_mixer_call(x, mods, g_attn, w_main, sink, w_conv, g_q, w_uq, g_kv, w_uk, w_uv_t, nab, caches, layer, name):
    n, t = DEC_BATCH, DEC_SEQ
    one = pl.Buffered(1)
    const = lambda shape: pl.BlockSpec(shape, lambda s: (0,) * len(shape), pipeline_mode=one)
    cache = lambda shape: pl.BlockSpec((None, None) + shape, lambda s: (s, layer, 0, 0), pipeline_mode=one)
    rope_g = _rope_tables(HEAD_DIM, HEAD_DIM // 4)
    rope_m = _rope_tables(MLA_ROPE, MLA_ROPE // 4)
    span = BAND_BLOCK + 2 * WINDOW
    keys = t + PAST_LEN
    scratch = [
        pltpu.VMEM((t, D_MODEL), BF16),
        pltpu.VMEM((GQA_HEADS * HEAD_DIM, t), BF16),
        pltpu.VMEM((t, LANES), BF16),
        pltpu.VMEM((GQA_KV_HEADS, HEAD_DIM + ONES_ROWS, t), BF16),
        pltpu.VMEM((PAST_LEN, LANES), BF16),
        pltpu.VMEM((GQA_KV_HEADS, HEAD_DIM + ONES_ROWS, PAST_LEN), BF16),
        pltpu.VMEM((t, NA_HEADS // 2 * LANES), BF16),
        pltpu.VMEM((NA_HEADS, LANES, t), BF16),
        pltpu.VMEM((NA_HEADS, t, 2 * LANES), BF16),
        pltpu.VMEM((NA_HEADS // 2, LANES, 2 * PAST_LEN), BF16),
        pltpu.VMEM((NA_HEADS, PAST_LEN, 2 * LANES), BF16),
        pltpu.VMEM((3 * LANES, t), BF16),
        pltpu.VMEM((MLA_HEADS // 2, keys, 2 * LANES), BF16),
        pltpu.VMEM((MLA_HEADS, MLA_V + ONES_ROWS, keys), BF16),
    ]
    return pl.pallas_call(
        functools.partial(_lat_mixer_kernel, layer=layer),
        grid=(n,),
        in_specs=[
            pl.BlockSpec(memory_space=pltpu.SMEM),
            pl.BlockSpec((None, t, D_MODEL), lambda s: (s, 0, 0), pipeline_mode=one),
            _mod_spec(layer, False, 1),
            _layer_spec((1, D_MODEL), layer, 1),
            _layer_spec((D_MODEL, MAIN_COLS), layer, 1),
            _layer_spec((CONV_K, CONV_WIDTH), layer, 1),
            _layer_spec((1, MLA_Q_RANK), layer, 1),
            _layer_spec((MLA_Q_RANK, 3 * LANES), layer, 1),
            _layer_spec((1, MLA_KV_RANK), layer, 1),
            _layer_spec((MLA_KV_RANK, 2 * LANES), layer, 1),
            _layer_spec((2 * LANES, MLA_KV_RANK), layer, 1),
        ] + [const((t, LANES))] * 6 + [
            const((span, GQA_HEADS // GQA_KV_HEADS * BAND_BLOCK)),
            _layer_spec((NA_GROUP_ROWS * GRID_W, NA_HEADS * _na_table_width()), layer, 1),
            cache((2 * HEAD_DIM, PAST_LEN)), cache((2 * HEAD_DIM, PAST_LEN)),
            cache((4 * HEAD_DIM, PAST_LEN)), cache((4 * HEAD_DIM, PAST_LEN)),
            cache((PAST_LEN, MLA_KV_RANK)), cache((MLA_ROPE, PAST_LEN)),
        ],
        out_specs=pl.BlockSpec((None, t, Y_COLS), lambda s: (s, 0, 0)),
        out_shape=jax.ShapeDtypeStruct((n, t, Y_COLS), BF16),
        scratch_shapes=scratch,
        compiler_params=pltpu.CompilerParams(
            dimension_semantics=("arbitrary",), vmem_limit_bytes=VMEM_LIMIT),
        name=name,
    )(sink, x, mods, g_attn, w_main, w_conv, g_q, w_uq, g_kv, w_uk, w_uv_t, *rope_g, *rope_m, _band_mask(), nab, *caches)


def _post_kernel(x_ref, y_ref, mod_ref, ga_ref, gm_ref, gf_ref, wg_ref, wb_ref, wo_ref, w1_ref, w2_ref,
                 o_ref, *, final):
    d = D_MODEL
    mod = lambda i: mod_ref[:, i * d:(i + 1) * d]
    bounds = (Y_CONV, Y_GQA, Y_NA, Y_MLA, Y_COLS)
    for r0 in range(0, x_ref.shape[0], POST_CHAIN_ROWS):
        rows = slice(r0, r0 + POST_CHAIN_ROWS)
        x = x_ref[rows, :]
        h = _modulated_norm(x, ga_ref[...], mod_ref, 0, 1).astype(BF16)
        merged = None
        for i in range(N_BRANCH):
            lo, hi = bounds[i], bounds[i + 1]
            gate = jax.nn.sigmoid(_dot(h, wg_ref[:, i * d:(i + 1) * d]))
            term = gate * _dot(y_ref[rows, lo:hi], wb_ref[lo:hi, :])
            merged = term if merged is None else merged + term
        x = x + mod(2) * _dot(merged.astype(BF16), wo_ref[...])
        h = _modulated_norm(x, gm_ref[...], mod_ref, 3, 4).astype(BF16)
        mlp = None
        for c0 in range(0, D_FF, d):
            f = jnp.square(jnp.maximum(_dot(h, w1_ref[:, c0:c0 + d]), 0.0)).astype(BF16)
            term = _dot(f, w2_ref[c0:c0 + d, :])
            mlp = term if mlp is None else mlp + term
        x = x + mod(5) * mlp
        o_ref[rows, :] = _rms(x, gf_ref[...]) if final else x


def _post_call(x, y, mods, g_attn, g_mlp, g_final, w_gates, w_br, w_o, w_ff1, w_ff2, layer, ctx, name):
    n, t, _ = x.shape
    tm = 2 * POST_CHAIN_ROWS
    return pl.pallas_call(
        functools.partial(_post_kernel, final=layer == DEPTH - 1),
        grid=(n, t // tm),
        in_specs=[
            pl.BlockSpec((None, tm, D_MODEL), lambda s, i: (s, i, 0)),
            pl.BlockSpec((None, tm, Y_COLS), lambda s, i: (s, i, 0)),
            _mod_spec(layer, ctx, 2),
            _layer_spec((1, D_MODEL), layer, 2),
            _layer_spec((1, D_MODEL), layer, 2),
            pl.BlockSpec((1, D_MODEL), lambda s, i: (0, 0)),
            _layer_spec((D_MODEL, N_BRANCH * D_MODEL), layer, 2),
            _layer_spec((Y_COLS, D_MODEL), layer, 2),
            _layer_spec((D_MODEL, D_MODEL), layer, 2),
            _layer_spec((D_MODEL, D_FF), layer, 2),
            _layer_spec((D_FF, D_MODEL), layer, 2),
        ],
        out_specs=pl.BlockSpec((None, tm, D_MODEL), lambda s, i: (s, i, 0)),
        out_shape=jax.ShapeDtypeStruct((n, t, D_MODEL), F32),
        compiler_params=pltpu.CompilerParams(
            dimension_semantics=("arbitrary", "arbitrary"), vmem_limit_bytes=VMEM_LIMIT),
        name=name,
    )(x, y, mods, g_attn, g_mlp, g_final, w_gates, w_br, w_o, w_ff1, w_ff2)


def kernel(x_prompt, x_sample, cache_gqa_k, cache_gqa_v, cache_na_k, cache_na_v, cache_mla_ckv, cache_mla_krope, c, c_ctx, w_mod, b_mod, g_attn, g_mlp, w_in, w_conv, gqa_sink, na_rpb, mla_g_q, mla_w_uq, mla_g_kv, mla_w_ukv, w_branch_conv, w_branch_gqa, w_branch_na, w_branch_mla, w_o, w_ff1, w_ff2, g_final):
    w_main = w_in[:, :, :MAIN_COLS].astype(BF16)
    w_gates = w_in[:, :, GATE_COL0:].astype(BF16)
    uq = mla_w_uq.reshape(DEPTH, MLA_Q_RANK, MLA_HEADS, MLA_NOPE + MLA_ROPE)
    w_uq = jnp.concatenate([uq[..., :MLA_NOPE].reshape(DEPTH, MLA_Q_RANK, -1),
                            uq[..., MLA_NOPE:].reshape(DEPTH, MLA_Q_RANK, -1)], axis=-1).astype(BF16)
    ukv = mla_w_ukv.reshape(DEPTH, MLA_KV_RANK, MLA_HEADS, MLA_NOPE + MLA_V)
    w_uk = ukv[..., :MLA_NOPE].reshape(DEPTH, MLA_KV_RANK, -1).astype(BF16)
    w_uv = ukv[..., MLA_NOPE:].reshape(DEPTH, MLA_KV_RANK, -1).astype(BF16)
    w_uk_t, w_uv_t = jnp.transpose(w_uk, (0, 2, 1)), jnp.transpose(w_uv, (0, 2, 1))
    w_br = jnp.concatenate([w_branch_conv, w_branch_gqa, w_branch_na, w_branch_mla], axis=1).astype(BF16)
    w_o_b, w_ff1_b, w_ff2_b = w_o.astype(BF16), w_ff1.astype(BF16), w_ff2.astype(BF16)
    g_a, g_m = g_attn[:, None, :], g_mlp[:, None, :]
    g_q, g_kv, g_f = mla_g_q[:, None, :], mla_g_kv[:, None, :], g_final[None, :]

    c16 = jnp.concatenate([c, c_ctx[None, :], jnp.zeros((MOD_ROWS - DEC_BATCH - 1, D_MODEL), F32)], axis=0)
    mods = _mod_call(c16, w_mod, b_mod).reshape(DEPTH, MOD_ROWS, 1, 6 * D_MODEL)
    nab = _nab_call(na_rpb)

    heads_t = lambda a: jnp.transpose(a, (0, 1, 3, 4, 2)).reshape(a.shape[0], DEPTH, -1, a.shape[2])
    caches = (heads_t(cache_gqa_k), heads_t(cache_gqa_v), heads_t(cache_na_k), heads_t(cache_na_v),
              cache_mla_ckv, jnp.transpose(cache_mla_krope, (0, 1, 3, 2)))

    h_ctx, h_lat = x_prompt, x_sample
    states = None
    for l in range(DEPTH):
        mixer_w = (gqa_sink, w_conv, g_q, w_uq, g_kv)
        post_w = (g_a, g_m, g_f, w_gates, w_br, w_o_b, w_ff1_b, w_ff2_b)

        y, states = _ctx_mixer_call(h_ctx, mods, g_a, w_main, *mixer_w, w_uk_t, w_uv, states, l, f"mixer_ctx_{l}")
        flat = lambda a: a.reshape(1, BATCH * SEQ, a.shape[-1])
        h_ctx = _post_call(flat(h_ctx), flat(y), mods, *post_w, l, True, f"post_ctx_{l}").reshape(BATCH, SEQ, D_MODEL)

        y = _lat_mixer_call(h_lat, mods, g_a, w_main, *mixer_w, w_uk, w_uv_t, nab, caches, l, f"mixer_lat_{l}")
        h_lat = _post_call(h_lat, y, mods, *post_w, l, False, f"post_lat_{l}")

    def heads_out(a, heads):
        return jnp.transpose(a.reshape(BATCH, DEPTH, heads, HEAD_DIM, SEQ), (0, 1, 4, 2, 3))

    kg, vg, kn, vn, ckv, kr = states
    return (h_ctx, h_lat, heads_out(kg, GQA_KV_HEADS), heads_out(vg, GQA_KV_HEADS),
            heads_out(kn, NA_HEADS), heads_out(vn, NA_HEADS), ckv, jnp.transpose(kr, (0, 1, 3, 2)))
```

```python
import functools
import math

import numpy as np
import jax
import jax.numpy as jnp
from jax import lax
from jax.experimental import pallas as pl
from jax.experimental.pallas import tpu as pltpu

D_MODEL = 1024
BATCH = 32
SEQ = 256
DEPTH = 2
DEC_BATCH = 8
DEC_SEQ = 1024
PAST_LEN = 512
GRID_W = 64
GRID_ROWS = DEC_SEQ // GRID_W
HEAD_DIM = 64
CONV_WIDTH = 256
CONV_K = 3
GQA_HEADS = 8
GQA_KV_HEADS = 2
WINDOW = 128
BAND_BLOCK = 128
NA_HEADS = 4
NA_WIN_H = 8
NA_WIN_W = 16
MLA_HEADS = 4
MLA_Q_RANK = 256
MLA_KV_RANK = 128
MLA_NOPE = 64
MLA_ROPE = 32
MLA_V = 64
D_FF = 4 * D_MODEL
N_BRANCH = 4
ROPE_BASE = 10000.0
EPS = 1e-6
NEG_INF = -1e30
LOG2E = math.log2(math.e)
ATTN_SCALE = HEAD_DIM ** -0.5
MLA_SCALE = (MLA_NOPE + MLA_ROPE) ** -0.5

LANES = 128
MOD_ROWS = 16
CTX_MOD_ROW = DEC_BATCH

C_CB, C_CC, C_CV = 0, 256, 512
C_GQ, C_GK, C_GV = 768, 1280, 1408
C_NQ, C_NK, C_NV = 1536, 1792, 2048
C_MQ, C_MKV, C_MKR = 2304, 2560, 2688
MAIN_COLS = 2816
GATE_COL0 = 2720
Y_CONV, Y_GQA, Y_NA, Y_MLA = 0, 256, 768, 1024
Y_COLS = 1280
NA_GROUPS = 4
NA_GROUP_ROWS = GRID_ROWS // NA_GROUPS

VMEM_LIMIT = 56 * 1024 * 1024
POST_CHAIN_ROWS = 256
CTX_SEQS = 4

F32 = jnp.float32
BF16 = jnp.bfloat16


def _dot(a, b):
    return jnp.dot(a, b, preferred_element_type=F32)


def _dot_nt(a, b):
    return lax.dot_general(a, b, (((1,), (1,)), ((), ())), preferred_element_type=F32)


def _rms(x, g):
    return x * lax.rsqrt(jnp.mean(x * x, axis=-1, keepdims=True) + EPS) * g


def _lane_lt(shape, n):
    return lax.broadcasted_iota(jnp.int32, shape, len(shape) - 1) < n


def _row_lt(shape, n):
    return lax.broadcasted_iota(jnp.int32, shape, 0) < n


def _row_group(shape, lo, hi):
    row = lax.broadcasted_iota(jnp.int32, shape, 0)
    return (row >= lo) & (row < hi)


def _short_conv(cb, cc, cv, w):
    u = cc * cv
    t = u.shape[0]
    row = lax.broadcasted_iota(jnp.int32, u.shape, 0)
    prev = jnp.where(row == 0, 0.0, pltpu.roll(u, 1, 0))
    nxt = jnp.where(row == t - 1, 0.0, pltpu.roll(u, t - 1, 0))
    return cb * (prev * w[0:1, :] + u * w[1:2, :] + nxt * w[2:3, :])


def _rope(x, cos, sin_hi, sin_lo, half):
    n = x.shape[-1]
    return x * cos + pltpu.roll(x, n - half, 1) * sin_lo + pltpu.roll(x, half, 1) * sin_hi


def _repeat_rope_key(tile):
    k = jnp.where(_lane_lt(tile.shape, MLA_ROPE), tile, 0.0)
    k = k + pltpu.roll(k, MLA_ROPE, 1)
    return k + pltpu.roll(k, 2 * MLA_ROPE, 1)


def _mod_kernel(c_ref, w_ref, b_ref, o_ref):
    c = c_ref[...]
    s = c * jax.nn.sigmoid(c)
    o_ref[...] = _dot(s.astype(BF16), w_ref[...].astype(BF16)) + b_ref[...]


def _mod_call(c16, w_mod, b_mod):
    tn = 1536
    return pl.pallas_call(
        _mod_kernel,
        grid=(DEPTH, 6 * D_MODEL // tn),
        in_specs=[
            pl.BlockSpec((MOD_ROWS, D_MODEL), lambda l, j: (0, 0)),
            pl.BlockSpec((None, D_MODEL, tn), lambda l, j: (l, 0, j)),
            pl.BlockSpec((None, 1, tn), lambda l, j: (l, 0, j)),
        ],
        out_specs=pl.BlockSpec((None, MOD_ROWS, tn), lambda l, j: (l, 0, j)),
        out_shape=jax.ShapeDtypeStruct((DEPTH, MOD_ROWS, 6 * D_MODEL), F32),
        compiler_params=pltpu.CompilerParams(
            dimension_semantics=("arbitrary", "arbitrary"), vmem_limit_bytes=VMEM_LIMIT),
        name="adaln_mod",
    )(c16, w_mod, b_mod.reshape(DEPTH, 1, 6 * D_MODEL))


def _na_window_start(r):
    return min(max(r - NA_WIN_H // 2, 0), GRID_ROWS - NA_WIN_H)


def _na_group_slab(k):
    starts = [_na_window_start(r) for r in range(k * NA_GROUP_ROWS, (k + 1) * NA_GROUP_ROWS)]
    lo = min(starts) // 2 * 2
    n = -(-(max(starts) + NA_WIN_H - lo) // 4) * 4
    assert lo + n <= GRID_ROWS
    return lo, n


def _na_group_kinds():
    kinds, of_group = [], []
    for k in range(NA_GROUPS):
        lo, n = _na_group_slab(k)
        layout = {}
        for dr in range(NA_GROUP_ROWS):
            r = k * NA_GROUP_ROWS + dr
            r0 = _na_window_start(r)
            for a in range(n):
                inside = r0 <= lo + a < r0 + NA_WIN_H
                layout[(dr, a)] = lo + a - r + NA_WIN_H - 1 if inside else None
        if (n, layout) not in kinds:
            kinds.append((n, layout))
        of_group.append(kinds.index((n, layout)))
    return kinds, of_group


def _na_table_width():
    return sum(n for n, _ in _na_group_kinds()[0]) * GRID_W


def _nab_kernel(rpb_ref, o_ref):
    l = pl.program_id(0)
    c = lax.broadcasted_iota(jnp.int32, (GRID_W, GRID_W), 0)
    w = lax.broadcasted_iota(jnp.int32, (GRID_W, GRID_W), 1)
    dc = w - c + (NA_WIN_W - 1)
    c0 = jnp.clip(c - NA_WIN_W // 2, 0, GRID_W - NA_WIN_W)
    outside = (w < c0) | (w >= c0 + NA_WIN_W)
    n_dr, n_dc = 2 * NA_WIN_H - 1, 2 * NA_WIN_W - 1
    kinds, _ = _na_group_kinds()
    width = _na_table_width()

    def put(h, d, tile):
        col0 = h * width
        for n, layout in kinds:
            for (dr, a), want in layout.items():
                if want == d:
                    o_ref[dr * GRID_W:(dr + 1) * GRID_W, col0 + a * GRID_W:col0 + (a + 1) * GRID_W] = tile
            col0 += n * GRID_W

    for h in range(NA_HEADS):
        put(h, None, jnp.full((GRID_W, GRID_W), NEG_INF, F32))
        for d in range(n_dr):
            base = ((l * NA_HEADS + h) * n_dr + d) * n_dc
            tile = jnp.full((GRID_W, GRID_W), NEG_INF, F32)
            for j in range(n_dc):
                tile = jnp.where(dc == j, rpb_ref[base + j] * LOG2E, tile)
            put(h, d, jnp.where(outside, NEG_INF, tile))


def _nab_call(na_rpb):
    shape = (NA_GROUP_ROWS * GRID_W, NA_HEADS * _na_table_width())
    return pl.pallas_call(
        _nab_kernel,
        grid=(DEPTH,),
        in_specs=[pl.BlockSpec(memory_space=pltpu.SMEM)],
        out_specs=pl.BlockSpec((None,) + shape, lambda l: (l, 0, 0)),
        out_shape=jax.ShapeDtypeStruct((DEPTH,) + shape, F32),
        compiler_params=pltpu.CompilerParams(dimension_semantics=("arbitrary",), vmem_limit_bytes=VMEM_LIMIT),
        name="na_bias_tables",
    )(na_rpb.reshape(-1))


def _mod_spec(layer, ctx, n_grid):
    if n_grid == 1:
        index = (lambda s: (layer, CTX_MOD_ROW, 0, 0)) if ctx else (lambda s: (layer, s, 0, 0))
    else:
        index = (lambda s, i: (layer, CTX_MOD_ROW, 0, 0)) if ctx else (lambda s, i: (layer, s, 0, 0))
    return pl.BlockSpec((None, None, 1, 6 * D_MODEL), index)


def _layer_spec(shape, layer, n_grid):
    zeros = (0,) * len(shape)
    index = (lambda s: (layer,) + zeros) if n_grid == 1 else (lambda s, i: (layer,) + zeros)
    return pl.BlockSpec((None,) + tuple(shape), index, pipeline_mode=pl.Buffered(1))


_COLUMN_GROUPS = ((C_CB, C_GQ), (C_GQ, C_NQ), (C_NQ, C_MQ), (C_MQ, MAIN_COLS))


def _modulated_norm(x, g, mod_ref, shift, scale):
    d = D_MODEL
    return _rms(x, g) * (1.0 + mod_ref[:, scale * d:(scale + 1) * d]) + mod_ref[:, shift * d:(shift + 1) * d]


def _projected_columns(h_ref, w_ref):
    cache = {}

    def col(c, n):
        lo, hi = next(g for g in _COLUMN_GROUPS if g[0] <= c and c + n <= g[1])
        if lo not in cache:
            cache.clear()
            cache[lo] = _dot(h_ref[...], w_ref[:, lo:hi])
        return cache[lo][:, c - lo:c - lo + n]

    return col


def _key_planes(k_t, even_first, odd_first):
    lo = _row_lt(k_t.shape, HEAD_DIM)
    swapped = None
    if not even_first or odd_first:
        swapped = pltpu.roll(k_t, HEAD_DIM, 0)
    top = jnp.where(lo, k_t if even_first else swapped, 0.0)
    bot = jnp.where(lo, 0.0, swapped if odd_first else k_t)
    return top.astype(BF16), bot.astype(BF16)


def _value_planes(pair, even_first, odd_first):
    lo = _lane_lt(pair.shape, HEAD_DIM)
    swapped = None
    if not even_first or odd_first:
        swapped = pltpu.roll(pair, HEAD_DIM, 1)
    top = jnp.where(lo, pair if even_first else swapped, 0.0)
    bot = jnp.where(lo, 0.0, swapped if odd_first else pair)
    one_e = jnp.where(lo, 1.0, 0.0)
    return (jnp.concatenate([top, one_e], axis=1).astype(BF16),
            jnp.concatenate([bot, 1.0 - one_e], axis=1).astype(BF16))


def _probabilities(parts, sink=None):
    m = parts[0].max(axis=-1, keepdims=True)
    for p in parts[1:]:
        m = jnp.maximum(m, p.max(axis=-1, keepdims=True))
    if sink is not None:
        m = jnp.maximum(m, sink)
    probs = [jnp.exp2((p - m).astype(BF16)) for p in parts]
    return probs, (None if sink is None else jnp.exp2(sink - m))


def _attend(terms, sink_e=None, sink_o=None):
    o = None
    for p, v in terms:
        t = _dot(p, v)
        o = t if o is None else o + t
    den = o[:, LANES:]
    if sink_e is not None:
        den = den + jnp.where(_lane_lt(den.shape, HEAD_DIM), sink_e, sink_o)
    return o[:, :LANES] / den


ONES_ROWS = 16


def _value_plane_t(v_t):
    return jnp.concatenate([v_t, jnp.ones((ONES_ROWS, v_t.shape[1]), v_t.dtype)], axis=0).astype(BF16)


def _probabilities_t(parts, sink=None):
    m = parts[0].max(axis=0, keepdims=True)
    for p in parts[1:]:
        m = jnp.maximum(m, p.max(axis=0, keepdims=True))
    if sink is not None:
        m = jnp.maximum(m, sink)
    probs = [jnp.exp2((p - m).astype(BF16)) for p in parts]
    return probs, (None if sink is None else jnp.exp2(sink - m))


def _attend_t(terms, sink_num=None):
    o = None
    for v_t, p in terms:
        t = _dot(v_t, p)
        o = t if o is None else o + t
    den = o[HEAD_DIM:HEAD_DIM + 1, :]
    if sink_num is not None:
        den = den + sink_num
    return o[0:HEAD_DIM, :] / den


def _mla_key_plane(kn_t_pair, kr4_t, j, odd):
    h = 2 * j + odd
    lo = _row_lt(kn_t_pair.shape, HEAD_DIM)
    nope = jnp.where(lo, 0.0, kn_t_pair) if odd else jnp.where(lo, kn_t_pair, 0.0)
    rope = jnp.where(_row_group(kr4_t.shape, h * MLA_ROPE, (h + 1) * MLA_ROPE), kr4_t, 0.0)
    return jnp.concatenate([nope, rope], axis=0).astype(BF16)


def _ctx_mixer_kernel(sink_ref, x_ref, mod_ref, ga_ref, win_ref, wc_ref, gq_ref, wuq_ref, gkv_ref, wukt_ref, wuv_ref,
                      *rest, layer, first):
    y_ref, kg_ref, vg_ref, kn_ref, vn_ref, ckv_ref, kr_ref, h_s = rest[-8:]
    t = SEQ
    for b in range(CTX_SEQS):
        h_s[b * t:(b + 1) * t, :] = _modulated_norm(x_ref[b], ga_ref[...], mod_ref, 0, 1).astype(BF16)
    col_all = _projected_columns(h_s, win_ref)

    def put_state(ref, b, val):
        if first:
            for d in range(DEPTH):
                ref[b, d] = val if d == layer else jnp.zeros_like(val)
        else:
            ref[b] = val

    def conv(b, col):
        y_ref[b, :, Y_CONV:Y_CONV + CONV_WIDTH] = _short_conv(
            col(C_CB, CONV_WIDTH), col(C_CC, CONV_WIDTH), col(C_CV, CONV_WIDTH), wc_ref[...]).astype(BF16)

    def gqa(b, col):
        kg_t, vpair = col(C_GK, LANES).T, col(C_GV, LANES)
        put_state(kg_ref, b, kg_t)
        put_state(vg_ref, b, vpair.T)
        row2 = lax.broadcasted_iota(jnp.int32, (2 * t, 1), 0) < t
        for g in range(GQA_KV_HEADS):
            ke, ko = _key_planes(kg_t, g == 0, g == 0)
            ve, vo = _value_planes(vpair, g == 0, g == 0)
            q = jnp.concatenate([col(C_GQ + (2 * g) * LANES, LANES), col(C_GQ + (2 * g + 1) * LANES, LANES)], axis=0)
            q = (q * (ATTN_SCALE * LOG2E)).astype(BF16)
            s = _dot(q, jnp.concatenate([ke, ko], axis=1))
            sink = [jnp.where(row2, sink_ref[layer, 4 * g + odd], sink_ref[layer, 4 * g + 2 + odd]) * LOG2E
                    for odd in range(2)]
            (pe,), xe = _probabilities([s[:, 0:t]], sink[0])
            (po,), xo = _probabilities([s[:, t:2 * t]], sink[1])
            o = _attend([(pe, ve), (po, vo)], xe, xo).astype(BF16)
            c0 = Y_GQA + (2 * g) * LANES
            y_ref[b, :, c0:c0 + LANES] = o[0:t]
            y_ref[b, :, c0 + LANES:c0 + 2 * LANES] = o[t:2 * t]

    def na(b, col):
        kn_t = col(C_NK, 2 * LANES).T
        put_state(kn_ref, b, kn_t)
        put_state(vn_ref, b, col(C_NV, 2 * LANES).T)
        for j in range(NA_HEADS // 2):
            ke, ko = _key_planes(kn_t[j * LANES:(j + 1) * LANES, :], True, False)
            ve, vo = _value_planes(col(C_NV + j * LANES, LANES), True, False)
            q = (col(C_NQ + j * LANES, LANES) * (ATTN_SCALE * LOG2E)).astype(BF16)
            s = _dot(q, jnp.concatenate([ke, ko], axis=1))
            (pe,), _ = _probabilities([s[:, 0:t]])
            (po,), _ = _probabilities([s[:, t:2 * t]])
            y_ref[b, :, Y_NA + j * LANES:Y_NA + (j + 1) * LANES] = _attend([(pe, ve), (po, vo)]).astype(BF16)

    def mla(b, col):
        ckv = _rms(col(C_MKV, MLA_KV_RANK), gkv_ref[...])
        put_state(ckv_ref, b, ckv)
        kr_t = col(C_MKR, LANES).T[0:MLA_ROPE, :]
        put_state(kr_ref, b, kr_t)
        ckv_b = ckv.astype(BF16)
        q = _dot(_rms(col(C_MQ, MLA_Q_RANK), gq_ref[...]).astype(BF16), wuq_ref[...]) * (MLA_SCALE * LOG2E)
        kn_t_all = _dot_nt(wukt_ref[...], ckv_b)
        v_all = _dot(ckv_b, wuv_ref[...])
        kr4_t = jnp.concatenate([kr_t] * MLA_HEADS, axis=0)
        q_rope = q[:, 2 * LANES:3 * LANES]
        for j in range(MLA_HEADS // 2):
            qj = jnp.concatenate([q[:, j * LANES:(j + 1) * LANES], q_rope], axis=1).astype(BF16)
            kn_t_pair = kn_t_all[j * LANES:(j + 1) * LANES, :]
            keys = jnp.concatenate(
                [_mla_key_plane(kn_t_pair, kr4_t, j, 0), _mla_key_plane(kn_t_pair, kr4_t, j, 1)], axis=1)
            s = _dot(qj, keys)
            ve, vo = _value_planes(v_all[:, j * LANES:(j + 1) * LANES], True, False)
            (pe,), _ = _probabilities([s[:, 0:t]])
            (po,), _ = _probabilities([s[:, t:2 * t]])
            y_ref[b, :, Y_MLA + j * LANES:Y_MLA + (j + 1) * LANES] = _attend([(pe, ve), (po, vo)]).astype(BF16)

    for mixer in (conv, gqa, na, mla):
        for b in range(CTX_SEQS):
            mixer(b, lambda c, n, b=b: col_all(c, n)[b * t:(b + 1) * t])


def _ctx_mixer_call(x, mods, g_attn, w_main, sink, w_conv, g_q, w_uq, g_kv, w_uk_t, w_uv, prev_states, layer, name):
    n, nb = BATCH, CTX_SEQS
    state_shapes = ((2 * HEAD_DIM, SEQ), (2 * HEAD_DIM, SEQ), (4 * HEAD_DIM, SEQ), (4 * HEAD_DIM, SEQ),
                    (SEQ, MLA_KV_RANK), (MLA_ROPE, SEQ))
    first = prev_states is None
    if first:
        state_spec = lambda shape: pl.BlockSpec((nb, DEPTH) + shape, lambda s: (s, 0, 0, 0))
    else:
        state_spec = lambda shape: pl.BlockSpec((nb, None) + shape, lambda s: (s, layer, 0, 0))
    in_specs = [
        pl.BlockSpec(memory_space=pltpu.SMEM),
        pl.BlockSpec((nb, SEQ, D_MODEL), lambda s: (s, 0, 0)),
        _mod_spec(layer, True, 1),
        _layer_spec((1, D_MODEL), layer, 1),
        _layer_spec((D_MODEL, MAIN_COLS), layer, 1),
        _layer_spec((CONV_K, CONV_WIDTH), layer, 1),
        _layer_spec((1, MLA_Q_RANK), layer, 1),
        _layer_spec((MLA_Q_RANK, 3 * LANES), layer, 1),
        _layer_spec((1, MLA_KV_RANK), layer, 1),
        _layer_spec((2 * LANES, MLA_KV_RANK), layer, 1),
        _layer_spec((MLA_KV_RANK, 2 * LANES), layer, 1),
    ]
    args = [sink, x, mods, g_attn, w_main, w_conv, g_q, w_uq, g_kv, w_uk_t, w_uv]
    aliases = {}
    if not first:
        for i, st in enumerate(prev_states):
            aliases[len(args)] = 1 + i
            in_specs.append(pl.BlockSpec(memory_space=pl.ANY))
            args.append(st)
    outs = pl.pallas_call(
        functools.partial(_ctx_mixer_kernel, layer=layer, first=first),
        grid=(n // nb,),
        in_specs=in_specs,
        out_specs=[pl.BlockSpec((nb, SEQ, Y_COLS), lambda s: (s, 0, 0))] + [state_spec(s) for s in state_shapes],
        out_shape=[jax.ShapeDtypeStruct((n, SEQ, Y_COLS), BF16)]
        + [jax.ShapeDtypeStruct((n, DEPTH) + s, F32) for s in state_shapes],
        input_output_aliases=aliases,
        scratch_shapes=[pltpu.VMEM((nb * SEQ, D_MODEL), BF16)],
        compiler_params=pltpu.CompilerParams(
            dimension_semantics=("arbitrary",), vmem_limit_bytes=VMEM_LIMIT),
        name=name,
    )(*args)
    return outs[0], outs[1:]


def _lat_mixer_kernel(sink_ref, x_ref, mod_ref, ga_ref, win_ref, wc_ref, gq_ref, wuq_ref, gkv_ref, wuk_ref, wuvt_ref,
                      rc_ref, rsh_ref, rsl_ref, mc_ref, msh_ref, msl_ref, band_ref, nab_ref,
                      cgk_ref, cgv_ref, cnk_ref, cnv_ref, cckv_ref, ckr_ref,
                      y_ref,
                      h_s, gq_s, gk_s, gv_s, gkc_s, gvc_s, nq_s, nk_s, nv_s, nkc_s, nvc_s, mq_s, mk_s, mv_s, *, layer):
    t = DEC_SEQ
    h_s[...] = _modulated_norm(x_ref[...], ga_ref[...], mod_ref, 0, 1).astype(BF16)
    col = _projected_columns(h_s, win_ref)
    rope64 = lambda x: _rope(x, rc_ref[...], rsh_ref[...], rsl_ref[...], HEAD_DIM // 4)
    rope32 = lambda x: _rope(x, mc_ref[...], msh_ref[...], msl_ref[...], MLA_ROPE // 4)

    y_ref[:, Y_CONV:Y_CONV + CONV_WIDTH] = _short_conv(
        col(C_CB, CONV_WIDTH), col(C_CC, CONV_WIDTH), col(C_CV, CONV_WIDTH), wc_ref[...]).astype(BF16)

    group = GQA_HEADS // GQA_KV_HEADS
    for j in range(GQA_HEADS // 2):
        q_pair = rope64(col(C_GQ + j * LANES, LANES)) * (ATTN_SCALE * LOG2E)
        gq_s[j * LANES:(j + 1) * LANES, :] = q_pair.T.astype(BF16)
    gk_s[...] = rope64(col(C_GK, LANES)).astype(BF16)
    gkc_s[...] = cgk_ref[...].T.astype(BF16)
    v_t = col(C_GV, LANES).T
    for g in range(GQA_KV_HEADS):
        gv_s[g] = _value_plane_t(v_t[g * HEAD_DIM:(g + 1) * HEAD_DIM, :])
        gvc_s[g] = _value_plane_t(cgv_ref[g * HEAD_DIM:(g + 1) * HEAD_DIM, :])

    def gqa_block(b, c_lo, c_hi):
        q0 = pl.multiple_of(b * BAND_BLOCK, BAND_BLOCK)
        k0 = pl.multiple_of(q0 + (c_lo - WINDOW), BAND_BLOCK)
        n = c_hi - c_lo
        band = band_ref[c_lo:c_hi, :]
        lane_head = lax.shift_right_logical(
            lax.broadcasted_iota(jnp.int32, (1, group * BAND_BLOCK), 1), BAND_BLOCK.bit_length() - 1)
        zeros = jnp.zeros((HEAD_DIM, group * BAND_BLOCK), BF16)
        for g in range(GQA_KV_HEADS):
            q_t = jnp.concatenate(
                [gq_s[(group * g + h) * HEAD_DIM:(group * g + h + 1) * HEAD_DIM, pl.ds(q0, BAND_BLOCK)]
                 for h in range(group)], axis=1)
            q_t = jnp.concatenate([q_t, zeros] if g == 0 else [zeros, q_t], axis=0)
            s_loc = _dot(gk_s[pl.ds(k0, n), :], q_t) + band
            s_ctx = _dot(gkc_s[...], q_t)
            sink = sink_ref[layer, group * g + group - 1]
            for h in range(group - 2, -1, -1):
                sink = jnp.where(lane_head == h, sink_ref[layer, group * g + h], sink)
            (p_loc, p_ctx), x = _probabilities_t([s_loc, s_ctx], sink * LOG2E)
            o = _attend_t([(gv_s[g, :, pl.ds(k0, n)], p_loc), (gvc_s[g], p_ctx)], x)
            for pr in range(group // 2):
                pair = jnp.concatenate([o[:, (2 * pr) * BAND_BLOCK:(2 * pr + 1) * BAND_BLOCK],
                                        o[:, (2 * pr + 1) * BAND_BLOCK:(2 * pr + 2) * BAND_BLOCK]], axis=0)
                c0 = Y_GQA + (group // 2 * g + pr) * LANES
                y_ref[pl.ds(q0, BAND_BLOCK), c0:c0 + LANES] = pair.T.astype(BF16)

    for j in range(NA_HEADS // 2):
        nq_s[:, j * LANES:(j + 1) * LANES] = (col(C_NQ + j * LANES, LANES) * (ATTN_SCALE * LOG2E)).astype(BF16)
        nk_s[2 * j], nk_s[2 * j + 1] = _key_planes(col(C_NK + j * LANES, LANES).T, True, False)
        nv_s[2 * j], nv_s[2 * j + 1] = _value_planes(col(C_NV + j * LANES, LANES), True, False)
        nkc_s[j] = jnp.concatenate(_key_planes(cnk_ref[j * LANES:(j + 1) * LANES, :], True, False), axis=1)
        nvc_s[2 * j], nvc_s[2 * j + 1] = _value_planes(cnv_ref[j * LANES:(j + 1) * LANES, :].T, True, False)

    na_kinds, na_kind_of = _na_group_kinds()
    na_width = _na_table_width()

    def na_group(k):
        lo, n = _na_group_slab(k)
        q0, nq = k * NA_GROUP_ROWS * GRID_W, NA_GROUP_ROWS * GRID_W
        k0, nk = lo * GRID_W, n * GRID_W
        b0 = sum(kn for kn, _ in na_kinds[:na_kind_of[k]]) * GRID_W
        for j in range(NA_HEADS // 2):
            q = nq_s[q0:q0 + nq, j * LANES:(j + 1) * LANES]
            s_ctx = _dot(q, nkc_s[j])
            terms = []
            for odd in range(2):
                h = 2 * j + odd
                s_loc = _dot(q, nk_s[h, :, k0:k0 + nk]) + nab_ref[:, h * na_width + b0:h * na_width + b0 + nk]
                (p_loc, p_ctx), _ = _probabilities([s_loc, s_ctx[:, odd * PAST_LEN:(odd + 1) * PAST_LEN]])
                terms += [(p_loc, nv_s[h, k0:k0 + nk, :]), (p_ctx, nvc_s[h])]
            y_ref[q0:q0 + nq, Y_NA + j * LANES:Y_NA + (j + 1) * LANES] = _attend(terms).astype(BF16)

    ckv_b = _rms(col(C_MKV, MLA_KV_RANK), gkv_ref[...]).astype(BF16)
    cckv_b = cckv_ref[...].astype(BF16)
    q = _dot(_rms(col(C_MQ, MLA_Q_RANK), gq_ref[...]).astype(BF16), wuq_ref[...]) * (MLA_SCALE * LOG2E)
    for i in range(3):
        tile = q[:, i * LANES:(i + 1) * LANES]
        mq_s[i * LANES:(i + 1) * LANES, :] = (rope32(tile) if i == 2 else tile).T.astype(BF16)
    kr = jnp.where(_lane_lt((t, LANES), MLA_ROPE), rope32(col(C_MKR, LANES)), 0.0)
    kr_c = jnp.concatenate([ckr_ref[...], jnp.zeros((LANES - MLA_ROPE, PAST_LEN), F32)], axis=0).T
    for rows, ckv_x, kr_x in ((slice(0, t), ckv_b, kr), (slice(t, t + PAST_LEN), cckv_b, kr_c)):
        kn_all = _dot(ckv_x, wuk_ref[...])
        v_t_all = _dot_nt(wuvt_ref[...], ckv_x)
        for j in range(MLA_HEADS // 2):
            mk_s[j, rows, :] = jnp.concatenate([kn_all[:, j * LANES:(j + 1) * LANES], kr_x], axis=1).astype(BF16)
        for h in range(MLA_HEADS):
            mv_s[h, :, rows] = _value_plane_t(v_t_all[h * MLA_V:(h + 1) * MLA_V, :])

    tq = 256

    def mla_block(i, carry):
        q0 = pl.multiple_of(i * tq, tq)
        zeros = lambda r: jnp.zeros((r, tq), BF16)
        q_nope = lambda h: mq_s[h * MLA_NOPE:(h + 1) * MLA_NOPE, pl.ds(q0, tq)]
        q_rope = lambda h: mq_s[MLA_HEADS * MLA_NOPE + h * MLA_ROPE:MLA_HEADS * MLA_NOPE + (h + 1) * MLA_ROPE,
                                pl.ds(q0, tq)]
        pad = 2 * LANES - 2 * MLA_NOPE - MLA_ROPE
        for j in range(MLA_HEADS // 2):
            he, ho = 2 * j, 2 * j + 1
            q_e = jnp.concatenate([q_nope(he), zeros(MLA_NOPE), q_rope(he), zeros(pad)], axis=0)
            q_o = jnp.concatenate([zeros(MLA_NOPE), q_nope(ho), q_rope(ho), zeros(pad)], axis=0)
            s = _dot(mk_s[j], jnp.concatenate([q_e, q_o], axis=1))
            outs = []
            for odd in range(2):
                (p,), _ = _probabilities_t([s[:, odd * tq:(odd + 1) * tq]])
                outs.append(_attend_t([(mv_s[2 * j + odd], p)]))
            y_ref[pl.ds(q0, tq), Y_MLA + j * LANES:Y_MLA + (j + 1) * LANES] = (
                jnp.concatenate(outs, axis=0).T.astype(BF16))
        return carry

    span = BAND_BLOCK + 2 * WINDOW
    nb = t // BAND_BLOCK
    for i in range(nb):
        gqa_block(jnp.int32(i), WINDOW if i == 0 else 0, span - WINDOW if i == nb - 1 else span)
        if i % (nb // NA_GROUPS) == 0:
            na_group(i // (nb // NA_GROUPS))
        if i % (nb * tq // t) == 0:
            mla_block(jnp.int32(i // (nb * tq // t)), 0)


def _rope_tables(group, half):
    tok = np.arange(DEC_SEQ)
    pos = np.stack([tok // GRID_W, tok % GRID_W], axis=1).astype(np.float64)
    inv = ROPE_BASE ** (-np.arange(half, dtype=np.float64) / half)
    lane = np.arange(LANES) % group
    axis = lane // (2 * half)
    within = lane % (2 * half)
    ang = pos[:, axis] * inv[within % half][None, :]
    cos, sin = np.cos(ang), np.sin(ang)
    upper = (within >= half)[None, :]
    sin_hi = np.where(upper, sin, 0.0)
    sin_lo = np.where(upper, 0.0, -sin)
    return tuple(jnp.asarray(a, dtype=F32) for a in (cos, sin_hi, sin_lo))


def _band_mask():
    c = np.arange(BAND_BLOCK + 2 * WINDOW)[:, None]
    i = np.arange(BAND_BLOCK)[None, :]
    ok = (c >= i) & (c <= i + 2 * WINDOW)
    m = np.where(ok, 0.0, NEG_INF)
    return jnp.asarray(np.concatenate([m] * (GQA_HEADS // GQA_KV_HEADS), axis=1), dtype=F32)


def _lat_mixer_call(x, mods, g_attn, w_main, sink, w_conv, g_q, w_uq, g_kv, w_uk, w_uv_t, nab, caches, layer, name):
    n, t = DEC_BATCH, DEC_SEQ
    one = pl.Buffered(1)
    const = lambda shape: pl.BlockSpec(shape, lambda s: (0,) * len(shape), pipeline_mode=one)
    cache = lambda shape: pl.BlockSpec((None, None) + shape, lambda s: (s, layer, 0, 0), pipeline_mode=one)
    rope_g = _rope_tables(HEAD_DIM, HEAD_DIM // 4)
    rope_m = _rope_tables(MLA_ROPE, MLA_ROPE // 4)
    span = BAND_BLOCK + 2 * WINDOW
    keys = t + PAST_LEN
    scratch = [
        pltpu.VMEM((t, D_MODEL), BF16),
        pltpu.VMEM((GQA_HEADS * HEAD_DIM, t), BF16),
        pltpu.VMEM((t, LANES), BF16),
        pltpu.VMEM((GQA_KV_HEADS, HEAD_DIM + ONES_ROWS, t), BF16),
        pltpu.VMEM((PAST_LEN, LANES), BF16),
        pltpu.VMEM((GQA_KV_HEADS, HEAD_DIM + ONES_ROWS, PAST_LEN), BF16),
        pltpu.VMEM((t, NA_HEADS // 2 * LANES), BF16),
        pltpu.VMEM((NA_HEADS, LANES, t), BF16),
        pltpu.VMEM((NA_HEADS, t, 2 * LANES), BF16),
        pltpu.VMEM((NA_HEADS // 2, LANES, 2 * PAST_LEN), BF16),
        pltpu.VMEM((NA_HEADS, PAST_LEN, 2 * LANES), BF16),
        pltpu.VMEM((3 * LANES, t), BF16),
        pltpu.VMEM((MLA_HEADS // 2, keys, 2 * LANES), BF16),
        pltpu.VMEM((MLA_HEADS, MLA_V + ONES_ROWS, keys), BF16),
    ]
    return pl.pallas_call(
        functools.partial(_lat_mixer_kernel, layer=layer),
        grid=(n,),
        in_specs=[
            pl.BlockSpec(memory_space=pltpu.SMEM),
            pl.BlockSpec((None, t, D_MODEL), lambda s: (s, 0, 0), pipeline_mode=one),
            _mod_spec(layer, False, 1),
            _layer_spec((1, D_MODEL), layer, 1),
            _layer_spec((D_MODEL, MAIN_COLS), layer, 1),
            _layer_spec((CONV_K, CONV_WIDTH), layer, 1),
            _layer_spec((1, MLA_Q_RANK), layer, 1),
            _layer_spec((MLA_Q_RANK, 3 * LANES), layer, 1),
            _layer_spec((1, MLA_KV_RANK), layer, 1),
            _layer_spec((MLA_KV_RANK, 2 * LANES), layer, 1),
            _layer_spec((2 * LANES, MLA_KV_RANK), layer, 1),
        ] + [const((t, LANES))] * 6 + [
            const((span, GQA_HEADS // GQA_KV_HEADS * BAND_BLOCK)),
            _layer_spec((NA_GROUP_ROWS * GRID_W, NA_HEADS * _na_table_width()), layer, 1),
            cache((2 * HEAD_DIM, PAST_LEN)), cache((2 * HEAD_DIM, PAST_LEN)),
            cache((4 * HEAD_DIM, PAST_LEN)), cache((4 * HEAD_DIM, PAST_LEN)),
            cache((PAST_LEN, MLA_KV_RANK)), cache((MLA_ROPE, PAST_LEN)),
        ],
        out_specs=pl.BlockSpec((None, t, Y_COLS), lambda s: (s, 0, 0)),
        out_shape=jax.ShapeDtypeStruct((n, t, Y_COLS), BF16),
        scratch_shapes=scratch,
        compiler_params=pltpu.CompilerParams(
            dimension_semantics=("arbitrary",), vmem_limit_bytes=VMEM_LIMIT),
        name=name,
    )(sink, x, mods, g_attn, w_main, w_conv, g_q, w_uq, g_kv, w_uk, w_uv_t, *rope_g, *rope_m, _band_mask(), nab, *caches)


def _post_kernel(x_ref, y_ref, mod_ref, ga_ref, gm_ref, gf_ref, wg_ref, wb_ref, wo_ref, w1_ref, w2_ref,
                 o_ref, *, final):
    d = D_MODEL
    mod = lambda i: mod_ref[:, i * d:(i + 1) * d]
    bounds = (Y_CONV, Y_GQA, Y_NA, Y_MLA, Y_COLS)
    for r0 in range(0, x_ref.shape[0], POST_CHAIN_ROWS):
        rows = slice(r0, r0 + POST_CHAIN_ROWS)
        x = x_ref[rows, :]
        h = _modulated_norm(x, ga_ref[...], mod_ref, 0, 1).astype(BF16)
        merged = None
        for i in range(N_BRANCH):
            lo, hi = bounds[i], bounds[i + 1]
            gate = jax.nn.sigmoid(_dot(h, wg_ref[:, i * d:(i + 1) * d]))
            term = gate * _dot(y_ref[rows, lo:hi], wb_ref[lo:hi, :])
            merged = term if merged is None else merged + term
        x = x + mod(2) * _dot(merged.astype(BF16), wo_ref[...])
        h = _modulated_norm(x, gm_ref[...], mod_ref, 3, 4).astype(BF16)
        mlp = None
        for c0 in range(0, D_FF, d):
            f = jnp.square(jnp.maximum(_dot(h, w1_ref[:, c0:c0 + d]), 0.0)).astype(BF16)
            term = _dot(f, w2_ref[c0:c0 + d, :])
            mlp = term if mlp is None else mlp + term
        x = x + mod(5) * mlp
        o_ref[rows, :] = _rms(x, gf_ref[...]) if final else x


def _post_call(x, y, mods, g_attn, g_mlp, g_final, w_gates, w_br, w_o, w_ff1, w_ff2, layer, ctx, name):
    n, t, _ = x.shape
    tm = 2 * POST_CHAIN_ROWS
    return pl.pallas_call(
        functools.partial(_post_kernel, final=layer == DEPTH - 1),
        grid=(n, t // tm),
        in_specs=[
            pl.BlockSpec((None, tm, D_MODEL), lambda s, i: (s, i, 0)),
            pl.BlockSpec((None, tm, Y_COLS), lambda s, i: (s, i, 0)),
            _mod_spec(layer, ctx, 2),
            _layer_spec((1, D_MODEL), layer, 2),
            _layer_spec((1, D_MODEL), layer, 2),
            pl.BlockSpec((1, D_MODEL), lambda s, i: (0, 0)),
            _layer_spec((D_MODEL, N_BRANCH * D_MODEL), layer, 2),
            _layer_spec((Y_COLS, D_MODEL), layer, 2),
            _layer_spec((D_MODEL, D_MODEL), layer, 2),
            _layer_spec((D_MODEL, D_FF), layer, 2),
            _layer_spec((D_FF, D_MODEL), layer, 2),
        ],
        out_specs=pl.BlockSpec((None, tm, D_MODEL), lambda s, i: (s, i, 0)),
        out_shape=jax.ShapeDtypeStruct((n, t, D_MODEL), F32),
        compiler_params=pltpu.CompilerParams(
            dimension_semantics=("arbitrary", "arbitrary"), vmem_limit_bytes=VMEM_LIMIT),
        name=name,
    )(x, y, mods, g_attn, g_mlp, g_final, w_gates, w_br, w_o, w_ff1, w_ff2)


def kernel(x_prompt, x_sample, cache_gqa_k, cache_gqa_v, cache_na_k, cache_na_v, cache_mla_ckv, cache_mla_krope, c, c_ctx, w_mod, b_mod, g_attn, g_mlp, w_in, w_conv, gqa_sink, na_rpb, mla_g_q, mla_w_uq, mla_g_kv, mla_w_ukv, w_branch_conv, w_branch_gqa, w_branch_na, w_branch_mla, w_o, w_ff1, w_ff2, g_final):
    w_main = w_in[:, :, :MAIN_COLS].astype(BF16)
    w_gates = w_in[:, :, GATE_COL0:].astype(BF16)
    uq = mla_w_uq.reshape(DEPTH, MLA_Q_RANK, MLA_HEADS, MLA_NOPE + MLA_ROPE)
    w_uq = jnp.concatenate([uq[..., :MLA_NOPE].reshape(DEPTH, MLA_Q_RANK, -1),
                            uq[..., MLA_NOPE:].reshape(DEPTH, MLA_Q_RANK, -1)], axis=-1).astype(BF16)
    ukv = mla_w_ukv.reshape(DEPTH, MLA_KV_RANK, MLA_HEADS, MLA_NOPE + MLA_V)
    w_uk = ukv[..., :MLA_NOPE].reshape(DEPTH, MLA_KV_RANK, -1).astype(BF16)
    w_uv = ukv[..., MLA_NOPE:].reshape(DEPTH, MLA_KV_RANK, -1).astype(BF16)
    w_uk_t, w_uv_t = jnp.transpose(w_uk, (0, 2, 1)), jnp.transpose(w_uv, (0, 2, 1))
    w_br = jnp.concatenate([w_branch_conv, w_branch_gqa, w_branch_na, w_branch_mla], axis=1).astype(BF16)
    w_o_b, w_ff1_b, w_ff2_b = w_o.astype(BF16), w_ff1.astype(BF16), w_ff2.astype(BF16)
    g_a, g_m = g_attn[:, None, :], g_mlp[:, None, :]
    g_q, g_kv, g_f = mla_g_q[:, None, :], mla_g_kv[:, None, :], g_final[None, :]

    c16 = jnp.concatenate([c, c_ctx[None, :], jnp.zeros((MOD_ROWS - DEC_BATCH - 1, D_MODEL), F32)], axis=0)
    mods = _mod_call(c16, w_mod, b_mod).reshape(DEPTH, MOD_ROWS, 1, 6 * D_MODEL)
    nab = _nab_call(na_rpb)

    heads_t = lambda a: jnp.transpose(a, (0, 1, 3, 4, 2)).reshape(a.shape[0], DEPTH, -1, a.shape[2])
    caches = (heads_t(cache_gqa_k), heads_t(cache_gqa_v), heads_t(cache_na_k), heads_t(cache_na_v),
              cache_mla_ckv, jnp.transpose(cache_mla_krope, (0, 1, 3, 2)))

    h_ctx, h_lat = x_prompt, x_sample
    states = None
    for l in range(DEPTH):
        mixer_w = (gqa_sink, w_conv, g_q, w_uq, g_kv)
        post_w = (g_a, g_m, g_f, w_gates, w_br, w_o_b, w_ff1_b, w_ff2_b)

        y, states = _ctx_mixer_call(h_ctx, mods, g_a, w_main, *mixer_w, w_uk_t, w_uv, states, l, f"mixer_ctx_{l}")
        flat = lambda a: a.reshape(1, BATCH * SEQ, a.shape[-1])
        h_ctx = _post_call(flat(h_ctx), flat(y), mods, *post_w, l, True, f"post_ctx_{l}").reshape(BATCH, SEQ, D_MODEL)

        y = _lat_mixer_call(h_lat, mods, g_a, w_main, *mixer_w, w_uk, w_uv_t, nab, caches, l, f"mixer_lat_{l}")
        h_lat = _post_call(h_lat, y, mods, *post_w, l, False, f"post_lat_{l}")

    def heads_out(a, heads):
        return jnp.transpose(a.reshape(BATCH, DEPTH, heads, HEAD_DIM, SEQ), (0, 1, 4, 2, 3))

    kg, vg, kn, vn, ckv, kr = states
    return (h_ctx, h_lat, heads_out(kg, GQA_KV_HEADS), heads_out(vg, GQA_KV_HEADS),
            heads_out(kn, NA_HEADS), heads_out(vn, NA_HEADS), ckv, jnp.transpose(kr, (0, 1, 3, 2)))
```

```python
import functools
import math

import numpy as np
import jax
import jax.numpy as jnp
from jax import lax
from jax.experimental import pallas as pl
from jax.experimental.pallas import tpu as pltpu

D_MODEL = 1024
BATCH = 32
SEQ = 256
DEPTH = 2
DEC_BATCH = 8
DEC_SEQ = 1024
PAST_LEN = 512
GRID_W = 64
GRID_ROWS = DEC_SEQ // GRID_W
HEAD_DIM = 64
CONV_WIDTH = 256
CONV_K = 3
GQA_HEADS = 8
GQA_KV_HEADS = 2
WINDOW = 128
BAND_BLOCK = 128
NA_HEADS = 4
NA_WIN_H = 8
NA_WIN_W = 16
MLA_HEADS = 4
MLA_Q_RANK = 256
MLA_KV_RANK = 128
MLA_NOPE = 64
MLA_ROPE = 32
MLA_V = 64
D_FF = 4 * D_MODEL
N_BRANCH = 4
ROPE_BASE = 10000.0
EPS = 1e-6
NEG_INF = -1e30
LOG2E = math.log2(math.e)
ATTN_SCALE = HEAD_DIM ** -0.5
MLA_SCALE = (MLA_NOPE + MLA_ROPE) ** -0.5

LANES = 128
MOD_ROWS = 16
CTX_MOD_ROW = DEC_BATCH

C_CB, C_CC, C_CV = 0, 256, 512
C_GQ, C_GK, C_GV = 768, 1280, 1408
C_NQ, C_NK, C_NV = 1536, 1792, 2048
C_MQ, C_MKV, C_MKR = 2304, 2560, 2688
MAIN_COLS = 2816
GATE_COL0 = 2720
Y_CONV, Y_GQA, Y_NA, Y_MLA = 0, 256, 768, 1024
Y_COLS = 1280
NA_GROUPS = 4
NA_GROUP_ROWS = GRID_ROWS // NA_GROUPS

VMEM_LIMIT = 56 * 1024 * 1024
POST_CHAIN_ROWS = 256
CTX_SEQS = 4

F32 = jnp.float32
BF16 = jnp.bfloat16


def _dot(a, b):
    return jnp.dot(a, b, preferred_element_type=F32)


def _dot_nt(a, b):
    return lax.dot_general(a, b, (((1,), (1,)), ((), ())), preferred_element_type=F32)


def _rms(x, g):
    return x * lax.rsqrt(jnp.mean(x * x, axis=-1, keepdims=True) + EPS) * g


def _lane_lt(shape, n):
    return lax.broadcasted_iota(jnp.int32, shape, len(shape) - 1) < n


def _row_lt(shape, n):
    return lax.broadcasted_iota(jnp.int32, shape, 0) < n


def _row_group(shape, lo, hi):
    row = lax.broadcasted_iota(jnp.int32, shape, 0)
    return (row >= lo) & (row < hi)


def _short_conv(cb, cc, cv, w):
    u = cc * cv
    t = u.shape[0]
    row = lax.broadcasted_iota(jnp.int32, u.shape, 0)
    prev = jnp.where(row == 0, 0.0, pltpu.roll(u, 1, 0))
    nxt = jnp.where(row == t - 1, 0.0, pltpu.roll(u, t - 1, 0))
    return cb * (prev * w[0:1, :] + u * w[1:2, :] + nxt * w[2:3, :])


def _rope(x, cos, sin_hi, sin_lo, half):
    n = x.shape[-1]
    return x * cos + pltpu.roll(x, n - half, 1) * sin_lo + pltpu.roll(x, half, 1) * sin_hi


def _repeat_rope_key(tile):
    k = jnp.where(_lane_lt(tile.shape, MLA_ROPE), tile, 0.0)
    k = k + pltpu.roll(k, MLA_ROPE, 1)
    return k + pltpu.roll(k, 2 * MLA_ROPE, 1)


def _mod_kernel(c_ref, w_ref, b_ref, o_ref):
    c = c_ref[...]
    s = c * jax.nn.sigmoid(c)
    o_ref[...] = _dot(s.astype(BF16), w_ref[...].astype(BF16)) + b_ref[...]


def _mod_call(c16, w_mod, b_mod):
    tn = 1536
    return pl.pallas_call(
        _mod_kernel,
        grid=(DEPTH, 6 * D_MODEL // tn),
        in_specs=[
            pl.BlockSpec((MOD_ROWS, D_MODEL), lambda l, j: (0, 0)),
            pl.BlockSpec((None, D_MODEL, tn), lambda l, j: (l, 0, j)),
            pl.BlockSpec((None, 1, tn), lambda l, j: (l, 0, j)),
        ],
        out_specs=pl.BlockSpec((None, MOD_ROWS, tn), lambda l, j: (l, 0, j)),
        out_shape=jax.ShapeDtypeStruct((DEPTH, MOD_ROWS, 6 * D_MODEL), F32),
        compiler_params=pltpu.CompilerParams(
            dimension_semantics=("arbitrary", "arbitrary"), vmem_limit_bytes=VMEM_LIMIT),
        name="adaln_mod",
    )(c16, w_mod, b_mod.reshape(DEPTH, 1, 6 * D_MODEL))


def _na_window_start(r):
    return min(max(r - NA_WIN_H // 2, 0), GRID_ROWS - NA_WIN_H)


def _na_group_slab(k):
    starts = [_na_window_start(r) for r in range(k * NA_GROUP_ROWS, (k + 1) * NA_GROUP_ROWS)]
    lo = min(starts) // 2 * 2
    n = -(-(max(starts) + NA_WIN_H - lo) // 4) * 4
    assert lo + n <= GRID_ROWS
    return lo, n


def _na_group_kinds():
    kinds, of_group = [], []
    for k in range(NA_GROUPS):
        lo, n = _na_group_slab(k)
        layout = {}
        for dr in range(NA_GROUP_ROWS):
            r = k * NA_GROUP_ROWS + dr
            r0 = _na_window_start(r)
            for a in range(n):
                inside = r0 <= lo + a < r0 + NA_WIN_H
                layout[(dr, a)] = lo + a - r + NA_WIN_H - 1 if inside else None
        if (n, layout) not in kinds:
            kinds.append((n, layout))
        of_group.append(kinds.index((n, layout)))
    return kinds, of_group


def _na_table_width():
    return sum(n for n, _ in _na_group_kinds()[0]) * GRID_W


def _nab_kernel(rpb_ref, o_ref):
    l = pl.program_id(0)
    c = lax.broadcasted_iota(jnp.int32, (GRID_W, GRID_W), 0)
    w = lax.broadcasted_iota(jnp.int32, (GRID_W, GRID_W), 1)
    dc = w - c + (NA_WIN_W - 1)
    c0 = jnp.clip(c - NA_WIN_W // 2, 0, GRID_W - NA_WIN_W)
    outside = (w < c0) | (w >= c0 + NA_WIN_W)
    n_dr, n_dc = 2 * NA_WIN_H - 1, 2 * NA_WIN_W - 1
    kinds, _ = _na_group_kinds()
    width = _na_table_width()

    def put(h, d, tile):
        col0 = h * width
        for n, layout in kinds:
            for (dr, a), want in layout.items():
                if want == d:
                    o_ref[dr * GRID_W:(dr + 1) * GRID_W, col0 + a * GRID_W:col0 + (a + 1) * GRID_W] = tile
            col0 += n * GRID_W

    for h in range(NA_HEADS):
        put(h, None, jnp.full((GRID_W, GRID_W), NEG_INF, F32))
        for d in range(n_dr):
            base = ((l * NA_HEADS + h) * n_dr + d) * n_dc
            tile = jnp.full((GRID_W, GRID_W), NEG_INF, F32)
            for j in range(n_dc):
                tile = jnp.where(dc == j, rpb_ref[base + j] * LOG2E, tile)
            put(h, d, jnp.where(outside, NEG_INF, tile))


def _nab_call(na_rpb):
    shape = (NA_GROUP_ROWS * GRID_W, NA_HEADS * _na_table_width())
    return pl.pallas_call(
        _nab_kernel,
        grid=(DEPTH,),
        in_specs=[pl.BlockSpec(memory_space=pltpu.SMEM)],
        out_specs=pl.BlockSpec((None,) + shape, lambda l: (l, 0, 0)),
        out_shape=jax.ShapeDtypeStruct((DEPTH,) + shape, F32),
        compiler_params=pltpu.CompilerParams(dimension_semantics=("arbitrary",), vmem_limit_bytes=VMEM_LIMIT),
        name="na_bias_tables",
    )(na_rpb.reshape(-1))


def _mod_spec(layer, ctx, n_grid):
    if n_grid == 1:
        index = (lambda s: (layer, CTX_MOD_ROW, 0, 0)) if ctx else (lambda s: (layer, s, 0, 0))
    else:
        index = (lambda s, i: (layer, CTX_MOD_ROW, 0, 0)) if ctx else (lambda s, i: (layer, s, 0, 0))
    return pl.BlockSpec((None, None, 1, 6 * D_MODEL), index)


def _layer_spec(shape, layer, n_grid):
    zeros = (0,) * len(shape)
    index = (lambda s: (layer,) + zeros) if n_grid == 1 else (lambda s, i: (layer,) + zeros)
    return pl.BlockSpec((None,) + tuple(shape), index, pipeline_mode=pl.Buffered(1))


_COLUMN_GROUPS = ((C_CB, C_GQ), (C_GQ, C_NQ), (C_NQ, C_MQ), (C_MQ, MAIN_COLS))


def _modulated_norm(x, g, mod_ref, shift, scale):
    d = D_MODEL
    return _rms(x, g) * (1.0 + mod_ref[:, scale * d:(scale + 1) * d]) + mod_ref[:, shift * d:(shift + 1) * d]


def _projected_columns(h_ref, w_ref):
    cache = {}

    def col(c, n):
        lo, hi = next(g for g in _COLUMN_GROUPS if g[0] <= c and c + n <= g[1])
        if lo not in cache:
            cache.clear()
            cache[lo] = _dot_nt(h_ref[...], w_ref[lo:hi, :])
        return cache[lo][:, c - lo:c - lo + n]

    return col


def _key_planes(k_t, even_first, odd_first):
    lo = _row_lt(k_t.shape, HEAD_DIM)
    swapped = None
    if not even_first or odd_first:
        swapped = pltpu.roll(k_t, HEAD_DIM, 0)
    top = jnp.where(lo, k_t if even_first else swapped, 0.0)
    bot = jnp.where(lo, 0.0, swapped if odd_first else k_t)
    return top.astype(BF16), bot.astype(BF16)


def _value_planes(pair, even_first, odd_first):
    lo = _lane_lt(pair.shape, HEAD_DIM)
    swapped = None
    if not even_first or odd_first:
        swapped = pltpu.roll(pair, HEAD_DIM, 1)
    top = jnp.where(lo, pair if even_first else swapped, 0.0)
    bot = jnp.where(lo, 0.0, swapped if odd_first else pair)
    one_e = jnp.where(lo, 1.0, 0.0)
    return (jnp.concatenate([top, one_e], axis=1).astype(BF16),
            jnp.concatenate([bot, 1.0 - one_e], axis=1).astype(BF16))


def _probabilities(parts, sink=None):
    m = parts[0].max(axis=-1, keepdims=True)
    for p in parts[1:]:
        m = jnp.maximum(m, p.max(axis=-1, keepdims=True))
    if sink is not None:
        m = jnp.maximum(m, sink)
    probs = [jnp.exp2((p - m).astype(BF16)) for p in parts]
    return probs, (None if sink is None else jnp.exp2(sink - m))


def _attend(terms, sink_e=None, sink_o=None):
    o = None
    for p, v in terms:
        t = _dot(p, v)
        o = t if o is None else o + t
    den = o[:, LANES:]
    if sink_e is not None:
        den = den + jnp.where(_lane_lt(den.shape, HEAD_DIM), sink_e, sink_o)
    return o[:, :LANES] / den


ONES_ROWS = 16


def _value_plane_t(v_t):
    return jnp.concatenate([v_t, jnp.ones((ONES_ROWS, v_t.shape[1]), v_t.dtype)], axis=0).astype(BF16)


def _probabilities_t(parts, sink=None):
    m = parts[0].max(axis=0, keepdims=True)
    for p in parts[1:]:
        m = jnp.maximum(m, p.max(axis=0, keepdims=True))
    if sink is not None:
        m = jnp.maximum(m, sink)
    probs = [jnp.exp2((p - m).astype(BF16)) for p in parts]
    return probs, (None if sink is None else jnp.exp2(sink - m))


def _attend_t(terms, sink_num=None):
    o = None
    for v_t, p in terms:
        t = _dot(v_t, p)
        o = t if o is None else o + t
    den = o[HEAD_DIM:HEAD_DIM + 1, :]
    if sink_num is not None:
        den = den + sink_num
    return o[0:HEAD_DIM, :] / den


def _mla_key_plane(kn_t_pair, kr4_t, j, odd):
    h = 2 * j + odd
    lo = _row_lt(kn_t_pair.shape, HEAD_DIM)
    nope = jnp.where(lo, 0.0, kn_t_pair) if odd else jnp.where(lo, kn_t_pair, 0.0)
    rope = jnp.where(_row_group(kr4_t.shape, h * MLA_ROPE, (h + 1) * MLA_ROPE), kr4_t, 0.0)
    return jnp.concatenate([nope, rope], axis=0).astype(BF16)


def _ctx_mixer_kernel(sink_ref, x_ref, mod_ref, ga_ref, win_ref, wc_ref, gq_ref, wuq_ref, gkv_ref, wukt_ref, wuv_ref,
                      *rest, layer, first):
    y_ref, kg_ref, vg_ref, kn_ref, vn_ref, ckv_ref, kr_ref, h_s = rest[-8:]
    t = SEQ
    for b in range(CTX_SEQS):
        h_s[b * t:(b + 1) * t, :] = _modulated_norm(x_ref[b], ga_ref[...], mod_ref, 0, 1).astype(BF16)
    col_all = _projected_columns(h_s, win_ref)

    def put_state(ref, b, val):
        if first:
            for d in range(DEPTH):
                ref[b, d] = val if d == layer else jnp.zeros_like(val)
        else:
            ref[b] = val

    def conv(b, col):
        y_ref[b, :, Y_CONV:Y_CONV + CONV_WIDTH] = _short_conv(
            col(C_CB, CONV_WIDTH), col(C_CC, CONV_WIDTH), col(C_CV, CONV_WIDTH), wc_ref[...]).astype(BF16)

    def gqa(b, col):
        kg_t, vpair = col(C_GK, LANES).T, col(C_GV, LANES)
        put_state(kg_ref, b, kg_t)
        put_state(vg_ref, b, vpair.T)
        row2 = lax.broadcasted_iota(jnp.int32, (2 * t, 1), 0) < t
        for g in range(GQA_KV_HEADS):
            ke, ko = _key_planes(kg_t, g == 0, g == 0)
            ve, vo = _value_planes(vpair, g == 0, g == 0)
            q = jnp.concatenate([col(C_GQ + (2 * g) * LANES, LANES), col(C_GQ + (2 * g + 1) * LANES, LANES)], axis=0)
            q = (q * (ATTN_SCALE * LOG2E)).astype(BF16)
            s = _dot(q, jnp.concatenate([ke, ko], axis=1))
            sink = [jnp.where(row2, sink_ref[layer, 4 * g + odd], sink_ref[layer, 4 * g + 2 + odd]) * LOG2E
                    for odd in range(2)]
            (pe,), xe = _probabilities([s[:, 0:t]], sink[0])
            (po,), xo = _probabilities([s[:, t:2 * t]], sink[1])
            o = _attend([(pe, ve), (po, vo)], xe, xo).astype(BF16)
            c0 = Y_GQA + (2 * g) * LANES
            y_ref[b, :, c0:c0 + LANES] = o[0:t]
            y_ref[b, :, c0 + LANES:c0 + 2 * LANES] = o[t:2 * t]

    def na(b, col):
        kn_t = col(C_NK, 2 * LANES).T
        put_state(kn_ref, b, kn_t)
        put_state(vn_ref, b, col(C_NV, 2 * LANES).T)
        for j in range(NA_HEADS // 2):
            ke, ko = _key_planes(kn_t[j * LANES:(j + 1) * LANES, :], True, False)
            ve, vo = _value_planes(col(C_NV + j * LANES, LANES), True, False)
            q = (col(C_NQ + j * LANES, LANES) * (ATTN_SCALE * LOG2E)).astype(BF16)
            s = _dot(q, jnp.concatenate([ke, ko], axis=1))
            (pe,), _ = _probabilities([s[:, 0:t]])
            (po,), _ = _probabilities([s[:, t:2 * t]])
            y_ref[b, :, Y_NA + j * LANES:Y_NA + (j + 1) * LANES] = _attend([(pe, ve), (po, vo)]).astype(BF16)

    def mla(b, col):
        ckv = _rms(col(C_MKV, MLA_KV_RANK), gkv_ref[...])
        put_state(ckv_ref, b, ckv)
        kr_t = col(C_MKR, LANES).T[0:MLA_ROPE, :]
        put_state(kr_ref, b, kr_t)
        ckv_b = ckv.astype(BF16)
        q = _dot(_rms(col(C_MQ, MLA_Q_RANK), gq_ref[...]).astype(BF16), wuq_ref[...]) * (MLA_SCALE * LOG2E)
        kn_t_all = _dot_nt(wukt_ref[...], ckv_b)
        v_all = _dot(ckv_b, wuv_ref[...])
        kr4_t = jnp.concatenate([kr_t] * MLA_HEADS, axis=0)
        q_rope = q[:, 2 * LANES:3 * LANES]
        for j in range(MLA_HEADS // 2):
            qj = jnp.concatenate([q[:, j * LANES:(j + 1) * LANES], q_rope], axis=1).astype(BF16)
            kn_t_pair = kn_t_all[j * LANES:(j + 1) * LANES, :]
            keys = jnp.concatenate(
                [_mla_key_plane(kn_t_pair, kr4_t, j, 0), _mla_key_plane(kn_t_pair, kr4_t, j, 1)], axis=1)
            s = _dot(qj, keys)
            ve, vo = _value_planes(v_all[:, j * LANES:(j + 1) * LANES], True, False)
            (pe,), _ = _probabilities([s[:, 0:t]])
            (po,), _ = _probabilities([s[:, t:2 * t]])
            y_ref[b, :, Y_MLA + j * LANES:Y_MLA + (j + 1) * LANES] = _attend([(pe, ve), (po, vo)]).astype(BF16)

    for mixer in (conv, gqa, na, mla):
        for b in range(CTX_SEQS):
            mixer(b, lambda c, n, b=b: col_all(c, n)[b * t:(b + 1) * t])


def _ctx_mixer_call(x, mods, g_attn, w_main, sink, w_conv, g_q, w_uq, g_kv, w_uk_t, w_uv, prev_states, layer, name):
    n, nb = BATCH, CTX_SEQS
    state_shapes = ((2 * HEAD_DIM, SEQ), (2 * HEAD_DIM, SEQ), (4 * HEAD_DIM, SEQ), (4 * HEAD_DIM, SEQ),
                    (SEQ, MLA_KV_RANK), (MLA_ROPE, SEQ))
    first = prev_states is None
    if first:
        state_spec = lambda shape: pl.BlockSpec((nb, DEPTH) + shape, lambda s: (s, 0, 0, 0))
    else:
        state_spec = lambda shape: pl.BlockSpec((nb, None) + shape, lambda s: (s, layer, 0, 0))
    in_specs = [
        pl.BlockSpec(memory_space=pltpu.SMEM),
        pl.BlockSpec((nb, SEQ, D_MODEL), lambda s: (s, 0, 0)),
        _mod_spec(layer, True, 1),
        _layer_spec((1, D_MODEL), layer, 1),
        _layer_spec((MAIN_COLS, D_MODEL), layer, 1),
        _layer_spec((CONV_K, CONV_WIDTH), layer, 1),
        _layer_spec((1, MLA_Q_RANK), layer, 1),
        _layer_spec((MLA_Q_RANK, 3 * LANES), layer, 1),
        _layer_spec((1, MLA_KV_RANK), layer, 1),
        _layer_spec((2 * LANES, MLA_KV_RANK), layer, 1),
        _layer_spec((MLA_KV_RANK, 2 * LANES), layer, 1),
    ]
    args = [sink, x, mods, g_attn, w_main, w_conv, g_q, w_uq, g_kv, w_uk_t, w_uv]
    aliases = {}
    if not first:
        for i, st in enumerate(prev_states):
            aliases[len(args)] = 1 + i
            in_specs.append(pl.BlockSpec(memory_space=pl.ANY))
            args.append(st)
    outs = pl.pallas_call(
        functools.partial(_ctx_mixer_kernel, layer=layer, first=first),
        grid=(n // nb,),
        in_specs=in_specs,
        out_specs=[pl.BlockSpec((nb, SEQ, Y_COLS), lambda s: (s, 0, 0))] + [state_spec(s) for s in state_shapes],
        out_shape=[jax.ShapeDtypeStruct((n, SEQ, Y_COLS), BF16)]
        + [jax.ShapeDtypeStruct((n, DEPTH) + s, F32) for s in state_shapes],
        input_output_aliases=aliases,
        scratch_shapes=[pltpu.VMEM((nb * SEQ, D_MODEL), BF16)],
        compiler_params=pltpu.CompilerParams(
            dimension_semantics=("arbitrary",), vmem_limit_bytes=VMEM_LIMIT),
        name=name,
    )(*args)
    return outs[0], outs[1:]


def _lat_mixer_kernel(sink_ref, x_ref, mod_ref, ga_ref, win_ref, wc_ref, gq_ref, wuq_ref, gkv_ref, wuk_ref, wuvt_ref,
                      rc_ref, rsh_ref, rsl_ref, mc_ref, msh_ref, msl_ref, band_ref, nab_ref,
                      cgk_ref, cgv_ref, cnk_ref, cnv_ref, cckv_ref, ckr_ref,
                      y_ref,
                      h_s, gq_s, gk_s, gv_s, gkc_s, gvc_s, nq_s, nk_s, nv_s, nkc_s, nvc_s, mq_s, mk_s, mv_s, *, layer):
    t = DEC_SEQ
    h_s[...] = _modulated_norm(x_ref[...], ga_ref[...], mod_ref, 0, 1).astype(BF16)
    col = _projected_columns(h_s, win_ref)
    rope64 = lambda x: _rope(x, rc_ref[...], rsh_ref[...], rsl_ref[...], HEAD_DIM // 4)
    rope32 = lambda x: _rope(x, mc_ref[...], msh_ref[...], msl_ref[...], MLA_ROPE // 4)

    y_ref[:, Y_CONV:Y_CONV + CONV_WIDTH] = _short_conv(
        col(C_CB, CONV_WIDTH), col(C_CC, CONV_WIDTH), col(C_CV, CONV_WIDTH), wc_ref[...]).astype(BF16)

    group = GQA_HEADS // GQA_KV_HEADS
    for j in range(GQA_HEADS // 2):
        q_pair = rope64(col(C_GQ + j * LANES, LANES)) * (ATTN_SCALE * LOG2E)
        gq_s[j * LANES:(j + 1) * LANES, :] = q_pair.T.astype(BF16)
    gk_s[...] = rope64(col(C_GK, LANES)).astype(BF16)
    gkc_s[...] = cgk_ref[...].T.astype(BF16)
    v_t = col(C_GV, LANES).T
    for g in range(GQA_KV_HEADS):
        gv_s[g] = _value_plane_t(v_t[g * HEAD_DIM:(g + 1) * HEAD_DIM, :])
        gvc_s[g] = _value_plane_t(cgv_ref[g * HEAD_DIM:(g + 1) * HEAD_DIM, :])

    def gqa_block(b, c_lo, c_hi):
        q0 = pl.multiple_of(b * BAND_BLOCK, BAND_BLOCK)
        k0 = pl.multiple_of(q0 + (c_lo - WINDOW), BAND_BLOCK)
        n = c_hi - c_lo
        band = band_ref[c_lo:c_hi, :]
        lane_head = lax.shift_right_logical(
            lax.broadcasted_iota(jnp.int32, (1, group * BAND_BLOCK), 1), BAND_BLOCK.bit_length() - 1)
        zeros = jnp.zeros((HEAD_DIM, group * BAND_BLOCK), BF16)
        for g in range(GQA_KV_HEADS):
            q_t = jnp.concatenate(
                [gq_s[(group * g + h) * HEAD_DIM:(group * g + h + 1) * HEAD_DIM, pl.ds(q0, BAND_BLOCK)]
                 for h in range(group)], axis=1)
            q_t = jnp.concatenate([q_t, zeros] if g == 0 else [zeros, q_t], axis=0)
            s_loc = _dot(gk_s[pl.ds(k0, n), :], q_t) + band
            s_ctx = _dot(gkc_s[...], q_t)
            sink = sink_ref[layer, group * g + group - 1]
            for h in range(group - 2, -1, -1):
                sink = jnp.where(lane_head == h, sink_ref[layer, group * g + h], sink)
            (p_loc, p_ctx), x = _probabilities_t([s_loc, s_ctx], sink * LOG2E)
            o = _attend_t([(gv_s[g, :, pl.ds(k0, n)], p_loc), (gvc_s[g], p_ctx)], x)
            for pr in range(group // 2):
                pair = jnp.concatenate([o[:, (2 * pr) * BAND_BLOCK:(2 * pr + 1) * BAND_BLOCK],
                                        o[:, (2 * pr + 1) * BAND_BLOCK:(2 * pr + 2) * BAND_BLOCK]], axis=0)
                c0 = Y_GQA + (group // 2 * g + pr) * LANES
                y_ref[pl.ds(q0, BAND_BLOCK), c0:c0 + LANES] = pair.T.astype(BF16)

    for j in range(NA_HEADS // 2):
        nq_s[:, j * LANES:(j + 1) * LANES] = (col(C_NQ + j * LANES, LANES) * (ATTN_SCALE * LOG2E)).astype(BF16)
        nk_s[2 * j], nk_s[2 * j + 1] = _key_planes(col(C_NK + j * LANES, LANES).T, True, False)
        nv_s[2 * j], nv_s[2 * j + 1] = _value_planes(col(C_NV + j * LANES, LANES), True, False)
        nkc_s[j] = jnp.concatenate(_key_planes(cnk_ref[j * LANES:(j + 1) * LANES, :], True, False), axis=1)
        nvc_s[2 * j], nvc_s[2 * j + 1] = _value_planes(cnv_ref[j * LANES:(j + 1) * LANES, :].T, True, False)

    na_kinds, na_kind_of = _na_group_kinds()
    na_width = _na_table_width()

    def na_group(k):
        lo, n = _na_group_slab(k)
        q0, nq = k * NA_GROUP_ROWS * GRID_W, NA_GROUP_ROWS * GRID_W
        k0, nk = lo * GRID_W, n * GRID_W
        b0 = sum(kn for kn, _ in na_kinds[:na_kind_of[k]]) * GRID_W
        for j in range(NA_HEADS // 2):
            q = nq_s[q0:q0 + nq, j * LANES:(j + 1) * LANES]
            s_ctx = _dot(q, nkc_s[j])
            terms = []
            for odd in range(2):
                h = 2 * j + odd
                s_loc = _dot(q, nk_s[h, :, k0:k0 + nk]) + nab_ref[:, h * na_width + b0:h * na_width + b0 + nk]
                (p_loc, p_ctx), _ = _probabilities([s_loc, s_ctx[:, odd * PAST_LEN:(odd + 1) * PAST_LEN]])
                terms += [(p_loc, nv_s[h, k0:k0 + nk, :]), (p_ctx, nvc_s[h])]
            y_ref[q0:q0 + nq, Y_NA + j * LANES:Y_NA + (j + 1) * LANES] = _attend(terms).astype(BF16)

    ckv_b = _rms(col(C_MKV, MLA_KV_RANK), gkv_ref[...]).astype(BF16)
    cckv_b = cckv_ref[...].astype(BF16)
    q = _dot(_rms(col(C_MQ, MLA_Q_RANK), gq_ref[...]).astype(BF16), wuq_ref[...]) * (MLA_SCALE * LOG2E)
    for i in range(3):
        tile = q[:, i * LANES:(i + 1) * LANES]
        mq_s[i * LANES:(i + 1) * LANES, :] = (rope32(tile) if i == 2 else tile).T.astype(BF16)
    kr = jnp.where(_lane_lt((t, LANES), MLA_ROPE), rope32(col(C_MKR, LANES)), 0.0)
    kr_c = jnp.concatenate([ckr_ref[...], jnp.zeros((LANES - MLA_ROPE, PAST_LEN), F32)], axis=0).T
    for rows, ckv_x, kr_x in ((slice(0, t), ckv_b, kr), (slice(t, t + PAST_LEN), cckv_b, kr_c)):
        kn_all = _dot(ckv_x, wuk_ref[...])
        v_t_all = _dot_nt(wuvt_ref[...], ckv_x)
        for j in range(MLA_HEADS // 2):
            mk_s[j, rows, :] = jnp.concatenate([kn_all[:, j * LANES:(j + 1) * LANES], kr_x], axis=1).astype(BF16)
        for h in range(MLA_HEADS):
            mv_s[h, :, rows] = _value_plane_t(v_t_all[h * MLA_V:(h + 1) * MLA_V, :])

    tq = 256

    def mla_block(i, carry):
        q0 = pl.multiple_of(i * tq, tq)
        zeros = lambda r: jnp.zeros((r, tq), BF16)
        q_nope = lambda h: mq_s[h * MLA_NOPE:(h + 1) * MLA_NOPE, pl.ds(q0, tq)]
        q_rope = lambda h: mq_s[MLA_HEADS * MLA_NOPE + h * MLA_ROPE:MLA_HEADS * MLA_NOPE + (h + 1) * MLA_ROPE,
                                pl.ds(q0, tq)]
        pad = 2 * LANES - 2 * MLA_NOPE - MLA_ROPE
        for j in range(MLA_HEADS // 2):
            he, ho = 2 * j, 2 * j + 1
            q_e = jnp.concatenate([q_nope(he), zeros(MLA_NOPE), q_rope(he), zeros(pad)], axis=0)
            q_o = jnp.concatenate([zeros(MLA_NOPE), q_nope(ho), q_rope(ho), zeros(pad)], axis=0)
            s = _dot(mk_s[j], jnp.concatenate([q_e, q_o], axis=1))
            outs = []
            for odd in range(2):
                (p,), _ = _probabilities_t([s[:, odd * tq:(odd + 1) * tq]])
                outs.append(_attend_t([(mv_s[2 * j + odd], p)]))
            y_ref[pl.ds(q0, tq), Y_MLA + j * LANES:Y_MLA + (j + 1) * LANES] = (
                jnp.concatenate(outs, axis=0).T.astype(BF16))
        return carry

    span = BAND_BLOCK + 2 * WINDOW
    nb = t // BAND_BLOCK
    for i in range(nb):
        gqa_block(jnp.int32(i), WINDOW if i == 0 else 0, span - WINDOW if i == nb - 1 else span)
        if i % (nb // NA_GROUPS) == 0:
            na_group(i // (nb // NA_GROUPS))
        if i % (nb * tq // t) == 0:
            mla_block(jnp.int32(i // (nb * tq // t)), 0)


def _rope_tables(group, half):
    tok = np.arange(DEC_SEQ)
    pos = np.stack([tok // GRID_W, tok % GRID_W], axis=1).astype(np.float64)
    inv = ROPE_BASE ** (-np.arange(half, dtype=np.float64) / half)
    lane = np.arange(LANES) % group
    axis = lane // (2 * half)
    within = lane % (2 * half)
    ang = pos[:, axis] * inv[within % half][None, :]
    cos, sin = np.cos(ang), np.sin(ang)
    upper = (within >= half)[None, :]
    sin_hi = np.where(upper, sin, 0.0)
    sin_lo = np.where(upper, 0.0, -sin)
    return tuple(jnp.asarray(a, dtype=F32) for a in (cos, sin_hi, sin_lo))


def _band_mask():
    c = np.arange(BAND_BLOCK + 2 * WINDOW)[:, None]
    i = np.arange(BAND_BLOCK)[None, :]
    ok = (c >= i) & (c <= i + 2 * WINDOW)
    m = np.where(ok, 0.0, NEG_INF)
    return jnp.asarray(np.concatenate([m] * (GQA_HEADS // GQA_KV_HEADS), axis=1), dtype=F32)


def _lat_mixer_call(x, mods, g_attn, w_main, sink, w_conv, g_q, w_uq, g_kv, w_uk, w_uv_t, nab, caches, layer, name):
    n, t = DEC_BATCH, DEC_SEQ
    one = pl.Buffered(1)
    const = lambda shape: pl.BlockSpec(shape, lambda s: (0,) * len(shape), pipeline_mode=one)
    cache = lambda shape: pl.BlockSpec((None, None) + shape, lambda s: (s, layer, 0, 0), pipeline_mode=one)
    rope_g = _rope_tables(HEAD_DIM, HEAD_DIM // 4)
    rope_m = _rope_tables(MLA_ROPE, MLA_ROPE // 4)
    span = BAND_BLOCK + 2 * WINDOW
    keys = t + PAST_LEN
    scratch = [
        pltpu.VMEM((t, D_MODEL), BF16),
        pltpu.VMEM((GQA_HEADS * HEAD_DIM, t), BF16),
        pltpu.VMEM((t, LANES), BF16),
        pltpu.VMEM((GQA_KV_HEADS, HEAD_DIM + ONES_ROWS, t), BF16),
        pltpu.VMEM((PAST_LEN, LANES), BF16),
        pltpu.VMEM((GQA_KV_HEADS, HEAD_DIM + ONES_ROWS, PAST_LEN), BF16),
        pltpu.VMEM((t, NA_HEADS // 2 * LANES), BF16),
        pltpu.VMEM((NA_HEADS, LANES, t), BF16),
        pltpu.VMEM((NA_HEADS, t, 2 * LANES), BF16),
        pltpu.VMEM((NA_HEADS // 2, LANES, 2 * PAST_LEN), BF16),
        pltpu.VMEM((NA_HEADS, PAST_LEN, 2 * LANES), BF16),
        pltpu.VMEM((3 * LANES, t), BF16),
        pltpu.VMEM((MLA_HEADS // 2, keys, 2 * LANES), BF16),
        pltpu.VMEM((MLA_HEADS, MLA_V + ONES_ROWS, keys), BF16),
    ]
    return pl.pallas_call(
        functools.partial(_lat_mixer_kernel, layer=layer),
        grid=(n,),
        in_specs=[
            pl.BlockSpec(memory_space=pltpu.SMEM),
            pl.BlockSpec((None, t, D_MODEL), lambda s: (s, 0, 0), pipeline_mode=one),
            _mod_spec(layer, False, 1),
            _layer_spec((1, D_MODEL), layer, 1),
            _layer_spec((MAIN_COLS, D_MODEL), layer, 1),
            _layer_spec((CONV_K, CONV_WIDTH), layer, 1),
            _layer_spec((1, MLA_Q_RANK), layer, 1),
            _layer_spec((MLA_Q_RANK, 3 * LANES), layer, 1),
            _layer_spec((1, MLA_KV_RANK), layer, 1),
            _layer_spec((MLA_KV_RANK, 2 * LANES), layer, 1),
            _layer_spec((2 * LANES, MLA_KV_RANK), layer, 1),
        ] + [const((t, LANES))] * 6 + [
            const((span, GQA_HEADS // GQA_KV_HEADS * BAND_BLOCK)),
            _layer_spec((NA_GROUP_ROWS * GRID_W, NA_HEADS * _na_table_width()), layer, 1),
            cache((2 * HEAD_DIM, PAST_LEN)), cache((2 * HEAD_DIM, PAST_LEN)),
            cache((4 * HEAD_DIM, PAST_LEN)), cache((4 * HEAD_DIM, PAST_LEN)),
            cache((PAST_LEN, MLA_KV_RANK)), cache((MLA_ROPE, PAST_LEN)),
        ],
        out_specs=pl.BlockSpec((None, t, Y_COLS), lambda s: (s, 0, 0)),
        out_shape=jax.ShapeDtypeStruct((n, t, Y_COLS), BF16),
        scratch_shapes=scratch,
        compiler_params=pltpu.CompilerParams(
            dimension_semantics=("arbitrary",), vmem_limit_bytes=VMEM_LIMIT),
        name=name,
    )(sink, x, mods, g_attn, w_main, w_conv, g_q, w_uq, g_kv, w_uk, w_uv_t, *rope_g, *rope_m, _band_mask(), nab, *caches)


def _post_kernel(x_ref, y_ref, mod_ref, ga_ref, gm_ref, gf_ref, wg_ref, wb_ref, wo_ref, w1_ref, w2_ref,
                 o_ref, *, final):
    d = D_MODEL
    mod = lambda i: mod_ref[:, i * d:(i + 1) * d]
    bounds = (Y_CONV, Y_GQA, Y_NA, Y_MLA, Y_COLS)
    for r0 in range(0, x_ref.shape[0], POST_CHAIN_ROWS):
        rows = slice(r0, r0 + POST_CHAIN_ROWS)
        x = x_ref[rows, :]
        h = _modulated_norm(x, ga_ref[...], mod_ref, 0, 1).astype(BF16)
        merged = None
        for i in range(N_BRANCH):
            lo, hi = bounds[i], bounds[i + 1]
            gate = jax.nn.sigmoid(_dot_nt(h, wg_ref[i * d:(i + 1) * d, :]))
            term = gate * _dot(y_ref[rows, lo:hi], wb_ref[lo:hi, :])
            merged = term if merged is None else merged + term
        x = x + mod(2) * _dot(merged.astype(BF16), wo_ref[...])
        h = _modulated_norm(x, gm_ref[...], mod_ref, 3, 4).astype(BF16)
        mlp = None
        for c0 in range(0, D_FF, d):
            f = jnp.square(jnp.maximum(_dot(h, w1_ref[:, c0:c0 + d]), 0.0)).astype(BF16)
            term = _dot(f, w2_ref[c0:c0 + d, :])
            mlp = term if mlp is None else mlp + term
        x = x + mod(5) * mlp
        o_ref[rows, :] = _rms(x, gf_ref[...]) if final else x


def _post_call(x, y, mods, g_attn, g_mlp, g_final, w_gates, w_br, w_o, w_ff1, w_ff2, layer, ctx, name):
    n, t, _ = x.shape
    tm = 2 * POST_CHAIN_ROWS
    return pl.pallas_call(
        functools.partial(_post_kernel, final=layer == DEPTH - 1),
        grid=(n, t // tm),
        in_specs=[
            pl.BlockSpec((None, tm, D_MODEL), lambda s, i: (s, i, 0)),
            pl.BlockSpec((None, tm, Y_COLS), lambda s, i: (s, i, 0)),
            _mod_spec(layer, ctx, 2),
            _layer_spec((1, D_MODEL), layer, 2),
            _layer_spec((1, D_MODEL), layer, 2),
            pl.BlockSpec((1, D_MODEL), lambda s, i: (0, 0)),
            _layer_spec((N_BRANCH * D_MODEL, D_MODEL), layer, 2),
            _layer_spec((Y_COLS, D_MODEL), layer, 2),
            _layer_spec((D_MODEL, D_MODEL), layer, 2),
            _layer_spec((D_MODEL, D_FF), layer, 2),
            _layer_spec((D_FF, D_MODEL), layer, 2),
        ],
        out_specs=pl.BlockSpec((None, tm, D_MODEL), lambda s, i: (s, i, 0)),
        out_shape=jax.ShapeDtypeStruct((n, t, D_MODEL), F32),
        compiler_params=pltpu.CompilerParams(
            dimension_semantics=("arbitrary", "arbitrary"), vmem_limit_bytes=VMEM_LIMIT),
        name=name,
    )(x, y, mods, g_attn, g_mlp, g_final, w_gates, w_br, w_o, w_ff1, w_ff2)


def kernel(x_prompt, x_sample, cache_gqa_k, cache_gqa_v, cache_na_k, cache_na_v, cache_mla_ckv, cache_mla_krope, c, c_ctx, w_mod, b_mod, g_attn, g_mlp, w_in, w_conv, gqa_sink, na_rpb, mla_g_q, mla_w_uq, mla_g_kv, mla_w_ukv, w_branch_conv, w_branch_gqa, w_branch_na, w_branch_mla, w_o, w_ff1, w_ff2, g_final):
    w_in_t = jnp.transpose(w_in, (0, 2, 1))
    w_main = w_in_t[:, :MAIN_COLS, :].astype(BF16)
    w_gates = w_in_t[:, GATE_COL0:, :].astype(BF16)
    uq = mla_w_uq.reshape(DEPTH, MLA_Q_RANK, MLA_HEADS, MLA_NOPE + MLA_ROPE)
    w_uq = jnp.concatenate([uq[..., :MLA_NOPE].reshape(DEPTH, MLA_Q_RANK, -1),
                            uq[..., MLA_NOPE:].reshape(DEPTH, MLA_Q_RANK, -1)], axis=-1).astype(BF16)
    ukv = mla_w_ukv.reshape(DEPTH, MLA_KV_RANK, MLA_HEADS, MLA_NOPE + MLA_V)
    w_uk = ukv[..., :MLA_NOPE].reshape(DEPTH, MLA_KV_RANK, -1).astype(BF16)
    w_uv = ukv[..., MLA_NOPE:].reshape(DEPTH, MLA_KV_RANK, -1).astype(BF16)
    w_uk_t, w_uv_t = jnp.transpose(w_uk, (0, 2, 1)), jnp.transpose(w_uv, (0, 2, 1))
    w_br = jnp.concatenate([w_branch_conv, w_branch_gqa, w_branch_na, w_branch_mla], axis=1).astype(BF16)
    w_o_b, w_ff1_b, w_ff2_b = w_o.astype(BF16), w_ff1.astype(BF16), w_ff2.astype(BF16)
    g_a, g_m = g_attn[:, None, :], g_mlp[:, None, :]
    g_q, g_kv, g_f = mla_g_q[:, None, :], mla_g_kv[:, None, :], g_final[None, :]

    c16 = jnp.concatenate([c, c_ctx[None, :], jnp.zeros((MOD_ROWS - DEC_BATCH - 1, D_MODEL), F32)], axis=0)
    mods = _mod_call(c16, w_mod, b_mod).reshape(DEPTH, MOD_ROWS, 1, 6 * D_MODEL)
    nab = _nab_call(na_rpb)

    heads_t = lambda a: jnp.transpose(a, (0, 1, 3, 4, 2)).reshape(a.shape[0], DEPTH, -1, a.shape[2])
    caches = (heads_t(cache_gqa_k), heads_t(cache_gqa_v), heads_t(cache_na_k), heads_t(cache_na_v),
              cache_mla_ckv, jnp.transpose(cache_mla_krope, (0, 1, 3, 2)))

    h_ctx, h_lat = x_prompt, x_sample
    states = None
    for l in range(DEPTH):
        mixer_w = (gqa_sink, w_conv, g_q, w_uq, g_kv)
        post_w = (g_a, g_m, g_f, w_gates, w_br, w_o_b, w_ff1_b, w_ff2_b)

        y, states = _ctx_mixer_call(h_ctx, mods, g_a, w_main, *mixer_w, w_uk_t, w_uv, states, l, f"mixer_ctx_{l}")
        flat = lambda a: a.reshape(1, BATCH * SEQ, a.shape[-1])
        h_ctx = _post_call(flat(h_ctx), flat(y), mods, *post_w, l, True, f"post_ctx_{l}").reshape(BATCH, SEQ, D_MODEL)

        y = _lat_mixer_call(h_lat, mods, g_a, w_main, *mixer_w, w_uk, w_uv_t, nab, caches, l, f"mixer_lat_{l}")
        h_lat = _post_call(h_lat, y, mods, *post_w, l, False, f"post_lat_{l}")

    def heads_out(a, heads):
        return jnp.transpose(a.reshape(BATCH, DEPTH, heads, HEAD_DIM, SEQ), (0, 1, 4, 2, 3))

    kg, vg, kn, vn, ckv, kr = states
    return (h_ctx, h_lat, heads_out(kg, GQA_KV_HEADS), heads_out(vg, GQA_KV_HEADS),
            heads_out(kn, NA_HEADS), heads_out(vn, NA_HEADS), ckv, jnp.transpose(kr, (0, 1, 3, 2)))
```

```python
import functools
import math

import numpy as np
import jax
import jax.numpy as jnp
from jax import lax
from jax.experimental import pallas as pl
from jax.experimental.pallas import tpu as pltpu

D_MODEL = 1024
BATCH = 32
SEQ = 256
DEPTH = 2
DEC_BATCH = 8
DEC_SEQ = 1024
PAST_LEN = 512
GRID_W = 64
GRID_ROWS = DEC_SEQ // GRID_W
HEAD_DIM = 64
CONV_WIDTH = 256
CONV_K = 3
GQA_HEADS = 8
GQA_KV_HEADS = 2
WINDOW = 128
BAND_BLOCK = 128
NA_HEADS = 4
NA_WIN_H = 8
NA_WIN_W = 16
MLA_HEADS = 4
MLA_Q_RANK = 256
MLA_KV_RANK = 128
MLA_NOPE = 64
MLA_ROPE = 32
MLA_V = 64
D_FF = 4 * D_MODEL
N_BRANCH = 4
ROPE_BASE = 10000.0
EPS = 1e-6
NEG_INF = -1e30
LOG2E = math.log2(math.e)
ATTN_SCALE = HEAD_DIM ** -0.5
MLA_SCALE = (MLA_NOPE + MLA_ROPE) ** -0.5

LANES = 128
MOD_ROWS = 16
CTX_MOD_ROW = DEC_BATCH

C_CB, C_CC, C_CV = 0, 256, 512
C_GQ, C_GK, C_GV = 768, 1280, 1408
C_NQ, C_NK, C_NV = 1536, 1792, 2048
C_MQ, C_MKV, C_MKR = 2304, 2560, 2688
MAIN_COLS = 2816
GATE_COL0 = 2720
Y_CONV, Y_GQA, Y_NA, Y_MLA = 0, 256, 768, 1024
Y_COLS = 1280
NA_GROUPS = 4
NA_GROUP_ROWS = GRID_ROWS // NA_GROUPS

VMEM_LIMIT = 56 * 1024 * 1024
POST_CHAIN_ROWS = 256
CTX_SEQS = 4

F32 = jnp.float32
BF16 = jnp.bfloat16


def _dot(a, b):
    return jnp.dot(a, b, preferred_element_type=F32)


def _dot_nt(a, b):
    return lax.dot_general(a, b, (((1,), (1,)), ((), ())), preferred_element_type=F32)


def _rms(x, g):
    return x * lax.rsqrt(jnp.mean(x * x, axis=-1, keepdims=True) + EPS) * g


def _lane_lt(shape, n):
    return lax.broadcasted_iota(jnp.int32, shape, len(shape) - 1) < n


def _row_lt(shape, n):
    return lax.broadcasted_iota(jnp.int32, shape, 0) < n


def _row_group(shape, lo, hi):
    row = lax.broadcasted_iota(jnp.int32, shape, 0)
    return (row >= lo) & (row < hi)


def _short_conv(cb, cc, cv, w):
    u = cc * cv
    t = u.shape[0]
    row = lax.broadcasted_iota(jnp.int32, u.shape, 0)
    prev = jnp.where(row == 0, 0.0, pltpu.roll(u, 1, 0))
    nxt = jnp.where(row == t - 1, 0.0, pltpu.roll(u, t - 1, 0))
    return cb * (prev * w[0:1, :] + u * w[1:2, :] + nxt * w[2:3, :])


def _rope(x, cos, sin_hi, sin_lo, half):
    n = x.shape[-1]
    return x * cos + pltpu.roll(x, n - half, 1) * sin_lo + pltpu.roll(x, half, 1) * sin_hi


def _repeat_rope_key(tile):
    k = jnp.where(_lane_lt(tile.shape, MLA_ROPE), tile, 0.0)
    k = k + pltpu.roll(k, MLA_ROPE, 1)
    return k + pltpu.roll(k, 2 * MLA_ROPE, 1)


def _mod_kernel(c_ref, w_ref, b_ref, o_ref):
    c = c_ref[...]
    s = c * jax.nn.sigmoid(c)
    o_ref[...] = _dot(s.astype(BF16), w_ref[...].astype(BF16)) + b_ref[...]


def _mod_call(c16, w_mod, b_mod):
    tn = 1536
    return pl.pallas_call(
        _mod_kernel,
        grid=(DEPTH, 6 * D_MODEL // tn),
        in_specs=[
            pl.BlockSpec((MOD_ROWS, D_MODEL), lambda l, j: (0, 0)),
            pl.BlockSpec((None, D_MODEL, tn), lambda l, j: (l, 0, j)),
            pl.BlockSpec((None, 1, tn), lambda l, j: (l, 0, j)),
        ],
        out_specs=pl.BlockSpec((None, MOD_ROWS, tn), lambda l, j: (l, 0, j)),
        out_shape=jax.ShapeDtypeStruct((DEPTH, MOD_ROWS, 6 * D_MODEL), F32),
        compiler_params=pltpu.CompilerParams(
            dimension_semantics=("arbitrary", "arbitrary"), vmem_limit_bytes=VMEM_LIMIT),
        name="adaln_mod",
    )(c16, w_mod, b_mod.reshape(DEPTH, 1, 6 * D_MODEL))


def _na_window_start(r):
    return min(max(r - NA_WIN_H // 2, 0), GRID_ROWS - NA_WIN_H)


def _na_group_slab(k):
    starts = [_na_window_start(r) for r in range(k * NA_GROUP_ROWS, (k + 1) * NA_GROUP_ROWS)]
    lo = min(starts) // 2 * 2
    n = -(-(max(starts) + NA_WIN_H - lo) // 4) * 4
    assert lo + n <= GRID_ROWS
    return lo, n


def _na_group_kinds():
    kinds, of_group = [], []
    for k in range(NA_GROUPS):
        lo, n = _na_group_slab(k)
        layout = {}
        for dr in range(NA_GROUP_ROWS):
            r = k * NA_GROUP_ROWS + dr
            r0 = _na_window_start(r)
            for a in range(n):
                inside = r0 <= lo + a < r0 + NA_WIN_H
                layout[(dr, a)] = lo + a - r + NA_WIN_H - 1 if inside else None
        if (n, layout) not in kinds:
            kinds.append((n, layout))
        of_group.append(kinds.index((n, layout)))
    return kinds, of_group


def _na_table_width():
    return sum(n for n, _ in _na_group_kinds()[0]) * GRID_W


def _nab_kernel(rpb_ref, o_ref):
    l = pl.program_id(0)
    c = lax.broadcasted_iota(jnp.int32, (GRID_W, GRID_W), 0)
    w = lax.broadcasted_iota(jnp.int32, (GRID_W, GRID_W), 1)
    dc = w - c + (NA_WIN_W - 1)
    c0 = jnp.clip(c - NA_WIN_W // 2, 0, GRID_W - NA_WIN_W)
    outside = (w < c0) | (w >= c0 + NA_WIN_W)
    n_dr, n_dc = 2 * NA_WIN_H - 1, 2 * NA_WIN_W - 1
    kinds, _ = _na_group_kinds()
    width = _na_table_width()

    def put(h, d, tile):
        col0 = h * width
        for n, layout in kinds:
            for (dr, a), want in layout.items():
                if want == d:
                    o_ref[dr * GRID_W:(dr + 1) * GRID_W, col0 + a * GRID_W:col0 + (a + 1) * GRID_W] = tile
            col0 += n * GRID_W

    for h in range(NA_HEADS):
        put(h, None, jnp.full((GRID_W, GRID_W), NEG_INF, F32))
        for d in range(n_dr):
            base = ((l * NA_HEADS + h) * n_dr + d) * n_dc
            tile = jnp.full((GRID_W, GRID_W), NEG_INF, F32)
            for j in range(n_dc):
                tile = jnp.where(dc == j, rpb_ref[base + j] * LOG2E, tile)
            put(h, d, jnp.where(outside, NEG_INF, tile))


def _nab_call(na_rpb):
    shape = (NA_GROUP_ROWS * GRID_W, NA_HEADS * _na_table_width())
    return pl.pallas_call(
        _nab_kernel,
        grid=(DEPTH,),
        in_specs=[pl.BlockSpec(memory_space=pltpu.SMEM)],
        out_specs=pl.BlockSpec((None,) + shape, lambda l: (l, 0, 0)),
        out_shape=jax.ShapeDtypeStruct((DEPTH,) + shape, F32),
        compiler_params=pltpu.CompilerParams(dimension_semantics=("arbitrary",), vmem_limit_bytes=VMEM_LIMIT),
        name="na_bias_tables",
    )(na_rpb.reshape(-1))


def _mod_spec(layer, ctx, n_grid):
    if n_grid == 1:
        index = (lambda s: (layer, CTX_MOD_ROW, 0, 0)) if ctx else (lambda s: (layer, s, 0, 0))
    else:
        index = (lambda s, i: (layer, CTX_MOD_ROW, 0, 0)) if ctx else (lambda s, i: (layer, s, 0, 0))
    return pl.BlockSpec((None, None, 1, 6 * D_MODEL), index)


def _layer_spec(shape, layer, n_grid):
    zeros = (0,) * len(shape)
    index = (lambda s: (layer,) + zeros) if n_grid == 1 else (lambda s, i: (layer,) + zeros)
    return pl.BlockSpec((None,) + tuple(shape), index, pipeline_mode=pl.Buffered(1))


_COLUMN_GROUPS = ((C_CB, C_GQ), (C_GQ, C_NQ), (C_NQ, C_MQ), (C_MQ, MAIN_COLS))


def _modulated_norm(x, g, mod_ref, shift, scale):
    d = D_MODEL
    return _rms(x, g) * (1.0 + mod_ref[:, scale * d:(scale + 1) * d]) + mod_ref[:, shift * d:(shift + 1) * d]


def _projected_columns(h_ref, w_ref):
    cache = {}

    def col(c, n):
        lo, hi = next(g for g in _COLUMN_GROUPS if g[0] <= c and c + n <= g[1])
        if lo not in cache:
            cache.clear()
            cache[lo] = _dot_nt(h_ref[...], w_ref[lo:hi, :])
        return cache[lo][:, c - lo:c - lo + n]

    return col


def _key_planes(k_t, even_first, odd_first):
    lo = _row_lt(k_t.shape, HEAD_DIM)
    swapped = None
    if not even_first or odd_first:
        swapped = pltpu.roll(k_t, HEAD_DIM, 0)
    top = jnp.where(lo, k_t if even_first else swapped, 0.0)
    bot = jnp.where(lo, 0.0, swapped if odd_first else k_t)
    return top.astype(BF16), bot.astype(BF16)


def _value_planes(pair, even_first, odd_first):
    lo = _lane_lt(pair.shape, HEAD_DIM)
    swapped = None
    if not even_first or odd_first:
        swapped = pltpu.roll(pair, HEAD_DIM, 1)
    top = jnp.where(lo, pair if even_first else swapped, 0.0)
    bot = jnp.where(lo, 0.0, swapped if odd_first else pair)
    one_e = jnp.where(lo, 1.0, 0.0)
    return (jnp.concatenate([top, one_e], axis=1).astype(BF16),
            jnp.concatenate([bot, 1.0 - one_e], axis=1).astype(BF16))


def _probabilities(parts, sink=None):
    m = parts[0].max(axis=-1, keepdims=True)
    for p in parts[1:]:
        m = jnp.maximum(m, p.max(axis=-1, keepdims=True))
    if sink is not None:
        m = jnp.maximum(m, sink)
    probs = [jnp.exp2((p - m).astype(BF16)) for p in parts]
    return probs, (None if sink is None else jnp.exp2(sink - m))


def _attend(terms, sink_e=None, sink_o=None):
    o = None
    for p, v in terms:
        t = _dot(p, v)
        o = t if o is None else o + t
    den = o[:, LANES:]
    if sink_e is not None:
        den = den + jnp.where(_lane_lt(den.shape, HEAD_DIM), sink_e, sink_o)
    return o[:, :LANES] / den


ONES_ROWS = 16


def _value_plane_t(v_t):
    return jnp.concatenate([v_t, jnp.ones((ONES_ROWS, v_t.shape[1]), v_t.dtype)], axis=0).astype(BF16)


def _probabilities_t(parts, sink=None):
    m = parts[0].max(axis=0, keepdims=True)
    for p in parts[1:]:
        m = jnp.maximum(m, p.max(axis=0, keepdims=True))
    if sink is not None:
        m = jnp.maximum(m, sink)
    probs = [jnp.exp2((p - m).astype(BF16)) for p in parts]
    return probs, (None if sink is None else jnp.exp2(sink - m))


def _attend_t(terms, sink_num=None):
    o = None
    for v_t, p in terms:
        t = _dot(v_t, p)
        o = t if o is None else o + t
    den = o[HEAD_DIM:HEAD_DIM + 1, :]
    if sink_num is not None:
        den = den + sink_num
    return o[0:HEAD_DIM, :] / den


def _mla_key_plane(kn_t_pair, kr4_t, j, odd):
    h = 2 * j + odd
    lo = _row_lt(kn_t_pair.shape, HEAD_DIM)
    nope = jnp.where(lo, 0.0, kn_t_pair) if odd else jnp.where(lo, kn_t_pair, 0.0)
    rope = jnp.where(_row_group(kr4_t.shape, h * MLA_ROPE, (h + 1) * MLA_ROPE), kr4_t, 0.0)
    return jnp.concatenate([nope, rope], axis=0).astype(BF16)


def _ctx_mixer_kernel(sink_ref, x_ref, mod_ref, ga_ref, win_ref, wc_ref, gq_ref, wuq_ref, gkv_ref, wukt_ref, wuv_ref,
                      *rest, layer, first):
    y_ref, kg_ref, vg_ref, kn_ref, vn_ref, ckv_ref, kr_ref, h_s = rest[-8:]
    t = SEQ
    for b in range(CTX_SEQS):
        h_s[b * t:(b + 1) * t, :] = _modulated_norm(x_ref[b], ga_ref[...], mod_ref, 0, 1).astype(BF16)
    col_all = _projected_columns(h_s, win_ref)

    def put_state(ref, b, val):
        if first:
            for d in range(DEPTH):
                ref[b, d] = val if d == layer else jnp.zeros_like(val)
        else:
            ref[b] = val

    def conv(b, col):
        y_ref[b, :, Y_CONV:Y_CONV + CONV_WIDTH] = _short_conv(
            col(C_CB, CONV_WIDTH), col(C_CC, CONV_WIDTH), col(C_CV, CONV_WIDTH), wc_ref[...]).astype(BF16)

    def gqa(b, col):
        kg_t, vpair = col(C_GK, LANES).T, col(C_GV, LANES)
        put_state(kg_ref, b, kg_t)
        put_state(vg_ref, b, vpair.T)
        row2 = lax.broadcasted_iota(jnp.int32, (2 * t, 1), 0) < t
        for g in range(GQA_KV_HEADS):
            ke, ko = _key_planes(kg_t, g == 0, g == 0)
            ve, vo = _value_planes(vpair, g == 0, g == 0)
            q = jnp.concatenate([col(C_GQ + (2 * g) * LANES, LANES), col(C_GQ + (2 * g + 1) * LANES, LANES)], axis=0)
            q = (q * (ATTN_SCALE * LOG2E)).astype(BF16)
            s = _dot(q, jnp.concatenate([ke, ko], axis=1))
            sink = [jnp.where(row2, sink_ref[layer, 4 * g + odd], sink_ref[layer, 4 * g + 2 + odd]) * LOG2E
                    for odd in range(2)]
            (pe,), xe = _probabilities([s[:, 0:t]], sink[0])
            (po,), xo = _probabilities([s[:, t:2 * t]], sink[1])
            o = _attend([(pe, ve), (po, vo)], xe, xo).astype(BF16)
            c0 = Y_GQA + (2 * g) * LANES
            y_ref[b, :, c0:c0 + LANES] = o[0:t]
            y_ref[b, :, c0 + LANES:c0 + 2 * LANES] = o[t:2 * t]

    def na(b, col):
        kn_t = col(C_NK, 2 * LANES).T
        put_state(kn_ref, b, kn_t)
        put_state(vn_ref, b, col(C_NV, 2 * LANES).T)
        for j in range(NA_HEADS // 2):
            ke, ko = _key_planes(kn_t[j * LANES:(j + 1) * LANES, :], True, False)
            ve, vo = _value_planes(col(C_NV + j * LANES, LANES), True, False)
            q = (col(C_NQ + j * LANES, LANES) * (ATTN_SCALE * LOG2E)).astype(BF16)
            s = _dot(q, jnp.concatenate([ke, ko], axis=1))
            (pe,), _ = _probabilities([s[:, 0:t]])
            (po,), _ = _probabilities([s[:, t:2 * t]])
            y_ref[b, :, Y_NA + j * LANES:Y_NA + (j + 1) * LANES] = _attend([(pe, ve), (po, vo)]).astype(BF16)

    def mla(b, col):
        ckv = _rms(col(C_MKV, MLA_KV_RANK), gkv_ref[...])
        put_state(ckv_ref, b, ckv)
        kr_t = col(C_MKR, LANES).T[0:MLA_ROPE, :]
        put_state(kr_ref, b, kr_t)
        ckv_b = ckv.astype(BF16)
        q = _dot(_rms(col(C_MQ, MLA_Q_RANK), gq_ref[...]).astype(BF16), wuq_ref[...]) * (MLA_SCALE * LOG2E)
        kn_t_all = _dot_nt(wukt_ref[...], ckv_b)
        v_all = _dot(ckv_b, wuv_ref[...])
        kr4_t = jnp.concatenate([kr_t] * MLA_HEADS, axis=0)
        q_rope = q[:, 2 * LANES:3 * LANES]
        for j in range(MLA_HEADS // 2):
            qj = jnp.concatenate([q[:, j * LANES:(j + 1) * LANES], q_rope], axis=1).astype(BF16)
            kn_t_pair = kn_t_all[j * LANES:(j + 1) * LANES, :]
            keys = jnp.concatenate(
                [_mla_key_plane(kn_t_pair, kr4_t, j, 0), _mla_key_plane(kn_t_pair, kr4_t, j, 1)], axis=1)
            s = _dot(qj, keys)
            ve, vo = _value_planes(v_all[:, j * LANES:(j + 1) * LANES], True, False)
            (pe,), _ = _probabilities([s[:, 0:t]])
            (po,), _ = _probabilities([s[:, t:2 * t]])
            y_ref[b, :, Y_MLA + j * LANES:Y_MLA + (j + 1) * LANES] = _attend([(pe, ve), (po, vo)]).astype(BF16)

    for mixer in (conv, gqa, na, mla):
        for b in range(CTX_SEQS):
            mixer(b, lambda c, n, b=b: col_all(c, n)[b * t:(b + 1) * t])


def _ctx_mixer_call(x, mods, g_attn, w_main, sink, w_conv, g_q, w_uq, g_kv, w_uk_t, w_uv, prev_states, layer, name):
    n, nb = BATCH, CTX_SEQS
    state_shapes = ((2 * HEAD_DIM, SEQ), (2 * HEAD_DIM, SEQ), (4 * HEAD_DIM, SEQ), (4 * HEAD_DIM, SEQ),
                    (SEQ, MLA_KV_RANK), (MLA_ROPE, SEQ))
    first = prev_states is None
    if first:
        state_spec = lambda shape: pl.BlockSpec((nb, DEPTH) + shape, lambda s: (s, 0, 0, 0))
    else:
        state_spec = lambda shape: pl.BlockSpec((nb, None) + shape, lambda s: (s, layer, 0, 0))
    in_specs = [
        pl.BlockSpec(memory_space=pltpu.SMEM),
        pl.BlockSpec((nb, SEQ, D_MODEL), lambda s: (s, 0, 0)),
        _mod_spec(layer, True, 1),
        _layer_spec((1, D_MODEL), layer, 1),
        _layer_spec((MAIN_COLS, D_MODEL), layer, 1),
        _layer_spec((CONV_K, CONV_WIDTH), layer, 1),
        _layer_spec((1, MLA_Q_RANK), layer, 1),
        _layer_spec((MLA_Q_RANK, 3 * LANES), layer, 1),
        _layer_spec((1, MLA_KV_RANK), layer, 1),
        _layer_spec((2 * LANES, MLA_KV_RANK), layer, 1),
        _layer_spec((MLA_KV_RANK, 2 * LANES), layer, 1),
    ]
    args = [sink, x, mods, g_attn, w_main, w_conv, g_q, w_uq, g_kv, w_uk_t, w_uv]
    aliases = {}
    if not first:
        for i, st in enumerate(prev_states):
            aliases[len(args)] = 1 + i
            in_specs.append(pl.BlockSpec(memory_space=pl.ANY))
            args.append(st)
    outs = pl.pallas_call(
        functools.partial(_ctx_mixer_kernel, layer=layer, first=first),
        grid=(n // nb,),
        in_specs=in_specs,
        out_specs=[pl.BlockSpec((nb, SEQ, Y_COLS), lambda s: (s, 0, 0))] + [state_spec(s) for s in state_shapes],
        out_shape=[jax.ShapeDtypeStruct((n, SEQ, Y_COLS), BF16)]
        + [jax.ShapeDtypeStruct((n, DEPTH) + s, F32) for s in state_shapes],
        input_output_aliases=aliases,
        scratch_shapes=[pltpu.VMEM((nb * SEQ, D_MODEL), BF16)],
        compiler_params=pltpu.CompilerParams(
            dimension_semantics=("arbitrary",), vmem_limit_bytes=VMEM_LIMIT),
        name=name,
    )(*args)
    return outs[0], outs[1:]


def _lat_mixer_kernel(sink_ref, x_ref, mod_ref, ga_ref, win_ref, wc_ref, gq_ref, wuq_ref, gkv_ref, wuk_ref, wuvt_ref,
                      rc_ref, rsh_ref, rsl_ref, mc_ref, msh_ref, msl_ref, band_ref, nab_ref,
                      cgk_ref, cgv_ref, cnk_ref, cnv_ref, cckv_ref, ckr_ref,
                      y_ref,
                      h_s, gq_s, gk_s, gv_s, gkc_s, gvc_s, nq_s, nk_s, nv_s, nkc_s, nvc_s, mq_s, mk_s, mv_s, *, layer):
    t = DEC_SEQ
    h_s[...] = _modulated_norm(x_ref[...], ga_ref[...], mod_ref, 0, 1).astype(BF16)
    col = _projected_columns(h_s, win_ref)
    rope64 = lambda x: _rope(x, rc_ref[...], rsh_ref[...], rsl_ref[...], HEAD_DIM // 4)
    rope32 = lambda x: _rope(x, mc_ref[...], msh_ref[...], msl_ref[...], MLA_ROPE // 4)

    y_ref[:, Y_CONV:Y_CONV + CONV_WIDTH] = _short_conv(
        col(C_CB, CONV_WIDTH), col(C_CC, CONV_WIDTH), col(C_CV, CONV_WIDTH), wc_ref[...]).astype(BF16)

    group = GQA_HEADS // GQA_KV_HEADS
    for j in range(GQA_HEADS // 2):
        q_pair = rope64(col(C_GQ + j * LANES, LANES)) * (ATTN_SCALE * LOG2E)
        gq_s[j * LANES:(j + 1) * LANES, :] = q_pair.T.astype(BF16)
    gk_s[...] = rope64(col(C_GK, LANES)).astype(BF16)
    gkc_s[...] = cgk_ref[...].T.astype(BF16)
    v_t = col(C_GV, LANES).T
    for g in range(GQA_KV_HEADS):
        gv_s[g] = _value_plane_t(v_t[g * HEAD_DIM:(g + 1) * HEAD_DIM, :])
        gvc_s[g] = _value_plane_t(cgv_ref[g * HEAD_DIM:(g + 1) * HEAD_DIM, :])

    def gqa_block(b, c_lo, c_hi):
        q0 = pl.multiple_of(b * BAND_BLOCK, BAND_BLOCK)
        k0 = pl.multiple_of(q0 + (c_lo - WINDOW), BAND_BLOCK)
        n = c_hi - c_lo
        band = band_ref[c_lo:c_hi, :]
        lane_head = lax.shift_right_logical(
            lax.broadcasted_iota(jnp.int32, (1, group * BAND_BLOCK), 1), BAND_BLOCK.bit_length() - 1)
        zeros = jnp.zeros((HEAD_DIM, group * BAND_BLOCK), BF16)
        for g in range(GQA_KV_HEADS):
            q_t = jnp.concatenate(
                [gq_s[(group * g + h) * HEAD_DIM:(group * g + h + 1) * HEAD_DIM, pl.ds(q0, BAND_BLOCK)]
                 for h in range(group)], axis=1)
            q_t = jnp.concatenate([q_t, zeros] if g == 0 else [zeros, q_t], axis=0)
            s_loc = _dot(gk_s[pl.ds(k0, n), :], q_t) + band
            s_ctx = _dot(gkc_s[...], q_t)
            sink = sink_ref[layer, group * g + group - 1]
            for h in range(group - 2, -1, -1):
                sink = jnp.where(lane_head == h, sink_ref[layer, group * g + h], sink)
            (p_loc, p_ctx), x = _probabilities_t([s_loc, s_ctx], sink * LOG2E)
            o = _attend_t([(gv_s[g, :, pl.ds(k0, n)], p_loc), (gvc_s[g], p_ctx)], x)
            for pr in range(group // 2):
                pair = jnp.concatenate([o[:, (2 * pr) * BAND_BLOCK:(2 * pr + 1) * BAND_BLOCK],
                                        o[:, (2 * pr + 1) * BAND_BLOCK:(2 * pr + 2) * BAND_BLOCK]], axis=0)
                c0 = Y_GQA + (group // 2 * g + pr) * LANES
                y_ref[pl.ds(q0, BAND_BLOCK), c0:c0 + LANES] = pair.T.astype(BF16)

    for j in range(NA_HEADS // 2):
        nq_s[:, j * LANES:(j + 1) * LANES] = (col(C_NQ + j * LANES, LANES) * (ATTN_SCALE * LOG2E)).astype(BF16)
        nk_s[2 * j], nk_s[2 * j + 1] = _key_planes(col(C_NK + j * LANES, LANES).T, True, False)
        nv_s[2 * j], nv_s[2 * j + 1] = _value_planes(col(C_NV + j * LANES, LANES), True, False)
        nkc_s[j] = jnp.concatenate(_key_planes(cnk_ref[j * LANES:(j + 1) * LANES, :], True, False), axis=1)
        nvc_s[2 * j], nvc_s[2 * j + 1] = _value_planes(cnv_ref[j * LANES:(j + 1) * LANES, :].T, True, False)

    na_kinds, na_kind_of = _na_group_kinds()
    na_width = _na_table_width()

    def na_group(k):
        lo, n = _na_group_slab(k)
        q0, nq = k * NA_GROUP_ROWS * GRID_W, NA_GROUP_ROWS * GRID_W
        k0, nk = lo * GRID_W, n * GRID_W
        b0 = sum(kn for kn, _ in na_kinds[:na_kind_of[k]]) * GRID_W
        for j in range(NA_HEADS // 2):
            q = nq_s[q0:q0 + nq, j * LANES:(j + 1) * LANES]
            s_ctx = _dot(q, nkc_s[j])
            terms = []
            for odd in range(2):
                h = 2 * j + odd
                s_loc = _dot(q, nk_s[h, :, k0:k0 + nk]) + nab_ref[:, h * na_width + b0:h * na_width + b0 + nk]
                (p_loc, p_ctx), _ = _probabilities([s_loc, s_ctx[:, odd * PAST_LEN:(odd + 1) * PAST_LEN]])
                terms += [(p_loc, nv_s[h, k0:k0 + nk, :]), (p_ctx, nvc_s[h])]
            y_ref[q0:q0 + nq, Y_NA + j * LANES:Y_NA + (j + 1) * LANES] = _attend(terms).astype(BF16)

    ckv_b = _rms(col(C_MKV, MLA_KV_RANK), gkv_ref[...]).astype(BF16)
    cckv_b = cckv_ref[...].astype(BF16)
    q = _dot(_rms(col(C_MQ, MLA_Q_RANK), gq_ref[...]).astype(BF16), wuq_ref[...]) * (MLA_SCALE * LOG2E)
    for i in range(3):
        tile = q[:, i * LANES:(i + 1) * LANES]
        mq_s[i * LANES:(i + 1) * LANES, :] = (rope32(tile) if i == 2 else tile).T.astype(BF16)
    kr = jnp.where(_lane_lt((t, LANES), MLA_ROPE), rope32(col(C_MKR, LANES)), 0.0)
    kr_c = jnp.concatenate([ckr_ref[...], jnp.zeros((LANES - MLA_ROPE, PAST_LEN), F32)], axis=0).T
    for rows, ckv_x, kr_x in ((slice(0, t), ckv_b, kr), (slice(t, t + PAST_LEN), cckv_b, kr_c)):
        kn_all = _dot(ckv_x, wuk_ref[...])
        v_t_all = _dot_nt(wuvt_ref[...], ckv_x)
        for j in range(MLA_HEADS // 2):
            mk_s[j, rows, :] = jnp.concatenate([kn_all[:, j * LANES:(j + 1) * LANES], kr_x], axis=1).astype(BF16)
        for h in range(MLA_HEADS):
            mv_s[h, :, rows] = _value_plane_t(v_t_all[h * MLA_V:(h + 1) * MLA_V, :])

    tq = 256

    def mla_block(i, carry):
        q0 = pl.multiple_of(i * tq, tq)
        zeros = lambda r: jnp.zeros((r, tq), BF16)
        q_nope = lambda h: mq_s[h * MLA_NOPE:(h + 1) * MLA_NOPE, pl.ds(q0, tq)]
        q_rope = lambda h: mq_s[MLA_HEADS * MLA_NOPE + h * MLA_ROPE:MLA_HEADS * MLA_NOPE + (h + 1) * MLA_ROPE,
                                pl.ds(q0, tq)]
        pad = 2 * LANES - 2 * MLA_NOPE - MLA_ROPE
        for j in range(MLA_HEADS // 2):
            he, ho = 2 * j, 2 * j + 1
            q_e = jnp.concatenate([q_nope(he), zeros(MLA_NOPE), q_rope(he), zeros(pad)], axis=0)
            q_o = jnp.concatenate([zeros(MLA_NOPE), q_nope(ho), q_rope(ho), zeros(pad)], axis=0)
            s = _dot(mk_s[j], jnp.concatenate([q_e, q_o], axis=1))
            outs = []
            for odd in range(2):
                (p,), _ = _probabilities_t([s[:, odd * tq:(odd + 1) * tq]])
                outs.append(_attend_t([(mv_s[2 * j + odd], p)]))
            y_ref[pl.ds(q0, tq), Y_MLA + j * LANES:Y_MLA + (j + 1) * LANES] = (
                jnp.concatenate(outs, axis=0).T.astype(BF16))
        return carry

    span = BAND_BLOCK + 2 * WINDOW
    nb = t // BAND_BLOCK
    for i in range(nb):
        gqa_block(jnp.int32(i), WINDOW if i == 0 else 0, span - WINDOW if i == nb - 1 else span)
        if i % (nb // NA_GROUPS) == 0:
            na_group(i // (nb // NA_GROUPS))
        if i % (nb * tq // t) == 0:
            mla_block(jnp.int32(i // (nb * tq // t)), 0)


def _rope_tables(group, half):
    tok = np.arange(DEC_SEQ)
    pos = np.stack([tok // GRID_W, tok % GRID_W], axis=1).astype(np.float64)
    inv = ROPE_BASE ** (-np.arange(half, dtype=np.float64) / half)
    lane = np.arange(LANES) % group
    axis = lane // (2 * half)
    within = lane % (2 * half)
    ang = pos[:, axis] * inv[within % half][None, :]
    cos, sin = np.cos(ang), np.sin(ang)
    upper = (within >= half)[None, :]
    sin_hi = np.where(upper, sin, 0.0)
    sin_lo = np.where(upper, 0.0, -sin)
    return tuple(jnp.asarray(a, dtype=F32) for a in (cos, sin_hi, sin_lo))


def _band_mask():
    c = np.arange(BAND_BLOCK + 2 * WINDOW)[:, None]
    i = np.arange(BAND_BLOCK)[None, :]
    ok = (c >= i) & (c <= i + 2 * WINDOW)
    m = np.where(ok, 0.0, NEG_INF)
    return jnp.asarray(np.concatenate([m] * (GQA_HEADS // GQA_KV_HEADS), axis=1), dtype=F32)


def _lat_mixer_call(x, mods, g_attn, w_main, sink, w_conv, g_q, w_uq, g_kv, w_uk, w_uv_t, nab, caches, layer, name):
    n, t = DEC_BATCH, DEC_SEQ
    one = pl.Buffered(1)
    const = lambda shape: pl.BlockSpec(shape, lambda s: (0,) * len(shape), pipeline_mode=one)
    cache = lambda shape: pl.BlockSpec((None, None) + shape, lambda s: (s, layer, 0, 0), pipeline_mode=one)
    rope_g = _rope_tables(HEAD_DIM, HEAD_DIM // 4)
    rope_m = _rope_tables(MLA_ROPE, MLA_ROPE // 4)
    span = BAND_BLOCK + 2 * WINDOW
    keys = t + PAST_LEN
    scratch = [
        pltpu.VMEM((t, D_MODEL), BF16),
        pltpu.VMEM((GQA_HEADS * HEAD_DIM, t), BF16),
        pltpu.VMEM((t, LANES), BF16),
        pltpu.VMEM((GQA_KV_HEADS, HEAD_DIM + ONES_ROWS, t), BF16),
        pltpu.VMEM((PAST_LEN, LANES), BF16),
        pltpu.VMEM((GQA_KV_HEADS, HEAD_DIM + ONES_ROWS, PAST_LEN), BF16),
        pltpu.VMEM((t, NA_HEADS // 2 * LANES), BF16),
        pltpu.VMEM((NA_HEADS, LANES, t), BF16),
        pltpu.VMEM((NA_HEADS, t, 2 * LANES), BF16),
        pltpu.VMEM((NA_HEADS // 2, LANES, 2 * PAST_LEN), BF16),
        pltpu.VMEM((NA_HEADS, PAST_LEN, 2 * LANES), BF16),
        pltpu.VMEM((3 * LANES, t), BF16),
        pltpu.VMEM((MLA_HEADS // 2, keys, 2 * LANES), BF16),
        pltpu.VMEM((MLA_HEADS, MLA_V + ONES_ROWS, keys), BF16),
    ]
    return pl.pallas_call(
        functools.partial(_lat_mixer_kernel, layer=layer),
        grid=(n,),
        in_specs=[
            pl.BlockSpec(memory_space=pltpu.SMEM),
            pl.BlockSpec((None, t, D_MODEL), lambda s: (s, 0, 0), pipeline_mode=one),
            _mod_spec(layer, False, 1),
            _layer_spec((1, D_MODEL), layer, 1),
            _layer_spec((MAIN_COLS, D_MODEL), layer, 1),
            _layer_spec((CONV_K, CONV_WIDTH), layer, 1),
            _layer_spec((1, MLA_Q_RANK), layer, 1),
            _layer_spec((MLA_Q_RANK, 3 * LANES), layer, 1),
            _layer_spec((1, MLA_KV_RANK), layer, 1),
            _layer_spec((MLA_KV_RANK, 2 * LANES), layer, 1),
            _layer_spec((2 * LANES, MLA_KV_RANK), layer, 1),
        ] + [const((t, LANES))] * 6 + [
            const((span, GQA_HEADS // GQA_KV_HEADS * BAND_BLOCK)),
            _layer_spec((NA_GROUP_ROWS * GRID_W, NA_HEADS * _na_table_width()), layer, 1),
            cache((2 * HEAD_DIM, PAST_LEN)), cache((2 * HEAD_DIM, PAST_LEN)),
            cache((4 * HEAD_DIM, PAST_LEN)), cache((4 * HEAD_DIM, PAST_LEN)),
            cache((PAST_LEN, MLA_KV_RANK)), cache((MLA_ROPE, PAST_LEN)),
        ],
        out_specs=pl.BlockSpec((None, t, Y_COLS), lambda s: (s, 0, 0)),
        out_shape=jax.ShapeDtypeStruct((n, t, Y_COLS), BF16),
        scratch_shapes=scratch,
        compiler_params=pltpu.CompilerParams(
            dimension_semantics=("arbitrary",), vmem_limit_bytes=VMEM_LIMIT),
        name=name,
    )(sink, x, mods, g_attn, w_main, w_conv, g_q, w_uq, g_kv, w_uk, w_uv_t, *rope_g, *rope_m, _band_mask(), nab, *caches)


def _post_kernel(x_ref, y_ref, mod_ref, ga_ref, gm_ref, gf_ref, wg_ref, wb_ref, wo_ref, w1_ref, w2_ref,
                 o_ref, *, final):
    d = D_MODEL
    mod = lambda i: mod_ref[:, i * d:(i + 1) * d]
    bounds = (Y_CONV, Y_GQA, Y_NA, Y_MLA, Y_COLS)
    for r0 in range(0, x_ref.shape[0], POST_CHAIN_ROWS):
        rows = slice(r0, r0 + POST_CHAIN_ROWS)
        x = x_ref[rows, :]
        h = _modulated_norm(x, ga_ref[...], mod_ref, 0, 1).astype(BF16)
        merged = None
        for i in range(N_BRANCH):
            lo, hi = bounds[i], bounds[i + 1]
            gate = jax.nn.sigmoid(_dot_nt(h, wg_ref[0, i * d:(i + 1) * d, :]))
            term = gate * _dot(y_ref[rows, lo:hi], wb_ref[lo:hi, :])
            merged = term if merged is None else merged + term
        x = x + mod(2) * _dot(merged.astype(BF16), wo_ref[...])
        h = _modulated_norm(x, gm_ref[...], mod_ref, 3, 4).astype(BF16)
        mlp = None
        for c0 in range(0, D_FF, d):
            f = jnp.square(jnp.maximum(_dot(h, w1_ref[:, c0:c0 + d]), 0.0)).astype(BF16)
            term = _dot(f, w2_ref[c0:c0 + d, :])
            mlp = term if mlp is None else mlp + term
        x = x + mod(5) * mlp
        o_ref[rows, :] = _rms(x, gf_ref[...]) if final else x


def _post_call(x, y, mods, g_attn, g_mlp, g_final, w_gates, w_br, w_o, w_ff1, w_ff2, layer, ctx, name):
    n, t, _ = x.shape
    tm = 2 * POST_CHAIN_ROWS
    return pl.pallas_call(
        functools.partial(_post_kernel, final=layer == DEPTH - 1),
        grid=(n, t // tm),
        in_specs=[
            pl.BlockSpec((None, tm, D_MODEL), lambda s, i: (s, i, 0)),
            pl.BlockSpec((None, tm, Y_COLS), lambda s, i: (s, i, 0)),
            _mod_spec(layer, ctx, 2),
            _layer_spec((1, D_MODEL), layer, 2),
            _layer_spec((1, D_MODEL), layer, 2),
            pl.BlockSpec((1, D_MODEL), lambda s, i: (0, 0)),
            pl.BlockSpec((pl.Element(1), pl.Element(N_BRANCH * D_MODEL), pl.Element(D_MODEL)),
                         lambda s, i: (layer, GATE_COL0, 0), pipeline_mode=pl.Buffered(1)),
            _layer_spec((Y_COLS, D_MODEL), layer, 2),
            _layer_spec((D_MODEL, D_MODEL), layer, 2),
            _layer_spec((D_MODEL, D_FF), layer, 2),
            _layer_spec((D_FF, D_MODEL), layer, 2),
        ],
        out_specs=pl.BlockSpec((None, tm, D_MODEL), lambda s, i: (s, i, 0)),
        out_shape=jax.ShapeDtypeStruct((n, t, D_MODEL), F32),
        compiler_params=pltpu.CompilerParams(
            dimension_semantics=("arbitrary", "arbitrary"), vmem_limit_bytes=VMEM_LIMIT),
        name=name,
    )(x, y, mods, g_attn, g_mlp, g_final, w_gates, w_br, w_o, w_ff1, w_ff2)


def kernel(x_prompt, x_sample, cache_gqa_k, cache_gqa_v, cache_na_k, cache_na_v, cache_mla_ckv, cache_mla_krope, c, c_ctx, w_mod, b_mod, g_attn, g_mlp, w_in, w_conv, gqa_sink, na_rpb, mla_g_q, mla_w_uq, mla_g_kv, mla_w_ukv, w_branch_conv, w_branch_gqa, w_branch_na, w_branch_mla, w_o, w_ff1, w_ff2, g_final):
    w_main = w_gates = jnp.transpose(w_in, (0, 2, 1)).astype(BF16)
    uq = mla_w_uq.reshape(DEPTH, MLA_Q_RANK, MLA_HEADS, MLA_NOPE + MLA_ROPE)
    w_uq = jnp.concatenate([uq[..., :MLA_NOPE].reshape(DEPTH, MLA_Q_RANK, -1),
                            uq[..., MLA_NOPE:].reshape(DEPTH, MLA_Q_RANK, -1)], axis=-1).astype(BF16)
    ukv = mla_w_ukv.reshape(DEPTH, MLA_KV_RANK, MLA_HEADS, MLA_NOPE + MLA_V)
    w_uk = ukv[..., :MLA_NOPE].reshape(DEPTH, MLA_KV_RANK, -1).astype(BF16)
    w_uv = ukv[..., MLA_NOPE:].reshape(DEPTH, MLA_KV_RANK, -1).astype(BF16)
    w_uk_t, w_uv_t = jnp.transpose(w_uk, (0, 2, 1)), jnp.transpose(w_uv, (0, 2, 1))
    w_br = jnp.concatenate([w_branch_conv, w_branch_gqa, w_branch_na, w_branch_mla], axis=1).astype(BF16)
    w_o_b, w_ff1_b, w_ff2_b = w_o.astype(BF16), w_ff1.astype(BF16), w_ff2.astype(BF16)
    g_a, g_m = g_attn[:, None, :], g_mlp[:, None, :]
    g_q, g_kv, g_f = mla_g_q[:, None, :], mla_g_kv[:, None, :], g_final[None, :]

    c16 = jnp.concatenate([c, c_ctx[None, :], jnp.zeros((MOD_ROWS - DEC_BATCH - 1, D_MODEL), F32)], axis=0)
    mods = _mod_call(c16, w_mod, b_mod).reshape(DEPTH, MOD_ROWS, 1, 6 * D_MODEL)
    nab = _nab_call(na_rpb)

    heads_t = lambda a: jnp.transpose(a, (0, 1, 3, 4, 2)).reshape(a.shape[0], DEPTH, -1, a.shape[2])
    caches = (heads_t(cache_gqa_k), heads_t(cache_gqa_v), heads_t(cache_na_k), heads_t(cache_na_v),
              cache_mla_ckv, jnp.transpose(cache_mla_krope, (0, 1, 3, 2)))

    h_ctx, h_lat = x_prompt, x_sample
    states = None
    for l in range(DEPTH):
        mixer_w = (gqa_sink, w_conv, g_q, w_uq, g_kv)
        post_w = (g_a, g_m, g_f, w_gates, w_br, w_o_b, w_ff1_b, w_ff2_b)

        y, states = _ctx_mixer_call(h_ctx, mods, g_a, w_main, *mixer_w, w_uk_t, w_uv, states, l, f"mixer_ctx_{l}")
        flat = lambda a: a.reshape(1, BATCH * SEQ, a.shape[-1])
        h_ctx = _post_call(flat(h_ctx), flat(y), mods, *post_w, l, True, f"post_ctx_{l}").reshape(BATCH, SEQ, D_MODEL)

        y = _lat_mixer_call(h_lat, mods, g_a, w_main, *mixer_w, w_uk, w_uv_t, nab, caches, l, f"mixer_lat_{l}")
        h_lat = _post_call(h_lat, y, mods, *post_w, l, False, f"post_lat_{l}")

    def heads_out(a, heads):
        return jnp.transpose(a.reshape(BATCH, DEPTH, heads, HEAD_DIM, SEQ), (0, 1, 4, 2, 3))

    kg, vg, kn, vn, ckv, kr = states
    return (h_ctx, h_lat, heads_out(kg, GQA_KV_HEADS), heads_out(vg, GQA_KV_HEADS),
            heads_out(kn, NA_HEADS), heads_out(vn, NA_HEADS), ckv, jnp.transpose(kr, (0, 1, 3, 2)))
```

```python
import functools
import math

import numpy as np
import jax
import jax.numpy as jnp
from jax import lax
from jax.experimental import pallas as pl
from jax.experimental.pallas import tpu as pltpu

D_MODEL = 1024
BATCH = 32
SEQ = 256
DEPTH = 2
DEC_BATCH = 8
DEC_SEQ = 1024
PAST_LEN = 512
GRID_W = 64
GRID_ROWS = DEC_SEQ // GRID_W
HEAD_DIM = 64
CONV_WIDTH = 256
CONV_K = 3
GQA_HEADS = 8
GQA_KV_HEADS = 2
WINDOW = 128
BAND_BLOCK = 128
NA_HEADS = 4
NA_WIN_H = 8
NA_WIN_W = 16
MLA_HEADS = 4
MLA_Q_RANK = 256
MLA_KV_RANK = 128
MLA_NOPE = 64
MLA_ROPE = 32
MLA_V = 64
D_FF = 4 * D_MODEL
N_BRANCH = 4
ROPE_BASE = 10000.0
EPS = 1e-6
NEG_INF = -1e30
LOG2E = math.log2(math.e)
ATTN_SCALE = HEAD_DIM ** -0.5
MLA_SCALE = (MLA_NOPE + MLA_ROPE) ** -0.5

LANES = 128
MOD_ROWS = 16
CTX_MOD_ROW = DEC_BATCH

C_CB, C_CC, C_CV = 0, 256, 512
C_GQ, C_GK, C_GV = 768, 1280, 1408
C_NQ, C_NK, C_NV = 1536, 1792, 2048
C_MQ, C_MKV, C_MKR = 2304, 2560, 2688
MAIN_COLS = 2816
GATE_COL0 = 2720
Y_CONV, Y_GQA, Y_NA, Y_MLA = 0, 256, 768, 1024
Y_COLS = 1280
NA_GROUPS = 4
NA_GROUP_ROWS = GRID_ROWS // NA_GROUPS

VMEM_LIMIT = 56 * 1024 * 1024
POST_CHAIN_ROWS = 256
CTX_SEQS = 2

F32 = jnp.float32
BF16 = jnp.bfloat16


def _dot(a, b):
    return jnp.dot(a, b, preferred_element_type=F32)


def _dot_nt(a, b):
    return lax.dot_general(a, b, (((1,), (1,)), ((), ())), preferred_element_type=F32)


def _rms(x, g):
    return x * lax.rsqrt(jnp.mean(x * x, axis=-1, keepdims=True) + EPS) * g


def _lane_lt(shape, n):
    return lax.broadcasted_iota(jnp.int32, shape, len(shape) - 1) < n


def _row_lt(shape, n):
    return lax.broadcasted_iota(jnp.int32, shape, 0) < n


def _row_group(shape, lo, hi):
    row = lax.broadcasted_iota(jnp.int32, shape, 0)
    return (row >= lo) & (row < hi)


def _short_conv(cb, cc, cv, w):
    u = cc * cv
    t = u.shape[0]
    row = lax.broadcasted_iota(jnp.int32, u.shape, 0)
    prev = jnp.where(row == 0, 0.0, pltpu.roll(u, 1, 0))
    nxt = jnp.where(row == t - 1, 0.0, pltpu.roll(u, t - 1, 0))
    return cb * (prev * w[0:1, :] + u * w[1:2, :] + nxt * w[2:3, :])


def _rope(x, cos, sin_hi, sin_lo, half):
    n = x.shape[-1]
    return x * cos + pltpu.roll(x, n - half, 1) * sin_lo + pltpu.roll(x, half, 1) * sin_hi


def _repeat_rope_key(tile):
    k = jnp.where(_lane_lt(tile.shape, MLA_ROPE), tile, 0.0)
    k = k + pltpu.roll(k, MLA_ROPE, 1)
    return k + pltpu.roll(k, 2 * MLA_ROPE, 1)


def _mod_kernel(c_ref, w_ref, b_ref, o_ref):
    c = c_ref[...]
    s = c * jax.nn.sigmoid(c)
    o_ref[...] = _dot(s.astype(BF16), w_ref[...].astype(BF16)) + b_ref[...]


def _mod_call(c16, w_mod, b_mod):
    tn = 3072
    return pl.pallas_call(
        _mod_kernel,
        grid=(DEPTH, 6 * D_MODEL // tn),
        in_specs=[
            pl.BlockSpec((MOD_ROWS, D_MODEL), lambda l, j: (0, 0)),
            pl.BlockSpec((None, D_MODEL, tn), lambda l, j: (l, 0, j)),
            pl.BlockSpec((None, 1, tn), lambda l, j: (l, 0, j)),
        ],
        out_specs=pl.BlockSpec((None, MOD_ROWS, tn), lambda l, j: (l, 0, j)),
        out_shape=jax.ShapeDtypeStruct((DEPTH, MOD_ROWS, 6 * D_MODEL), F32),
        compiler_params=pltpu.CompilerParams(
            dimension_semantics=("arbitrary", "arbitrary"), vmem_limit_bytes=VMEM_LIMIT),
        name="adaln_mod",
    )(c16, w_mod, b_mod.reshape(DEPTH, 1, 6 * D_MODEL))


def _na_window_start(r):
    return min(max(r - NA_WIN_H // 2, 0), GRID_ROWS - NA_WIN_H)


def _na_group_slab(k):
    starts = [_na_window_start(r) for r in range(k * NA_GROUP_ROWS, (k + 1) * NA_GROUP_ROWS)]
    lo = min(starts) // 2 * 2
    n = -(-(max(starts) + NA_WIN_H - lo) // 4) * 4
    assert lo + n <= GRID_ROWS
    return lo, n


def _na_group_kinds():
    kinds, of_group = [], []
    for k in range(NA_GROUPS):
        lo, n = _na_group_slab(k)
        layout = {}
        for dr in range(NA_GROUP_ROWS):
            r = k * NA_GROUP_ROWS + dr
            r0 = _na_window_start(r)
            for a in range(n):
                inside = r0 <= lo + a < r0 + NA_WIN_H
                layout[(dr, a)] = lo + a - r + NA_WIN_H - 1 if inside else None
        if (n, layout) not in kinds:
            kinds.append((n, layout))
        of_group.append(kinds.index((n, layout)))
    return kinds, of_group


def _na_table_width():
    return sum(n for n, _ in _na_group_kinds()[0]) * GRID_W


def _nab_kernel(rpb_ref, o_ref):
    l = pl.program_id(0)
    c = lax.broadcasted_iota(jnp.int32, (GRID_W, GRID_W), 0)
    w = lax.broadcasted_iota(jnp.int32, (GRID_W, GRID_W), 1)
    dc = w - c + (NA_WIN_W - 1)
    c0 = jnp.clip(c - NA_WIN_W // 2, 0, GRID_W - NA_WIN_W)
    outside = (w < c0) | (w >= c0 + NA_WIN_W)
    n_dr, n_dc = 2 * NA_WIN_H - 1, 2 * NA_WIN_W - 1
    kinds, _ = _na_group_kinds()
    width = _na_table_width()

    def put(h, d, tile):
        col0 = h * width
        for n, layout in kinds:
            for (dr, a), want in layout.items():
                if want == d:
                    o_ref[dr * GRID_W:(dr + 1) * GRID_W, col0 + a * GRID_W:col0 + (a + 1) * GRID_W] = tile
            col0 += n * GRID_W

    for h in range(NA_HEADS):
        put(h, None, jnp.full((GRID_W, GRID_W), NEG_INF, F32))
        for d in range(n_dr):
            base = ((l * NA_HEADS + h) * n_dr + d) * n_dc
            tile = jnp.full((GRID_W, GRID_W), NEG_INF, F32)
            for j in range(n_dc):
                tile = jnp.where(dc == j, rpb_ref[base + j] * LOG2E, tile)
            put(h, d, jnp.where(outside, NEG_INF, tile))


def _nab_call(na_rpb):
    shape = (NA_GROUP_ROWS * GRID_W, NA_HEADS * _na_table_width())
    return pl.pallas_call(
        _nab_kernel,
        grid=(DEPTH,),
        in_specs=[pl.BlockSpec(memory_space=pltpu.SMEM)],
        out_specs=pl.BlockSpec((None,) + shape, lambda l: (l, 0, 0)),
        out_shape=jax.ShapeDtypeStruct((DEPTH,) + shape, F32),
        compiler_params=pltpu.CompilerParams(dimension_semantics=("arbitrary",), vmem_limit_bytes=VMEM_LIMIT),
        name="na_bias_tables",
    )(na_rpb.reshape(-1))


def _mod_spec(layer, ctx, n_grid):
    if n_grid == 1:
        index = (lambda s: (layer, CTX_MOD_ROW, 0, 0)) if ctx else (lambda s: (layer, s, 0, 0))
    else:
        index = (lambda s, i: (layer, CTX_MOD_ROW, 0, 0)) if ctx else (lambda s, i: (layer, s, 0, 0))
    return pl.BlockSpec((None, None, 1, 6 * D_MODEL), index)


def _layer_spec(shape, layer, n_grid):
    zeros = (0,) * len(shape)
    index = (lambda s: (layer,) + zeros) if n_grid == 1 else (lambda s, i: (layer,) + zeros)
    return pl.BlockSpec((None,) + tuple(shape), index, pipeline_mode=pl.Buffered(1))


_COLUMN_GROUPS = ((C_CB, C_GQ), (C_GQ, C_NQ), (C_NQ, C_MQ), (C_MQ, MAIN_COLS))


def _modulated_norm(x, g, mod_ref, shift, scale):
    d = D_MODEL
    return _rms(x, g) * (1.0 + mod_ref[:, scale * d:(scale + 1) * d]) + mod_ref[:, shift * d:(shift + 1) * d]


def _projected_columns(h_ref, w_ref):
    cache = {}

    def col(c, n):
        lo, hi = next(g for g in _COLUMN_GROUPS if g[0] <= c and c + n <= g[1])
        if lo not in cache:
            cache.clear()
            cache[lo] = _dot_nt(h_ref[...], w_ref[lo:hi, :])
        return cache[lo][:, c - lo:c - lo + n]

    return col


def _key_planes(k_t, even_first, odd_first):
    lo = _row_lt(k_t.shape, HEAD_DIM)
    swapped = None
    if not even_first or odd_first:
        swapped = pltpu.roll(k_t, HEAD_DIM, 0)
    top = jnp.where(lo, k_t if even_first else swapped, 0.0)
    bot = jnp.where(lo, 0.0, swapped if odd_first else k_t)
    return top.astype(BF16), bot.astype(BF16)


def _value_planes(pair, even_first, odd_first):
    lo = _lane_lt(pair.shape, HEAD_DIM)
    swapped = None
    if not even_first or odd_first:
        swapped = pltpu.roll(pair, HEAD_DIM, 1)
    top = jnp.where(lo, pair if even_first else swapped, 0.0)
    bot = jnp.where(lo, 0.0, swapped if odd_first else pair)
    one_e = jnp.where(lo, 1.0, 0.0)
    return (jnp.concatenate([top, one_e], axis=1).astype(BF16),
            jnp.concatenate([bot, 1.0 - one_e], axis=1).astype(BF16))


def _probabilities(parts, sink=None):
    m = parts[0].max(axis=-1, keepdims=True)
    for p in parts[1:]:
        m = jnp.maximum(m, p.max(axis=-1, keepdims=True))
    if sink is not None:
        m = jnp.maximum(m, sink)
    probs = [jnp.exp2((p - m).astype(BF16)) for p in parts]
    return probs, (None if sink is None else jnp.exp2(sink - m))


def _attend(terms, sink_e=None, sink_o=None):
    o = None
    for p, v in terms:
        t = _dot(p, v)
        o = t if o is None else o + t
    den = o[:, LANES:]
    if sink_e is not None:
        den = den + jnp.where(_lane_lt(den.shape, HEAD_DIM), sink_e, sink_o)
    return o[:, :LANES] / den


ONES_ROWS = 16


def _value_plane_t(v_t):
    return jnp.concatenate([v_t, jnp.ones((ONES_ROWS, v_t.shape[1]), v_t.dtype)], axis=0).astype(BF16)


def _probabilities_t(parts, sink=None):
    m = parts[0].max(axis=0, keepdims=True)
    for p in parts[1:]:
        m = jnp.maximum(m, p.max(axis=0, keepdims=True))
    if sink is not None:
        m = jnp.maximum(m, sink)
    probs = [jnp.exp2((p - m).astype(BF16)) for p in parts]
    return probs, (None if sink is None else jnp.exp2(sink - m))


def _attend_t(terms, sink_num=None):
    o = None
    for v_t, p in terms:
        t = _dot(v_t, p)
        o = t if o is None else o + t
    den = o[HEAD_DIM:HEAD_DIM + 1, :]
    if sink_num is not None:
        den = den + sink_num
    return o[0:HEAD_DIM, :] / den


def _mla_key_plane(kn_t_pair, kr4_t, j, odd):
    h = 2 * j + odd
    lo = _row_lt(kn_t_pair.shape, HEAD_DIM)
    nope = jnp.where(lo, 0.0, kn_t_pair) if odd else jnp.where(lo, kn_t_pair, 0.0)
    rope = jnp.where(_row_group(kr4_t.shape, h * MLA_ROPE, (h + 1) * MLA_ROPE), kr4_t, 0.0)
    return jnp.concatenate([nope, rope], axis=0).astype(BF16)


def _ctx_mixer_kernel(sink_ref, x_ref, mod_ref, ga_ref, win_ref, wc_ref, gq_ref, wuq_ref, gkv_ref, wukt_ref, wuv_ref,
                      *rest, layer, first):
    y_ref, kg_ref, vg_ref, kn_ref, vn_ref, ckv_ref, kr_ref, h_s = rest[-8:]
    t = SEQ
    for b in range(CTX_SEQS):
        h_s[b * t:(b + 1) * t, :] = _modulated_norm(x_ref[b], ga_ref[...], mod_ref, 0, 1).astype(BF16)
    col_all = _projected_columns(h_s, win_ref)

    def put_state(ref, b, val):
        if first:
            for d in range(DEPTH):
                ref[b, d] = val if d == layer else jnp.zeros_like(val)
        else:
            ref[b] = val

    def conv(b, col):
        y_ref[b, :, Y_CONV:Y_CONV + CONV_WIDTH] = _short_conv(
            col(C_CB, CONV_WIDTH), col(C_CC, CONV_WIDTH), col(C_CV, CONV_WIDTH), wc_ref[...]).astype(BF16)

    def gqa(b, col):
        kg_t, vpair = col(C_GK, LANES).T, col(C_GV, LANES)
        put_state(kg_ref, b, kg_t)
        put_state(vg_ref, b, vpair.T)
        row2 = lax.broadcasted_iota(jnp.int32, (2 * t, 1), 0) < t
        for g in range(GQA_KV_HEADS):
            ke, ko = _key_planes(kg_t, g == 0, g == 0)
            ve, vo = _value_planes(vpair, g == 0, g == 0)
            q = jnp.concatenate([col(C_GQ + (2 * g) * LANES, LANES), col(C_GQ + (2 * g + 1) * LANES, LANES)], axis=0)
            q = (q * (ATTN_SCALE * LOG2E)).astype(BF16)
            s = _dot(q, jnp.concatenate([ke, ko], axis=1))
            sink = [jnp.where(row2, sink_ref[layer, 4 * g + odd], sink_ref[layer, 4 * g + 2 + odd]) * LOG2E
                    for odd in range(2)]
            (pe,), xe = _probabilities([s[:, 0:t]], sink[0])
            (po,), xo = _probabilities([s[:, t:2 * t]], sink[1])
            o = _attend([(pe, ve), (po, vo)], xe, xo).astype(BF16)
            c0 = Y_GQA + (2 * g) * LANES
            y_ref[b, :, c0:c0 + LANES] = o[0:t]
            y_ref[b, :, c0 + LANES:c0 + 2 * LANES] = o[t:2 * t]

    def na(b, col):
        kn_t = col(C_NK, 2 * LANES).T
        put_state(kn_ref, b, kn_t)
        put_state(vn_ref, b, col(C_NV, 2 * LANES).T)
        for j in range(NA_HEADS // 2):
            ke, ko = _key_planes(kn_t[j * LANES:(j + 1) * LANES, :], True, False)
            ve, vo = _value_planes(col(C_NV + j * LANES, LANES), True, False)
            q = (col(C_NQ + j * LANES, LANES) * (ATTN_SCALE * LOG2E)).astype(BF16)
            s = _dot(q, jnp.concatenate([ke, ko], axis=1))
            (pe,), _ = _probabilities([s[:, 0:t]])
            (po,), _ = _probabilities([s[:, t:2 * t]])
            y_ref[b, :, Y_NA + j * LANES:Y_NA + (j + 1) * LANES] = _attend([(pe, ve), (po, vo)]).astype(BF16)

    def mla(b, col):
        ckv = _rms(col(C_MKV, MLA_KV_RANK), gkv_ref[...])
        put_state(ckv_ref, b, ckv)
        kr_t = col(C_MKR, LANES).T[0:MLA_ROPE, :]
        put_state(kr_ref, b, kr_t)
        ckv_b = ckv.astype(BF16)
        q = _dot(_rms(col(C_MQ, MLA_Q_RANK), gq_ref[...]).astype(BF16), wuq_ref[...]) * (MLA_SCALE * LOG2E)
        kn_t_all = _dot_nt(wukt_ref[...], ckv_b)
        v_all = _dot(ckv_b, wuv_ref[...])
        kr4_t = jnp.concatenate([kr_t] * MLA_HEADS, axis=0)
        q_rope = q[:, 2 * LANES:3 * LANES]
        for j in range(MLA_HEADS // 2):
            qj = jnp.concatenate([q[:, j * LANES:(j + 1) * LANES], q_rope], axis=1).astype(BF16)
            kn_t_pair = kn_t_all[j * LANES:(j + 1) * LANES, :]
            keys = jnp.concatenate(
                [_mla_key_plane(kn_t_pair, kr4_t, j, 0), _mla_key_plane(kn_t_pair, kr4_t, j, 1)], axis=1)
            s = _dot(qj, keys)
            ve, vo = _value_planes(v_all[:, j * LANES:(j + 1) * LANES], True, False)
            (pe,), _ = _probabilities([s[:, 0:t]])
            (po,), _ = _probabilities([s[:, t:2 * t]])
            y_ref[b, :, Y_MLA + j * LANES:Y_MLA + (j + 1) * LANES] = _attend([(pe, ve), (po, vo)]).astype(BF16)

    for mixer in (conv, gqa, na, mla):
        for b in range(CTX_SEQS):
            mixer(b, lambda c, n, b=b: col_all(c, n)[b * t:(b + 1) * t])


def _ctx_mixer_call(x, mods, g_attn, w_main, sink, w_conv, g_q, w_uq, g_kv, w_uk_t, w_uv, prev_states, layer, name):
    n, nb = BATCH, CTX_SEQS
    state_shapes = ((2 * HEAD_DIM, SEQ), (2 * HEAD_DIM, SEQ), (4 * HEAD_DIM, SEQ), (4 * HEAD_DIM, SEQ),
                    (SEQ, MLA_KV_RANK), (MLA_ROPE, SEQ))
    first = prev_states is None
    if first:
        state_spec = lambda shape: pl.BlockSpec((nb, DEPTH) + shape, lambda s: (s, 0, 0, 0))
    else:
        state_spec = lambda shape: pl.BlockSpec((nb, None) + shape, lambda s: (s, layer, 0, 0))
    in_specs = [
        pl.BlockSpec(memory_space=pltpu.SMEM),
        pl.BlockSpec((nb, SEQ, D_MODEL), lambda s: (s, 0, 0)),
        _mod_spec(layer, True, 1),
        _layer_spec((1, D_MODEL), layer, 1),
        _layer_spec((MAIN_COLS, D_MODEL), layer, 1),
        _layer_spec((CONV_K, CONV_WIDTH), layer, 1),
        _layer_spec((1, MLA_Q_RANK), layer, 1),
        _layer_spec((MLA_Q_RANK, 3 * LANES), layer, 1),
        _layer_spec((1, MLA_KV_RANK), layer, 1),
        _layer_spec((2 * LANES, MLA_KV_RANK), layer, 1),
        _layer_spec((MLA_KV_RANK, 2 * LANES), layer, 1),
    ]
    args = [sink, x, mods, g_attn, w_main, w_conv, g_q, w_uq, g_kv, w_uk_t, w_uv]
    aliases = {}
    if not first:
        for i, st in enumerate(prev_states):
            aliases[len(args)] = 1 + i
            in_specs.append(pl.BlockSpec(memory_space=pl.ANY))
            args.append(st)
    outs = pl.pallas_call(
        functools.partial(_ctx_mixer_kernel, layer=layer, first=first),
        grid=(n // nb,),
        in_specs=in_specs,
        out_specs=[pl.BlockSpec((nb, SEQ, Y_COLS), lambda s: (s, 0, 0))] + [state_spec(s) for s in state_shapes],
        out_shape=[jax.ShapeDtypeStruct((n, SEQ, Y_COLS), BF16)]
        + [jax.ShapeDtypeStruct((n, DEPTH) + s, F32) for s in state_shapes],
        input_output_aliases=aliases,
        scratch_shapes=[pltpu.VMEM((nb * SEQ, D_MODEL), BF16)],
        compiler_params=pltpu.CompilerParams(
            dimension_semantics=("arbitrary",), vmem_limit_bytes=VMEM_LIMIT),
        name=name,
    )(*args)
    return outs[0], outs[1:]


def _lat_mixer_kernel(sink_ref, x_ref, mod_ref, ga_ref, win_ref, wc_ref, gq_ref, wuq_ref, gkv_ref, wuk_ref, wuvt_ref,
                      rc_ref, rsh_ref, rsl_ref, mc_ref, msh_ref, msl_ref, band_ref, nab_ref,
                      cgk_ref, cgv_ref, cnk_ref, cnv_ref, cckv_ref, ckr_ref,
                      y_ref,
                      h_s, gq_s, gk_s, gv_s, gkc_s, gvc_s, nq_s, nk_s, nv_s, nkc_s, nvc_s, mq_s, mk_s, mv_s, *, layer):
    t = DEC_SEQ
    h_s[...] = _modulated_norm(x_ref[...], ga_ref[...], mod_ref, 0, 1).astype(BF16)
    col = _projected_columns(h_s, win_ref)
    rope64 = lambda x: _rope(x, rc_ref[...], rsh_ref[...], rsl_ref[...], HEAD_DIM // 4)
    rope32 = lambda x: _rope(x, mc_ref[...], msh_ref[...], msl_ref[...], MLA_ROPE // 4)

    y_ref[:, Y_CONV:Y_CONV + CONV_WIDTH] = _short_conv(
        col(C_CB, CONV_WIDTH), col(C_CC, CONV_WIDTH), col(C_CV, CONV_WIDTH), wc_ref[...]).astype(BF16)

    group = GQA_HEADS // GQA_KV_HEADS
    for j in range(GQA_HEADS // 2):
        q_pair = rope64(col(C_GQ + j * LANES, LANES)) * (ATTN_SCALE * LOG2E)
        gq_s[j * LANES:(j + 1) * LANES, :] = q_pair.T.astype(BF16)
    gk_s[...] = rope64(col(C_GK, LANES)).astype(BF16)
    gkc_s[...] = cgk_ref[...].T.astype(BF16)
    v_t = col(C_GV, LANES).T
    for g in range(GQA_KV_HEADS):
        gv_s[g] = _value_plane_t(v_t[g * HEAD_DIM:(g + 1) * HEAD_DIM, :])
        gvc_s[g] = _value_plane_t(cgv_ref[g * HEAD_DIM:(g + 1) * HEAD_DIM, :])

    def gqa_block(b, c_lo, c_hi):
        q0 = pl.multiple_of(b * BAND_BLOCK, BAND_BLOCK)
        k0 = pl.multiple_of(q0 + (c_lo - WINDOW), BAND_BLOCK)
        n = c_hi - c_lo
        band = band_ref[c_lo:c_hi, :]
        lane_head = lax.shift_right_logical(
            lax.broadcasted_iota(jnp.int32, (1, group * BAND_BLOCK), 1), BAND_BLOCK.bit_length() - 1)
        zeros = jnp.zeros((HEAD_DIM, group * BAND_BLOCK), BF16)
        for g in range(GQA_KV_HEADS):
            q_t = jnp.concatenate(
                [gq_s[(group * g + h) * HEAD_DIM:(group * g + h + 1) * HEAD_DIM, pl.ds(q0, BAND_BLOCK)]
                 for h in range(group)], axis=1)
            q_t = jnp.concatenate([q_t, zeros] if g == 0 else [zeros, q_t], axis=0)
            s_loc = _dot(gk_s[pl.ds(k0, n), :], q_t) + band
            s_ctx = _dot(gkc_s[...], q_t)
            sink = sink_ref[layer, group * g + group - 1]
            for h in range(group - 2, -1, -1):
                sink = jnp.where(lane_head == h, sink_ref[layer, group * g + h], sink)
            (p_loc, p_ctx), x = _probabilities_t([s_loc, s_ctx], sink * LOG2E)
            o = _attend_t([(gv_s[g, :, pl.ds(k0, n)], p_loc), (gvc_s[g], p_ctx)], x)
            for pr in range(group // 2):
                pair = jnp.concatenate([o[:, (2 * pr) * BAND_BLOCK:(2 * pr + 1) * BAND_BLOCK],
                                        o[:, (2 * pr + 1) * BAND_BLOCK:(2 * pr + 2) * BAND_BLOCK]], axis=0)
                c0 = Y_GQA + (group // 2 * g + pr) * LANES
                y_ref[pl.ds(q0, BAND_BLOCK), c0:c0 + LANES] = pair.T.astype(BF16)

    for j in range(NA_HEADS // 2):
        nq_s[:, j * LANES:(j + 1) * LANES] = (col(C_NQ + j * LANES, LANES) * (ATTN_SCALE * LOG2E)).astype(BF16)
        nk_s[2 * j], nk_s[2 * j + 1] = _key_planes(col(C_NK + j * LANES, LANES).T, True, False)
        nv_s[2 * j], nv_s[2 * j + 1] = _value_planes(col(C_NV + j * LANES, LANES), True, False)
        nkc_s[j] = jnp.concatenate(_key_planes(cnk_ref[j * LANES:(j + 1) * LANES, :], True, False), axis=1)
        nvc_s[2 * j], nvc_s[2 * j + 1] = _value_planes(cnv_ref[j * LANES:(j + 1) * LANES, :].T, True, False)

    na_kinds, na_kind_of = _na_group_kinds()
    na_width = _na_table_width()

    def na_group(k):
        lo, n = _na_group_slab(k)
        q0, nq = k * NA_GROUP_ROWS * GRID_W, NA_GROUP_ROWS * GRID_W
        k0, nk = lo * GRID_W, n * GRID_W
        b0 = sum(kn for kn, _ in na_kinds[:na_kind_of[k]]) * GRID_W
        for j in range(NA_HEADS // 2):
            q = nq_s[q0:q0 + nq, j * LANES:(j + 1) * LANES]
            s_ctx = _dot(q, nkc_s[j])
            terms = []
            for odd in range(2):
                h = 2 * j + odd
                s_loc = _dot(q, nk_s[h, :, k0:k0 + nk]) + nab_ref[:, h * na_width + b0:h * na_width + b0 + nk]
                (p_loc, p_ctx), _ = _probabilities([s_loc, s_ctx[:, odd * PAST_LEN:(odd + 1) * PAST_LEN]])
                terms += [(p_loc, nv_s[h, k0:k0 + nk, :]), (p_ctx, nvc_s[h])]
            y_ref[q0:q0 + nq, Y_NA + j * LANES:Y_NA + (j + 1) * LANES] = _attend(terms).astype(BF16)

    ckv_b = _rms(col(C_MKV, MLA_KV_RANK), gkv_ref[...]).astype(BF16)
    cckv_b = cckv_ref[...].astype(BF16)
    q = _dot(_rms(col(C_MQ, MLA_Q_RANK), gq_ref[...]).astype(BF16), wuq_ref[...]) * (MLA_SCALE * LOG2E)
    for i in range(3):
        tile = q[:, i * LANES:(i + 1) * LANES]
        mq_s[i * LANES:(i + 1) * LANES, :] = (rope32(tile) if i == 2 else tile).T.astype(BF16)
    kr = jnp.where(_lane_lt((t, LANES), MLA_ROPE), rope32(col(C_MKR, LANES)), 0.0)
    kr_c = jnp.concatenate([ckr_ref[...], jnp.zeros((LANES - MLA_ROPE, PAST_LEN), F32)], axis=0).T
    for rows, ckv_x, kr_x in ((slice(0, t), ckv_b, kr), (slice(t, t + PAST_LEN), cckv_b, kr_c)):
        kn_all = _dot(ckv_x, wuk_ref[...])
        v_t_all = _dot_nt(wuvt_ref[...], ckv_x)
        for j in range(MLA_HEADS // 2):
            mk_s[j, rows, :] = jnp.concatenate([kn_all[:, j * LANES:(j + 1) * LANES], kr_x], axis=1).astype(BF16)
        for h in range(MLA_HEADS):
            mv_s[h, :, rows] = _value_plane_t(v_t_all[h * MLA_V:(h + 1) * MLA_V, :])

    tq = 256

    def mla_block(i, carry):
        q0 = pl.multiple_of(i * tq, tq)
        zeros = lambda r: jnp.zeros((r, tq), BF16)
        q_nope = lambda h: mq_s[h * MLA_NOPE:(h + 1) * MLA_NOPE, pl.ds(q0, tq)]
        q_rope = lambda h: mq_s[MLA_HEADS * MLA_NOPE + h * MLA_ROPE:MLA_HEADS * MLA_NOPE + (h + 1) * MLA_ROPE,
                                pl.ds(q0, tq)]
        pad = 2 * LANES - 2 * MLA_NOPE - MLA_ROPE
        for j in range(MLA_HEADS // 2):
            he, ho = 2 * j, 2 * j + 1
            q_e = jnp.concatenate([q_nope(he), zeros(MLA_NOPE), q_rope(he), zeros(pad)], axis=0)
            q_o = jnp.concatenate([zeros(MLA_NOPE), q_nope(ho), q_rope(ho), zeros(pad)], axis=0)
            s = _dot(mk_s[j], jnp.concatenate([q_e, q_o], axis=1))
            outs = []
            for odd in range(2):
                (p,), _ = _probabilities_t([s[:, odd * tq:(odd + 1) * tq]])
                outs.append(_attend_t([(mv_s[2 * j + odd], p)]))
            y_ref[pl.ds(q0, tq), Y_MLA + j * LANES:Y_MLA + (j + 1) * LANES] = (
                jnp.concatenate(outs, axis=0).T.astype(BF16))
        return carry

    span = BAND_BLOCK + 2 * WINDOW
    nb = t // BAND_BLOCK
    for i in range(nb):
        gqa_block(jnp.int32(i), WINDOW if i == 0 else 0, span - WINDOW if i == nb - 1 else span)
        if i % (nb // NA_GROUPS) == 0:
            na_group(i // (nb // NA_GROUPS))
        if i % (nb * tq // t) == 0:
            mla_block(jnp.int32(i // (nb * tq // t)), 0)


def _rope_tables(group, half):
    tok = np.arange(DEC_SEQ)
    pos = np.stack([tok // GRID_W, tok % GRID_W], axis=1).astype(np.float64)
    inv = ROPE_BASE ** (-np.arange(half, dtype=np.float64) / half)
    lane = np.arange(LANES) % group
    axis = lane // (2 * half)
    within = lane % (2 * half)
    ang = pos[:, axis] * inv[within % half][None, :]
    cos, sin = np.cos(ang), np.sin(ang)
    upper = (within >= half)[None, :]
    sin_hi = np.where(upper, sin, 0.0)
    sin_lo = np.where(upper, 0.0, -sin)
    return tuple(jnp.asarray(a, dtype=F32) for a in (cos, sin_hi, sin_lo))


def _band_mask():
    c = np.arange(BAND_BLOCK + 2 * WINDOW)[:, None]
    i = np.arange(BAND_BLOCK)[None, :]
    ok = (c >= i) & (c <= i + 2 * WINDOW)
    m = np.where(ok, 0.0, NEG_INF)
    return jnp.asarray(np.concatenate([m] * (GQA_HEADS // GQA_KV_HEADS), axis=1), dtype=F32)


def _lat_mixer_call(x, mods, g_attn, w_main, sink, w_conv, g_q, w_uq, g_kv, w_uk, w_uv_t, nab, caches, layer, name):
    n, t = DEC_BATCH, DEC_SEQ
    one = pl.Buffered(1)
    const = lambda shape: pl.BlockSpec(shape, lambda s: (0,) * len(shape), pipeline_mode=one)
    cache = lambda shape: pl.BlockSpec((None, None) + shape, lambda s: (s, layer, 0, 0), pipeline_mode=one)
    rope_g = _rope_tables(HEAD_DIM, HEAD_DIM // 4)
    rope_m = _rope_tables(MLA_ROPE, MLA_ROPE // 4)
    span = BAND_BLOCK + 2 * WINDOW
    keys = t + PAST_LEN
    scratch = [
        pltpu.VMEM((t, D_MODEL), BF16),
        pltpu.VMEM((GQA_HEADS * HEAD_DIM, t), BF16),
        pltpu.VMEM((t, LANES), BF16),
        pltpu.VMEM((GQA_KV_HEADS, HEAD_DIM + ONES_ROWS, t), BF16),
        pltpu.VMEM((PAST_LEN, LANES), BF16),
        pltpu.VMEM((GQA_KV_HEADS, HEAD_DIM + ONES_ROWS, PAST_LEN), BF16),
        pltpu.VMEM((t, NA_HEADS // 2 * LANES), BF16),
        pltpu.VMEM((NA_HEADS, LANES, t), BF16),
        pltpu.VMEM((NA_HEADS, t, 2 * LANES), BF16),
        pltpu.VMEM((NA_HEADS // 2, LANES, 2 * PAST_LEN), BF16),
        pltpu.VMEM((NA_HEADS, PAST_LEN, 2 * LANES), BF16),
        pltpu.VMEM((3 * LANES, t), BF16),
        pltpu.VMEM((MLA_HEADS // 2, keys, 2 * LANES), BF16),
        pltpu.VMEM((MLA_HEADS, MLA_V + ONES_ROWS, keys), BF16),
    ]
    return pl.pallas_call(
        functools.partial(_lat_mixer_kernel, layer=layer),
        grid=(n,),
        in_specs=[
            pl.BlockSpec(memory_space=pltpu.SMEM),
            pl.BlockSpec((None, t, D_MODEL), lambda s: (s, 0, 0), pipeline_mode=one),
            _mod_spec(layer, False, 1),
            _layer_spec((1, D_MODEL), layer, 1),
            _layer_spec((MAIN_COLS, D_MODEL), layer, 1),
            _layer_spec((CONV_K, CONV_WIDTH), layer, 1),
            _layer_spec((1, MLA_Q_RANK), layer, 1),
            _layer_spec((MLA_Q_RANK, 3 * LANES), layer, 1),
            _layer_spec((1, MLA_KV_RANK), layer, 1),
            _layer_spec((MLA_KV_RANK, 2 * LANES), layer, 1),
            _layer_spec((2 * LANES, MLA_KV_RANK), layer, 1),
        ] + [const((t, LANES))] * 6 + [
            const((span, GQA_HEADS // GQA_KV_HEADS * BAND_BLOCK)),
            _layer_spec((NA_GROUP_ROWS * GRID_W, NA_HEADS * _na_table_width()), layer, 1),
            cache((2 * HEAD_DIM, PAST_LEN)), cache((2 * HEAD_DIM, PAST_LEN)),
            cache((4 * HEAD_DIM, PAST_LEN)), cache((4 * HEAD_DIM, PAST_LEN)),
            cache((PAST_LEN, MLA_KV_RANK)), cache((MLA_ROPE, PAST_LEN)),
        ],
        out_specs=pl.BlockSpec((None, t, Y_COLS), lambda s: (s, 0, 0)),
        out_shape=jax.ShapeDtypeStruct((n, t, Y_COLS), BF16),
        scratch_shapes=scratch,
        compiler_params=pltpu.CompilerParams(
            dimension_semantics=("arbitrary",), vmem_limit_bytes=VMEM_LIMIT),
        name=name,
    )(sink, x, mods, g_attn, w_main, w_conv, g_q, w_uq, g_kv, w_uk, w_uv_t, *rope_g, *rope_m, _band_mask(), nab, *caches)


def _post_kernel(x_ref, y_ref, mod_ref, ga_ref, gm_ref, gf_ref, wg_ref, wb_ref, wo_ref, w1_ref, w2_ref,
                 o_ref, *, final):
    d = D_MODEL
    mod = lambda i: mod_ref[:, i * d:(i + 1) * d]
    bounds = (Y_CONV, Y_GQA, Y_NA, Y_MLA, Y_COLS)
    for r0 in range(0, x_ref.shape[0], POST_CHAIN_ROWS):
        rows = slice(r0, r0 + POST_CHAIN_ROWS)
        x = x_ref[rows, :]
        h = _modulated_norm(x, ga_ref[...], mod_ref, 0, 1).astype(BF16)
        merged = None
        for i in range(N_BRANCH):
            lo, hi = bounds[i], bounds[i + 1]
            gate = jax.nn.sigmoid(_dot_nt(h, wg_ref[0, i * d:(i + 1) * d, :]))
            term = gate * _dot(y_ref[rows, lo:hi], wb_ref[lo:hi, :])
            merged = term if merged is None else merged + term
        x = x + mod(2) * _dot(merged.astype(BF16), wo_ref[...])
        h = _modulated_norm(x, gm_ref[...], mod_ref, 3, 4).astype(BF16)
        mlp = None
        for c0 in range(0, D_FF, d):
            f = jnp.square(jnp.maximum(_dot(h, w1_ref[:, c0:c0 + d]), 0.0)).astype(BF16)
            term = _dot(f, w2_ref[c0:c0 + d, :])
            mlp = term if mlp is None else mlp + term
        x = x + mod(5) * mlp
        o_ref[rows, :] = _rms(x, gf_ref[...]) if final else x


def _post_call(x, y, mods, g_attn, g_mlp, g_final, w_gates, w_br, w_o, w_ff1, w_ff2, layer, ctx, name):
    n, t, _ = x.shape
    tm = 2 * POST_CHAIN_ROWS
    return pl.pallas_call(
        functools.partial(_post_kernel, final=layer == DEPTH - 1),
        grid=(n, t // tm),
        in_specs=[
            pl.BlockSpec((None, tm, D_MODEL), lambda s, i: (s, i, 0)),
            pl.BlockSpec((None, tm, Y_COLS), lambda s, i: (s, i, 0)),
            _mod_spec(layer, ctx, 2),
            _layer_spec((1, D_MODEL), layer, 2),
            _layer_spec((1, D_MODEL), layer, 2),
            pl.BlockSpec((1, D_MODEL), lambda s, i: (0, 0)),
            pl.BlockSpec((pl.Element(1), pl.Element(N_BRANCH * D_MODEL), pl.Element(D_MODEL)),
                         lambda s, i: (layer, GATE_COL0, 0), pipeline_mode=pl.Buffered(1)),
            _layer_spec((Y_COLS, D_MODEL), layer, 2),
            _layer_spec((D_MODEL, D_MODEL), layer, 2),
            _layer_spec((D_MODEL, D_FF), layer, 2),
            _layer_spec((D_FF, D_MODEL), layer, 2),
        ],
        out_specs=pl.BlockSpec((None, tm, D_MODEL), lambda s, i: (s, i, 0)),
        out_shape=jax.ShapeDtypeStruct((n, t, D_MODEL), F32),
        compiler_params=pltpu.CompilerParams(
            dimension_semantics=("arbitrary", "arbitrary"), vmem_limit_bytes=VMEM_LIMIT),
        name=name,
    )(x, y, mods, g_attn, g_mlp, g_final, w_gates, w_br, w_o, w_ff1, w_ff2)


def kernel(x_prompt, x_sample, cache_gqa_k, cache_gqa_v, cache_na_k, cache_na_v, cache_mla_ckv, cache_mla_krope, c, c_ctx, w_mod, b_mod, g_attn, g_mlp, w_in, w_conv, gqa_sink, na_rpb, mla_g_q, mla_w_uq, mla_g_kv, mla_w_ukv, w_branch_conv, w_branch_gqa, w_branch_na, w_branch_mla, w_o, w_ff1, w_ff2, g_final):
    w_main = w_gates = jnp.transpose(w_in, (0, 2, 1)).astype(BF16)
    uq = mla_w_uq.reshape(DEPTH, MLA_Q_RANK, MLA_HEADS, MLA_NOPE + MLA_ROPE)
    w_uq = jnp.concatenate([uq[..., :MLA_NOPE].reshape(DEPTH, MLA_Q_RANK, -1),
                            uq[..., MLA_NOPE:].reshape(DEPTH, MLA_Q_RANK, -1)], axis=-1).astype(BF16)
    ukv = mla_w_ukv.reshape(DEPTH, MLA_KV_RANK, MLA_HEADS, MLA_NOPE + MLA_V)
    w_uk = ukv[..., :MLA_NOPE].reshape(DEPTH, MLA_KV_RANK, -1).astype(BF16)
    w_uv = ukv[..., MLA_NOPE:].reshape(DEPTH, MLA_KV_RANK, -1).astype(BF16)
    w_uk_t, w_uv_t = jnp.transpose(w_uk, (0, 2, 1)), jnp.transpose(w_uv, (0, 2, 1))
    w_br = jnp.concatenate([w_branch_conv, w_branch_gqa, w_branch_na, w_branch_mla], axis=1).astype(BF16)
    w_o_b, w_ff1_b, w_ff2_b = w_o.astype(BF16), w_ff1.astype(BF16), w_ff2.astype(BF16)
    g_a, g_m = g_attn[:, None, :], g_mlp[:, None, :]
    g_q, g_kv, g_f = mla_g_q[:, None, :], mla_g_kv[:, None, :], g_final[None, :]

    c16 = jnp.concatenate([c, c_ctx[None, :], jnp.zeros((MOD_ROWS - DEC_BATCH - 1, D_MODEL), F32)], axis=0)
    mods = _mod_call(c16, w_mod, b_mod).reshape(DEPTH, MOD_ROWS, 1, 6 * D_MODEL)
    nab = _nab_call(na_rpb)

    heads_t = lambda a: jnp.transpose(a, (0, 1, 3, 4, 2)).reshape(a.shape[0], DEPTH, -1, a.shape[2])
    caches = (heads_t(cache_gqa_k), heads_t(cache_gqa_v), heads_t(cache_na_k), heads_t(cache_na_v),
              cache_mla_ckv, jnp.transpose(cache_mla_krope, (0, 1, 3, 2)))

    h_ctx, h_lat = x_prompt, x_sample
    states = None
    for l in range(DEPTH):
        mixer_w = (gqa_sink, w_conv, g_q, w_uq, g_kv)
        post_w = (g_a, g_m, g_f, w_gates, w_br, w_o_b, w_ff1_b, w_ff2_b)

        y, states = _ctx_mixer_call(h_ctx, mods, g_a, w_main, *mixer_w, w_uk_t, w_uv, states, l, f"mixer_ctx_{l}")
        flat = lambda a: a.reshape(1, BATCH * SEQ, a.shape[-1])
        h_ctx = _post_call(flat(h_ctx), flat(y), mods, *post_w, l, True, f"post_ctx_{l}").reshape(BATCH, SEQ, D_MODEL)

        y = _lat_mixer_call(h_lat, mods, g_a, w_main, *mixer_w, w_uk, w_uv_t, nab, caches, l, f"mixer_lat_{l}")
        h_lat = _post_call(h_lat, y, mods, *post_w, l, False, f"post_lat_{l}")

    def heads_out(a, heads):
        return jnp.transpose(a.reshape(BATCH, DEPTH, heads, HEAD_DIM, SEQ), (0, 1, 4, 2, 3))

    kg, vg, kn, vn, ckv, kr = states
    return (h_ctx, h_lat, heads_out(kg, GQA_KV_HEADS), heads_out(vg, GQA_KV_HEADS),
            heads_out(kn, NA_HEADS), heads_out(vn, NA_HEADS), ckv, jnp.transpose(kr, (0, 1, 3, 2)))
```

```python
import functools
import math

import numpy as np
import jax
import jax.numpy as jnp
from jax import lax
from jax.experimental import pallas as pl
from jax.experimental.pallas import tpu as pltpu

D_MODEL = 1024
BATCH = 32
SEQ = 256
DEPTH = 2
DEC_BATCH = 8
DEC_SEQ = 1024
PAST_LEN = 512
GRID_W = 64
GRID_ROWS = DEC_SEQ // GRID_W
HEAD_DIM = 64
CONV_WIDTH = 256
CONV_K = 3
GQA_HEADS = 8
GQA_KV_HEADS = 2
WINDOW = 128
BAND_BLOCK = 128
NA_HEADS = 4
NA_WIN_H = 8
NA_WIN_W = 16
MLA_HEADS = 4
MLA_Q_RANK = 256
MLA_KV_RANK = 128
MLA_NOPE = 64
MLA_ROPE = 32
MLA_V = 64
D_FF = 4 * D_MODEL
N_BRANCH = 4
ROPE_BASE = 10000.0
EPS = 1e-6
NEG_INF = -1e30
LOG2E = math.log2(math.e)
ATTN_SCALE = HEAD_DIM ** -0.5
MLA_SCALE = (MLA_NOPE + MLA_ROPE) ** -0.5

LANES = 128
MOD_ROWS = 16
CTX_MOD_ROW = DEC_BATCH

C_CB, C_CC, C_CV = 0, 256, 512
C_GQ, C_GK, C_GV = 768, 1280, 1408
C_NQ, C_NK, C_NV = 1536, 1792, 2048
C_MQ, C_MKV, C_MKR = 2304, 2560, 2688
MAIN_COLS = 2816
GATE_COL0 = 2720
Y_CONV, Y_GQA, Y_NA, Y_MLA = 0, 256, 768, 1024
Y_COLS = 1280
NA_GROUPS = 4
NA_GROUP_ROWS = GRID_ROWS // NA_GROUPS

VMEM_LIMIT = 56 * 1024 * 1024
POST_CHAIN_ROWS = 256
CTX_SEQS = 2

F32 = jnp.float32
BF16 = jnp.bfloat16


def _dot(a, b):
    return jnp.dot(a, b, preferred_element_type=F32)


def _dot_nt(a, b):
    return lax.dot_general(a, b, (((1,), (1,)), ((), ())), preferred_element_type=F32)


def _rms(x, g):
    return x * lax.rsqrt(jnp.mean(x * x, axis=-1, keepdims=True) + EPS) * g


def _lane_lt(shape, n):
    return lax.broadcasted_iota(jnp.int32, shape, len(shape) - 1) < n


def _row_lt(shape, n):
    return lax.broadcasted_iota(jnp.int32, shape, 0) < n


def _row_group(shape, lo, hi):
    row = lax.broadcasted_iota(jnp.int32, shape, 0)
    return (row >= lo) & (row < hi)


def _short_conv(cb, cc, cv, w):
    u = cc * cv
    t = u.shape[0]
    row = lax.broadcasted_iota(jnp.int32, u.shape, 0)
    prev = jnp.where(row == 0, 0.0, pltpu.roll(u, 1, 0))
    nxt = jnp.where(row == t - 1, 0.0, pltpu.roll(u, t - 1, 0))
    return cb * (prev * w[0:1, :] + u * w[1:2, :] + nxt * w[2:3, :])


def _rope(x, cos, sin_hi, sin_lo, half):
    n = x.shape[-1]
    return x * cos + pltpu.roll(x, n - half, 1) * sin_lo + pltpu.roll(x, half, 1) * sin_hi


def _repeat_rope_key(tile):
    k = jnp.where(_lane_lt(tile.shape, MLA_ROPE), tile, 0.0)
    k = k + pltpu.roll(k, MLA_ROPE, 1)
    return k + pltpu.roll(k, 2 * MLA_ROPE, 1)


def _mod_kernel(c_ref, w_ref, b_ref, o_ref):
    c = c_ref[...]
    s = c * jax.nn.sigmoid(c)
    o_ref[...] = _dot(s.astype(BF16), w_ref[...].astype(BF16)) + b_ref[...]


def _mod_call(c16, w_mod, b_mod):
    tn = 3072
    return pl.pallas_call(
        _mod_kernel,
        grid=(DEPTH, 6 * D_MODEL // tn),
        in_specs=[
            pl.BlockSpec((MOD_ROWS, D_MODEL), lambda l, j: (0, 0)),
            pl.BlockSpec((None, D_MODEL, tn), lambda l, j: (l, 0, j)),
            pl.BlockSpec((None, 1, tn), lambda l, j: (l, 0, j)),
        ],
        out_specs=pl.BlockSpec((None, MOD_ROWS, tn), lambda l, j: (l, 0, j)),
        out_shape=jax.ShapeDtypeStruct((DEPTH, MOD_ROWS, 6 * D_MODEL), F32),
        compiler_params=pltpu.CompilerParams(
            dimension_semantics=("arbitrary", "arbitrary"), vmem_limit_bytes=VMEM_LIMIT),
        name="adaln_mod",
    )(c16, w_mod, b_mod.reshape(DEPTH, 1, 6 * D_MODEL))


def _na_window_start(r):
    return min(max(r - NA_WIN_H // 2, 0), GRID_ROWS - NA_WIN_H)


def _na_group_slab(k):
    starts = [_na_window_start(r) for r in range(k * NA_GROUP_ROWS, (k + 1) * NA_GROUP_ROWS)]
    lo = min(starts) // 2 * 2
    n = -(-(max(starts) + NA_WIN_H - lo) // 4) * 4
    assert lo + n <= GRID_ROWS
    return lo, n


def _na_group_kinds():
    kinds, of_group = [], []
    for k in range(NA_GROUPS):
        lo, n = _na_group_slab(k)
        layout = {}
        for dr in range(NA_GROUP_ROWS):
            r = k * NA_GROUP_ROWS + dr
            r0 = _na_window_start(r)
            for a in range(n):
                inside = r0 <= lo + a < r0 + NA_WIN_H
                layout[(dr, a)] = lo + a - r + NA_WIN_H - 1 if inside else None
        if (n, layout) not in kinds:
            kinds.append((n, layout))
        of_group.append(kinds.index((n, layout)))
    return kinds, of_group


def _na_table_width():
    return sum(n for n, _ in _na_group_kinds()[0]) * GRID_W


def _nab_kernel(rpb_ref, o_ref):
    l = pl.program_id(0)
    c = lax.broadcasted_iota(jnp.int32, (GRID_W, GRID_W), 0)
    w = lax.broadcasted_iota(jnp.int32, (GRID_W, GRID_W), 1)
    dc = w - c + (NA_WIN_W - 1)
    c0 = jnp.clip(c - NA_WIN_W // 2, 0, GRID_W - NA_WIN_W)
    outside = (w < c0) | (w >= c0 + NA_WIN_W)
    n_dr, n_dc = 2 * NA_WIN_H - 1, 2 * NA_WIN_W - 1
    kinds, _ = _na_group_kinds()
    width = _na_table_width()

    def put(h, d, tile):
        col0 = h * width
        for n, layout in kinds:
            for (dr, a), want in layout.items():
                if want == d:
                    o_ref[dr * GRID_W:(dr + 1) * GRID_W, col0 + a * GRID_W:col0 + (a + 1) * GRID_W] = tile
            col0 += n * GRID_W

    for h in range(NA_HEADS):
        put(h, None, jnp.full((GRID_W, GRID_W), NEG_INF, F32))
        for d in range(n_dr):
            base = ((l * NA_HEADS + h) * n_dr + d) * n_dc
            tile = jnp.full((GRID_W, GRID_W), NEG_INF, F32)
            for j in range(n_dc):
                tile = jnp.where(dc == j, rpb_ref[base + j] * LOG2E, tile)
            put(h, d, jnp.where(outside, NEG_INF, tile))


def _nab_call(na_rpb):
    shape = (NA_GROUP_ROWS * GRID_W, NA_HEADS * _na_table_width())
    return pl.pallas_call(
        _nab_kernel,
        grid=(DEPTH,),
        in_specs=[pl.BlockSpec(memory_space=pltpu.SMEM)],
        out_specs=pl.BlockSpec((None,) + shape, lambda l: (l, 0, 0)),
        out_shape=jax.ShapeDtypeStruct((DEPTH,) + shape, F32),
        compiler_params=pltpu.CompilerParams(dimension_semantics=("arbitrary",), vmem_limit_bytes=VMEM_LIMIT),
        name="na_bias_tables",
    )(na_rpb.reshape(-1))


def _mod_spec(layer, ctx, n_grid):
    if n_grid == 1:
        index = (lambda s: (layer, CTX_MOD_ROW, 0, 0)) if ctx else (lambda s: (layer, s, 0, 0))
    else:
        index = (lambda s, i: (layer, CTX_MOD_ROW, 0, 0)) if ctx else (lambda s, i: (layer, s, 0, 0))
    return pl.BlockSpec((None, None, 1, 6 * D_MODEL), index)


def _layer_spec(shape, layer, n_grid):
    zeros = (0,) * len(shape)
    index = (lambda s: (layer,) + zeros) if n_grid == 1 else (lambda s, i: (layer,) + zeros)
    return pl.BlockSpec((None,) + tuple(shape), index, pipeline_mode=pl.Buffered(1))


_COLUMN_GROUPS = ((C_CB, C_GQ), (C_GQ, C_NQ), (C_NQ, C_MQ), (C_MQ, MAIN_COLS))


def _modulated_norm(x, g, mod_ref, shift, scale):
    d = D_MODEL
    return _rms(x, g) * (1.0 + mod_ref[:, scale * d:(scale + 1) * d]) + mod_ref[:, shift * d:(shift + 1) * d]


def _projected_columns(h_ref, w_ref):
    cache = {}

    def col(c, n):
        lo, hi = next(g for g in _COLUMN_GROUPS if g[0] <= c and c + n <= g[1])
        if lo not in cache:
            cache.clear()
            cache[lo] = _dot_nt(h_ref[...], w_ref[lo:hi, :])
        return cache[lo][:, c - lo:c - lo + n]

    return col


def _key_planes(k_t, even_first, odd_first):
    lo = _row_lt(k_t.shape, HEAD_DIM)
    swapped = None
    if not even_first or odd_first:
        swapped = pltpu.roll(k_t, HEAD_DIM, 0)
    top = jnp.where(lo, k_t if even_first else swapped, 0.0)
    bot = jnp.where(lo, 0.0, swapped if odd_first else k_t)
    return top.astype(BF16), bot.astype(BF16)


def _value_planes(pair, even_first, odd_first):
    lo = _lane_lt(pair.shape, HEAD_DIM)
    swapped = None
    if not even_first or odd_first:
        swapped = pltpu.roll(pair, HEAD_DIM, 1)
    top = jnp.where(lo, pair if even_first else swapped, 0.0)
    bot = jnp.where(lo, 0.0, swapped if odd_first else pair)
    one_e = jnp.where(lo, 1.0, 0.0)
    return (jnp.concatenate([top, one_e], axis=1).astype(BF16),
            jnp.concatenate([bot, 1.0 - one_e], axis=1).astype(BF16))


def _probabilities(parts, sink=None):
    m = parts[0].max(axis=-1, keepdims=True)
    for p in parts[1:]:
        m = jnp.maximum(m, p.max(axis=-1, keepdims=True))
    if sink is not None:
        m = jnp.maximum(m, sink)
    probs = [jnp.exp2((p - m).astype(BF16)) for p in parts]
    return probs, (None if sink is None else jnp.exp2(sink - m))


def _attend(terms, sink_e=None, sink_o=None):
    o = None
    for p, v in terms:
        t = _dot(p, v)
        o = t if o is None else o + t
    den = o[:, LANES:]
    if sink_e is not None:
        den = den + jnp.where(_lane_lt(den.shape, HEAD_DIM), sink_e, sink_o)
    return o[:, :LANES] / den


ONES_ROWS = 16


def _value_plane_t(v_t):
    return jnp.concatenate([v_t, jnp.ones((ONES_ROWS, v_t.shape[1]), v_t.dtype)], axis=0).astype(BF16)


def _probabilities_t(parts, sink=None):
    m = parts[0].max(axis=0, keepdims=True)
    for p in parts[1:]:
        m = jnp.maximum(m, p.max(axis=0, keepdims=True))
    if sink is not None:
        m = jnp.maximum(m, sink)
    probs = [jnp.exp2((p - m).astype(BF16)) for p in parts]
    return probs, (None if sink is None else jnp.exp2(sink - m))


def _attend_t(terms, sink_num=None):
    o = None
    for v_t, p in terms:
        t = _dot(v_t, p)
        o = t if o is None else o + t
    den = o[HEAD_DIM:HEAD_DIM + 1, :]
    if sink_num is not None:
        den = den + sink_num
    return o[0:HEAD_DIM, :] / den


def _mla_key_plane(kn_t_pair, kr4_t, j, odd):
    h = 2 * j + odd
    lo = _row_lt(kn_t_pair.shape, HEAD_DIM)
    nope = jnp.where(lo, 0.0, kn_t_pair) if odd else jnp.where(lo, kn_t_pair, 0.0)
    rope = jnp.where(_row_group(kr4_t.shape, h * MLA_ROPE, (h + 1) * MLA_ROPE), kr4_t, 0.0)
    return jnp.concatenate([nope, rope], axis=0).astype(BF16)


def _ctx_mixer_kernel(sink_ref, x_ref, mod_ref, ga_ref, win_ref, wc_ref, gq_ref, wuq_ref, gkv_ref, wukt_ref, wuv_ref,
                      *rest, layer, first):
    y_ref, kg_ref, vg_ref, kn_ref, vn_ref, ckv_ref, kr_ref, h_s = rest[-8:]
    t = SEQ
    for b in range(CTX_SEQS):
        h_s[b * t:(b + 1) * t, :] = _modulated_norm(x_ref[b], ga_ref[...], mod_ref, 0, 1).astype(BF16)
    col_all = _projected_columns(h_s, win_ref)

    def put_state(ref, b, val):
        if first:
            for d in range(DEPTH):
                ref[b, d] = val if d == layer else jnp.zeros_like(val)
        else:
            ref[b] = val

    def conv(b, col):
        y_ref[b, :, Y_CONV:Y_CONV + CONV_WIDTH] = _short_conv(
            col(C_CB, CONV_WIDTH), col(C_CC, CONV_WIDTH), col(C_CV, CONV_WIDTH), wc_ref[...]).astype(BF16)

    def gqa(b, col):
        kg_t, vpair = col(C_GK, LANES).T, col(C_GV, LANES)
        put_state(kg_ref, b, kg_t)
        put_state(vg_ref, b, vpair.T)
        row2 = lax.broadcasted_iota(jnp.int32, (2 * t, 1), 0) < t
        for g in range(GQA_KV_HEADS):
            ke, ko = _key_planes(kg_t, g == 0, g == 0)
            ve, vo = _value_planes(vpair, g == 0, g == 0)
            q = jnp.concatenate([col(C_GQ + (2 * g) * LANES, LANES), col(C_GQ + (2 * g + 1) * LANES, LANES)], axis=0)
            q = (q * (ATTN_SCALE * LOG2E)).astype(BF16)
            s = _dot(q, jnp.concatenate([ke, ko], axis=1))
            sink = [jnp.where(row2, sink_ref[layer, 4 * g + odd], sink_ref[layer, 4 * g + 2 + odd]) * LOG2E
                    for odd in range(2)]
            (pe,), xe = _probabilities([s[:, 0:t]], sink[0])
            (po,), xo = _probabilities([s[:, t:2 * t]], sink[1])
            o = _attend([(pe, ve), (po, vo)], xe, xo).astype(BF16)
            c0 = Y_GQA + (2 * g) * LANES
            y_ref[b, :, c0:c0 + LANES] = o[0:t]
            y_ref[b, :, c0 + LANES:c0 + 2 * LANES] = o[t:2 * t]

    def na(b, col):
        kn_t = col(C_NK, 2 * LANES).T
        put_state(kn_ref, b, kn_t)
        put_state(vn_ref, b, col(C_NV, 2 * LANES).T)
        for j in range(NA_HEADS // 2):
            ke, ko = _key_planes(kn_t[j * LANES:(j + 1) * LANES, :], True, False)
            ve, vo = _value_planes(col(C_NV + j * LANES, LANES), True, False)
            q = (col(C_NQ + j * LANES, LANES) * (ATTN_SCALE * LOG2E)).astype(BF16)
            s = _dot(q, jnp.concatenate([ke, ko], axis=1))
            (pe,), _ = _probabilities([s[:, 0:t]])
            (po,), _ = _probabilities([s[:, t:2 * t]])
            y_ref[b, :, Y_NA + j * LANES:Y_NA + (j + 1) * LANES] = _attend([(pe, ve), (po, vo)]).astype(BF16)

    def mla(b, col):
        ckv = _rms(col(C_MKV, MLA_KV_RANK), gkv_ref[...])
        put_state(ckv_ref, b, ckv)
        kr_t = col(C_MKR, LANES).T[0:MLA_ROPE, :]
        put_state(kr_ref, b, kr_t)
        ckv_b = ckv.astype(BF16)
        q = _dot(_rms(col(C_MQ, MLA_Q_RANK), gq_ref[...]).astype(BF16), wuq_ref[...]) * (MLA_SCALE * LOG2E)
        kn_t_all = _dot_nt(wukt_ref[...], ckv_b)
        v_all = _dot(ckv_b, wuv_ref[...])
        kr4_t = jnp.concatenate([kr_t] * MLA_HEADS, axis=0)
        q_rope = q[:, 2 * LANES:3 * LANES]
        for j in range(MLA_HEADS // 2):
            qj = jnp.concatenate([q[:, j * LANES:(j + 1) * LANES], q_rope], axis=1).astype(BF16)
            kn_t_pair = kn_t_all[j * LANES:(j + 1) * LANES, :]
            keys = jnp.concatenate(
                [_mla_key_plane(kn_t_pair, kr4_t, j, 0), _mla_key_plane(kn_t_pair, kr4_t, j, 1)], axis=1)
            s = _dot(qj, keys)
            ve, vo = _value_planes(v_all[:, j * LANES:(j + 1) * LANES], True, False)
            (pe,), _ = _probabilities([s[:, 0:t]])
            (po,), _ = _probabilities([s[:, t:2 * t]])
            y_ref[b, :, Y_MLA + j * LANES:Y_MLA + (j + 1) * LANES] = _attend([(pe, ve), (po, vo)]).astype(BF16)

    for mixer in (conv, gqa, na, mla):
        for b in range(CTX_SEQS):
            mixer(b, lambda c, n, b=b: col_all(c, n)[b * t:(b + 1) * t])


def _ctx_mixer_call(x, mods, g_attn, w_main, sink, w_conv, g_q, w_uq, g_kv, w_uk_t, w_uv, prev_states, layer, name):
    n, nb = BATCH, CTX_SEQS
    state_shapes = ((2 * HEAD_DIM, SEQ), (2 * HEAD_DIM, SEQ), (4 * HEAD_DIM, SEQ), (4 * HEAD_DIM, SEQ),
                    (SEQ, MLA_KV_RANK), (MLA_ROPE, SEQ))
    first = prev_states is None
    if first:
        state_spec = lambda shape: pl.BlockSpec((nb, DEPTH) + shape, lambda s: (s, 0, 0, 0))
    else:
        state_spec = lambda shape: pl.BlockSpec((nb, None) + shape, lambda s: (s, layer, 0, 0))
    in_specs = [
        pl.BlockSpec(memory_space=pltpu.SMEM),
        pl.BlockSpec((nb, SEQ, D_MODEL), lambda s: (s, 0, 0)),
        _mod_spec(layer, True, 1),
        _layer_spec((1, D_MODEL), layer, 1),
        _layer_spec((MAIN_COLS, D_MODEL), layer, 1),
        _layer_spec((CONV_K, CONV_WIDTH), layer, 1),
        _layer_spec((1, MLA_Q_RANK), layer, 1),
        _layer_spec((MLA_Q_RANK, 3 * LANES), layer, 1),
        _layer_spec((1, MLA_KV_RANK), layer, 1),
        _layer_spec((2 * LANES, MLA_KV_RANK), layer, 1),
        _layer_spec((MLA_KV_RANK, 2 * LANES), layer, 1),
    ]
    args = [sink, x, mods, g_attn, w_main, w_conv, g_q, w_uq, g_kv, w_uk_t, w_uv]
    aliases = {}
    if not first:
        for i, st in enumerate(prev_states):
            aliases[len(args)] = 1 + i
            in_specs.append(pl.BlockSpec(memory_space=pl.ANY))
            args.append(st)
    outs = pl.pallas_call(
        functools.partial(_ctx_mixer_kernel, layer=layer, first=first),
        grid=(n // nb,),
        in_specs=in_specs,
        out_specs=[pl.BlockSpec((nb, SEQ, Y_COLS), lambda s: (s, 0, 0))] + [state_spec(s) for s in state_shapes],
        out_shape=[jax.ShapeDtypeStruct((n, SEQ, Y_COLS), BF16)]
        + [jax.ShapeDtypeStruct((n, DEPTH) + s, F32) for s in state_shapes],
        input_output_aliases=aliases,
        scratch_shapes=[pltpu.VMEM((nb * SEQ, D_MODEL), BF16)],
        compiler_params=pltpu.CompilerParams(
            dimension_semantics=("arbitrary",), vmem_limit_bytes=VMEM_LIMIT),
        name=name,
    )(*args)
    return outs[0], outs[1:]


def _lat_mixer_kernel(sink_ref, x_ref, mod_ref, ga_ref, win_ref, wc_ref, gq_ref, wuq_ref, gkv_ref, wuk_ref, wuvt_ref,
                      rc_ref, rsh_ref, rsl_ref, mc_ref, msh_ref, msl_ref, band_ref, nab_ref,
                      cgk_ref, cgv_ref, cnk_ref, cnv_ref, cckv_ref, ckr_ref,
                      y_ref,
                      h_s, gq_s, gk_s, gv_s, gkc_s, gvc_s, nq_s, nk_s, nv_s, nkc_s, nvc_s, mq_s, mk_s, mv_s, *, layer):
    t = DEC_SEQ
    h_s[...] = _modulated_norm(x_ref[...], ga_ref[...], mod_ref, 0, 1).astype(BF16)
    col = _projected_columns(h_s, win_ref)
    rope64 = lambda x: _rope(x, rc_ref[...], rsh_ref[...], rsl_ref[...], HEAD_DIM // 4)
    rope32 = lambda x: _rope(x, mc_ref[...], msh_ref[...], msl_ref[...], MLA_ROPE // 4)

    group = GQA_HEADS // GQA_KV_HEADS
    for j in range(GQA_HEADS // 2):
        q_pair = rope64(col(C_GQ + j * LANES, LANES)) * (ATTN_SCALE * LOG2E)
        gq_s[j * LANES:(j + 1) * LANES, :] = q_pair.T.astype(BF16)
    gk_s[...] = rope64(col(C_GK, LANES)).astype(BF16)
    gkc_s[...] = cgk_ref[...].T.astype(BF16)
    v_t = col(C_GV, LANES).T
    for g in range(GQA_KV_HEADS):
        gv_s[g] = _value_plane_t(v_t[g * HEAD_DIM:(g + 1) * HEAD_DIM, :])
        gvc_s[g] = _value_plane_t(cgv_ref[g * HEAD_DIM:(g + 1) * HEAD_DIM, :])

    def gqa_block(b, c_lo, c_hi):
        q0 = pl.multiple_of(b * BAND_BLOCK, BAND_BLOCK)
        k0 = pl.multiple_of(q0 + (c_lo - WINDOW), BAND_BLOCK)
        n = c_hi - c_lo
        band = band_ref[c_lo:c_hi, :]
        lane_head = lax.shift_right_logical(
            lax.broadcasted_iota(jnp.int32, (1, group * BAND_BLOCK), 1), BAND_BLOCK.bit_length() - 1)
        zeros = jnp.zeros((HEAD_DIM, group * BAND_BLOCK), BF16)
        for g in range(GQA_KV_HEADS):
            q_t = jnp.concatenate(
                [gq_s[(group * g + h) * HEAD_DIM:(group * g + h + 1) * HEAD_DIM, pl.ds(q0, BAND_BLOCK)]
                 for h in range(group)], axis=1)
            q_t = jnp.concatenate([q_t, zeros] if g == 0 else [zeros, q_t], axis=0)
            s_loc = _dot(gk_s[pl.ds(k0, n), :], q_t) + band
            s_ctx = _dot(gkc_s[...], q_t)
            sink = sink_ref[layer, group * g + group - 1]
            for h in range(group - 2, -1, -1):
                sink = jnp.where(lane_head == h, sink_ref[layer, group * g + h], sink)
            (p_loc, p_ctx), x = _probabilities_t([s_loc, s_ctx], sink * LOG2E)
            o = _attend_t([(gv_s[g, :, pl.ds(k0, n)], p_loc), (gvc_s[g], p_ctx)], x)
            for pr in range(group // 2):
                pair = jnp.concatenate([o[:, (2 * pr) * BAND_BLOCK:(2 * pr + 1) * BAND_BLOCK],
                                        o[:, (2 * pr + 1) * BAND_BLOCK:(2 * pr + 2) * BAND_BLOCK]], axis=0)
                c0 = Y_GQA + (group // 2 * g + pr) * LANES
                y_ref[pl.ds(q0, BAND_BLOCK), c0:c0 + LANES] = pair.T.astype(BF16)

    span = BAND_BLOCK + 2 * WINDOW
    nb = t // BAND_BLOCK
    gqa = lambda i: gqa_block(jnp.int32(i), WINDOW if i == 0 else 0, span - WINDOW if i == nb - 1 else span)
    gqa(0)

    for j in range(NA_HEADS // 2):
        nq_s[:, j * LANES:(j + 1) * LANES] = (col(C_NQ + j * LANES, LANES) * (ATTN_SCALE * LOG2E)).astype(BF16)
        nk_s[2 * j], nk_s[2 * j + 1] = _key_planes(col(C_NK + j * LANES, LANES).T, True, False)
        nv_s[2 * j], nv_s[2 * j + 1] = _value_planes(col(C_NV + j * LANES, LANES), True, False)
        nkc_s[j] = jnp.concatenate(_key_planes(cnk_ref[j * LANES:(j + 1) * LANES, :], True, False), axis=1)
        nvc_s[2 * j], nvc_s[2 * j + 1] = _value_planes(cnv_ref[j * LANES:(j + 1) * LANES, :].T, True, False)

    na_kinds, na_kind_of = _na_group_kinds()
    na_width = _na_table_width()

    def na_group(k):
        lo, n = _na_group_slab(k)
        q0, nq = k * NA_GROUP_ROWS * GRID_W, NA_GROUP_ROWS * GRID_W
        k0, nk = lo * GRID_W, n * GRID_W
        b0 = sum(kn for kn, _ in na_kinds[:na_kind_of[k]]) * GRID_W
        for j in range(NA_HEADS // 2):
            q = nq_s[q0:q0 + nq, j * LANES:(j + 1) * LANES]
            s_ctx = _dot(q, nkc_s[j])
            terms = []
            for odd in range(2):
                h = 2 * j + odd
                s_loc = _dot(q, nk_s[h, :, k0:k0 + nk]) + nab_ref[:, h * na_width + b0:h * na_width + b0 + nk]
                (p_loc, p_ctx), _ = _probabilities([s_loc, s_ctx[:, odd * PAST_LEN:(odd + 1) * PAST_LEN]])
                terms += [(p_loc, nv_s[h, k0:k0 + nk, :]), (p_ctx, nvc_s[h])]
            y_ref[q0:q0 + nq, Y_NA + j * LANES:Y_NA + (j + 1) * LANES] = _attend(terms).astype(BF16)

    gqa(1)
    na_group(0)
    gqa(2)

    ckv_b = _rms(col(C_MKV, MLA_KV_RANK), gkv_ref[...]).astype(BF16)
    cckv_b = cckv_ref[...].astype(BF16)
    q = _dot(_rms(col(C_MQ, MLA_Q_RANK), gq_ref[...]).astype(BF16), wuq_ref[...]) * (MLA_SCALE * LOG2E)
    for i in range(3):
        tile = q[:, i * LANES:(i + 1) * LANES]
        mq_s[i * LANES:(i + 1) * LANES, :] = (rope32(tile) if i == 2 else tile).T.astype(BF16)
    kr = jnp.where(_lane_lt((t, LANES), MLA_ROPE), rope32(col(C_MKR, LANES)), 0.0)
    kr_c = jnp.concatenate([ckr_ref[...], jnp.zeros((LANES - MLA_ROPE, PAST_LEN), F32)], axis=0).T
    for rows, ckv_x, kr_x in ((slice(0, t), ckv_b, kr), (slice(t, t + PAST_LEN), cckv_b, kr_c)):
        kn_all = _dot(ckv_x, wuk_ref[...])
        v_t_all = _dot_nt(wuvt_ref[...], ckv_x)
        for j in range(MLA_HEADS // 2):
            mk_s[j, rows, :] = jnp.concatenate([kn_all[:, j * LANES:(j + 1) * LANES], kr_x], axis=1).astype(BF16)
        for h in range(MLA_HEADS):
            mv_s[h, :, rows] = _value_plane_t(v_t_all[h * MLA_V:(h + 1) * MLA_V, :])

    tq = 256

    def mla_block(i, carry):
        q0 = pl.multiple_of(i * tq, tq)
        zeros = lambda r: jnp.zeros((r, tq), BF16)
        q_nope = lambda h: mq_s[h * MLA_NOPE:(h + 1) * MLA_NOPE, pl.ds(q0, tq)]
        q_rope = lambda h: mq_s[MLA_HEADS * MLA_NOPE + h * MLA_ROPE:MLA_HEADS * MLA_NOPE + (h + 1) * MLA_ROPE,
                                pl.ds(q0, tq)]
        pad = 2 * LANES - 2 * MLA_NOPE - MLA_ROPE
        for j in range(MLA_HEADS // 2):
            he, ho = 2 * j, 2 * j + 1
            q_e = jnp.concatenate([q_nope(he), zeros(MLA_NOPE), q_rope(he), zeros(pad)], axis=0)
            q_o = jnp.concatenate([zeros(MLA_NOPE), q_nope(ho), q_rope(ho), zeros(pad)], axis=0)
            s = _dot(mk_s[j], jnp.concatenate([q_e, q_o], axis=1))
            outs = []
            for odd in range(2):
                (p,), _ = _probabilities_t([s[:, odd * tq:(odd + 1) * tq]])
                outs.append(_attend_t([(mv_s[2 * j + odd], p)]))
            y_ref[pl.ds(q0, tq), Y_MLA + j * LANES:Y_MLA + (j + 1) * LANES] = (
                jnp.concatenate(outs, axis=0).T.astype(BF16))
        return carry

    mla = lambda i: mla_block(jnp.int32(i), 0)
    gqa(3)
    na_group(1)
    mla(0)
    gqa(4)
    na_group(2)
    mla(1)
    gqa(5)
    na_group(3)
    mla(2)

    y_ref[:, Y_CONV:Y_CONV + CONV_WIDTH] = _short_conv(
        col(C_CB, CONV_WIDTH), col(C_CC, CONV_WIDTH), col(C_CV, CONV_WIDTH), wc_ref[...]).astype(BF16)

    gqa(6)
    gqa(7)
    mla(3)


def _rope_tables(group, half):
    tok = np.arange(DEC_SEQ)
    pos = np.stack([tok // GRID_W, tok % GRID_W], axis=1).astype(np.float64)
    inv = ROPE_BASE ** (-np.arange(half, dtype=np.float64) / half)
    lane = np.arange(LANES) % group
    axis = lane // (2 * half)
    within = lane % (2 * half)
    ang = pos[:, axis] * inv[within % half][None, :]
    cos, sin = np.cos(ang), np.sin(ang)
    upper = (within >= half)[None, :]
    sin_hi = np.where(upper, sin, 0.0)
    sin_lo = np.where(upper, 0.0, -sin)
    return tuple(jnp.asarray(a, dtype=F32) for a in (cos, sin_hi, sin_lo))


def _band_mask():
    c = np.arange(BAND_BLOCK + 2 * WINDOW)[:, None]
    i = np.arange(BAND_BLOCK)[None, :]
    ok = (c >= i) & (c <= i + 2 * WINDOW)
    m = np.where(ok, 0.0, NEG_INF)
    return jnp.asarray(np.concatenate([m] * (GQA_HEADS // GQA_KV_HEADS), axis=1), dtype=F32)


def _lat_mixer_call(x, mods, g_attn, w_main, sink, w_conv, g_q, w_uq, g_kv, w_uk, w_uv_t, nab, caches, layer, name):
    n, t = DEC_BATCH, DEC_SEQ
    one = pl.Buffered(1)
    const = lambda shape: pl.BlockSpec(shape, lambda s: (0,) * len(shape), pipeline_mode=one)
    cache = lambda shape: pl.BlockSpec((None, None) + shape, lambda s: (s, layer, 0, 0), pipeline_mode=one)
    rope_g = _rope_tables(HEAD_DIM, HEAD_DIM // 4)
    rope_m = _rope_tables(MLA_ROPE, MLA_ROPE // 4)
    span = BAND_BLOCK + 2 * WINDOW
    keys = t + PAST_LEN
    scratch = [
        pltpu.VMEM((t, D_MODEL), BF16),
        pltpu.VMEM((GQA_HEADS * HEAD_DIM, t), BF16),
        pltpu.VMEM((t, LANES), BF16),
        pltpu.VMEM((GQA_KV_HEADS, HEAD_DIM + ONES_ROWS, t), BF16),
        pltpu.VMEM((PAST_LEN, LANES), BF16),
        pltpu.VMEM((GQA_KV_HEADS, HEAD_DIM + ONES_ROWS, PAST_LEN), BF16),
        pltpu.VMEM((t, NA_HEADS // 2 * LANES), BF16),
        pltpu.VMEM((NA_HEADS, LANES, t), BF16),
        pltpu.VMEM((NA_HEADS, t, 2 * LANES), BF16),
        pltpu.VMEM((NA_HEADS // 2, LANES, 2 * PAST_LEN), BF16),
        pltpu.VMEM((NA_HEADS, PAST_LEN, 2 * LANES), BF16),
        pltpu.VMEM((3 * LANES, t), BF16),
        pltpu.VMEM((MLA_HEADS // 2, keys, 2 * LANES), BF16),
        pltpu.VMEM((MLA_HEADS, MLA_V + ONES_ROWS, keys), BF16),
    ]
    return pl.pallas_call(
        functools.partial(_lat_mixer_kernel, layer=layer),
        grid=(n,),
        in_specs=[
            pl.BlockSpec(memory_space=pltpu.SMEM),
            pl.BlockSpec((None, t, D_MODEL), lambda s: (s, 0, 0), pipeline_mode=one),
            _mod_spec(layer, False, 1),
            _layer_spec((1, D_MODEL), layer, 1),
            _layer_spec((MAIN_COLS, D_MODEL), layer, 1),
            _layer_spec((CONV_K, CONV_WIDTH), layer, 1),
            _layer_spec((1, MLA_Q_RANK), layer, 1),
            _layer_spec((MLA_Q_RANK, 3 * LANES), layer, 1),
            _layer_spec((1, MLA_KV_RANK), layer, 1),
            _layer_spec((MLA_KV_RANK, 2 * LANES), layer, 1),
            _layer_spec((2 * LANES, MLA_KV_RANK), layer, 1),
        ] + [const((t, LANES))] * 6 + [
            const((span, GQA_HEADS // GQA_KV_HEADS * BAND_BLOCK)),
            _layer_spec((NA_GROUP_ROWS * GRID_W, NA_HEADS * _na_table_width()), layer, 1),
            cache((2 * HEAD_DIM, PAST_LEN)), cache((2 * HEAD_DIM, PAST_LEN)),
            cache((4 * HEAD_DIM, PAST_LEN)), cache((4 * HEAD_DIM, PAST_LEN)),
            cache((PAST_LEN, MLA_KV_RANK)), cache((MLA_ROPE, PAST_LEN)),
        ],
        out_specs=pl.BlockSpec((None, t, Y_COLS), lambda s: (s, 0, 0)),
        out_shape=jax.ShapeDtypeStruct((n, t, Y_COLS), BF16),
        scratch_shapes=scratch,
        compiler_params=pltpu.CompilerParams(
            dimension_semantics=("arbitrary",), vmem_limit_bytes=VMEM_LIMIT),
        name=name,
    )(sink, x, mods, g_attn, w_main, w_conv, g_q, w_uq, g_kv, w_uk, w_uv_t, *rope_g, *rope_m, _band_mask(), nab, *caches)


def _post_kernel(x_ref, y_ref, mod_ref, ga_ref, gm_ref, gf_ref, wg_ref, wb_ref, wo_ref, w1_ref, w2_ref,
                 o_ref, *, final):
    d = D_MODEL
    mod = lambda i: mod_ref[:, i * d:(i + 1) * d]
    bounds = (Y_CONV, Y_GQA, Y_NA, Y_MLA, Y_COLS)
    for r0 in range(0, x_ref.shape[0], POST_CHAIN_ROWS):
        rows = slice(r0, r0 + POST_CHAIN_ROWS)
        x = x_ref[rows, :]
        h = _modulated_norm(x, ga_ref[...], mod_ref, 0, 1).astype(BF16)
        merged = None
        for i in range(N_BRANCH):
            lo, hi = bounds[i], bounds[i + 1]
            gate = jax.nn.sigmoid(_dot_nt(h, wg_ref[0, i * d:(i + 1) * d, :]))
            term = gate * _dot(y_ref[rows, lo:hi], wb_ref[lo:hi, :])
            merged = term if merged is None else merged + term
        x = x + mod(2) * _dot(merged.astype(BF16), wo_ref[...])
        h = _modulated_norm(x, gm_ref[...], mod_ref, 3, 4).astype(BF16)
        mlp = None
        for c0 in range(0, D_FF, d):
            f = jnp.square(jnp.maximum(_dot(h, w1_ref[:, c0:c0 + d]), 0.0)).astype(BF16)
            term = _dot(f, w2_ref[c0:c0 + d, :])
            mlp = term if mlp is None else mlp + term
        x = x + mod(5) * mlp
        o_ref[rows, :] = _rms(x, gf_ref[...]) if final else x


def _post_call(x, y, mods, g_attn, g_mlp, g_final, w_gates, w_br, w_o, w_ff1, w_ff2, layer, ctx, name):
    n, t, _ = x.shape
    tm = 2 * POST_CHAIN_ROWS
    return pl.pallas_call(
        functools.partial(_post_kernel, final=layer == DEPTH - 1),
        grid=(n, t // tm),
        in_specs=[
            pl.BlockSpec((None, tm, D_MODEL), lambda s, i: (s, i, 0)),
            pl.BlockSpec((None, tm, Y_COLS), lambda s, i: (s, i, 0)),
            _mod_spec(layer, ctx, 2),
            _layer_spec((1, D_MODEL), layer, 2),
            _layer_spec((1, D_MODEL), layer, 2),
            pl.BlockSpec((1, D_MODEL), lambda s, i: (0, 0)),
            pl.BlockSpec((pl.Element(1), pl.Element(N_BRANCH * D_MODEL), pl.Element(D_MODEL)),
                         lambda s, i: (layer, GATE_COL0, 0), pipeline_mode=pl.Buffered(1)),
            _layer_spec((Y_COLS, D_MODEL), layer, 2),
            _layer_spec((D_MODEL, D_MODEL), layer, 2),
            _layer_spec((D_MODEL, D_FF), layer, 2),
            _layer_spec((D_FF, D_MODEL), layer, 2),
        ],
        out_specs=pl.BlockSpec((None, tm, D_MODEL), lambda s, i: (s, i, 0)),
        out_shape=jax.ShapeDtypeStruct((n, t, D_MODEL), F32),
        compiler_params=pltpu.CompilerParams(
            dimension_semantics=("arbitrary", "arbitrary"), vmem_limit_bytes=VMEM_LIMIT),
        name=name,
    )(x, y, mods, g_attn, g_mlp, g_final, w_gates, w_br, w_o, w_ff1, w_ff2)


def kernel(x_prompt, x_sample, cache_gqa_k, cache_gqa_v, cache_na_k, cache_na_v, cache_mla_ckv, cache_mla_krope, c, c_ctx, w_mod, b_mod, g_attn, g_mlp, w_in, w_conv, gqa_sink, na_rpb, mla_g_q, mla_w_uq, mla_g_kv, mla_w_ukv, w_branch_conv, w_branch_gqa, w_branch_na, w_branch_mla, w_o, w_ff1, w_ff2, g_final):
    w_main = w_gates = jnp.transpose(w_in, (0, 2, 1)).astype(BF16)
    uq = mla_w_uq.reshape(DEPTH, MLA_Q_RANK, MLA_HEADS, MLA_NOPE + MLA_ROPE)
    w_uq = jnp.concatenate([uq[..., :MLA_NOPE].reshape(DEPTH, MLA_Q_RANK, -1),
                            uq[..., MLA_NOPE:].reshape(DEPTH, MLA_Q_RANK, -1)], axis=-1).astype(BF16)
    ukv = mla_w_ukv.reshape(DEPTH, MLA_KV_RANK, MLA_HEADS, MLA_NOPE + MLA_V)
    w_uk = ukv[..., :MLA_NOPE].reshape(DEPTH, MLA_KV_RANK, -1).astype(BF16)
    w_uv = ukv[..., MLA_NOPE:].reshape(DEPTH, MLA_KV_RANK, -1).astype(BF16)
    w_uk_t, w_uv_t = jnp.transpose(w_uk, (0, 2, 1)), jnp.transpose(w_uv, (0, 2, 1))
    w_br = jnp.concatenate([w_branch_conv, w_branch_gqa, w_branch_na, w_branch_mla], axis=1).astype(BF16)
    w_o_b, w_ff1_b, w_ff2_b = w_o.astype(BF16), w_ff1.astype(BF16), w_ff2.astype(BF16)
    g_a, g_m = g_attn[:, None, :], g_mlp[:, None, :]
    g_q, g_kv, g_f = mla_g_q[:, None, :], mla_g_kv[:, None, :], g_final[None, :]

    c16 = jnp.concatenate([c, c_ctx[None, :], jnp.zeros((MOD_ROWS - DEC_BATCH - 1, D_MODEL), F32)], axis=0)
    mods = _mod_call(c16, w_mod, b_mod).reshape(DEPTH, MOD_ROWS, 1, 6 * D_MODEL)
    nab = _nab_call(na_rpb)

    heads_t = lambda a: jnp.transpose(a, (0, 1, 3, 4, 2)).reshape(a.shape[0], DEPTH, -1, a.shape[2])
    caches = (heads_t(cache_gqa_k), heads_t(cache_gqa_v), heads_t(cache_na_k), heads_t(cache_na_v),
              cache_mla_ckv, jnp.transpose(cache_mla_krope, (0, 1, 3, 2)))

    h_ctx, h_lat = x_prompt, x_sample
    states = None
    for l in range(DEPTH):
        mixer_w = (gqa_sink, w_conv, g_q, w_uq, g_kv)
        post_w = (g_a, g_m, g_f, w_gates, w_br, w_o_b, w_ff1_b, w_ff2_b)

        y, states = _ctx_mixer_call(h_ctx, mods, g_a, w_main, *mixer_w, w_uk_t, w_uv, states, l, f"mixer_ctx_{l}")
        flat = lambda a: a.reshape(1, BATCH * SEQ, a.shape[-1])
        h_ctx = _post_call(flat(h_ctx), flat(y), mods, *post_w, l, True, f"post_ctx_{l}").reshape(BATCH, SEQ, D_MODEL)

        y = _lat_mixer_call(h_lat, mods, g_a, w_main, *mixer_w, w_uk, w_uv_t, nab, caches, l, f"mixer_lat_{l}")
        h_lat = _post_call(h_lat, y, mods, *post_w, l, False, f"post_lat_{l}")

    def heads_out(a, heads):
        return jnp.transpose(a.reshape(BATCH, DEPTH, heads, HEAD_DIM, SEQ), (0, 1, 4, 2, 3))

    kg, vg, kn, vn, ckv, kr = states
    return (h_ctx, h_lat, heads_out(kg, GQA_KV_HEADS), heads_out(vg, GQA_KV_HEADS),
            heads_out(kn, NA_HEADS), heads_out(vn, NA_HEADS), ckv, jnp.transpose(kr, (0, 1, 3, 2)))
```

```python
import functools
import math

import numpy as np
import jax
import jax.numpy as jnp
from jax import lax
from jax.experimental import pallas as pl
from jax.experimental.pallas import tpu as pltpu

D_MODEL = 1024
BATCH = 32
SEQ = 256
DEPTH = 2
DEC_BATCH = 8
DEC_SEQ = 1024
PAST_LEN = 512
GRID_W = 64
GRID_ROWS = DEC_SEQ // GRID_W
HEAD_DIM = 64
CONV_WIDTH = 256
CONV_K = 3
GQA_HEADS = 8
GQA_KV_HEADS = 2
WINDOW = 128
BAND_BLOCK = 128
NA_HEADS = 4
NA_WIN_H = 8
NA_WIN_W = 16
MLA_HEADS = 4
MLA_Q_RANK = 256
MLA_KV_RANK = 128
MLA_NOPE = 64
MLA_ROPE = 32
MLA_V = 64
D_FF = 4 * D_MODEL
N_BRANCH = 4
ROPE_BASE = 10000.0
EPS = 1e-6
NEG_INF = -1e30
LOG2E = math.log2(math.e)
ATTN_SCALE = HEAD_DIM ** -0.5
MLA_SCALE = (MLA_NOPE + MLA_ROPE) ** -0.5

LANES = 128
MOD_ROWS = 16
CTX_MOD_ROW = DEC_BATCH

C_CB, C_CC, C_CV = 0, 256, 512
C_GQ, C_GK, C_GV = 768, 1280, 1408
C_NQ, C_NK, C_NV = 1536, 1792, 2048
C_MQ, C_MKV, C_MKR = 2304, 2560, 2688
MAIN_COLS = 2816
GATE_COL0 = 2720
Y_CONV, Y_GQA, Y_NA, Y_MLA = 0, 256, 768, 1024
Y_COLS = 1280
NA_GROUPS = 4
NA_GROUP_ROWS = GRID_ROWS // NA_GROUPS

VMEM_LIMIT = 56 * 1024 * 1024
POST_CHAIN_ROWS = 256
CTX_SEQS = 4

F32 = jnp.float32
BF16 = jnp.bfloat16


def _dot(a, b):
    return jnp.dot(a, b, preferred_element_type=F32)


def _dot_nt(a, b):
    return lax.dot_general(a, b, (((1,), (1,)), ((), ())), preferred_element_type=F32)


def _rms(x, g):
    return x * lax.rsqrt(jnp.mean(x * x, axis=-1, keepdims=True) + EPS) * g


def _lane_lt(shape, n):
    return lax.broadcasted_iota(jnp.int32, shape, len(shape) - 1) < n


def _row_lt(shape, n):
    return lax.broadcasted_iota(jnp.int32, shape, 0) < n


def _row_group(shape, lo, hi):
    row = lax.broadcasted_iota(jnp.int32, shape, 0)
    return (row >= lo) & (row < hi)


def _short_conv(cb, cc, cv, w):
    u = cc * cv
    t = u.shape[0]
    row = lax.broadcasted_iota(jnp.int32, u.shape, 0)
    prev = jnp.where(row == 0, 0.0, pltpu.roll(u, 1, 0))
    nxt = jnp.where(row == t - 1, 0.0, pltpu.roll(u, t - 1, 0))
    return cb * (prev * w[0:1, :] + u * w[1:2, :] + nxt * w[2:3, :])


def _rope(x, cos, sin_hi, sin_lo, half):
    n = x.shape[-1]
    return x * cos + pltpu.roll(x, n - half, 1) * sin_lo + pltpu.roll(x, half, 1) * sin_hi


def _mod_kernel(c_ref, w_ref, b_ref, o_ref):
    c = c_ref[...]
    s = c * jax.nn.sigmoid(c)
    o_ref[...] = _dot(s.astype(BF16), w_ref[...].astype(BF16)) + b_ref[...]


def _mod_call(c16, w_mod, b_mod):
    tn = 1536
    return pl.pallas_call(
        _mod_kernel,
        grid=(DEPTH, 6 * D_MODEL // tn),
        in_specs=[
            pl.BlockSpec((MOD_ROWS, D_MODEL), lambda l, j: (0, 0)),
            pl.BlockSpec((None, D_MODEL, tn), lambda l, j: (l, 0, j)),
            pl.BlockSpec((None, 1, tn), lambda l, j: (l, 0, j)),
        ],
        out_specs=pl.BlockSpec((None, MOD_ROWS, tn), lambda l, j: (l, 0, j)),
        out_shape=jax.ShapeDtypeStruct((DEPTH, MOD_ROWS, 6 * D_MODEL), F32),
        compiler_params=pltpu.CompilerParams(
            dimension_semantics=("arbitrary", "arbitrary"), vmem_limit_bytes=VMEM_LIMIT),
        name="adaln_mod",
    )(c16, w_mod, b_mod.reshape(DEPTH, 1, 6 * D_MODEL))


def _na_window_start(r):
    return min(max(r - NA_WIN_H // 2, 0), GRID_ROWS - NA_WIN_H)


def _na_group_slab(k):
    starts = [_na_window_start(r) for r in range(k * NA_GROUP_ROWS, (k + 1) * NA_GROUP_ROWS)]
    lo = min(starts) // 2 * 2
    n = -(-(max(starts) + NA_WIN_H - lo) // 4) * 4
    assert lo + n <= GRID_ROWS
    return lo, n


def _na_group_kinds():
    kinds, of_group = [], []
    for k in range(NA_GROUPS):
        lo, n = _na_group_slab(k)
        layout = {}
        for dr in range(NA_GROUP_ROWS):
            r = k * NA_GROUP_ROWS + dr
            r0 = _na_window_start(r)
            for a in range(n):
                inside = r0 <= lo + a < r0 + NA_WIN_H
                layout[(dr, a)] = lo + a - r + NA_WIN_H - 1 if inside else None
        if (n, layout) not in kinds:
            kinds.append((n, layout))
        of_group.append(kinds.index((n, layout)))
    return kinds, of_group


def _na_table_width():
    return sum(n for n, _ in _na_group_kinds()[0]) * GRID_W


def _nab_kernel(rpb_ref, o_ref):
    l = pl.program_id(0)
    c = lax.broadcasted_iota(jnp.int32, (GRID_W, GRID_W), 0)
    w = lax.broadcasted_iota(jnp.int32, (GRID_W, GRID_W), 1)
    dc = w - c + (NA_WIN_W - 1)
    c0 = jnp.clip(c - NA_WIN_W // 2, 0, GRID_W - NA_WIN_W)
    outside = (w < c0) | (w >= c0 + NA_WIN_W)
    n_dr, n_dc = 2 * NA_WIN_H - 1, 2 * NA_WIN_W - 1
    kinds, _ = _na_group_kinds()
    width = _na_table_width()

    def put(h, d, tile):
        col0 = h * width
        for n, layout in kinds:
            for (dr, a), want in layout.items():
                if want == d:
                    o_ref[dr * GRID_W:(dr + 1) * GRID_W, col0 + a * GRID_W:col0 + (a + 1) * GRID_W] = tile
            col0 += n * GRID_W

    for h in range(NA_HEADS):
        put(h, None, jnp.full((GRID_W, GRID_W), NEG_INF, F32))
        for d in range(n_dr):
            base = ((l * NA_HEADS + h) * n_dr + d) * n_dc
            tile = jnp.full((GRID_W, GRID_W), NEG_INF, F32)
            for j in range(n_dc):
                tile = jnp.where(dc == j, rpb_ref[base + j] * LOG2E, tile)
            put(h, d, jnp.where(outside, NEG_INF, tile))


def _nab_call(na_rpb):
    shape = (NA_GROUP_ROWS * GRID_W, NA_HEADS * _na_table_width())
    return pl.pallas_call(
        _nab_kernel,
        grid=(DEPTH,),
        in_specs=[pl.BlockSpec(memory_space=pltpu.SMEM)],
        out_specs=pl.BlockSpec((None,) + shape, lambda l: (l, 0, 0)),
        out_shape=jax.ShapeDtypeStruct((DEPTH,) + shape, F32),
        compiler_params=pltpu.CompilerParams(dimension_semantics=("arbitrary",), vmem_limit_bytes=VMEM_LIMIT),
        name="na_bias_tables",
    )(na_rpb.reshape(-1))


def _mod_spec(layer, ctx, n_grid):
    if n_grid == 1:
        index = (lambda s: (layer, CTX_MOD_ROW, 0, 0)) if ctx else (lambda s: (layer, s, 0, 0))
    else:
        index = (lambda s, i: (layer, CTX_MOD_ROW, 0, 0)) if ctx else (lambda s, i: (layer, s, 0, 0))
    return pl.BlockSpec((None, None, 1, 6 * D_MODEL), index)


def _layer_spec(shape, layer, n_grid):
    zeros = (0,) * len(shape)
    index = (lambda s: (layer,) + zeros) if n_grid == 1 else (lambda s, i: (layer,) + zeros)
    return pl.BlockSpec((None,) + tuple(shape), index, pipeline_mode=pl.Buffered(1))


_COLUMN_GROUPS = ((C_CB, C_GQ), (C_GQ, C_NQ), (C_NQ, C_MQ), (C_MQ, MAIN_COLS))


def _modulated_norm(x, g, mod_ref, shift, scale):
    d = D_MODEL
    return _rms(x, g) * (1.0 + mod_ref[:, scale * d:(scale + 1) * d]) + mod_ref[:, shift * d:(shift + 1) * d]


def _projected_columns(h_ref, w_ref):
    cache = {}

    def col(c, n):
        lo, hi = next(g for g in _COLUMN_GROUPS if g[0] <= c and c + n <= g[1])
        if lo not in cache:
            cache.clear()
            cache[lo] = _dot_nt(h_ref[...], w_ref[lo:hi, :])
        return cache[lo][:, c - lo:c - lo + n]

    return col


def _key_planes(k_t, even_first, odd_first):
    lo = _row_lt(k_t.shape, HEAD_DIM)
    swapped = None
    if not even_first or odd_first:
        swapped = pltpu.roll(k_t, HEAD_DIM, 0)
    top = jnp.where(lo, k_t if even_first else swapped, 0.0)
    bot = jnp.where(lo, 0.0, swapped if odd_first else k_t)
    return top.astype(BF16), bot.astype(BF16)


def _value_planes(pair, even_first, odd_first):
    lo = _lane_lt(pair.shape, HEAD_DIM)
    swapped = None
    if not even_first or odd_first:
        swapped = pltpu.roll(pair, HEAD_DIM, 1)
    top = jnp.where(lo, pair if even_first else swapped, 0.0)
    bot = jnp.where(lo, 0.0, swapped if odd_first else pair)
    one_e = jnp.where(lo, 1.0, 0.0)
    return (jnp.concatenate([top, one_e], axis=1).astype(BF16),
            jnp.concatenate([bot, 1.0 - one_e], axis=1).astype(BF16))


def _probabilities(parts, sink=None):
    m = parts[0].max(axis=-1, keepdims=True)
    for p in parts[1:]:
        m = jnp.maximum(m, p.max(axis=-1, keepdims=True))
    if sink is not None:
        m = jnp.maximum(m, sink)
    probs = [jnp.exp2((p - m).astype(BF16)) for p in parts]
    return probs, (None if sink is None else jnp.exp2(sink - m))


def _attend(terms, sink_e=None, sink_o=None):
    o = None
    for p, v in terms:
        t = _dot(p, v)
        o = t if o is None else o + t
    den = o[:, LANES:]
    if sink_e is not None:
        den = den + jnp.where(_lane_lt(den.shape, HEAD_DIM), sink_e, sink_o)
    return o[:, :LANES] / den


ONES_ROWS = 16


def _value_plane_t(v_t):
    return jnp.concatenate([v_t, jnp.ones((ONES_ROWS, v_t.shape[1]), v_t.dtype)], axis=0).astype(BF16)


def _probabilities_t(parts, sink=None):
    m = parts[0].max(axis=0, keepdims=True)
    for p in parts[1:]:
        m = jnp.maximum(m, p.max(axis=0, keepdims=True))
    if sink is not None:
        m = jnp.maximum(m, sink)
    probs = [jnp.exp2((p - m).astype(BF16)) for p in parts]
    return probs, (None if sink is None else jnp.exp2(sink - m))


def _attend_t(terms, sink_num=None):
    o = None
    for v_t, p in terms:
        t = _dot(v_t, p)
        o = t if o is None else o + t
    den = o[HEAD_DIM:HEAD_DIM + 1, :]
    if sink_num is not None:
        den = den + sink_num
    return o[0:HEAD_DIM, :] / den


def _mla_key_plane(kn_t_pair, kr4_t, j, odd):
    h = 2 * j + odd
    lo = _row_lt(kn_t_pair.shape, HEAD_DIM)
    nope = jnp.where(lo, 0.0, kn_t_pair) if odd else jnp.where(lo, kn_t_pair, 0.0)
    rope = jnp.where(_row_group(kr4_t.shape, h * MLA_ROPE, (h + 1) * MLA_ROPE), kr4_t, 0.0)
    return jnp.concatenate([nope, rope], axis=0).astype(BF16)


def _ctx_mixer_kernel(sink_ref, x_ref, mod_ref, ga_ref, win_ref, wc_ref, gq_ref, wuq_ref, gkv_ref, wukt_ref, wuv_ref,
                      *rest, layer, first):
    y_ref, kg_ref, vg_ref, kn_ref, vn_ref, ckv_ref, kr_ref, h_s = rest[-8:]
    t = SEQ
    for b in range(CTX_SEQS):
        h_s[b * t:(b + 1) * t, :] = _modulated_norm(x_ref[b], ga_ref[...], mod_ref, 0, 1).astype(BF16)
    col_all = _projected_columns(h_s, win_ref)

    def put_state(ref, b, val):
        if first:
            for d in range(DEPTH):
                ref[b, d] = val if d == layer else jnp.zeros_like(val)
        else:
            ref[b] = val

    def conv(b, col):
        y_ref[b, :, Y_CONV:Y_CONV + CONV_WIDTH] = _short_conv(
            col(C_CB, CONV_WIDTH), col(C_CC, CONV_WIDTH), col(C_CV, CONV_WIDTH), wc_ref[...]).astype(BF16)

    def gqa(b, col):
        kg_t, vpair = col(C_GK, LANES).T, col(C_GV, LANES)
        put_state(kg_ref, b, kg_t)
        put_state(vg_ref, b, vpair.T)
        row2 = lax.broadcasted_iota(jnp.int32, (2 * t, 1), 0) < t
        for g in range(GQA_KV_HEADS):
            ke, ko = _key_planes(kg_t, g == 0, g == 0)
            ve, vo = _value_planes(vpair, g == 0, g == 0)
            q = jnp.concatenate([col(C_GQ + (2 * g) * LANES, LANES), col(C_GQ + (2 * g + 1) * LANES, LANES)], axis=0)
            q = (q * (ATTN_SCALE * LOG2E)).astype(BF16)
            s = _dot(q, jnp.concatenate([ke, ko], axis=1))
            sink = [jnp.where(row2, sink_ref[layer, 4 * g + odd], sink_ref[layer, 4 * g + 2 + odd]) * LOG2E
                    for odd in range(2)]
            (pe,), xe = _probabilities([s[:, 0:t]], sink[0])
            (po,), xo = _probabilities([s[:, t:2 * t]], sink[1])
            o = _attend([(pe, ve), (po, vo)], xe, xo).astype(BF16)
            c0 = Y_GQA + (2 * g) * LANES
            y_ref[b, :, c0:c0 + LANES] = o[0:t]
            y_ref[b, :, c0 + LANES:c0 + 2 * LANES] = o[t:2 * t]

    def na(b, col):
        kn_t = col(C_NK, 2 * LANES).T
        put_state(kn_ref, b, kn_t)
        put_state(vn_ref, b, col(C_NV, 2 * LANES).T)
        for j in range(NA_HEADS // 2):
            ke, ko = _key_planes(kn_t[j * LANES:(j + 1) * LANES, :], True, False)
            ve, vo = _value_planes(col(C_NV + j * LANES, LANES), True, False)
            q = (col(C_NQ + j * LANES, LANES) * (ATTN_SCALE * LOG2E)).astype(BF16)
            s = _dot(q, jnp.concatenate([ke, ko], axis=1))
            (pe,), _ = _probabilities([s[:, 0:t]])
            (po,), _ = _probabilities([s[:, t:2 * t]])
            y_ref[b, :, Y_NA + j * LANES:Y_NA + (j + 1) * LANES] = _attend([(pe, ve), (po, vo)]).astype(BF16)

    def mla(b, col):
        ckv = _rms(col(C_MKV, MLA_KV_RANK), gkv_ref[...])
        put_state(ckv_ref, b, ckv)
        kr_t = col(C_MKR, LANES).T[0:MLA_ROPE, :]
        put_state(kr_ref, b, kr_t)
        ckv_b = ckv.astype(BF16)
        q = _dot(_rms(col(C_MQ, MLA_Q_RANK), gq_ref[...]).astype(BF16), wuq_ref[...]) * (MLA_SCALE * LOG2E)
        kn_t_all = _dot_nt(wukt_ref[...], ckv_b)
        v_all = _dot(ckv_b, wuv_ref[...])
        kr4_t = jnp.concatenate([kr_t] * MLA_HEADS, axis=0)
        q_rope = q[:, 2 * LANES:3 * LANES]
        for j in range(MLA_HEADS // 2):
            qj = jnp.concatenate([q[:, j * LANES:(j + 1) * LANES], q_rope], axis=1).astype(BF16)
            kn_t_pair = kn_t_all[j * LANES:(j + 1) * LANES, :]
            keys = jnp.concatenate(
                [_mla_key_plane(kn_t_pair, kr4_t, j, 0), _mla_key_plane(kn_t_pair, kr4_t, j, 1)], axis=1)
            s = _dot(qj, keys)
            ve, vo = _value_planes(v_all[:, j * LANES:(j + 1) * LANES], True, False)
            (pe,), _ = _probabilities([s[:, 0:t]])
            (po,), _ = _probabilities([s[:, t:2 * t]])
            y_ref[b, :, Y_MLA + j * LANES:Y_MLA + (j + 1) * LANES] = _attend([(pe, ve), (po, vo)]).astype(BF16)

    for mixer in (conv, gqa, na, mla):
        for b in range(CTX_SEQS):
            mixer(b, lambda c, n, b=b: col_all(c, n)[b * t:(b + 1) * t])


def _ctx_mixer_call(x, mods, g_attn, w_main, sink, w_conv, g_q, w_uq, g_kv, w_uk_t, w_uv, prev_states, layer, name):
    n, nb = BATCH, CTX_SEQS
    state_shapes = ((2 * HEAD_DIM, SEQ), (2 * HEAD_DIM, SEQ), (4 * HEAD_DIM, SEQ), (4 * HEAD_DIM, SEQ),
                    (SEQ, MLA_KV_RANK), (MLA_ROPE, SEQ))
    first = prev_states is None
    if first:
        state_spec = lambda shape: pl.BlockSpec((nb, DEPTH) + shape, lambda s: (s, 0, 0, 0))
    else:
        state_spec = lambda shape: pl.BlockSpec((nb, None) + shape, lambda s: (s, layer, 0, 0))
    in_specs = [
        pl.BlockSpec(memory_space=pltpu.SMEM),
        pl.BlockSpec((nb, SEQ, D_MODEL), lambda s: (s, 0, 0)),
        _mod_spec(layer, True, 1),
        _layer_spec((1, D_MODEL), layer, 1),
        _layer_spec((MAIN_COLS, D_MODEL), layer, 1),
        _layer_spec((CONV_K, CONV_WIDTH), layer, 1),
        _layer_spec((1, MLA_Q_RANK), layer, 1),
        _layer_spec((MLA_Q_RANK, 3 * LANES), layer, 1),
        _layer_spec((1, MLA_KV_RANK), layer, 1),
        _layer_spec((2 * LANES, MLA_KV_RANK), layer, 1),
        _layer_spec((MLA_KV_RANK, 2 * LANES), layer, 1),
    ]
    args = [sink, x, mods, g_attn, w_main, w_conv, g_q, w_uq, g_kv, w_uk_t, w_uv]
    aliases = {}
    if not first:
        for i, st in enumerate(prev_states):
            aliases[len(args)] = 1 + i
            in_specs.append(pl.BlockSpec(memory_space=pl.ANY))
            args.append(st)
    outs = pl.pallas_call(
        functools.partial(_ctx_mixer_kernel, layer=layer, first=first),
        grid=(n // nb,),
        in_specs=in_specs,
        out_specs=[pl.BlockSpec((nb, SEQ, Y_COLS), lambda s: (s, 0, 0))] + [state_spec(s) for s in state_shapes],
        out_shape=[jax.ShapeDtypeStruct((n, SEQ, Y_COLS), BF16)]
        + [jax.ShapeDtypeStruct((n, DEPTH) + s, F32) for s in state_shapes],
        input_output_aliases=aliases,
        scratch_shapes=[pltpu.VMEM((nb * SEQ, D_MODEL), BF16)],
        compiler_params=pltpu.CompilerParams(
            dimension_semantics=("arbitrary",), vmem_limit_bytes=VMEM_LIMIT),
        name=name,
    )(*args)
    return outs[0], outs[1:]


def _lat_mixer_kernel(sink_ref, x_ref, mod_ref, ga_ref, win_ref, wc_ref, gq_ref, wuq_ref, gkv_ref, wuk_ref, wuvt_ref,
                      rc_ref, rsh_ref, rsl_ref, mc_ref, msh_ref, msl_ref, band_ref, nab_ref,
                      cgk_ref, cgv_ref, cnk_ref, cnv_ref, cckv_ref, ckr_ref,
                      y_ref,
                      h_s, gq_s, gk_s, gv_s, gkc_s, gvc_s, nq_s, nk_s, nv_s, nkc_s, nvc_s, mq_s, mk_s, mv_s, *, layer):
    t = DEC_SEQ
    h_s[...] = _modulated_norm(x_ref[...], ga_ref[...], mod_ref, 0, 1).astype(BF16)
    col = _projected_columns(h_s, win_ref)
    rope64 = lambda x: _rope(x, rc_ref[...], rsh_ref[...], rsl_ref[...], HEAD_DIM // 4)
    rope32 = lambda x: _rope(x, mc_ref[...], msh_ref[...], msl_ref[...], MLA_ROPE // 4)

    y_ref[:, Y_CONV:Y_CONV + CONV_WIDTH] = _short_conv(
        col(C_CB, CONV_WIDTH), col(C_CC, CONV_WIDTH), col(C_CV, CONV_WIDTH), wc_ref[...]).astype(BF16)

    group = GQA_HEADS // GQA_KV_HEADS
    for j in range(GQA_HEADS // 2):
        q_pair = rope64(col(C_GQ + j * LANES, LANES)) * (ATTN_SCALE * LOG2E)
        gq_s[j * LANES:(j + 1) * LANES, :] = q_pair.T.astype(BF16)
    gk_s[...] = rope64(col(C_GK, LANES)).astype(BF16)
    gkc_s[...] = cgk_ref[...].T.astype(BF16)
    v_t = col(C_GV, LANES).T
    for g in range(GQA_KV_HEADS):
        gv_s[g] = _value_plane_t(v_t[g * HEAD_DIM:(g + 1) * HEAD_DIM, :])
        gvc_s[g] = _value_plane_t(cgv_ref[g * HEAD_DIM:(g + 1) * HEAD_DIM, :])

    def gqa_block(b, c_lo, c_hi):
        q0 = pl.multiple_of(b * BAND_BLOCK, BAND_BLOCK)
        k0 = pl.multiple_of(q0 + (c_lo - WINDOW), BAND_BLOCK)
        n = c_hi - c_lo
        band = band_ref[c_lo:c_hi, :]
        lane_head = lax.shift_right_logical(
            lax.broadcasted_iota(jnp.int32, (1, group * BAND_BLOCK), 1), BAND_BLOCK.bit_length() - 1)
        zeros = jnp.zeros((HEAD_DIM, group * BAND_BLOCK), BF16)
        for g in range(GQA_KV_HEADS):
            q_t = jnp.concatenate(
                [gq_s[(group * g + h) * HEAD_DIM:(group * g + h + 1) * HEAD_DIM, pl.ds(q0, BAND_BLOCK)]
                 for h in range(group)], axis=1)
            q_t = jnp.concatenate([q_t, zeros] if g == 0 else [zeros, q_t], axis=0)
            s_loc = _dot(gk_s[pl.ds(k0, n), :], q_t) + band
            s_ctx = _dot(gkc_s[...], q_t)
            sink = sink_ref[layer, group * g + group - 1]
            for h in range(group - 2, -1, -1):
                sink = jnp.where(lane_head == h, sink_ref[layer, group * g + h], sink)
            (p_loc, p_ctx), x = _probabilities_t([s_loc, s_ctx], sink * LOG2E)
            o = _attend_t([(gv_s[g, :, pl.ds(k0, n)], p_loc), (gvc_s[g], p_ctx)], x)
            for pr in range(group // 2):
                pair = jnp.concatenate([o[:, (2 * pr) * BAND_BLOCK:(2 * pr + 1) * BAND_BLOCK],
                                        o[:, (2 * pr + 1) * BAND_BLOCK:(2 * pr + 2) * BAND_BLOCK]], axis=0)
                c0 = Y_GQA + (group // 2 * g + pr) * LANES
                y_ref[pl.ds(q0, BAND_BLOCK), c0:c0 + LANES] = pair.T.astype(BF16)

    for j in range(NA_HEADS // 2):
        nq_s[:, j * LANES:(j + 1) * LANES] = (col(C_NQ + j * LANES, LANES) * (ATTN_SCALE * LOG2E)).astype(BF16)
        nk_s[2 * j], nk_s[2 * j + 1] = _key_planes(col(C_NK + j * LANES, LANES).T, True, False)
        nv_s[2 * j], nv_s[2 * j + 1] = _value_planes(col(C_NV + j * LANES, LANES), True, False)
        nkc_s[j] = jnp.concatenate(_key_planes(cnk_ref[j * LANES:(j + 1) * LANES, :], True, False), axis=1)
        nvc_s[2 * j], nvc_s[2 * j + 1] = _value_planes(cnv_ref[j * LANES:(j + 1) * LANES, :].T, True, False)

    na_kinds, na_kind_of = _na_group_kinds()
    na_width = _na_table_width()

    def na_group(k):
        lo, n = _na_group_slab(k)
        q0, nq = k * NA_GROUP_ROWS * GRID_W, NA_GROUP_ROWS * GRID_W
        k0, nk = lo * GRID_W, n * GRID_W
        b0 = sum(kn for kn, _ in na_kinds[:na_kind_of[k]]) * GRID_W
        for j in range(NA_HEADS // 2):
            q = nq_s[q0:q0 + nq, j * LANES:(j + 1) * LANES]
            s_ctx = _dot(q, nkc_s[j])
            terms = []
            for odd in range(2):
                h = 2 * j + odd
                s_loc = _dot(q, nk_s[h, :, k0:k0 + nk]) + nab_ref[:, h * na_width + b0:h * na_width + b0 + nk]
                (p_loc, p_ctx), _ = _probabilities([s_loc, s_ctx[:, odd * PAST_LEN:(odd + 1) * PAST_LEN]])
                terms += [(p_loc, nv_s[h, k0:k0 + nk, :]), (p_ctx, nvc_s[h])]
            y_ref[q0:q0 + nq, Y_NA + j * LANES:Y_NA + (j + 1) * LANES] = _attend(terms).astype(BF16)

    ckv_b = _rms(col(C_MKV, MLA_KV_RANK), gkv_ref[...]).astype(BF16)
    cckv_b = cckv_ref[...].astype(BF16)
    q = _dot(_rms(col(C_MQ, MLA_Q_RANK), gq_ref[...]).astype(BF16), wuq_ref[...]) * (MLA_SCALE * LOG2E)
    for i in range(3):
        tile = q[:, i * LANES:(i + 1) * LANES]
        mq_s[i * LANES:(i + 1) * LANES, :] = (rope32(tile) if i == 2 else tile).T.astype(BF16)
    kr = jnp.where(_lane_lt((t, LANES), MLA_ROPE), rope32(col(C_MKR, LANES)), 0.0)
    kr_c = jnp.concatenate([ckr_ref[...], jnp.zeros((LANES - MLA_ROPE, PAST_LEN), F32)], axis=0).T
    for rows, ckv_x, kr_x in ((slice(0, t), ckv_b, kr), (slice(t, t + PAST_LEN), cckv_b, kr_c)):
        kn_all = _dot(ckv_x, wuk_ref[...])
        v_t_all = _dot_nt(wuvt_ref[...], ckv_x)
        for j in range(MLA_HEADS // 2):
            mk_s[j, rows, :] = jnp.concatenate([kn_all[:, j * LANES:(j + 1) * LANES], kr_x], axis=1).astype(BF16)
        for h in range(MLA_HEADS):
            mv_s[h, :, rows] = _value_plane_t(v_t_all[h * MLA_V:(h + 1) * MLA_V, :])

    tq = 256

    def mla_block(i, carry):
        q0 = pl.multiple_of(i * tq, tq)
        zeros = lambda r: jnp.zeros((r, tq), BF16)
        q_nope = lambda h: mq_s[h * MLA_NOPE:(h + 1) * MLA_NOPE, pl.ds(q0, tq)]
        q_rope = lambda h: mq_s[MLA_HEADS * MLA_NOPE + h * MLA_ROPE:MLA_HEADS * MLA_NOPE + (h + 1) * MLA_ROPE,
                                pl.ds(q0, tq)]
        pad = 2 * LANES - 2 * MLA_NOPE - MLA_ROPE
        for j in range(MLA_HEADS // 2):
            he, ho = 2 * j, 2 * j + 1
            q_e = jnp.concatenate([q_nope(he), zeros(MLA_NOPE), q_rope(he), zeros(pad)], axis=0)
            q_o = jnp.concatenate([zeros(MLA_NOPE), q_nope(ho), q_rope(ho), zeros(pad)], axis=0)
            s = _dot(mk_s[j], jnp.concatenate([q_e, q_o], axis=1))
            outs = []
            for odd in range(2):
                (p,), _ = _probabilities_t([s[:, odd * tq:(odd + 1) * tq]])
                outs.append(_attend_t([(mv_s[2 * j + odd], p)]))
            y_ref[pl.ds(q0, tq), Y_MLA + j * LANES:Y_MLA + (j + 1) * LANES] = (
                jnp.concatenate(outs, axis=0).T.astype(BF16))
        return carry

    span = BAND_BLOCK + 2 * WINDOW
    nb = t // BAND_BLOCK
    for i in range(nb):
        gqa_block(jnp.int32(i), WINDOW if i == 0 else 0, span - WINDOW if i == nb - 1 else span)
        if i % (nb // NA_GROUPS) == 0:
            na_group(i // (nb // NA_GROUPS))
        if i % (nb * tq // t) == 0:
            mla_block(jnp.int32(i // (nb * tq // t)), 0)


def _rope_tables(group, half):
    tok = np.arange(DEC_SEQ)
    pos = np.stack([tok // GRID_W, tok % GRID_W], axis=1).astype(np.float64)
    inv = ROPE_BASE ** (-np.arange(half, dtype=np.float64) / half)
    lane = np.arange(LANES) % group
    axis = lane // (2 * half)
    within = lane % (2 * half)
    ang = pos[:, axis] * inv[within % half][None, :]
    cos, sin = np.cos(ang), np.sin(ang)
    upper = (within >= half)[None, :]
    sin_hi = np.where(upper, sin, 0.0)
    sin_lo = np.where(upper, 0.0, -sin)
    return tuple(jnp.asarray(a, dtype=F32) for a in (cos, sin_hi, sin_lo))


def _band_mask():
    c = np.arange(BAND_BLOCK + 2 * WINDOW)[:, None]
    i = np.arange(BAND_BLOCK)[None, :]
    ok = (c >= i) & (c <= i + 2 * WINDOW)
    m = np.where(ok, 0.0, NEG_INF)
    return jnp.asarray(np.concatenate([m] * (GQA_HEADS // GQA_KV_HEADS), axis=1), dtype=F32)


def _lat_mixer_call(x, mods, g_attn, w_main, sink, w_conv, g_q, w_uq, g_kv, w_uk, w_uv_t, nab, caches, layer, name):
    n, t = DEC_BATCH, DEC_SEQ
    one = pl.Buffered(1)
    const = lambda shape: pl.BlockSpec(shape, lambda s: (0,) * len(shape), pipeline_mode=one)
    cache = lambda shape: pl.BlockSpec((None, None) + shape, lambda s: (s, layer, 0, 0), pipeline_mode=one)
    rope_g = _rope_tables(HEAD_DIM, HEAD_DIM // 4)
    rope_m = _rope_tables(MLA_ROPE, MLA_ROPE // 4)
    span = BAND_BLOCK + 2 * WINDOW
    keys = t + PAST_LEN
    scratch = [
        pltpu.VMEM((t, D_MODEL), BF16),
        pltpu.VMEM((GQA_HEADS * HEAD_DIM, t), BF16),
        pltpu.VMEM((t, LANES), BF16),
        pltpu.VMEM((GQA_KV_HEADS, HEAD_DIM + ONES_ROWS, t), BF16),
        pltpu.VMEM((PAST_LEN, LANES), BF16),
        pltpu.VMEM((GQA_KV_HEADS, HEAD_DIM + ONES_ROWS, PAST_LEN), BF16),
        pltpu.VMEM((t, NA_HEADS // 2 * LANES), BF16),
        pltpu.VMEM((NA_HEADS, LANES, t), BF16),
        pltpu.VMEM((NA_HEADS, t, 2 * LANES), BF16),
        pltpu.VMEM((NA_HEADS // 2, LANES, 2 * PAST_LEN), BF16),
        pltpu.VMEM((NA_HEADS, PAST_LEN, 2 * LANES), BF16),
        pltpu.VMEM((3 * LANES, t), BF16),
        pltpu.VMEM((MLA_HEADS // 2, keys, 2 * LANES), BF16),
        pltpu.VMEM((MLA_HEADS, MLA_V + ONES_ROWS, keys), BF16),
    ]
    return pl.pallas_call(
        functools.partial(_lat_mixer_kernel, layer=layer),
        grid=(n,),
        in_specs=[
            pl.BlockSpec(memory_space=pltpu.SMEM),
            pl.BlockSpec((None, t, D_MODEL), lambda s: (s, 0, 0), pipeline_mode=one),
            _mod_spec(layer, False, 1),
            _layer_spec((1, D_MODEL), layer, 1),
            _layer_spec((MAIN_COLS, D_MODEL), layer, 1),
            _layer_spec((CONV_K, CONV_WIDTH), layer, 1),
            _layer_spec((1, MLA_Q_RANK), layer, 1),
            _layer_spec((MLA_Q_RANK, 3 * LANES), layer, 1),
            _layer_spec((1, MLA_KV_RANK), layer, 1),
            _layer_spec((MLA_KV_RANK, 2 * LANES), layer, 1),
            _layer_spec((2 * LANES, MLA_KV_RANK), layer, 1),
        ] + [const((t, LANES))] * 6 + [
            const((span, GQA_HEADS // GQA_KV_HEADS * BAND_BLOCK)),
            _layer_spec((NA_GROUP_ROWS * GRID_W, NA_HEADS * _na_table_width()), layer, 1),
            cache((2 * HEAD_DIM, PAST_LEN)), cache((2 * HEAD_DIM, PAST_LEN)),
            cache((4 * HEAD_DIM, PAST_LEN)), cache((4 * HEAD_DIM, PAST_LEN)),
            cache((PAST_LEN, MLA_KV_RANK)), cache((MLA_ROPE, PAST_LEN)),
        ],
        out_specs=pl.BlockSpec((None, t, Y_COLS), lambda s: (s, 0, 0)),
        out_shape=jax.ShapeDtypeStruct((n, t, Y_COLS), BF16),
        scratch_shapes=scratch,
        compiler_params=pltpu.CompilerParams(
            dimension_semantics=("arbitrary",), vmem_limit_bytes=VMEM_LIMIT),
        name=name,
    )(sink, x, mods, g_attn, w_main, w_conv, g_q, w_uq, g_kv, w_uk, w_uv_t, *rope_g, *rope_m, _band_mask(), nab, *caches)


def _post_kernel(x_ref, y_ref, mod_ref, ga_ref, gm_ref, gf_ref, wg_ref, wb_ref, wo_ref, w1_ref, w2_ref,
                 o_ref, *, final):
    d = D_MODEL
    mod = lambda i: mod_ref[:, i * d:(i + 1) * d]
    bounds = (Y_CONV, Y_GQA, Y_NA, Y_MLA, Y_COLS)
    for r0 in range(0, x_ref.shape[0], POST_CHAIN_ROWS):
        rows = slice(r0, r0 + POST_CHAIN_ROWS)
        x = x_ref[rows, :]
        h = _modulated_norm(x, ga_ref[...], mod_ref, 0, 1).astype(BF16)
        merged = None
        for i in range(N_BRANCH):
            lo, hi = bounds[i], bounds[i + 1]
            gate = jax.nn.sigmoid(_dot_nt(h, wg_ref[0, i * d:(i + 1) * d, :]))
            term = gate * _dot(y_ref[rows, lo:hi], wb_ref[lo:hi, :])
            merged = term if merged is None else merged + term
        x = x + mod(2) * _dot(merged.astype(BF16), wo_ref[...])
        h = _modulated_norm(x, gm_ref[...], mod_ref, 3, 4).astype(BF16)
        mlp = None
        for c0 in range(0, D_FF, d):
            f = jnp.square(jnp.maximum(_dot(h, w1_ref[:, c0:c0 + d]), 0.0)).astype(BF16)
            term = _dot(f, w2_ref[c0:c0 + d, :])
            mlp = term if mlp is None else mlp + term
        x = x + mod(5) * mlp
        o_ref[rows, :] = _rms(x, gf_ref[...]) if final else x


def _post_call(x, y, mods, g_attn, g_mlp, g_final, w_gates, w_br, w_o, w_ff1, w_ff2, layer, ctx, name):
    n, t, _ = x.shape
    tm = 2 * POST_CHAIN_ROWS
    return pl.pallas_call(
        functools.partial(_post_kernel, final=layer == DEPTH - 1),
        grid=(n, t // tm),
        in_specs=[
            pl.BlockSpec((None, tm, D_MODEL), lambda s, i: (s, i, 0)),
            pl.BlockSpec((None, tm, Y_COLS), lambda s, i: (s, i, 0)),
            _mod_spec(layer, ctx, 2),
            _layer_spec((1, D_MODEL), layer, 2),
            _layer_spec((1, D_MODEL), layer, 2),
            pl.BlockSpec((1, D_MODEL), lambda s, i: (0, 0)),
            pl.BlockSpec((pl.Element(1), pl.Element(N_BRANCH * D_MODEL), pl.Element(D_MODEL)),
                         lambda s, i: (layer, GATE_COL0, 0), pipeline_mode=pl.Buffered(1)),
            _layer_spec((Y_COLS, D_MODEL), layer, 2),
            _layer_spec((D_MODEL, D_MODEL), layer, 2),
            _layer_spec((D_MODEL, D_FF), layer, 2),
            _layer_spec((D_FF, D_MODEL), layer, 2),
        ],
        out_specs=pl.BlockSpec((None, tm, D_MODEL), lambda s, i: (s, i, 0)),
        out_shape=jax.ShapeDtypeStruct((n, t, D_MODEL), F32),
        compiler_params=pltpu.CompilerParams(
            dimension_semantics=("arbitrary", "arbitrary"), vmem_limit_bytes=VMEM_LIMIT),
        name=name,
    )(x, y, mods, g_attn, g_mlp, g_final, w_gates, w_br, w_o, w_ff1, w_ff2)


def kernel(x_prompt, x_sample, cache_gqa_k, cache_gqa_v, cache_na_k, cache_na_v, cache_mla_ckv, cache_mla_krope, c, c_ctx, w_mod, b_mod, g_attn, g_mlp, w_in, w_conv, gqa_sink, na_rpb, mla_g_q, mla_w_uq, mla_g_kv, mla_w_ukv, w_branch_conv, w_branch_gqa, w_branch_na, w_branch_mla, w_o, w_ff1, w_ff2, g_final):
    w_main = w_gates = jnp.transpose(w_in, (0, 2, 1)).astype(BF16)
    uq = mla_w_uq.reshape(DEPTH, MLA_Q_RANK, MLA_HEADS, MLA_NOPE + MLA_ROPE)
    w_uq = jnp.concatenate([uq[..., :MLA_NOPE].reshape(DEPTH, MLA_Q_RANK, -1),
                            uq[..., MLA_NOPE:].reshape(DEPTH, MLA_Q_RANK, -1)], axis=-1).astype(BF16)
    ukv = mla_w_ukv.reshape(DEPTH, MLA_KV_RANK, MLA_HEADS, MLA_NOPE + MLA_V)
    w_uk = ukv[..., :MLA_NOPE].reshape(DEPTH, MLA_KV_RANK, -1).astype(BF16)
    w_uv = ukv[..., MLA_NOPE:].reshape(DEPTH, MLA_KV_RANK, -1).astype(BF16)
    w_uk_t, w_uv_t = jnp.transpose(w_uk, (0, 2, 1)), jnp.transpose(w_uv, (0, 2, 1))
    w_br = jnp.concatenate([w_branch_conv, w_branch_gqa, w_branch_na, w_branch_mla], axis=1).astype(BF16)
    w_o_b, w_ff1_b, w_ff2_b = w_o.astype(BF16), w_ff1.astype(BF16), w_ff2.astype(BF16)
    g_a, g_m = g_attn[:, None, :], g_mlp[:, None, :]
    g_q, g_kv, g_f = mla_g_q[:, None, :], mla_g_kv[:, None, :], g_final[None, :]

    c16 = jnp.concatenate([c, c_ctx[None, :], jnp.zeros((MOD_ROWS - DEC_BATCH - 1, D_MODEL), F32)], axis=0)
    mods = _mod_call(c16, w_mod, b_mod).reshape(DEPTH, MOD_ROWS, 1, 6 * D_MODEL)
    nab = _nab_call(na_rpb)

    heads_t = lambda a: jnp.transpose(a, (0, 1, 3, 4, 2)).reshape(a.shape[0], DEPTH, -1, a.shape[2])
    caches = (heads_t(cache_gqa_k), heads_t(cache_gqa_v), heads_t(cache_na_k), heads_t(cache_na_v),
              cache_mla_ckv, jnp.transpose(cache_mla_krope, (0, 1, 3, 2)))

    h_ctx, h_lat = x_prompt, x_sample
    states = None
    for l in range(DEPTH):
        mixer_w = (gqa_sink, w_conv, g_q, w_uq, g_kv)
        post_w = (g_a, g_m, g_f, w_gates, w_br, w_o_b, w_ff1_b, w_ff2_b)

        y, states = _ctx_mixer_call(h_ctx, mods, g_a, w_main, *mixer_w, w_uk_t, w_uv, states, l, f"mixer_ctx_{l}")
        flat = lambda a: a.reshape(1, BATCH * SEQ, a.shape[-1])
        h_ctx = _post_call(flat(h_ctx), flat(y), mods, *post_w, l, True, f"post_ctx_{l}").reshape(BATCH, SEQ, D_MODEL)

        y = _lat_mixer_call(h_lat, mods, g_a, w_main, *mixer_w, w_uk, w_uv_t, nab, caches, l, f"mixer_lat_{l}")
        h_lat = _post_call(h_lat, y, mods, *post_w, l, False, f"post_lat_{l}")

    def heads_out(a, heads):
        return jnp.transpose(a.reshape(BATCH, DEPTH, heads, HEAD_DIM, SEQ), (0, 1, 4, 2, 3))

    kg, vg, kn, vn, ckv, kr = states
    return (h_ctx, h_lat, heads_out(kg, GQA_KV_HEADS), heads_out(vg, GQA_KV_HEADS),
            heads_out(kn, NA_HEADS), heads_out(vn, NA_HEADS), ckv, jnp.transpose(kr, (0, 1, 3, 2)))
```

```python
import functools
import math

import numpy as np
import jax
import jax.numpy as jnp
from jax import lax
from jax.experimental import pallas as pl
from jax.experimental.pallas import tpu as pltpu

D_MODEL = 1024
BATCH = 32
SEQ = 256
DEPTH = 2
DEC_BATCH = 8
DEC_SEQ = 1024
PAST_LEN = 512
GRID_W = 64
GRID_ROWS = DEC_SEQ // GRID_W
HEAD_DIM = 64
CONV_WIDTH = 256
CONV_K = 3
GQA_HEADS = 8
GQA_KV_HEADS = 2
WINDOW = 128
BAND_BLOCK = 128
NA_HEADS = 4
NA_WIN_H = 8
NA_WIN_W = 16
MLA_HEADS = 4
MLA_Q_RANK = 256
MLA_KV_RANK = 128
MLA_NOPE = 64
MLA_ROPE = 32
MLA_V = 64
D_FF = 4 * D_MODEL
N_BRANCH = 4
ROPE_BASE = 10000.0
EPS = 1e-6
NEG_INF = -1e30
LOG2E = math.log2(math.e)
ATTN_SCALE = HEAD_DIM ** -0.5
MLA_SCALE = (MLA_NOPE + MLA_ROPE) ** -0.5

LANES = 128
MOD_ROWS = 16
CTX_MOD_ROW = DEC_BATCH

C_CB, C_CC, C_CV = 0, 256, 512
C_GQ, C_GK, C_GV = 768, 1280, 1408
C_NQ, C_NK, C_NV = 1536, 1792, 2048
C_MQ, C_MKV, C_MKR = 2304, 2560, 2688
MAIN_COLS = 2816
GATE_COL0 = 2720
Y_CONV, Y_GQA, Y_NA, Y_MLA = 0, 256, 768, 1024
Y_COLS = 1280
NA_GROUPS = 4
NA_GROUP_ROWS = GRID_ROWS // NA_GROUPS

VMEM_LIMIT = 56 * 1024 * 1024
POST_CHAIN_ROWS = 512
CTX_SEQS = 4

F32 = jnp.float32
BF16 = jnp.bfloat16


def _dot(a, b):
    return jnp.dot(a, b, preferred_element_type=F32)


def _dot_nt(a, b):
    return lax.dot_general(a, b, (((1,), (1,)), ((), ())), preferred_element_type=F32)


def _rms(x, g):
    return x * lax.rsqrt(jnp.mean(x * x, axis=-1, keepdims=True) + EPS) * g


def _lane_lt(shape, n):
    return lax.broadcasted_iota(jnp.int32, shape, len(shape) - 1) < n


def _row_lt(shape, n):
    return lax.broadcasted_iota(jnp.int32, shape, 0) < n


def _row_group(shape, lo, hi):
    row = lax.broadcasted_iota(jnp.int32, shape, 0)
    return (row >= lo) & (row < hi)


def _short_conv(cb, cc, cv, w):
    u = cc * cv
    t = u.shape[0]
    row = lax.broadcasted_iota(jnp.int32, u.shape, 0)
    prev = jnp.where(row == 0, 0.0, pltpu.roll(u, 1, 0))
    nxt = jnp.where(row == t - 1, 0.0, pltpu.roll(u, t - 1, 0))
    return cb * (prev * w[0:1, :] + u * w[1:2, :] + nxt * w[2:3, :])


def _rope(x, cos, sin_hi, sin_lo, half):
    n = x.shape[-1]
    return x * cos + pltpu.roll(x, n - half, 1) * sin_lo + pltpu.roll(x, half, 1) * sin_hi


def _mod_kernel(c_ref, w_ref, b_ref, o_ref):
    c = c_ref[...]
    s = c * jax.nn.sigmoid(c)
    o_ref[...] = _dot(s.astype(BF16), w_ref[...].astype(BF16)) + b_ref[...]


def _mod_call(c16, w_mod, b_mod):
    tn = 1536
    return pl.pallas_call(
        _mod_kernel,
        grid=(DEPTH, 6 * D_MODEL // tn),
        in_specs=[
            pl.BlockSpec((MOD_ROWS, D_MODEL), lambda l, j: (0, 0)),
            pl.BlockSpec((None, D_MODEL, tn), lambda l, j: (l, 0, j)),
            pl.BlockSpec((None, 1, tn), lambda l, j: (l, 0, j)),
        ],
        out_specs=pl.BlockSpec((None, MOD_ROWS, tn), lambda l, j: (l, 0, j)),
        out_shape=jax.ShapeDtypeStruct((DEPTH, MOD_ROWS, 6 * D_MODEL), F32),
        compiler_params=pltpu.CompilerParams(
            dimension_semantics=("arbitrary", "arbitrary"), vmem_limit_bytes=VMEM_LIMIT),
        name="adaln_mod",
    )(c16, w_mod, b_mod.reshape(DEPTH, 1, 6 * D_MODEL))


def _na_window_start(r):
    return min(max(r - NA_WIN_H // 2, 0), GRID_ROWS - NA_WIN_H)


def _na_group_slab(k):
    starts = [_na_window_start(r) for r in range(k * NA_GROUP_ROWS, (k + 1) * NA_GROUP_ROWS)]
    lo = min(starts) // 2 * 2
    n = -(-(max(starts) + NA_WIN_H - lo) // 4) * 4
    assert lo + n <= GRID_ROWS
    return lo, n


def _na_group_kinds():
    kinds, of_group = [], []
    for k in range(NA_GROUPS):
        lo, n = _na_group_slab(k)
        layout = {}
        for dr in range(NA_GROUP_ROWS):
            r = k * NA_GROUP_ROWS + dr
            r0 = _na_window_start(r)
            for a in range(n):
                inside = r0 <= lo + a < r0 + NA_WIN_H
                layout[(dr, a)] = lo + a - r + NA_WIN_H - 1 if inside else None
        if (n, layout) not in kinds:
            kinds.append((n, layout))
        of_group.append(kinds.index((n, layout)))
    return kinds, of_group


def _na_table_width():
    return sum(n for n, _ in _na_group_kinds()[0]) * GRID_W


def _nab_kernel(rpb_ref, o_ref):
    l = pl.program_id(0)
    c = lax.broadcasted_iota(jnp.int32, (GRID_W, GRID_W), 0)
    w = lax.broadcasted_iota(jnp.int32, (GRID_W, GRID_W), 1)
    dc = w - c + (NA_WIN_W - 1)
    c0 = jnp.clip(c - NA_WIN_W // 2, 0, GRID_W - NA_WIN_W)
    outside = (w < c0) | (w >= c0 + NA_WIN_W)
    n_dr, n_dc = 2 * NA_WIN_H - 1, 2 * NA_WIN_W - 1
    kinds, _ = _na_group_kinds()
    width = _na_table_width()

    def put(h, d, tile):
        col0 = h * width
        for n, layout in kinds:
            for (dr, a), want in layout.items():
                if want == d:
                    o_ref[dr * GRID_W:(dr + 1) * GRID_W, col0 + a * GRID_W:col0 + (a + 1) * GRID_W] = tile
            col0 += n * GRID_W

    for h in range(NA_HEADS):
        put(h, None, jnp.full((GRID_W, GRID_W), NEG_INF, F32))
        for d in range(n_dr):
            base = ((l * NA_HEADS + h) * n_dr + d) * n_dc
            tile = jnp.full((GRID_W, GRID_W), NEG_INF, F32)
            for j in range(n_dc):
                tile = jnp.where(dc == j, rpb_ref[base + j] * LOG2E, tile)
            put(h, d, jnp.where(outside, NEG_INF, tile))


def _nab_call(na_rpb):
    shape = (NA_GROUP_ROWS * GRID_W, NA_HEADS * _na_table_width())
    return pl.pallas_call(
        _nab_kernel,
        grid=(DEPTH,),
        in_specs=[pl.BlockSpec(memory_space=pltpu.SMEM)],
        out_specs=pl.BlockSpec((None,) + shape, lambda l: (l, 0, 0)),
        out_shape=jax.ShapeDtypeStruct((DEPTH,) + shape, F32),
        compiler_params=pltpu.CompilerParams(dimension_semantics=("arbitrary",), vmem_limit_bytes=VMEM_LIMIT),
        name="na_bias_tables",
    )(na_rpb.reshape(-1))


def _mod_spec(layer, ctx, n_grid):
    if n_grid == 1:
        index = (lambda s: (layer, CTX_MOD_ROW, 0, 0)) if ctx else (lambda s: (layer, s, 0, 0))
    else:
        index = (lambda s, i: (layer, CTX_MOD_ROW, 0, 0)) if ctx else (lambda s, i: (layer, s, 0, 0))
    return pl.BlockSpec((None, None, 1, 6 * D_MODEL), index)


def _layer_spec(shape, layer, n_grid):
    zeros = (0,) * len(shape)
    index = (lambda s: (layer,) + zeros) if n_grid == 1 else (lambda s, i: (layer,) + zeros)
    return pl.BlockSpec((None,) + tuple(shape), index, pipeline_mode=pl.Buffered(1))


_COLUMN_GROUPS = ((C_CB, C_GQ), (C_GQ, C_NQ), (C_NQ, C_MQ), (C_MQ, MAIN_COLS))


def _modulated_norm(x, g, mod_ref, shift, scale):
    d = D_MODEL
    return _rms(x, g) * (1.0 + mod_ref[:, scale * d:(scale + 1) * d]) + mod_ref[:, shift * d:(shift + 1) * d]


def _projected_columns(h_ref, w_ref):
    cache = {}

    def col(c, n):
        lo, hi = next(g for g in _COLUMN_GROUPS if g[0] <= c and c + n <= g[1])
        if lo not in cache:
            cache.clear()
            cache[lo] = _dot_nt(h_ref[...], w_ref[lo:hi, :])
        return cache[lo][:, c - lo:c - lo + n]

    return col


def _key_planes(k_t, even_first, odd_first):
    lo = _row_lt(k_t.shape, HEAD_DIM)
    swapped = None
    if not even_first or odd_first:
        swapped = pltpu.roll(k_t, HEAD_DIM, 0)
    top = jnp.where(lo, k_t if even_first else swapped, 0.0)
    bot = jnp.where(lo, 0.0, swapped if odd_first else k_t)
    return top.astype(BF16), bot.astype(BF16)


def _value_planes(pair, even_first, odd_first):
    lo = _lane_lt(pair.shape, HEAD_DIM)
    swapped = None
    if not even_first or odd_first:
        swapped = pltpu.roll(pair, HEAD_DIM, 1)
    top = jnp.where(lo, pair if even_first else swapped, 0.0)
    bot = jnp.where(lo, 0.0, swapped if odd_first else pair)
    one_e = jnp.where(lo, 1.0, 0.0)
    return (jnp.concatenate([top, one_e], axis=1).astype(BF16),
            jnp.concatenate([bot, 1.0 - one_e], axis=1).astype(BF16))


def _probabilities(parts, sink=None):
    m = parts[0].max(axis=-1, keepdims=True)
    for p in parts[1:]:
        m = jnp.maximum(m, p.max(axis=-1, keepdims=True))
    if sink is not None:
        m = jnp.maximum(m, sink)
    probs = [jnp.exp2((p - m).astype(BF16)) for p in parts]
    return probs, (None if sink is None else jnp.exp2(sink - m))


def _attend(terms, sink_e=None, sink_o=None):
    o = None
    for p, v in terms:
        t = _dot(p, v)
        o = t if o is None else o + t
    den = o[:, LANES:]
    if sink_e is not None:
        den = den + jnp.where(_lane_lt(den.shape, HEAD_DIM), sink_e, sink_o)
    return o[:, :LANES] / den


ONES_ROWS = 16


def _value_plane_t(v_t):
    return jnp.concatenate([v_t, jnp.ones((ONES_ROWS, v_t.shape[1]), v_t.dtype)], axis=0).astype(BF16)


def _probabilities_t(parts, sink=None):
    m = parts[0].max(axis=0, keepdims=True)
    for p in parts[1:]:
        m = jnp.maximum(m, p.max(axis=0, keepdims=True))
    if sink is not None:
        m = jnp.maximum(m, sink)
    probs = [jnp.exp2((p - m).astype(BF16)) for p in parts]
    return probs, (None if sink is None else jnp.exp2(sink - m))


def _attend_t(terms, sink_num=None):
    o = None
    for v_t, p in terms:
        t = _dot(v_t, p)
        o = t if o is None else o + t
    den = o[HEAD_DIM:HEAD_DIM + 1, :]
    if sink_num is not None:
        den = den + sink_num
    return o[0:HEAD_DIM, :] / den


def _mla_key_plane(kn_t_pair, kr4_t, j, odd):
    h = 2 * j + odd
    lo = _row_lt(kn_t_pair.shape, HEAD_DIM)
    nope = jnp.where(lo, 0.0, kn_t_pair) if odd else jnp.where(lo, kn_t_pair, 0.0)
    rope = jnp.where(_row_group(kr4_t.shape, h * MLA_ROPE, (h + 1) * MLA_ROPE), kr4_t, 0.0)
    return jnp.concatenate([nope, rope], axis=0).astype(BF16)


def _ctx_mixer_kernel(sink_ref, x_ref, mod_ref, ga_ref, win_ref, wc_ref, gq_ref, wuq_ref, gkv_ref, wukt_ref, wuv_ref,
                      *rest, layer, first):
    y_ref, kg_ref, vg_ref, kn_ref, vn_ref, ckv_ref, kr_ref, h_s = rest[-8:]
    t = SEQ
    for b in range(CTX_SEQS):
        h_s[b * t:(b + 1) * t, :] = _modulated_norm(x_ref[b], ga_ref[...], mod_ref, 0, 1).astype(BF16)
    col_all = _projected_columns(h_s, win_ref)

    def put_state(ref, b, val):
        if first:
            for d in range(DEPTH):
                ref[b, d] = val if d == layer else jnp.zeros_like(val)
        else:
            ref[b] = val

    def conv(b, col):
        y_ref[b, :, Y_CONV:Y_CONV + CONV_WIDTH] = _short_conv(
            col(C_CB, CONV_WIDTH), col(C_CC, CONV_WIDTH), col(C_CV, CONV_WIDTH), wc_ref[...]).astype(BF16)

    def gqa(b, col):
        kg_t, vpair = col(C_GK, LANES).T, col(C_GV, LANES)
        put_state(kg_ref, b, kg_t)
        put_state(vg_ref, b, vpair.T)
        row2 = lax.broadcasted_iota(jnp.int32, (2 * t, 1), 0) < t
        for g in range(GQA_KV_HEADS):
            ke, ko = _key_planes(kg_t, g == 0, g == 0)
            ve, vo = _value_planes(vpair, g == 0, g == 0)
            q = jnp.concatenate([col(C_GQ + (2 * g) * LANES, LANES), col(C_GQ + (2 * g + 1) * LANES, LANES)], axis=0)
            q = (q * (ATTN_SCALE * LOG2E)).astype(BF16)
            s = _dot(q, jnp.concatenate([ke, ko], axis=1))
            sink = [jnp.where(row2, sink_ref[layer, 4 * g + odd], sink_ref[layer, 4 * g + 2 + odd]) * LOG2E
                    for odd in range(2)]
            (pe,), xe = _probabilities([s[:, 0:t]], sink[0])
            (po,), xo = _probabilities([s[:, t:2 * t]], sink[1])
            o = _attend([(pe, ve), (po, vo)], xe, xo).astype(BF16)
            c0 = Y_GQA + (2 * g) * LANES
            y_ref[b, :, c0:c0 + LANES] = o[0:t]
            y_ref[b, :, c0 + LANES:c0 + 2 * LANES] = o[t:2 * t]

    def na(b, col):
        kn_t = col(C_NK, 2 * LANES).T
        put_state(kn_ref, b, kn_t)
        put_state(vn_ref, b, col(C_NV, 2 * LANES).T)
        for j in range(NA_HEADS // 2):
            ke, ko = _key_planes(kn_t[j * LANES:(j + 1) * LANES, :], True, False)
            ve, vo = _value_planes(col(C_NV + j * LANES, LANES), True, False)
            q = (col(C_NQ + j * LANES, LANES) * (ATTN_SCALE * LOG2E)).astype(BF16)
            s = _dot(q, jnp.concatenate([ke, ko], axis=1))
            (pe,), _ = _probabilities([s[:, 0:t]])
            (po,), _ = _probabilities([s[:, t:2 * t]])
            y_ref[b, :, Y_NA + j * LANES:Y_NA + (j + 1) * LANES] = _attend([(pe, ve), (po, vo)]).astype(BF16)

    def mla(b, col):
        ckv = _rms(col(C_MKV, MLA_KV_RANK), gkv_ref[...])
        put_state(ckv_ref, b, ckv)
        kr_t = col(C_MKR, LANES).T[0:MLA_ROPE, :]
        put_state(kr_ref, b, kr_t)
        ckv_b = ckv.astype(BF16)
        q = _dot(_rms(col(C_MQ, MLA_Q_RANK), gq_ref[...]).astype(BF16), wuq_ref[...]) * (MLA_SCALE * LOG2E)
        kn_t_all = _dot_nt(wukt_ref[...], ckv_b)
        v_all = _dot(ckv_b, wuv_ref[...])
        kr4_t = jnp.concatenate([kr_t] * MLA_HEADS, axis=0)
        q_rope = q[:, 2 * LANES:3 * LANES]
        for j in range(MLA_HEADS // 2):
            qj = jnp.concatenate([q[:, j * LANES:(j + 1) * LANES], q_rope], axis=1).astype(BF16)
            kn_t_pair = kn_t_all[j * LANES:(j + 1) * LANES, :]
            keys = jnp.concatenate(
                [_mla_key_plane(kn_t_pair, kr4_t, j, 0), _mla_key_plane(kn_t_pair, kr4_t, j, 1)], axis=1)
            s = _dot(qj, keys)
            ve, vo = _value_planes(v_all[:, j * LANES:(j + 1) * LANES], True, False)
            (pe,), _ = _probabilities([s[:, 0:t]])
            (po,), _ = _probabilities([s[:, t:2 * t]])
            y_ref[b, :, Y_MLA + j * LANES:Y_MLA + (j + 1) * LANES] = _attend([(pe, ve), (po, vo)]).astype(BF16)

    for mixer in (conv, gqa, na, mla):
        for b in range(CTX_SEQS):
            mixer(b, lambda c, n, b=b: col_all(c, n)[b * t:(b + 1) * t])


def _ctx_mixer_call(x, mods, g_attn, w_main, sink, w_conv, g_q, w_uq, g_kv, w_uk_t, w_uv, prev_states, layer, name):
    n, nb = BATCH, CTX_SEQS
    state_shapes = ((2 * HEAD_DIM, SEQ), (2 * HEAD_DIM, SEQ), (4 * HEAD_DIM, SEQ), (4 * HEAD_DIM, SEQ),
                    (SEQ, MLA_KV_RANK), (MLA_ROPE, SEQ))
    first = prev_states is None
    if first:
        state_spec = lambda shape: pl.BlockSpec((nb, DEPTH) + shape, lambda s: (s, 0, 0, 0))
    else:
        state_spec = lambda shape: pl.BlockSpec((nb, None) + shape, lambda s: (s, layer, 0, 0))
    in_specs = [
        pl.BlockSpec(memory_space=pltpu.SMEM),
        pl.BlockSpec((nb, SEQ, D_MODEL), lambda s: (s, 0, 0)),
        _mod_spec(layer, True, 1),
        _layer_spec((1, D_MODEL), layer, 1),
        _layer_spec((MAIN_COLS, D_MODEL), layer, 1),
        _layer_spec((CONV_K, CONV_WIDTH), layer, 1),
        _layer_spec((1, MLA_Q_RANK), layer, 1),
        _layer_spec((MLA_Q_RANK, 3 * LANES), layer, 1),
        _layer_spec((1, MLA_KV_RANK), layer, 1),
        _layer_spec((2 * LANES, MLA_KV_RANK), layer, 1),
        _layer_spec((MLA_KV_RANK, 2 * LANES), layer, 1),
    ]
    args = [sink, x, mods, g_attn, w_main, w_conv, g_q, w_uq, g_kv, w_uk_t, w_uv]
    aliases = {}
    if not first:
        for i, st in enumerate(prev_states):
            aliases[len(args)] = 1 + i
            in_specs.append(pl.BlockSpec(memory_space=pl.ANY))
            args.append(st)
    outs = pl.pallas_call(
        functools.partial(_ctx_mixer_kernel, layer=layer, first=first),
        grid=(n // nb,),
        in_specs=in_specs,
        out_specs=[pl.BlockSpec((nb, SEQ, Y_COLS), lambda s: (s, 0, 0))] + [state_spec(s) for s in state_shapes],
        out_shape=[jax.ShapeDtypeStruct((n, SEQ, Y_COLS), BF16)]
        + [jax.ShapeDtypeStruct((n, DEPTH) + s, F32) for s in state_shapes],
        input_output_aliases=aliases,
        scratch_shapes=[pltpu.VMEM((nb * SEQ, D_MODEL), BF16)],
        compiler_params=pltpu.CompilerParams(
            dimension_semantics=("arbitrary",), vmem_limit_bytes=VMEM_LIMIT),
        name=name,
    )(*args)
    return outs[0], outs[1:]


def _lat_mixer_kernel(sink_ref, x_ref, mod_ref, ga_ref, win_ref, wc_ref, gq_ref, wuq_ref, gkv_ref, wuk_ref, wuvt_ref,
                      rc_ref, rsh_ref, rsl_ref, mc_ref, msh_ref, msl_ref, band_ref, nab_ref,
                      cgk_ref, cgv_ref, cnk_ref, cnv_ref, cckv_ref, ckr_ref,
                      y_ref,
                      h_s, gq_s, gk_s, gv_s, gkc_s, gvc_s, nq_s, nk_s, nv_s, nkc_s, nvc_s, mq_s, mk_s, mv_s, *, layer):
    t = DEC_SEQ
    h_s[...] = _modulated_norm(x_ref[...], ga_ref[...], mod_ref, 0, 1).astype(BF16)
    col = _projected_columns(h_s, win_ref)
    rope64 = lambda x: _rope(x, rc_ref[...], rsh_ref[...], rsl_ref[...], HEAD_DIM // 4)
    rope32 = lambda x: _rope(x, mc_ref[...], msh_ref[...], msl_ref[...], MLA_ROPE // 4)

    y_ref[:, Y_CONV:Y_CONV + CONV_WIDTH] = _short_conv(
        col(C_CB, CONV_WIDTH), col(C_CC, CONV_WIDTH), col(C_CV, CONV_WIDTH), wc_ref[...]).astype(BF16)

    group = GQA_HEADS // GQA_KV_HEADS
    for j in range(GQA_HEADS // 2):
        q_pair = rope64(col(C_GQ + j * LANES, LANES)) * (ATTN_SCALE * LOG2E)
        gq_s[j * LANES:(j + 1) * LANES, :] = q_pair.T.astype(BF16)
    gk_s[...] = rope64(col(C_GK, LANES)).astype(BF16)
    gkc_s[...] = cgk_ref[...].T.astype(BF16)
    v_t = col(C_GV, LANES).T
    for g in range(GQA_KV_HEADS):
        gv_s[g] = _value_plane_t(v_t[g * HEAD_DIM:(g + 1) * HEAD_DIM, :])
        gvc_s[g] = _value_plane_t(cgv_ref[g * HEAD_DIM:(g + 1) * HEAD_DIM, :])

    def gqa_block(b, c_lo, c_hi):
        q0 = pl.multiple_of(b * BAND_BLOCK, BAND_BLOCK)
        k0 = pl.multiple_of(q0 + (c_lo - WINDOW), BAND_BLOCK)
        n = c_hi - c_lo
        band = band_ref[c_lo:c_hi, :]
        lane_head = lax.shift_right_logical(
            lax.broadcasted_iota(jnp.int32, (1, group * BAND_BLOCK), 1), BAND_BLOCK.bit_length() - 1)
        zeros = jnp.zeros((HEAD_DIM, group * BAND_BLOCK), BF16)
        for g in range(GQA_KV_HEADS):
            q_t = jnp.concatenate(
                [gq_s[(group * g + h) * HEAD_DIM:(group * g + h + 1) * HEAD_DIM, pl.ds(q0, BAND_BLOCK)]
                 for h in range(group)], axis=1)
            q_t = jnp.concatenate([q_t, zeros] if g == 0 else [zeros, q_t], axis=0)
            s_loc = _dot(gk_s[pl.ds(k0, n), :], q_t) + band
            s_ctx = _dot(gkc_s[...], q_t)
            sink = sink_ref[layer, group * g + group - 1]
            for h in range(group - 2, -1, -1):
                sink = jnp.where(lane_head == h, sink_ref[layer, group * g + h], sink)
            (p_loc, p_ctx), x = _probabilities_t([s_loc, s_ctx], sink * LOG2E)
            o = _attend_t([(gv_s[g, :, pl.ds(k0, n)], p_loc), (gvc_s[g], p_ctx)], x)
            for pr in range(group // 2):
                pair = jnp.concatenate([o[:, (2 * pr) * BAND_BLOCK:(2 * pr + 1) * BAND_BLOCK],
                                        o[:, (2 * pr + 1) * BAND_BLOCK:(2 * pr + 2) * BAND_BLOCK]], axis=0)
                c0 = Y_GQA + (group // 2 * g + pr) * LANES
                y_ref[pl.ds(q0, BAND_BLOCK), c0:c0 + LANES] = pair.T.astype(BF16)

    for j in range(NA_HEADS // 2):
        nq_s[:, j * LANES:(j + 1) * LANES] = (col(C_NQ + j * LANES, LANES) * (ATTN_SCALE * LOG2E)).astype(BF16)
        nk_s[2 * j], nk_s[2 * j + 1] = _key_planes(col(C_NK + j * LANES, LANES).T, True, False)
        nv_s[2 * j], nv_s[2 * j + 1] = _value_planes(col(C_NV + j * LANES, LANES), True, False)
        nkc_s[j] = jnp.concatenate(_key_planes(cnk_ref[j * LANES:(j + 1) * LANES, :], True, False), axis=1)
        nvc_s[2 * j], nvc_s[2 * j + 1] = _value_planes(cnv_ref[j * LANES:(j + 1) * LANES, :].T, True, False)

    na_kinds, na_kind_of = _na_group_kinds()
    na_width = _na_table_width()

    def na_group(k):
        lo, n = _na_group_slab(k)
        q0, nq = k * NA_GROUP_ROWS * GRID_W, NA_GROUP_ROWS * GRID_W
        k0, nk = lo * GRID_W, n * GRID_W
        b0 = sum(kn for kn, _ in na_kinds[:na_kind_of[k]]) * GRID_W
        for j in range(NA_HEADS // 2):
            q = nq_s[q0:q0 + nq, j * LANES:(j + 1) * LANES]
            s_ctx = _dot(q, nkc_s[j])
            terms = []
            for odd in range(2):
                h = 2 * j + odd
                s_loc = _dot(q, nk_s[h, :, k0:k0 + nk]) + nab_ref[:, h * na_width + b0:h * na_width + b0 + nk]
                (p_loc, p_ctx), _ = _probabilities([s_loc, s_ctx[:, odd * PAST_LEN:(odd + 1) * PAST_LEN]])
                terms += [(p_loc, nv_s[h, k0:k0 + nk, :]), (p_ctx, nvc_s[h])]
            y_ref[q0:q0 + nq, Y_NA + j * LANES:Y_NA + (j + 1) * LANES] = _attend(terms).astype(BF16)

    ckv_b = _rms(col(C_MKV, MLA_KV_RANK), gkv_ref[...]).astype(BF16)
    cckv_b = cckv_ref[...].astype(BF16)
    q = _dot(_rms(col(C_MQ, MLA_Q_RANK), gq_ref[...]).astype(BF16), wuq_ref[...]) * (MLA_SCALE * LOG2E)
    for i in range(3):
        tile = q[:, i * LANES:(i + 1) * LANES]
        mq_s[i * LANES:(i + 1) * LANES, :] = (rope32(tile) if i == 2 else tile).T.astype(BF16)
    kr = jnp.where(_lane_lt((t, LANES), MLA_ROPE), rope32(col(C_MKR, LANES)), 0.0)
    kr_c = jnp.concatenate([ckr_ref[...], jnp.zeros((LANES - MLA_ROPE, PAST_LEN), F32)], axis=0).T
    for rows, ckv_x, kr_x in ((slice(0, t), ckv_b, kr), (slice(t, t + PAST_LEN), cckv_b, kr_c)):
        kn_all = _dot(ckv_x, wuk_ref[...])
        v_t_all = _dot_nt(wuvt_ref[...], ckv_x)
        for j in range(MLA_HEADS // 2):
            mk_s[j, rows, :] = jnp.concatenate([kn_all[:, j * LANES:(j + 1) * LANES], kr_x], axis=1).astype(BF16)
        for h in range(MLA_HEADS):
            mv_s[h, :, rows] = _value_plane_t(v_t_all[h * MLA_V:(h + 1) * MLA_V, :])

    tq = 256

    def mla_block(i, carry):
        q0 = pl.multiple_of(i * tq, tq)
        zeros = lambda r: jnp.zeros((r, tq), BF16)
        q_nope = lambda h: mq_s[h * MLA_NOPE:(h + 1) * MLA_NOPE, pl.ds(q0, tq)]
        q_rope = lambda h: mq_s[MLA_HEADS * MLA_NOPE + h * MLA_ROPE:MLA_HEADS * MLA_NOPE + (h + 1) * MLA_ROPE,
                                pl.ds(q0, tq)]
        pad = 2 * LANES - 2 * MLA_NOPE - MLA_ROPE
        for j in range(MLA_HEADS // 2):
            he, ho = 2 * j, 2 * j + 1
            q_e = jnp.concatenate([q_nope(he), zeros(MLA_NOPE), q_rope(he), zeros(pad)], axis=0)
            q_o = jnp.concatenate([zeros(MLA_NOPE), q_nope(ho), q_rope(ho), zeros(pad)], axis=0)
            s = _dot(mk_s[j], jnp.concatenate([q_e, q_o], axis=1))
            outs = []
            for odd in range(2):
                (p,), _ = _probabilities_t([s[:, odd * tq:(odd + 1) * tq]])
                outs.append(_attend_t([(mv_s[2 * j + odd], p)]))
            y_ref[pl.ds(q0, tq), Y_MLA + j * LANES:Y_MLA + (j + 1) * LANES] = (
                jnp.concatenate(outs, axis=0).T.astype(BF16))
        return carry

    span = BAND_BLOCK + 2 * WINDOW
    nb = t // BAND_BLOCK
    for i in range(nb):
        gqa_block(jnp.int32(i), WINDOW if i == 0 else 0, span - WINDOW if i == nb - 1 else span)
        if i % (nb // NA_GROUPS) == 0:
            na_group(i // (nb // NA_GROUPS))
        if i % (nb * tq // t) == 0:
            mla_block(jnp.int32(i // (nb * tq // t)), 0)


def _rope_tables(group, half):
    tok = np.arange(DEC_SEQ)
    pos = np.stack([tok // GRID_W, tok % GRID_W], axis=1).astype(np.float64)
    inv = ROPE_BASE ** (-np.arange(half, dtype=np.float64) / half)
    lane = np.arange(LANES) % group
    axis = lane // (2 * half)
    within = lane % (2 * half)
    ang = pos[:, axis] * inv[within % half][None, :]
    cos, sin = np.cos(ang), np.sin(ang)
    upper = (within >= half)[None, :]
    sin_hi = np.where(upper, sin, 0.0)
    sin_lo = np.where(upper, 0.0, -sin)
    return tuple(jnp.asarray(a, dtype=F32) for a in (cos, sin_hi, sin_lo))


def _band_mask():
    c = np.arange(BAND_BLOCK + 2 * WINDOW)[:, None]
    i = np.arange(BAND_BLOCK)[None, :]
    ok = (c >= i) & (c <= i + 2 * WINDOW)
    m = np.where(ok, 0.0, NEG_INF)
    return jnp.asarray(np.concatenate([m] * (GQA_HEADS // GQA_KV_HEADS), axis=1), dtype=F32)


def _lat_mixer_call(x, mods, g_attn, w_main, sink, w_conv, g_q, w_uq, g_kv, w_uk, w_uv_t, nab, caches, layer, name):
    n, t = DEC_BATCH, DEC_SEQ
    one = pl.Buffered(1)
    const = lambda shape: pl.BlockSpec(shape, lambda s: (0,) * len(shape), pipeline_mode=one)
    cache = lambda shape: pl.BlockSpec((None, None) + shape, lambda s: (s, layer, 0, 0), pipeline_mode=one)
    rope_g = _rope_tables(HEAD_DIM, HEAD_DIM // 4)
    rope_m = _rope_tables(MLA_ROPE, MLA_ROPE // 4)
    span = BAND_BLOCK + 2 * WINDOW
    keys = t + PAST_LEN
    scratch = [
        pltpu.VMEM((t, D_MODEL), BF16),
        pltpu.VMEM((GQA_HEADS * HEAD_DIM, t), BF16),
        pltpu.VMEM((t, LANES), BF16),
        pltpu.VMEM((GQA_KV_HEADS, HEAD_DIM + ONES_ROWS, t), BF16),
        pltpu.VMEM((PAST_LEN, LANES), BF16),
        pltpu.VMEM((GQA_KV_HEADS, HEAD_DIM + ONES_ROWS, PAST_LEN), BF16),
        pltpu.VMEM((t, NA_HEADS // 2 * LANES), BF16),
        pltpu.VMEM((NA_HEADS, LANES, t), BF16),
        pltpu.VMEM((NA_HEADS, t, 2 * LANES), BF16),
        pltpu.VMEM((NA_HEADS // 2, LANES, 2 * PAST_LEN), BF16),
        pltpu.VMEM((NA_HEADS, PAST_LEN, 2 * LANES), BF16),
        pltpu.VMEM((3 * LANES, t), BF16),
        pltpu.VMEM((MLA_HEADS // 2, keys, 2 * LANES), BF16),
        pltpu.VMEM((MLA_HEADS, MLA_V + ONES_ROWS, keys), BF16),
    ]
    return pl.pallas_call(
        functools.partial(_lat_mixer_kernel, layer=layer),
        grid=(n,),
        in_specs=[
            pl.BlockSpec(memory_space=pltpu.SMEM),
            pl.BlockSpec((None, t, D_MODEL), lambda s: (s, 0, 0), pipeline_mode=one),
            _mod_spec(layer, False, 1),
            _layer_spec((1, D_MODEL), layer, 1),
            _layer_spec((MAIN_COLS, D_MODEL), layer, 1),
            _layer_spec((CONV_K, CONV_WIDTH), layer, 1),
            _layer_spec((1, MLA_Q_RANK), layer, 1),
            _layer_spec((MLA_Q_RANK, 3 * LANES), layer, 1),
            _layer_spec((1, MLA_KV_RANK), layer, 1),
            _layer_spec((MLA_KV_RANK, 2 * LANES), layer, 1),
            _layer_spec((2 * LANES, MLA_KV_RANK), layer, 1),
        ] + [const((t, LANES))] * 6 + [
            const((span, GQA_HEADS // GQA_KV_HEADS * BAND_BLOCK)),
            _layer_spec((NA_GROUP_ROWS * GRID_W, NA_HEADS * _na_table_width()), layer, 1),
            cache((2 * HEAD_DIM, PAST_LEN)), cache((2 * HEAD_DIM, PAST_LEN)),
            cache((4 * HEAD_DIM, PAST_LEN)), cache((4 * HEAD_DIM, PAST_LEN)),
            cache((PAST_LEN, MLA_KV_RANK)), cache((MLA_ROPE, PAST_LEN)),
        ],
        out_specs=pl.BlockSpec((None, t, Y_COLS), lambda s: (s, 0, 0)),
        out_shape=jax.ShapeDtypeStruct((n, t, Y_COLS), BF16),
        scratch_shapes=scratch,
        compiler_params=pltpu.CompilerParams(
            dimension_semantics=("arbitrary",), vmem_limit_bytes=VMEM_LIMIT),
        name=name,
    )(sink, x, mods, g_attn, w_main, w_conv, g_q, w_uq, g_kv, w_uk, w_uv_t, *rope_g, *rope_m, _band_mask(), nab, *caches)


def _post_kernel(x_ref, y_ref, mod_ref, ga_ref, gm_ref, gf_ref, wg_ref, wb_ref, wo_ref, w1_ref, w2_ref,
                 o_ref, *, final):
    d = D_MODEL
    mod = lambda i: mod_ref[:, i * d:(i + 1) * d]
    bounds = (Y_CONV, Y_GQA, Y_NA, Y_MLA, Y_COLS)
    for r0 in range(0, x_ref.shape[0], POST_CHAIN_ROWS):
        rows = slice(r0, r0 + POST_CHAIN_ROWS)
        x = x_ref[rows, :]
        h = _modulated_norm(x, ga_ref[...], mod_ref, 0, 1).astype(BF16)
        merged = None
        for i in range(N_BRANCH):
            lo, hi = bounds[i], bounds[i + 1]
            gate = jax.nn.sigmoid(_dot_nt(h, wg_ref[0, i * d:(i + 1) * d, :]))
            term = gate * _dot(y_ref[rows, lo:hi], wb_ref[lo:hi, :])
            merged = term if merged is None else merged + term
        x = x + mod(2) * _dot(merged.astype(BF16), wo_ref[...])
        h = _modulated_norm(x, gm_ref[...], mod_ref, 3, 4).astype(BF16)
        mlp = None
        for c0 in range(0, D_FF, d):
            f = jnp.square(jnp.maximum(_dot(h, w1_ref[:, c0:c0 + d]), 0.0)).astype(BF16)
            term = _dot(f, w2_ref[c0:c0 + d, :])
            mlp = term if mlp is None else mlp + term
        x = x + mod(5) * mlp
        o_ref[rows, :] = _rms(x, gf_ref[...]) if final else x


def _post_call(x, y, mods, g_attn, g_mlp, g_final, w_gates, w_br, w_o, w_ff1, w_ff2, layer, ctx, name):
    n, t, _ = x.shape
    tm = POST_CHAIN_ROWS
    return pl.pallas_call(
        functools.partial(_post_kernel, final=layer == DEPTH - 1),
        grid=(n, t // tm),
        in_specs=[
            pl.BlockSpec((None, tm, D_MODEL), lambda s, i: (s, i, 0)),
            pl.BlockSpec((None, tm, Y_COLS), lambda s, i: (s, i, 0)),
            _mod_spec(layer, ctx, 2),
            _layer_spec((1, D_MODEL), layer, 2),
            _layer_spec((1, D_MODEL), layer, 2),
            pl.BlockSpec((1, D_MODEL), lambda s, i: (0, 0)),
            pl.BlockSpec((pl.Element(1), pl.Element(N_BRANCH * D_MODEL), pl.Element(D_MODEL)),
                         lambda s, i: (layer, GATE_COL0, 0), pipeline_mode=pl.Buffered(1)),
            _layer_spec((Y_COLS, D_MODEL), layer, 2),
            _layer_spec((D_MODEL, D_MODEL), layer, 2),
            _layer_spec((D_MODEL, D_FF), layer, 2),
            _layer_spec((D_FF, D_MODEL), layer, 2),
        ],
        out_specs=pl.BlockSpec((None, tm, D_MODEL), lambda s, i: (s, i, 0)),
        out_shape=jax.ShapeDtypeStruct((n, t, D_MODEL), F32),
        compiler_params=pltpu.CompilerParams(
            dimension_semantics=("arbitrary", "arbitrary"), vmem_limit_bytes=VMEM_LIMIT),
        name=name,
    )(x, y, mods, g_attn, g_mlp, g_final, w_gates, w_br, w_o, w_ff1, w_ff2)


def kernel(x_prompt, x_sample, cache_gqa_k, cache_gqa_v, cache_na_k, cache_na_v, cache_mla_ckv, cache_mla_krope, c, c_ctx, w_mod, b_mod, g_attn, g_mlp, w_in, w_conv, gqa_sink, na_rpb, mla_g_q, mla_w_uq, mla_g_kv, mla_w_ukv, w_branch_conv, w_branch_gqa, w_branch_na, w_branch_mla, w_o, w_ff1, w_ff2, g_final):
    w_main = w_gates = jnp.transpose(w_in, (0, 2, 1)).astype(BF16)
    uq = mla_w_uq.reshape(DEPTH, MLA_Q_RANK, MLA_HEADS, MLA_NOPE + MLA_ROPE)
    w_uq = jnp.concatenate([uq[..., :MLA_NOPE].reshape(DEPTH, MLA_Q_RANK, -1),
                            uq[..., MLA_NOPE:].reshape(DEPTH, MLA_Q_RANK, -1)], axis=-1).astype(BF16)
    ukv = mla_w_ukv.reshape(DEPTH, MLA_KV_RANK, MLA_HEADS, MLA_NOPE + MLA_V)
    w_uk = ukv[..., :MLA_NOPE].reshape(DEPTH, MLA_KV_RANK, -1).astype(BF16)
    w_uv = ukv[..., MLA_NOPE:].reshape(DEPTH, MLA_KV_RANK, -1).astype(BF16)
    w_uk_t, w_uv_t = jnp.transpose(w_uk, (0, 2, 1)), jnp.transpose(w_uv, (0, 2, 1))
    w_br = jnp.concatenate([w_branch_conv, w_branch_gqa, w_branch_na, w_branch_mla], axis=1).astype(BF16)
    w_o_b, w_ff1_b, w_ff2_b = w_o.astype(BF16), w_ff1.astype(BF16), w_ff2.astype(BF16)
    g_a, g_m = g_attn[:, None, :], g_mlp[:, None, :]
    g_q, g_kv, g_f = mla_g_q[:, None, :], mla_g_kv[:, None, :], g_final[None, :]

    c16 = jnp.concatenate([c, c_ctx[None, :], jnp.zeros((MOD_ROWS - DEC_BATCH - 1, D_MODEL), F32)], axis=0)
    mods = _mod_call(c16, w_mod, b_mod).reshape(DEPTH, MOD_ROWS, 1, 6 * D_MODEL)
    nab = _nab_call(na_rpb)

    heads_t = lambda a: jnp.transpose(a, (0, 1, 3, 4, 2)).reshape(a.shape[0], DEPTH, -1, a.shape[2])
    caches = (heads_t(cache_gqa_k), heads_t(cache_gqa_v), heads_t(cache_na_k), heads_t(cache_na_v),
              cache_mla_ckv, jnp.transpose(cache_mla_krope, (0, 1, 3, 2)))

    h_ctx, h_lat = x_prompt, x_sample
    states = None
    for l in range(DEPTH):
        mixer_w = (gqa_sink, w_conv, g_q, w_uq, g_kv)
        post_w = (g_a, g_m, g_f, w_gates, w_br, w_o_b, w_ff1_b, w_ff2_b)

        y, states = _ctx_mixer_call(h_ctx, mods, g_a, w_main, *mixer_w, w_uk_t, w_uv, states, l, f"mixer_ctx_{l}")
        flat = lambda a: a.reshape(1, BATCH * SEQ, a.shape[-1])
        h_ctx = _post_call(flat(h_ctx), flat(y), mods, *post_w, l, True, f"post_ctx_{l}").reshape(BATCH, SEQ, D_MODEL)

        y = _lat_mixer_call(h_lat, mods, g_a, w_main, *mixer_w, w_uk, w_uv_t, nab, caches, l, f"mixer_lat_{l}")
        h_lat = _post_call(h_lat, y, mods, *post_w, l, False, f"post_lat_{l}")

    def heads_out(a, heads):
        return jnp.transpose(a.reshape(BATCH, DEPTH, heads, HEAD_DIM, SEQ), (0, 1, 4, 2, 3))

    kg, vg, kn, vn, ckv, kr = states
    return (h_ctx, h_lat, heads_out(kg, GQA_KV_HEADS), heads_out(vg, GQA_KV_HEADS),
            heads_out(kn, NA_HEADS), heads_out(vn, NA_HEADS), ckv, jnp.transpose(kr, (0, 1, 3, 2)))
```

```python
import functools
import math

import numpy as np
import jax
import jax.numpy as jnp
from jax import lax
from jax.experimental import pallas as pl
from jax.experimental.pallas import tpu as pltpu

D_MODEL = 1024
BATCH = 32
SEQ = 256
DEPTH = 2
DEC_BATCH = 8
DEC_SEQ = 1024
PAST_LEN = 512
GRID_W = 64
GRID_ROWS = DEC_SEQ // GRID_W
HEAD_DIM = 64
CONV_WIDTH = 256
CONV_K = 3
GQA_HEADS = 8
GQA_KV_HEADS = 2
WINDOW = 128
BAND_BLOCK = 128
NA_HEADS = 4
NA_WIN_H = 8
NA_WIN_W = 16
MLA_HEADS = 4
MLA_Q_RANK = 256
MLA_KV_RANK = 128
MLA_NOPE = 64
MLA_ROPE = 32
MLA_V = 64
D_FF = 4 * D_MODEL
N_BRANCH = 4
ROPE_BASE = 10000.0
EPS = 1e-6
NEG_INF = -1e30
LOG2E = math.log2(math.e)
ATTN_SCALE = HEAD_DIM ** -0.5
MLA_SCALE = (MLA_NOPE + MLA_ROPE) ** -0.5

LANES = 128
MOD_ROWS = 16
CTX_MOD_ROW = DEC_BATCH

C_CB, C_CC, C_CV = 0, 256, 512
C_GQ, C_GK, C_GV = 768, 1280, 1408
C_NQ, C_NK, C_NV = 1536, 1792, 2048
C_MQ, C_MKV, C_MKR = 2304, 2560, 2688
MAIN_COLS = 2816
GATE_COL0 = 2720
Y_CONV, Y_GQA, Y_NA, Y_MLA = 0, 256, 768, 1024
Y_COLS = 1280
NA_GROUPS = 4
NA_GROUP_ROWS = GRID_ROWS // NA_GROUPS

VMEM_LIMIT = 56 * 1024 * 1024
POST_CHAIN_ROWS = 256
CTX_SEQS = 4
CTX_SEQS_FIRST = 2

F32 = jnp.float32
BF16 = jnp.bfloat16


def _dot(a, b):
    return jnp.dot(a, b, preferred_element_type=F32)


def _dot_nt(a, b):
    return lax.dot_general(a, b, (((1,), (1,)), ((), ())), preferred_element_type=F32)


def _rms(x, g):
    return x * lax.rsqrt(jnp.mean(x * x, axis=-1, keepdims=True) + EPS) * g


def _lane_lt(shape, n):
    return lax.broadcasted_iota(jnp.int32, shape, len(shape) - 1) < n


def _row_lt(shape, n):
    return lax.broadcasted_iota(jnp.int32, shape, 0) < n


def _row_group(shape, lo, hi):
    row = lax.broadcasted_iota(jnp.int32, shape, 0)
    return (row >= lo) & (row < hi)


def _short_conv(cb, cc, cv, w):
    u = cc * cv
    t = u.shape[0]
    row = lax.broadcasted_iota(jnp.int32, u.shape, 0)
    prev = jnp.where(row == 0, 0.0, pltpu.roll(u, 1, 0))
    nxt = jnp.where(row == t - 1, 0.0, pltpu.roll(u, t - 1, 0))
    return cb * (prev * w[0:1, :] + u * w[1:2, :] + nxt * w[2:3, :])


def _rope(x, cos, sin_hi, sin_lo, half):
    n = x.shape[-1]
    return x * cos + pltpu.roll(x, n - half, 1) * sin_lo + pltpu.roll(x, half, 1) * sin_hi


def _mod_kernel(c_ref, w_ref, b_ref, o_ref):
    c = c_ref[...]
    s = c * jax.nn.sigmoid(c)
    o_ref[...] = _dot(s.astype(BF16), w_ref[...].astype(BF16)) + b_ref[...]


def _mod_call(c16, w_mod, b_mod):
    tn = 1536
    return pl.pallas_call(
        _mod_kernel,
        grid=(DEPTH, 6 * D_MODEL // tn),
        in_specs=[
            pl.BlockSpec((MOD_ROWS, D_MODEL), lambda l, j: (0, 0)),
            pl.BlockSpec((None, D_MODEL, tn), lambda l, j: (l, 0, j)),
            pl.BlockSpec((None, 1, tn), lambda l, j: (l, 0, j)),
        ],
        out_specs=pl.BlockSpec((None, MOD_ROWS, tn), lambda l, j: (l, 0, j)),
        out_shape=jax.ShapeDtypeStruct((DEPTH, MOD_ROWS, 6 * D_MODEL), F32),
        compiler_params=pltpu.CompilerParams(
            dimension_semantics=("arbitrary", "arbitrary"), vmem_limit_bytes=VMEM_LIMIT),
        name="adaln_mod",
    )(c16, w_mod, b_mod.reshape(DEPTH, 1, 6 * D_MODEL))


def _na_window_start(r):
    return min(max(r - NA_WIN_H // 2, 0), GRID_ROWS - NA_WIN_H)


def _na_group_slab(k):
    starts = [_na_window_start(r) for r in range(k * NA_GROUP_ROWS, (k + 1) * NA_GROUP_ROWS)]
    lo = min(starts) // 2 * 2
    n = -(-(max(starts) + NA_WIN_H - lo) // 4) * 4
    assert lo + n <= GRID_ROWS
    return lo, n


def _na_group_kinds():
    kinds, of_group = [], []
    for k in range(NA_GROUPS):
        lo, n = _na_group_slab(k)
        layout = {}
        for dr in range(NA_GROUP_ROWS):
            r = k * NA_GROUP_ROWS + dr
            r0 = _na_window_start(r)
            for a in range(n):
                inside = r0 <= lo + a < r0 + NA_WIN_H
                layout[(dr, a)] = lo + a - r + NA_WIN_H - 1 if inside else None
        if (n, layout) not in kinds:
            kinds.append((n, layout))
        of_group.append(kinds.index((n, layout)))
    return kinds, of_group


def _na_table_width():
    return sum(n for n, _ in _na_group_kinds()[0]) * GRID_W


def _nab_kernel(rpb_ref, o_ref):
    l = pl.program_id(0)
    c = lax.broadcasted_iota(jnp.int32, (GRID_W, GRID_W), 0)
    w = lax.broadcasted_iota(jnp.int32, (GRID_W, GRID_W), 1)
    dc = w - c + (NA_WIN_W - 1)
    c0 = jnp.clip(c - NA_WIN_W // 2, 0, GRID_W - NA_WIN_W)
    outside = (w < c0) | (w >= c0 + NA_WIN_W)
    n_dr, n_dc = 2 * NA_WIN_H - 1, 2 * NA_WIN_W - 1
    kinds, _ = _na_group_kinds()
    width = _na_table_width()

    def put(h, d, tile):
        col0 = h * width
        for n, layout in kinds:
            for (dr, a), want in layout.items():
                if want == d:
                    o_ref[dr * GRID_W:(dr + 1) * GRID_W, col0 + a * GRID_W:col0 + (a + 1) * GRID_W] = tile
            col0 += n * GRID_W

    for h in range(NA_HEADS):
        put(h, None, jnp.full((GRID_W, GRID_W), NEG_INF, F32))
        for d in range(n_dr):
            base = ((l * NA_HEADS + h) * n_dr + d) * n_dc
            tile = jnp.full((GRID_W, GRID_W), NEG_INF, F32)
            for j in range(n_dc):
                tile = jnp.where(dc == j, rpb_ref[base + j] * LOG2E, tile)
            put(h, d, jnp.where(outside, NEG_INF, tile))


def _nab_call(na_rpb):
    shape = (NA_GROUP_ROWS * GRID_W, NA_HEADS * _na_table_width())
    return pl.pallas_call(
        _nab_kernel,
        grid=(DEPTH,),
        in_specs=[pl.BlockSpec(memory_space=pltpu.SMEM)],
        out_specs=pl.BlockSpec((None,) + shape, lambda l: (l, 0, 0)),
        out_shape=jax.ShapeDtypeStruct((DEPTH,) + shape, F32),
        compiler_params=pltpu.CompilerParams(dimension_semantics=("arbitrary",), vmem_limit_bytes=VMEM_LIMIT),
        name="na_bias_tables",
    )(na_rpb.reshape(-1))


def _mod_spec(layer, ctx, n_grid):
    if n_grid == 1:
        index = (lambda s: (layer, CTX_MOD_ROW, 0, 0)) if ctx else (lambda s: (layer, s, 0, 0))
    else:
        index = (lambda s, i: (layer, CTX_MOD_ROW, 0, 0)) if ctx else (lambda s, i: (layer, s, 0, 0))
    return pl.BlockSpec((None, None, 1, 6 * D_MODEL), index)


def _layer_spec(shape, layer, n_grid):
    zeros = (0,) * len(shape)
    index = (lambda s: (layer,) + zeros) if n_grid == 1 else (lambda s, i: (layer,) + zeros)
    return pl.BlockSpec((None,) + tuple(shape), index, pipeline_mode=pl.Buffered(1))


_COLUMN_GROUPS = ((C_CB, C_GQ), (C_GQ, C_NQ), (C_NQ, C_MQ), (C_MQ, MAIN_COLS))


def _modulated_norm(x, g, mod_ref, shift, scale):
    d = D_MODEL
    return _rms(x, g) * (1.0 + mod_ref[:, scale * d:(scale + 1) * d]) + mod_ref[:, shift * d:(shift + 1) * d]


def _projected_columns(h_ref, w_ref):
    cache = {}

    def col(c, n):
        lo, hi = next(g for g in _COLUMN_GROUPS if g[0] <= c and c + n <= g[1])
        if lo not in cache:
            cache.clear()
            cache[lo] = _dot_nt(h_ref[...], w_ref[lo:hi, :])
        return cache[lo][:, c - lo:c - lo + n]

    return col


def _key_planes(k_t, even_first, odd_first):
    lo = _row_lt(k_t.shape, HEAD_DIM)
    swapped = None
    if not even_first or odd_first:
        swapped = pltpu.roll(k_t, HEAD_DIM, 0)
    top = jnp.where(lo, k_t if even_first else swapped, 0.0)
    bot = jnp.where(lo, 0.0, swapped if odd_first else k_t)
    return top.astype(BF16), bot.astype(BF16)


def _value_planes(pair, even_first, odd_first):
    lo = _lane_lt(pair.shape, HEAD_DIM)
    swapped = None
    if not even_first or odd_first:
        swapped = pltpu.roll(pair, HEAD_DIM, 1)
    top = jnp.where(lo, pair if even_first else swapped, 0.0)
    bot = jnp.where(lo, 0.0, swapped if odd_first else pair)
    one_e = jnp.where(lo, 1.0, 0.0)
    return (jnp.concatenate([top, one_e], axis=1).astype(BF16),
            jnp.concatenate([bot, 1.0 - one_e], axis=1).astype(BF16))


def _probabilities(parts, sink=None):
    m = parts[0].max(axis=-1, keepdims=True)
    for p in parts[1:]:
        m = jnp.maximum(m, p.max(axis=-1, keepdims=True))
    if sink is not None:
        m = jnp.maximum(m, sink)
    probs = [jnp.exp2((p - m).astype(BF16)) for p in parts]
    return probs, (None if sink is None else jnp.exp2(sink - m))


def _attend(terms, sink_e=None, sink_o=None):
    o = None
    for p, v in terms:
        t = _dot(p, v)
        o = t if o is None else o + t
    den = o[:, LANES:]
    if sink_e is not None:
        den = den + jnp.where(_lane_lt(den.shape, HEAD_DIM), sink_e, sink_o)
    return o[:, :LANES] / den


ONES_ROWS = 16


def _value_plane_t(v_t):
    return jnp.concatenate([v_t, jnp.ones((ONES_ROWS, v_t.shape[1]), v_t.dtype)], axis=0).astype(BF16)


def _probabilities_t(parts, sink=None):
    m = parts[0].max(axis=0, keepdims=True)
    for p in parts[1:]:
        m = jnp.maximum(m, p.max(axis=0, keepdims=True))
    if sink is not None:
        m = jnp.maximum(m, sink)
    probs = [jnp.exp2((p - m).astype(BF16)) for p in parts]
    return probs, (None if sink is None else jnp.exp2(sink - m))


def _attend_t(terms, sink_num=None):
    o = None
    for v_t, p in terms:
        t = _dot(v_t, p)
        o = t if o is None else o + t
    den = o[HEAD_DIM:HEAD_DIM + 1, :]
    if sink_num is not None:
        den = den + sink_num
    return o[0:HEAD_DIM, :] / den


def _mla_key_plane(kn_t_pair, kr4_t, j, odd):
    h = 2 * j + odd
    lo = _row_lt(kn_t_pair.shape, HEAD_DIM)
    nope = jnp.where(lo, 0.0, kn_t_pair) if odd else jnp.where(lo, kn_t_pair, 0.0)
    rope = jnp.where(_row_group(kr4_t.shape, h * MLA_ROPE, (h + 1) * MLA_ROPE), kr4_t, 0.0)
    return jnp.concatenate([nope, rope], axis=0).astype(BF16)


def _ctx_mixer_kernel(sink_ref, x_ref, mod_ref, ga_ref, win_ref, wc_ref, gq_ref, wuq_ref, gkv_ref, wukt_ref, wuv_ref,
                      *rest, layer, first):
    y_ref, kg_ref, vg_ref, kn_ref, vn_ref, ckv_ref, kr_ref, h_s = rest[-8:]
    t, nb = SEQ, x_ref.shape[0]
    for b in range(nb):
        h_s[b * t:(b + 1) * t, :] = _modulated_norm(x_ref[b], ga_ref[...], mod_ref, 0, 1).astype(BF16)
    col_all = _projected_columns(h_s, win_ref)

    def put_state(ref, b, val):
        if first:
            for d in range(DEPTH):
                ref[b, d] = val if d == layer else jnp.zeros_like(val)
        else:
            ref[b] = val

    def conv(b, col):
        y_ref[b, :, Y_CONV:Y_CONV + CONV_WIDTH] = _short_conv(
            col(C_CB, CONV_WIDTH), col(C_CC, CONV_WIDTH), col(C_CV, CONV_WIDTH), wc_ref[...]).astype(BF16)

    def gqa(b, col):
        kg_t, vpair = col(C_GK, LANES).T, col(C_GV, LANES)
        put_state(kg_ref, b, kg_t)
        put_state(vg_ref, b, vpair.T)
        row2 = lax.broadcasted_iota(jnp.int32, (2 * t, 1), 0) < t
        for g in range(GQA_KV_HEADS):
            ke, ko = _key_planes(kg_t, g == 0, g == 0)
            ve, vo = _value_planes(vpair, g == 0, g == 0)
            q = jnp.concatenate([col(C_GQ + (2 * g) * LANES, LANES), col(C_GQ + (2 * g + 1) * LANES, LANES)], axis=0)
            q = (q * (ATTN_SCALE * LOG2E)).astype(BF16)
            s = _dot(q, jnp.concatenate([ke, ko], axis=1))
            sink = [jnp.where(row2, sink_ref[layer, 4 * g + odd], sink_ref[layer, 4 * g + 2 + odd]) * LOG2E
                    for odd in range(2)]
            (pe,), xe = _probabilities([s[:, 0:t]], sink[0])
            (po,), xo = _probabilities([s[:, t:2 * t]], sink[1])
            o = _attend([(pe, ve), (po, vo)], xe, xo).astype(BF16)
            c0 = Y_GQA + (2 * g) * LANES
            y_ref[b, :, c0:c0 + LANES] = o[0:t]
            y_ref[b, :, c0 + LANES:c0 + 2 * LANES] = o[t:2 * t]

    def na(b, col):
        kn_t = col(C_NK, 2 * LANES).T
        put_state(kn_ref, b, kn_t)
        put_state(vn_ref, b, col(C_NV, 2 * LANES).T)
        for j in range(NA_HEADS // 2):
            ke, ko = _key_planes(kn_t[j * LANES:(j + 1) * LANES, :], True, False)
            ve, vo = _value_planes(col(C_NV + j * LANES, LANES), True, False)
            q = (col(C_NQ + j * LANES, LANES) * (ATTN_SCALE * LOG2E)).astype(BF16)
            s = _dot(q, jnp.concatenate([ke, ko], axis=1))
            (pe,), _ = _probabilities([s[:, 0:t]])
            (po,), _ = _probabilities([s[:, t:2 * t]])
            y_ref[b, :, Y_NA + j * LANES:Y_NA + (j + 1) * LANES] = _attend([(pe, ve), (po, vo)]).astype(BF16)

    def mla(b, col):
        ckv = _rms(col(C_MKV, MLA_KV_RANK), gkv_ref[...])
        put_state(ckv_ref, b, ckv)
        kr_t = col(C_MKR, LANES).T[0:MLA_ROPE, :]
        put_state(kr_ref, b, kr_t)
        ckv_b = ckv.astype(BF16)
        q = _dot(_rms(col(C_MQ, MLA_Q_RANK), gq_ref[...]).astype(BF16), wuq_ref[...]) * (MLA_SCALE * LOG2E)
        kn_t_all = _dot_nt(wukt_ref[...], ckv_b)
        v_all = _dot(ckv_b, wuv_ref[...])
        kr4_t = jnp.concatenate([kr_t] * MLA_HEADS, axis=0)
        q_rope = q[:, 2 * LANES:3 * LANES]
        for j in range(MLA_HEADS // 2):
            qj = jnp.concatenate([q[:, j * LANES:(j + 1) * LANES], q_rope], axis=1).astype(BF16)
            kn_t_pair = kn_t_all[j * LANES:(j + 1) * LANES, :]
            keys = jnp.concatenate(
                [_mla_key_plane(kn_t_pair, kr4_t, j, 0), _mla_key_plane(kn_t_pair, kr4_t, j, 1)], axis=1)
            s = _dot(qj, keys)
            ve, vo = _value_planes(v_all[:, j * LANES:(j + 1) * LANES], True, False)
            (pe,), _ = _probabilities([s[:, 0:t]])
            (po,), _ = _probabilities([s[:, t:2 * t]])
            y_ref[b, :, Y_MLA + j * LANES:Y_MLA + (j + 1) * LANES] = _attend([(pe, ve), (po, vo)]).astype(BF16)

    for mixer in (conv, gqa, na, mla):
        for b in range(nb):
            mixer(b, lambda c, n, b=b: col_all(c, n)[b * t:(b + 1) * t])


def _ctx_mixer_call(x, mods, g_attn, w_main, sink, w_conv, g_q, w_uq, g_kv, w_uk_t, w_uv, prev_states, layer, name):
    first = prev_states is None
    n, nb = BATCH, (CTX_SEQS_FIRST if first else CTX_SEQS)
    state_shapes = ((2 * HEAD_DIM, SEQ), (2 * HEAD_DIM, SEQ), (4 * HEAD_DIM, SEQ), (4 * HEAD_DIM, SEQ),
                    (SEQ, MLA_KV_RANK), (MLA_ROPE, SEQ))
    if first:
        state_spec = lambda shape: pl.BlockSpec((nb, DEPTH) + shape, lambda s: (s, 0, 0, 0))
    else:
        state_spec = lambda shape: pl.BlockSpec((nb, None) + shape, lambda s: (s, layer, 0, 0))
    in_specs = [
        pl.BlockSpec(memory_space=pltpu.SMEM),
        pl.BlockSpec((nb, SEQ, D_MODEL), lambda s: (s, 0, 0)),
        _mod_spec(layer, True, 1),
        _layer_spec((1, D_MODEL), layer, 1),
        _layer_spec((MAIN_COLS, D_MODEL), layer, 1),
        _layer_spec((CONV_K, CONV_WIDTH), layer, 1),
        _layer_spec((1, MLA_Q_RANK), layer, 1),
        _layer_spec((MLA_Q_RANK, 3 * LANES), layer, 1),
        _layer_spec((1, MLA_KV_RANK), layer, 1),
        _layer_spec((2 * LANES, MLA_KV_RANK), layer, 1),
        _layer_spec((MLA_KV_RANK, 2 * LANES), layer, 1),
    ]
    args = [sink, x, mods, g_attn, w_main, w_conv, g_q, w_uq, g_kv, w_uk_t, w_uv]
    aliases = {}
    if not first:
        for i, st in enumerate(prev_states):
            aliases[len(args)] = 1 + i
            in_specs.append(pl.BlockSpec(memory_space=pl.ANY))
            args.append(st)
    outs = pl.pallas_call(
        functools.partial(_ctx_mixer_kernel, layer=layer, first=first),
        grid=(n // nb,),
        in_specs=in_specs,
        out_specs=[pl.BlockSpec((nb, SEQ, Y_COLS), lambda s: (s, 0, 0))] + [state_spec(s) for s in state_shapes],
        out_shape=[jax.ShapeDtypeStruct((n, SEQ, Y_COLS), BF16)]
        + [jax.ShapeDtypeStruct((n, DEPTH) + s, F32) for s in state_shapes],
        input_output_aliases=aliases,
        scratch_shapes=[pltpu.VMEM((nb * SEQ, D_MODEL), BF16)],
        compiler_params=pltpu.CompilerParams(
            dimension_semantics=("arbitrary",), vmem_limit_bytes=VMEM_LIMIT),
        name=name,
    )(*args)
    return outs[0], outs[1:]


def _lat_mixer_kernel(sink_ref, x_ref, mod_ref, ga_ref, win_ref, wc_ref, gq_ref, wuq_ref, gkv_ref, wuk_ref, wuvt_ref,
                      rc_ref, rsh_ref, rsl_ref, mc_ref, msh_ref, msl_ref, band_ref, nab_ref,
                      cgk_ref, cgv_ref, cnk_ref, cnv_ref, cckv_ref, ckr_ref,
                      y_ref,
                      h_s, gq_s, gk_s, gv_s, gkc_s, gvc_s, nq_s, nk_s, nv_s, nkc_s, nvc_s, mq_s, mk_s, mv_s, *, layer):
    t = DEC_SEQ
    h_s[...] = _modulated_norm(x_ref[...], ga_ref[...], mod_ref, 0, 1).astype(BF16)
    col = _projected_columns(h_s, win_ref)
    rope64 = lambda x: _rope(x, rc_ref[...], rsh_ref[...], rsl_ref[...], HEAD_DIM // 4)
    rope32 = lambda x: _rope(x, mc_ref[...], msh_ref[...], msl_ref[...], MLA_ROPE // 4)

    y_ref[:, Y_CONV:Y_CONV + CONV_WIDTH] = _short_conv(
        col(C_CB, CONV_WIDTH), col(C_CC, CONV_WIDTH), col(C_CV, CONV_WIDTH), wc_ref[...]).astype(BF16)

    group = GQA_HEADS // GQA_KV_HEADS
    for j in range(GQA_HEADS // 2):
        q_pair = rope64(col(C_GQ + j * LANES, LANES)) * (ATTN_SCALE * LOG2E)
        gq_s[j * LANES:(j + 1) * LANES, :] = q_pair.T.astype(BF16)
    gk_s[...] = rope64(col(C_GK, LANES)).astype(BF16)
    gkc_s[...] = cgk_ref[...].T.astype(BF16)
    v_t = col(C_GV, LANES).T
    for g in range(GQA_KV_HEADS):
        gv_s[g] = _value_plane_t(v_t[g * HEAD_DIM:(g + 1) * HEAD_DIM, :])
        gvc_s[g] = _value_plane_t(cgv_ref[g * HEAD_DIM:(g + 1) * HEAD_DIM, :])

    def gqa_block(b, c_lo, c_hi):
        q0 = pl.multiple_of(b * BAND_BLOCK, BAND_BLOCK)
        k0 = pl.multiple_of(q0 + (c_lo - WINDOW), BAND_BLOCK)
        n = c_hi - c_lo
        band = band_ref[c_lo:c_hi, :]
        lane_head = lax.shift_right_logical(
            lax.broadcasted_iota(jnp.int32, (1, group * BAND_BLOCK), 1), BAND_BLOCK.bit_length() - 1)
        zeros = jnp.zeros((HEAD_DIM, group * BAND_BLOCK), BF16)
        for g in range(GQA_KV_HEADS):
            q_t = jnp.concatenate(
                [gq_s[(group * g + h) * HEAD_DIM:(group * g + h + 1) * HEAD_DIM, pl.ds(q0, BAND_BLOCK)]
                 for h in range(group)], axis=1)
            q_t = jnp.concatenate([q_t, zeros] if g == 0 else [zeros, q_t], axis=0)
            s_loc = _dot(gk_s[pl.ds(k0, n), :], q_t) + band
            s_ctx = _dot(gkc_s[...], q_t)
            sink = sink_ref[layer, group * g + group - 1]
            for h in range(group - 2, -1, -1):
                sink = jnp.where(lane_head == h, sink_ref[layer, group * g + h], sink)
            (p_loc, p_ctx), x = _probabilities_t([s_loc, s_ctx], sink * LOG2E)
            o = _attend_t([(gv_s[g, :, pl.ds(k0, n)], p_loc), (gvc_s[g], p_ctx)], x)
            for pr in range(group // 2):
                pair = jnp.concatenate([o[:, (2 * pr) * BAND_BLOCK:(2 * pr + 1) * BAND_BLOCK],
                                        o[:, (2 * pr + 1) * BAND_BLOCK:(2 * pr + 2) * BAND_BLOCK]], axis=0)
                c0 = Y_GQA + (group // 2 * g + pr) * LANES
                y_ref[pl.ds(q0, BAND_BLOCK), c0:c0 + LANES] = pair.T.astype(BF16)

    for j in range(NA_HEADS // 2):
        nq_s[:, j * LANES:(j + 1) * LANES] = (col(C_NQ + j * LANES, LANES) * (ATTN_SCALE * LOG2E)).astype(BF16)
        nk_s[2 * j], nk_s[2 * j + 1] = _key_planes(col(C_NK + j * LANES, LANES).T, True, False)
        nv_s[2 * j], nv_s[2 * j + 1] = _value_planes(col(C_NV + j * LANES, LANES), True, False)
        nkc_s[j] = jnp.concatenate(_key_planes(cnk_ref[j * LANES:(j + 1) * LANES, :], True, False), axis=1)
        nvc_s[2 * j], nvc_s[2 * j + 1] = _value_planes(cnv_ref[j * LANES:(j + 1) * LANES, :].T, True, False)

    na_kinds, na_kind_of = _na_group_kinds()
    na_width = _na_table_width()

    def na_group(k):
        lo, n = _na_group_slab(k)
        q0, nq = k * NA_GROUP_ROWS * GRID_W, NA_GROUP_ROWS * GRID_W
        k0, nk = lo * GRID_W, n * GRID_W
        b0 = sum(kn for kn, _ in na_kinds[:na_kind_of[k]]) * GRID_W
        for j in range(NA_HEADS // 2):
            q = nq_s[q0:q0 + nq, j * LANES:(j + 1) * LANES]
            s_ctx = _dot(q, nkc_s[j])
            terms = []
            for odd in range(2):
                h = 2 * j + odd
                s_loc = _dot(q, nk_s[h, :, k0:k0 + nk]) + nab_ref[:, h * na_width + b0:h * na_width + b0 + nk]
                (p_loc, p_ctx), _ = _probabilities([s_loc, s_ctx[:, odd * PAST_LEN:(odd + 1) * PAST_LEN]])
                terms += [(p_loc, nv_s[h, k0:k0 + nk, :]), (p_ctx, nvc_s[h])]
            y_ref[q0:q0 + nq, Y_NA + j * LANES:Y_NA + (j + 1) * LANES] = _attend(terms).astype(BF16)

    ckv_b = _rms(col(C_MKV, MLA_KV_RANK), gkv_ref[...]).astype(BF16)
    cckv_b = cckv_ref[...].astype(BF16)
    q = _dot(_rms(col(C_MQ, MLA_Q_RANK), gq_ref[...]).astype(BF16), wuq_ref[...]) * (MLA_SCALE * LOG2E)
    for i in range(3):
        tile = q[:, i * LANES:(i + 1) * LANES]
        mq_s[i * LANES:(i + 1) * LANES, :] = (rope32(tile) if i == 2 else tile).T.astype(BF16)
    kr = jnp.where(_lane_lt((t, LANES), MLA_ROPE), rope32(col(C_MKR, LANES)), 0.0)
    kr_c = jnp.concatenate([ckr_ref[...], jnp.zeros((LANES - MLA_ROPE, PAST_LEN), F32)], axis=0).T
    for rows, ckv_x, kr_x in ((slice(0, t), ckv_b, kr), (slice(t, t + PAST_LEN), cckv_b, kr_c)):
        kn_all = _dot(ckv_x, wuk_ref[...])
        v_t_all = _dot_nt(wuvt_ref[...], ckv_x)
        for j in range(MLA_HEADS // 2):
            mk_s[j, rows, :] = jnp.concatenate([kn_all[:, j * LANES:(j + 1) * LANES], kr_x], axis=1).astype(BF16)
        for h in range(MLA_HEADS):
            mv_s[h, :, rows] = _value_plane_t(v_t_all[h * MLA_V:(h + 1) * MLA_V, :])

    tq = 256

    def mla_block(i, carry):
        q0 = pl.multiple_of(i * tq, tq)
        zeros = lambda r: jnp.zeros((r, tq), BF16)
        q_nope = lambda h: mq_s[h * MLA_NOPE:(h + 1) * MLA_NOPE, pl.ds(q0, tq)]
        q_rope = lambda h: mq_s[MLA_HEADS * MLA_NOPE + h * MLA_ROPE:MLA_HEADS * MLA_NOPE + (h + 1) * MLA_ROPE,
                                pl.ds(q0, tq)]
        pad = 2 * LANES - 2 * MLA_NOPE - MLA_ROPE
        for j in range(MLA_HEADS // 2):
            he, ho = 2 * j, 2 * j + 1
            q_e = jnp.concatenate([q_nope(he), zeros(MLA_NOPE), q_rope(he), zeros(pad)], axis=0)
            q_o = jnp.concatenate([zeros(MLA_NOPE), q_nope(ho), q_rope(ho), zeros(pad)], axis=0)
            s = _dot(mk_s[j], jnp.concatenate([q_e, q_o], axis=1))
            outs = []
            for odd in range(2):
                (p,), _ = _probabilities_t([s[:, odd * tq:(odd + 1) * tq]])
                outs.append(_attend_t([(mv_s[2 * j + odd], p)]))
            y_ref[pl.ds(q0, tq), Y_MLA + j * LANES:Y_MLA + (j + 1) * LANES] = (
                jnp.concatenate(outs, axis=0).T.astype(BF16))
        return carry

    span = BAND_BLOCK + 2 * WINDOW
    nb = t // BAND_BLOCK
    for i in range(nb):
        gqa_block(jnp.int32(i), WINDOW if i == 0 else 0, span - WINDOW if i == nb - 1 else span)
        if i % (nb // NA_GROUPS) == 0:
            na_group(i // (nb // NA_GROUPS))
        if i % (nb * tq // t) == 0:
            mla_block(jnp.int32(i // (nb * tq // t)), 0)


def _rope_tables(group, half):
    tok = np.arange(DEC_SEQ)
    pos = np.stack([tok // GRID_W, tok % GRID_W], axis=1).astype(np.float64)
    inv = ROPE_BASE ** (-np.arange(half, dtype=np.float64) / half)
    lane = np.arange(LANES) % group
    axis = lane // (2 * half)
    within = lane % (2 * half)
    ang = pos[:, axis] * inv[within % half][None, :]
    cos, sin = np.cos(ang), np.sin(ang)
    upper = (within >= half)[None, :]
    sin_hi = np.where(upper, sin, 0.0)
    sin_lo = np.where(upper, 0.0, -sin)
    return tuple(jnp.asarray(a, dtype=F32) for a in (cos, sin_hi, sin_lo))


def _band_mask():
    c = np.arange(BAND_BLOCK + 2 * WINDOW)[:, None]
    i = np.arange(BAND_BLOCK)[None, :]
    ok = (c >= i) & (c <= i + 2 * WINDOW)
    m = np.where(ok, 0.0, NEG_INF)
    return jnp.asarray(np.concatenate([m] * (GQA_HEADS // GQA_KV_HEADS), axis=1), dtype=F32)


def _lat_mixer_call(x, mods, g_attn, w_main, sink, w_conv, g_q, w_uq, g_kv, w_uk, w_uv_t, nab, caches, layer, name):
    n, t = DEC_BATCH, DEC_SEQ
    one = pl.Buffered(1)
    const = lambda shape: pl.BlockSpec(shape, lambda s: (0,) * len(shape), pipeline_mode=one)
    cache = lambda shape: pl.BlockSpec((None, None) + shape, lambda s: (s, layer, 0, 0), pipeline_mode=one)
    rope_g = _rope_tables(HEAD_DIM, HEAD_DIM // 4)
    rope_m = _rope_tables(MLA_ROPE, MLA_ROPE // 4)
    span = BAND_BLOCK + 2 * WINDOW
    keys = t + PAST_LEN
    scratch = [
        pltpu.VMEM((t, D_MODEL), BF16),
        pltpu.VMEM((GQA_HEADS * HEAD_DIM, t), BF16),
        pltpu.VMEM((t, LANES), BF16),
        pltpu.VMEM((GQA_KV_HEADS, HEAD_DIM + ONES_ROWS, t), BF16),
        pltpu.VMEM((PAST_LEN, LANES), BF16),
        pltpu.VMEM((GQA_KV_HEADS, HEAD_DIM + ONES_ROWS, PAST_LEN), BF16),
        pltpu.VMEM((t, NA_HEADS // 2 * LANES), BF16),
        pltpu.VMEM((NA_HEADS, LANES, t), BF16),
        pltpu.VMEM((NA_HEADS, t, 2 * LANES), BF16),
        pltpu.VMEM((NA_HEADS // 2, LANES, 2 * PAST_LEN), BF16),
        pltpu.VMEM((NA_HEADS, PAST_LEN, 2 * LANES), BF16),
        pltpu.VMEM((3 * LANES, t), BF16),
        pltpu.VMEM((MLA_HEADS // 2, keys, 2 * LANES), BF16),
        pltpu.VMEM((MLA_HEADS, MLA_V + ONES_ROWS, keys), BF16),
    ]
    return pl.pallas_call(
        functools.partial(_lat_mixer_kernel, layer=layer),
        grid=(n,),
        in_specs=[
            pl.BlockSpec(memory_space=pltpu.SMEM),
            pl.BlockSpec((None, t, D_MODEL), lambda s: (s, 0, 0), pipeline_mode=one),
            _mod_spec(layer, False, 1),
            _layer_spec((1, D_MODEL), layer, 1),
            _layer_spec((MAIN_COLS, D_MODEL), layer, 1),
            _layer_spec((CONV_K, CONV_WIDTH), layer, 1),
            _layer_spec((1, MLA_Q_RANK), layer, 1),
            _layer_spec((MLA_Q_RANK, 3 * LANES), layer, 1),
            _layer_spec((1, MLA_KV_RANK), layer, 1),
            _layer_spec((MLA_KV_RANK, 2 * LANES), layer, 1),
            _layer_spec((2 * LANES, MLA_KV_RANK), layer, 1),
        ] + [const((t, LANES))] * 6 + [
            const((span, GQA_HEADS // GQA_KV_HEADS * BAND_BLOCK)),
            _layer_spec((NA_GROUP_ROWS * GRID_W, NA_HEADS * _na_table_width()), layer, 1),
            cache((2 * HEAD_DIM, PAST_LEN)), cache((2 * HEAD_DIM, PAST_LEN)),
            cache((4 * HEAD_DIM, PAST_LEN)), cache((4 * HEAD_DIM, PAST_LEN)),
            cache((PAST_LEN, MLA_KV_RANK)), cache((MLA_ROPE, PAST_LEN)),
        ],
        out_specs=pl.BlockSpec((None, t, Y_COLS), lambda s: (s, 0, 0)),
        out_shape=jax.ShapeDtypeStruct((n, t, Y_COLS), BF16),
        scratch_shapes=scratch,
        compiler_params=pltpu.CompilerParams(
            dimension_semantics=("arbitrary",), vmem_limit_bytes=VMEM_LIMIT),
        name=name,
    )(sink, x, mods, g_attn, w_main, w_conv, g_q, w_uq, g_kv, w_uk, w_uv_t, *rope_g, *rope_m, _band_mask(), nab, *caches)


def _post_kernel(x_ref, y_ref, mod_ref, ga_ref, gm_ref, gf_ref, wg_ref, wb_ref, wo_ref, w1_ref, w2_ref,
                 o_ref, *, final):
    d = D_MODEL
    mod = lambda i: mod_ref[:, i * d:(i + 1) * d]
    bounds = (Y_CONV, Y_GQA, Y_NA, Y_MLA, Y_COLS)
    for r0 in range(0, x_ref.shape[0], POST_CHAIN_ROWS):
        rows = slice(r0, r0 + POST_CHAIN_ROWS)
        x = x_ref[rows, :]
        h = _modulated_norm(x, ga_ref[...], mod_ref, 0, 1).astype(BF16)
        merged = None
        for i in range(N_BRANCH):
            lo, hi = bounds[i], bounds[i + 1]
            gate = jax.nn.sigmoid(_dot_nt(h, wg_ref[0, i * d:(i + 1) * d, :]))
            term = gate * _dot(y_ref[rows, lo:hi], wb_ref[lo:hi, :])
            merged = term if merged is None else merged + term
        x = x + mod(2) * _dot(merged.astype(BF16), wo_ref[...])
        h = _modulated_norm(x, gm_ref[...], mod_ref, 3, 4).astype(BF16)
        mlp = None
        for c0 in range(0, D_FF, d):
            f = jnp.square(jnp.maximum(_dot(h, w1_ref[:, c0:c0 + d]), 0.0)).astype(BF16)
            term = _dot(f, w2_ref[c0:c0 + d, :])
            mlp = term if mlp is None else mlp + term
        x = x + mod(5) * mlp
        o_ref[rows, :] = _rms(x, gf_ref[...]) if final else x


def _post_call(x, y, mods, g_attn, g_mlp, g_final, w_gates, w_br, w_o, w_ff1, w_ff2, layer, ctx, name):
    n, t, _ = x.shape
    tm = 2 * POST_CHAIN_ROWS
    return pl.pallas_call(
        functools.partial(_post_kernel, final=layer == DEPTH - 1),
        grid=(n, t // tm),
        in_specs=[
            pl.BlockSpec((None, tm, D_MODEL), lambda s, i: (s, i, 0)),
            pl.BlockSpec((None, tm, Y_COLS), lambda s, i: (s, i, 0)),
            _mod_spec(layer, ctx, 2),
            _layer_spec((1, D_MODEL), layer, 2),
            _layer_spec((1, D_MODEL), layer, 2),
            pl.BlockSpec((1, D_MODEL), lambda s, i: (0, 0)),
            pl.BlockSpec((pl.Element(1), pl.Element(N_BRANCH * D_MODEL), pl.Element(D_MODEL)),
                         lambda s, i: (layer, GATE_COL0, 0), pipeline_mode=pl.Buffered(1)),
            _layer_spec((Y_COLS, D_MODEL), layer, 2),
            _layer_spec((D_MODEL, D_MODEL), layer, 2),
            _layer_spec((D_MODEL, D_FF), layer, 2),
            _layer_spec((D_FF, D_MODEL), layer, 2),
        ],
        out_specs=pl.BlockSpec((None, tm, D_MODEL), lambda s, i: (s, i, 0)),
        out_shape=jax.ShapeDtypeStruct((n, t, D_MODEL), F32),
        compiler_params=pltpu.CompilerParams(
            dimension_semantics=("arbitrary", "arbitrary"), vmem_limit_bytes=VMEM_LIMIT),
        name=name,
    )(x, y, mods, g_attn, g_mlp, g_final, w_gates, w_br, w_o, w_ff1, w_ff2)


def kernel(x_prompt, x_sample, cache_gqa_k, cache_gqa_v, cache_na_k, cache_na_v, cache_mla_ckv, cache_mla_krope, c, c_ctx, w_mod, b_mod, g_attn, g_mlp, w_in, w_conv, gqa_sink, na_rpb, mla_g_q, mla_w_uq, mla_g_kv, mla_w_ukv, w_branch_conv, w_branch_gqa, w_branch_na, w_branch_mla, w_o, w_ff1, w_ff2, g_final):
    w_main = w_gates = jnp.transpose(w_in, (0, 2, 1)).astype(BF16)
    uq = mla_w_uq.reshape(DEPTH, MLA_Q_RANK, MLA_HEADS, MLA_NOPE + MLA_ROPE)
    w_uq = jnp.concatenate([uq[..., :MLA_NOPE].reshape(DEPTH, MLA_Q_RANK, -1),
                            uq[..., MLA_NOPE:].reshape(DEPTH, MLA_Q_RANK, -1)], axis=-1).astype(BF16)
    ukv = mla_w_ukv.reshape(DEPTH, MLA_KV_RANK, MLA_HEADS, MLA_NOPE + MLA_V)
    w_uk = ukv[..., :MLA_NOPE].reshape(DEPTH, MLA_KV_RANK, -1).astype(BF16)
    w_uv = ukv[..., MLA_NOPE:].reshape(DEPTH, MLA_KV_RANK, -1).astype(BF16)
    w_uk_t, w_uv_t = jnp.transpose(w_uk, (0, 2, 1)), jnp.transpose(w_uv, (0, 2, 1))
    w_br = jnp.concatenate([w_branch_conv, w_branch_gqa, w_branch_na, w_branch_mla], axis=1).astype(BF16)
    w_o_b, w_ff1_b, w_ff2_b = w_o.astype(BF16), w_ff1.astype(BF16), w_ff2.astype(BF16)
    g_a, g_m = g_attn[:, None, :], g_mlp[:, None, :]
    g_q, g_kv, g_f = mla_g_q[:, None, :], mla_g_kv[:, None, :], g_final[None, :]

    c16 = jnp.concatenate([c, c_ctx[None, :], jnp.zeros((MOD_ROWS - DEC_BATCH - 1, D_MODEL), F32)], axis=0)
    mods = _mod_call(c16, w_mod, b_mod).reshape(DEPTH, MOD_ROWS, 1, 6 * D_MODEL)
    nab = _nab_call(na_rpb)

    heads_t = lambda a: jnp.transpose(a, (0, 1, 3, 4, 2)).reshape(a.shape[0], DEPTH, -1, a.shape[2])
    caches = (heads_t(cache_gqa_k), heads_t(cache_gqa_v), heads_t(cache_na_k), heads_t(cache_na_v),
              cache_mla_ckv, jnp.transpose(cache_mla_krope, (0, 1, 3, 2)))

    h_ctx, h_lat = x_prompt, x_sample
    states = None
    for l in range(DEPTH):
        mixer_w = (gqa_sink, w_conv, g_q, w_uq, g_kv)
        post_w = (g_a, g_m, g_f, w_gates, w_br, w_o_b, w_ff1_b, w_ff2_b)

        y, states = _ctx_mixer_call(h_ctx, mods, g_a, w_main, *mixer_w, w_uk_t, w_uv, states, l, f"mixer_ctx_{l}")
        flat = lambda a: a.reshape(1, BATCH * SEQ, a.shape[-1])
        h_ctx = _post_call(flat(h_ctx), flat(y), mods, *post_w, l, True, f"post_ctx_{l}").reshape(BATCH, SEQ, D_MODEL)

        y = _lat_mixer_call(h_lat, mods, g_a, w_main, *mixer_w, w_uk, w_uv_t, nab, caches, l, f"mixer_lat_{l}")
        h_lat = _post_call(h_lat, y, mods, *post_w, l, False, f"post_lat_{l}")

    def heads_out(a, heads):
        return jnp.transpose(a.reshape(BATCH, DEPTH, heads, HEAD_DIM, SEQ), (0, 1, 4, 2, 3))

    kg, vg, kn, vn, ckv, kr = states
    return (h_ctx, h_lat, heads_out(kg, GQA_KV_HEADS), heads_out(vg, GQA_KV_HEADS),
            heads_out(kn, NA_HEADS), heads_out(vn, NA_HEADS), ckv, jnp.transpose(kr, (0, 1, 3, 2)))
```

```python
import functools
import math

import numpy as np
import jax
import jax.numpy as jnp
from jax import lax
from jax.experimental import pallas as pl
from jax.experimental.pallas import tpu as pltpu

D_MODEL = 1024
BATCH = 32
SEQ = 256
DEPTH = 2
DEC_BATCH = 8
DEC_SEQ = 1024
PAST_LEN = 512
GRID_W = 64
GRID_ROWS = DEC_SEQ // GRID_W
HEAD_DIM = 64
CONV_WIDTH = 256
CONV_K = 3
GQA_HEADS = 8
GQA_KV_HEADS = 2
WINDOW = 128
BAND_BLOCK = 128
NA_HEADS = 4
NA_WIN_H = 8
NA_WIN_W = 16
MLA_HEADS = 4
MLA_Q_RANK = 256
MLA_KV_RANK = 128
MLA_NOPE = 64
MLA_ROPE = 32
MLA_V = 64
D_FF = 4 * D_MODEL
N_BRANCH = 4
ROPE_BASE = 10000.0
EPS = 1e-6
NEG_INF = -1e30
LOG2E = math.log2(math.e)
ATTN_SCALE = HEAD_DIM ** -0.5
MLA_SCALE = (MLA_NOPE + MLA_ROPE) ** -0.5

LANES = 128
MOD_ROWS = 16
CTX_MOD_ROW = DEC_BATCH

C_CB, C_CC, C_CV = 0, 256, 512
C_GQ, C_GK, C_GV = 768, 1280, 1408
C_NQ, C_NK, C_NV = 1536, 1792, 2048
C_MQ, C_MKV, C_MKR = 2304, 2560, 2688
MAIN_COLS = 2816
GATE_COL0 = 2720
Y_CONV, Y_GQA, Y_NA, Y_MLA = 0, 256, 768, 1024
Y_COLS = 1280
NA_GROUPS = 4
NA_GROUP_ROWS = GRID_ROWS // NA_GROUPS

VMEM_LIMIT = 56 * 1024 * 1024
POST_CHAIN_ROWS = 256
CTX_SEQS = 4
CTX_SEQS_FIRST = 2

F32 = jnp.float32
BF16 = jnp.bfloat16


def _dot(a, b):
    return jnp.dot(a, b, preferred_element_type=F32)


def _dot_nt(a, b):
    return lax.dot_general(a, b, (((1,), (1,)), ((), ())), preferred_element_type=F32)


def _rms(x, g):
    return x * lax.rsqrt(jnp.mean(x * x, axis=-1, keepdims=True) + EPS) * g


def _lane_lt(shape, n):
    return lax.broadcasted_iota(jnp.int32, shape, len(shape) - 1) < n


def _row_lt(shape, n):
    return lax.broadcasted_iota(jnp.int32, shape, 0) < n


def _row_group(shape, lo, hi):
    row = lax.broadcasted_iota(jnp.int32, shape, 0)
    return (row >= lo) & (row < hi)


def _short_conv(cb, cc, cv, w):
    u = cc * cv
    t = u.shape[0]
    row = lax.broadcasted_iota(jnp.int32, u.shape, 0)
    prev = jnp.where(row == 0, 0.0, pltpu.roll(u, 1, 0))
    nxt = jnp.where(row == t - 1, 0.0, pltpu.roll(u, t - 1, 0))
    return cb * (prev * w[0:1, :] + u * w[1:2, :] + nxt * w[2:3, :])


def _rope(x, cos, sin_hi, sin_lo, half):
    n = x.shape[-1]
    return x * cos + pltpu.roll(x, n - half, 1) * sin_lo + pltpu.roll(x, half, 1) * sin_hi


def _mod_kernel(c_ref, w_ref, b_ref, o_ref):
    c = c_ref[...]
    s = c * jax.nn.sigmoid(c)
    o_ref[...] = _dot(s.astype(BF16), w_ref[...].astype(BF16)) + b_ref[...]


def _mod_call(c16, w_mod, b_mod):
    tn = 1536
    return pl.pallas_call(
        _mod_kernel,
        grid=(DEPTH, 6 * D_MODEL // tn),
        in_specs=[
            pl.BlockSpec((MOD_ROWS, D_MODEL), lambda l, j: (0, 0)),
            pl.BlockSpec((None, D_MODEL, tn), lambda l, j: (l, 0, j)),
            pl.BlockSpec((None, 1, tn), lambda l, j: (l, 0, j)),
        ],
        out_specs=pl.BlockSpec((None, MOD_ROWS, tn), lambda l, j: (l, 0, j)),
        out_shape=jax.ShapeDtypeStruct((DEPTH, MOD_ROWS, 6 * D_MODEL), F32),
        compiler_params=pltpu.CompilerParams(
            dimension_semantics=("arbitrary", "arbitrary"), vmem_limit_bytes=VMEM_LIMIT),
        name="adaln_mod",
    )(c16, w_mod, b_mod.reshape(DEPTH, 1, 6 * D_MODEL))


def _na_window_start(r):
    return min(max(r - NA_WIN_H // 2, 0), GRID_ROWS - NA_WIN_H)


def _na_group_slab(k):
    starts = [_na_window_start(r) for r in range(k * NA_GROUP_ROWS, (k + 1) * NA_GROUP_ROWS)]
    lo = min(starts) // 2 * 2
    n = -(-(max(starts) + NA_WIN_H - lo) // 4) * 4
    assert lo + n <= GRID_ROWS
    return lo, n


def _na_group_kinds():
    kinds, of_group = [], []
    for k in range(NA_GROUPS):
        lo, n = _na_group_slab(k)
        layout = {}
        for dr in range(NA_GROUP_ROWS):
            r = k * NA_GROUP_ROWS + dr
            r0 = _na_window_start(r)
            for a in range(n):
                inside = r0 <= lo + a < r0 + NA_WIN_H
                layout[(dr, a)] = lo + a - r + NA_WIN_H - 1 if inside else None
        if (n, layout) not in kinds:
            kinds.append((n, layout))
        of_group.append(kinds.index((n, layout)))
    return kinds, of_group


def _na_table_width():
    return sum(n for n, _ in _na_group_kinds()[0]) * GRID_W


def _nab_kernel(rpb_ref, o_ref):
    l = pl.program_id(0)
    c = lax.broadcasted_iota(jnp.int32, (GRID_W, GRID_W), 0)
    w = lax.broadcasted_iota(jnp.int32, (GRID_W, GRID_W), 1)
    dc = w - c + (NA_WIN_W - 1)
    c0 = jnp.clip(c - NA_WIN_W // 2, 0, GRID_W - NA_WIN_W)
    outside = (w < c0) | (w >= c0 + NA_WIN_W)
    n_dr, n_dc = 2 * NA_WIN_H - 1, 2 * NA_WIN_W - 1
    kinds, _ = _na_group_kinds()
    width = _na_table_width()

    def put(h, d, tile):
        col0 = h * width
        for n, layout in kinds:
            for (dr, a), want in layout.items():
                if want == d:
                    o_ref[dr * GRID_W:(dr + 1) * GRID_W, col0 + a * GRID_W:col0 + (a + 1) * GRID_W] = tile
            col0 += n * GRID_W

    for h in range(NA_HEADS):
        put(h, None, jnp.full((GRID_W, GRID_W), NEG_INF, F32))
        for d in range(n_dr):
            base = ((l * NA_HEADS + h) * n_dr + d) * n_dc
            tile = jnp.full((GRID_W, GRID_W), NEG_INF, F32)
            for j in range(n_dc):
                tile = jnp.where(dc == j, rpb_ref[base + j] * LOG2E, tile)
            put(h, d, jnp.where(outside, NEG_INF, tile))


def _nab_call(na_rpb):
    shape = (NA_GROUP_ROWS * GRID_W, NA_HEADS * _na_table_width())
    return pl.pallas_call(
        _nab_kernel,
        grid=(DEPTH,),
        in_specs=[pl.BlockSpec(memory_space=pltpu.SMEM)],
        out_specs=pl.BlockSpec((None,) + shape, lambda l: (l, 0, 0)),
        out_shape=jax.ShapeDtypeStruct((DEPTH,) + shape, F32),
        compiler_params=pltpu.CompilerParams(dimension_semantics=("arbitrary",), vmem_limit_bytes=VMEM_LIMIT),
        name="na_bias_tables",
    )(na_rpb.reshape(-1))


def _mod_spec(layer, ctx, n_grid):
    if n_grid == 1:
        index = (lambda s: (layer, CTX_MOD_ROW, 0, 0)) if ctx else (lambda s: (layer, s, 0, 0))
    else:
        index = (lambda s, i: (layer, CTX_MOD_ROW, 0, 0)) if ctx else (lambda s, i: (layer, s, 0, 0))
    return pl.BlockSpec((None, None, 1, 6 * D_MODEL), index)


def _layer_spec(shape, layer, n_grid):
    zeros = (0,) * len(shape)
    index = (lambda s: (layer,) + zeros) if n_grid == 1 else (lambda s, i: (layer,) + zeros)
    return pl.BlockSpec((None,) + tuple(shape), index, pipeline_mode=pl.Buffered(1))


_COLUMN_GROUPS = ((C_CB, C_GQ), (C_GQ, C_NQ), (C_NQ, C_MQ), (C_MQ, MAIN_COLS))


def _modulated_norm(x, g, mod_ref, shift, scale):
    d = D_MODEL
    return _rms(x, g) * (1.0 + mod_ref[:, scale * d:(scale + 1) * d]) + mod_ref[:, shift * d:(shift + 1) * d]


def _projected_columns(h_ref, w_ref):
    cache = {}

    def col(c, n):
        lo, hi = next(g for g in _COLUMN_GROUPS if g[0] <= c and c + n <= g[1])
        if lo not in cache:
            cache.clear()
            cache[lo] = _dot_nt(h_ref[...], w_ref[lo:hi, :])
        return cache[lo][:, c - lo:c - lo + n]

    return col


def _key_planes(k_t, even_first, odd_first):
    lo = _row_lt(k_t.shape, HEAD_DIM)
    swapped = None
    if not even_first or odd_first:
        swapped = pltpu.roll(k_t, HEAD_DIM, 0)
    top = jnp.where(lo, k_t if even_first else swapped, 0.0)
    bot = jnp.where(lo, 0.0, swapped if odd_first else k_t)
    return top.astype(BF16), bot.astype(BF16)


def _value_planes(pair, even_first, odd_first):
    lo = _lane_lt(pair.shape, HEAD_DIM)
    swapped = None
    if not even_first or odd_first:
        swapped = pltpu.roll(pair, HEAD_DIM, 1)
    top = jnp.where(lo, pair if even_first else swapped, 0.0)
    bot = jnp.where(lo, 0.0, swapped if odd_first else pair)
    one_e = jnp.where(lo, 1.0, 0.0)
    return (jnp.concatenate([top, one_e], axis=1).astype(BF16),
            jnp.concatenate([bot, 1.0 - one_e], axis=1).astype(BF16))


def _probabilities(parts, sink=None):
    m = parts[0].max(axis=-1, keepdims=True)
    for p in parts[1:]:
        m = jnp.maximum(m, p.max(axis=-1, keepdims=True))
    if sink is not None:
        m = jnp.maximum(m, sink)
    probs = [jnp.exp2((p - m).astype(BF16)) for p in parts]
    return probs, (None if sink is None else jnp.exp2(sink - m))


def _attend(terms, sink_e=None, sink_o=None):
    o = None
    for p, v in terms:
        t = _dot(p, v)
        o = t if o is None else o + t
    den = o[:, LANES:]
    if sink_e is not None:
        den = den + jnp.where(_lane_lt(den.shape, HEAD_DIM), sink_e, sink_o)
    return o[:, :LANES] / den


ONES_ROWS = 16


def _value_plane_t(v_t):
    return jnp.concatenate([v_t, jnp.ones((ONES_ROWS, v_t.shape[1]), v_t.dtype)], axis=0).astype(BF16)


def _probabilities_t(parts, sink=None):
    m = parts[0].max(axis=0, keepdims=True)
    for p in parts[1:]:
        m = jnp.maximum(m, p.max(axis=0, keepdims=True))
    if sink is not None:
        m = jnp.maximum(m, sink)
    probs = [jnp.exp2((p - m).astype(BF16)) for p in parts]
    return probs, (None if sink is None else jnp.exp2(sink - m))


def _attend_t(terms, sink_num=None):
    o = None
    for v_t, p in terms:
        t = _dot(v_t, p)
        o = t if o is None else o + t
    den = o[HEAD_DIM:HEAD_DIM + 1, :]
    if sink_num is not None:
        den = den + sink_num
    return o[0:HEAD_DIM, :] / den


def _mla_key_plane(kn_t_pair, kr4_t, j, odd):
    h = 2 * j + odd
    lo = _row_lt(kn_t_pair.shape, HEAD_DIM)
    nope = jnp.where(lo, 0.0, kn_t_pair) if odd else jnp.where(lo, kn_t_pair, 0.0)
    rope = jnp.where(_row_group(kr4_t.shape, h * MLA_ROPE, (h + 1) * MLA_ROPE), kr4_t, 0.0)
    return jnp.concatenate([nope, rope], axis=0).astype(BF16)


def _ctx_mixer_kernel(sink_ref, x_ref, mod_ref, ga_ref, win_ref, wc_ref, gq_ref, wuq_ref, gkv_ref, wukt_ref, wuv_ref,
                      *rest, layer, first):
    y_ref, kg_ref, vg_ref, kn_ref, vn_ref, ckv_ref, kr_ref, h_s = rest[-8:]
    t, nb = SEQ, x_ref.shape[0]
    for b in range(nb):
        h_s[b * t:(b + 1) * t, :] = _modulated_norm(x_ref[b], ga_ref[...], mod_ref, 0, 1).astype(BF16)
    col_all = _projected_columns(h_s, win_ref)

    def put_state(ref, b, val):
        if first:
            for d in range(DEPTH):
                ref[b, d] = val if d == layer else jnp.zeros_like(val)
        else:
            ref[b] = val

    def conv(b, col):
        y_ref[b, :, Y_CONV:Y_CONV + CONV_WIDTH] = _short_conv(
            col(C_CB, CONV_WIDTH), col(C_CC, CONV_WIDTH), col(C_CV, CONV_WIDTH), wc_ref[...]).astype(BF16)

    def gqa(b, col):
        kg_t, vpair = col(C_GK, LANES).T, col(C_GV, LANES)
        put_state(kg_ref, b, kg_t)
        put_state(vg_ref, b, vpair.T)
        row2 = lax.broadcasted_iota(jnp.int32, (2 * t, 1), 0) < t
        for g in range(GQA_KV_HEADS):
            ke, ko = _key_planes(kg_t, g == 0, g == 0)
            ve, vo = _value_planes(vpair, g == 0, g == 0)
            q = jnp.concatenate([col(C_GQ + (2 * g) * LANES, LANES), col(C_GQ + (2 * g + 1) * LANES, LANES)], axis=0)
            q = (q * (ATTN_SCALE * LOG2E)).astype(BF16)
            s = _dot(q, jnp.concatenate([ke, ko], axis=1))
            sink = [jnp.where(row2, sink_ref[layer, 4 * g + odd], sink_ref[layer, 4 * g + 2 + odd]) * LOG2E
                    for odd in range(2)]
            (pe,), xe = _probabilities([s[:, 0:t]], sink[0])
            (po,), xo = _probabilities([s[:, t:2 * t]], sink[1])
            o = _attend([(pe, ve), (po, vo)], xe, xo).astype(BF16)
            c0 = Y_GQA + (2 * g) * LANES
            y_ref[b, :, c0:c0 + LANES] = o[0:t]
            y_ref[b, :, c0 + LANES:c0 + 2 * LANES] = o[t:2 * t]

    def na(b, col):
        kn_t = col(C_NK, 2 * LANES).T
        put_state(kn_ref, b, kn_t)
        put_state(vn_ref, b, col(C_NV, 2 * LANES).T)
        for j in range(NA_HEADS // 2):
            ke, ko = _key_planes(kn_t[j * LANES:(j + 1) * LANES, :], True, False)
            ve, vo = _value_planes(col(C_NV + j * LANES, LANES), True, False)
            q = (col(C_NQ + j * LANES, LANES) * (ATTN_SCALE * LOG2E)).astype(BF16)
            s = _dot(q, jnp.concatenate([ke, ko], axis=1))
            (pe,), _ = _probabilities([s[:, 0:t]])
            (po,), _ = _probabilities([s[:, t:2 * t]])
            y_ref[b, :, Y_NA + j * LANES:Y_NA + (j + 1) * LANES] = _attend([(pe, ve), (po, vo)]).astype(BF16)

    def mla(b, col):
        ckv = _rms(col(C_MKV, MLA_KV_RANK), gkv_ref[...])
        put_state(ckv_ref, b, ckv)
        kr_t = col(C_MKR, LANES).T[0:MLA_ROPE, :]
        put_state(kr_ref, b, kr_t)
        ckv_b = ckv.astype(BF16)
        q = _dot(_rms(col(C_MQ, MLA_Q_RANK), gq_ref[...]).astype(BF16), wuq_ref[...]) * (MLA_SCALE * LOG2E)
        kn_t_all = _dot_nt(wukt_ref[...], ckv_b)
        v_all = _dot(ckv_b, wuv_ref[...])
        kr4_t = jnp.concatenate([kr_t] * MLA_HEADS, axis=0)
        q_rope = q[:, 2 * LANES:3 * LANES]
        for j in range(MLA_HEADS // 2):
            qj = jnp.concatenate([q[:, j * LANES:(j + 1) * LANES], q_rope], axis=1).astype(BF16)
            kn_t_pair = kn_t_all[j * LANES:(j + 1) * LANES, :]
            keys = jnp.concatenate(
                [_mla_key_plane(kn_t_pair, kr4_t, j, 0), _mla_key_plane(kn_t_pair, kr4_t, j, 1)], axis=1)
            s = _dot(qj, keys)
            ve, vo = _value_planes(v_all[:, j * LANES:(j + 1) * LANES], True, False)
            (pe,), _ = _probabilities([s[:, 0:t]])
            (po,), _ = _probabilities([s[:, t:2 * t]])
            y_ref[b, :, Y_MLA + j * LANES:Y_MLA + (j + 1) * LANES] = _attend([(pe, ve), (po, vo)]).astype(BF16)

    for mixer in (conv, gqa, na, mla):
        for b in range(nb):
            mixer(b, lambda c, n, b=b: col_all(c, n)[b * t:(b + 1) * t])


def _ctx_mixer_call(x, mods, g_attn, w_main, sink, w_conv, g_q, w_uq, g_kv, w_uk_t, w_uv, prev_states, layer, name):
    first = prev_states is None
    n, nb = BATCH, (CTX_SEQS_FIRST if first else CTX_SEQS)
    state_shapes = ((2 * HEAD_DIM, SEQ), (2 * HEAD_DIM, SEQ), (4 * HEAD_DIM, SEQ), (4 * HEAD_DIM, SEQ),
                    (SEQ, MLA_KV_RANK), (MLA_ROPE, SEQ))
    if first:
        state_spec = lambda shape: pl.BlockSpec((nb, DEPTH) + shape, lambda s: (s, 0, 0, 0))
    else:
        state_spec = lambda shape: pl.BlockSpec((nb, None) + shape, lambda s: (s, layer, 0, 0))
    in_specs = [
        pl.BlockSpec(memory_space=pltpu.SMEM),
        pl.BlockSpec((nb, SEQ, D_MODEL), lambda s: (s, 0, 0)),
        _mod_spec(layer, True, 1),
        _layer_spec((1, D_MODEL), layer, 1),
        _layer_spec((MAIN_COLS, D_MODEL), layer, 1),
        _layer_spec((CONV_K, CONV_WIDTH), layer, 1),
        _layer_spec((1, MLA_Q_RANK), layer, 1),
        _layer_spec((MLA_Q_RANK, 3 * LANES), layer, 1),
        _layer_spec((1, MLA_KV_RANK), layer, 1),
        _layer_spec((2 * LANES, MLA_KV_RANK), layer, 1),
        _layer_spec((MLA_KV_RANK, 2 * LANES), layer, 1),
    ]
    args = [sink, x, mods, g_attn, w_main, w_conv, g_q, w_uq, g_kv, w_uk_t, w_uv]
    aliases = {}
    if not first:
        for i, st in enumerate(prev_states):
            aliases[len(args)] = 1 + i
            in_specs.append(pl.BlockSpec(memory_space=pl.ANY))
            args.append(st)
    outs = pl.pallas_call(
        functools.partial(_ctx_mixer_kernel, layer=layer, first=first),
        grid=(n // nb,),
        in_specs=in_specs,
        out_specs=[pl.BlockSpec((nb, SEQ, Y_COLS), lambda s: (s, 0, 0))] + [state_spec(s) for s in state_shapes],
        out_shape=[jax.ShapeDtypeStruct((n, SEQ, Y_COLS), BF16)]
        + [jax.ShapeDtypeStruct((n, DEPTH) + s, F32) for s in state_shapes],
        input_output_aliases=aliases,
        scratch_shapes=[pltpu.VMEM((nb * SEQ, D_MODEL), BF16)],
        compiler_params=pltpu.CompilerParams(
            dimension_semantics=("arbitrary",), vmem_limit_bytes=VMEM_LIMIT),
        name=name,
    )(*args)
    return outs[0], outs[1:]


def _lat_mixer_kernel(sink_ref, x_ref, mod_ref, ga_ref, win_ref, wc_ref, gq_ref, wuq_ref, gkv_ref, wuk_ref, wuvt_ref,
                      rc_ref, rsh_ref, rsl_ref, mc_ref, msh_ref, msl_ref, band_ref, nab_ref,
                      cgk_ref, cgv_ref, cnk_ref, cnv_ref, cckv_ref, ckr_ref,
                      y_ref,
                      h_s, gq_s, gk_s, gv_s, gkc_s, gvc_s, nq_s, nk_s, nv_s, nkc_s, nvc_s, mq_s, mk_s, mv_s, *, layer):
    t = DEC_SEQ
    h_s[...] = _modulated_norm(x_ref[...], ga_ref[...], mod_ref, 0, 1).astype(BF16)
    col = _projected_columns(h_s, win_ref)
    rope64 = lambda x: _rope(x, rc_ref[...], rsh_ref[...], rsl_ref[...], HEAD_DIM // 4)
    rope32 = lambda x: _rope(x, mc_ref[...], msh_ref[...], msl_ref[...], MLA_ROPE // 4)

    y_ref[:, Y_CONV:Y_CONV + CONV_WIDTH] = _short_conv(
        col(C_CB, CONV_WIDTH), col(C_CC, CONV_WIDTH), col(C_CV, CONV_WIDTH), wc_ref[...]).astype(BF16)

    group = GQA_HEADS // GQA_KV_HEADS
    for j in range(GQA_HEADS // 2):
        q_pair = rope64(col(C_GQ + j * LANES, LANES)) * (ATTN_SCALE * LOG2E)
        gq_s[j * LANES:(j + 1) * LANES, :] = q_pair.T.astype(BF16)
    gk_s[...] = rope64(col(C_GK, LANES)).astype(BF16)
    gkc_s[...] = cgk_ref[...].T.astype(BF16)
    v_t = col(C_GV, LANES).T
    for g in range(GQA_KV_HEADS):
        gv_s[g] = _value_plane_t(v_t[g * HEAD_DIM:(g + 1) * HEAD_DIM, :])
        gvc_s[g] = _value_plane_t(cgv_ref[g * HEAD_DIM:(g + 1) * HEAD_DIM, :])

    def gqa_block(b, c_lo, c_hi):
        q0 = pl.multiple_of(b * BAND_BLOCK, BAND_BLOCK)
        k0 = pl.multiple_of(q0 + (c_lo - WINDOW), BAND_BLOCK)
        n = c_hi - c_lo
        band = band_ref[c_lo:c_hi, :]
        lane_head = lax.shift_right_logical(
            lax.broadcasted_iota(jnp.int32, (1, group * BAND_BLOCK), 1), BAND_BLOCK.bit_length() - 1)
        zeros = jnp.zeros((HEAD_DIM, group * BAND_BLOCK), BF16)
        for g in range(GQA_KV_HEADS):
            q_t = jnp.concatenate(
                [gq_s[(group * g + h) * HEAD_DIM:(group * g + h + 1) * HEAD_DIM, pl.ds(q0, BAND_BLOCK)]
                 for h in range(group)], axis=1)
            q_t = jnp.concatenate([q_t, zeros] if g == 0 else [zeros, q_t], axis=0)
            s_loc = _dot(gk_s[pl.ds(k0, n), :], q_t) + band
            s_ctx = _dot(gkc_s[...], q_t)
            sink = sink_ref[layer, group * g + group - 1]
            for h in range(group - 2, -1, -1):
                sink = jnp.where(lane_head == h, sink_ref[layer, group * g + h], sink)
            (p_loc, p_ctx), x = _probabilities_t([s_loc, s_ctx], sink * LOG2E)
            o = _attend_t([(gv_s[g, :, pl.ds(k0, n)], p_loc), (gvc_s[g], p_ctx)], x)
            for pr in range(group // 2):
                pair = jnp.concatenate([o[:, (2 * pr) * BAND_BLOCK:(2 * pr + 1) * BAND_BLOCK],
                                        o[:, (2 * pr + 1) * BAND_BLOCK:(2 * pr + 2) * BAND_BLOCK]], axis=0)
                c0 = Y_GQA + (group // 2 * g + pr) * LANES
                y_ref[pl.ds(q0, BAND_BLOCK), c0:c0 + LANES] = pair.T.astype(BF16)

    for j in range(NA_HEADS // 2):
        nq_s[:, j * LANES:(j + 1) * LANES] = (col(C_NQ + j * LANES, LANES) * (ATTN_SCALE * LOG2E)).astype(BF16)
        nk_s[2 * j], nk_s[2 * j + 1] = _key_planes(col(C_NK + j * LANES, LANES).T, True, False)
        nv_s[2 * j], nv_s[2 * j + 1] = _value_planes(col(C_NV + j * LANES, LANES), True, False)
        nkc_s[j] = jnp.concatenate(_key_planes(cnk_ref[j * LANES:(j + 1) * LANES, :], True, False), axis=1)
        nvc_s[2 * j], nvc_s[2 * j + 1] = _value_planes(cnv_ref[j * LANES:(j + 1) * LANES, :].T, True, False)

    na_kinds, na_kind_of = _na_group_kinds()
    na_width = _na_table_width()

    def na_group(k):
        lo, n = _na_group_slab(k)
        q0, nq = k * NA_GROUP_ROWS * GRID_W, NA_GROUP_ROWS * GRID_W
        k0, nk = lo * GRID_W, n * GRID_W
        b0 = sum(kn for kn, _ in na_kinds[:na_kind_of[k]]) * GRID_W
        for j in range(NA_HEADS // 2):
            q = nq_s[q0:q0 + nq, j * LANES:(j + 1) * LANES]
            s_ctx = _dot(q, nkc_s[j])
            terms = []
            for odd in range(2):
                h = 2 * j + odd
                s_loc = _dot(q, nk_s[h, :, k0:k0 + nk]) + nab_ref[:, h * na_width + b0:h * na_width + b0 + nk]
                (p_loc, p_ctx), _ = _probabilities([s_loc, s_ctx[:, odd * PAST_LEN:(odd + 1) * PAST_LEN]])
                terms += [(p_loc, nv_s[h, k0:k0 + nk, :]), (p_ctx, nvc_s[h])]
            y_ref[q0:q0 + nq, Y_NA + j * LANES:Y_NA + (j + 1) * LANES] = _attend(terms).astype(BF16)

    ckv_b = _rms(col(C_MKV, MLA_KV_RANK), gkv_ref[...]).astype(BF16)
    cckv_b = cckv_ref[...].astype(BF16)
    q = _dot(_rms(col(C_MQ, MLA_Q_RANK), gq_ref[...]).astype(BF16), wuq_ref[...]) * (MLA_SCALE * LOG2E)
    for i in range(3):
        tile = q[:, i * LANES:(i + 1) * LANES]
        mq_s[i * LANES:(i + 1) * LANES, :] = (rope32(tile) if i == 2 else tile).T.astype(BF16)
    kr = jnp.where(_lane_lt((t, LANES), MLA_ROPE), rope32(col(C_MKR, LANES)), 0.0)
    kr_c = jnp.concatenate([ckr_ref[...], jnp.zeros((LANES - MLA_ROPE, PAST_LEN), F32)], axis=0).T
    for rows, ckv_x, kr_x in ((slice(0, t), ckv_b, kr), (slice(t, t + PAST_LEN), cckv_b, kr_c)):
        kn_all = _dot(ckv_x, wuk_ref[...])
        v_t_all = _dot_nt(wuvt_ref[...], ckv_x)
        for j in range(MLA_HEADS // 2):
            mk_s[j, rows, :] = jnp.concatenate([kn_all[:, j * LANES:(j + 1) * LANES], kr_x], axis=1).astype(BF16)
        for h in range(MLA_HEADS):
            mv_s[h, :, rows] = _value_plane_t(v_t_all[h * MLA_V:(h + 1) * MLA_V, :])

    tq = 256

    def mla_block(i, carry):
        q0 = pl.multiple_of(i * tq, tq)
        zeros = lambda r: jnp.zeros((r, tq), BF16)
        q_nope = lambda h: mq_s[h * MLA_NOPE:(h + 1) * MLA_NOPE, pl.ds(q0, tq)]
        q_rope = lambda h: mq_s[MLA_HEADS * MLA_NOPE + h * MLA_ROPE:MLA_HEADS * MLA_NOPE + (h + 1) * MLA_ROPE,
                                pl.ds(q0, tq)]
        pad = 2 * LANES - 2 * MLA_NOPE - MLA_ROPE
        for j in range(MLA_HEADS // 2):
            he, ho = 2 * j, 2 * j + 1
            q_e = jnp.concatenate([q_nope(he), zeros(MLA_NOPE), q_rope(he), zeros(pad)], axis=0)
            q_o = jnp.concatenate([zeros(MLA_NOPE), q_nope(ho), q_rope(ho), zeros(pad)], axis=0)
            s = _dot(mk_s[j], jnp.concatenate([q_e, q_o], axis=1))
            outs = []
            for odd in range(2):
                (p,), _ = _probabilities_t([s[:, odd * tq:(odd + 1) * tq]])
                outs.append(_attend_t([(mv_s[2 * j + odd], p)]))
            y_ref[pl.ds(q0, tq), Y_MLA + j * LANES:Y_MLA + (j + 1) * LANES] = (
                jnp.concatenate(outs, axis=0).T.astype(BF16))
        return carry

    span = BAND_BLOCK + 2 * WINDOW
    nb = t // BAND_BLOCK
    for i in range(nb):
        gqa_block(jnp.int32(i), WINDOW if i == 0 else 0, span - WINDOW if i == nb - 1 else span)
        if i % (nb // NA_GROUPS) == 0:
            na_group(i // (nb // NA_GROUPS))
        if i % (nb * tq // t) == 0:
            mla_block(jnp.int32(i // (nb * tq // t)), 0)


def _rope_tables(group, half):
    tok = np.arange(DEC_SEQ)
    pos = np.stack([tok // GRID_W, tok % GRID_W], axis=1).astype(np.float64)
    inv = ROPE_BASE ** (-np.arange(half, dtype=np.float64) / half)
    lane = np.arange(LANES) % group
    axis = lane // (2 * half)
    within = lane % (2 * half)
    ang = pos[:, axis] * inv[within % half][None, :]
    cos, sin = np.cos(ang), np.sin(ang)
    upper = (within >= half)[None, :]
    sin_hi = np.where(upper, sin, 0.0)
    sin_lo = np.where(upper, 0.0, -sin)
    return tuple(jnp.asarray(a, dtype=F32) for a in (cos, sin_hi, sin_lo))


def _band_mask():
    c = np.arange(BAND_BLOCK + 2 * WINDOW)[:, None]
    i = np.arange(BAND_BLOCK)[None, :]
    ok = (c >= i) & (c <= i + 2 * WINDOW)
    m = np.where(ok, 0.0, NEG_INF)
    return jnp.asarray(np.concatenate([m] * (GQA_HEADS // GQA_KV_HEADS), axis=1), dtype=F32)


def _lat_mixer_call(x, mods, g_attn, w_main, sink, w_conv, g_q, w_uq, g_kv, w_uk, w_uv_t, nab, caches, layer, name):
    n, t = DEC_BATCH, DEC_SEQ
    one = pl.Buffered(1)
    const = lambda shape: pl.BlockSpec(shape, lambda s: (0,) * len(shape), pipeline_mode=one)
    cache = lambda shape: pl.BlockSpec((None, None) + shape, lambda s: (s, layer, 0, 0), pipeline_mode=one)
    rope_g = _rope_tables(HEAD_DIM, HEAD_DIM // 4)
    rope_m = _rope_tables(MLA_ROPE, MLA_ROPE // 4)
    span = BAND_BLOCK + 2 * WINDOW
    keys = t + PAST_LEN
    scratch = [
        pltpu.VMEM((t, D_MODEL), BF16),
        pltpu.VMEM((GQA_HEADS * HEAD_DIM, t), BF16),
        pltpu.VMEM((t, LANES), BF16),
        pltpu.VMEM((GQA_KV_HEADS, HEAD_DIM + ONES_ROWS, t), BF16),
        pltpu.VMEM((PAST_LEN, LANES), BF16),
        pltpu.VMEM((GQA_KV_HEADS, HEAD_DIM + ONES_ROWS, PAST_LEN), BF16),
        pltpu.VMEM((t, NA_HEADS // 2 * LANES), BF16),
        pltpu.VMEM((NA_HEADS, LANES, t), BF16),
        pltpu.VMEM((NA_HEADS, t, 2 * LANES), BF16),
        pltpu.VMEM((NA_HEADS // 2, LANES, 2 * PAST_LEN), BF16),
        pltpu.VMEM((NA_HEADS, PAST_LEN, 2 * LANES), BF16),
        pltpu.VMEM((3 * LANES, t), BF16),
        pltpu.VMEM((MLA_HEADS // 2, keys, 2 * LANES), BF16),
        pltpu.VMEM((MLA_HEADS, MLA_V + ONES_ROWS, keys), BF16),
    ]
    return pl.pallas_call(
        functools.partial(_lat_mixer_kernel, layer=layer),
        grid=(n,),
        in_specs=[
            pl.BlockSpec(memory_space=pltpu.SMEM),
            pl.BlockSpec((None, t, D_MODEL), lambda s: (s, 0, 0), pipeline_mode=one),
            _mod_spec(layer, False, 1),
            _layer_spec((1, D_MODEL), layer, 1),
            _layer_spec((MAIN_COLS, D_MODEL), layer, 1),
            _layer_spec((CONV_K, CONV_WIDTH), layer, 1),
            _layer_spec((1, MLA_Q_RANK), layer, 1),
            _layer_spec((MLA_Q_RANK, 3 * LANES), layer, 1),
            _layer_spec((1, MLA_KV_RANK), layer, 1),
            _layer_spec((MLA_KV_RANK, 2 * LANES), layer, 1),
            _layer_spec((2 * LANES, MLA_KV_RANK), layer, 1),
        ] + [const((t, LANES))] * 6 + [
            const((span, GQA_HEADS // GQA_KV_HEADS * BAND_BLOCK)),
            _layer_spec((NA_GROUP_ROWS * GRID_W, NA_HEADS * _na_table_width()), layer, 1),
            cache((2 * HEAD_DIM, PAST_LEN)), cache((2 * HEAD_DIM, PAST_LEN)),
            cache((4 * HEAD_DIM, PAST_LEN)), cache((4 * HEAD_DIM, PAST_LEN)),
            cache((PAST_LEN, MLA_KV_RANK)), cache((MLA_ROPE, PAST_LEN)),
        ],
        out_specs=pl.BlockSpec((None, t, Y_COLS), lambda s: (s, 0, 0)),
        out_shape=jax.ShapeDtypeStruct((n, t, Y_COLS), BF16),
        scratch_shapes=scratch,
        compiler_params=pltpu.CompilerParams(
            dimension_semantics=("arbitrary",), vmem_limit_bytes=VMEM_LIMIT),
        name=name,
    )(sink, x, mods, g_attn, w_main, w_conv, g_q, w_uq, g_kv, w_uk, w_uv_t, *rope_g, *rope_m, _band_mask(), nab, *caches)


def _post_kernel(x_ref, y_ref, mod_ref, ga_ref, gm_ref, gf_ref, wg_ref, wbc_ref, wbg_ref, wbn_ref, wbm_ref,
                 wo_ref, w1_ref, w2_ref, o_ref, *, final):
    wb_refs = (wbc_ref, wbg_ref, wbn_ref, wbm_ref)
    d = D_MODEL
    mod = lambda i: mod_ref[:, i * d:(i + 1) * d]
    bounds = (Y_CONV, Y_GQA, Y_NA, Y_MLA, Y_COLS)
    for r0 in range(0, x_ref.shape[0], POST_CHAIN_ROWS):
        rows = slice(r0, r0 + POST_CHAIN_ROWS)
        x = x_ref[rows, :]
        h = _modulated_norm(x, ga_ref[...], mod_ref, 0, 1).astype(BF16)
        merged = None
        for i in range(N_BRANCH):
            lo, hi = bounds[i], bounds[i + 1]
            gate = jax.nn.sigmoid(_dot_nt(h, wg_ref[0, i * d:(i + 1) * d, :]))
            term = gate * _dot(y_ref[rows, lo:hi], wb_refs[i][...])
            merged = term if merged is None else merged + term
        x = x + mod(2) * _dot(merged.astype(BF16), wo_ref[...])
        h = _modulated_norm(x, gm_ref[...], mod_ref, 3, 4).astype(BF16)
        mlp = None
        for c0 in range(0, D_FF, d):
            f = jnp.square(jnp.maximum(_dot(h, w1_ref[:, c0:c0 + d]), 0.0)).astype(BF16)
            term = _dot(f, w2_ref[c0:c0 + d, :])
            mlp = term if mlp is None else mlp + term
        x = x + mod(5) * mlp
        o_ref[rows, :] = _rms(x, gf_ref[...]) if final else x


def _post_call(x, y, mods, g_attn, g_mlp, g_final, w_gates, w_br, w_o, w_ff1, w_ff2, layer, ctx, name):
    n, t, _ = x.shape
    tm = 2 * POST_CHAIN_ROWS
    return pl.pallas_call(
        functools.partial(_post_kernel, final=layer == DEPTH - 1),
        grid=(n, t // tm),
        in_specs=[
            pl.BlockSpec((None, tm, D_MODEL), lambda s, i: (s, i, 0)),
            pl.BlockSpec((None, tm, Y_COLS), lambda s, i: (s, i, 0)),
            _mod_spec(layer, ctx, 2),
            _layer_spec((1, D_MODEL), layer, 2),
            _layer_spec((1, D_MODEL), layer, 2),
            pl.BlockSpec((1, D_MODEL), lambda s, i: (0, 0)),
            pl.BlockSpec((pl.Element(1), pl.Element(N_BRANCH * D_MODEL), pl.Element(D_MODEL)),
                         lambda s, i: (layer, GATE_COL0, 0), pipeline_mode=pl.Buffered(1)),
            *[_layer_spec((hi - lo, D_MODEL), layer, 2)
              for lo, hi in zip((Y_CONV, Y_GQA, Y_NA, Y_MLA), (Y_GQA, Y_NA, Y_MLA, Y_COLS))],
            _layer_spec((D_MODEL, D_MODEL), layer, 2),
            _layer_spec((D_MODEL, D_FF), layer, 2),
            _layer_spec((D_FF, D_MODEL), layer, 2),
        ],
        out_specs=pl.BlockSpec((None, tm, D_MODEL), lambda s, i: (s, i, 0)),
        out_shape=jax.ShapeDtypeStruct((n, t, D_MODEL), F32),
        compiler_params=pltpu.CompilerParams(
            dimension_semantics=("arbitrary", "arbitrary"), vmem_limit_bytes=VMEM_LIMIT),
        name=name,
    )(x, y, mods, g_attn, g_mlp, g_final, w_gates, *w_br, w_o, w_ff1, w_ff2)


def kernel(x_prompt, x_sample, cache_gqa_k, cache_gqa_v, cache_na_k, cache_na_v, cache_mla_ckv, cache_mla_krope, c, c_ctx, w_mod, b_mod, g_attn, g_mlp, w_in, w_conv, gqa_sink, na_rpb, mla_g_q, mla_w_uq, mla_g_kv, mla_w_ukv, w_branch_conv, w_branch_gqa, w_branch_na, w_branch_mla, w_o, w_ff1, w_ff2, g_final):
    w_main = w_gates = jnp.transpose(w_in, (0, 2, 1)).astype(BF16)
    uq = mla_w_uq.reshape(DEPTH, MLA_Q_RANK, MLA_HEADS, MLA_NOPE + MLA_ROPE)
    w_uq = jnp.concatenate([uq[..., :MLA_NOPE].reshape(DEPTH, MLA_Q_RANK, -1),
                            uq[..., MLA_NOPE:].reshape(DEPTH, MLA_Q_RANK, -1)], axis=-1).astype(BF16)
    ukv = mla_w_ukv.reshape(DEPTH, MLA_KV_RANK, MLA_HEADS, MLA_NOPE + MLA_V)
    w_uk = ukv[..., :MLA_NOPE].reshape(DEPTH, MLA_KV_RANK, -1).astype(BF16)
    w_uv = ukv[..., MLA_NOPE:].reshape(DEPTH, MLA_KV_RANK, -1).astype(BF16)
    w_uk_t, w_uv_t = jnp.transpose(w_uk, (0, 2, 1)), jnp.transpose(w_uv, (0, 2, 1))
    w_br = tuple(w.astype(BF16) for w in (w_branch_conv, w_branch_gqa, w_branch_na, w_branch_mla))
    w_o_b, w_ff1_b, w_ff2_b = w_o.astype(BF16), w_ff1.astype(BF16), w_ff2.astype(BF16)
    g_a, g_m = g_attn[:, None, :], g_mlp[:, None, :]
    g_q, g_kv, g_f = mla_g_q[:, None, :], mla_g_kv[:, None, :], g_final[None, :]

    c16 = jnp.concatenate([c, c_ctx[None, :], jnp.zeros((MOD_ROWS - DEC_BATCH - 1, D_MODEL), F32)], axis=0)
    mods = _mod_call(c16, w_mod, b_mod).reshape(DEPTH, MOD_ROWS, 1, 6 * D_MODEL)
    nab = _nab_call(na_rpb)

    heads_t = lambda a: jnp.transpose(a, (0, 1, 3, 4, 2)).reshape(a.shape[0], DEPTH, -1, a.shape[2])
    caches = (heads_t(cache_gqa_k), heads_t(cache_gqa_v), heads_t(cache_na_k), heads_t(cache_na_v),
              cache_mla_ckv, jnp.transpose(cache_mla_krope, (0, 1, 3, 2)))

    h_ctx, h_lat = x_prompt, x_sample
    states = None
    for l in range(DEPTH):
        mixer_w = (gqa_sink, w_conv, g_q, w_uq, g_kv)
        post_w = (g_a, g_m, g_f, w_gates, w_br, w_o_b, w_ff1_b, w_ff2_b)

        y, states = _ctx_mixer_call(h_ctx, mods, g_a, w_main, *mixer_w, w_uk_t, w_uv, states, l, f"mixer_ctx_{l}")
        flat = lambda a: a.reshape(1, BATCH * SEQ, a.shape[-1])
        h_ctx = _post_call(flat(h_ctx), flat(y), mods, *post_w, l, True, f"post_ctx_{l}").reshape(BATCH, SEQ, D_MODEL)

        y = _lat_mixer_call(h_lat, mods, g_a, w_main, *mixer_w, w_uk, w_uv_t, nab, caches, l, f"mixer_lat_{l}")
        h_lat = _post_call(h_lat, y, mods, *post_w, l, False, f"post_lat_{l}")

    def heads_out(a, heads):
        return jnp.transpose(a.reshape(BATCH, DEPTH, heads, HEAD_DIM, SEQ), (0, 1, 4, 2, 3))

    kg, vg, kn, vn, ckv, kr = states
    return (h_ctx, h_lat, heads_out(kg, GQA_KV_HEADS), heads_out(vg, GQA_KV_HEADS),
            heads_out(kn, NA_HEADS), heads_out(vn, NA_HEADS), ckv, jnp.transpose(kr, (0, 1, 3, 2)))
```

```python
import functools
import math

import numpy as np
import jax
import jax.numpy as jnp
from jax import lax
from jax.experimental import pallas as pl
from jax.experimental.pallas import tpu as pltpu

D_MODEL = 1024
BATCH = 32
SEQ = 256
DEPTH = 2
DEC_BATCH = 8
DEC_SEQ = 1024
PAST_LEN = 512
GRID_W = 64
GRID_ROWS = DEC_SEQ // GRID_W
HEAD_DIM = 64
CONV_WIDTH = 256
CONV_K = 3
GQA_HEADS = 8
GQA_KV_HEADS = 2
WINDOW = 128
BAND_BLOCK = 128
NA_HEADS = 4
NA_WIN_H = 8
NA_WIN_W = 16
MLA_HEADS = 4
MLA_Q_RANK = 256
MLA_KV_RANK = 128
MLA_NOPE = 64
MLA_ROPE = 32
MLA_V = 64
D_FF = 4 * D_MODEL
N_BRANCH = 4
ROPE_BASE = 10000.0
EPS = 1e-6
NEG_INF = -1e30
LOG2E = math.log2(math.e)
ATTN_SCALE = HEAD_DIM ** -0.5
MLA_SCALE = (MLA_NOPE + MLA_ROPE) ** -0.5

LANES = 128
MOD_ROWS = 16
CTX_MOD_ROW = DEC_BATCH

C_CB, C_CC, C_CV = 0, 256, 512
C_GQ, C_GK, C_GV = 768, 1280, 1408
C_NQ, C_NK, C_NV = 1536, 1792, 2048
C_MQ, C_MKV, C_MKR = 2304, 2560, 2688
MAIN_COLS = 2816
GATE_COL0 = 2720
Y_CONV, Y_GQA, Y_NA, Y_MLA = 0, 256, 768, 1024
Y_COLS = 1280
NA_GROUPS = 4
NA_GROUP_ROWS = GRID_ROWS // NA_GROUPS

VMEM_LIMIT = 56 * 1024 * 1024
LAT_MIXER_VMEM_LIMIT = 60 * 1024 * 1024
POST_CHAIN_ROWS = 256
CTX_SEQS = 4
CTX_SEQS_FIRST = 2

F32 = jnp.float32
BF16 = jnp.bfloat16


def _dot(a, b):
    return jnp.dot(a, b, preferred_element_type=F32)


def _dot_nt(a, b):
    return lax.dot_general(a, b, (((1,), (1,)), ((), ())), preferred_element_type=F32)


def _rms(x, g):
    return x * lax.rsqrt(jnp.mean(x * x, axis=-1, keepdims=True) + EPS) * g


def _lane_lt(shape, n):
    return lax.broadcasted_iota(jnp.int32, shape, len(shape) - 1) < n


def _row_lt(shape, n):
    return lax.broadcasted_iota(jnp.int32, shape, 0) < n


def _row_group(shape, lo, hi):
    row = lax.broadcasted_iota(jnp.int32, shape, 0)
    return (row >= lo) & (row < hi)


def _short_conv(cb, cc, cv, w):
    u = cc * cv
    t = u.shape[0]
    row = lax.broadcasted_iota(jnp.int32, u.shape, 0)
    prev = jnp.where(row == 0, 0.0, pltpu.roll(u, 1, 0))
    nxt = jnp.where(row == t - 1, 0.0, pltpu.roll(u, t - 1, 0))
    return cb * (prev * w[0:1, :] + u * w[1:2, :] + nxt * w[2:3, :])


def _rope(x, cos, sin_hi, sin_lo, half):
    n = x.shape[-1]
    return x * cos + pltpu.roll(x, n - half, 1) * sin_lo + pltpu.roll(x, half, 1) * sin_hi


def _mod_kernel(c_ref, w_ref, b_ref, o_ref):
    c = c_ref[...]
    s = c * jax.nn.sigmoid(c)
    o_ref[...] = _dot(s.astype(BF16), w_ref[...].astype(BF16)) + b_ref[...]


def _mod_call(c16, w_mod, b_mod):
    tn = 1536
    return pl.pallas_call(
        _mod_kernel,
        grid=(DEPTH, 6 * D_MODEL // tn),
        in_specs=[
            pl.BlockSpec((MOD_ROWS, D_MODEL), lambda l, j: (0, 0)),
            pl.BlockSpec((None, D_MODEL, tn), lambda l, j: (l, 0, j)),
            pl.BlockSpec((None, 1, tn), lambda l, j: (l, 0, j)),
        ],
        out_specs=pl.BlockSpec((None, MOD_ROWS, tn), lambda l, j: (l, 0, j)),
        out_shape=jax.ShapeDtypeStruct((DEPTH, MOD_ROWS, 6 * D_MODEL), F32),
        compiler_params=pltpu.CompilerParams(
            dimension_semantics=("arbitrary", "arbitrary"), vmem_limit_bytes=VMEM_LIMIT),
        name="adaln_mod",
    )(c16, w_mod, b_mod.reshape(DEPTH, 1, 6 * D_MODEL))


def _na_window_start(r):
    return min(max(r - NA_WIN_H // 2, 0), GRID_ROWS - NA_WIN_H)


def _na_group_slab(k):
    starts = [_na_window_start(r) for r in range(k * NA_GROUP_ROWS, (k + 1) * NA_GROUP_ROWS)]
    lo = min(starts) // 2 * 2
    n = -(-(max(starts) + NA_WIN_H - lo) // 4) * 4
    assert lo + n <= GRID_ROWS
    return lo, n


def _na_group_kinds():
    kinds, of_group = [], []
    for k in range(NA_GROUPS):
        lo, n = _na_group_slab(k)
        layout = {}
        for dr in range(NA_GROUP_ROWS):
            r = k * NA_GROUP_ROWS + dr
            r0 = _na_window_start(r)
            for a in range(n):
                inside = r0 <= lo + a < r0 + NA_WIN_H
                layout[(dr, a)] = lo + a - r + NA_WIN_H - 1 if inside else None
        if (n, layout) not in kinds:
            kinds.append((n, layout))
        of_group.append(kinds.index((n, layout)))
    return kinds, of_group


def _na_table_width():
    return sum(n for n, _ in _na_group_kinds()[0]) * GRID_W


def _nab_kernel(rpb_ref, o_ref):
    l = pl.program_id(0)
    c = lax.broadcasted_iota(jnp.int32, (GRID_W, GRID_W), 0)
    w = lax.broadcasted_iota(jnp.int32, (GRID_W, GRID_W), 1)
    dc = w - c + (NA_WIN_W - 1)
    c0 = jnp.clip(c - NA_WIN_W // 2, 0, GRID_W - NA_WIN_W)
    outside = (w < c0) | (w >= c0 + NA_WIN_W)
    n_dr, n_dc = 2 * NA_WIN_H - 1, 2 * NA_WIN_W - 1
    kinds, _ = _na_group_kinds()
    width = _na_table_width()

    def put(h, d, tile):
        col0 = h * width
        for n, layout in kinds:
            for (dr, a), want in layout.items():
                if want == d:
                    o_ref[dr * GRID_W:(dr + 1) * GRID_W, col0 + a * GRID_W:col0 + (a + 1) * GRID_W] = tile
            col0 += n * GRID_W

    for h in range(NA_HEADS):
        put(h, None, jnp.full((GRID_W, GRID_W), NEG_INF, F32))
        for d in range(n_dr):
            base = ((l * NA_HEADS + h) * n_dr + d) * n_dc
            tile = jnp.full((GRID_W, GRID_W), NEG_INF, F32)
            for j in range(n_dc):
                tile = jnp.where(dc == j, rpb_ref[base + j] * LOG2E, tile)
            put(h, d, jnp.where(outside, NEG_INF, tile))


def _nab_call(na_rpb):
    shape = (NA_GROUP_ROWS * GRID_W, NA_HEADS * _na_table_width())
    return pl.pallas_call(
        _nab_kernel,
        grid=(DEPTH,),
        in_specs=[pl.BlockSpec(memory_space=pltpu.SMEM)],
        out_specs=pl.BlockSpec((None,) + shape, lambda l: (l, 0, 0)),
        out_shape=jax.ShapeDtypeStruct((DEPTH,) + shape, F32),
        compiler_params=pltpu.CompilerParams(dimension_semantics=("arbitrary",), vmem_limit_bytes=VMEM_LIMIT),
        name="na_bias_tables",
    )(na_rpb.reshape(-1))


def _mod_spec(layer, ctx, n_grid):
    if n_grid == 1:
        index = (lambda s: (layer, CTX_MOD_ROW, 0, 0)) if ctx else (lambda s: (layer, s, 0, 0))
    else:
        index = (lambda s, i: (layer, CTX_MOD_ROW, 0, 0)) if ctx else (lambda s, i: (layer, s, 0, 0))
    return pl.BlockSpec((None, None, 1, 6 * D_MODEL), index)


def _layer_spec(shape, layer, n_grid):
    zeros = (0,) * len(shape)
    index = (lambda s: (layer,) + zeros) if n_grid == 1 else (lambda s, i: (layer,) + zeros)
    return pl.BlockSpec((None,) + tuple(shape), index, pipeline_mode=pl.Buffered(1))


_COLUMN_GROUPS = ((C_CB, C_GQ), (C_GQ, C_NQ), (C_NQ, C_MQ), (C_MQ, MAIN_COLS))


def _modulated_norm(x, g, mod_ref, shift, scale):
    d = D_MODEL
    return _rms(x, g) * (1.0 + mod_ref[:, scale * d:(scale + 1) * d]) + mod_ref[:, shift * d:(shift + 1) * d]


def _projected_columns(h_ref, w_ref):
    cache = {}

    def col(c, n):
        lo, hi = next(g for g in _COLUMN_GROUPS if g[0] <= c and c + n <= g[1])
        if lo not in cache:
            cache.clear()
            cache[lo] = _dot_nt(h_ref[...], w_ref[lo:hi, :])
        return cache[lo][:, c - lo:c - lo + n]

    return col


def _key_planes(k_t, even_first, odd_first):
    lo = _row_lt(k_t.shape, HEAD_DIM)
    swapped = None
    if not even_first or odd_first:
        swapped = pltpu.roll(k_t, HEAD_DIM, 0)
    top = jnp.where(lo, k_t if even_first else swapped, 0.0)
    bot = jnp.where(lo, 0.0, swapped if odd_first else k_t)
    return top.astype(BF16), bot.astype(BF16)


def _value_planes(pair, even_first, odd_first):
    lo = _lane_lt(pair.shape, HEAD_DIM)
    swapped = None
    if not even_first or odd_first:
        swapped = pltpu.roll(pair, HEAD_DIM, 1)
    top = jnp.where(lo, pair if even_first else swapped, 0.0)
    bot = jnp.where(lo, 0.0, swapped if odd_first else pair)
    one_e = jnp.where(lo, 1.0, 0.0)
    return (jnp.concatenate([top, one_e], axis=1).astype(BF16),
            jnp.concatenate([bot, 1.0 - one_e], axis=1).astype(BF16))


def _probabilities(parts, sink=None):
    m = parts[0].max(axis=-1, keepdims=True)
    for p in parts[1:]:
        m = jnp.maximum(m, p.max(axis=-1, keepdims=True))
    if sink is not None:
        m = jnp.maximum(m, sink)
    probs = [jnp.exp2((p - m).astype(BF16)) for p in parts]
    return probs, (None if sink is None else jnp.exp2(sink - m))


def _attend(terms, sink_e=None, sink_o=None):
    o = None
    for p, v in terms:
        t = _dot(p, v)
        o = t if o is None else o + t
    den = o[:, LANES:]
    if sink_e is not None:
        den = den + jnp.where(_lane_lt(den.shape, HEAD_DIM), sink_e, sink_o)
    return o[:, :LANES] / den


ONES_ROWS = 16


def _value_plane_t(v_t):
    return jnp.concatenate([v_t, jnp.ones((ONES_ROWS, v_t.shape[1]), v_t.dtype)], axis=0).astype(BF16)


def _probabilities_t(parts, sink=None):
    m = parts[0].max(axis=0, keepdims=True)
    for p in parts[1:]:
        m = jnp.maximum(m, p.max(axis=0, keepdims=True))
    if sink is not None:
        m = jnp.maximum(m, sink)
    probs = [jnp.exp2((p - m).astype(BF16)) for p in parts]
    return probs, (None if sink is None else jnp.exp2(sink - m))


def _attend_t(terms, sink_num=None):
    o = None
    for v_t, p in terms:
        t = _dot(v_t, p)
        o = t if o is None else o + t
    den = o[HEAD_DIM:HEAD_DIM + 1, :]
    if sink_num is not None:
        den = den + sink_num
    return o[0:HEAD_DIM, :] / den


def _mla_key_plane(kn_t_pair, kr4_t, j, odd):
    h = 2 * j + odd
    lo = _row_lt(kn_t_pair.shape, HEAD_DIM)
    nope = jnp.where(lo, 0.0, kn_t_pair) if odd else jnp.where(lo, kn_t_pair, 0.0)
    rope = jnp.where(_row_group(kr4_t.shape, h * MLA_ROPE, (h + 1) * MLA_ROPE), kr4_t, 0.0)
    return jnp.concatenate([nope, rope], axis=0).astype(BF16)


def _ctx_mixer_kernel(sink_ref, x_ref, mod_ref, ga_ref, win_ref, wc_ref, gq_ref, wuq_ref, gkv_ref, wukt_ref, wuv_ref,
                      *rest, layer, first):
    y_ref, kg_ref, vg_ref, kn_ref, vn_ref, ckv_ref, kr_ref, h_s = rest[-8:]
    t, nb = SEQ, x_ref.shape[0]
    for b in range(nb):
        h_s[b * t:(b + 1) * t, :] = _modulated_norm(x_ref[b], ga_ref[...], mod_ref, 0, 1).astype(BF16)
    col_all = _projected_columns(h_s, win_ref)

    def put_state(ref, b, val):
        if first:
            for d in range(DEPTH):
                ref[b, d] = val if d == layer else jnp.zeros_like(val)
        else:
            ref[b] = val

    def conv(b, col):
        y_ref[b, :, Y_CONV:Y_CONV + CONV_WIDTH] = _short_conv(
            col(C_CB, CONV_WIDTH), col(C_CC, CONV_WIDTH), col(C_CV, CONV_WIDTH), wc_ref[...]).astype(BF16)

    def gqa(b, col):
        kg_t, vpair = col(C_GK, LANES).T, col(C_GV, LANES)
        put_state(kg_ref, b, kg_t)
        put_state(vg_ref, b, vpair.T)
        row2 = lax.broadcasted_iota(jnp.int32, (2 * t, 1), 0) < t
        for g in range(GQA_KV_HEADS):
            ke, ko = _key_planes(kg_t, g == 0, g == 0)
            ve, vo = _value_planes(vpair, g == 0, g == 0)
            q = jnp.concatenate([col(C_GQ + (2 * g) * LANES, LANES), col(C_GQ + (2 * g + 1) * LANES, LANES)], axis=0)
            q = (q * (ATTN_SCALE * LOG2E)).astype(BF16)
            s = _dot(q, jnp.concatenate([ke, ko], axis=1))
            sink = [jnp.where(row2, sink_ref[layer, 4 * g + odd], sink_ref[layer, 4 * g + 2 + odd]) * LOG2E
                    for odd in range(2)]
            (pe,), xe = _probabilities([s[:, 0:t]], sink[0])
            (po,), xo = _probabilities([s[:, t:2 * t]], sink[1])
            o = _attend([(pe, ve), (po, vo)], xe, xo).astype(BF16)
            c0 = Y_GQA + (2 * g) * LANES
            y_ref[b, :, c0:c0 + LANES] = o[0:t]
            y_ref[b, :, c0 + LANES:c0 + 2 * LANES] = o[t:2 * t]

    def na(b, col):
        kn_t = col(C_NK, 2 * LANES).T
        put_state(kn_ref, b, kn_t)
        put_state(vn_ref, b, col(C_NV, 2 * LANES).T)
        for j in range(NA_HEADS // 2):
            ke, ko = _key_planes(kn_t[j * LANES:(j + 1) * LANES, :], True, False)
            ve, vo = _value_planes(col(C_NV + j * LANES, LANES), True, False)
            q = (col(C_NQ + j * LANES, LANES) * (ATTN_SCALE * LOG2E)).astype(BF16)
            s = _dot(q, jnp.concatenate([ke, ko], axis=1))
            (pe,), _ = _probabilities([s[:, 0:t]])
            (po,), _ = _probabilities([s[:, t:2 * t]])
            y_ref[b, :, Y_NA + j * LANES:Y_NA + (j + 1) * LANES] = _attend([(pe, ve), (po, vo)]).astype(BF16)

    def mla(b, col):
        ckv = _rms(col(C_MKV, MLA_KV_RANK), gkv_ref[...])
        put_state(ckv_ref, b, ckv)
        kr_t = col(C_MKR, LANES).T[0:MLA_ROPE, :]
        put_state(kr_ref, b, kr_t)
        ckv_b = ckv.astype(BF16)
        q = _dot(_rms(col(C_MQ, MLA_Q_RANK), gq_ref[...]).astype(BF16), wuq_ref[...]) * (MLA_SCALE * LOG2E)
        kn_t_all = _dot_nt(wukt_ref[...], ckv_b)
        v_all = _dot(ckv_b, wuv_ref[...])
        kr4_t = jnp.concatenate([kr_t] * MLA_HEADS, axis=0)
        q_rope = q[:, 2 * LANES:3 * LANES]
        for j in range(MLA_HEADS // 2):
            qj = jnp.concatenate([q[:, j * LANES:(j + 1) * LANES], q_rope], axis=1).astype(BF16)
            kn_t_pair = kn_t_all[j * LANES:(j + 1) * LANES, :]
            keys = jnp.concatenate(
                [_mla_key_plane(kn_t_pair, kr4_t, j, 0), _mla_key_plane(kn_t_pair, kr4_t, j, 1)], axis=1)
            s = _dot(qj, keys)
            ve, vo = _value_planes(v_all[:, j * LANES:(j + 1) * LANES], True, False)
            (pe,), _ = _probabilities([s[:, 0:t]])
            (po,), _ = _probabilities([s[:, t:2 * t]])
            y_ref[b, :, Y_MLA + j * LANES:Y_MLA + (j + 1) * LANES] = _attend([(pe, ve), (po, vo)]).astype(BF16)

    for mixer in (conv, gqa, na, mla):
        for b in range(nb):
            mixer(b, lambda c, n, b=b: col_all(c, n)[b * t:(b + 1) * t])


def _ctx_mixer_call(x, mods, g_attn, w_main, sink, w_conv, g_q, w_uq, g_kv, w_uk_t, w_uv, prev_states, layer, name):
    first = prev_states is None
    n, nb = BATCH, (CTX_SEQS_FIRST if first else CTX_SEQS)
    state_shapes = ((2 * HEAD_DIM, SEQ), (2 * HEAD_DIM, SEQ), (4 * HEAD_DIM, SEQ), (4 * HEAD_DIM, SEQ),
                    (SEQ, MLA_KV_RANK), (MLA_ROPE, SEQ))
    if first:
        state_spec = lambda shape: pl.BlockSpec((nb, DEPTH) + shape, lambda s: (s, 0, 0, 0))
    else:
        state_spec = lambda shape: pl.BlockSpec((nb, None) + shape, lambda s: (s, layer, 0, 0))
    in_specs = [
        pl.BlockSpec(memory_space=pltpu.SMEM),
        pl.BlockSpec((nb, SEQ, D_MODEL), lambda s: (s, 0, 0)),
        _mod_spec(layer, True, 1),
        _layer_spec((1, D_MODEL), layer, 1),
        _layer_spec((MAIN_COLS, D_MODEL), layer, 1),
        _layer_spec((CONV_K, CONV_WIDTH), layer, 1),
        _layer_spec((1, MLA_Q_RANK), layer, 1),
        _layer_spec((MLA_Q_RANK, 3 * LANES), layer, 1),
        _layer_spec((1, MLA_KV_RANK), layer, 1),
        _layer_spec((2 * LANES, MLA_KV_RANK), layer, 1),
        _layer_spec((MLA_KV_RANK, 2 * LANES), layer, 1),
    ]
    args = [sink, x, mods, g_attn, w_main, w_conv, g_q, w_uq, g_kv, w_uk_t, w_uv]
    aliases = {}
    if not first:
        for i, st in enumerate(prev_states):
            aliases[len(args)] = 1 + i
            in_specs.append(pl.BlockSpec(memory_space=pl.ANY))
            args.append(st)
    outs = pl.pallas_call(
        functools.partial(_ctx_mixer_kernel, layer=layer, first=first),
        grid=(n // nb,),
        in_specs=in_specs,
        out_specs=[pl.BlockSpec((nb, SEQ, Y_COLS), lambda s: (s, 0, 0))] + [state_spec(s) for s in state_shapes],
        out_shape=[jax.ShapeDtypeStruct((n, SEQ, Y_COLS), BF16)]
        + [jax.ShapeDtypeStruct((n, DEPTH) + s, F32) for s in state_shapes],
        input_output_aliases=aliases,
        scratch_shapes=[pltpu.VMEM((nb * SEQ, D_MODEL), BF16)],
        compiler_params=pltpu.CompilerParams(
            dimension_semantics=("arbitrary",), vmem_limit_bytes=VMEM_LIMIT),
        name=name,
    )(*args)
    return outs[0], outs[1:]


def _lat_mixer_kernel(sink_ref, x_ref, mod_ref, ga_ref, win_ref, wc_ref, gq_ref, wuq_ref, gkv_ref, wuk_ref, wuvt_ref,
                      rc_ref, rsh_ref, rsl_ref, mc_ref, msh_ref, msl_ref, band_ref, nab_ref,
                      cgk_ref, cgv_ref, cnk_ref, cnv_ref, cckv_ref, ckr_ref,
                      y_ref,
                      h_s, gq_s, gk_s, gv_s, gkc_s, gvc_s, nq_s, nk_s, nv_s, nkc_s, nvc_s, mq_s, mk_s, mv_s, *, layer):
    t = DEC_SEQ
    h_s[...] = _modulated_norm(x_ref[...], ga_ref[...], mod_ref, 0, 1).astype(BF16)
    col = _projected_columns(h_s, win_ref)
    rope64 = lambda x: _rope(x, rc_ref[...], rsh_ref[...], rsl_ref[...], HEAD_DIM // 4)
    rope32 = lambda x: _rope(x, mc_ref[...], msh_ref[...], msl_ref[...], MLA_ROPE // 4)

    y_ref[:, Y_CONV:Y_CONV + CONV_WIDTH] = _short_conv(
        col(C_CB, CONV_WIDTH), col(C_CC, CONV_WIDTH), col(C_CV, CONV_WIDTH), wc_ref[...]).astype(BF16)

    group = GQA_HEADS // GQA_KV_HEADS
    for j in range(GQA_HEADS // 2):
        q_pair = rope64(col(C_GQ + j * LANES, LANES)) * (ATTN_SCALE * LOG2E)
        gq_s[j * LANES:(j + 1) * LANES, :] = q_pair.T.astype(BF16)
    gk_s[...] = rope64(col(C_GK, LANES)).astype(BF16)
    gkc_s[...] = cgk_ref[...].T.astype(BF16)
    v_t = col(C_GV, LANES).T
    for g in range(GQA_KV_HEADS):
        gv_s[g] = _value_plane_t(v_t[g * HEAD_DIM:(g + 1) * HEAD_DIM, :])
        gvc_s[g] = _value_plane_t(cgv_ref[g * HEAD_DIM:(g + 1) * HEAD_DIM, :])

    def gqa_block(b, c_lo, c_hi):
        q0 = pl.multiple_of(b * BAND_BLOCK, BAND_BLOCK)
        k0 = pl.multiple_of(q0 + (c_lo - WINDOW), BAND_BLOCK)
        n = c_hi - c_lo
        band = band_ref[c_lo:c_hi, :]
        lane_head = lax.shift_right_logical(
            lax.broadcasted_iota(jnp.int32, (1, group * BAND_BLOCK), 1), BAND_BLOCK.bit_length() - 1)
        zeros = jnp.zeros((HEAD_DIM, group * BAND_BLOCK), BF16)
        for g in range(GQA_KV_HEADS):
            q_t = jnp.concatenate(
                [gq_s[(group * g + h) * HEAD_DIM:(group * g + h + 1) * HEAD_DIM, pl.ds(q0, BAND_BLOCK)]
                 for h in range(group)], axis=1)
            q_t = jnp.concatenate([q_t, zeros] if g == 0 else [zeros, q_t], axis=0)
            s_loc = _dot(gk_s[pl.ds(k0, n), :], q_t) + band
            s_ctx = _dot(gkc_s[...], q_t)
            sink = sink_ref[layer, group * g + group - 1]
            for h in range(group - 2, -1, -1):
                sink = jnp.where(lane_head == h, sink_ref[layer, group * g + h], sink)
            (p_loc, p_ctx), x = _probabilities_t([s_loc, s_ctx], sink * LOG2E)
            o = _attend_t([(gv_s[g, :, pl.ds(k0, n)], p_loc), (gvc_s[g], p_ctx)], x)
            for pr in range(group // 2):
                pair = jnp.concatenate([o[:, (2 * pr) * BAND_BLOCK:(2 * pr + 1) * BAND_BLOCK],
                                        o[:, (2 * pr + 1) * BAND_BLOCK:(2 * pr + 2) * BAND_BLOCK]], axis=0)
                c0 = Y_GQA + (group // 2 * g + pr) * LANES
                y_ref[pl.ds(q0, BAND_BLOCK), c0:c0 + LANES] = pair.T.astype(BF16)

    for j in range(NA_HEADS // 2):
        nq_s[:, j * LANES:(j + 1) * LANES] = (col(C_NQ + j * LANES, LANES) * (ATTN_SCALE * LOG2E)).astype(BF16)
        nk_s[2 * j], nk_s[2 * j + 1] = _key_planes(col(C_NK + j * LANES, LANES).T, True, False)
        nv_s[2 * j], nv_s[2 * j + 1] = _value_planes(col(C_NV + j * LANES, LANES), True, False)
        nkc_s[j] = jnp.concatenate(_key_planes(cnk_ref[j * LANES:(j + 1) * LANES, :], True, False), axis=1)
        nvc_s[2 * j], nvc_s[2 * j + 1] = _value_planes(cnv_ref[j * LANES:(j + 1) * LANES, :].T, True, False)

    na_kinds, na_kind_of = _na_group_kinds()
    na_width = _na_table_width()

    def na_group(k):
        lo, n = _na_group_slab(k)
        q0, nq = k * NA_GROUP_ROWS * GRID_W, NA_GROUP_ROWS * GRID_W
        k0, nk = lo * GRID_W, n * GRID_W
        b0 = sum(kn for kn, _ in na_kinds[:na_kind_of[k]]) * GRID_W
        for j in range(NA_HEADS // 2):
            q = nq_s[q0:q0 + nq, j * LANES:(j + 1) * LANES]
            s_ctx = _dot(q, nkc_s[j])
            terms = []
            for odd in range(2):
                h = 2 * j + odd
                s_loc = _dot(q, nk_s[h, :, k0:k0 + nk]) + nab_ref[:, h * na_width + b0:h * na_width + b0 + nk]
                (p_loc, p_ctx), _ = _probabilities([s_loc, s_ctx[:, odd * PAST_LEN:(odd + 1) * PAST_LEN]])
                terms += [(p_loc, nv_s[h, k0:k0 + nk, :]), (p_ctx, nvc_s[h])]
            y_ref[q0:q0 + nq, Y_NA + j * LANES:Y_NA + (j + 1) * LANES] = _attend(terms).astype(BF16)

    ckv_b = _rms(col(C_MKV, MLA_KV_RANK), gkv_ref[...]).astype(BF16)
    cckv_b = cckv_ref[...].astype(BF16)
    q = _dot(_rms(col(C_MQ, MLA_Q_RANK), gq_ref[...]).astype(BF16), wuq_ref[...]) * (MLA_SCALE * LOG2E)
    for i in range(3):
        tile = q[:, i * LANES:(i + 1) * LANES]
        mq_s[i * LANES:(i + 1) * LANES, :] = (rope32(tile) if i == 2 else tile).T.astype(BF16)
    kr = jnp.where(_lane_lt((t, LANES), MLA_ROPE), rope32(col(C_MKR, LANES)), 0.0)
    kr_c = jnp.concatenate([ckr_ref[...], jnp.zeros((LANES - MLA_ROPE, PAST_LEN), F32)], axis=0).T
    for rows, ckv_x, kr_x in ((slice(0, t), ckv_b, kr), (slice(t, t + PAST_LEN), cckv_b, kr_c)):
        kn_all = _dot(ckv_x, wuk_ref[...])
        v_t_all = _dot_nt(wuvt_ref[...], ckv_x)
        for j in range(MLA_HEADS // 2):
            mk_s[j, rows, :] = jnp.concatenate([kn_all[:, j * LANES:(j + 1) * LANES], kr_x], axis=1).astype(BF16)
        for h in range(MLA_HEADS):
            mv_s[h, :, rows] = _value_plane_t(v_t_all[h * MLA_V:(h + 1) * MLA_V, :])

    tq = 256

    def mla_block(i, carry):
        q0 = pl.multiple_of(i * tq, tq)
        zeros = lambda r: jnp.zeros((r, tq), BF16)
        q_nope = lambda h: mq_s[h * MLA_NOPE:(h + 1) * MLA_NOPE, pl.ds(q0, tq)]
        q_rope = lambda h: mq_s[MLA_HEADS * MLA_NOPE + h * MLA_ROPE:MLA_HEADS * MLA_NOPE + (h + 1) * MLA_ROPE,
                                pl.ds(q0, tq)]
        pad = 2 * LANES - 2 * MLA_NOPE - MLA_ROPE
        for j in range(MLA_HEADS // 2):
            he, ho = 2 * j, 2 * j + 1
            q_e = jnp.concatenate([q_nope(he), zeros(MLA_NOPE), q_rope(he), zeros(pad)], axis=0)
            q_o = jnp.concatenate([zeros(MLA_NOPE), q_nope(ho), q_rope(ho), zeros(pad)], axis=0)
            s = _dot(mk_s[j], jnp.concatenate([q_e, q_o], axis=1))
            outs = []
            for odd in range(2):
                (p,), _ = _probabilities_t([s[:, odd * tq:(odd + 1) * tq]])
                outs.append(_attend_t([(mv_s[2 * j + odd], p)]))
            y_ref[pl.ds(q0, tq), Y_MLA + j * LANES:Y_MLA + (j + 1) * LANES] = (
                jnp.concatenate(outs, axis=0).T.astype(BF16))
        return carry

    span = BAND_BLOCK + 2 * WINDOW
    nb = t // BAND_BLOCK
    for i in range(nb):
        gqa_block(jnp.int32(i), WINDOW if i == 0 else 0, span - WINDOW if i == nb - 1 else span)
        if i % (nb // NA_GROUPS) == 0:
            na_group(i // (nb // NA_GROUPS))
        if i % (nb * tq // t) == 0:
            mla_block(jnp.int32(i // (nb * tq // t)), 0)


def _rope_tables(group, half):
    tok = np.arange(DEC_SEQ)
    pos = np.stack([tok // GRID_W, tok % GRID_W], axis=1).astype(np.float64)
    inv = ROPE_BASE ** (-np.arange(half, dtype=np.float64) / half)
    lane = np.arange(LANES) % group
    axis = lane // (2 * half)
    within = lane % (2 * half)
    ang = pos[:, axis] * inv[within % half][None, :]
    cos, sin = np.cos(ang), np.sin(ang)
    upper = (within >= half)[None, :]
    sin_hi = np.where(upper, sin, 0.0)
    sin_lo = np.where(upper, 0.0, -sin)
    return tuple(jnp.asarray(a, dtype=F32) for a in (cos, sin_hi, sin_lo))


def _band_mask():
    c = np.arange(BAND_BLOCK + 2 * WINDOW)[:, None]
    i = np.arange(BAND_BLOCK)[None, :]
    ok = (c >= i) & (c <= i + 2 * WINDOW)
    m = np.where(ok, 0.0, NEG_INF)
    return jnp.asarray(np.concatenate([m] * (GQA_HEADS // GQA_KV_HEADS), axis=1), dtype=F32)


def _lat_mixer_call(x, mods, g_attn, w_main, sink, w_conv, g_q, w_uq, g_kv, w_uk, w_uv_t, nab, caches, layer, name):
    n, t = DEC_BATCH, DEC_SEQ
    one = pl.Buffered(1)
    const = lambda shape: pl.BlockSpec(shape, lambda s: (0,) * len(shape), pipeline_mode=one)
    cache = lambda shape: pl.BlockSpec((None, None) + shape, lambda s: (s, layer, 0, 0))
    rope_g = _rope_tables(HEAD_DIM, HEAD_DIM // 4)
    rope_m = _rope_tables(MLA_ROPE, MLA_ROPE // 4)
    span = BAND_BLOCK + 2 * WINDOW
    keys = t + PAST_LEN
    scratch = [
        pltpu.VMEM((t, D_MODEL), BF16),
        pltpu.VMEM((GQA_HEADS * HEAD_DIM, t), BF16),
        pltpu.VMEM((t, LANES), BF16),
        pltpu.VMEM((GQA_KV_HEADS, HEAD_DIM + ONES_ROWS, t), BF16),
        pltpu.VMEM((PAST_LEN, LANES), BF16),
        pltpu.VMEM((GQA_KV_HEADS, HEAD_DIM + ONES_ROWS, PAST_LEN), BF16),
        pltpu.VMEM((t, NA_HEADS // 2 * LANES), BF16),
        pltpu.VMEM((NA_HEADS, LANES, t), BF16),
        pltpu.VMEM((NA_HEADS, t, 2 * LANES), BF16),
        pltpu.VMEM((NA_HEADS // 2, LANES, 2 * PAST_LEN), BF16),
        pltpu.VMEM((NA_HEADS, PAST_LEN, 2 * LANES), BF16),
        pltpu.VMEM((3 * LANES, t), BF16),
        pltpu.VMEM((MLA_HEADS // 2, keys, 2 * LANES), BF16),
        pltpu.VMEM((MLA_HEADS, MLA_V + ONES_ROWS, keys), BF16),
    ]
    return pl.pallas_call(
        functools.partial(_lat_mixer_kernel, layer=layer),
        grid=(n,),
        in_specs=[
            pl.BlockSpec(memory_space=pltpu.SMEM),
            pl.BlockSpec((None, t, D_MODEL), lambda s: (s, 0, 0)),
            _mod_spec(layer, False, 1),
            _layer_spec((1, D_MODEL), layer, 1),
            _layer_spec((MAIN_COLS, D_MODEL), layer, 1),
            _layer_spec((CONV_K, CONV_WIDTH), layer, 1),
            _layer_spec((1, MLA_Q_RANK), layer, 1),
            _layer_spec((MLA_Q_RANK, 3 * LANES), layer, 1),
            _layer_spec((1, MLA_KV_RANK), layer, 1),
            _layer_spec((MLA_KV_RANK, 2 * LANES), layer, 1),
            _layer_spec((2 * LANES, MLA_KV_RANK), layer, 1),
        ] + [const((t, LANES))] * 6 + [
            const((span, GQA_HEADS // GQA_KV_HEADS * BAND_BLOCK)),
            _layer_spec((NA_GROUP_ROWS * GRID_W, NA_HEADS * _na_table_width()), layer, 1),
            cache((2 * HEAD_DIM, PAST_LEN)), cache((2 * HEAD_DIM, PAST_LEN)),
            cache((4 * HEAD_DIM, PAST_LEN)), cache((4 * HEAD_DIM, PAST_LEN)),
            cache((PAST_LEN, MLA_KV_RANK)), cache((MLA_ROPE, PAST_LEN)),
        ],
        out_specs=pl.BlockSpec((None, t, Y_COLS), lambda s: (s, 0, 0)),
        out_shape=jax.ShapeDtypeStruct((n, t, Y_COLS), BF16),
        scratch_shapes=scratch,
        compiler_params=pltpu.CompilerParams(
            dimension_semantics=("arbitrary",), vmem_limit_bytes=LAT_MIXER_VMEM_LIMIT),
        name=name,
    )(sink, x, mods, g_attn, w_main, w_conv, g_q, w_uq, g_kv, w_uk, w_uv_t, *rope_g, *rope_m, _band_mask(), nab, *caches)


def _post_kernel(x_ref, y_ref, mod_ref, ga_ref, gm_ref, gf_ref, wg_ref, wbc_ref, wbg_ref, wbn_ref, wbm_ref,
                 wo_ref, w1_ref, w2_ref, o_ref, *, final):
    wb_refs = (wbc_ref, wbg_ref, wbn_ref, wbm_ref)
    d = D_MODEL
    mod = lambda i: mod_ref[:, i * d:(i + 1) * d]
    bounds = (Y_CONV, Y_GQA, Y_NA, Y_MLA, Y_COLS)
    for r0 in range(0, x_ref.shape[0], POST_CHAIN_ROWS):
        rows = slice(r0, r0 + POST_CHAIN_ROWS)
        x = x_ref[rows, :]
        h = _modulated_norm(x, ga_ref[...], mod_ref, 0, 1).astype(BF16)
        merged = None
        for i in range(N_BRANCH):
            lo, hi = bounds[i], bounds[i + 1]
            gate = jax.nn.sigmoid(_dot_nt(h, wg_ref[0, i * d:(i + 1) * d, :]))
            term = gate * _dot(y_ref[rows, lo:hi], wb_refs[i][...])
            merged = term if merged is None else merged + term
        x = x + mod(2) * _dot(merged.astype(BF16), wo_ref[...])
        h = _modulated_norm(x, gm_ref[...], mod_ref, 3, 4).astype(BF16)
        mlp = None
        for c0 in range(0, D_FF, d):
            f = jnp.square(jnp.maximum(_dot(h, w1_ref[:, c0:c0 + d]), 0.0)).astype(BF16)
            term = _dot(f, w2_ref[c0:c0 + d, :])
            mlp = term if mlp is None else mlp + term
        x = x + mod(5) * mlp
        o_ref[rows, :] = _rms(x, gf_ref[...]) if final else x


def _post_call(x, y, mods, g_attn, g_mlp, g_final, w_gates, w_br, w_o, w_ff1, w_ff2, layer, ctx, name):
    n, t, _ = x.shape
    tm = 2 * POST_CHAIN_ROWS
    return pl.pallas_call(
        functools.partial(_post_kernel, final=layer == DEPTH - 1),
        grid=(n, t // tm),
        in_specs=[
            pl.BlockSpec((None, tm, D_MODEL), lambda s, i: (s, i, 0)),
            pl.BlockSpec((None, tm, Y_COLS), lambda s, i: (s, i, 0)),
            _mod_spec(layer, ctx, 2),
            _layer_spec((1, D_MODEL), layer, 2),
            _layer_spec((1, D_MODEL), layer, 2),
            pl.BlockSpec((1, D_MODEL), lambda s, i: (0, 0)),
            pl.BlockSpec((pl.Element(1), pl.Element(N_BRANCH * D_MODEL), pl.Element(D_MODEL)),
                         lambda s, i: (layer, GATE_COL0, 0), pipeline_mode=pl.Buffered(1)),
            *[_layer_spec((hi - lo, D_MODEL), layer, 2)
              for lo, hi in zip((Y_CONV, Y_GQA, Y_NA, Y_MLA), (Y_GQA, Y_NA, Y_MLA, Y_COLS))],
            _layer_spec((D_MODEL, D_MODEL), layer, 2),
            _layer_spec((D_MODEL, D_FF), layer, 2),
            _layer_spec((D_FF, D_MODEL), layer, 2),
        ],
        out_specs=pl.BlockSpec((None, tm, D_MODEL), lambda s, i: (s, i, 0)),
        out_shape=jax.ShapeDtypeStruct((n, t, D_MODEL), F32),
        compiler_params=pltpu.CompilerParams(
            dimension_semantics=("arbitrary", "arbitrary"), vmem_limit_bytes=VMEM_LIMIT),
        name=name,
    )(x, y, mods, g_attn, g_mlp, g_final, w_gates, *w_br, w_o, w_ff1, w_ff2)


def kernel(x_prompt, x_sample, cache_gqa_k, cache_gqa_v, cache_na_k, cache_na_v, cache_mla_ckv, cache_mla_krope, c, c_ctx, w_mod, b_mod, g_attn, g_mlp, w_in, w_conv, gqa_sink, na_rpb, mla_g_q, mla_w_uq, mla_g_kv, mla_w_ukv, w_branch_conv, w_branch_gqa, w_branch_na, w_branch_mla, w_o, w_ff1, w_ff2, g_final):
    w_main = w_gates = jnp.transpose(w_in, (0, 2, 1)).astype(BF16)
    uq = mla_w_uq.reshape(DEPTH, MLA_Q_RANK, MLA_HEADS, MLA_NOPE + MLA_ROPE)
    w_uq = jnp.concatenate([uq[..., :MLA_NOPE].reshape(DEPTH, MLA_Q_RANK, -1),
                            uq[..., MLA_NOPE:].reshape(DEPTH, MLA_Q_RANK, -1)], axis=-1).astype(BF16)
    ukv = mla_w_ukv.reshape(DEPTH, MLA_KV_RANK, MLA_HEADS, MLA_NOPE + MLA_V)
    w_uk = ukv[..., :MLA_NOPE].reshape(DEPTH, MLA_KV_RANK, -1).astype(BF16)
    w_uv = ukv[..., MLA_NOPE:].reshape(DEPTH, MLA_KV_RANK, -1).astype(BF16)
    w_uk_t, w_uv_t = jnp.transpose(w_uk, (0, 2, 1)), jnp.transpose(w_uv, (0, 2, 1))
    w_br = tuple(w.astype(BF16) for w in (w_branch_conv, w_branch_gqa, w_branch_na, w_branch_mla))
    w_o_b, w_ff1_b, w_ff2_b = w_o.astype(BF16), w_ff1.astype(BF16), w_ff2.astype(BF16)
    g_a, g_m = g_attn[:, None, :], g_mlp[:, None, :]
    g_q, g_kv, g_f = mla_g_q[:, None, :], mla_g_kv[:, None, :], g_final[None, :]

    c16 = jnp.concatenate([c, c_ctx[None, :], jnp.zeros((MOD_ROWS - DEC_BATCH - 1, D_MODEL), F32)], axis=0)
    mods = _mod_call(c16, w_mod, b_mod).reshape(DEPTH, MOD_ROWS, 1, 6 * D_MODEL)
    nab = _nab_call(na_rpb)

    heads_t = lambda a: jnp.transpose(a, (0, 1, 3, 4, 2)).reshape(a.shape[0], DEPTH, -1, a.shape[2])
    caches = (heads_t(cache_gqa_k), heads_t(cache_gqa_v), heads_t(cache_na_k), heads_t(cache_na_v),
              cache_mla_ckv, jnp.transpose(cache_mla_krope, (0, 1, 3, 2)))

    h_ctx, h_lat = x_prompt, x_sample
    states = None
    for l in range(DEPTH):
        mixer_w = (gqa_sink, w_conv, g_q, w_uq, g_kv)
        post_w = (g_a, g_m, g_f, w_gates, w_br, w_o_b, w_ff1_b, w_ff2_b)

        y, states = _ctx_mixer_call(h_ctx, mods, g_a, w_main, *mixer_w, w_uk_t, w_uv, states, l, f"mixer_ctx_{l}")
        flat = lambda a: a.reshape(1, BATCH * SEQ, a.shape[-1])
        h_ctx = _post_call(flat(h_ctx), flat(y), mods, *post_w, l, True, f"post_ctx_{l}").reshape(BATCH, SEQ, D_MODEL)

        y = _lat_mixer_call(h_lat, mods, g_a, w_main, *mixer_w, w_uk, w_uv_t, nab, caches, l, f"mixer_lat_{l}")
        h_lat = _post_call(h_lat, y, mods, *post_w, l, False, f"post_lat_{l}")

    def heads_out(a, heads):
        return jnp.transpose(a.reshape(BATCH, DEPTH, heads, HEAD_DIM, SEQ), (0, 1, 4, 2, 3))

    kg, vg, kn, vn, ckv, kr = states
    return (h_ctx, h_lat, heads_out(kg, GQA_KV_HEADS), heads_out(vg, GQA_KV_HEADS),
            heads_out(kn, NA_HEADS), heads_out(vn, NA_HEADS), ckv, jnp.transpose(kr, (0, 1, 3, 2)))
```
